```python
import math
import numpy as np
import jax, jax.numpy as jnp
from jax import lax

D_MODEL = 2048
BATCH = 2
SEQ = 4096
DEPTH = 2

HEAD_DIM = 128
ROPE_THETA = 500000.0
ROPE_FRACTION_DIV = 4
Q_BLK = 128
NEG = -1e30
FORCE = 1e6

A_HEADS = 8
A_LATENT = 512
IDX_HEADS = 16
IDX_DIM = 64
DSA_TOPK = 256

B_HEADS = 8
B_KV_GROUPS = 2
B_HPG = B_HEADS // B_KV_GROUPS
CMP_LEN = 32
CMP_STRIDE = 16
SLC_LEN = 64
SLC_TOPN = 16
WIN_LEN = 512

C_HEADS = 4
C_DIM = 128

BRANCH_W = A_HEADS * HEAD_DIM
N_BRANCH = 3

D_FF = int(math.ceil(8 * D_MODEL / 3 / 256)) * 256

ALPHA = (2 * DEPTH) ** 0.25
BETA = (8 * DEPTH) ** -0.25

A_Q_W = A_HEADS * HEAD_DIM
IDX_Q_W = IDX_HEADS * IDX_DIM
B_Q_W = B_HEADS * HEAD_DIM
B_KV_W = 3 * 2 * B_KV_GROUPS * HEAD_DIM
B_GATE_W = 3 * B_HEADS
C_QK_W = C_HEADS * 2 * C_DIM
C_V_W = C_HEADS * 2 * C_DIM
MERGE_GATE_W = N_BRANCH * D_MODEL
IN_SPLITS = (A_Q_W, A_LATENT, IDX_Q_W, IDX_DIM, IDX_HEADS,
             B_Q_W, B_KV_W, B_GATE_W,
             C_QK_W, C_QK_W, C_V_W,
             MERGE_GATE_W)
N_IN = sum(IN_SPLITS)
IN_OFFSETS = tuple(int(o) for o in np.cumsum(IN_SPLITS)[:-1])

kernel_name = 'hybrid_dsa_nsa_diffattn_deepnorm_adaln'

f32 = jnp.float32


def layer_norm(x, g=None, b=None, eps=1e-5):
    xf = x.astype(f32)
    mu = xf.mean(-1, keepdims=True)
    var = jnp.mean(jnp.square(xf - mu), -1, keepdims=True)
    y = (xf - mu) * lax.rsqrt(var + eps)
    if g is not None:
        y = y * g.astype(f32) + b.astype(f32)
    return y.astype(x.dtype)


def rms_norm(x, g, eps=1e-6):
    xf = x.astype(f32)
    y = xf * lax.rsqrt(jnp.mean(jnp.square(xf), -1, keepdims=True) + eps)
    return (y * g.astype(f32)).astype(x.dtype)


def partial_rope(x, pos):
    d = x.shape[-1]
    r = d // ROPE_FRACTION_DIV
    half = r // 2
    inv = ROPE_THETA ** (-(jnp.arange(half, dtype=f32) * 2.0) / r)
    ang = pos.astype(f32)[:, None] * inv[None, :]
    cos = jnp.cos(ang)[:, None, :]
    sin = jnp.sin(ang)[:, None, :]
    x1 = x[..., :half].astype(f32)
    x2 = x[..., half:r].astype(f32)
    rot = jnp.concatenate([x1 * cos - x2 * sin, x2 * cos + x1 * sin], -1).astype(x.dtype)
    return jnp.concatenate([rot, x[..., r:]], -1)


def masked_softmax(s, mask):
    s = jnp.where(mask, s.astype(f32), NEG)
    s = s - s.max(-1, keepdims=True)
    p = jnp.exp(s) * mask
    return p / jnp.maximum(p.sum(-1, keepdims=True), 1e-30)


def sweep_query_blocks(block_fn, L):
    out = lax.map(block_fn, jnp.arange(L // Q_BLK))
    n, B, q, w = out.shape
    return out.transpose(1, 0, 2, 3).reshape(B, n * q, w)


def dsa_attention(q, k, v, iq, ik, iw):
    B, L = q.shape[:2]
    k_sel = min(DSA_TOPK, L // 4)
    scale = HEAD_DIM ** -0.5
    idx_scale = IDX_DIM ** -0.5
    key_pos = jnp.arange(L)
    gather = jax.vmap(lambda arr, ii: arr[ii])

    def block(j):
        t = j * Q_BLK + jnp.arange(Q_BLK)
        qb = lax.dynamic_slice_in_dim(q, j * Q_BLK, Q_BLK, 1)
        iqb = lax.dynamic_slice_in_dim(iq, j * Q_BLK, Q_BLK, 1)
        iwb = lax.dynamic_slice_in_dim(iw, j * Q_BLK, Q_BLK, 1)
        rel = jax.nn.relu(jnp.einsum('bqhd,bsd->bqhs', iqb, ik) * idx_scale)
        score = jnp.einsum('bqh,bqhs->bqs', iwb, rel).astype(f32)
        causal = key_pos[None, :] <= t[:, None]
        score = jnp.where(causal, score, NEG)
        _, idx = lax.top_k(score, k_sel)
        valid = idx <= t[None, :, None]
        kg = gather(k, idx)
        vg = gather(v, idx)
        s = jnp.einsum('bqhd,bqkhd->bhqk', qb, kg) * scale
        p = masked_softmax(s, valid[:, None])
        o = jnp.einsum('bhqk,bqkhd->bqhd', p, vg)
        return o.reshape(B, Q_BLK, A_HEADS * HEAD_DIM)

    return sweep_query_blocks(block, L)


def nsa_attention(q, k, v, gate_logits, cmp_w1, cmp_w2, cmp_pe):
    B, L = q.shape[:2]
    G, D = B_KV_GROUPS, HEAD_DIM
    scale = D ** -0.5
    n_cmp = (L - CMP_LEN) // CMP_STRIDE + 1
    starts = np.arange(n_cmp) * CMP_STRIDE
    tok = starts[:, None] + np.arange(CMP_LEN)[None, :]
    cmp_end = jnp.asarray(starts + CMP_LEN - 1)

    def compress(xs, w1, w2, pe):
        blk = xs[:, tok] + pe[None, None, :, None, :]
        blk = blk.transpose(0, 1, 3, 2, 4).reshape(B, n_cmp, G, CMP_LEN * D)
        return jax.nn.silu(blk @ w1) @ w2

    kc = compress(k[:, :, 0], cmp_w1[0], cmp_w2[0], cmp_pe[0])
    vc = compress(v[:, :, 0], cmp_w1[1], cmp_w2[1], cmp_pe[1])
    n_slc = L // SLC_LEN
    n_sel = min(SLC_TOPN, n_slc)
    slc_start = np.arange(n_slc) * SLC_LEN
    cover = jnp.asarray(((starts[:, None] < slc_start[None, :] + SLC_LEN)
                         & (starts[:, None] + CMP_LEN > slc_start[None, :])).astype(np.float32))
    ks_blk = k[:, :, 1].reshape(B, n_slc, SLC_LEN, G, D).transpose(0, 3, 1, 2, 4)
    vs_blk = v[:, :, 1].reshape(B, n_slc, SLC_LEN, G, D).transpose(0, 3, 1, 2, 4)
    gather2 = jax.vmap(jax.vmap(lambda blocks, ii: blocks[ii]))
    blk_id = jnp.arange(n_slc)
    pad = ((0, 0), (WIN_LEN, 0), (0, 0), (0, 0))
    kw_pad = jnp.pad(k[:, :, 2], pad)
    vw_pad = jnp.pad(v[:, :, 2], pad)
    gates = jax.nn.sigmoid(gate_logits.astype(f32))

    def block(j):
        t = j * Q_BLK + jnp.arange(Q_BLK)
        qb = lax.dynamic_slice_in_dim(q, j * Q_BLK, Q_BLK, 1).reshape(B, Q_BLK, G, B_HPG, D)
        gb = lax.dynamic_slice_in_dim(gates, j * Q_BLK, Q_BLK, 1).reshape(B, Q_BLK, G, B_HPG, 3)
        s = jnp.einsum('bqghd,bngd->bghqn', qb, kc) * scale
        p_cmp = masked_softmax(s, cmp_end[None, :] <= t[:, None])
        o_cmp = jnp.einsum('bghqn,bngd->bqghd', p_cmp, vc)
        imp = jnp.einsum('bghqn,nm->bgqm', p_cmp, cover)
        cur = t // SLC_LEN
        forced = ((blk_id[None, :] == 0) | (blk_id[None, :] == cur[:, None])
                  | (blk_id[None, :] == cur[:, None] - 1))
        admissible = blk_id[None, :] <= cur[:, None]
        imp = jnp.where(forced, FORCE, jnp.where(admissible, imp, NEG))
        _, sel = lax.top_k(imp, n_sel)
        kg = gather2(ks_blk, sel).reshape(B, G, Q_BLK, n_sel * SLC_LEN, D)
        vg = gather2(vs_blk, sel).reshape(B, G, Q_BLK, n_sel * SLC_LEN, D)
        tok_pos = sel[..., None] * SLC_LEN + jnp.arange(SLC_LEN)
        m_slc = (tok_pos <= t[None, None, :, None, None]).reshape(B, G, 1, Q_BLK, n_sel * SLC_LEN)
        s = jnp.einsum('bqghd,bgqxd->bghqx', qb, kg) * scale
        p = masked_softmax(s, m_slc)
        o_slc = jnp.einsum('bghqx,bgqxd->bqghd', p, vg)
        kw = lax.dynamic_slice_in_dim(kw_pad, j * Q_BLK, WIN_LEN + Q_BLK, 1)
        vw = lax.dynamic_slice_in_dim(vw_pad, j * Q_BLK, WIN_LEN + Q_BLK, 1)
        s_pos = j * Q_BLK - WIN_LEN + jnp.arange(WIN_LEN + Q_BLK)
        diff = t[:, None] - s_pos[None, :]
        m_win = (s_pos[None, :] >= 0) & (diff >= 0) & (diff < WIN_LEN)
        s = jnp.einsum('bqghd,bkgd->bghqk', qb, kw) * scale
        p = masked_softmax(s, m_win)
        o_win = jnp.einsum('bghqk,bkgd->bqghd', p, vw)
        o = gb[..., 0:1] * o_cmp + gb[..., 1:2] * o_slc + gb[..., 2:3] * o_win
        return o.reshape(B, Q_BLK, B_HEADS * D)

    return sweep_query_blocks(block, L)


def diff_attention(q, k, v, lam, subln_g, lam_init):
    B, L = q.shape[:2]
    scale = C_DIM ** -0.5
    lamf = lam.astype(f32)
    lam_val = (jnp.exp(jnp.sum(lamf[0] * lamf[1])) - jnp.exp(jnp.sum(lamf[2] * lamf[3])) + lam_init)
    key_pos = jnp.arange(L)

    def block(j):
        t = j * Q_BLK + jnp.arange(Q_BLK)
        qb = lax.dynamic_slice_in_dim(q, j * Q_BLK, Q_BLK, 1)
        s = jnp.einsum('bqhmd,bkhmd->bmhqk', qb, k) * scale
        p = masked_softmax(s, key_pos[None, :] <= t[:, None])
        attn = p[:, 0] - lam_val * p[:, 1]
        o = jnp.einsum('bhqk,bkhe->bqhe', attn, v)
        o = rms_norm(o, subln_g) * (1.0 - lam_init)
        return o.reshape(B, Q_BLK, C_HEADS * 2 * C_DIM)

    return sweep_query_blocks(block, L)


def token_mixing(u, w_in, a_lat_g, a_up, cmp_w1, cmp_w2, cmp_pe, lam, c_subln_g, w_br, w_o, lam_init):
    B, L, _ = u.shape
    pos = jnp.arange(L)
    h = u @ w_in
    aq, alat, iq, ik, iw, bq, bkv, bg, cq, ck, cv, gl = jnp.split(h, list(IN_OFFSETS), axis=-1)
    aq = partial_rope(aq.reshape(B, L, A_HEADS, HEAD_DIM), pos)
    akv = (rms_norm(alat, a_lat_g) @ a_up).reshape(B, L, 2, A_HEADS, HEAD_DIM)
    ak = partial_rope(akv[:, :, 0], pos)
    av = akv[:, :, 1]
    iq = partial_rope(iq.reshape(B, L, IDX_HEADS, IDX_DIM), pos)
    ik = partial_rope(layer_norm(ik)[:, :, None, :], pos)[:, :, 0]
    iw = iw * (IDX_HEADS ** -0.5)
    ya = dsa_attention(aq, ak, av, iq, ik, iw)
    bq = partial_rope(bq.reshape(B, L, B_HEADS, HEAD_DIM), pos)
    bkv = bkv.reshape(B, L, 3, 2, B_KV_GROUPS, HEAD_DIM)
    bk = partial_rope(bkv[:, :, :, 0].reshape(B, L, 3 * B_KV_GROUPS, HEAD_DIM), pos)
    bk = bk.reshape(B, L, 3, B_KV_GROUPS, HEAD_DIM)
    bv = bkv[:, :, :, 1]
    yb = nsa_attention(bq, bk, bv, bg.reshape(B, L, B_HEADS, 3), cmp_w1, cmp_w2, cmp_pe)
    cq = partial_rope(cq.reshape(B, L, C_HEADS * 2, C_DIM), pos).reshape(B, L, C_HEADS, 2, C_DIM)
    ck = partial_rope(ck.reshape(B, L, C_HEADS * 2, C_DIM), pos).reshape(B, L, C_HEADS, 2, C_DIM)
    cv = cv.reshape(B, L, C_HEADS, 2 * C_DIM)
    yc = diff_attention(cq, ck, cv, lam, c_subln_g, lam_init)
    y = jnp.stack([ya, yb, yc])
    br = jnp.einsum('rbse,red->rbsd', y, w_br)
    g = jax.nn.sigmoid(gl.astype(f32)).reshape(B, L, N_BRANCH, D_MODEL)
    merged = jnp.einsum('bsrd,rbsd->bsd', g, br)
    return merged @ w_o


def setup_inputs(seed: int = 0) -> dict:
    key = jax.random.key(seed)
    ks = jax.random.split(key, 18)

    def nrm(k, shape, s):
        return jax.random.normal(k, shape, f32) * s

    return {
        'x': nrm(ks[0], (BATCH, SEQ, D_MODEL), 1.0),
        'c': nrm(ks[1], (BATCH, D_MODEL), 1.0),
        'w_ada': nrm(ks[2], (DEPTH, D_MODEL, 6 * D_MODEL), 0.5 * D_MODEL ** -0.5),
        'b_ada': nrm(ks[3], (DEPTH, 6 * D_MODEL), 0.02),
        'w_in': nrm(ks[4], (DEPTH, D_MODEL, N_IN), D_MODEL ** -0.5),
        'a_lat_g': 1.0 + nrm(ks[5], (DEPTH, A_LATENT), 0.02),
        'a_up': nrm(ks[6], (DEPTH, A_LATENT, 2 * A_HEADS * HEAD_DIM), A_LATENT ** -0.5),
        'cmp_w1': nrm(ks[7], (DEPTH, 2, CMP_LEN * HEAD_DIM, HEAD_DIM), (CMP_LEN * HEAD_DIM) ** -0.5),
        'cmp_w2': nrm(ks[8], (DEPTH, 2, HEAD_DIM, HEAD_DIM), HEAD_DIM ** -0.5),
        'cmp_pe': nrm(ks[9], (DEPTH, 2, CMP_LEN, HEAD_DIM), 0.1),
        'lam': nrm(ks[10], (DEPTH, 4, C_DIM), 0.1),
        'c_subln_g': 1.0 + nrm(ks[11], (DEPTH, 2 * C_DIM), 0.02),
        'w_br': nrm(ks[12], (DEPTH, N_BRANCH, BRANCH_W, D_MODEL), BETA * BRANCH_W ** -0.5),
        'w_o': nrm(ks[13], (DEPTH, D_MODEL, D_MODEL), BETA * D_MODEL ** -0.5),
        'w_ffn_in': nrm(ks[14], (DEPTH, D_MODEL, 2 * D_FF), D_MODEL ** -0.5),
        'w_ffn_out': nrm(ks[15], (DEPTH, D_FF, D_MODEL), BETA * D_FF ** -0.5),
        'ln_g': 1.0 + nrm(ks[16], (DEPTH, 2, D_MODEL), 0.02),
        'ln_b': nrm(ks[17], (DEPTH, 2, D_MODEL), 0.02),
    }


def reference(x, c, w_ada, b_ada, w_in, a_lat_g, a_up, cmp_w1, cmp_w2, cmp_pe, lam, c_subln_g,
              w_br, w_o, w_ffn_in, w_ffn_out, ln_g, ln_b):
    cs = jax.nn.silu(c)
    for l in range(DEPTH):
        lam_init = 0.8 - 0.6 * math.exp(-0.3 * l)
        mod = cs @ w_ada[l] + b_ada[l]
        sh_a, sc_a, g_a, sh_f, sc_f, g_f = jnp.split(mod, 6, axis=-1)
        u = layer_norm(x) * (1.0 + sc_a[:, None]) + sh_a[:, None]
        y = token_mixing(u, w_in[l], a_lat_g[l], a_up[l], cmp_w1[l], cmp_w2[l], cmp_pe[l],
                         lam[l], c_subln_g[l], w_br[l], w_o[l], lam_init)
        x = layer_norm(ALPHA * x + g_a[:, None] * y, ln_g[l, 0], ln_b[l, 0])
        u = layer_norm(x) * (1.0 + sc_f[:, None]) + sh_f[:, None]
        gate, up = jnp.split(u @ w_ffn_in[l], 2, axis=-1)
        f = (jax.nn.silu(gate) * up) @ w_ffn_out[l]
        x = layer_norm(ALPHA * x + g_f[:, None] * f, ln_g[l, 1], ln_b[l, 1])
    return x
```

```python
import functools
import math

import numpy as np
import jax
import jax.numpy as jnp
from jax import lax
from jax.experimental import pallas as pl
from jax.experimental.pallas import tpu as pltpu

f32 = jnp.float32
bf16 = jnp.bfloat16
i32 = jnp.int32

D_MODEL = 2048
HEAD_DIM = 128
ROPE_THETA = 500000.0
NEG = -1e30
FORCE = 1e6
A_HEADS = 8
A_LATENT = 512
IDX_HEADS = 16
IDX_DIM = 64
DSA_TOPK = 256
B_HEADS = 8
B_KV_GROUPS = 2
B_HPG = B_HEADS // B_KV_GROUPS
CMP_LEN = 32
CMP_STRIDE = 16
SLC_LEN = 64
SLC_TOPN = 16
WIN_LEN = 512
C_HEADS = 4
C_DIM = 128
BRANCH_W = A_HEADS * HEAD_DIM
N_BRANCH = 3
D_FF = int(math.ceil(8 * D_MODEL / 3 / 256)) * 256

_O_AQ = 0
_O_ALAT = _O_AQ + A_HEADS * HEAD_DIM
_O_IQ = _O_ALAT + A_LATENT
_O_IK = _O_IQ + IDX_HEADS * IDX_DIM
_O_IW = _O_IK + IDX_DIM
_O_BQ = _O_IW + IDX_HEADS
_O_BKV = _O_BQ + B_HEADS * HEAD_DIM
_O_BG = _O_BKV + 3 * 2 * B_KV_GROUPS * HEAD_DIM
_O_CQ = _O_BG + 3 * B_HEADS
_O_CK = _O_CQ + C_HEADS * 2 * C_DIM
_O_CV = _O_CK + C_HEADS * 2 * C_DIM
_O_GL = _O_CV + C_HEADS * 2 * C_DIM
_N_IN = _O_GL + N_BRANCH * D_MODEL

_P_GL = 0
_P_AQ = 6144
_P_IQ = 7168
_P_BQ = 8192
_P_CQ = 9216
_P_CK = 10240
_P_CV = 11264
_P_BK = 12288
_P_BV = 13056
_P_ALAT = 13824
_P_IKW = 14336
_P_BG = 14464
_P_TOT = 14592

_VMEM_LIMIT = 48 * 1024 * 1024


def _cparams(sem, vmem=_VMEM_LIMIT):
    return pltpu.CompilerParams(dimension_semantics=sem, vmem_limit_bytes=vmem)


def _sigmoid(x):
    return 1.0 / (1.0 + jnp.exp(-x))


def _dot_nt(a, b):
    return lax.dot_general(a, b, (((1,), (1,)), ((), ())), preferred_element_type=f32)


def _ada_kernel(c_ref, w_ref, b_ref, o_ref):
    c = c_ref[...]
    cs = c * _sigmoid(c)
    o_ref[0] = jnp.dot(cs, w_ref[0], preferred_element_type=f32,
                       precision=lax.Precision.HIGHEST) + b_ref[0]


def _ada(c_pad, w_ada, b_ada):
    depth, d, n = w_ada.shape
    tn = 512
    return pl.pallas_call(
        _ada_kernel,
        grid=(depth, n // tn),
        in_specs=[pl.BlockSpec((8, d), lambda l, j: (0, 0)),
                  pl.BlockSpec((1, d, tn), lambda l, j: (l, 0, j)),
                  pl.BlockSpec((1, 1, tn), lambda l, j: (l, 0, j))],
        out_specs=pl.BlockSpec((1, 8, tn), lambda l, j: (l, 0, j)),
        out_shape=jax.ShapeDtypeStruct((depth, 8, n), f32),
        compiler_params=_cparams(("parallel", "parallel")),
    )(c_pad, w_ada, b_ada.reshape(depth, 1, n))


def _ln_rows(x, eps):
    mu = jnp.mean(x, axis=-1, keepdims=True)
    d = x - mu
    var = jnp.mean(d * d, axis=-1, keepdims=True)
    return d * lax.rsqrt(var + eps)


def _lnmod_kernel(x_ref, sc_ref, sh_ref, o_ref):
    y = _ln_rows(x_ref[...], 1e-5)
    o_ref[...] = (y * (1.0 + sc_ref[0]) + sh_ref[0]).astype(o_ref.dtype)


def _lnmod(x2, sc, sh, L):
    m, d = x2.shape
    tm = 512
    per_b = L // tm
    return pl.pallas_call(
        _lnmod_kernel,
        grid=(m // tm,),
        in_specs=[pl.BlockSpec((tm, d), lambda i: (i, 0)),
                  pl.BlockSpec((1, 1, d), lambda i: (i // per_b, 0, 0)),
                  pl.BlockSpec((1, 1, d), lambda i: (i // per_b, 0, 0))],
        out_specs=pl.BlockSpec((tm, d), lambda i: (i, 0)),
        out_shape=jax.ShapeDtypeStruct((m, d), bf16),
        compiler_params=_cparams(("parallel",)),
    )(x2, sc, sh)


def _mm_kernel(a_ref, w_ref, o_ref):
    o_ref[...] = jnp.dot(a_ref[...], w_ref[...], preferred_element_type=f32).astype(o_ref.dtype)


def _mm(a, w, tm, tn, out_dtype):
    m, k = a.shape
    n = w.shape[1]
    return pl.pallas_call(
        _mm_kernel,
        grid=(n // tn, m // tm),
        in_specs=[pl.BlockSpec((tm, k), lambda j, i: (i, 0)),
                  pl.BlockSpec((k, tn), lambda j, i: (0, j))],
        out_specs=pl.BlockSpec((tm, tn), lambda j, i: (i, j)),
        out_shape=jax.ShapeDtypeStruct((m, n), out_dtype),
        compiler_params=_cparams(("parallel", "parallel")),
    )(a, w)


def _rope_heads(x, c, s1, s2, half):
    return x * c + pltpu.roll(x, half, 1) * s1 + pltpu.roll(x, 128 - half, 1) * s2


def _mm_rope_kernel(a_ref, w_ref, c_ref, s1_ref, s2_ref, o_ref):
    acc = jnp.dot(a_ref[...], w_ref[...], preferred_element_type=f32)
    c, s1, s2 = c_ref[...], s1_ref[...], s2_ref[...]
    for h in range(acc.shape[1] // 128):
        sl = slice(h * 128, (h + 1) * 128)
        o_ref[:, sl] = _rope_heads(acc[:, sl], c, s1, s2, 16).astype(o_ref.dtype)


def _mm_rope(a, w, tabs, L, tm, tn):
    m, k = a.shape
    n = w.shape[1]
    per_b = L // tm
    tspec = pl.BlockSpec((tm, 128), lambda j, i: (i % per_b, 0))
    return pl.pallas_call(
        _mm_rope_kernel,
        grid=(n // tn, m // tm),
        in_specs=[pl.BlockSpec((tm, k), lambda j, i: (i, 0)),
                  pl.BlockSpec((k, tn), lambda j, i: (0, j)),
                  tspec, tspec, tspec],
        out_specs=pl.BlockSpec((tm, tn), lambda j, i: (i, j)),
        out_shape=jax.ShapeDtypeStruct((m, n), bf16),
        compiler_params=_cparams(("parallel", "parallel")),
    )(a, w, *tabs)


def _prep_kernel(aq_ref, iq_ref, bq_ref, cq_ref, ck_ref, bk_ref, alat_ref, ikw_ref, bg_ref, alg_ref,
                 c_ref, s1_ref, s2_ref, c6_ref, s16_ref, s26_ref,
                 aq_o, iq_o, bq_o, cq_o, ck_o, bk_o, alat_o, ikw_o, g_o):
    c, s1, s2 = c_ref[...], s1_ref[...], s2_ref[...]
    c6, s16, s26 = c6_ref[...], s16_ref[...], s26_ref[...]
    scale = HEAD_DIM ** -0.5

    def rope_all(src, dst, mult, tabs, half):
        for h in range(src.shape[1] // 128):
            sl = slice(h * 128, (h + 1) * 128)
            y = _rope_heads(src[:, sl].astype(f32), *tabs, half)
            if mult != 1.0:
                y = y * mult
            dst[:, sl] = y.astype(dst.dtype)

    rope_all(aq_ref, aq_o, scale, (c, s1, s2), 16)
    rope_all(bq_ref, bq_o, scale, (c, s1, s2), 16)
    rope_all(cq_ref, cq_o, C_DIM ** -0.5, (c, s1, s2), 16)
    rope_all(ck_ref, ck_o, 1.0, (c, s1, s2), 16)
    rope_all(bk_ref, bk_o, 1.0, (c, s1, s2), 16)
    rope_all(iq_ref, iq_o, 1.0, (c6, s16, s26), 8)

    a = alat_ref[...].astype(f32)
    ms = jnp.mean(a * a, axis=-1, keepdims=True)
    alat_o[...] = (a * lax.rsqrt(ms + 1e-6) * alg_ref[...]).astype(alat_o.dtype)

    x = ikw_ref[...].astype(f32)
    lane = lax.broadcasted_iota(i32, x.shape, 1)
    isk = lane < IDX_DIM
    mu = jnp.sum(jnp.where(isk, x, 0.0), axis=-1, keepdims=True) * (1.0 / IDX_DIM)
    d = jnp.where(isk, x - mu, 0.0)
    var = jnp.sum(d * d, axis=-1, keepdims=True) * (1.0 / IDX_DIM)
    y = d * lax.rsqrt(var + 1e-5)
    yr = _rope_heads(y, c6, s16, s26, 8)
    ikw_o[...] = jnp.where(isk, yr, x * (IDX_DIM ** -0.5 * IDX_HEADS ** -0.5))

    g_o[...] = _sigmoid(bg_ref[...].astype(f32))


def _prep(h, a_lat_g, tabs128, tabs64, L):
    m = h.shape[0]
    tm = 256
    per_b = L // tm

    def hs(width, off):
        return pl.BlockSpec((tm, width), lambda i, _o=off // width: (i, _o))

    tspec = pl.BlockSpec((tm, 128), lambda i: (i % per_b, 0))

    def os(width):
        return pl.BlockSpec((tm, width), lambda i: (i, 0))

    outs = [(1024, bf16)] * 5 + [(768, bf16), (512, bf16), (128, f32), (128, f32)]
    return pl.pallas_call(
        _prep_kernel,
        grid=(m // tm,),
        in_specs=[hs(1024, _P_AQ), hs(1024, _P_IQ), hs(1024, _P_BQ), hs(1024, _P_CQ), hs(1024, _P_CK),
                  hs(768, _P_BK), hs(512, _P_ALAT), hs(128, _P_IKW), hs(128, _P_BG),
                  pl.BlockSpec((1, A_LATENT), lambda i: (0, 0))] + [tspec] * 6,
        out_specs=[os(w) for w, _ in outs],
        out_shape=[jax.ShapeDtypeStruct((m, w), dt) for w, dt in outs],
        compiler_params=_cparams(("parallel",)),
    )(h, h, h, h, h, h, h, h, h, a_lat_g.reshape(1, A_LATENT), *tabs128, *tabs64)


def _dsa_mask_kernel(ik_ref, iq_ref, iw_ref, o_ref, key_ref, j_ref, *, L, tq, ksel):
    j = pl.program_id(1)
    ck = 512
    nck = L // ck
    t_row = j * tq + lax.broadcasted_iota(i32, (1, tq), 1)

    def score_chunk(c, carry):
        off = pl.multiple_of(c * ck, ck)
        ikc = ik_ref[0, pl.ds(off, ck), :]
        acc = jnp.zeros((ck, tq), f32)
        for h in range(IDX_HEADS):
            qh = iq_ref[0, :, h * IDX_DIM:(h + 1) * IDX_DIM]
            s = _dot_nt(ikc, qh)
            acc = acc + jnp.maximum(s, 0.0) * iw_ref[0, h:h + 1, :]
        acc = jnp.where(acc == 0.0, 0.0, acc)
        sidx = off + lax.broadcasted_iota(i32, (ck, tq), 0)
        sc = jnp.where(sidx <= t_row, acc, NEG)
        bits = lax.bitcast_convert_type(sc, i32)
        key_ref[pl.ds(off, ck), :] = bits ^ ((bits >> 31) & 0x7FFFFFFF)
        return carry

    lax.fori_loop(0, nck, score_chunk, 0)

    def count(pred_fn):
        def body(c, cnt):
            off = pl.multiple_of(c * ck, ck)
            k = key_ref[pl.ds(off, ck), :]
            sidx = off + lax.broadcasted_iota(i32, (ck, tq), 0)
            ones = pred_fn(k, sidx)
            return cnt + jnp.sum(ones.reshape(ck // 8, 8, tq), axis=0)
        part = lax.fori_loop(0, nck, body, jnp.zeros((8, tq), f32))
        return jnp.sum(part, axis=0, keepdims=True)

    kf = float(ksel)

    def bit_body(i, thr):
        cand = thr + lax.shift_left(jnp.int32(1), 31 - i)
        cnt = count(lambda k, s: jnp.where(k >= cand, 1.0, 0.0))
        return jnp.where(cnt >= kf, cand, thr)

    thr = lax.fori_loop(0, 32, bit_body, jnp.full((1, tq), -2 ** 31, i32))
    cnt_gt = count(lambda k, s: jnp.where(k > thr, 1.0, 0.0))
    cnt_ge = count(lambda k, s: jnp.where(k >= thr, 1.0, 0.0))
    need = kf - cnt_gt

    j_ref[...] = jnp.full((1, tq), L, i32)

    @pl.when(jnp.max(cnt_ge) > kf)
    def _():
        nbits = L.bit_length() - 1

        def jbit(i, cur):
            cand = cur | lax.shift_left(jnp.int32(1), nbits - 1 - i)
            f = count(lambda k, s: jnp.where(k == thr, jnp.where(s < cand, 1.0, 0.0), 0.0))
            return jnp.where(f < need, cand, cur)

        j_ref[...] = lax.fori_loop(0, nbits, jbit, jnp.zeros((1, tq), i32))

    jlast = j_ref[...]

    def write(c, carry):
        off = pl.multiple_of(c * ck, ck)
        k = key_ref[pl.ds(off, ck), :]
        sidx = off + lax.broadcasted_iota(i32, (ck, tq), 0)
        sel = jnp.where(k > thr, 1.0, jnp.where(k == thr, jnp.where(sidx <= jlast, 1.0, 0.0), 0.0))
        o_ref[0, pl.ds(off, ck), :] = jnp.where(sidx <= t_row, sel, 0.0)
        return carry

    lax.fori_loop(0, nck, write, 0)


def _dsa_mask(ik, iq, iw_t):
    B, L, _ = iq.shape
    tq = 128
    ksel = min(DSA_TOPK, L // 4)
    return pl.pallas_call(
        functools.partial(_dsa_mask_kernel, L=L, tq=tq, ksel=ksel),
        grid=(B, L // tq),
        in_specs=[pl.BlockSpec((1, L, IDX_DIM), lambda b, j: (b, 0, 0)),
                  pl.BlockSpec((1, tq, IDX_HEADS * IDX_DIM), lambda b, j: (b, j, 0)),
                  pl.BlockSpec((1, IDX_HEADS, tq), lambda b, j: (b, 0, j))],
        out_specs=pl.BlockSpec((1, L, tq), lambda b, j: (b, 0, j)),
        out_shape=jax.ShapeDtypeStruct((B, L, L), f32),
        scratch_shapes=[pltpu.VMEM((L, tq), i32), pltpu.VMEM((1, tq), i32)],
        compiler_params=_cparams(("parallel", "parallel")),
    )(ik, iq, iw_t)


def _flash_step(s, msk, v_t, m_ref, l_ref, acc_ref):
    s = jnp.where(msk, s, NEG)
    m_prev = m_ref[...]
    m_new = jnp.maximum(m_prev, jnp.max(s, axis=0, keepdims=True))
    alpha = jnp.exp(m_prev - m_new)
    p = jnp.where(msk, jnp.exp(s - m_new), 0.0)
    l_ref[...] = alpha * l_ref[...] + jnp.sum(p, axis=0, keepdims=True)
    acc_ref[...] = acc_ref[...] * alpha + jnp.dot(v_t, p.astype(bf16), preferred_element_type=f32)
    m_ref[...] = m_new


def _flash_init(m_ref, l_ref, acc_ref):
    m_ref[...] = jnp.full(m_ref.shape, NEG, f32)
    l_ref[...] = jnp.zeros(l_ref.shape, f32)
    acc_ref[...] = jnp.zeros(acc_ref.shape, f32)


def _dsa_attn_kernel(q_ref, k_ref, vt_ref, mask_ref, o_ref, m_ref, l_ref, acc_ref, *, tq, tk, nkc):
    j = pl.program_id(1)
    c = pl.program_id(2)

    @pl.when(c == 0)
    def _():
        _flash_init(m_ref, l_ref, acc_ref)

    @pl.when(c * tk < (j + 1) * tq)
    def _():
        msk = mask_ref[0] > 0.0
        for h in range(A_HEADS):
            sl = slice(h * HEAD_DIM, (h + 1) * HEAD_DIM)
            s = _dot_nt(k_ref[0, :, sl], q_ref[0, :, sl])
            _flash_step(s, msk, vt_ref[0, sl, :], m_ref.at[h], l_ref.at[h], acc_ref.at[h])

    @pl.when(c == nkc - 1)
    def _():
        for h in range(A_HEADS):
            o = acc_ref[h] / jnp.maximum(l_ref[h], 1e-30)
            o_ref[0, :, h * HEAD_DIM:(h + 1) * HEAD_DIM] = o.T.astype(o_ref.dtype)


def _dsa_attn(q, k, v_t, mask_t):
    B, L, W = q.shape
    tq, tk = 256, 512
    nkc = L // tk

    def last(j):
        return ((j + 1) * tq - 1) // tk

    return pl.pallas_call(
        functools.partial(_dsa_attn_kernel, tq=tq, tk=tk, nkc=nkc),
        grid=(B, L // tq, nkc),
        in_specs=[pl.BlockSpec((1, tq, W), lambda b, j, c: (b, j, 0)),
                  pl.BlockSpec((1, tk, W), lambda b, j, c: (b, jnp.minimum(c, last(j)), 0)),
                  pl.BlockSpec((1, W, tk), lambda b, j, c: (b, 0, jnp.minimum(c, last(j)))),
                  pl.BlockSpec((1, tk, tq), lambda b, j, c: (b, jnp.minimum(c, last(j)), j))],
        out_specs=pl.BlockSpec((1, tq, W), lambda b, j, c: (b, j, 0)),
        out_shape=jax.ShapeDtypeStruct((B, L, W), bf16),
        scratch_shapes=[pltpu.VMEM((A_HEADS, 1, tq), f32), pltpu.VMEM((A_HEADS, 1, tq), f32),
                        pltpu.VMEM((A_HEADS, HEAD_DIM, tq), f32)],
        compiler_params=_cparams(("parallel", "parallel", "arbitrary")),
    )(q, k, v_t, mask_t)


def _nsa_cmp_kernel(x_ref, w1_ref, w2_ref, pe_ref, o_ref):
    x = x_ref[0, 0, 0]
    w1 = w1_ref[0]
    half = CMP_STRIDE * HEAD_DIM
    a = jnp.dot(x, w1[:half], preferred_element_type=f32)
    b = jnp.dot(x, w1[half:], preferred_element_type=f32)
    pe = jnp.dot(pe_ref[0], w1, preferred_element_type=f32)[0:1]
    n = a.shape[0]
    pre = a + pltpu.roll(b, n - 1, 0) + pe
    act = pre * _sigmoid(pre)
    o_ref[0, 0, 0] = jnp.dot(act.astype(bf16), w2_ref[0], preferred_element_type=f32).astype(o_ref.dtype)


def _nsa_compress(x, w1, w2, pe):
    B, _, G, n, wd = x.shape
    return pl.pallas_call(
        _nsa_cmp_kernel,
        grid=(B, 2, G),
        in_specs=[pl.BlockSpec((1, 1, 1, n, wd), lambda b, t, g: (b, t, g, 0, 0)),
                  pl.BlockSpec((1, CMP_LEN * HEAD_DIM, HEAD_DIM), lambda b, t, g: (t, 0, 0)),
                  pl.BlockSpec((1, HEAD_DIM, HEAD_DIM), lambda b, t, g: (t, 0, 0)),
                  pl.BlockSpec((1, 8, CMP_LEN * HEAD_DIM), lambda b, t, g: (t, 0, 0))],
        out_specs=pl.BlockSpec((1, 1, 1, n, HEAD_DIM), lambda b, t, g: (b, t, g, 0, 0)),
        out_shape=jax.ShapeDtypeStruct((B, 2, G, n, HEAD_DIM), bf16),
        compiler_params=_cparams(("parallel", "parallel", "parallel")),
    )(x, w1, w2, pe)


def _nsa_attn_kernel(q_ref, kc_ref, vct_ref, ks_ref, vst_ref, kw_ref, vwt_ref, g_ref, cov_ref, o_ref,
                     m_ref, l_ref, acc_ref, out_ref, sel_ref, *, L, tq, tk, n_sel):
    j = pl.program_id(2)
    t_row = j * tq + lax.broadcasted_iota(i32, (1, tq), 1)
    n_cmp = kc_ref.shape[2]
    n_slc = L // SLC_LEN
    qs = [q_ref[0, :, hh * HEAD_DIM:(hh + 1) * HEAD_DIM] for hh in range(B_HPG)]

    def gate(hh, i):
        return g_ref[0, 0, hh * 3 + i:hh * 3 + i + 1, :]

    kc = kc_ref[0, 0]
    vct = vct_ref[0, 0]
    cend = lax.broadcasted_iota(i32, (n_cmp, tq), 0) * CMP_STRIDE + (CMP_LEN - 1)
    mc = cend <= t_row
    psum = jnp.zeros((n_cmp, tq), f32)
    for hh in range(B_HPG):
        s = jnp.where(mc, _dot_nt(kc, qs[hh]), NEG)
        mx = jnp.max(s, axis=0, keepdims=True)
        p = jnp.where(mc, jnp.exp(s - mx), 0.0)
        p = p / jnp.maximum(jnp.sum(p, axis=0, keepdims=True), 1e-30)
        psum = psum + p
        out_ref[hh] = gate(hh, 0) * jnp.dot(vct, p.astype(bf16), preferred_element_type=f32)
    imp = jnp.dot(cov_ref[...], psum, preferred_element_type=f32, precision=lax.Precision.HIGHEST)
    blk = lax.broadcasted_iota(i32, (n_slc, tq), 0)
    cur = t_row >> (SLC_LEN.bit_length() - 1)
    forced = (blk == 0) | (blk == cur) | (blk == cur - 1)
    imp = jnp.where(forced, FORCE, jnp.where(blk <= cur, imp, NEG))
    rank = jnp.zeros((n_slc, tq), f32)
    for r in range(n_slc):
        row = imp[r:r + 1, :]
        rank = rank + jnp.where(row > imp, 1.0, jnp.where(row == imp, jnp.where(blk > r, 1.0, 0.0), 0.0))
    sel_ref[...] = jnp.where(rank < float(n_sel), 1.0, 0.0)

    def finish(i):
        for hh in range(B_HPG):
            o = acc_ref[hh] / jnp.maximum(l_ref[hh], 1e-30)
            out_ref[hh] = out_ref[hh] + gate(hh, i) * o

    def run_branch(k_ref, vt_ref, c_lo, c_hi, mask_fn):
        for hh in range(B_HPG):
            _flash_init(m_ref.at[hh], l_ref.at[hh], acc_ref.at[hh])

        def body(c, carry):
            off = pl.multiple_of(c * tk, tk)
            kch = k_ref[0, pl.ds(off, tk), :]
            vch = vt_ref[0, 0, c]
            sidx = off + lax.broadcasted_iota(i32, (tk, tq), 0)
            msk = mask_fn(c, sidx)
            for hh in range(B_HPG):
                _flash_step(_dot_nt(kch, qs[hh]), msk, vch, m_ref.at[hh], l_ref.at[hh], acc_ref.at[hh])
            return carry

        lax.fori_loop(c_lo, c_hi, body, 0)

    bpc = tk // SLC_LEN

    def slc_mask(c, sidx):
        rows = [jnp.broadcast_to(sel_ref[pl.ds(c * bpc + b, 1), :], (SLC_LEN, tq)) for b in range(bpc)]
        return jnp.where(sidx <= t_row, jnp.concatenate(rows, axis=0), 0.0) > 0.0

    c_hi = ((j + 1) * tq + tk - 1) // tk
    run_branch(ks_ref, vst_ref, 0, c_hi, slc_mask)
    finish(1)

    def win_mask(c, sidx):
        d = t_row - sidx
        return jnp.where(d >= 0, jnp.where(d < WIN_LEN, 1.0, 0.0), 0.0) > 0.0

    c_lo = jnp.maximum(j * tq - (WIN_LEN - 1), 0) // tk
    run_branch(kw_ref, vwt_ref, c_lo, c_hi, win_mask)
    finish(2)

    for hh in range(B_HPG):
        o_ref[0, :, hh * HEAD_DIM:(hh + 1) * HEAD_DIM] = out_ref[hh].T.astype(o_ref.dtype)


def _nsa_attn(bq, kc, vc_t, bk, bv_tc, gates_t, cov_t):
    B, L, _ = bq.shape
    G = B_KV_GROUPS
    tq, tk = 128, 512
    n_cmp = kc.shape[2]
    n_slc = L // SLC_LEN
    n_sel = min(SLC_TOPN, n_slc)
    gw = B_HPG * HEAD_DIM
    nc = L // tk
    return pl.pallas_call(
        functools.partial(_nsa_attn_kernel, L=L, tq=tq, tk=tk, n_sel=n_sel),
        grid=(B, G, L // tq),
        in_specs=[pl.BlockSpec((1, tq, gw), lambda b, g, j: (b, j, g)),
                  pl.BlockSpec((1, 1, n_cmp, HEAD_DIM), lambda b, g, j: (b, g, 0, 0)),
                  pl.BlockSpec((1, 1, HEAD_DIM, n_cmp), lambda b, g, j: (b, g, 0, 0)),
                  pl.BlockSpec((1, L, HEAD_DIM), lambda b, g, j: (b, 0, 2 + g)),
                  pl.BlockSpec((1, 1, nc, HEAD_DIM, tk), lambda b, g, j: (b, 2 + g, 0, 0, 0)),
                  pl.BlockSpec((1, L, HEAD_DIM), lambda b, g, j: (b, 0, 4 + g)),
                  pl.BlockSpec((1, 1, nc, HEAD_DIM, tk), lambda b, g, j: (b, 4 + g, 0, 0, 0)),
                  pl.BlockSpec((1, 1, 3 * B_HPG, tq), lambda b, g, j: (b, g, 0, j)),
                  pl.BlockSpec((n_slc, n_cmp), lambda b, g, j: (0, 0))],
        out_specs=pl.BlockSpec((1, tq, gw), lambda b, g, j: (b, j, g)),
        out_shape=jax.ShapeDtypeStruct((B, L, B_HEADS * HEAD_DIM), bf16),
        scratch_shapes=[pltpu.VMEM((B_HPG, 1, tq), f32), pltpu.VMEM((B_HPG, 1, tq), f32),
                        pltpu.VMEM((B_HPG, HEAD_DIM, tq), f32), pltpu.VMEM((B_HPG, HEAD_DIM, tq), f32),
                        pltpu.VMEM((n_slc, tq), f32)],
        compiler_params=_cparams(("parallel", "parallel", "parallel")),
    )(bq, kc, vc_t, bk, bv_tc, bk, bv_tc, gates_t, cov_t)


def _diff_attn_kernel(q_ref, k_ref, vt_ref, lam_ref, g_ref, o_ref, m_ref, l_ref, acc_ref,
                      *, tq, tk, lam_init):
    j = pl.program_id(2)
    t_row = j * tq + lax.broadcasted_iota(i32, (1, tq), 1)
    qs = [q_ref[0, :, mi * C_DIM:(mi + 1) * C_DIM] for mi in range(2)]
    for mi in range(2):
        _flash_init(m_ref.at[mi], l_ref.at[mi], acc_ref.at[mi])

    def body(c, carry):
        off = pl.multiple_of(c * tk, tk)
        vch = vt_ref[0, 0, c]
        sidx = off + lax.broadcasted_iota(i32, (tk, tq), 0)
        msk = sidx <= t_row
        for mi in range(2):
            kch = k_ref[0, pl.ds(off, tk), mi * C_DIM:(mi + 1) * C_DIM]
            _flash_step(_dot_nt(kch, qs[mi]), msk, vch, m_ref.at[mi], l_ref.at[mi], acc_ref.at[mi])
        return carry

    lax.fori_loop(0, ((j + 1) * tq + tk - 1) // tk, body, 0)

    lam = lam_ref[...]
    lam_val = (jnp.exp(jnp.sum(lam[0:1] * lam[1:2], axis=1, keepdims=True))
               - jnp.exp(jnp.sum(lam[2:3] * lam[3:4], axis=1, keepdims=True)) + lam_init)
    o = (acc_ref[0] / jnp.maximum(l_ref[0], 1e-30)
         - lam_val * (acc_ref[1] / jnp.maximum(l_ref[1], 1e-30)))
    ms = jnp.mean(o * o, axis=0, keepdims=True)
    y = o * lax.rsqrt(ms + 1e-6) * g_ref[...] * (1.0 - lam_init)
    o_ref[0] = y.T.astype(o_ref.dtype)


def _diff_attn(cq, ck, cv_tc, lam, g_col, lam_init):
    B, L, _ = cq.shape
    tq, tk = 256, 512
    hw = 2 * C_DIM
    nc = L // tk
    return pl.pallas_call(
        functools.partial(_diff_attn_kernel, tq=tq, tk=tk, lam_init=lam_init),
        grid=(B, C_HEADS, L // tq),
        in_specs=[pl.BlockSpec((1, tq, hw), lambda b, h, j: (b, j, h)),
                  pl.BlockSpec((1, L, hw), lambda b, h, j: (b, 0, h)),
                  pl.BlockSpec((1, 1, nc, hw, tk), lambda b, h, j: (b, h, 0, 0, 0)),
                  pl.BlockSpec((4, C_DIM), lambda b, h, j: (0, 0)),
                  pl.BlockSpec((hw, 1), lambda b, h, j: (0, 0))],
        out_specs=pl.BlockSpec((1, tq, hw), lambda b, h, j: (b, j, h)),
        out_shape=jax.ShapeDtypeStruct((B, L, C_HEADS * hw), bf16),
        scratch_shapes=[pltpu.VMEM((2, 1, tq), f32), pltpu.VMEM((2, 1, tq), f32),
                        pltpu.VMEM((2, hw, tq), f32)],
        compiler_params=_cparams(("parallel", "parallel", "parallel")),
    )(cq, ck, cv_tc, lam, g_col)


def _merge_kernel(ya_ref, yb_ref, yc_ref, w_ref, g0_ref, g1_ref, g2_ref, o_ref):
    acc = None
    for r, (y_ref, g_ref) in enumerate(((ya_ref, g0_ref), (yb_ref, g1_ref), (yc_ref, g2_ref))):
        br = jnp.dot(y_ref[...], w_ref[r], preferred_element_type=f32)
        t = _sigmoid(g_ref[...].astype(f32)) * br
        acc = t if acc is None else acc + t
    o_ref[...] = acc.astype(o_ref.dtype)


def _merge(ya, yb, yc, w_br, h):
    m, kw = ya.shape
    tm, tn = 512, 512
    npb = D_MODEL // tn
    yspec = pl.BlockSpec((tm, kw), lambda j, i: (i, 0))

    def gspec(r):
        return pl.BlockSpec((tm, tn), lambda j, i, _r=r: (i, _r * npb + j))

    return pl.pallas_call(
        _merge_kernel,
        grid=(npb, m // tm),
        in_specs=[yspec, yspec, yspec,
                  pl.BlockSpec((N_BRANCH, kw, tn), lambda j, i: (0, 0, j)),
                  gspec(0), gspec(1), gspec(2)],
        out_specs=pl.BlockSpec((tm, tn), lambda j, i: (i, j)),
        out_shape=jax.ShapeDtypeStruct((m, D_MODEL), bf16),
        compiler_params=_cparams(("parallel", "parallel")),
    )(ya, yb, yc, w_br, h, h, h)


def _mm_res_ln_kernel(a_ref, w_ref, x_ref, gate_ref, lg_ref, lb_ref, sc_ref, sh_ref, *rest,
                      nk, alpha, emit_u):
    if emit_u:
        xo_ref, u_ref, acc_ref = rest
    else:
        xo_ref, acc_ref = rest
    k = pl.program_id(1)

    @pl.when(k == 0)
    def _():
        acc_ref[...] = jnp.zeros(acc_ref.shape, f32)

    acc_ref[...] += jnp.dot(a_ref[...], w_ref[...], preferred_element_type=f32)

    @pl.when(k == nk - 1)
    def _():
        z = alpha * x_ref[...] + gate_ref[0] * acc_ref[...]
        xn = _ln_rows(z, 1e-5) * lg_ref[...] + lb_ref[...]
        xo_ref[...] = xn
        if emit_u:
            u_ref[...] = (_ln_rows(xn, 1e-5) * (1.0 + sc_ref[0]) + sh_ref[0]).astype(u_ref.dtype)


def _mm_res_ln(a, w, x2, gate, ln_g, ln_b, sc, sh, L, alpha, emit_u):
    m, kdim = a.shape
    d = w.shape[1]
    tm, tk = 512, 512
    nk = kdim // tk
    per_b = L // tm
    bspec = pl.BlockSpec((1, 1, d), lambda i, k: (i // per_b, 0, 0))
    vspec = pl.BlockSpec((1, d), lambda i, k: (0, 0))
    rspec = pl.BlockSpec((tm, d), lambda i, k: (i, 0))
    out_shape = [jax.ShapeDtypeStruct((m, d), f32)]
    out_specs = [rspec]
    if emit_u:
        out_shape.append(jax.ShapeDtypeStruct((m, d), bf16))
        out_specs.append(rspec)
    res = pl.pallas_call(
        functools.partial(_mm_res_ln_kernel, nk=nk, alpha=alpha, emit_u=emit_u),
        grid=(m // tm, nk),
        in_specs=[pl.BlockSpec((tm, tk), lambda i, k: (i, k)),
                  pl.BlockSpec((tk, d), lambda i, k: (k, 0)),
                  rspec, bspec, vspec, vspec, bspec, bspec],
        out_specs=out_specs,
        out_shape=out_shape,
        scratch_shapes=[pltpu.VMEM((tm, d), f32)],
        compiler_params=_cparams(("parallel", "arbitrary"), 56 * 1024 * 1024),
    )(a, w, x2, gate, ln_g.reshape(1, d), ln_b.reshape(1, d), sc, sh)
    return res if emit_u else (res[0], None)


def _ffn_in_kernel(a_ref, wg_ref, wu_ref, o_ref):
    a = a_ref[...]
    g = jnp.dot(a, wg_ref[...], preferred_element_type=f32)
    u = jnp.dot(a, wu_ref[...], preferred_element_type=f32)
    o_ref[...] = (g * _sigmoid(g) * u).astype(o_ref.dtype)


def _ffn_in(a, w):
    m, k = a.shape
    tm, tn = 512, 512
    nb = D_FF // tn
    return pl.pallas_call(
        _ffn_in_kernel,
        grid=(nb, m // tm),
        in_specs=[pl.BlockSpec((tm, k), lambda j, i: (i, 0)),
                  pl.BlockSpec((k, tn), lambda j, i: (0, j)),
                  pl.BlockSpec((k, tn), lambda j, i: (0, nb + j))],
        out_specs=pl.BlockSpec((tm, tn), lambda j, i: (i, j)),
        out_shape=jax.ShapeDtypeStruct((m, D_FF), bf16),
        compiler_params=_cparams(("parallel", "parallel")),
    )(a, w, w)


def _rope_tables(L, d):
    r = d // 4
    half = r // 2
    inv = ROPE_THETA ** (-(jnp.arange(half, dtype=f32) * 2.0) / r)
    ang = jnp.arange(L).astype(f32)[:, None] * inv[None, :]
    cos, sin = jnp.cos(ang), jnp.sin(ang)
    z = jnp.zeros((L, d - r), f32)
    zh = jnp.zeros((L, half), f32)
    c = jnp.concatenate([cos, cos, jnp.ones((L, d - r), f32)], axis=1)
    s1 = jnp.concatenate([zh, sin, z], axis=1)
    s2 = jnp.concatenate([-sin, zh, z], axis=1)
    rep = 128 // d
    return tuple(jnp.tile(t, (1, rep)) for t in (c, s1, s2))


def _pack_w_in(w):
    def cols(o, n):
        return w[:, o:o + n]

    bk = [cols(_O_BKV + ((i * 2 + 0) * B_KV_GROUPS + g) * HEAD_DIM, HEAD_DIM)
          for i in range(3) for g in range(B_KV_GROUPS)]
    bv = [cols(_O_BKV + ((i * 2 + 1) * B_KV_GROUPS + g) * HEAD_DIM, HEAD_DIM)
          for i in range(3) for g in range(B_KV_GROUPS)]
    k = w.shape[0]
    parts = [cols(_O_GL, N_BRANCH * D_MODEL), cols(_O_AQ, 1024), cols(_O_IQ, 1024), cols(_O_BQ, 1024),
             cols(_O_CQ, 1024), cols(_O_CK, 1024), cols(_O_CV, 1024)] + bk + bv + [
             cols(_O_ALAT, A_LATENT), cols(_O_IK, IDX_DIM), cols(_O_IW, IDX_HEADS),
             jnp.zeros((k, 128 - IDX_DIM - IDX_HEADS), w.dtype),
             cols(_O_BG, 3 * B_HEADS), jnp.zeros((k, 128 - 3 * B_HEADS), w.dtype)]
    return jnp.concatenate(parts, axis=1).astype(bf16)


def _cover_t(L):
    n_cmp_pad = L // CMP_STRIDE
    starts = np.arange(n_cmp_pad) * CMP_STRIDE
    slc_start = np.arange(L // SLC_LEN) * SLC_LEN
    cover = ((starts[:, None] < slc_start[None, :] + SLC_LEN)
             & (starts[:, None] + CMP_LEN > slc_start[None, :])).astype(np.float32)
    n_cmp = (L - CMP_LEN) // CMP_STRIDE + 1
    cover[n_cmp:] = 0.0
    return jnp.asarray(cover.T)


def _chunked_t(v, tk):
    B, L, n, d = v.shape
    return v.reshape(B, L // tk, tk, n, d).transpose(0, 3, 1, 4, 2)


def _token_mixing(h, B, L, lw, tabs128, tabs64, cov_t, lam_init):
    M = B * L
    G = B_KV_GROUPS
    aq, iq, bq, cq, ck, bk, alat_n, ikw, gates = _prep(h, lw['a_lat_g'], tabs128, tabs64, L)

    ak = _mm_rope(alat_n, lw['a_up_k'], tabs128, L, 512, 1024)
    av = _mm(alat_n, lw['a_up_v'], 512, 1024, bf16)
    ik = ikw[:, :IDX_DIM].astype(bf16).reshape(B, L, IDX_DIM)
    iw_t = ikw[:, IDX_DIM:IDX_DIM + IDX_HEADS].reshape(B, L, IDX_HEADS).transpose(0, 2, 1)
    mask_t = _dsa_mask(ik, iq.reshape(B, L, -1), iw_t)
    av_t = av.reshape(B, L, -1).transpose(0, 2, 1)
    ya = _dsa_attn(aq.reshape(B, L, -1), ak.reshape(B, L, -1), av_t, mask_t)

    bv = h[:, _P_BV:_P_BV + 768].reshape(B, L, 3 * G, HEAD_DIM)
    bk4 = bk.reshape(B, L, 3 * G, HEAD_DIM)
    n_row = L // CMP_STRIDE
    xcmp = jnp.stack([bk4[:, :, 0:G], bv[:, :, 0:G]], axis=1)
    xcmp = xcmp.transpose(0, 1, 3, 2, 4).reshape(B, 2, G, n_row, CMP_STRIDE * HEAD_DIM)
    cmp_out = _nsa_compress(xcmp, lw['cmp_w1'], lw['cmp_w2'], lw['cmp_pe'])
    kc = cmp_out[:, 0]
    vc_t = cmp_out[:, 1].transpose(0, 1, 3, 2)
    bv_tc = _chunked_t(bv, 512)
    gates_t = gates[:, :3 * B_HEADS].reshape(B, L, G, 3 * B_HPG).transpose(0, 2, 3, 1)
    yb = _nsa_attn(bq.reshape(B, L, -1), kc, vc_t, bk.reshape(B, L, -1), bv_tc, gates_t, cov_t)

    cv = h[:, _P_CV:_P_CV + 1024].reshape(B, L, C_HEADS, 2 * C_DIM)
    cv_tc = _chunked_t(cv, 512)
    yc = _diff_attn(cq.reshape(B, L, -1), ck.reshape(B, L, -1), cv_tc, lw['lam'],
                    lw['c_subln_g'].reshape(2 * C_DIM, 1), lam_init)

    return _merge(ya.reshape(M, -1), yb.reshape(M, -1), yc.reshape(M, -1), lw['w_br'], h)


def kernel(x, c, w_ada, b_ada, w_in, a_lat_g, a_up, cmp_w1, cmp_w2, cmp_pe, lam, c_subln_g, w_br, w_o,
           w_ffn_in, w_ffn_out, ln_g, ln_b):
    B, L, D = x.shape
    depth = w_ada.shape[0]
    M = B * L
    alpha = (2 * depth) ** 0.25

    c_pad = jnp.zeros((8, D), f32).at[:B].set(c)
    mod = _ada(c_pad, w_ada, b_ada)[:, :B]
    mods = [[mod[l, :, i * D:(i + 1) * D].reshape(B, 1, D) for i in range(6)] for l in range(depth)]

    tabs128 = _rope_tables(L, HEAD_DIM)
    tabs64 = _rope_tables(L, IDX_DIM)
    cov_t = _cover_t(L)

    x2 = x.reshape(M, D)
    u = _lnmod(x2, mods[0][1], mods[0][0], L)
    for l in range(depth):
        lam_init = 0.8 - 0.6 * math.exp(-0.3 * l)
        sh_a, sc_a, g_a, sh_f, sc_f, g_f = mods[l]
        pe_flat = jnp.zeros((2, 8, CMP_LEN * HEAD_DIM), f32).at[:, 0].set(
            cmp_pe[l].reshape(2, CMP_LEN * HEAD_DIM)).astype(bf16)
        lw = dict(a_lat_g=a_lat_g[l],
                  a_up_k=a_up[l][:, :A_HEADS * HEAD_DIM].astype(bf16),
                  a_up_v=a_up[l][:, A_HEADS * HEAD_DIM:].astype(bf16),
                  cmp_w1=cmp_w1[l].astype(bf16), cmp_w2=cmp_w2[l].astype(bf16), cmp_pe=pe_flat,
                  lam=lam[l], c_subln_g=c_subln_g[l], w_br=w_br[l].astype(bf16))
        h = _mm(u, _pack_w_in(w_in[l]), 512, 768, bf16)
        merged = _token_mixing(h, B, L, lw, tabs128, tabs64, cov_t, lam_init)
        x2, u = _mm_res_ln(merged, w_o[l].astype(bf16), x2, g_a, ln_g[l, 0], ln_b[l, 0], sc_f, sh_f,
                           L, alpha, True)
        f = _ffn_in(u, w_ffn_in[l].astype(bf16))
        last = l == depth - 1
        nsc, nsh = (sc_f, sh_f) if last else (mods[l + 1][1], mods[l + 1][0])
        x2, u = _mm_res_ln(f, w_ffn_out[l].astype(bf16), x2, g_f, ln_g[l, 1], ln_b[l, 1], nsc, nsh,
                           L, alpha, not last)
    return x2.reshape(B, L, D)
```

```python
import functools
import math

import numpy as np
import jax
import jax.numpy as jnp
from jax import lax
from jax.experimental import pallas as pl
from jax.experimental.pallas import tpu as pltpu

f32 = jnp.float32
bf16 = jnp.bfloat16
i32 = jnp.int32

D_MODEL = 2048
HEAD_DIM = 128
ROPE_THETA = 500000.0
NEG = -1e30
FORCE = 1e6
A_HEADS = 8
A_LATENT = 512
IDX_HEADS = 16
IDX_DIM = 64
DSA_TOPK = 256
B_HEADS = 8
B_KV_GROUPS = 2
B_HPG = B_HEADS // B_KV_GROUPS
CMP_LEN = 32
CMP_STRIDE = 16
SLC_LEN = 64
SLC_TOPN = 16
WIN_LEN = 512
C_HEADS = 4
C_DIM = 128
BRANCH_W = A_HEADS * HEAD_DIM
N_BRANCH = 3
D_FF = int(math.ceil(8 * D_MODEL / 3 / 256)) * 256

_O_AQ = 0
_O_ALAT = _O_AQ + A_HEADS * HEAD_DIM
_O_IQ = _O_ALAT + A_LATENT
_O_IK = _O_IQ + IDX_HEADS * IDX_DIM
_O_IW = _O_IK + IDX_DIM
_O_BQ = _O_IW + IDX_HEADS
_O_BKV = _O_BQ + B_HEADS * HEAD_DIM
_O_BG = _O_BKV + 3 * 2 * B_KV_GROUPS * HEAD_DIM
_O_CQ = _O_BG + 3 * B_HEADS
_O_CK = _O_CQ + C_HEADS * 2 * C_DIM
_O_CV = _O_CK + C_HEADS * 2 * C_DIM
_O_GL = _O_CV + C_HEADS * 2 * C_DIM
_N_IN = _O_GL + N_BRANCH * D_MODEL

_P_GL = 0
_P_AQ = 6144
_P_IQ = 7168
_P_BQ = 8192
_P_CQ = 9216
_P_CK = 10240
_P_CV = 11264
_P_BK = 12288
_P_BV = 13056
_P_ALAT = 13824
_P_IKW = 14336
_P_BG = 14464
_P_TOT = 14592

_VMEM_LIMIT = 48 * 1024 * 1024
_LOG2E = 1.4426950408889634
_INF = float("inf")


def _cparams(sem, vmem=_VMEM_LIMIT):
    return pltpu.CompilerParams(dimension_semantics=sem, vmem_limit_bytes=vmem)


def _sigmoid(x):
    return 1.0 / (1.0 + jnp.exp(-x))


def _dot_nt(a, b):
    return lax.dot_general(a, b, (((1,), (1,)), ((), ())), preferred_element_type=f32)


def _ada_kernel(c_ref, w_ref, b_ref, o_ref):
    c = c_ref[...]
    cs = c * _sigmoid(c)
    o_ref[0] = jnp.dot(cs, w_ref[0], preferred_element_type=f32,
                       precision=lax.Precision.HIGHEST) + b_ref[0]


def _ada(c_pad, w_ada, b_ada):
    depth, d, n = w_ada.shape
    tn = 512
    return pl.pallas_call(
        _ada_kernel,
        grid=(depth, n // tn),
        in_specs=[pl.BlockSpec((8, d), lambda l, j: (0, 0)),
                  pl.BlockSpec((1, d, tn), lambda l, j: (l, 0, j)),
                  pl.BlockSpec((1, 1, tn), lambda l, j: (l, 0, j))],
        out_specs=pl.BlockSpec((1, 8, tn), lambda l, j: (l, 0, j)),
        out_shape=jax.ShapeDtypeStruct((depth, 8, n), f32),
        compiler_params=_cparams(("parallel", "parallel")),
    )(c_pad, w_ada, b_ada.reshape(depth, 1, n))


def _ln_rows(x, eps):
    mu = jnp.mean(x, axis=-1, keepdims=True)
    d = x - mu
    var = jnp.mean(d * d, axis=-1, keepdims=True)
    return d * lax.rsqrt(var + eps)


def _lnmod_kernel(x_ref, sc_ref, sh_ref, o_ref):
    y = _ln_rows(x_ref[...], 1e-5)
    o_ref[...] = (y * (1.0 + sc_ref[0]) + sh_ref[0]).astype(o_ref.dtype)


def _lnmod(x2, sc, sh, L):
    m, d = x2.shape
    tm = 512
    per_b = L // tm
    return pl.pallas_call(
        _lnmod_kernel,
        grid=(m // tm,),
        in_specs=[pl.BlockSpec((tm, d), lambda i: (i, 0)),
                  pl.BlockSpec((1, 1, d), lambda i: (i // per_b, 0, 0)),
                  pl.BlockSpec((1, 1, d), lambda i: (i // per_b, 0, 0))],
        out_specs=pl.BlockSpec((tm, d), lambda i: (i, 0)),
        out_shape=jax.ShapeDtypeStruct((m, d), bf16),
        compiler_params=_cparams(("parallel",)),
    )(x2, sc, sh)


def _mm_kernel(a_ref, w_ref, o_ref):
    o_ref[...] = jnp.dot(a_ref[...], w_ref[...], preferred_element_type=f32).astype(o_ref.dtype)


def _mm(a, w, tm, tn, out_dtype):
    m, k = a.shape
    n = w.shape[1]
    return pl.pallas_call(
        _mm_kernel,
        grid=(n // tn, m // tm),
        in_specs=[pl.BlockSpec((tm, k), lambda j, i: (i, 0)),
                  pl.BlockSpec((k, tn), lambda j, i: (0, j))],
        out_specs=pl.BlockSpec((tm, tn), lambda j, i: (i, j)),
        out_shape=jax.ShapeDtypeStruct((m, n), out_dtype),
        compiler_params=_cparams(("parallel", "parallel")),
    )(a, w)


def _rope_heads(x, c, s1, s2, half):
    return x * c + pltpu.roll(x, half, 1) * s1 + pltpu.roll(x, 128 - half, 1) * s2


def _mm_rope_kernel(a_ref, w_ref, c_ref, s1_ref, s2_ref, o_ref):
    acc = jnp.dot(a_ref[...], w_ref[...], preferred_element_type=f32)
    c, s1, s2 = c_ref[...], s1_ref[...], s2_ref[...]
    for h in range(acc.shape[1] // 128):
        sl = slice(h * 128, (h + 1) * 128)
        o_ref[:, sl] = _rope_heads(acc[:, sl], c, s1, s2, 16).astype(o_ref.dtype)


def _mm_rope(a, w, tabs, L, tm, tn):
    m, k = a.shape
    n = w.shape[1]
    per_b = L // tm
    tspec = pl.BlockSpec((tm, 128), lambda j, i: (i % per_b, 0))
    return pl.pallas_call(
        _mm_rope_kernel,
        grid=(n // tn, m // tm),
        in_specs=[pl.BlockSpec((tm, k), lambda j, i: (i, 0)),
                  pl.BlockSpec((k, tn), lambda j, i: (0, j)),
                  tspec, tspec, tspec],
        out_specs=pl.BlockSpec((tm, tn), lambda j, i: (i, j)),
        out_shape=jax.ShapeDtypeStruct((m, n), bf16),
        compiler_params=_cparams(("parallel", "parallel")),
    )(a, w, *tabs)


def _prep_kernel(aq_ref, iq_ref, bq_ref, cq_ref, ck_ref, bk_ref, alat_ref, ikw_ref, bg_ref, alg_ref,
                 c_ref, s1_ref, s2_ref, c6_ref, s16_ref, s26_ref,
                 aq_o, iq_o, bq_o, cq_o, ck_o, bk_o, alat_o, ikw_o, g_o):
    c, s1, s2 = c_ref[...], s1_ref[...], s2_ref[...]
    c6, s16, s26 = c6_ref[...], s16_ref[...], s26_ref[...]
    scale = HEAD_DIM ** -0.5 * _LOG2E

    def rope_all(src, dst, mult, tabs, half):
        for h in range(src.shape[1] // 128):
            sl = slice(h * 128, (h + 1) * 128)
            y = _rope_heads(src[:, sl].astype(f32), *tabs, half)
            if mult != 1.0:
                y = y * mult
            dst[:, sl] = y.astype(dst.dtype)

    rope_all(aq_ref, aq_o, scale, (c, s1, s2), 16)
    rope_all(bq_ref, bq_o, scale, (c, s1, s2), 16)
    rope_all(cq_ref, cq_o, C_DIM ** -0.5 * _LOG2E, (c, s1, s2), 16)
    rope_all(ck_ref, ck_o, 1.0, (c, s1, s2), 16)
    rope_all(bk_ref, bk_o, 1.0, (c, s1, s2), 16)
    rope_all(iq_ref, iq_o, 1.0, (c6, s16, s26), 8)

    a = alat_ref[...].astype(f32)
    ms = jnp.mean(a * a, axis=-1, keepdims=True)
    alat_o[...] = (a * lax.rsqrt(ms + 1e-6) * alg_ref[...]).astype(alat_o.dtype)

    x = ikw_ref[...].astype(f32)
    lane = lax.broadcasted_iota(i32, x.shape, 1)
    isk = lane < IDX_DIM
    mu = jnp.sum(jnp.where(isk, x, 0.0), axis=-1, keepdims=True) * (1.0 / IDX_DIM)
    d = jnp.where(isk, x - mu, 0.0)
    var = jnp.sum(d * d, axis=-1, keepdims=True) * (1.0 / IDX_DIM)
    y = d * lax.rsqrt(var + 1e-5)
    yr = _rope_heads(y, c6, s16, s26, 8)
    ikw_o[...] = jnp.where(isk, yr, x * (IDX_DIM ** -0.5 * IDX_HEADS ** -0.5))

    g_o[...] = _sigmoid(bg_ref[...].astype(f32))


def _prep(h, a_lat_g, tabs128, tabs64, L):
    m = h.shape[0]
    tm = 256
    per_b = L // tm

    def hs(width, off):
        return pl.BlockSpec((tm, width), lambda i, _o=off // width: (i, _o))

    tspec = pl.BlockSpec((tm, 128), lambda i: (i % per_b, 0))

    def os(width):
        return pl.BlockSpec((tm, width), lambda i: (i, 0))

    outs = [(1024, bf16)] * 5 + [(768, bf16), (512, bf16), (128, f32), (128, f32)]
    return pl.pallas_call(
        _prep_kernel,
        grid=(m // tm,),
        in_specs=[hs(1024, _P_AQ), hs(1024, _P_IQ), hs(1024, _P_BQ), hs(1024, _P_CQ), hs(1024, _P_CK),
                  hs(768, _P_BK), hs(512, _P_ALAT), hs(128, _P_IKW), hs(128, _P_BG),
                  pl.BlockSpec((1, A_LATENT), lambda i: (0, 0))] + [tspec] * 6,
        out_specs=[os(w) for w, _ in outs],
        out_shape=[jax.ShapeDtypeStruct((m, w), dt) for w, dt in outs],
        compiler_params=_cparams(("parallel",)),
    )(h, h, h, h, h, h, h, h, h, a_lat_g.reshape(1, A_LATENT), *tabs128, *tabs64)


def _f32_order_key(x):
    b = int(np.float32(x).view(np.int32))
    return b ^ ((b >> 31) & 0x7FFFFFFF)


_KEY_NEG = _f32_order_key(NEG)


def _dsa_mask_kernel(ik_ref, iq_ref, iw_ref, o_ref, key_ref, qp_ref, j_ref, *, L, tq, ksel):
    j = pl.program_id(1)
    ck = 512
    nck = L // ck
    nc = ((j + 1) * tq + ck - 1) // ck
    n_out_i = L - nc * ck
    n_out = n_out_i.astype(f32)
    t_row = j * tq + lax.broadcasted_iota(i32, (1, tq), 1)

    for hp in range(IDX_HEADS // 2):
        for e in range(2):
            h = 2 * hp + e
            qp_ref[hp, e * tq:(e + 1) * tq, :] = iq_ref[0, :, h * IDX_DIM:(h + 1) * IDX_DIM]

    def score_chunk(c, carry):
        off = pl.multiple_of(c * ck, ck)
        ikc = ik_ref[0, pl.ds(off, ck), :]
        acc = jnp.zeros((ck, tq), f32)
        for hp in range(IDX_HEADS // 2):
            s2 = _dot_nt(ikc, qp_ref[hp])
            acc = acc + jnp.maximum(s2[:, :tq], 0.0) * iw_ref[0, 2 * hp:2 * hp + 1, :]
            acc = acc + jnp.maximum(s2[:, tq:], 0.0) * iw_ref[0, 2 * hp + 1:2 * hp + 2, :]
        acc = jnp.where(acc == 0.0, 0.0, acc)
        sidx = off + lax.broadcasted_iota(i32, (ck, tq), 0)
        sc = jnp.where(sidx <= t_row, acc, NEG)
        bits = lax.bitcast_convert_type(sc, i32)
        key_ref[pl.ds(off, ck), :] = bits ^ ((bits >> 31) & 0x7FFFFFFF)
        return carry

    lax.fori_loop(0, nc, score_chunk, 0)

    def count(pred_fn):
        def body(c, cnt):
            off = pl.multiple_of(c * ck, ck)
            k = key_ref[pl.ds(off, ck), :]
            sidx = off + lax.broadcasted_iota(i32, (ck, tq), 0)
            return cnt + jnp.sum(pred_fn(k, sidx).reshape(ck // 64, 64, tq), axis=0)
        part = lax.fori_loop(0, nc, body, jnp.zeros((64, tq), f32))
        return jnp.sum(part, axis=0, keepdims=True)

    kf = float(ksel)

    def bit_body(i, carry):
        thr, cnt_thr = carry
        cand = thr + lax.shift_left(jnp.int32(1), 31 - i)
        cnt = count(lambda k, s: jnp.where(k >= cand, 1.0, 0.0)) + jnp.where(_KEY_NEG >= cand, n_out, 0.0)
        ok = cnt >= kf
        return jnp.where(ok, cand, thr), jnp.where(ok, cnt, cnt_thr)

    thr, cnt_ge = lax.fori_loop(0, 32, bit_body, (jnp.full((1, tq), -2 ** 31, i32),
                                                   jnp.full((1, tq), float(L), f32)))
    cnt_gt = count(lambda k, s: jnp.where(k > thr, 1.0, 0.0)) + jnp.where(_KEY_NEG > thr, n_out, 0.0)
    need = kf - cnt_gt

    j_ref[...] = jnp.full((1, tq), L, i32)

    @pl.when(jnp.max(cnt_ge) > kf)
    def _():
        nbits = L.bit_length() - 1

        def jbit(i, cur):
            cand = cur | lax.shift_left(jnp.int32(1), nbits - 1 - i)
            f = count(lambda k, s: jnp.where(k == thr, jnp.where(s < cand, 1.0, 0.0), 0.0))
            f = f + jnp.where(thr == _KEY_NEG, jnp.clip(cand - nc * ck, 0, n_out_i).astype(f32), 0.0)
            return jnp.where(f < need, cand, cur)

        j_ref[...] = lax.fori_loop(0, nbits, jbit, jnp.zeros((1, tq), i32))

    jlast = j_ref[...]

    def write(c, carry):
        off = pl.multiple_of(c * ck, ck)
        k = key_ref[pl.ds(off, ck), :]
        sidx = off + lax.broadcasted_iota(i32, (ck, tq), 0)
        sel = jnp.where(k > thr, _INF, jnp.where(k == thr, jnp.where(sidx <= jlast, _INF, NEG), NEG))
        o_ref[0, pl.ds(off, ck), :] = jnp.where(sidx <= t_row, sel, NEG)
        return carry

    lax.fori_loop(0, nc, write, 0)

    def write_rest(c, carry):
        off = pl.multiple_of(c * ck, ck)
        o_ref[0, pl.ds(off, ck), :] = jnp.full((ck, tq), NEG, f32)
        return carry

    lax.fori_loop(nc, nck, write_rest, 0)


def _dsa_mask(ik, iq, iw_t):
    B, L, _ = iq.shape
    tq = 128
    ksel = min(DSA_TOPK, L // 4)
    return pl.pallas_call(
        functools.partial(_dsa_mask_kernel, L=L, tq=tq, ksel=ksel),
        grid=(B, L // tq),
        in_specs=[pl.BlockSpec((1, L, IDX_DIM), lambda b, j: (b, 0, 0)),
                  pl.BlockSpec((1, tq, IDX_HEADS * IDX_DIM), lambda b, j: (b, j, 0)),
                  pl.BlockSpec((1, IDX_HEADS, tq), lambda b, j: (b, 0, j))],
        out_specs=pl.BlockSpec((1, L, tq), lambda b, j: (b, 0, j)),
        out_shape=jax.ShapeDtypeStruct((B, L, L), f32),
        scratch_shapes=[pltpu.VMEM((L, tq), i32), pltpu.VMEM((IDX_HEADS // 2, 2 * tq, IDX_DIM), bf16),
                        pltpu.VMEM((1, tq), i32)],
        compiler_params=_cparams(("parallel", "parallel")),
    )(ik, iq, iw_t)


def _col_reduce(x, op):
    r, c = x.shape
    if r > 64:
        x = op(x.reshape(r // 64, 64, c), axis=0)
    return op(x, axis=0, keepdims=True)


def _flash_step(s, cap, v_t, m_ref, l_ref, acc_ref):
    if cap is not None:
        s = jnp.minimum(s, cap)
    m_prev = m_ref[...]
    m_new = jnp.maximum(m_prev, _col_reduce(s, jnp.max))
    alpha = jnp.exp2(m_prev - m_new)
    p = jnp.exp2(s - m_new)
    l_new = alpha * l_ref[...] + _col_reduce(p, jnp.sum)
    acc_new = acc_ref[...] * alpha + jnp.dot(v_t, p.astype(bf16), preferred_element_type=f32)
    if cap is not None:
        dead = m_new <= NEG
        l_new = jnp.where(dead, 0.0, l_new)
        acc_new = jnp.where(dead, 0.0, acc_new)
    l_ref[...] = l_new
    acc_ref[...] = acc_new
    m_ref[...] = m_new


def _flash_init(m_ref, l_ref, acc_ref):
    m_ref[...] = jnp.full(m_ref.shape, NEG, f32)
    l_ref[...] = jnp.zeros(l_ref.shape, f32)
    acc_ref[...] = jnp.zeros(acc_ref.shape, f32)


def _dsa_attn_kernel(q_ref, k_ref, vt_ref, mask_ref, o_ref, m_ref, l_ref, acc_ref, *, tq, tk, nkc):
    j = pl.program_id(1)
    c = pl.program_id(2)

    @pl.when(c == 0)
    def _():
        _flash_init(m_ref, l_ref, acc_ref)

    @pl.when(c * tk < (j + 1) * tq)
    def _():
        cap = mask_ref[0]

        def scores(h):
            sl = slice(h * HEAD_DIM, (h + 1) * HEAD_DIM)
            return _dot_nt(k_ref[0, :, sl], q_ref[0, :, sl])

        s_next = scores(0)
        for h in range(A_HEADS):
            s = s_next
            if h + 1 < A_HEADS:
                s_next = scores(h + 1)
            sl = slice(h * HEAD_DIM, (h + 1) * HEAD_DIM)
            _flash_step(s, cap, vt_ref[0, sl, :], m_ref.at[h], l_ref.at[h], acc_ref.at[h])

    @pl.when(c == nkc - 1)
    def _():
        for h in range(A_HEADS):
            o = acc_ref[h] / jnp.maximum(l_ref[h], 1e-30)
            o_ref[0, :, h * HEAD_DIM:(h + 1) * HEAD_DIM] = o.T.astype(o_ref.dtype)


def _dsa_attn(q, k, v_t, mask_t):
    B, L, W = q.shape
    tq, tk = 256, 512
    nkc = L // tk

    def last(j):
        return ((j + 1) * tq - 1) // tk

    return pl.pallas_call(
        functools.partial(_dsa_attn_kernel, tq=tq, tk=tk, nkc=nkc),
        grid=(B, L // tq, nkc),
        in_specs=[pl.BlockSpec((1, tq, W), lambda b, j, c: (b, j, 0)),
                  pl.BlockSpec((1, tk, W), lambda b, j, c: (b, jnp.minimum(c, last(j)), 0)),
                  pl.BlockSpec((1, W, tk), lambda b, j, c: (b, 0, jnp.minimum(c, last(j)))),
                  pl.BlockSpec((1, tk, tq), lambda b, j, c: (b, jnp.minimum(c, last(j)), j))],
        out_specs=pl.BlockSpec((1, tq, W), lambda b, j, c: (b, j, 0)),
        out_shape=jax.ShapeDtypeStruct((B, L, W), bf16),
        scratch_shapes=[pltpu.VMEM((A_HEADS, 1, tq), f32), pltpu.VMEM((A_HEADS, 1, tq), f32),
                        pltpu.VMEM((A_HEADS, HEAD_DIM, tq), f32)],
        compiler_params=_cparams(("parallel", "parallel", "arbitrary")),
    )(q, k, v_t, mask_t)


def _nsa_cmp_kernel(x_ref, w1_ref, w2_ref, pe_ref, o_ref):
    x = x_ref[0, 0, 0]
    w1 = w1_ref[0]
    half = CMP_STRIDE * HEAD_DIM
    a = jnp.dot(x, w1[:half], preferred_element_type=f32)
    b = jnp.dot(x, w1[half:], preferred_element_type=f32)
    pe = jnp.dot(pe_ref[0], w1, preferred_element_type=f32)[0:1]
    n = a.shape[0]
    pre = a + pltpu.roll(b, n - 1, 0) + pe
    act = pre * _sigmoid(pre)
    o_ref[0, 0, 0] = jnp.dot(act.astype(bf16), w2_ref[0], preferred_element_type=f32).astype(o_ref.dtype)


def _nsa_compress(x, w1, w2, pe):
    B, _, G, n, wd = x.shape
    return pl.pallas_call(
        _nsa_cmp_kernel,
        grid=(B, 2, G),
        in_specs=[pl.BlockSpec((1, 1, 1, n, wd), lambda b, t, g: (b, t, g, 0, 0)),
                  pl.BlockSpec((1, CMP_LEN * HEAD_DIM, HEAD_DIM), lambda b, t, g: (t, 0, 0)),
                  pl.BlockSpec((1, HEAD_DIM, HEAD_DIM), lambda b, t, g: (t, 0, 0)),
                  pl.BlockSpec((1, 8, CMP_LEN * HEAD_DIM), lambda b, t, g: (t, 0, 0))],
        out_specs=pl.BlockSpec((1, 1, 1, n, HEAD_DIM), lambda b, t, g: (b, t, g, 0, 0)),
        out_shape=jax.ShapeDtypeStruct((B, 2, G, n, HEAD_DIM), bf16),
        compiler_params=_cparams(("parallel", "parallel", "parallel")),
    )(x, w1, w2, pe)


def _nsa_attn_kernel(q_ref, kc_ref, vct_ref, ks_ref, vst_ref, kw_ref, vwt_ref, g_ref, cov_ref, o_ref,
                     m_ref, l_ref, acc_ref, out_ref, sel_ref, *, L, tq, tk, n_sel):
    j = pl.program_id(2)
    t_row = j * tq + lax.broadcasted_iota(i32, (1, tq), 1)
    n_cmp = kc_ref.shape[2]
    n_slc = L // SLC_LEN
    qs = [q_ref[0, :, hh * HEAD_DIM:(hh + 1) * HEAD_DIM] for hh in range(B_HPG)]

    def gate(hh, i):
        return g_ref[0, 0, hh * 3 + i:hh * 3 + i + 1, :]

    kc = kc_ref[0, 0]
    vct = vct_ref[0, 0]
    cend = lax.broadcasted_iota(i32, (n_cmp, tq), 0) * CMP_STRIDE + (CMP_LEN - 1)
    mc = cend <= t_row
    psum = jnp.zeros((n_cmp, tq), f32)
    for hh in range(B_HPG):
        s = jnp.where(mc, _dot_nt(kc, qs[hh]), NEG)
        mx = _col_reduce(s, jnp.max)
        p = jnp.where(mc, jnp.exp2(s - mx), 0.0)
        p = p / jnp.maximum(_col_reduce(p, jnp.sum), 1e-30)
        psum = psum + p
        out_ref[hh] = gate(hh, 0) * jnp.dot(vct, p.astype(bf16), preferred_element_type=f32)
    imp = jnp.dot(cov_ref[...], psum, preferred_element_type=f32, precision=lax.Precision.HIGHEST)
    blk = lax.broadcasted_iota(i32, (n_slc, tq), 0)
    cur = t_row >> (SLC_LEN.bit_length() - 1)
    forced = (blk == 0) | (blk == cur) | (blk == cur - 1)
    imp = jnp.where(forced, FORCE, jnp.where(blk <= cur, imp, NEG))
    rank = jnp.zeros((n_slc, tq), f32)
    for r in range(n_slc):
        row = imp[r:r + 1, :]
        rank = rank + jnp.where(row > imp, 1.0, jnp.where(row == imp, jnp.where(blk > r, 1.0, 0.0), 0.0))
    sel_ref[...] = jnp.where(rank < float(n_sel), _INF, NEG)

    def finish(i):
        for hh in range(B_HPG):
            o = acc_ref[hh] / jnp.maximum(l_ref[hh], 1e-30)
            out_ref[hh] = out_ref[hh] + gate(hh, i) * o

    def run_branch(k_ref, vt_ref, c_lo, c_hi, mask_fn):
        for hh in range(B_HPG):
            _flash_init(m_ref.at[hh], l_ref.at[hh], acc_ref.at[hh])

        def body(c, carry):
            off = pl.multiple_of(c * tk, tk)
            kch = k_ref[0, pl.ds(off, tk), :]
            vch = vt_ref[0, 0, c]
            sidx = off + lax.broadcasted_iota(i32, (tk, tq), 0)
            cap = mask_fn(c, sidx)
            s_next = _dot_nt(kch, qs[0])
            for hh in range(B_HPG):
                s = s_next
                if hh + 1 < B_HPG:
                    s_next = _dot_nt(kch, qs[hh + 1])
                _flash_step(s, cap, vch, m_ref.at[hh], l_ref.at[hh], acc_ref.at[hh])
            return carry

        lax.fori_loop(c_lo, c_hi, body, 0)

    bpc = tk // SLC_LEN

    def slc_mask(c, sidx):
        rows = [jnp.broadcast_to(sel_ref[pl.ds(c * bpc + b, 1), :], (SLC_LEN, tq)) for b in range(bpc)]
        return jnp.where(sidx <= t_row, jnp.concatenate(rows, axis=0), NEG)

    c_hi = ((j + 1) * tq + tk - 1) // tk
    run_branch(ks_ref, vst_ref, 0, c_hi, slc_mask)
    finish(1)

    def win_mask(c, sidx):
        d = t_row - sidx
        return jnp.where(d >= 0, jnp.where(d < WIN_LEN, _INF, NEG), NEG)

    c_lo = jnp.maximum(j * tq - (WIN_LEN - 1), 0) // tk
    run_branch(kw_ref, vwt_ref, c_lo, c_hi, win_mask)
    finish(2)

    for hh in range(B_HPG):
        o_ref[0, :, hh * HEAD_DIM:(hh + 1) * HEAD_DIM] = out_ref[hh].T.astype(o_ref.dtype)


def _nsa_attn(bq, kc, vc_t, bk, bv_tc, gates_t, cov_t):
    B, L, _ = bq.shape
    G = B_KV_GROUPS
    tq, tk = 256, 512
    n_cmp = kc.shape[2]
    n_slc = L // SLC_LEN
    n_sel = min(SLC_TOPN, n_slc)
    gw = B_HPG * HEAD_DIM
    nc = L // tk
    return pl.pallas_call(
        functools.partial(_nsa_attn_kernel, L=L, tq=tq, tk=tk, n_sel=n_sel),
        grid=(B, G, L // tq),
        in_specs=[pl.BlockSpec((1, tq, gw), lambda b, g, j: (b, j, g)),
                  pl.BlockSpec((1, 1, n_cmp, HEAD_DIM), lambda b, g, j: (b, g, 0, 0)),
                  pl.BlockSpec((1, 1, HEAD_DIM, n_cmp), lambda b, g, j: (b, g, 0, 0)),
                  pl.BlockSpec((1, L, HEAD_DIM), lambda b, g, j: (b, 0, 2 + g)),
                  pl.BlockSpec((1, 1, nc, HEAD_DIM, tk), lambda b, g, j: (b, 2 + g, 0, 0, 0)),
                  pl.BlockSpec((1, L, HEAD_DIM), lambda b, g, j: (b, 0, 4 + g)),
                  pl.BlockSpec((1, 1, nc, HEAD_DIM, tk), lambda b, g, j: (b, 4 + g, 0, 0, 0)),
                  pl.BlockSpec((1, 1, 3 * B_HPG, tq), lambda b, g, j: (b, g, 0, j)),
                  pl.BlockSpec((n_slc, n_cmp), lambda b, g, j: (0, 0))],
        out_specs=pl.BlockSpec((1, tq, gw), lambda b, g, j: (b, j, g)),
        out_shape=jax.ShapeDtypeStruct((B, L, B_HEADS * HEAD_DIM), bf16),
        scratch_shapes=[pltpu.VMEM((B_HPG, 1, tq), f32), pltpu.VMEM((B_HPG, 1, tq), f32),
                        pltpu.VMEM((B_HPG, HEAD_DIM, tq), f32), pltpu.VMEM((B_HPG, HEAD_DIM, tq), f32),
                        pltpu.VMEM((n_slc, tq), f32)],
        compiler_params=_cparams(("parallel", "parallel", "parallel")),
    )(bq, kc, vc_t, bk, bv_tc, bk, bv_tc, gates_t, cov_t)


def _diff_attn_kernel(q_ref, k_ref, vt_ref, lam_ref, g_ref, o_ref, m_ref, l_ref, acc_ref,
                      *, tq, tk, lam_init):
    j = pl.program_id(2)
    t_row = j * tq + lax.broadcasted_iota(i32, (1, tq), 1)
    qs = [q_ref[0, :, mi * C_DIM:(mi + 1) * C_DIM] for mi in range(2)]
    for mi in range(2):
        _flash_init(m_ref.at[mi], l_ref.at[mi], acc_ref.at[mi])

    def chunk(c, masked):
        off = pl.multiple_of(c * tk, tk)
        vch = vt_ref[0, 0, c]
        cap = None
        if masked:
            sidx = off + lax.broadcasted_iota(i32, (tk, tq), 0)
            cap = jnp.where(sidx <= t_row, _INF, NEG)
        ss = [_dot_nt(k_ref[0, pl.ds(off, tk), mi * C_DIM:(mi + 1) * C_DIM], qs[mi]) for mi in range(2)]
        for mi in range(2):
            _flash_step(ss[mi], cap, vch, m_ref.at[mi], l_ref.at[mi], acc_ref.at[mi])

    def body(c, carry):
        chunk(c, False)
        return carry

    c_diag = (j * tq) // tk
    lax.fori_loop(0, c_diag, body, 0)
    chunk(c_diag, True)

    lam = lam_ref[...]
    lam_val = (jnp.exp(jnp.sum(lam[0:1] * lam[1:2], axis=1, keepdims=True))
               - jnp.exp(jnp.sum(lam[2:3] * lam[3:4], axis=1, keepdims=True)) + lam_init)
    o = (acc_ref[0] / jnp.maximum(l_ref[0], 1e-30)
         - lam_val * (acc_ref[1] / jnp.maximum(l_ref[1], 1e-30)))
    ms = jnp.mean(o * o, axis=0, keepdims=True)
    y = o * lax.rsqrt(ms + 1e-6) * g_ref[...] * (1.0 - lam_init)
    o_ref[0] = y.T.astype(o_ref.dtype)


def _diff_attn(cq, ck, cv_tc, lam, g_col, lam_init):
    B, L, _ = cq.shape
    tq, tk = 256, 512
    hw = 2 * C_DIM
    nc = L // tk
    return pl.pallas_call(
        functools.partial(_diff_attn_kernel, tq=tq, tk=tk, lam_init=lam_init),
        grid=(B, C_HEADS, L // tq),
        in_specs=[pl.BlockSpec((1, tq, hw), lambda b, h, j: (b, j, h)),
                  pl.BlockSpec((1, L, hw), lambda b, h, j: (b, 0, h)),
                  pl.BlockSpec((1, 1, nc, hw, tk), lambda b, h, j: (b, h, 0, 0, 0)),
                  pl.BlockSpec((4, C_DIM), lambda b, h, j: (0, 0)),
                  pl.BlockSpec((hw, 1), lambda b, h, j: (0, 0))],
        out_specs=pl.BlockSpec((1, tq, hw), lambda b, h, j: (b, j, h)),
        out_shape=jax.ShapeDtypeStruct((B, L, C_HEADS * hw), bf16),
        scratch_shapes=[pltpu.VMEM((2, 1, tq), f32), pltpu.VMEM((2, 1, tq), f32),
                        pltpu.VMEM((2, hw, tq), f32)],
        compiler_params=_cparams(("parallel", "parallel", "parallel")),
    )(cq, ck, cv_tc, lam, g_col)


def _merge_kernel(ya_ref, yb_ref, yc_ref, w_ref, g0_ref, g1_ref, g2_ref, o_ref):
    acc = None
    for r, (y_ref, g_ref) in enumerate(((ya_ref, g0_ref), (yb_ref, g1_ref), (yc_ref, g2_ref))):
        br = jnp.dot(y_ref[...], w_ref[r], preferred_element_type=f32)
        t = _sigmoid(g_ref[...].astype(f32)) * br
        acc = t if acc is None else acc + t
    o_ref[...] = acc.astype(o_ref.dtype)


def _merge(ya, yb, yc, w_br, h):
    m, kw = ya.shape
    tm, tn = 512, 512
    npb = D_MODEL // tn
    yspec = pl.BlockSpec((tm, kw), lambda j, i: (i, 0))

    def gspec(r):
        return pl.BlockSpec((tm, tn), lambda j, i, _r=r: (i, _r * npb + j))

    return pl.pallas_call(
        _merge_kernel,
        grid=(npb, m // tm),
        in_specs=[yspec, yspec, yspec,
                  pl.BlockSpec((N_BRANCH, kw, tn), lambda j, i: (0, 0, j)),
                  gspec(0), gspec(1), gspec(2)],
        out_specs=pl.BlockSpec((tm, tn), lambda j, i: (i, j)),
        out_shape=jax.ShapeDtypeStruct((m, D_MODEL), bf16),
        compiler_params=_cparams(("parallel", "parallel")),
    )(ya, yb, yc, w_br, h, h, h)


def _mm_res_ln_kernel(a_ref, w_ref, x_ref, gate_ref, lg_ref, lb_ref, sc_ref, sh_ref, *rest,
                      nk, alpha, emit_u):
    if emit_u:
        xo_ref, u_ref, acc_ref = rest
    else:
        xo_ref, acc_ref = rest
    k = pl.program_id(1)

    @pl.when(k == 0)
    def _():
        acc_ref[...] = jnp.zeros(acc_ref.shape, f32)

    acc_ref[...] += jnp.dot(a_ref[...], w_ref[...], preferred_element_type=f32)

    @pl.when(k == nk - 1)
    def _():
        z = alpha * x_ref[...] + gate_ref[0] * acc_ref[...]
        xn = _ln_rows(z, 1e-5) * lg_ref[...] + lb_ref[...]
        xo_ref[...] = xn
        if emit_u:
            u_ref[...] = (_ln_rows(xn, 1e-5) * (1.0 + sc_ref[0]) + sh_ref[0]).astype(u_ref.dtype)


def _mm_res_ln(a, w, x2, gate, ln_g, ln_b, sc, sh, L, alpha, emit_u):
    m, kdim = a.shape
    d = w.shape[1]
    tm, tk = 512, 512
    nk = kdim // tk
    per_b = L // tm
    bspec = pl.BlockSpec((1, 1, d), lambda i, k: (i // per_b, 0, 0))
    vspec = pl.BlockSpec((1, d), lambda i, k: (0, 0))
    rspec = pl.BlockSpec((tm, d), lambda i, k: (i, 0))
    out_shape = [jax.ShapeDtypeStruct((m, d), f32)]
    out_specs = [rspec]
    if emit_u:
        out_shape.append(jax.ShapeDtypeStruct((m, d), bf16))
        out_specs.append(rspec)
    res = pl.pallas_call(
        functools.partial(_mm_res_ln_kernel, nk=nk, alpha=alpha, emit_u=emit_u),
        grid=(m // tm, nk),
        in_specs=[pl.BlockSpec((tm, tk), lambda i, k: (i, k)),
                  pl.BlockSpec((tk, d), lambda i, k: (k, 0)),
                  rspec, bspec, vspec, vspec, bspec, bspec],
        out_specs=out_specs,
        out_shape=out_shape,
        scratch_shapes=[pltpu.VMEM((tm, d), f32)],
        compiler_params=_cparams(("parallel", "arbitrary"), 56 * 1024 * 1024),
    )(a, w, x2, gate, ln_g.reshape(1, d), ln_b.reshape(1, d), sc, sh)
    return res if emit_u else (res[0], None)


def _ffn_in_kernel(a_ref, wg_ref, wu_ref, o_ref):
    a = a_ref[...]
    g = jnp.dot(a, wg_ref[...], preferred_element_type=f32)
    u = jnp.dot(a, wu_ref[...], preferred_element_type=f32)
    o_ref[...] = (g * _sigmoid(g) * u).astype(o_ref.dtype)


def _ffn_in(a, w):
    m, k = a.shape
    tm, tn = 512, 512
    nb = D_FF // tn
    return pl.pallas_call(
        _ffn_in_kernel,
        grid=(nb, m // tm),
        in_specs=[pl.BlockSpec((tm, k), lambda j, i: (i, 0)),
                  pl.BlockSpec((k, tn), lambda j, i: (0, j)),
                  pl.BlockSpec((k, tn), lambda j, i: (0, nb + j))],
        out_specs=pl.BlockSpec((tm, tn), lambda j, i: (i, j)),
        out_shape=jax.ShapeDtypeStruct((m, D_FF), bf16),
        compiler_params=_cparams(("parallel", "parallel")),
    )(a, w, w)


def _rope_tables(L, d):
    r = d // 4
    half = r // 2
    inv = ROPE_THETA ** (-(jnp.arange(half, dtype=f32) * 2.0) / r)
    ang = jnp.arange(L).astype(f32)[:, None] * inv[None, :]
    cos, sin = jnp.cos(ang), jnp.sin(ang)
    z = jnp.zeros((L, d - r), f32)
    zh = jnp.zeros((L, half), f32)
    c = jnp.concatenate([cos, cos, jnp.ones((L, d - r), f32)], axis=1)
    s1 = jnp.concatenate([zh, sin, z], axis=1)
    s2 = jnp.concatenate([-sin, zh, z], axis=1)
    rep = 128 // d
    return tuple(jnp.tile(t, (1, rep)) for t in (c, s1, s2))


def _pack_w_in(w):
    def cols(o, n):
        return w[:, o:o + n]

    bk = [cols(_O_BKV + ((i * 2 + 0) * B_KV_GROUPS + g) * HEAD_DIM, HEAD_DIM)
          for i in range(3) for g in range(B_KV_GROUPS)]
    bv = [cols(_O_BKV + ((i * 2 + 1) * B_KV_GROUPS + g) * HEAD_DIM, HEAD_DIM)
          for i in range(3) for g in range(B_KV_GROUPS)]
    k = w.shape[0]
    parts = [cols(_O_GL, N_BRANCH * D_MODEL), cols(_O_AQ, 1024), cols(_O_IQ, 1024), cols(_O_BQ, 1024),
             cols(_O_CQ, 1024), cols(_O_CK, 1024), cols(_O_CV, 1024)] + bk + bv + [
             cols(_O_ALAT, A_LATENT), cols(_O_IK, IDX_DIM), cols(_O_IW, IDX_HEADS),
             jnp.zeros((k, 128 - IDX_DIM - IDX_HEADS), w.dtype),
             cols(_O_BG, 3 * B_HEADS), jnp.zeros((k, 128 - 3 * B_HEADS), w.dtype)]
    return jnp.concatenate(parts, axis=1).astype(bf16)


def _cover_t(L):
    n_cmp_pad = L // CMP_STRIDE
    starts = np.arange(n_cmp_pad) * CMP_STRIDE
    slc_start = np.arange(L // SLC_LEN) * SLC_LEN
    cover = ((starts[:, None] < slc_start[None, :] + SLC_LEN)
             & (starts[:, None] + CMP_LEN > slc_start[None, :])).astype(np.float32)
    n_cmp = (L - CMP_LEN) // CMP_STRIDE + 1
    cover[n_cmp:] = 0.0
    return jnp.asarray(cover.T)


def _chunked_t(v, tk):
    B, L, n, d = v.shape
    return v.reshape(B, L // tk, tk, n, d).transpose(0, 3, 1, 4, 2)


def _token_mixing(h, B, L, lw, tabs128, tabs64, cov_t, lam_init):
    M = B * L
    G = B_KV_GROUPS
    aq, iq, bq, cq, ck, bk, alat_n, ikw, gates = _prep(h, lw['a_lat_g'], tabs128, tabs64, L)

    ak = _mm_rope(alat_n, lw['a_up_k'], tabs128, L, 512, 1024)
    av = _mm(alat_n, lw['a_up_v'], 512, 1024, bf16)
    ik = ikw[:, :IDX_DIM].astype(bf16).reshape(B, L, IDX_DIM)
    iw_t = ikw[:, IDX_DIM:IDX_DIM + IDX_HEADS].reshape(B, L, IDX_HEADS).transpose(0, 2, 1)
    mask_t = _dsa_mask(ik, iq.reshape(B, L, -1), iw_t)
    av_t = av.reshape(B, L, -1).transpose(0, 2, 1)
    ya = _dsa_attn(aq.reshape(B, L, -1), ak.reshape(B, L, -1), av_t, mask_t)

    bv = h[:, _P_BV:_P_BV + 768].reshape(B, L, 3 * G, HEAD_DIM)
    bk4 = bk.reshape(B, L, 3 * G, HEAD_DIM)
    n_row = L // CMP_STRIDE
    xcmp = jnp.stack([bk4[:, :, 0:G], bv[:, :, 0:G]], axis=1)
    xcmp = xcmp.transpose(0, 1, 3, 2, 4).reshape(B, 2, G, n_row, CMP_STRIDE * HEAD_DIM)
    cmp_out = _nsa_compress(xcmp, lw['cmp_w1'], lw['cmp_w2'], lw['cmp_pe'])
    kc = cmp_out[:, 0]
    vc_t = cmp_out[:, 1].transpose(0, 1, 3, 2)
    bv_tc = _chunked_t(bv, 512)
    gates_t = gates[:, :3 * B_HEADS].reshape(B, L, G, 3 * B_HPG).transpose(0, 2, 3, 1)
    yb = _nsa_attn(bq.reshape(B, L, -1), kc, vc_t, bk.reshape(B, L, -1), bv_tc, gates_t, cov_t)

    cv = h[:, _P_CV:_P_CV + 1024].reshape(B, L, C_HEADS, 2 * C_DIM)
    cv_tc = _chunked_t(cv, 512)
    yc = _diff_attn(cq.reshape(B, L, -1), ck.reshape(B, L, -1), cv_tc, lw['lam'],
                    lw['c_subln_g'].reshape(2 * C_DIM, 1), lam_init)

    return _merge(ya.reshape(M, -1), yb.reshape(M, -1), yc.reshape(M, -1), lw['w_br'], h)


def kernel(x, c, w_ada, b_ada, w_in, a_lat_g, a_up, cmp_w1, cmp_w2, cmp_pe, lam, c_subln_g, w_br, w_o,
           w_ffn_in, w_ffn_out, ln_g, ln_b):
    B, L, D = x.shape
    depth = w_ada.shape[0]
    M = B * L
    alpha = (2 * depth) ** 0.25

    c_pad = jnp.zeros((8, D), f32).at[:B].set(c)
    mod = _ada(c_pad, w_ada, b_ada)[:, :B]
    mods = [[mod[l, :, i * D:(i + 1) * D].reshape(B, 1, D) for i in range(6)] for l in range(depth)]

    tabs128 = _rope_tables(L, HEAD_DIM)
    tabs64 = _rope_tables(L, IDX_DIM)
    cov_t = _cover_t(L)

    x2 = x.reshape(M, D)
    u = _lnmod(x2, mods[0][1], mods[0][0], L)
    for l in range(depth):
        lam_init = 0.8 - 0.6 * math.exp(-0.3 * l)
        sh_a, sc_a, g_a, sh_f, sc_f, g_f = mods[l]
        pe_flat = jnp.zeros((2, 8, CMP_LEN * HEAD_DIM), f32).at[:, 0].set(
            cmp_pe[l].reshape(2, CMP_LEN * HEAD_DIM)).astype(bf16)
        lw = dict(a_lat_g=a_lat_g[l],
                  a_up_k=a_up[l][:, :A_HEADS * HEAD_DIM].astype(bf16),
                  a_up_v=a_up[l][:, A_HEADS * HEAD_DIM:].astype(bf16),
                  cmp_w1=cmp_w1[l].astype(bf16), cmp_w2=cmp_w2[l].astype(bf16), cmp_pe=pe_flat,
                  lam=lam[l], c_subln_g=c_subln_g[l], w_br=w_br[l].astype(bf16))
        h = _mm(u, _pack_w_in(w_in[l]), 512, 768, bf16)
        merged = _token_mixing(h, B, L, lw, tabs128, tabs64, cov_t, lam_init)
        x2, u = _mm_res_ln(merged, w_o[l].astype(bf16), x2, g_a, ln_g[l, 0], ln_b[l, 0], sc_f, sh_f,
                           L, alpha, True)
        f = _ffn_in(u, w_ffn_in[l].astype(bf16))
        last = l == depth - 1
        nsc, nsh = (sc_f, sh_f) if last else (mods[l + 1][1], mods[l + 1][0])
        x2, u = _mm_res_ln(f, w_ffn_out[l].astype(bf16), x2, g_f, ln_g[l, 1], ln_b[l, 1], nsc, nsh,
                           L, alpha, not last)
    return x2.reshape(B, L, D)
```

```python
import functools
import math

import numpy as np
import jax
import jax.numpy as jnp
from jax import lax
from jax.experimental import pallas as pl
from jax.experimental.pallas import tpu as pltpu

f32 = jnp.float32
bf16 = jnp.bfloat16
i32 = jnp.int32

D_MODEL = 2048
HEAD_DIM = 128
ROPE_THETA = 500000.0
NEG = -1e30
FORCE = 1e6
A_HEADS = 8
A_LATENT = 512
IDX_HEADS = 16
IDX_DIM = 64
DSA_TOPK = 256
B_HEADS = 8
B_KV_GROUPS = 2
B_HPG = B_HEADS // B_KV_GROUPS
CMP_LEN = 32
CMP_STRIDE = 16
SLC_LEN = 64
SLC_TOPN = 16
WIN_LEN = 512
C_HEADS = 4
C_DIM = 128
BRANCH_W = A_HEADS * HEAD_DIM
N_BRANCH = 3
D_FF = int(math.ceil(8 * D_MODEL / 3 / 256)) * 256

_O_AQ = 0
_O_ALAT = _O_AQ + A_HEADS * HEAD_DIM
_O_IQ = _O_ALAT + A_LATENT
_O_IK = _O_IQ + IDX_HEADS * IDX_DIM
_O_IW = _O_IK + IDX_DIM
_O_BQ = _O_IW + IDX_HEADS
_O_BKV = _O_BQ + B_HEADS * HEAD_DIM
_O_BG = _O_BKV + 3 * 2 * B_KV_GROUPS * HEAD_DIM
_O_CQ = _O_BG + 3 * B_HEADS
_O_CK = _O_CQ + C_HEADS * 2 * C_DIM
_O_CV = _O_CK + C_HEADS * 2 * C_DIM
_O_GL = _O_CV + C_HEADS * 2 * C_DIM
_N_IN = _O_GL + N_BRANCH * D_MODEL

_P_GL = 0
_P_AQ = 6144
_P_IQ = 7168
_P_BQ = 8192
_P_CQ = 9216
_P_CK = 10240
_P_CV = 11264
_P_BK = 12288
_P_BV = 13056
_P_ALAT = 13824
_P_IKW = 14336
_P_BG = 14464
_P_TOT = 14592

_VMEM_LIMIT = 48 * 1024 * 1024
_LOG2E = 1.4426950408889634
_INF = float("inf")
_HEAD_GROUP = 2


def _cparams(sem, vmem=_VMEM_LIMIT):
    return pltpu.CompilerParams(dimension_semantics=sem, vmem_limit_bytes=vmem)


def _sigmoid(x):
    return 1.0 / (1.0 + jnp.exp(-x))


def _dot_nt(a, b):
    return lax.dot_general(a, b, (((1,), (1,)), ((), ())), preferred_element_type=f32)


def _ada_kernel(c_ref, w_ref, b_ref, o_ref):
    c = c_ref[...]
    cs = c * _sigmoid(c)
    o_ref[0] = jnp.dot(cs, w_ref[0], preferred_element_type=f32,
                       precision=lax.Precision.HIGHEST) + b_ref[0]


def _ada(c_pad, w_ada, b_ada):
    depth, d, n = w_ada.shape
    tn = 512
    return pl.pallas_call(
        _ada_kernel,
        grid=(depth, n // tn),
        in_specs=[pl.BlockSpec((8, d), lambda l, j: (0, 0)),
                  pl.BlockSpec((1, d, tn), lambda l, j: (l, 0, j)),
                  pl.BlockSpec((1, 1, tn), lambda l, j: (l, 0, j))],
        out_specs=pl.BlockSpec((1, 8, tn), lambda l, j: (l, 0, j)),
        out_shape=jax.ShapeDtypeStruct((depth, 8, n), f32),
        compiler_params=_cparams(("parallel", "parallel")),
    )(c_pad, w_ada, b_ada.reshape(depth, 1, n))


def _ln_rows(x, eps):
    mu = jnp.mean(x, axis=-1, keepdims=True)
    d = x - mu
    var = jnp.mean(d * d, axis=-1, keepdims=True)
    return d * lax.rsqrt(var + eps)


def _lnmod_kernel(x_ref, sc_ref, sh_ref, o_ref):
    y = _ln_rows(x_ref[...], 1e-5)
    o_ref[...] = (y * (1.0 + sc_ref[0]) + sh_ref[0]).astype(o_ref.dtype)


def _lnmod(x2, sc, sh, L):
    m, d = x2.shape
    tm = 512
    per_b = L // tm
    return pl.pallas_call(
        _lnmod_kernel,
        grid=(m // tm,),
        in_specs=[pl.BlockSpec((tm, d), lambda i: (i, 0)),
                  pl.BlockSpec((1, 1, d), lambda i: (i // per_b, 0, 0)),
                  pl.BlockSpec((1, 1, d), lambda i: (i // per_b, 0, 0))],
        out_specs=pl.BlockSpec((tm, d), lambda i: (i, 0)),
        out_shape=jax.ShapeDtypeStruct((m, d), bf16),
        compiler_params=_cparams(("parallel",)),
    )(x2, sc, sh)


def _mm_kernel(a_ref, w_ref, o_ref):
    o_ref[...] = jnp.dot(a_ref[...], w_ref[...], preferred_element_type=f32).astype(o_ref.dtype)


def _mm(a, w, tm, tn, out_dtype):
    m, k = a.shape
    n = w.shape[1]
    return pl.pallas_call(
        _mm_kernel,
        grid=(n // tn, m // tm),
        in_specs=[pl.BlockSpec((tm, k), lambda j, i: (i, 0)),
                  pl.BlockSpec((k, tn), lambda j, i: (0, j))],
        out_specs=pl.BlockSpec((tm, tn), lambda j, i: (i, j)),
        out_shape=jax.ShapeDtypeStruct((m, n), out_dtype),
        compiler_params=_cparams(("parallel", "parallel")),
    )(a, w)


def _rope_heads(x, c, s1, s2, half):
    return x * c + pltpu.roll(x, half, 1) * s1 + pltpu.roll(x, 128 - half, 1) * s2


def _mm_rope_kernel(a_ref, w_ref, c_ref, s1_ref, s2_ref, o_ref):
    acc = jnp.dot(a_ref[...], w_ref[...], preferred_element_type=f32)
    c, s1, s2 = c_ref[...], s1_ref[...], s2_ref[...]
    for h in range(acc.shape[1] // 128):
        sl = slice(h * 128, (h + 1) * 128)
        o_ref[:, sl] = _rope_heads(acc[:, sl], c, s1, s2, 16).astype(o_ref.dtype)


def _mm_rope(a, w, tabs, L, tm, tn):
    m, k = a.shape
    n = w.shape[1]
    per_b = L // tm
    tspec = pl.BlockSpec((tm, 128), lambda j, i: (i % per_b, 0))
    return pl.pallas_call(
        _mm_rope_kernel,
        grid=(n // tn, m // tm),
        in_specs=[pl.BlockSpec((tm, k), lambda j, i: (i, 0)),
                  pl.BlockSpec((k, tn), lambda j, i: (0, j)),
                  tspec, tspec, tspec],
        out_specs=pl.BlockSpec((tm, tn), lambda j, i: (i, j)),
        out_shape=jax.ShapeDtypeStruct((m, n), bf16),
        compiler_params=_cparams(("parallel", "parallel")),
    )(a, w, *tabs)


def _prep_kernel(aq_ref, iq_ref, bq_ref, cq_ref, ck_ref, bk_ref, alat_ref, ikw_ref, bg_ref, alg_ref,
                 c_ref, s1_ref, s2_ref, c6_ref, s16_ref, s26_ref,
                 aq_o, iq_o, bq_o, cq_o, ck_o, bk_o, alat_o, ikw_o, g_o):
    c, s1, s2 = c_ref[...], s1_ref[...], s2_ref[...]
    c6, s16, s26 = c6_ref[...], s16_ref[...], s26_ref[...]
    scale = HEAD_DIM ** -0.5 * _LOG2E

    def rope_all(src, dst, mult, tabs, half):
        for h in range(src.shape[1] // 128):
            sl = slice(h * 128, (h + 1) * 128)
            y = _rope_heads(src[:, sl].astype(f32), *tabs, half)
            if mult != 1.0:
                y = y * mult
            dst[:, sl] = y.astype(dst.dtype)

    rope_all(aq_ref, aq_o, scale, (c, s1, s2), 16)
    rope_all(bq_ref, bq_o, scale, (c, s1, s2), 16)
    rope_all(cq_ref, cq_o, C_DIM ** -0.5 * _LOG2E, (c, s1, s2), 16)
    rope_all(ck_ref, ck_o, 1.0, (c, s1, s2), 16)
    rope_all(bk_ref, bk_o, 1.0, (c, s1, s2), 16)
    rope_all(iq_ref, iq_o, 1.0, (c6, s16, s26), 8)

    a = alat_ref[...].astype(f32)
    ms = jnp.mean(a * a, axis=-1, keepdims=True)
    alat_o[...] = (a * lax.rsqrt(ms + 1e-6) * alg_ref[...]).astype(alat_o.dtype)

    x = ikw_ref[...].astype(f32)
    lane = lax.broadcasted_iota(i32, x.shape, 1)
    isk = lane < IDX_DIM
    mu = jnp.sum(jnp.where(isk, x, 0.0), axis=-1, keepdims=True) * (1.0 / IDX_DIM)
    d = jnp.where(isk, x - mu, 0.0)
    var = jnp.sum(d * d, axis=-1, keepdims=True) * (1.0 / IDX_DIM)
    y = d * lax.rsqrt(var + 1e-5)
    yr = _rope_heads(y, c6, s16, s26, 8)
    ikw_o[...] = jnp.where(isk, yr, x * (IDX_DIM ** -0.5 * IDX_HEADS ** -0.5))

    g_o[...] = _sigmoid(bg_ref[...].astype(f32))


def _prep(h, a_lat_g, tabs128, tabs64, L):
    m = h.shape[0]
    tm = 256
    per_b = L // tm

    def hs(width, off):
        return pl.BlockSpec((tm, width), lambda i, _o=off // width: (i, _o))

    tspec = pl.BlockSpec((tm, 128), lambda i: (i % per_b, 0))

    def os(width):
        return pl.BlockSpec((tm, width), lambda i: (i, 0))

    outs = [(1024, bf16)] * 5 + [(768, bf16), (512, bf16), (128, f32), (128, f32)]
    return pl.pallas_call(
        _prep_kernel,
        grid=(m // tm,),
        in_specs=[hs(1024, _P_AQ), hs(1024, _P_IQ), hs(1024, _P_BQ), hs(1024, _P_CQ), hs(1024, _P_CK),
                  hs(768, _P_BK), hs(512, _P_ALAT), hs(128, _P_IKW), hs(128, _P_BG),
                  pl.BlockSpec((1, A_LATENT), lambda i: (0, 0))] + [tspec] * 6,
        out_specs=[os(w) for w, _ in outs],
        out_shape=[jax.ShapeDtypeStruct((m, w), dt) for w, dt in outs],
        compiler_params=_cparams(("parallel",)),
    )(h, h, h, h, h, h, h, h, h, a_lat_g.reshape(1, A_LATENT), *tabs128, *tabs64)


def _f32_order_key(x):
    b = int(np.float32(x).view(np.int32))
    return b ^ ((b >> 31) & 0x7FFFFFFF)


_KEY_NEG = _f32_order_key(NEG)


def _dsa_mask_kernel(ik_ref, iq_ref, iw_ref, o_ref, key_ref, qp_ref, j_ref, *, L, tq, ksel):
    j = pl.program_id(1)
    ck = 512
    nck = L // ck
    nc = ((j + 1) * tq + ck - 1) // ck
    n_out_i = L - nc * ck
    n_out = n_out_i.astype(f32)
    t_row = j * tq + lax.broadcasted_iota(i32, (1, tq), 1)

    for hp in range(IDX_HEADS // 2):
        for e in range(2):
            h = 2 * hp + e
            qp_ref[hp, e * tq:(e + 1) * tq, :] = iq_ref[0, :, h * IDX_DIM:(h + 1) * IDX_DIM]

    def score_chunk(c, carry):
        off = pl.multiple_of(c * ck, ck)
        ikc = ik_ref[0, pl.ds(off, ck), :]
        acc = jnp.zeros((ck, tq), f32)
        for hp in range(IDX_HEADS // 2):
            s2 = _dot_nt(ikc, qp_ref[hp])
            acc = acc + jnp.maximum(s2[:, :tq], 0.0) * iw_ref[0, 2 * hp:2 * hp + 1, :]
            acc = acc + jnp.maximum(s2[:, tq:], 0.0) * iw_ref[0, 2 * hp + 1:2 * hp + 2, :]
        acc = jnp.where(acc == 0.0, 0.0, acc)
        sidx = off + lax.broadcasted_iota(i32, (ck, tq), 0)
        sc = jnp.where(sidx <= t_row, acc, NEG)
        bits = lax.bitcast_convert_type(sc, i32)
        key_ref[pl.ds(off, ck), :] = bits ^ ((bits >> 31) & 0x7FFFFFFF)
        return carry

    lax.fori_loop(0, nc, score_chunk, 0)

    def count(pred_fn):
        def body(c, cnt):
            off = pl.multiple_of(c * ck, ck)
            k = key_ref[pl.ds(off, ck), :]
            sidx = off + lax.broadcasted_iota(i32, (ck, tq), 0)
            return cnt + jnp.sum(pred_fn(k, sidx).reshape(ck // 64, 64, tq), axis=0)
        part = lax.fori_loop(0, nc, body, jnp.zeros((64, tq), f32))
        return jnp.sum(part, axis=0, keepdims=True)

    kf = float(ksel)

    def bit_body(i, carry):
        thr, cnt_thr = carry
        cand = thr + lax.shift_left(jnp.int32(1), 31 - i)
        cnt = count(lambda k, s: jnp.where(k >= cand, 1.0, 0.0)) + jnp.where(_KEY_NEG >= cand, n_out, 0.0)
        ok = cnt >= kf
        return jnp.where(ok, cand, thr), jnp.where(ok, cnt, cnt_thr)

    thr, cnt_ge = lax.fori_loop(0, 32, bit_body, (jnp.full((1, tq), -2 ** 31, i32),
                                                   jnp.full((1, tq), float(L), f32)))
    cnt_gt = count(lambda k, s: jnp.where(k > thr, 1.0, 0.0)) + jnp.where(_KEY_NEG > thr, n_out, 0.0)
    need = kf - cnt_gt

    j_ref[...] = jnp.full((1, tq), L, i32)

    @pl.when(jnp.max(cnt_ge) > kf)
    def _():
        nbits = L.bit_length() - 1

        def jbit(i, cur):
            cand = cur | lax.shift_left(jnp.int32(1), nbits - 1 - i)
            f = count(lambda k, s: jnp.where(k == thr, jnp.where(s < cand, 1.0, 0.0), 0.0))
            f = f + jnp.where(thr == _KEY_NEG, jnp.clip(cand - nc * ck, 0, n_out_i).astype(f32), 0.0)
            return jnp.where(f < need, cand, cur)

        j_ref[...] = lax.fori_loop(0, nbits, jbit, jnp.zeros((1, tq), i32))

    jlast = j_ref[...]

    def write(c, carry):
        off = pl.multiple_of(c * ck, ck)
        k = key_ref[pl.ds(off, ck), :]
        sidx = off + lax.broadcasted_iota(i32, (ck, tq), 0)
        sel = jnp.where(k > thr, _INF, jnp.where(k == thr, jnp.where(sidx <= jlast, _INF, NEG), NEG))
        o_ref[0, pl.ds(off, ck), :] = jnp.where(sidx <= t_row, sel, NEG)
        return carry

    lax.fori_loop(0, nc, write, 0)

    def write_rest(c, carry):
        off = pl.multiple_of(c * ck, ck)
        o_ref[0, pl.ds(off, ck), :] = jnp.full((ck, tq), NEG, f32)
        return carry

    lax.fori_loop(nc, nck, write_rest, 0)


def _dsa_mask(ik, iq, iw_t):
    B, L, _ = iq.shape
    tq = 128
    ksel = min(DSA_TOPK, L // 4)
    return pl.pallas_call(
        functools.partial(_dsa_mask_kernel, L=L, tq=tq, ksel=ksel),
        grid=(B, L // tq),
        in_specs=[pl.BlockSpec((1, L, IDX_DIM), lambda b, j: (b, 0, 0)),
                  pl.BlockSpec((1, tq, IDX_HEADS * IDX_DIM), lambda b, j: (b, j, 0)),
                  pl.BlockSpec((1, IDX_HEADS, tq), lambda b, j: (b, 0, j))],
        out_specs=pl.BlockSpec((1, L, tq), lambda b, j: (b, 0, j)),
        out_shape=jax.ShapeDtypeStruct((B, L, L), f32),
        scratch_shapes=[pltpu.VMEM((L, tq), i32), pltpu.VMEM((IDX_HEADS // 2, 2 * tq, IDX_DIM), bf16),
                        pltpu.VMEM((1, tq), i32)],
        compiler_params=_cparams(("parallel", "parallel")),
    )(ik, iq, iw_t)


def _col_reduce(x, op):
    r, c = x.shape
    if r > 64:
        x = op(x.reshape(r // 64, 64, c), axis=0)
    return op(x, axis=0, keepdims=True)


def _flash_group(ss, cap, v_ts, m_refs, l_refs, acc_refs):
    n = len(ss)
    if cap is not None:
        ss = [jnp.minimum(s, cap) for s in ss]
    m_prev = [r[...] for r in m_refs]
    m_new = [jnp.maximum(m_prev[i], _col_reduce(ss[i], jnp.max)) for i in range(n)]
    alpha = [jnp.exp2(m_prev[i] - m_new[i]) for i in range(n)]
    ps = [jnp.exp2(ss[i] - m_new[i]) for i in range(n)]
    l_new = [alpha[i] * l_refs[i][...] + _col_reduce(ps[i], jnp.sum) for i in range(n)]
    pv = [jnp.dot(v_ts[i], ps[i].astype(bf16), preferred_element_type=f32) for i in range(n)]
    for i in range(n):
        acc_new = acc_refs[i][...] * alpha[i] + pv[i]
        if cap is not None:
            dead = m_new[i] <= NEG
            l_new[i] = jnp.where(dead, 0.0, l_new[i])
            acc_new = jnp.where(dead, 0.0, acc_new)
        l_refs[i][...] = l_new[i]
        acc_refs[i][...] = acc_new
        m_refs[i][...] = m_new[i]


def _flash_heads(score_fns, cap, v_ts, m_ref, l_ref, acc_ref, group):
    n = len(score_fns)
    groups = [list(range(g, min(g + group, n))) for g in range(0, n, group)]
    nxt = [score_fns[h]() for h in groups[0]]
    for gi, hs in enumerate(groups):
        ss = nxt
        if gi + 1 < len(groups):
            nxt = [score_fns[h]() for h in groups[gi + 1]]
        _flash_group(ss, cap, [v_ts[h] for h in hs], [m_ref.at[h] for h in hs],
                     [l_ref.at[h] for h in hs], [acc_ref.at[h] for h in hs])


def _flash_init(m_ref, l_ref, acc_ref):
    m_ref[...] = jnp.full(m_ref.shape, NEG, f32)
    l_ref[...] = jnp.zeros(l_ref.shape, f32)
    acc_ref[...] = jnp.zeros(acc_ref.shape, f32)


def _dsa_attn_kernel(q_ref, k_ref, vt_ref, mask_ref, o_ref, m_ref, l_ref, acc_ref, *, tq, tk, nkc):
    j = pl.program_id(1)
    c = pl.program_id(2)

    @pl.when(c == 0)
    def _():
        _flash_init(m_ref, l_ref, acc_ref)

    @pl.when(c * tk < (j + 1) * tq)
    def _():
        cap = mask_ref[0]

        def scores(h):
            sl = slice(h * HEAD_DIM, (h + 1) * HEAD_DIM)
            return lambda: _dot_nt(k_ref[0, :, sl], q_ref[0, :, sl])

        v_ts = [vt_ref[0, h * HEAD_DIM:(h + 1) * HEAD_DIM, :] for h in range(A_HEADS)]
        _flash_heads([scores(h) for h in range(A_HEADS)], cap, v_ts, m_ref, l_ref, acc_ref, _HEAD_GROUP)

    @pl.when(c == nkc - 1)
    def _():
        for h in range(A_HEADS):
            o = acc_ref[h] / jnp.maximum(l_ref[h], 1e-30)
            o_ref[0, :, h * HEAD_DIM:(h + 1) * HEAD_DIM] = o.T.astype(o_ref.dtype)


def _dsa_attn(q, k, v_t, mask_t):
    B, L, W = q.shape
    tq, tk = 256, 512
    nkc = L // tk

    def last(j):
        return ((j + 1) * tq - 1) // tk

    return pl.pallas_call(
        functools.partial(_dsa_attn_kernel, tq=tq, tk=tk, nkc=nkc),
        grid=(B, L // tq, nkc),
        in_specs=[pl.BlockSpec((1, tq, W), lambda b, j, c: (b, j, 0)),
                  pl.BlockSpec((1, tk, W), lambda b, j, c: (b, jnp.minimum(c, last(j)), 0)),
                  pl.BlockSpec((1, W, tk), lambda b, j, c: (b, 0, jnp.minimum(c, last(j)))),
                  pl.BlockSpec((1, tk, tq), lambda b, j, c: (b, jnp.minimum(c, last(j)), j))],
        out_specs=pl.BlockSpec((1, tq, W), lambda b, j, c: (b, j, 0)),
        out_shape=jax.ShapeDtypeStruct((B, L, W), bf16),
        scratch_shapes=[pltpu.VMEM((A_HEADS, 1, tq), f32), pltpu.VMEM((A_HEADS, 1, tq), f32),
                        pltpu.VMEM((A_HEADS, HEAD_DIM, tq), f32)],
        compiler_params=_cparams(("parallel", "parallel", "arbitrary")),
    )(q, k, v_t, mask_t)


def _nsa_cmp_kernel(x_ref, w1_ref, w2_ref, pe_ref, o_ref):
    x = x_ref[0, 0, 0]
    w1 = w1_ref[0]
    half = CMP_STRIDE * HEAD_DIM
    a = jnp.dot(x, w1[:half], preferred_element_type=f32)
    b = jnp.dot(x, w1[half:], preferred_element_type=f32)
    pe = jnp.dot(pe_ref[0], w1, preferred_element_type=f32)[0:1]
    n = a.shape[0]
    pre = a + pltpu.roll(b, n - 1, 0) + pe
    act = pre * _sigmoid(pre)
    o_ref[0, 0, 0] = jnp.dot(act.astype(bf16), w2_ref[0], preferred_element_type=f32).astype(o_ref.dtype)


def _nsa_compress(x, w1, w2, pe):
    B, _, G, n, wd = x.shape
    return pl.pallas_call(
        _nsa_cmp_kernel,
        grid=(B, 2, G),
        in_specs=[pl.BlockSpec((1, 1, 1, n, wd), lambda b, t, g: (b, t, g, 0, 0)),
                  pl.BlockSpec((1, CMP_LEN * HEAD_DIM, HEAD_DIM), lambda b, t, g: (t, 0, 0)),
                  pl.BlockSpec((1, HEAD_DIM, HEAD_DIM), lambda b, t, g: (t, 0, 0)),
                  pl.BlockSpec((1, 8, CMP_LEN * HEAD_DIM), lambda b, t, g: (t, 0, 0))],
        out_specs=pl.BlockSpec((1, 1, 1, n, HEAD_DIM), lambda b, t, g: (b, t, g, 0, 0)),
        out_shape=jax.ShapeDtypeStruct((B, 2, G, n, HEAD_DIM), bf16),
        compiler_params=_cparams(("parallel", "parallel", "parallel")),
    )(x, w1, w2, pe)


def _nsa_attn_kernel(q_ref, kc_ref, vct_ref, ks_ref, vst_ref, kw_ref, vwt_ref, g_ref, cov_ref, o_ref,
                     m_ref, l_ref, acc_ref, out_ref, sel_ref, *, L, tq, tk, n_sel):
    j = pl.program_id(2)
    t_row = j * tq + lax.broadcasted_iota(i32, (1, tq), 1)
    n_cmp = kc_ref.shape[2]
    n_slc = L // SLC_LEN
    qs = [q_ref[0, :, hh * HEAD_DIM:(hh + 1) * HEAD_DIM] for hh in range(B_HPG)]

    def gate(hh, i):
        return g_ref[0, 0, hh * 3 + i:hh * 3 + i + 1, :]

    kc = kc_ref[0, 0]
    vct = vct_ref[0, 0]
    cend = lax.broadcasted_iota(i32, (n_cmp, tq), 0) * CMP_STRIDE + (CMP_LEN - 1)
    mc = cend <= t_row
    psum = jnp.zeros((n_cmp, tq), f32)
    for hh in range(B_HPG):
        s = jnp.where(mc, _dot_nt(kc, qs[hh]), NEG)
        mx = _col_reduce(s, jnp.max)
        p = jnp.where(mc, jnp.exp2(s - mx), 0.0)
        p = p / jnp.maximum(_col_reduce(p, jnp.sum), 1e-30)
        psum = psum + p
        out_ref[hh] = gate(hh, 0) * jnp.dot(vct, p.astype(bf16), preferred_element_type=f32)
    imp = jnp.dot(cov_ref[...], psum, preferred_element_type=f32, precision=lax.Precision.HIGHEST)
    blk = lax.broadcasted_iota(i32, (n_slc, tq), 0)
    cur = t_row >> (SLC_LEN.bit_length() - 1)
    forced = (blk == 0) | (blk == cur) | (blk == cur - 1)
    imp = jnp.where(forced, FORCE, jnp.where(blk <= cur, imp, NEG))
    rank = jnp.zeros((n_slc, tq), f32)
    for r in range(n_slc):
        row = imp[r:r + 1, :]
        rank = rank + jnp.where(row > imp, 1.0, jnp.where(row == imp, jnp.where(blk > r, 1.0, 0.0), 0.0))
    sel_ref[...] = jnp.where(rank < float(n_sel), _INF, NEG)

    def finish(i):
        for hh in range(B_HPG):
            o = acc_ref[hh] / jnp.maximum(l_ref[hh], 1e-30)
            out_ref[hh] = out_ref[hh] + gate(hh, i) * o

    def run_branch(k_ref, vt_ref, c_lo, c_hi, mask_fn):
        for hh in range(B_HPG):
            _flash_init(m_ref.at[hh], l_ref.at[hh], acc_ref.at[hh])

        def body(c, carry):
            off = pl.multiple_of(c * tk, tk)
            kch = k_ref[0, pl.ds(off, tk), :]
            vch = vt_ref[0, 0, c]
            sidx = off + lax.broadcasted_iota(i32, (tk, tq), 0)
            cap = mask_fn(c, sidx)
            fns = [functools.partial(_dot_nt, kch, qs[hh]) for hh in range(B_HPG)]
            _flash_heads(fns, cap, [vch] * B_HPG, m_ref, l_ref, acc_ref, _HEAD_GROUP)
            return carry

        lax.fori_loop(c_lo, c_hi, body, 0)

    bpc = tk // SLC_LEN

    def slc_mask(c, sidx):
        rows = [jnp.broadcast_to(sel_ref[pl.ds(c * bpc + b, 1), :], (SLC_LEN, tq)) for b in range(bpc)]
        return jnp.where(sidx <= t_row, jnp.concatenate(rows, axis=0), NEG)

    c_hi = ((j + 1) * tq + tk - 1) // tk
    run_branch(ks_ref, vst_ref, 0, c_hi, slc_mask)
    finish(1)

    def win_mask(c, sidx):
        d = t_row - sidx
        return jnp.where(d >= 0, jnp.where(d < WIN_LEN, _INF, NEG), NEG)

    c_lo = jnp.maximum(j * tq - (WIN_LEN - 1), 0) // tk
    run_branch(kw_ref, vwt_ref, c_lo, c_hi, win_mask)
    finish(2)

    for hh in range(B_HPG):
        o_ref[0, :, hh * HEAD_DIM:(hh + 1) * HEAD_DIM] = out_ref[hh].T.astype(o_ref.dtype)


def _nsa_attn(bq, kc, vc_t, bk, bv_tc, gates_t, cov_t):
    B, L, _ = bq.shape
    G = B_KV_GROUPS
    tq, tk = 256, 512
    n_cmp = kc.shape[2]
    n_slc = L // SLC_LEN
    n_sel = min(SLC_TOPN, n_slc)
    gw = B_HPG * HEAD_DIM
    nc = L // tk
    return pl.pallas_call(
        functools.partial(_nsa_attn_kernel, L=L, tq=tq, tk=tk, n_sel=n_sel),
        grid=(B, G, L // tq),
        in_specs=[pl.BlockSpec((1, tq, gw), lambda b, g, j: (b, j, g)),
                  pl.BlockSpec((1, 1, n_cmp, HEAD_DIM), lambda b, g, j: (b, g, 0, 0)),
                  pl.BlockSpec((1, 1, HEAD_DIM, n_cmp), lambda b, g, j: (b, g, 0, 0)),
                  pl.BlockSpec((1, L, HEAD_DIM), lambda b, g, j: (b, 0, 2 + g)),
                  pl.BlockSpec((1, 1, nc, HEAD_DIM, tk), lambda b, g, j: (b, 2 + g, 0, 0, 0)),
                  pl.BlockSpec((1, L, HEAD_DIM), lambda b, g, j: (b, 0, 4 + g)),
                  pl.BlockSpec((1, 1, nc, HEAD_DIM, tk), lambda b, g, j: (b, 4 + g, 0, 0, 0)),
                  pl.BlockSpec((1, 1, 3 * B_HPG, tq), lambda b, g, j: (b, g, 0, j)),
                  pl.BlockSpec((n_slc, n_cmp), lambda b, g, j: (0, 0))],
        out_specs=pl.BlockSpec((1, tq, gw), lambda b, g, j: (b, j, g)),
        out_shape=jax.ShapeDtypeStruct((B, L, B_HEADS * HEAD_DIM), bf16),
        scratch_shapes=[pltpu.VMEM((B_HPG, 1, tq), f32), pltpu.VMEM((B_HPG, 1, tq), f32),
                        pltpu.VMEM((B_HPG, HEAD_DIM, tq), f32), pltpu.VMEM((B_HPG, HEAD_DIM, tq), f32),
                        pltpu.VMEM((n_slc, tq), f32)],
        compiler_params=_cparams(("parallel", "parallel", "parallel")),
    )(bq, kc, vc_t, bk, bv_tc, bk, bv_tc, gates_t, cov_t)


def _diff_attn_kernel(q_ref, k_ref, vt_ref, lam_ref, g_ref, o_ref, m_ref, l_ref, acc_ref,
                      *, tq, tk, lam_init):
    j = pl.program_id(2)
    t_row = j * tq + lax.broadcasted_iota(i32, (1, tq), 1)
    qs = [q_ref[0, :, mi * C_DIM:(mi + 1) * C_DIM] for mi in range(2)]
    for mi in range(2):
        _flash_init(m_ref.at[mi], l_ref.at[mi], acc_ref.at[mi])

    def chunk(c, masked):
        off = pl.multiple_of(c * tk, tk)
        vch = vt_ref[0, 0, c]
        cap = None
        if masked:
            sidx = off + lax.broadcasted_iota(i32, (tk, tq), 0)
            cap = jnp.where(sidx <= t_row, _INF, NEG)
        ss = [_dot_nt(k_ref[0, pl.ds(off, tk), mi * C_DIM:(mi + 1) * C_DIM], qs[mi]) for mi in range(2)]
        _flash_group(ss, cap, [vch] * 2, [m_ref.at[mi] for mi in range(2)],
                     [l_ref.at[mi] for mi in range(2)], [acc_ref.at[mi] for mi in range(2)])

    def body(c, carry):
        chunk(c, False)
        return carry

    c_diag = (j * tq) // tk
    lax.fori_loop(0, c_diag, body, 0)
    chunk(c_diag, True)

    lam = lam_ref[...]
    lam_val = (jnp.exp(jnp.sum(lam[0:1] * lam[1:2], axis=1, keepdims=True))
               - jnp.exp(jnp.sum(lam[2:3] * lam[3:4], axis=1, keepdims=True)) + lam_init)
    o = (acc_ref[0] / jnp.maximum(l_ref[0], 1e-30)
         - lam_val * (acc_ref[1] / jnp.maximum(l_ref[1], 1e-30)))
    ms = jnp.mean(o * o, axis=0, keepdims=True)
    y = o * lax.rsqrt(ms + 1e-6) * g_ref[...] * (1.0 - lam_init)
    o_ref[0] = y.T.astype(o_ref.dtype)


def _diff_attn(cq, ck, cv_tc, lam, g_col, lam_init):
    B, L, _ = cq.shape
    tq, tk = 256, 512
    hw = 2 * C_DIM
    nc = L // tk
    return pl.pallas_call(
        functools.partial(_diff_attn_kernel, tq=tq, tk=tk, lam_init=lam_init),
        grid=(B, C_HEADS, L // tq),
        in_specs=[pl.BlockSpec((1, tq, hw), lambda b, h, j: (b, j, h)),
                  pl.BlockSpec((1, L, hw), lambda b, h, j: (b, 0, h)),
                  pl.BlockSpec((1, 1, nc, hw, tk), lambda b, h, j: (b, h, 0, 0, 0)),
                  pl.BlockSpec((4, C_DIM), lambda b, h, j: (0, 0)),
                  pl.BlockSpec((hw, 1), lambda b, h, j: (0, 0))],
        out_specs=pl.BlockSpec((1, tq, hw), lambda b, h, j: (b, j, h)),
        out_shape=jax.ShapeDtypeStruct((B, L, C_HEADS * hw), bf16),
        scratch_shapes=[pltpu.VMEM((2, 1, tq), f32), pltpu.VMEM((2, 1, tq), f32),
                        pltpu.VMEM((2, hw, tq), f32)],
        compiler_params=_cparams(("parallel", "parallel", "parallel")),
    )(cq, ck, cv_tc, lam, g_col)


def _merge_kernel(ya_ref, yb_ref, yc_ref, w_ref, g0_ref, g1_ref, g2_ref, o_ref):
    acc = None
    for r, (y_ref, g_ref) in enumerate(((ya_ref, g0_ref), (yb_ref, g1_ref), (yc_ref, g2_ref))):
        br = jnp.dot(y_ref[...], w_ref[r], preferred_element_type=f32)
        t = _sigmoid(g_ref[...].astype(f32)) * br
        acc = t if acc is None else acc + t
    o_ref[...] = acc.astype(o_ref.dtype)


def _merge(ya, yb, yc, w_br, h):
    m, kw = ya.shape
    tm, tn = 512, 512
    npb = D_MODEL // tn
    yspec = pl.BlockSpec((tm, kw), lambda j, i: (i, 0))

    def gspec(r):
        return pl.BlockSpec((tm, tn), lambda j, i, _r=r: (i, _r * npb + j))

    return pl.pallas_call(
        _merge_kernel,
        grid=(npb, m // tm),
        in_specs=[yspec, yspec, yspec,
                  pl.BlockSpec((N_BRANCH, kw, tn), lambda j, i: (0, 0, j)),
                  gspec(0), gspec(1), gspec(2)],
        out_specs=pl.BlockSpec((tm, tn), lambda j, i: (i, j)),
        out_shape=jax.ShapeDtypeStruct((m, D_MODEL), bf16),
        compiler_params=_cparams(("parallel", "parallel")),
    )(ya, yb, yc, w_br, h, h, h)


def _mm_res_ln_kernel(a_ref, w_ref, x_ref, gate_ref, lg_ref, lb_ref, sc_ref, sh_ref, *rest,
                      nk, alpha, emit_u):
    if emit_u:
        xo_ref, u_ref, acc_ref = rest
    else:
        xo_ref, acc_ref = rest
    k = pl.program_id(1)

    @pl.when(k == 0)
    def _():
        acc_ref[...] = jnp.zeros(acc_ref.shape, f32)

    acc_ref[...] += jnp.dot(a_ref[...], w_ref[...], preferred_element_type=f32)

    @pl.when(k == nk - 1)
    def _():
        z = alpha * x_ref[...] + gate_ref[0] * acc_ref[...]
        xn = _ln_rows(z, 1e-5) * lg_ref[...] + lb_ref[...]
        xo_ref[...] = xn
        if emit_u:
            u_ref[...] = (_ln_rows(xn, 1e-5) * (1.0 + sc_ref[0]) + sh_ref[0]).astype(u_ref.dtype)


def _mm_res_ln(a, w, x2, gate, ln_g, ln_b, sc, sh, L, alpha, emit_u):
    m, kdim = a.shape
    d = w.shape[1]
    tm, tk = 512, 512
    nk = kdim // tk
    per_b = L // tm
    bspec = pl.BlockSpec((1, 1, d), lambda i, k: (i // per_b, 0, 0))
    vspec = pl.BlockSpec((1, d), lambda i, k: (0, 0))
    rspec = pl.BlockSpec((tm, d), lambda i, k: (i, 0))
    out_shape = [jax.ShapeDtypeStruct((m, d), f32)]
    out_specs = [rspec]
    if emit_u:
        out_shape.append(jax.ShapeDtypeStruct((m, d), bf16))
        out_specs.append(rspec)
    res = pl.pallas_call(
        functools.partial(_mm_res_ln_kernel, nk=nk, alpha=alpha, emit_u=emit_u),
        grid=(m // tm, nk),
        in_specs=[pl.BlockSpec((tm, tk), lambda i, k: (i, k)),
                  pl.BlockSpec((tk, d), lambda i, k: (k, 0)),
                  rspec, bspec, vspec, vspec, bspec, bspec],
        out_specs=out_specs,
        out_shape=out_shape,
        scratch_shapes=[pltpu.VMEM((tm, d), f32)],
        compiler_params=_cparams(("parallel", "arbitrary"), 56 * 1024 * 1024),
    )(a, w, x2, gate, ln_g.reshape(1, d), ln_b.reshape(1, d), sc, sh)
    return res if emit_u else (res[0], None)


def _ffn_in_kernel(a_ref, wg_ref, wu_ref, o_ref):
    a = a_ref[...]
    g = jnp.dot(a, wg_ref[...], preferred_element_type=f32)
    u = jnp.dot(a, wu_ref[...], preferred_element_type=f32)
    o_ref[...] = (g * _sigmoid(g) * u).astype(o_ref.dtype)


def _ffn_in(a, w):
    m, k = a.shape
    tm, tn = 512, 512
    nb = D_FF // tn
    return pl.pallas_call(
        _ffn_in_kernel,
        grid=(nb, m // tm),
        in_specs=[pl.BlockSpec((tm, k), lambda j, i: (i, 0)),
                  pl.BlockSpec((k, tn), lambda j, i: (0, j)),
                  pl.BlockSpec((k, tn), lambda j, i: (0, nb + j))],
        out_specs=pl.BlockSpec((tm, tn), lambda j, i: (i, j)),
        out_shape=jax.ShapeDtypeStruct((m, D_FF), bf16),
        compiler_params=_cparams(("parallel", "parallel")),
    )(a, w, w)


def _rope_tables(L, d):
    r = d // 4
    half = r // 2
    inv = ROPE_THETA ** (-(jnp.arange(half, dtype=f32) * 2.0) / r)
    ang = jnp.arange(L).astype(f32)[:, None] * inv[None, :]
    cos, sin = jnp.cos(ang), jnp.sin(ang)
    z = jnp.zeros((L, d - r), f32)
    zh = jnp.zeros((L, half), f32)
    c = jnp.concatenate([cos, cos, jnp.ones((L, d - r), f32)], axis=1)
    s1 = jnp.concatenate([zh, sin, z], axis=1)
    s2 = jnp.concatenate([-sin, zh, z], axis=1)
    rep = 128 // d
    return tuple(jnp.tile(t, (1, rep)) for t in (c, s1, s2))


def _pack_w_in(w):
    def cols(o, n):
        return w[:, o:o + n]

    bk = [cols(_O_BKV + ((i * 2 + 0) * B_KV_GROUPS + g) * HEAD_DIM, HEAD_DIM)
          for i in range(3) for g in range(B_KV_GROUPS)]
    bv = [cols(_O_BKV + ((i * 2 + 1) * B_KV_GROUPS + g) * HEAD_DIM, HEAD_DIM)
          for i in range(3) for g in range(B_KV_GROUPS)]
    k = w.shape[0]
    parts = [cols(_O_GL, N_BRANCH * D_MODEL), cols(_O_AQ, 1024), cols(_O_IQ, 1024), cols(_O_BQ, 1024),
             cols(_O_CQ, 1024), cols(_O_CK, 1024), cols(_O_CV, 1024)] + bk + bv + [
             cols(_O_ALAT, A_LATENT), cols(_O_IK, IDX_DIM), cols(_O_IW, IDX_HEADS),
             jnp.zeros((k, 128 - IDX_DIM - IDX_HEADS), w.dtype),
             cols(_O_BG, 3 * B_HEADS), jnp.zeros((k, 128 - 3 * B_HEADS), w.dtype)]
    return jnp.concatenate(parts, axis=1).astype(bf16)


def _cover_t(L):
    n_cmp_pad = L // CMP_STRIDE
    starts = np.arange(n_cmp_pad) * CMP_STRIDE
    slc_start = np.arange(L // SLC_LEN) * SLC_LEN
    cover = ((starts[:, None] < slc_start[None, :] + SLC_LEN)
             & (starts[:, None] + CMP_LEN > slc_start[None, :])).astype(np.float32)
    n_cmp = (L - CMP_LEN) // CMP_STRIDE + 1
    cover[n_cmp:] = 0.0
    return jnp.asarray(cover.T)


def _chunked_t(v, tk):
    B, L, n, d = v.shape
    return v.reshape(B, L // tk, tk, n, d).transpose(0, 3, 1, 4, 2)


def _token_mixing(h, B, L, lw, tabs128, tabs64, cov_t, lam_init):
    M = B * L
    G = B_KV_GROUPS
    aq, iq, bq, cq, ck, bk, alat_n, ikw, gates = _prep(h, lw['a_lat_g'], tabs128, tabs64, L)

    ak = _mm_rope(alat_n, lw['a_up_k'], tabs128, L, 512, 1024)
    av = _mm(alat_n, lw['a_up_v'], 512, 1024, bf16)
    ik = ikw[:, :IDX_DIM].astype(bf16).reshape(B, L, IDX_DIM)
    iw_t = ikw[:, IDX_DIM:IDX_DIM + IDX_HEADS].reshape(B, L, IDX_HEADS).transpose(0, 2, 1)
    mask_t = _dsa_mask(ik, iq.reshape(B, L, -1), iw_t)
    av_t = av.reshape(B, L, -1).transpose(0, 2, 1)
    ya = _dsa_attn(aq.reshape(B, L, -1), ak.reshape(B, L, -1), av_t, mask_t)

    bv = h[:, _P_BV:_P_BV + 768].reshape(B, L, 3 * G, HEAD_DIM)
    bk4 = bk.reshape(B, L, 3 * G, HEAD_DIM)
    n_row = L // CMP_STRIDE
    xcmp = jnp.stack([bk4[:, :, 0:G], bv[:, :, 0:G]], axis=1)
    xcmp = xcmp.transpose(0, 1, 3, 2, 4).reshape(B, 2, G, n_row, CMP_STRIDE * HEAD_DIM)
    cmp_out = _nsa_compress(xcmp, lw['cmp_w1'], lw['cmp_w2'], lw['cmp_pe'])
    kc = cmp_out[:, 0]
    vc_t = cmp_out[:, 1].transpose(0, 1, 3, 2)
    bv_tc = _chunked_t(bv, 512)
    gates_t = gates[:, :3 * B_HEADS].reshape(B, L, G, 3 * B_HPG).transpose(0, 2, 3, 1)
    yb = _nsa_attn(bq.reshape(B, L, -1), kc, vc_t, bk.reshape(B, L, -1), bv_tc, gates_t, cov_t)

    cv = h[:, _P_CV:_P_CV + 1024].reshape(B, L, C_HEADS, 2 * C_DIM)
    cv_tc = _chunked_t(cv, 512)
    yc = _diff_attn(cq.reshape(B, L, -1), ck.reshape(B, L, -1), cv_tc, lw['lam'],
                    lw['c_subln_g'].reshape(2 * C_DIM, 1), lam_init)

    return _merge(ya.reshape(M, -1), yb.reshape(M, -1), yc.reshape(M, -1), lw['w_br'], h)


def kernel(x, c, w_ada, b_ada, w_in, a_lat_g, a_up, cmp_w1, cmp_w2, cmp_pe, lam, c_subln_g, w_br, w_o,
           w_ffn_in, w_ffn_out, ln_g, ln_b):
    B, L, D = x.shape
    depth = w_ada.shape[0]
    M = B * L
    alpha = (2 * depth) ** 0.25

    c_pad = jnp.zeros((8, D), f32).at[:B].set(c)
    mod = _ada(c_pad, w_ada, b_ada)[:, :B]
    mods = [[mod[l, :, i * D:(i + 1) * D].reshape(B, 1, D) for i in range(6)] for l in range(depth)]

    tabs128 = _rope_tables(L, HEAD_DIM)
    tabs64 = _rope_tables(L, IDX_DIM)
    cov_t = _cover_t(L)

    x2 = x.reshape(M, D)
    u = _lnmod(x2, mods[0][1], mods[0][0], L)
    for l in range(depth):
        lam_init = 0.8 - 0.6 * math.exp(-0.3 * l)
        sh_a, sc_a, g_a, sh_f, sc_f, g_f = mods[l]
        pe_flat = jnp.zeros((2, 8, CMP_LEN * HEAD_DIM), f32).at[:, 0].set(
            cmp_pe[l].reshape(2, CMP_LEN * HEAD_DIM)).astype(bf16)
        lw = dict(a_lat_g=a_lat_g[l],
                  a_up_k=a_up[l][:, :A_HEADS * HEAD_DIM].astype(bf16),
                  a_up_v=a_up[l][:, A_HEADS * HEAD_DIM:].astype(bf16),
                  cmp_w1=cmp_w1[l].astype(bf16), cmp_w2=cmp_w2[l].astype(bf16), cmp_pe=pe_flat,
                  lam=lam[l], c_subln_g=c_subln_g[l], w_br=w_br[l].astype(bf16))
        h = _mm(u, _pack_w_in(w_in[l]), 512, 768, bf16)
        merged = _token_mixing(h, B, L, lw, tabs128, tabs64, cov_t, lam_init)
        x2, u = _mm_res_ln(merged, w_o[l].astype(bf16), x2, g_a, ln_g[l, 0], ln_b[l, 0], sc_f, sh_f,
                           L, alpha, True)
        f = _ffn_in(u, w_ffn_in[l].astype(bf16))
        last = l == depth - 1
        nsc, nsh = (sc_f, sh_f) if last else (mods[l + 1][1], mods[l + 1][0])
        x2, u = _mm_res_ln(f, w_ffn_out[l].astype(bf16), x2, g_f, ln_g[l, 1], ln_b[l, 1], nsc, nsh,
                           L, alpha, not last)
    return x2.reshape(B, L, D)
```

```python
import functools
import math

import numpy as np
import jax
import jax.numpy as jnp
from jax import lax
from jax.experimental import pallas as pl
from jax.experimental.pallas import tpu as pltpu

f32 = jnp.float32
bf16 = jnp.bfloat16
i32 = jnp.int32

D_MODEL = 2048
HEAD_DIM = 128
ROPE_THETA = 500000.0
NEG = -1e30
FORCE = 1e6
A_HEADS = 8
A_LATENT = 512
IDX_HEADS = 16
IDX_DIM = 64
DSA_TOPK = 256
B_HEADS = 8
B_KV_GROUPS = 2
B_HPG = B_HEADS // B_KV_GROUPS
CMP_LEN = 32
CMP_STRIDE = 16
SLC_LEN = 64
SLC_TOPN = 16
WIN_LEN = 512
C_HEADS = 4
C_DIM = 128
BRANCH_W = A_HEADS * HEAD_DIM
N_BRANCH = 3
D_FF = int(math.ceil(8 * D_MODEL / 3 / 256)) * 256

_O_AQ = 0
_O_ALAT = _O_AQ + A_HEADS * HEAD_DIM
_O_IQ = _O_ALAT + A_LATENT
_O_IK = _O_IQ + IDX_HEADS * IDX_DIM
_O_IW = _O_IK + IDX_DIM
_O_BQ = _O_IW + IDX_HEADS
_O_BKV = _O_BQ + B_HEADS * HEAD_DIM
_O_BG = _O_BKV + 3 * 2 * B_KV_GROUPS * HEAD_DIM
_O_CQ = _O_BG + 3 * B_HEADS
_O_CK = _O_CQ + C_HEADS * 2 * C_DIM
_O_CV = _O_CK + C_HEADS * 2 * C_DIM
_O_GL = _O_CV + C_HEADS * 2 * C_DIM
_N_IN = _O_GL + N_BRANCH * D_MODEL

_P_GL = 0
_P_AQ = 6144
_P_IQ = 7168
_P_BQ = 8192
_P_CQ = 9216
_P_CK = 10240
_P_CV = 11264
_P_BK = 12288
_P_BV = 13056
_P_ALAT = 13824
_P_IKW = 14336
_P_BG = 14464
_P_TOT = 14592

_VMEM_LIMIT = 48 * 1024 * 1024
_LOG2E = 1.4426950408889634
_INF = float("inf")
_RES_LN_TM = 512
_RES_LN_SUB = 256
_HEAD_GROUP = 2


def _cparams(sem, vmem=_VMEM_LIMIT):
    return pltpu.CompilerParams(dimension_semantics=sem, vmem_limit_bytes=vmem)


def _sigmoid(x):
    return 1.0 / (1.0 + jnp.exp(-x))


def _dot_nt(a, b):
    return lax.dot_general(a, b, (((1,), (1,)), ((), ())), preferred_element_type=f32)


def _ada_kernel(c_ref, w_ref, b_ref, o_ref):
    c = c_ref[...]
    cs = c * _sigmoid(c)
    o_ref[0] = jnp.dot(cs, w_ref[0], preferred_element_type=f32,
                       precision=lax.Precision.HIGHEST) + b_ref[0]


def _ada(c_pad, w_ada, b_ada):
    depth, d, n = w_ada.shape
    tn = 512
    return pl.pallas_call(
        _ada_kernel,
        grid=(depth, n // tn),
        in_specs=[pl.BlockSpec((8, d), lambda l, j: (0, 0)),
                  pl.BlockSpec((1, d, tn), lambda l, j: (l, 0, j)),
                  pl.BlockSpec((1, 1, tn), lambda l, j: (l, 0, j))],
        out_specs=pl.BlockSpec((1, 8, tn), lambda l, j: (l, 0, j)),
        out_shape=jax.ShapeDtypeStruct((depth, 8, n), f32),
        compiler_params=_cparams(("parallel", "parallel")),
    )(c_pad, w_ada, b_ada.reshape(depth, 1, n))


def _ln_rows(x, eps):
    mu = jnp.mean(x, axis=-1, keepdims=True)
    d = x - mu
    var = jnp.mean(d * d, axis=-1, keepdims=True)
    return d * lax.rsqrt(var + eps)


def _lnmod_kernel(x_ref, sc_ref, sh_ref, o_ref):
    y = _ln_rows(x_ref[...], 1e-5)
    o_ref[...] = (y * (1.0 + sc_ref[0]) + sh_ref[0]).astype(o_ref.dtype)


def _lnmod(x2, sc, sh, L):
    m, d = x2.shape
    tm = 512
    per_b = L // tm
    return pl.pallas_call(
        _lnmod_kernel,
        grid=(m // tm,),
        in_specs=[pl.BlockSpec((tm, d), lambda i: (i, 0)),
                  pl.BlockSpec((1, 1, d), lambda i: (i // per_b, 0, 0)),
                  pl.BlockSpec((1, 1, d), lambda i: (i // per_b, 0, 0))],
        out_specs=pl.BlockSpec((tm, d), lambda i: (i, 0)),
        out_shape=jax.ShapeDtypeStruct((m, d), bf16),
        compiler_params=_cparams(("parallel",)),
    )(x2, sc, sh)


def _mm_kernel(a_ref, w_ref, o_ref):
    o_ref[...] = jnp.dot(a_ref[...], w_ref[...], preferred_element_type=f32).astype(o_ref.dtype)


def _mm(a, w, tm, tn, out_dtype):
    m, k = a.shape
    n = w.shape[1]
    return pl.pallas_call(
        _mm_kernel,
        grid=(n // tn, m // tm),
        in_specs=[pl.BlockSpec((tm, k), lambda j, i: (i, 0)),
                  pl.BlockSpec((k, tn), lambda j, i: (0, j))],
        out_specs=pl.BlockSpec((tm, tn), lambda j, i: (i, j)),
        out_shape=jax.ShapeDtypeStruct((m, n), out_dtype),
        compiler_params=_cparams(("parallel", "parallel")),
    )(a, w)


def _rope_heads(x, c, s1, s2, half):
    return x * c + pltpu.roll(x, half, 1) * s1 + pltpu.roll(x, 128 - half, 1) * s2


def _mm_rope_kernel(a_ref, w_ref, c_ref, s1_ref, s2_ref, o_ref):
    acc = jnp.dot(a_ref[...], w_ref[...], preferred_element_type=f32)
    c, s1, s2 = c_ref[...], s1_ref[...], s2_ref[...]
    for h in range(acc.shape[1] // 128):
        sl = slice(h * 128, (h + 1) * 128)
        o_ref[:, sl] = _rope_heads(acc[:, sl], c, s1, s2, 16).astype(o_ref.dtype)


def _mm_rope(a, w, tabs, L, tm, tn):
    m, k = a.shape
    n = w.shape[1]
    per_b = L // tm
    tspec = pl.BlockSpec((tm, 128), lambda j, i: (i % per_b, 0))
    return pl.pallas_call(
        _mm_rope_kernel,
        grid=(n // tn, m // tm),
        in_specs=[pl.BlockSpec((tm, k), lambda j, i: (i, 0)),
                  pl.BlockSpec((k, tn), lambda j, i: (0, j)),
                  tspec, tspec, tspec],
        out_specs=pl.BlockSpec((tm, tn), lambda j, i: (i, j)),
        out_shape=jax.ShapeDtypeStruct((m, n), bf16),
        compiler_params=_cparams(("parallel", "parallel")),
    )(a, w, *tabs)


def _prep_kernel(aq_ref, iq_ref, bq_ref, cq_ref, ck_ref, bk_ref, alat_ref, ikw_ref, bg_ref, alg_ref,
                 c_ref, s1_ref, s2_ref, c6_ref, s16_ref, s26_ref,
                 aq_o, iq_o, bq_o, cq_o, ck_o, bk_o, alat_o, ikw_o, g_o):
    c, s1, s2 = c_ref[...], s1_ref[...], s2_ref[...]
    c6, s16, s26 = c6_ref[...], s16_ref[...], s26_ref[...]
    scale = HEAD_DIM ** -0.5 * _LOG2E

    def rope_all(src, dst, mult, tabs, half):
        for h in range(src.shape[1] // 128):
            sl = slice(h * 128, (h + 1) * 128)
            y = _rope_heads(src[:, sl].astype(f32), *tabs, half)
            if mult != 1.0:
                y = y * mult
            dst[:, sl] = y.astype(dst.dtype)

    rope_all(aq_ref, aq_o, scale, (c, s1, s2), 16)
    rope_all(bq_ref, bq_o, scale, (c, s1, s2), 16)
    rope_all(cq_ref, cq_o, C_DIM ** -0.5 * _LOG2E, (c, s1, s2), 16)
    rope_all(ck_ref, ck_o, 1.0, (c, s1, s2), 16)
    rope_all(bk_ref, bk_o, 1.0, (c, s1, s2), 16)
    rope_all(iq_ref, iq_o, 1.0, (c6, s16, s26), 8)

    a = alat_ref[...].astype(f32)
    ms = jnp.mean(a * a, axis=-1, keepdims=True)
    alat_o[...] = (a * lax.rsqrt(ms + 1e-6) * alg_ref[...]).astype(alat_o.dtype)

    x = ikw_ref[...].astype(f32)
    lane = lax.broadcasted_iota(i32, x.shape, 1)
    isk = lane < IDX_DIM
    mu = jnp.sum(jnp.where(isk, x, 0.0), axis=-1, keepdims=True) * (1.0 / IDX_DIM)
    d = jnp.where(isk, x - mu, 0.0)
    var = jnp.sum(d * d, axis=-1, keepdims=True) * (1.0 / IDX_DIM)
    y = d * lax.rsqrt(var + 1e-5)
    yr = _rope_heads(y, c6, s16, s26, 8)
    ikw_o[...] = jnp.where(isk, yr, x * (IDX_DIM ** -0.5 * IDX_HEADS ** -0.5))

    g_o[...] = _sigmoid(bg_ref[...].astype(f32))


def _prep(h, a_lat_g, tabs128, tabs64, L):
    m = h.shape[0]
    tm = 256
    per_b = L // tm

    def hs(width, off):
        return pl.BlockSpec((tm, width), lambda i, _o=off // width: (i, _o))

    tspec = pl.BlockSpec((tm, 128), lambda i: (i % per_b, 0))

    def os(width):
        return pl.BlockSpec((tm, width), lambda i: (i, 0))

    outs = [(1024, bf16)] * 5 + [(768, bf16), (512, bf16), (128, f32), (128, f32)]
    return pl.pallas_call(
        _prep_kernel,
        grid=(m // tm,),
        in_specs=[hs(1024, _P_AQ), hs(1024, _P_IQ), hs(1024, _P_BQ), hs(1024, _P_CQ), hs(1024, _P_CK),
                  hs(768, _P_BK), hs(512, _P_ALAT), hs(128, _P_IKW), hs(128, _P_BG),
                  pl.BlockSpec((1, A_LATENT), lambda i: (0, 0))] + [tspec] * 6,
        out_specs=[os(w) for w, _ in outs],
        out_shape=[jax.ShapeDtypeStruct((m, w), dt) for w, dt in outs],
        compiler_params=_cparams(("parallel",)),
    )(h, h, h, h, h, h, h, h, h, a_lat_g.reshape(1, A_LATENT), *tabs128, *tabs64)


def _f32_order_key(x):
    b = int(np.float32(x).view(np.int32))
    return b ^ ((b >> 31) & 0x7FFFFFFF)


_KEY_NEG = _f32_order_key(NEG)


def _dsa_mask_kernel(ik_ref, iq_ref, iw_ref, o_ref, key_ref, qp_ref, j_ref, *, L, tq, ksel):
    j = pl.program_id(1)
    ck = 512
    nck = L // ck
    nc = ((j + 1) * tq + ck - 1) // ck
    n_out_i = L - nc * ck
    n_out = n_out_i.astype(f32)
    t_row = j * tq + lax.broadcasted_iota(i32, (1, tq), 1)

    for hp in range(IDX_HEADS // 2):
        for e in range(2):
            h = 2 * hp + e
            qp_ref[hp, e * tq:(e + 1) * tq, :] = iq_ref[0, :, h * IDX_DIM:(h + 1) * IDX_DIM]

    def score_chunk(c, carry):
        off = pl.multiple_of(c * ck, ck)
        ikc = ik_ref[0, pl.ds(off, ck), :]
        acc = jnp.zeros((ck, tq), f32)
        for hp in range(IDX_HEADS // 2):
            s2 = _dot_nt(ikc, qp_ref[hp])
            acc = acc + jnp.maximum(s2[:, :tq], 0.0) * iw_ref[0, 2 * hp:2 * hp + 1, :]
            acc = acc + jnp.maximum(s2[:, tq:], 0.0) * iw_ref[0, 2 * hp + 1:2 * hp + 2, :]
        acc = jnp.where(acc == 0.0, 0.0, acc)
        sidx = off + lax.broadcasted_iota(i32, (ck, tq), 0)
        sc = jnp.where(sidx <= t_row, acc, NEG)
        bits = lax.bitcast_convert_type(sc, i32)
        key_ref[pl.ds(off, ck), :] = bits ^ ((bits >> 31) & 0x7FFFFFFF)
        return carry

    lax.fori_loop(0, nc, score_chunk, 0)

    def count(pred_fn):
        def body(c, cnt):
            off = pl.multiple_of(c * ck, ck)
            k = key_ref[pl.ds(off, ck), :]
            sidx = off + lax.broadcasted_iota(i32, (ck, tq), 0)
            return cnt + jnp.sum(pred_fn(k, sidx).reshape(ck // 64, 64, tq), axis=0)
        part = lax.fori_loop(0, nc, body, jnp.zeros((64, tq), f32))
        return jnp.sum(part, axis=0, keepdims=True)

    kf = float(ksel)

    def bit_body(i, carry):
        thr, cnt_thr = carry
        cand = thr + lax.shift_left(jnp.int32(1), 31 - i)
        cnt = count(lambda k, s: jnp.where(k >= cand, 1.0, 0.0)) + jnp.where(_KEY_NEG >= cand, n_out, 0.0)
        ok = cnt >= kf
        return jnp.where(ok, cand, thr), jnp.where(ok, cnt, cnt_thr)

    thr, cnt_ge = lax.fori_loop(0, 32, bit_body, (jnp.full((1, tq), -2 ** 31, i32),
                                                   jnp.full((1, tq), float(L), f32)))
    cnt_gt = count(lambda k, s: jnp.where(k > thr, 1.0, 0.0)) + jnp.where(_KEY_NEG > thr, n_out, 0.0)
    need = kf - cnt_gt

    j_ref[...] = jnp.full((1, tq), L, i32)

    @pl.when(jnp.max(cnt_ge) > kf)
    def _():
        nbits = L.bit_length() - 1

        def jbit(i, cur):
            cand = cur | lax.shift_left(jnp.int32(1), nbits - 1 - i)
            f = count(lambda k, s: jnp.where(k == thr, jnp.where(s < cand, 1.0, 0.0), 0.0))
            f = f + jnp.where(thr == _KEY_NEG, jnp.clip(cand - nc * ck, 0, n_out_i).astype(f32), 0.0)
            return jnp.where(f < need, cand, cur)

        j_ref[...] = lax.fori_loop(0, nbits, jbit, jnp.zeros((1, tq), i32))

    jlast = j_ref[...]

    def write(c, carry):
        off = pl.multiple_of(c * ck, ck)
        k = key_ref[pl.ds(off, ck), :]
        sidx = off + lax.broadcasted_iota(i32, (ck, tq), 0)
        sel = jnp.where(k > thr, _INF, jnp.where(k == thr, jnp.where(sidx <= jlast, _INF, NEG), NEG))
        o_ref[0, pl.ds(off, ck), :] = jnp.where(sidx <= t_row, sel, NEG)
        return carry

    lax.fori_loop(0, nc, write, 0)

    def write_rest(c, carry):
        off = pl.multiple_of(c * ck, ck)
        o_ref[0, pl.ds(off, ck), :] = jnp.full((ck, tq), NEG, f32)
        return carry

    lax.fori_loop(nc, nck, write_rest, 0)


def _dsa_mask(ik, iq, iw_t):
    B, L, _ = iq.shape
    tq = 128
    ksel = min(DSA_TOPK, L // 4)
    return pl.pallas_call(
        functools.partial(_dsa_mask_kernel, L=L, tq=tq, ksel=ksel),
        grid=(B, L // tq),
        in_specs=[pl.BlockSpec((1, L, IDX_DIM), lambda b, j: (b, 0, 0)),
                  pl.BlockSpec((1, tq, IDX_HEADS * IDX_DIM), lambda b, j: (b, j, 0)),
                  pl.BlockSpec((1, IDX_HEADS, tq), lambda b, j: (b, 0, j))],
        out_specs=pl.BlockSpec((1, L, tq), lambda b, j: (b, 0, j)),
        out_shape=jax.ShapeDtypeStruct((B, L, L), f32),
        scratch_shapes=[pltpu.VMEM((L, tq), i32), pltpu.VMEM((IDX_HEADS // 2, 2 * tq, IDX_DIM), bf16),
                        pltpu.VMEM((1, tq), i32)],
        compiler_params=_cparams(("parallel", "parallel")),
    )(ik, iq, iw_t)


def _col_reduce(x, op):
    r, c = x.shape
    if r > 64:
        x = op(x.reshape(r // 64, 64, c), axis=0)
    return op(x, axis=0, keepdims=True)


def _flash_group(ss, cap, v_ts, m_refs, l_refs, acc_refs):
    n = len(ss)
    if cap is not None:
        ss = [jnp.minimum(s, cap) for s in ss]
    m_prev = [r[...] for r in m_refs]
    m_new = [jnp.maximum(m_prev[i], _col_reduce(ss[i], jnp.max)) for i in range(n)]
    alpha = [jnp.exp2(m_prev[i] - m_new[i]) for i in range(n)]
    ps = [jnp.exp2(ss[i] - m_new[i]) for i in range(n)]
    l_new = [alpha[i] * l_refs[i][...] + _col_reduce(ps[i], jnp.sum) for i in range(n)]
    pv = [jnp.dot(v_ts[i], ps[i].astype(bf16), preferred_element_type=f32) for i in range(n)]
    for i in range(n):
        acc_new = acc_refs[i][...] * alpha[i] + pv[i]
        if cap is not None:
            dead = m_new[i] <= NEG
            l_new[i] = jnp.where(dead, 0.0, l_new[i])
            acc_new = jnp.where(dead, 0.0, acc_new)
        l_refs[i][...] = l_new[i]
        acc_refs[i][...] = acc_new
        m_refs[i][...] = m_new[i]


def _flash_heads(score_fns, cap, v_ts, m_ref, l_ref, acc_ref, group):
    n = len(score_fns)
    groups = [list(range(g, min(g + group, n))) for g in range(0, n, group)]
    nxt = [score_fns[h]() for h in groups[0]]
    for gi, hs in enumerate(groups):
        ss = nxt
        if gi + 1 < len(groups):
            nxt = [score_fns[h]() for h in groups[gi + 1]]
        _flash_group(ss, cap, [v_ts[h] for h in hs], [m_ref.at[h] for h in hs],
                     [l_ref.at[h] for h in hs], [acc_ref.at[h] for h in hs])


def _flash_init(m_ref, l_ref, acc_ref):
    m_ref[...] = jnp.full(m_ref.shape, NEG, f32)
    l_ref[...] = jnp.zeros(l_ref.shape, f32)
    acc_ref[...] = jnp.zeros(acc_ref.shape, f32)


def _dsa_attn_kernel(q_ref, k_ref, vt_ref, mask_ref, o_ref, m_ref, l_ref, acc_ref, *, tq, tk, nkc):
    j = pl.program_id(1)
    c = pl.program_id(2)

    @pl.when(c == 0)
    def _():
        _flash_init(m_ref, l_ref, acc_ref)

    @pl.when(c * tk < (j + 1) * tq)
    def _():
        cap = mask_ref[0]

        def scores(h):
            sl = slice(h * HEAD_DIM, (h + 1) * HEAD_DIM)
            return lambda: _dot_nt(k_ref[0, :, sl], q_ref[0, :, sl])

        v_ts = [vt_ref[0, h * HEAD_DIM:(h + 1) * HEAD_DIM, :] for h in range(A_HEADS)]
        _flash_heads([scores(h) for h in range(A_HEADS)], cap, v_ts, m_ref, l_ref, acc_ref, _HEAD_GROUP)

    @pl.when(c == nkc - 1)
    def _():
        for h in range(A_HEADS):
            o = acc_ref[h] / jnp.maximum(l_ref[h], 1e-30)
            o_ref[0, :, h * HEAD_DIM:(h + 1) * HEAD_DIM] = o.T.astype(o_ref.dtype)


def _dsa_attn(q, k, v_t, mask_t):
    B, L, W = q.shape
    tq, tk = 256, 512
    nkc = L // tk

    def last(j):
        return ((j + 1) * tq - 1) // tk

    return pl.pallas_call(
        functools.partial(_dsa_attn_kernel, tq=tq, tk=tk, nkc=nkc),
        grid=(B, L // tq, nkc),
        in_specs=[pl.BlockSpec((1, tq, W), lambda b, j, c: (b, j, 0)),
                  pl.BlockSpec((1, tk, W), lambda b, j, c: (b, jnp.minimum(c, last(j)), 0)),
                  pl.BlockSpec((1, W, tk), lambda b, j, c: (b, 0, jnp.minimum(c, last(j)))),
                  pl.BlockSpec((1, tk, tq), lambda b, j, c: (b, jnp.minimum(c, last(j)), j))],
        out_specs=pl.BlockSpec((1, tq, W), lambda b, j, c: (b, j, 0)),
        out_shape=jax.ShapeDtypeStruct((B, L, W), bf16),
        scratch_shapes=[pltpu.VMEM((A_HEADS, 1, tq), f32), pltpu.VMEM((A_HEADS, 1, tq), f32),
                        pltpu.VMEM((A_HEADS, HEAD_DIM, tq), f32)],
        compiler_params=_cparams(("parallel", "parallel", "arbitrary")),
    )(q, k, v_t, mask_t)


def _nsa_cmp_kernel(x_ref, w1_ref, w2_ref, pe_ref, o_ref):
    x = x_ref[0, 0, 0]
    w1 = w1_ref[0]
    half = CMP_STRIDE * HEAD_DIM
    a = jnp.dot(x, w1[:half], preferred_element_type=f32)
    b = jnp.dot(x, w1[half:], preferred_element_type=f32)
    pe = jnp.dot(pe_ref[0], w1, preferred_element_type=f32)[0:1]
    n = a.shape[0]
    pre = a + pltpu.roll(b, n - 1, 0) + pe
    act = pre * _sigmoid(pre)
    o_ref[0, 0, 0] = jnp.dot(act.astype(bf16), w2_ref[0], preferred_element_type=f32).astype(o_ref.dtype)


def _nsa_compress(x, w1, w2, pe):
    B, _, G, n, wd = x.shape
    return pl.pallas_call(
        _nsa_cmp_kernel,
        grid=(B, 2, G),
        in_specs=[pl.BlockSpec((1, 1, 1, n, wd), lambda b, t, g: (b, t, g, 0, 0)),
                  pl.BlockSpec((1, CMP_LEN * HEAD_DIM, HEAD_DIM), lambda b, t, g: (t, 0, 0)),
                  pl.BlockSpec((1, HEAD_DIM, HEAD_DIM), lambda b, t, g: (t, 0, 0)),
                  pl.BlockSpec((1, 8, CMP_LEN * HEAD_DIM), lambda b, t, g: (t, 0, 0))],
        out_specs=pl.BlockSpec((1, 1, 1, n, HEAD_DIM), lambda b, t, g: (b, t, g, 0, 0)),
        out_shape=jax.ShapeDtypeStruct((B, 2, G, n, HEAD_DIM), bf16),
        compiler_params=_cparams(("parallel", "parallel", "parallel")),
    )(x, w1, w2, pe)


def _nsa_attn_kernel(q_ref, kc_ref, vct_ref, ks_ref, vst_ref, kw_ref, vwt_ref, g_ref, cov_ref, o_ref,
                     m_ref, l_ref, acc_ref, out_ref, sel_ref, *, L, tq, tk, n_sel):
    j = pl.program_id(2)
    t_row = j * tq + lax.broadcasted_iota(i32, (1, tq), 1)
    n_cmp = kc_ref.shape[2]
    n_slc = L // SLC_LEN
    qs = [q_ref[0, :, hh * HEAD_DIM:(hh + 1) * HEAD_DIM] for hh in range(B_HPG)]

    def gate(hh, i):
        return g_ref[0, 0, hh * 3 + i:hh * 3 + i + 1, :]

    kc = kc_ref[0, 0]
    vct = vct_ref[0, 0]
    cend = lax.broadcasted_iota(i32, (n_cmp, tq), 0) * CMP_STRIDE + (CMP_LEN - 1)
    mc = cend <= t_row
    psum = jnp.zeros((n_cmp, tq), f32)
    for hh in range(B_HPG):
        s = jnp.where(mc, _dot_nt(kc, qs[hh]), NEG)
        mx = _col_reduce(s, jnp.max)
        p = jnp.where(mc, jnp.exp2(s - mx), 0.0)
        p = p / jnp.maximum(_col_reduce(p, jnp.sum), 1e-30)
        psum = psum + p
        out_ref[hh] = gate(hh, 0) * jnp.dot(vct, p.astype(bf16), preferred_element_type=f32)
    imp = jnp.dot(cov_ref[...], psum, preferred_element_type=f32, precision=lax.Precision.HIGHEST)
    blk = lax.broadcasted_iota(i32, (n_slc, tq), 0)
    cur = t_row >> (SLC_LEN.bit_length() - 1)
    forced = (blk == 0) | (blk == cur) | (blk == cur - 1)
    imp = jnp.where(forced, FORCE, jnp.where(blk <= cur, imp, NEG))
    rank = jnp.zeros((n_slc, tq), f32)
    for r in range(n_slc):
        row = imp[r:r + 1, :]
        rank = rank + jnp.where(row > imp, 1.0, jnp.where(row == imp, jnp.where(blk > r, 1.0, 0.0), 0.0))
    sel_ref[...] = jnp.where(rank < float(n_sel), _INF, NEG)

    def finish(i):
        for hh in range(B_HPG):
            o = acc_ref[hh] / jnp.maximum(l_ref[hh], 1e-30)
            out_ref[hh] = out_ref[hh] + gate(hh, i) * o

    def run_branch(k_ref, vt_ref, c_lo, c_hi, mask_fn):
        for hh in range(B_HPG):
            _flash_init(m_ref.at[hh], l_ref.at[hh], acc_ref.at[hh])

        def body(c, carry):
            off = pl.multiple_of(c * tk, tk)
            kch = k_ref[0, pl.ds(off, tk), :]
            vch = vt_ref[0, 0, c]
            sidx = off + lax.broadcasted_iota(i32, (tk, tq), 0)
            cap = mask_fn(c, sidx)
            fns = [functools.partial(_dot_nt, kch, qs[hh]) for hh in range(B_HPG)]
            _flash_heads(fns, cap, [vch] * B_HPG, m_ref, l_ref, acc_ref, _HEAD_GROUP)
            return carry

        lax.fori_loop(c_lo, c_hi, body, 0)

    bpc = tk // SLC_LEN

    def slc_mask(c, sidx):
        rows = [jnp.broadcast_to(sel_ref[pl.ds(c * bpc + b, 1), :], (SLC_LEN, tq)) for b in range(bpc)]
        return jnp.where(sidx <= t_row, jnp.concatenate(rows, axis=0), NEG)

    c_hi = ((j + 1) * tq + tk - 1) // tk
    run_branch(ks_ref, vst_ref, 0, c_hi, slc_mask)
    finish(1)

    def win_mask(c, sidx):
        d = t_row - sidx
        return jnp.where(d >= 0, jnp.where(d < WIN_LEN, _INF, NEG), NEG)

    c_lo = jnp.maximum(j * tq - (WIN_LEN - 1), 0) // tk
    run_branch(kw_ref, vwt_ref, c_lo, c_hi, win_mask)
    finish(2)

    for hh in range(B_HPG):
        o_ref[0, :, hh * HEAD_DIM:(hh + 1) * HEAD_DIM] = out_ref[hh].T.astype(o_ref.dtype)


def _nsa_attn(bq, kc, vc_t, bk, bv_tc, gates_t, cov_t):
    B, L, _ = bq.shape
    G = B_KV_GROUPS
    tq, tk = 256, 512
    n_cmp = kc.shape[2]
    n_slc = L // SLC_LEN
    n_sel = min(SLC_TOPN, n_slc)
    gw = B_HPG * HEAD_DIM
    nc = L // tk
    return pl.pallas_call(
        functools.partial(_nsa_attn_kernel, L=L, tq=tq, tk=tk, n_sel=n_sel),
        grid=(B, G, L // tq),
        in_specs=[pl.BlockSpec((1, tq, gw), lambda b, g, j: (b, j, g)),
                  pl.BlockSpec((1, 1, n_cmp, HEAD_DIM), lambda b, g, j: (b, g, 0, 0)),
                  pl.BlockSpec((1, 1, HEAD_DIM, n_cmp), lambda b, g, j: (b, g, 0, 0)),
                  pl.BlockSpec((1, L, HEAD_DIM), lambda b, g, j: (b, 0, 2 + g)),
                  pl.BlockSpec((1, 1, nc, HEAD_DIM, tk), lambda b, g, j: (b, 2 + g, 0, 0, 0)),
                  pl.BlockSpec((1, L, HEAD_DIM), lambda b, g, j: (b, 0, 4 + g)),
                  pl.BlockSpec((1, 1, nc, HEAD_DIM, tk), lambda b, g, j: (b, 4 + g, 0, 0, 0)),
                  pl.BlockSpec((1, 1, 3 * B_HPG, tq), lambda b, g, j: (b, g, 0, j)),
                  pl.BlockSpec((n_slc, n_cmp), lambda b, g, j: (0, 0))],
        out_specs=pl.BlockSpec((1, tq, gw), lambda b, g, j: (b, j, g)),
        out_shape=jax.ShapeDtypeStruct((B, L, B_HEADS * HEAD_DIM), bf16),
        scratch_shapes=[pltpu.VMEM((B_HPG, 1, tq), f32), pltpu.VMEM((B_HPG, 1, tq), f32),
                        pltpu.VMEM((B_HPG, HEAD_DIM, tq), f32), pltpu.VMEM((B_HPG, HEAD_DIM, tq), f32),
                        pltpu.VMEM((n_slc, tq), f32)],
        compiler_params=_cparams(("parallel", "parallel", "parallel")),
    )(bq, kc, vc_t, bk, bv_tc, bk, bv_tc, gates_t, cov_t)


def _diff_attn_kernel(q_ref, k_ref, vt_ref, lam_ref, g_ref, o_ref, m_ref, l_ref, acc_ref,
                      *, tq, tk, lam_init):
    j = pl.program_id(2)
    t_row = j * tq + lax.broadcasted_iota(i32, (1, tq), 1)
    qs = [q_ref[0, :, mi * C_DIM:(mi + 1) * C_DIM] for mi in range(2)]
    for mi in range(2):
        _flash_init(m_ref.at[mi], l_ref.at[mi], acc_ref.at[mi])

    def chunk(c, masked):
        off = pl.multiple_of(c * tk, tk)
        vch = vt_ref[0, 0, c]
        cap = None
        if masked:
            sidx = off + lax.broadcasted_iota(i32, (tk, tq), 0)
            cap = jnp.where(sidx <= t_row, _INF, NEG)
        ss = [_dot_nt(k_ref[0, pl.ds(off, tk), mi * C_DIM:(mi + 1) * C_DIM], qs[mi]) for mi in range(2)]
        _flash_group(ss, cap, [vch] * 2, [m_ref.at[mi] for mi in range(2)],
                     [l_ref.at[mi] for mi in range(2)], [acc_ref.at[mi] for mi in range(2)])

    def body(c, carry):
        chunk(c, False)
        return carry

    c_diag = (j * tq) // tk
    lax.fori_loop(0, c_diag, body, 0)
    chunk(c_diag, True)

    lam = lam_ref[...]
    lam_val = (jnp.exp(jnp.sum(lam[0:1] * lam[1:2], axis=1, keepdims=True))
               - jnp.exp(jnp.sum(lam[2:3] * lam[3:4], axis=1, keepdims=True)) + lam_init)
    o = (acc_ref[0] / jnp.maximum(l_ref[0], 1e-30)
         - lam_val * (acc_ref[1] / jnp.maximum(l_ref[1], 1e-30)))
    ms = jnp.mean(o * o, axis=0, keepdims=True)
    y = o * lax.rsqrt(ms + 1e-6) * g_ref[...] * (1.0 - lam_init)
    o_ref[0] = y.T.astype(o_ref.dtype)


def _diff_attn(cq, ck, cv_tc, lam, g_col, lam_init):
    B, L, _ = cq.shape
    tq, tk = 256, 512
    hw = 2 * C_DIM
    nc = L // tk
    return pl.pallas_call(
        functools.partial(_diff_attn_kernel, tq=tq, tk=tk, lam_init=lam_init),
        grid=(B, C_HEADS, L // tq),
        in_specs=[pl.BlockSpec((1, tq, hw), lambda b, h, j: (b, j, h)),
                  pl.BlockSpec((1, L, hw), lambda b, h, j: (b, 0, h)),
                  pl.BlockSpec((1, 1, nc, hw, tk), lambda b, h, j: (b, h, 0, 0, 0)),
                  pl.BlockSpec((4, C_DIM), lambda b, h, j: (0, 0)),
                  pl.BlockSpec((hw, 1), lambda b, h, j: (0, 0))],
        out_specs=pl.BlockSpec((1, tq, hw), lambda b, h, j: (b, j, h)),
        out_shape=jax.ShapeDtypeStruct((B, L, C_HEADS * hw), bf16),
        scratch_shapes=[pltpu.VMEM((2, 1, tq), f32), pltpu.VMEM((2, 1, tq), f32),
                        pltpu.VMEM((2, hw, tq), f32)],
        compiler_params=_cparams(("parallel", "parallel", "parallel")),
    )(cq, ck, cv_tc, lam, g_col)


def _merge_kernel(ya_ref, yb_ref, yc_ref, w_ref, g0_ref, g1_ref, g2_ref, o_ref):
    acc = None
    for r, (y_ref, g_ref) in enumerate(((ya_ref, g0_ref), (yb_ref, g1_ref), (yc_ref, g2_ref))):
        br = jnp.dot(y_ref[...], w_ref[r], preferred_element_type=f32)
        t = _sigmoid(g_ref[...].astype(f32)) * br
        acc = t if acc is None else acc + t
    o_ref[...] = acc.astype(o_ref.dtype)


def _merge(ya, yb, yc, w_br, h):
    m, kw = ya.shape
    tm, tn = 512, 512
    npb = D_MODEL // tn
    yspec = pl.BlockSpec((tm, kw), lambda j, i: (i, 0))

    def gspec(r):
        return pl.BlockSpec((tm, tn), lambda j, i, _r=r: (i, _r * npb + j))

    return pl.pallas_call(
        _merge_kernel,
        grid=(npb, m // tm),
        in_specs=[yspec, yspec, yspec,
                  pl.BlockSpec((N_BRANCH, kw, tn), lambda j, i: (0, 0, j)),
                  gspec(0), gspec(1), gspec(2)],
        out_specs=pl.BlockSpec((tm, tn), lambda j, i: (i, j)),
        out_shape=jax.ShapeDtypeStruct((m, D_MODEL), bf16),
        compiler_params=_cparams(("parallel", "parallel")),
    )(ya, yb, yc, w_br, h, h, h)


def _mm_res_ln_kernel(a_ref, w_ref, x_ref, gate_ref, lg_ref, lb_ref, sc_ref, sh_ref, xo_ref, *u_refs,
                      alpha, sub):
    tm = a_ref.shape[0]
    for r in range(tm // sub):
        rows = slice(r * sub, (r + 1) * sub)
        y = jnp.dot(a_ref[rows, :], w_ref[...], preferred_element_type=f32)
        z = alpha * x_ref[rows, :] + gate_ref[0] * y
        xn = _ln_rows(z, 1e-5) * lg_ref[...] + lb_ref[...]
        xo_ref[rows, :] = xn
        if u_refs:
            u_refs[0][rows, :] = (_ln_rows(xn, 1e-5) * (1.0 + sc_ref[0]) + sh_ref[0]).astype(bf16)


def _mm_res_ln(a, w, x2, gate, ln_g, ln_b, sc, sh, L, alpha, emit_u):
    m, kdim = a.shape
    d = w.shape[1]
    tm = _RES_LN_TM if kdim * d * 2 <= 16 * 1024 * 1024 else _RES_LN_SUB
    per_b = L // tm
    bspec = pl.BlockSpec((1, 1, d), lambda i: (i // per_b, 0, 0))
    vspec = pl.BlockSpec((1, d), lambda i: (0, 0))
    rspec = pl.BlockSpec((tm, d), lambda i: (i, 0))
    out_shape = [jax.ShapeDtypeStruct((m, d), f32)]
    out_specs = [rspec]
    if emit_u:
        out_shape.append(jax.ShapeDtypeStruct((m, d), bf16))
        out_specs.append(rspec)
    res = pl.pallas_call(
        functools.partial(_mm_res_ln_kernel, alpha=alpha, sub=_RES_LN_SUB),
        grid=(m // tm,),
        in_specs=[pl.BlockSpec((tm, kdim), lambda i: (i, 0)),
                  pl.BlockSpec((kdim, d), lambda i: (0, 0), pipeline_mode=pl.Buffered(1)),
                  rspec, bspec, vspec, vspec, bspec, bspec],
        out_specs=out_specs,
        out_shape=out_shape,
        compiler_params=_cparams(("parallel",), 60 * 1024 * 1024),
    )(a, w, x2, gate, ln_g.reshape(1, d), ln_b.reshape(1, d), sc, sh)
    return res if emit_u else (res[0], None)


def _ffn_in_kernel(a_ref, wg_ref, wu_ref, o_ref):
    a = a_ref[...]
    g = jnp.dot(a, wg_ref[...], preferred_element_type=f32)
    u = jnp.dot(a, wu_ref[...], preferred_element_type=f32)
    o_ref[...] = (g * _sigmoid(g) * u).astype(o_ref.dtype)


def _ffn_in(a, w):
    m, k = a.shape
    tm, tn = 512, 512
    nb = D_FF // tn
    return pl.pallas_call(
        _ffn_in_kernel,
        grid=(nb, m // tm),
        in_specs=[pl.BlockSpec((tm, k), lambda j, i: (i, 0)),
                  pl.BlockSpec((k, tn), lambda j, i: (0, j)),
                  pl.BlockSpec((k, tn), lambda j, i: (0, nb + j))],
        out_specs=pl.BlockSpec((tm, tn), lambda j, i: (i, j)),
        out_shape=jax.ShapeDtypeStruct((m, D_FF), bf16),
        compiler_params=_cparams(("parallel", "parallel")),
    )(a, w, w)


def _rope_tables(L, d):
    r = d // 4
    half = r // 2
    inv = ROPE_THETA ** (-(jnp.arange(half, dtype=f32) * 2.0) / r)
    ang = jnp.arange(L).astype(f32)[:, None] * inv[None, :]
    cos, sin = jnp.cos(ang), jnp.sin(ang)
    z = jnp.zeros((L, d - r), f32)
    zh = jnp.zeros((L, half), f32)
    c = jnp.concatenate([cos, cos, jnp.ones((L, d - r), f32)], axis=1)
    s1 = jnp.concatenate([zh, sin, z], axis=1)
    s2 = jnp.concatenate([-sin, zh, z], axis=1)
    rep = 128 // d
    return tuple(jnp.tile(t, (1, rep)) for t in (c, s1, s2))


def _pack_w_in(w):
    def cols(o, n):
        return w[:, o:o + n]

    bk = [cols(_O_BKV + ((i * 2 + 0) * B_KV_GROUPS + g) * HEAD_DIM, HEAD_DIM)
          for i in range(3) for g in range(B_KV_GROUPS)]
    bv = [cols(_O_BKV + ((i * 2 + 1) * B_KV_GROUPS + g) * HEAD_DIM, HEAD_DIM)
          for i in range(3) for g in range(B_KV_GROUPS)]
    k = w.shape[0]
    parts = [cols(_O_GL, N_BRANCH * D_MODEL), cols(_O_AQ, 1024), cols(_O_IQ, 1024), cols(_O_BQ, 1024),
             cols(_O_CQ, 1024), cols(_O_CK, 1024), cols(_O_CV, 1024)] + bk + bv + [
             cols(_O_ALAT, A_LATENT), cols(_O_IK, IDX_DIM), cols(_O_IW, IDX_HEADS),
             jnp.zeros((k, 128 - IDX_DIM - IDX_HEADS), w.dtype),
             cols(_O_BG, 3 * B_HEADS), jnp.zeros((k, 128 - 3 * B_HEADS), w.dtype)]
    return jnp.concatenate(parts, axis=1).astype(bf16)


def _cover_t(L):
    n_cmp_pad = L // CMP_STRIDE
    starts = np.arange(n_cmp_pad) * CMP_STRIDE
    slc_start = np.arange(L // SLC_LEN) * SLC_LEN
    cover = ((starts[:, None] < slc_start[None, :] + SLC_LEN)
             & (starts[:, None] + CMP_LEN > slc_start[None, :])).astype(np.float32)
    n_cmp = (L - CMP_LEN) // CMP_STRIDE + 1
    cover[n_cmp:] = 0.0
    return jnp.asarray(cover.T)


def _chunked_t(v, tk):
    B, L, n, d = v.shape
    return v.reshape(B, L // tk, tk, n, d).transpose(0, 3, 1, 4, 2)


def _token_mixing(h, B, L, lw, tabs128, tabs64, cov_t, lam_init):
    M = B * L
    G = B_KV_GROUPS
    aq, iq, bq, cq, ck, bk, alat_n, ikw, gates = _prep(h, lw['a_lat_g'], tabs128, tabs64, L)

    ak = _mm_rope(alat_n, lw['a_up_k'], tabs128, L, 512, 1024)
    av = _mm(alat_n, lw['a_up_v'], 512, 1024, bf16)
    ik = ikw[:, :IDX_DIM].astype(bf16).reshape(B, L, IDX_DIM)
    iw_t = ikw[:, IDX_DIM:IDX_DIM + IDX_HEADS].reshape(B, L, IDX_HEADS).transpose(0, 2, 1)
    mask_t = _dsa_mask(ik, iq.reshape(B, L, -1), iw_t)
    av_t = av.reshape(B, L, -1).transpose(0, 2, 1)
    ya = _dsa_attn(aq.reshape(B, L, -1), ak.reshape(B, L, -1), av_t, mask_t)

    bv = h[:, _P_BV:_P_BV + 768].reshape(B, L, 3 * G, HEAD_DIM)
    bk4 = bk.reshape(B, L, 3 * G, HEAD_DIM)
    n_row = L // CMP_STRIDE
    xcmp = jnp.stack([bk4[:, :, 0:G], bv[:, :, 0:G]], axis=1)
    xcmp = xcmp.transpose(0, 1, 3, 2, 4).reshape(B, 2, G, n_row, CMP_STRIDE * HEAD_DIM)
    cmp_out = _nsa_compress(xcmp, lw['cmp_w1'], lw['cmp_w2'], lw['cmp_pe'])
    kc = cmp_out[:, 0]
    vc_t = cmp_out[:, 1].transpose(0, 1, 3, 2)
    bv_tc = _chunked_t(bv, 512)
    gates_t = gates[:, :3 * B_HEADS].reshape(B, L, G, 3 * B_HPG).transpose(0, 2, 3, 1)
    yb = _nsa_attn(bq.reshape(B, L, -1), kc, vc_t, bk.reshape(B, L, -1), bv_tc, gates_t, cov_t)

    cv = h[:, _P_CV:_P_CV + 1024].reshape(B, L, C_HEADS, 2 * C_DIM)
    cv_tc = _chunked_t(cv, 512)
    yc = _diff_attn(cq.reshape(B, L, -1), ck.reshape(B, L, -1), cv_tc, lw['lam'],
                    lw['c_subln_g'].reshape(2 * C_DIM, 1), lam_init)

    return _merge(ya.reshape(M, -1), yb.reshape(M, -1), yc.reshape(M, -1), lw['w_br'], h)


def kernel(x, c, w_ada, b_ada, w_in, a_lat_g, a_up, cmp_w1, cmp_w2, cmp_pe, lam, c_subln_g, w_br, w_o,
           w_ffn_in, w_ffn_out, ln_g, ln_b):
    B, L, D = x.shape
    depth = w_ada.shape[0]
    M = B * L
    alpha = (2 * depth) ** 0.25

    c_pad = jnp.zeros((8, D), f32).at[:B].set(c)
    mod = _ada(c_pad, w_ada, b_ada)[:, :B]
    mods = [[mod[l, :, i * D:(i + 1) * D].reshape(B, 1, D) for i in range(6)] for l in range(depth)]

    tabs128 = _rope_tables(L, HEAD_DIM)
    tabs64 = _rope_tables(L, IDX_DIM)
    cov_t = _cover_t(L)

    x2 = x.reshape(M, D)
    u = _lnmod(x2, mods[0][1], mods[0][0], L)
    for l in range(depth):
        lam_init = 0.8 - 0.6 * math.exp(-0.3 * l)
        sh_a, sc_a, g_a, sh_f, sc_f, g_f = mods[l]
        pe_flat = jnp.zeros((2, 8, CMP_LEN * HEAD_DIM), f32).at[:, 0].set(
            cmp_pe[l].reshape(2, CMP_LEN * HEAD_DIM)).astype(bf16)
        lw = dict(a_lat_g=a_lat_g[l],
                  a_up_k=a_up[l][:, :A_HEADS * HEAD_DIM].astype(bf16),
                  a_up_v=a_up[l][:, A_HEADS * HEAD_DIM:].astype(bf16),
                  cmp_w1=cmp_w1[l].astype(bf16), cmp_w2=cmp_w2[l].astype(bf16), cmp_pe=pe_flat,
                  lam=lam[l], c_subln_g=c_subln_g[l], w_br=w_br[l].astype(bf16))
        h = _mm(u, _pack_w_in(w_in[l]), 512, 768, bf16)
        merged = _token_mixing(h, B, L, lw, tabs128, tabs64, cov_t, lam_init)
        x2, u = _mm_res_ln(merged, w_o[l].astype(bf16), x2, g_a, ln_g[l, 0], ln_b[l, 0], sc_f, sh_f,
                           L, alpha, True)
        f = _ffn_in(u, w_ffn_in[l].astype(bf16))
        last = l == depth - 1
        nsc, nsh = (sc_f, sh_f) if last else (mods[l + 1][1], mods[l + 1][0])
        x2, u = _mm_res_ln(f, w_ffn_out[l].astype(bf16), x2, g_f, ln_g[l, 1], ln_b[l, 1], nsc, nsh,
                           L, alpha, not last)
    return x2.reshape(B, L, D)
```

```python
import functools
import math

import numpy as np
import jax
import jax.numpy as jnp
from jax import lax
from jax.experimental import pallas as pl
from jax.experimental.pallas import tpu as pltpu

f32 = jnp.float32
bf16 = jnp.bfloat16
i32 = jnp.int32

D_MODEL = 2048
HEAD_DIM = 128
ROPE_THETA = 500000.0
NEG = -1e30
FORCE = 1e6
A_HEADS = 8
A_LATENT = 512
IDX_HEADS = 16
IDX_DIM = 64
DSA_TOPK = 256
B_HEADS = 8
B_KV_GROUPS = 2
B_HPG = B_HEADS // B_KV_GROUPS
CMP_LEN = 32
CMP_STRIDE = 16
SLC_LEN = 64
SLC_TOPN = 16
WIN_LEN = 512
C_HEADS = 4
C_DIM = 128
BRANCH_W = A_HEADS * HEAD_DIM
N_BRANCH = 3
D_FF = int(math.ceil(8 * D_MODEL / 3 / 256)) * 256

_O_AQ = 0
_O_ALAT = _O_AQ + A_HEADS * HEAD_DIM
_O_IQ = _O_ALAT + A_LATENT
_O_IK = _O_IQ + IDX_HEADS * IDX_DIM
_O_IW = _O_IK + IDX_DIM
_O_BQ = _O_IW + IDX_HEADS
_O_BKV = _O_BQ + B_HEADS * HEAD_DIM
_O_BG = _O_BKV + 3 * 2 * B_KV_GROUPS * HEAD_DIM
_O_CQ = _O_BG + 3 * B_HEADS
_O_CK = _O_CQ + C_HEADS * 2 * C_DIM
_O_CV = _O_CK + C_HEADS * 2 * C_DIM
_O_GL = _O_CV + C_HEADS * 2 * C_DIM
_N_IN = _O_GL + N_BRANCH * D_MODEL

_P_GL = 0
_P_AQ = 6144
_P_IQ = 7168
_P_BQ = 8192
_P_CQ = 9216
_P_CK = 10240
_P_CV = 11264
_P_BK = 12288
_P_BV = 13056
_P_ALAT = 13824
_P_IKW = 14336
_P_BG = 14464
_P_TOT = 14592

_VMEM_LIMIT = 48 * 1024 * 1024
_LOG2E = 1.4426950408889634
_INF = float("inf")
_RES_LN_TM = 512
_RES_LN_SUB = 256
_DIFF_TK = 1024
_HEAD_GROUP = 2


def _cparams(sem, vmem=_VMEM_LIMIT):
    return pltpu.CompilerParams(dimension_semantics=sem, vmem_limit_bytes=vmem)


def _sigmoid(x):
    return 1.0 / (1.0 + jnp.exp(-x))


def _dot_nt(a, b):
    return lax.dot_general(a, b, (((1,), (1,)), ((), ())), preferred_element_type=f32)


def _ada_kernel(c_ref, w_ref, b_ref, o_ref):
    c = c_ref[...]
    cs = c * _sigmoid(c)
    o_ref[0] = jnp.dot(cs, w_ref[0], preferred_element_type=f32,
                       precision=lax.Precision.HIGHEST) + b_ref[0]


def _ada(c_pad, w_ada, b_ada):
    depth, d, n = w_ada.shape
    tn = 512
    return pl.pallas_call(
        _ada_kernel,
        grid=(depth, n // tn),
        in_specs=[pl.BlockSpec((8, d), lambda l, j: (0, 0)),
                  pl.BlockSpec((1, d, tn), lambda l, j: (l, 0, j)),
                  pl.BlockSpec((1, 1, tn), lambda l, j: (l, 0, j))],
        out_specs=pl.BlockSpec((1, 8, tn), lambda l, j: (l, 0, j)),
        out_shape=jax.ShapeDtypeStruct((depth, 8, n), f32),
        compiler_params=_cparams(("parallel", "parallel")),
    )(c_pad, w_ada, b_ada.reshape(depth, 1, n))


def _ln_rows(x, eps):
    mu = jnp.mean(x, axis=-1, keepdims=True)
    d = x - mu
    var = jnp.mean(d * d, axis=-1, keepdims=True)
    return d * lax.rsqrt(var + eps)


def _lnmod_kernel(x_ref, sc_ref, sh_ref, o_ref):
    y = _ln_rows(x_ref[...], 1e-5)
    o_ref[...] = (y * (1.0 + sc_ref[0]) + sh_ref[0]).astype(o_ref.dtype)


def _lnmod(x2, sc, sh, L):
    m, d = x2.shape
    tm = 512
    per_b = L // tm
    return pl.pallas_call(
        _lnmod_kernel,
        grid=(m // tm,),
        in_specs=[pl.BlockSpec((tm, d), lambda i: (i, 0)),
                  pl.BlockSpec((1, 1, d), lambda i: (i // per_b, 0, 0)),
                  pl.BlockSpec((1, 1, d), lambda i: (i // per_b, 0, 0))],
        out_specs=pl.BlockSpec((tm, d), lambda i: (i, 0)),
        out_shape=jax.ShapeDtypeStruct((m, d), bf16),
        compiler_params=_cparams(("parallel",)),
    )(x2, sc, sh)


def _mm_kernel(a_ref, w_ref, o_ref):
    o_ref[...] = jnp.dot(a_ref[...], w_ref[...], preferred_element_type=f32).astype(o_ref.dtype)


def _mm(a, w, l, tm, tn, out_dtype, col0=0, n=None):
    m, k = a.shape
    n = w.shape[2] if n is None else n
    cb = col0 // tn
    return pl.pallas_call(
        _mm_kernel,
        grid=(n // tn, m // tm),
        in_specs=[pl.BlockSpec((tm, k), lambda j, i: (i, 0)),
                  pl.BlockSpec((None, k, tn), lambda j, i: (l, 0, cb + j))],
        out_specs=pl.BlockSpec((tm, tn), lambda j, i: (i, j)),
        out_shape=jax.ShapeDtypeStruct((m, n), out_dtype),
        compiler_params=_cparams(("parallel", "parallel")),
    )(a, w)


def _rope_heads(x, c, s1, s2, half):
    return x * c + pltpu.roll(x, half, 1) * s1 + pltpu.roll(x, 128 - half, 1) * s2


def _mm_rope_kernel(a_ref, w_ref, c_ref, s1_ref, s2_ref, o_ref):
    acc = jnp.dot(a_ref[...], w_ref[...], preferred_element_type=f32)
    c, s1, s2 = c_ref[...], s1_ref[...], s2_ref[...]
    for h in range(acc.shape[1] // 128):
        sl = slice(h * 128, (h + 1) * 128)
        o_ref[:, sl] = _rope_heads(acc[:, sl], c, s1, s2, 16).astype(o_ref.dtype)


def _mm_rope(a, w, l, n, tabs, L, tm, tn):
    m, k = a.shape
    per_b = L // tm
    tspec = pl.BlockSpec((tm, 128), lambda j, i: (i % per_b, 0))
    return pl.pallas_call(
        _mm_rope_kernel,
        grid=(n // tn, m // tm),
        in_specs=[pl.BlockSpec((tm, k), lambda j, i: (i, 0)),
                  pl.BlockSpec((None, k, tn), lambda j, i: (l, 0, j)),
                  tspec, tspec, tspec],
        out_specs=pl.BlockSpec((tm, tn), lambda j, i: (i, j)),
        out_shape=jax.ShapeDtypeStruct((m, n), bf16),
        compiler_params=_cparams(("parallel", "parallel")),
    )(a, w, *tabs)


def _prep_kernel(aq_ref, iq_ref, bq_ref, cq_ref, ck_ref, bk_ref, alat_ref, ikw_ref, bg_ref, alg_ref,
                 c_ref, s1_ref, s2_ref, c6_ref, s16_ref, s26_ref,
                 aq_o, iq_o, bq_o, cq_o, ck_o, bk_o, alat_o, ikw_o, g_o):
    c, s1, s2 = c_ref[...], s1_ref[...], s2_ref[...]
    c6, s16, s26 = c6_ref[...], s16_ref[...], s26_ref[...]
    scale = HEAD_DIM ** -0.5 * _LOG2E

    def rope_all(src, dst, mult, tabs, half):
        for h in range(src.shape[1] // 128):
            sl = slice(h * 128, (h + 1) * 128)
            y = _rope_heads(src[:, sl].astype(f32), *tabs, half)
            if mult != 1.0:
                y = y * mult
            dst[:, sl] = y.astype(dst.dtype)

    rope_all(aq_ref, aq_o, scale, (c, s1, s2), 16)
    rope_all(bq_ref, bq_o, scale, (c, s1, s2), 16)
    rope_all(cq_ref, cq_o, C_DIM ** -0.5 * _LOG2E, (c, s1, s2), 16)
    rope_all(ck_ref, ck_o, 1.0, (c, s1, s2), 16)
    rope_all(bk_ref, bk_o, 1.0, (c, s1, s2), 16)
    rope_all(iq_ref, iq_o, 1.0, (c6, s16, s26), 8)

    a = alat_ref[...].astype(f32)
    ms = jnp.mean(a * a, axis=-1, keepdims=True)
    alat_o[...] = (a * lax.rsqrt(ms + 1e-6) * alg_ref[...]).astype(alat_o.dtype)

    x = ikw_ref[...].astype(f32)
    lane = lax.broadcasted_iota(i32, x.shape, 1)
    isk = lane < IDX_DIM
    mu = jnp.sum(jnp.where(isk, x, 0.0), axis=-1, keepdims=True) * (1.0 / IDX_DIM)
    d = jnp.where(isk, x - mu, 0.0)
    var = jnp.sum(d * d, axis=-1, keepdims=True) * (1.0 / IDX_DIM)
    y = d * lax.rsqrt(var + 1e-5)
    yr = _rope_heads(y, c6, s16, s26, 8)
    ikw_o[...] = jnp.where(isk, yr, x * (IDX_DIM ** -0.5 * IDX_HEADS ** -0.5))

    g_o[...] = _sigmoid(bg_ref[...].astype(f32))


def _prep(h, a_lat_g, tabs128, tabs64, L):
    m = h.shape[0]
    tm = 256
    per_b = L // tm

    def hs(width, off):
        return pl.BlockSpec((tm, width), lambda i, _o=off // width: (i, _o))

    tspec = pl.BlockSpec((tm, 128), lambda i: (i % per_b, 0))

    def os(width):
        return pl.BlockSpec((tm, width), lambda i: (i, 0))

    outs = [(1024, bf16)] * 5 + [(768, bf16), (512, bf16), (128, f32), (128, f32)]
    return pl.pallas_call(
        _prep_kernel,
        grid=(m // tm,),
        in_specs=[hs(1024, _P_AQ), hs(1024, _P_IQ), hs(1024, _P_BQ), hs(1024, _P_CQ), hs(1024, _P_CK),
                  hs(768, _P_BK), hs(512, _P_ALAT), hs(128, _P_IKW), hs(128, _P_BG),
                  pl.BlockSpec((1, A_LATENT), lambda i: (0, 0))] + [tspec] * 6,
        out_specs=[os(w) for w, _ in outs],
        out_shape=[jax.ShapeDtypeStruct((m, w), dt) for w, dt in outs],
        compiler_params=_cparams(("parallel",)),
    )(h, h, h, h, h, h, h, h, h, a_lat_g.reshape(1, A_LATENT), *tabs128, *tabs64)


def _f32_order_key(x):
    b = int(np.float32(x).view(np.int32))
    return b ^ ((b >> 31) & 0x7FFFFFFF)


_KEY_NEG = _f32_order_key(NEG)


def _dsa_mask_kernel(ik_ref, iq_ref, iw_ref, o_ref, key_ref, qp_ref, j_ref, *, L, tq, ksel):
    j = pl.program_id(1)
    ck = 512
    nck = L // ck
    nc = ((j + 1) * tq + ck - 1) // ck
    n_out_i = L - nc * ck
    n_out = n_out_i.astype(f32)
    t_row = j * tq + lax.broadcasted_iota(i32, (1, tq), 1)

    for hp in range(IDX_HEADS // 2):
        for e in range(2):
            h = 2 * hp + e
            qp_ref[hp, e * tq:(e + 1) * tq, :] = iq_ref[0, :, h * IDX_DIM:(h + 1) * IDX_DIM]

    def score_chunk(c, carry):
        off = pl.multiple_of(c * ck, ck)
        ikc = ik_ref[0, pl.ds(off, ck), :]
        acc = jnp.zeros((ck, tq), f32)
        for hp in range(IDX_HEADS // 2):
            s2 = _dot_nt(ikc, qp_ref[hp])
            acc = acc + jnp.maximum(s2[:, :tq], 0.0) * iw_ref[0, 2 * hp:2 * hp + 1, :]
            acc = acc + jnp.maximum(s2[:, tq:], 0.0) * iw_ref[0, 2 * hp + 1:2 * hp + 2, :]
        acc = jnp.where(acc == 0.0, 0.0, acc)
        sidx = off + lax.broadcasted_iota(i32, (ck, tq), 0)
        sc = jnp.where(sidx <= t_row, acc, NEG)
        bits = lax.bitcast_convert_type(sc, i32)
        key_ref[pl.ds(off, ck), :] = bits ^ ((bits >> 31) & 0x7FFFFFFF)
        return carry

    lax.fori_loop(0, nc, score_chunk, 0)

    def count(pred_fn):
        def body(c, cnt):
            off = pl.multiple_of(c * ck, ck)
            k = key_ref[pl.ds(off, ck), :]
            sidx = off + lax.broadcasted_iota(i32, (ck, tq), 0)
            return cnt + jnp.sum(pred_fn(k, sidx).reshape(ck // 64, 64, tq), axis=0)
        part = lax.fori_loop(0, nc, body, jnp.zeros((64, tq), f32))
        return jnp.sum(part, axis=0, keepdims=True)

    kf = float(ksel)

    def bit_body(i, carry):
        thr, cnt_thr = carry
        cand = thr + lax.shift_left(jnp.int32(1), 31 - i)
        cnt = count(lambda k, s: jnp.where(k >= cand, 1.0, 0.0)) + jnp.where(_KEY_NEG >= cand, n_out, 0.0)
        ok = cnt >= kf
        return jnp.where(ok, cand, thr), jnp.where(ok, cnt, cnt_thr)

    thr, cnt_ge = lax.fori_loop(0, 32, bit_body, (jnp.full((1, tq), -2 ** 31, i32),
                                                   jnp.full((1, tq), float(L), f32)))
    cnt_gt = count(lambda k, s: jnp.where(k > thr, 1.0, 0.0)) + jnp.where(_KEY_NEG > thr, n_out, 0.0)
    need = kf - cnt_gt

    j_ref[...] = jnp.full((1, tq), L, i32)

    @pl.when(jnp.max(cnt_ge) > kf)
    def _():
        nbits = L.bit_length() - 1

        def jbit(i, cur):
            cand = cur | lax.shift_left(jnp.int32(1), nbits - 1 - i)
            f = count(lambda k, s: jnp.where(k == thr, jnp.where(s < cand, 1.0, 0.0), 0.0))
            f = f + jnp.where(thr == _KEY_NEG, jnp.clip(cand - nc * ck, 0, n_out_i).astype(f32), 0.0)
            return jnp.where(f < need, cand, cur)

        j_ref[...] = lax.fori_loop(0, nbits, jbit, jnp.zeros((1, tq), i32))

    jlast = j_ref[...]

    def write(c, carry):
        off = pl.multiple_of(c * ck, ck)
        k = key_ref[pl.ds(off, ck), :]
        sidx = off + lax.broadcasted_iota(i32, (ck, tq), 0)
        sel = jnp.where(k > thr, _INF, jnp.where(k == thr, jnp.where(sidx <= jlast, _INF, NEG), NEG))
        o_ref[0, pl.ds(off, ck), :] = jnp.where(sidx <= t_row, sel, NEG)
        return carry

    lax.fori_loop(0, nc, write, 0)

    def write_rest(c, carry):
        off = pl.multiple_of(c * ck, ck)
        o_ref[0, pl.ds(off, ck), :] = jnp.full((ck, tq), NEG, f32)
        return carry

    lax.fori_loop(nc, nck, write_rest, 0)


def _dsa_mask(ik, iq, iw_t):
    B, L, _ = iq.shape
    tq = 128
    ksel = min(DSA_TOPK, L // 4)
    return pl.pallas_call(
        functools.partial(_dsa_mask_kernel, L=L, tq=tq, ksel=ksel),
        grid=(B, L // tq),
        in_specs=[pl.BlockSpec((1, L, IDX_DIM), lambda b, j: (b, 0, 0)),
                  pl.BlockSpec((1, tq, IDX_HEADS * IDX_DIM), lambda b, j: (b, j, 0)),
                  pl.BlockSpec((1, IDX_HEADS, tq), lambda b, j: (b, 0, j))],
        out_specs=pl.BlockSpec((1, L, tq), lambda b, j: (b, 0, j)),
        out_shape=jax.ShapeDtypeStruct((B, L, L), f32),
        scratch_shapes=[pltpu.VMEM((L, tq), i32), pltpu.VMEM((IDX_HEADS // 2, 2 * tq, IDX_DIM), bf16),
                        pltpu.VMEM((1, tq), i32)],
        compiler_params=_cparams(("parallel", "parallel")),
    )(ik, iq, iw_t)


def _col_reduce(x, op):
    r, c = x.shape
    if r > 64:
        x = op(x.reshape(r // 64, 64, c), axis=0)
    return op(x, axis=0, keepdims=True)


def _flash_group(ss, cap, v_ts, m_refs, l_refs, acc_refs):
    n = len(ss)
    if cap is not None:
        ss = [jnp.minimum(s, cap) for s in ss]
    m_prev = [r[...] for r in m_refs]
    m_new = [jnp.maximum(m_prev[i], _col_reduce(ss[i], jnp.max)) for i in range(n)]
    alpha = [jnp.exp2(m_prev[i] - m_new[i]) for i in range(n)]
    ps = [jnp.exp2(ss[i] - m_new[i]) for i in range(n)]
    l_new = [alpha[i] * l_refs[i][...] + _col_reduce(ps[i], jnp.sum) for i in range(n)]
    pv = [jnp.dot(v_ts[i], ps[i].astype(bf16), preferred_element_type=f32) for i in range(n)]
    for i in range(n):
        acc_new = acc_refs[i][...] * alpha[i] + pv[i]
        if cap is not None:
            dead = m_new[i] <= NEG
            l_new[i] = jnp.where(dead, 0.0, l_new[i])
            acc_new = jnp.where(dead, 0.0, acc_new)
        l_refs[i][...] = l_new[i]
        acc_refs[i][...] = acc_new
        m_refs[i][...] = m_new[i]


def _flash_heads(score_fns, cap, v_ts, m_ref, l_ref, acc_ref, group):
    n = len(score_fns)
    groups = [list(range(g, min(g + group, n))) for g in range(0, n, group)]
    nxt = [score_fns[h]() for h in groups[0]]
    for gi, hs in enumerate(groups):
        ss = nxt
        if gi + 1 < len(groups):
            nxt = [score_fns[h]() for h in groups[gi + 1]]
        _flash_group(ss, cap, [v_ts[h] for h in hs], [m_ref.at[h] for h in hs],
                     [l_ref.at[h] for h in hs], [acc_ref.at[h] for h in hs])


def _flash_init(m_ref, l_ref, acc_ref):
    m_ref[...] = jnp.full(m_ref.shape, NEG, f32)
    l_ref[...] = jnp.zeros(l_ref.shape, f32)
    acc_ref[...] = jnp.zeros(acc_ref.shape, f32)


def _dsa_attn_kernel(q_ref, k_ref, vt_ref, mask_ref, o_ref, m_ref, l_ref, acc_ref, *, tq, tk, nkc):
    j = pl.program_id(1)
    c = pl.program_id(2)

    @pl.when(c == 0)
    def _():
        _flash_init(m_ref, l_ref, acc_ref)

    @pl.when(c * tk < (j + 1) * tq)
    def _():
        cap = mask_ref[0]

        def scores(h):
            sl = slice(h * HEAD_DIM, (h + 1) * HEAD_DIM)
            return lambda: _dot_nt(k_ref[0, :, sl], q_ref[0, :, sl])

        v_ts = [vt_ref[0, h * HEAD_DIM:(h + 1) * HEAD_DIM, :] for h in range(A_HEADS)]
        _flash_heads([scores(h) for h in range(A_HEADS)], cap, v_ts, m_ref, l_ref, acc_ref, _HEAD_GROUP)

    @pl.when(c == nkc - 1)
    def _():
        for h in range(A_HEADS):
            o = acc_ref[h] / jnp.maximum(l_ref[h], 1e-30)
            o_ref[0, :, h * HEAD_DIM:(h + 1) * HEAD_DIM] = o.T.astype(o_ref.dtype)


def _dsa_attn(q, k, v_t, mask_t):
    B, L, W = q.shape
    tq, tk = 256, 512
    nkc = L // tk

    def last(j):
        return ((j + 1) * tq - 1) // tk

    return pl.pallas_call(
        functools.partial(_dsa_attn_kernel, tq=tq, tk=tk, nkc=nkc),
        grid=(B, L // tq, nkc),
        in_specs=[pl.BlockSpec((1, tq, W), lambda b, j, c: (b, j, 0)),
                  pl.BlockSpec((1, tk, W), lambda b, j, c: (b, jnp.minimum(c, last(j)), 0)),
                  pl.BlockSpec((1, W, tk), lambda b, j, c: (b, 0, jnp.minimum(c, last(j)))),
                  pl.BlockSpec((1, tk, tq), lambda b, j, c: (b, jnp.minimum(c, last(j)), j))],
        out_specs=pl.BlockSpec((1, tq, W), lambda b, j, c: (b, j, 0)),
        out_shape=jax.ShapeDtypeStruct((B, L, W), bf16),
        scratch_shapes=[pltpu.VMEM((A_HEADS, 1, tq), f32), pltpu.VMEM((A_HEADS, 1, tq), f32),
                        pltpu.VMEM((A_HEADS, HEAD_DIM, tq), f32)],
        compiler_params=_cparams(("parallel", "parallel", "arbitrary")),
    )(q, k, v_t, mask_t)


def _nsa_cmp_kernel(x_ref, w1_ref, w2_ref, pe_ref, o_ref):
    x = x_ref[0, 0, 0]
    w1 = w1_ref[0]
    half = CMP_STRIDE * HEAD_DIM
    a = jnp.dot(x, w1[:half], preferred_element_type=f32)
    b = jnp.dot(x, w1[half:], preferred_element_type=f32)
    pe = jnp.dot(pe_ref[0], w1, preferred_element_type=f32)[0:1]
    n = a.shape[0]
    pre = a + pltpu.roll(b, n - 1, 0) + pe
    act = pre * _sigmoid(pre)
    o_ref[0, 0, 0] = jnp.dot(act.astype(bf16), w2_ref[0], preferred_element_type=f32).astype(o_ref.dtype)


def _nsa_compress(x, w1, w2, pe, l):
    B, _, G, n, wd = x.shape
    return pl.pallas_call(
        _nsa_cmp_kernel,
        grid=(B, 2, G),
        in_specs=[pl.BlockSpec((1, 1, 1, n, wd), lambda b, t, g: (b, t, g, 0, 0)),
                  pl.BlockSpec((None, 1, CMP_LEN * HEAD_DIM, HEAD_DIM), lambda b, t, g: (l, t, 0, 0)),
                  pl.BlockSpec((None, 1, HEAD_DIM, HEAD_DIM), lambda b, t, g: (l, t, 0, 0)),
                  pl.BlockSpec((None, 1, 8, CMP_LEN * HEAD_DIM), lambda b, t, g: (l, t, 0, 0))],
        out_specs=pl.BlockSpec((1, 1, 1, n, HEAD_DIM), lambda b, t, g: (b, t, g, 0, 0)),
        out_shape=jax.ShapeDtypeStruct((B, 2, G, n, HEAD_DIM), bf16),
        compiler_params=_cparams(("parallel", "parallel", "parallel")),
    )(x, w1, w2, pe)


def _nsa_attn_kernel(q_ref, kc_ref, vct_ref, ks_ref, vst_ref, kw_ref, vwt_ref, g_ref, cov_ref, o_ref,
                     m_ref, l_ref, acc_ref, out_ref, sel_ref, *, L, tq, tk, n_sel):
    j = pl.program_id(2)
    t_row = j * tq + lax.broadcasted_iota(i32, (1, tq), 1)
    n_cmp = kc_ref.shape[2]
    n_slc = L // SLC_LEN
    qs = [q_ref[0, :, hh * HEAD_DIM:(hh + 1) * HEAD_DIM] for hh in range(B_HPG)]

    def gate(hh, i):
        return g_ref[0, 0, hh * 3 + i:hh * 3 + i + 1, :]

    kc = kc_ref[0, 0]
    vct = vct_ref[0, 0]
    cend = lax.broadcasted_iota(i32, (n_cmp, tq), 0) * CMP_STRIDE + (CMP_LEN - 1)
    mc = cend <= t_row
    psum = jnp.zeros((n_cmp, tq), f32)
    for hh in range(B_HPG):
        s = jnp.where(mc, _dot_nt(kc, qs[hh]), NEG)
        mx = _col_reduce(s, jnp.max)
        p = jnp.where(mc, jnp.exp2(s - mx), 0.0)
        p = p / jnp.maximum(_col_reduce(p, jnp.sum), 1e-30)
        psum = psum + p
        out_ref[hh] = gate(hh, 0) * jnp.dot(vct, p.astype(bf16), preferred_element_type=f32)
    imp = jnp.dot(cov_ref[...], psum, preferred_element_type=f32, precision=lax.Precision.HIGHEST)
    blk = lax.broadcasted_iota(i32, (n_slc, tq), 0)
    cur = t_row >> (SLC_LEN.bit_length() - 1)
    forced = (blk == 0) | (blk == cur) | (blk == cur - 1)
    imp = jnp.where(forced, FORCE, jnp.where(blk <= cur, imp, NEG))
    rank = jnp.zeros((n_slc, tq), f32)
    for r in range(n_slc):
        row = imp[r:r + 1, :]
        rank = rank + jnp.where(row > imp, 1.0, jnp.where(row == imp, jnp.where(blk > r, 1.0, 0.0), 0.0))
    sel_ref[...] = jnp.where(rank < float(n_sel), _INF, NEG)

    def finish(i):
        for hh in range(B_HPG):
            o = acc_ref[hh] / jnp.maximum(l_ref[hh], 1e-30)
            out_ref[hh] = out_ref[hh] + gate(hh, i) * o

    def run_branch(k_ref, vt_ref, c_lo, c_hi, mask_fn):
        for hh in range(B_HPG):
            _flash_init(m_ref.at[hh], l_ref.at[hh], acc_ref.at[hh])

        def body(c, carry):
            off = pl.multiple_of(c * tk, tk)
            kch = k_ref[0, pl.ds(off, tk), :]
            vch = vt_ref[0, 0, c]
            sidx = off + lax.broadcasted_iota(i32, (tk, tq), 0)
            cap = mask_fn(c, sidx)
            fns = [functools.partial(_dot_nt, kch, qs[hh]) for hh in range(B_HPG)]
            _flash_heads(fns, cap, [vch] * B_HPG, m_ref, l_ref, acc_ref, _HEAD_GROUP)
            return carry

        lax.fori_loop(c_lo, c_hi, body, 0)

    bpc = tk // SLC_LEN

    def slc_mask(c, sidx):
        rows = [jnp.broadcast_to(sel_ref[pl.ds(c * bpc + b, 1), :], (SLC_LEN, tq)) for b in range(bpc)]
        return jnp.where(sidx <= t_row, jnp.concatenate(rows, axis=0), NEG)

    c_hi = ((j + 1) * tq + tk - 1) // tk
    run_branch(ks_ref, vst_ref, 0, c_hi, slc_mask)
    finish(1)

    def win_mask(c, sidx):
        d = t_row - sidx
        return jnp.where(d >= 0, jnp.where(d < WIN_LEN, _INF, NEG), NEG)

    c_lo = jnp.maximum(j * tq - (WIN_LEN - 1), 0) // tk
    run_branch(kw_ref, vwt_ref, c_lo, c_hi, win_mask)
    finish(2)

    for hh in range(B_HPG):
        o_ref[0, :, hh * HEAD_DIM:(hh + 1) * HEAD_DIM] = out_ref[hh].T.astype(o_ref.dtype)


def _nsa_attn(bq, kc, vc_t, bk, bv_tc, gates_t, cov_t):
    B, L, _ = bq.shape
    G = B_KV_GROUPS
    tq, tk = 256, 512
    n_cmp = kc.shape[2]
    n_slc = L // SLC_LEN
    n_sel = min(SLC_TOPN, n_slc)
    gw = B_HPG * HEAD_DIM
    nc = L // tk
    return pl.pallas_call(
        functools.partial(_nsa_attn_kernel, L=L, tq=tq, tk=tk, n_sel=n_sel),
        grid=(B, G, L // tq),
        in_specs=[pl.BlockSpec((1, tq, gw), lambda b, g, j: (b, j, g)),
                  pl.BlockSpec((1, 1, n_cmp, HEAD_DIM), lambda b, g, j: (b, g, 0, 0)),
                  pl.BlockSpec((1, 1, HEAD_DIM, n_cmp), lambda b, g, j: (b, g, 0, 0)),
                  pl.BlockSpec((1, L, HEAD_DIM), lambda b, g, j: (b, 0, 2 + g)),
                  pl.BlockSpec((1, 1, nc, HEAD_DIM, tk), lambda b, g, j: (b, 2 + g, 0, 0, 0)),
                  pl.BlockSpec((1, L, HEAD_DIM), lambda b, g, j: (b, 0, 4 + g)),
                  pl.BlockSpec((1, 1, nc, HEAD_DIM, tk), lambda b, g, j: (b, 4 + g, 0, 0, 0)),
                  pl.BlockSpec((1, 1, 3 * B_HPG, tq), lambda b, g, j: (b, g, 0, j)),
                  pl.BlockSpec((n_slc, n_cmp), lambda b, g, j: (0, 0))],
        out_specs=pl.BlockSpec((1, tq, gw), lambda b, g, j: (b, j, g)),
        out_shape=jax.ShapeDtypeStruct((B, L, B_HEADS * HEAD_DIM), bf16),
        scratch_shapes=[pltpu.VMEM((B_HPG, 1, tq), f32), pltpu.VMEM((B_HPG, 1, tq), f32),
                        pltpu.VMEM((B_HPG, HEAD_DIM, tq), f32), pltpu.VMEM((B_HPG, HEAD_DIM, tq), f32),
                        pltpu.VMEM((n_slc, tq), f32)],
        compiler_params=_cparams(("parallel", "parallel", "parallel")),
    )(bq, kc, vc_t, bk, bv_tc, bk, bv_tc, gates_t, cov_t)


def _diff_attn_kernel(q_ref, k_ref, vt_ref, lam_ref, g_ref, o_ref, m_ref, l_ref, acc_ref,
                      *, tq, tk, lam_init):
    j = pl.program_id(2)
    t_row = j * tq + lax.broadcasted_iota(i32, (1, tq), 1)
    qs = [q_ref[0, :, mi * C_DIM:(mi + 1) * C_DIM] for mi in range(2)]
    for mi in range(2):
        _flash_init(m_ref.at[mi], l_ref.at[mi], acc_ref.at[mi])

    def chunk(c, masked):
        off = pl.multiple_of(c * tk, tk)
        vch = vt_ref[0, 0, c]
        cap = None
        if masked:
            sidx = off + lax.broadcasted_iota(i32, (tk, tq), 0)
            cap = jnp.where(sidx <= t_row, _INF, NEG)
        ss = [_dot_nt(k_ref[0, pl.ds(off, tk), mi * C_DIM:(mi + 1) * C_DIM], qs[mi]) for mi in range(2)]
        _flash_group(ss, cap, [vch] * 2, [m_ref.at[mi] for mi in range(2)],
                     [l_ref.at[mi] for mi in range(2)], [acc_ref.at[mi] for mi in range(2)])

    def body(c, carry):
        chunk(c, False)
        return carry

    c_diag = (j * tq) // tk
    lax.fori_loop(0, c_diag, body, 0)
    chunk(c_diag, True)

    lam = lam_ref[...]
    lam_val = (jnp.exp(jnp.sum(lam[0:1] * lam[1:2], axis=1, keepdims=True))
               - jnp.exp(jnp.sum(lam[2:3] * lam[3:4], axis=1, keepdims=True)) + lam_init)
    o = (acc_ref[0] / jnp.maximum(l_ref[0], 1e-30)
         - lam_val * (acc_ref[1] / jnp.maximum(l_ref[1], 1e-30)))
    ms = jnp.mean(o * o, axis=0, keepdims=True)
    y = o * lax.rsqrt(ms + 1e-6) * g_ref[...] * (1.0 - lam_init)
    o_ref[0] = y.T.astype(o_ref.dtype)


def _diff_attn(cq, ck, cv_tc, lam, g_col, lam_init):
    B, L, _ = cq.shape
    tq, tk = 256, cv_tc.shape[-1]
    hw = 2 * C_DIM
    nc = L // tk
    return pl.pallas_call(
        functools.partial(_diff_attn_kernel, tq=tq, tk=tk, lam_init=lam_init),
        grid=(B, C_HEADS, L // tq),
        in_specs=[pl.BlockSpec((1, tq, hw), lambda b, h, j: (b, j, h)),
                  pl.BlockSpec((1, L, hw), lambda b, h, j: (b, 0, h)),
                  pl.BlockSpec((1, 1, nc, hw, tk), lambda b, h, j: (b, h, 0, 0, 0)),
                  pl.BlockSpec((4, C_DIM), lambda b, h, j: (0, 0)),
                  pl.BlockSpec((hw, 1), lambda b, h, j: (0, 0))],
        out_specs=pl.BlockSpec((1, tq, hw), lambda b, h, j: (b, j, h)),
        out_shape=jax.ShapeDtypeStruct((B, L, C_HEADS * hw), bf16),
        scratch_shapes=[pltpu.VMEM((2, 1, tq), f32), pltpu.VMEM((2, 1, tq), f32),
                        pltpu.VMEM((2, hw, tq), f32)],
        compiler_params=_cparams(("parallel", "parallel", "parallel")),
    )(cq, ck, cv_tc, lam, g_col)


def _merge_kernel(ya_ref, yb_ref, yc_ref, w_ref, g0_ref, g1_ref, g2_ref, o_ref):
    acc = None
    for r, (y_ref, g_ref) in enumerate(((ya_ref, g0_ref), (yb_ref, g1_ref), (yc_ref, g2_ref))):
        br = jnp.dot(y_ref[...], w_ref[r], preferred_element_type=f32)
        t = _sigmoid(g_ref[...].astype(f32)) * br
        acc = t if acc is None else acc + t
    o_ref[...] = acc.astype(o_ref.dtype)


def _merge(ya, yb, yc, w_br, l, h):
    m, kw = ya.shape
    tm, tn = 512, 512
    npb = D_MODEL // tn
    yspec = pl.BlockSpec((tm, kw), lambda j, i: (i, 0))

    def gspec(r):
        return pl.BlockSpec((tm, tn), lambda j, i, _r=r: (i, _r * npb + j))

    return pl.pallas_call(
        _merge_kernel,
        grid=(npb, m // tm),
        in_specs=[yspec, yspec, yspec,
                  pl.BlockSpec((None, N_BRANCH, kw, tn), lambda j, i: (l, 0, 0, j)),
                  gspec(0), gspec(1), gspec(2)],
        out_specs=pl.BlockSpec((tm, tn), lambda j, i: (i, j)),
        out_shape=jax.ShapeDtypeStruct((m, D_MODEL), bf16),
        compiler_params=_cparams(("parallel", "parallel")),
    )(ya, yb, yc, w_br, h, h, h)


def _mm_res_ln_kernel(a_ref, w_ref, x_ref, gate_ref, lg_ref, lb_ref, sc_ref, sh_ref, xo_ref, *u_refs,
                      alpha, sub):
    tm = a_ref.shape[0]
    for r in range(tm // sub):
        rows = slice(r * sub, (r + 1) * sub)
        y = jnp.dot(a_ref[rows, :], w_ref[...], preferred_element_type=f32)
        z = alpha * x_ref[rows, :] + gate_ref[0] * y
        xn = _ln_rows(z, 1e-5) * lg_ref[...] + lb_ref[...]
        xo_ref[rows, :] = xn
        if u_refs:
            u_refs[0][rows, :] = (_ln_rows(xn, 1e-5) * (1.0 + sc_ref[0]) + sh_ref[0]).astype(bf16)


def _mm_res_ln(a, w, l, x2, gate, ln_g, ln_b, sc, sh, L, alpha, emit_u):
    m, kdim = a.shape
    d = w.shape[2]
    tm = _RES_LN_TM if kdim * d * 2 <= 16 * 1024 * 1024 else _RES_LN_SUB
    per_b = L // tm
    bspec = pl.BlockSpec((1, 1, d), lambda i: (i // per_b, 0, 0))
    vspec = pl.BlockSpec((1, d), lambda i: (0, 0))
    rspec = pl.BlockSpec((tm, d), lambda i: (i, 0))
    out_shape = [jax.ShapeDtypeStruct((m, d), f32)]
    out_specs = [rspec]
    if emit_u:
        out_shape.append(jax.ShapeDtypeStruct((m, d), bf16))
        out_specs.append(rspec)
    res = pl.pallas_call(
        functools.partial(_mm_res_ln_kernel, alpha=alpha, sub=_RES_LN_SUB),
        grid=(m // tm,),
        in_specs=[pl.BlockSpec((tm, kdim), lambda i: (i, 0)),
                  pl.BlockSpec((None, kdim, d), lambda i: (l, 0, 0), pipeline_mode=pl.Buffered(1)),
                  rspec, bspec, vspec, vspec, bspec, bspec],
        out_specs=out_specs,
        out_shape=out_shape,
        compiler_params=_cparams(("parallel",), 60 * 1024 * 1024),
    )(a, w, x2, gate, ln_g.reshape(1, d), ln_b.reshape(1, d), sc, sh)
    return res if emit_u else (res[0], None)


def _ffn_in_kernel(a_ref, wg_ref, wu_ref, o_ref):
    a = a_ref[...]
    g = jnp.dot(a, wg_ref[...], preferred_element_type=f32)
    u = jnp.dot(a, wu_ref[...], preferred_element_type=f32)
    o_ref[...] = (g * _sigmoid(g) * u).astype(o_ref.dtype)


def _ffn_in(a, w, l):
    m, k = a.shape
    tm, tn = 512, 512
    nb = D_FF // tn
    return pl.pallas_call(
        _ffn_in_kernel,
        grid=(nb, m // tm),
        in_specs=[pl.BlockSpec((tm, k), lambda j, i: (i, 0)),
                  pl.BlockSpec((None, k, tn), lambda j, i: (l, 0, j)),
                  pl.BlockSpec((None, k, tn), lambda j, i: (l, 0, nb + j))],
        out_specs=pl.BlockSpec((tm, tn), lambda j, i: (i, j)),
        out_shape=jax.ShapeDtypeStruct((m, D_FF), bf16),
        compiler_params=_cparams(("parallel", "parallel")),
    )(a, w, w)


def _rope_tables(L, d):
    r = d // 4
    half = r // 2
    inv = ROPE_THETA ** (-(jnp.arange(half, dtype=f32) * 2.0) / r)
    ang = jnp.arange(L).astype(f32)[:, None] * inv[None, :]
    cos, sin = jnp.cos(ang), jnp.sin(ang)
    z = jnp.zeros((L, d - r), f32)
    zh = jnp.zeros((L, half), f32)
    c = jnp.concatenate([cos, cos, jnp.ones((L, d - r), f32)], axis=1)
    s1 = jnp.concatenate([zh, sin, z], axis=1)
    s2 = jnp.concatenate([-sin, zh, z], axis=1)
    rep = 128 // d
    return tuple(jnp.tile(t, (1, rep)) for t in (c, s1, s2))


def _pack_w_in_kernel(w_ref, o_ref):
    def cp(src, width, dst):
        for o in range(0, width, 1024):
            wd = min(1024, width - o)
            o_ref[:, dst + o:dst + o + wd] = w_ref[:, src + o:src + o + wd].astype(bf16)

    cp(_O_GL, N_BRANCH * D_MODEL, _P_GL)
    for src, dst in ((_O_AQ, _P_AQ), (_O_IQ, _P_IQ), (_O_BQ, _P_BQ), (_O_CQ, _P_CQ), (_O_CK, _P_CK),
                     (_O_CV, _P_CV)):
        cp(src, 1024, dst)
    for i in range(3):
        for g in range(B_KV_GROUPS):
            for kv, dst in ((0, _P_BK), (1, _P_BV)):
                cp(_O_BKV + ((i * 2 + kv) * B_KV_GROUPS + g) * HEAD_DIM, HEAD_DIM,
                   dst + (i * B_KV_GROUPS + g) * HEAD_DIM)
    cp(_O_ALAT, A_LATENT, _P_ALAT)
    lane = lax.broadcasted_iota(i32, (w_ref.shape[0], 128), 1)
    assert _O_IK % 128 == 0 and _O_IW == _O_IK + IDX_DIM
    blk = w_ref[:, _O_IK:_O_IK + 128]
    o_ref[:, _P_IKW:_P_IKW + 128] = jnp.where(lane < IDX_DIM + IDX_HEADS, blk, 0.0).astype(bf16)
    base = (_O_BG // 128) * 128
    assert _O_BG + 3 * B_HEADS <= base + 128
    blk = pltpu.roll(w_ref[:, base:base + 128], 128 - (_O_BG - base), 1)
    o_ref[:, _P_BG:_P_BG + 128] = jnp.where(lane < 3 * B_HEADS, blk, 0.0).astype(bf16)


def _pack_w_in(w_in):
    depth, k, n = w_in.shape
    tr = 128
    return pl.pallas_call(
        _pack_w_in_kernel,
        grid=(depth, k // tr),
        in_specs=[pl.BlockSpec((None, tr, n), lambda l, i: (l, i, 0))],
        out_specs=pl.BlockSpec((None, tr, _P_TOT), lambda l, i: (l, i, 0)),
        out_shape=jax.ShapeDtypeStruct((depth, k, _P_TOT), bf16),
        compiler_params=_cparams(("parallel", "parallel")),
    )(w_in)


def _cover_t(L):
    n_cmp_pad = L // CMP_STRIDE
    starts = np.arange(n_cmp_pad) * CMP_STRIDE
    slc_start = np.arange(L // SLC_LEN) * SLC_LEN
    cover = ((starts[:, None] < slc_start[None, :] + SLC_LEN)
             & (starts[:, None] + CMP_LEN > slc_start[None, :])).astype(np.float32)
    n_cmp = (L - CMP_LEN) // CMP_STRIDE + 1
    cover[n_cmp:] = 0.0
    return jnp.asarray(cover.T)


def _chunked_t(v, tk):
    B, L, n, d = v.shape
    return v.reshape(B, L // tk, tk, n, d).transpose(0, 3, 1, 4, 2)


def _token_mixing(h, B, L, l, lw, tabs128, tabs64, cov_t, lam_init):
    M = B * L
    G = B_KV_GROUPS
    aq, iq, bq, cq, ck, bk, alat_n, ikw, gates = _prep(h, lw['a_lat_g'], tabs128, tabs64, L)

    aw = A_HEADS * HEAD_DIM
    ak = _mm_rope(alat_n, lw['a_up'], l, aw, tabs128, L, 512, aw)
    av = _mm(alat_n, lw['a_up'], l, 512, aw, bf16, col0=aw, n=aw)
    ik = ikw[:, :IDX_DIM].astype(bf16).reshape(B, L, IDX_DIM)
    iw_t = ikw[:, IDX_DIM:IDX_DIM + IDX_HEADS].reshape(B, L, IDX_HEADS).transpose(0, 2, 1)
    mask_t = _dsa_mask(ik, iq.reshape(B, L, -1), iw_t)
    av_t = av.reshape(B, L, -1).transpose(0, 2, 1)
    ya = _dsa_attn(aq.reshape(B, L, -1), ak.reshape(B, L, -1), av_t, mask_t)

    bv = h[:, _P_BV:_P_BV + 768].reshape(B, L, 3 * G, HEAD_DIM)
    bk4 = bk.reshape(B, L, 3 * G, HEAD_DIM)
    n_row = L // CMP_STRIDE
    xcmp = jnp.stack([bk4[:, :, 0:G], bv[:, :, 0:G]], axis=1)
    xcmp = xcmp.transpose(0, 1, 3, 2, 4).reshape(B, 2, G, n_row, CMP_STRIDE * HEAD_DIM)
    cmp_out = _nsa_compress(xcmp, lw['cmp_w1'], lw['cmp_w2'], lw['cmp_pe'], l)
    kc = cmp_out[:, 0]
    vc_t = cmp_out[:, 1].transpose(0, 1, 3, 2)
    bv_tc = _chunked_t(bv, 512)
    gates_t = gates[:, :3 * B_HEADS].reshape(B, L, G, 3 * B_HPG).transpose(0, 2, 3, 1)
    yb = _nsa_attn(bq.reshape(B, L, -1), kc, vc_t, bk.reshape(B, L, -1), bv_tc, gates_t, cov_t)

    cv = h[:, _P_CV:_P_CV + 1024].reshape(B, L, C_HEADS, 2 * C_DIM)
    cv_tc = _chunked_t(cv, _DIFF_TK)
    yc = _diff_attn(cq.reshape(B, L, -1), ck.reshape(B, L, -1), cv_tc, lw['lam'],
                    lw['c_subln_g'].reshape(2 * C_DIM, 1), lam_init)

    return _merge(ya.reshape(M, -1), yb.reshape(M, -1), yc.reshape(M, -1), lw['w_br'], l, h)


def kernel(x, c, w_ada, b_ada, w_in, a_lat_g, a_up, cmp_w1, cmp_w2, cmp_pe, lam, c_subln_g, w_br, w_o,
           w_ffn_in, w_ffn_out, ln_g, ln_b):
    B, L, D = x.shape
    depth = w_ada.shape[0]
    M = B * L
    alpha = (2 * depth) ** 0.25

    c_pad = jnp.zeros((8, D), f32).at[:B].set(c)
    mod = _ada(c_pad, w_ada, b_ada)[:, :B]
    mods = [[mod[l, :, i * D:(i + 1) * D].reshape(B, 1, D) for i in range(6)] for l in range(depth)]

    tabs128 = _rope_tables(L, HEAD_DIM)
    tabs64 = _rope_tables(L, IDX_DIM)
    cov_t = _cover_t(L)

    w_in_p = _pack_w_in(w_in)
    pe_flat = jnp.zeros((depth, 2, 8, CMP_LEN * HEAD_DIM), f32).at[:, :, 0].set(
        cmp_pe.reshape(depth, 2, CMP_LEN * HEAD_DIM)).astype(bf16)
    wb = dict(a_up=a_up.astype(bf16), cmp_w1=cmp_w1.astype(bf16), cmp_w2=cmp_w2.astype(bf16),
              cmp_pe=pe_flat, w_br=w_br.astype(bf16))
    w_o_b = w_o.astype(bf16)
    w_fi_b = w_ffn_in.astype(bf16)
    w_fo_b = w_ffn_out.astype(bf16)

    x2 = x.reshape(M, D)
    u = _lnmod(x2, mods[0][1], mods[0][0], L)
    for l in range(depth):
        lam_init = 0.8 - 0.6 * math.exp(-0.3 * l)
        sh_a, sc_a, g_a, sh_f, sc_f, g_f = mods[l]
        lw = dict(wb, a_lat_g=a_lat_g[l], lam=lam[l], c_subln_g=c_subln_g[l])
        h = _mm(u, w_in_p, l, 512, 768, bf16)
        merged = _token_mixing(h, B, L, l, lw, tabs128, tabs64, cov_t, lam_init)
        x2, u = _mm_res_ln(merged, w_o_b, l, x2, g_a, ln_g[l, 0], ln_b[l, 0], sc_f, sh_f, L, alpha, True)
        f = _ffn_in(u, w_fi_b, l)
        last = l == depth - 1
        nsc, nsh = (sc_f, sh_f) if last else (mods[l + 1][1], mods[l + 1][0])
        x2, u = _mm_res_ln(f, w_fo_b, l, x2, g_f, ln_g[l, 1], ln_b[l, 1], nsc, nsh, L, alpha, not last)
    return x2.reshape(B, L, D)
```

```python
import functools
import math

import numpy as np
import jax
import jax.numpy as jnp
from jax import lax
from jax.experimental import pallas as pl
from jax.experimental.pallas import tpu as pltpu

f32 = jnp.float32
bf16 = jnp.bfloat16
i32 = jnp.int32

D_MODEL = 2048
HEAD_DIM = 128
ROPE_THETA = 500000.0
NEG = -1e30
FORCE = 1e6
A_HEADS = 8
A_LATENT = 512
IDX_HEADS = 16
IDX_DIM = 64
DSA_TOPK = 256
B_HEADS = 8
B_KV_GROUPS = 2
B_HPG = B_HEADS // B_KV_GROUPS
CMP_LEN = 32
CMP_STRIDE = 16
SLC_LEN = 64
SLC_TOPN = 16
WIN_LEN = 512
C_HEADS = 4
C_DIM = 128
BRANCH_W = A_HEADS * HEAD_DIM
N_BRANCH = 3
D_FF = int(math.ceil(8 * D_MODEL / 3 / 256)) * 256

_O_AQ = 0
_O_ALAT = _O_AQ + A_HEADS * HEAD_DIM
_O_IQ = _O_ALAT + A_LATENT
_O_IK = _O_IQ + IDX_HEADS * IDX_DIM
_O_IW = _O_IK + IDX_DIM
_O_BQ = _O_IW + IDX_HEADS
_O_BKV = _O_BQ + B_HEADS * HEAD_DIM
_O_BG = _O_BKV + 3 * 2 * B_KV_GROUPS * HEAD_DIM
_O_CQ = _O_BG + 3 * B_HEADS
_O_CK = _O_CQ + C_HEADS * 2 * C_DIM
_O_CV = _O_CK + C_HEADS * 2 * C_DIM
_O_GL = _O_CV + C_HEADS * 2 * C_DIM
_N_IN = _O_GL + N_BRANCH * D_MODEL

_P_GL = 0
_P_BK = 6144
_P_BVC = 6912
_P_AQ = 7168
_P_IQ = 8192
_P_BQ = 9216
_P_CQ = 10240
_P_CK = 11264
_P_ALAT = 12288
_P_IKW = 12800
_P_BG = 12928
_P_TOT = 13056
_PV_CV = 0
_PV_BV = 1024
_PV_TOT = 1536

_VMEM_LIMIT = 48 * 1024 * 1024
_LOG2E = 1.4426950408889634
_INF = float("inf")
_RES_LN_TM = 512
_RES_LN_SUB = 256
_DIFF_TK = 1024
_HEAD_GROUP = 2


def _cparams(sem, vmem=_VMEM_LIMIT):
    return pltpu.CompilerParams(dimension_semantics=sem, vmem_limit_bytes=vmem)


def _sigmoid(x):
    return 1.0 / (1.0 + jnp.exp(-x))


def _dot_nt(a, b):
    return lax.dot_general(a, b, (((1,), (1,)), ((), ())), preferred_element_type=f32)


def _ada_kernel(c_ref, w_ref, b_ref, o_ref):
    c = c_ref[...]
    cs = c * _sigmoid(c)
    o_ref[0] = jnp.dot(cs, w_ref[0], preferred_element_type=f32,
                       precision=lax.Precision.HIGHEST) + b_ref[0]


def _ada(c_pad, w_ada, b_ada):
    depth, d, n = w_ada.shape
    tn = 512
    return pl.pallas_call(
        _ada_kernel,
        grid=(depth, n // tn),
        in_specs=[pl.BlockSpec((8, d), lambda l, j: (0, 0)),
                  pl.BlockSpec((1, d, tn), lambda l, j: (l, 0, j)),
                  pl.BlockSpec((1, 1, tn), lambda l, j: (l, 0, j))],
        out_specs=pl.BlockSpec((1, 8, tn), lambda l, j: (l, 0, j)),
        out_shape=jax.ShapeDtypeStruct((depth, 8, n), f32),
        compiler_params=_cparams(("parallel", "parallel")),
    )(c_pad, w_ada, b_ada.reshape(depth, 1, n))


def _ln_rows(x, eps):
    mu = jnp.mean(x, axis=-1, keepdims=True)
    d = x - mu
    var = jnp.mean(d * d, axis=-1, keepdims=True)
    return d * lax.rsqrt(var + eps)


def _lnmod_kernel(x_ref, sc_ref, sh_ref, o_ref):
    y = _ln_rows(x_ref[...], 1e-5)
    o_ref[...] = (y * (1.0 + sc_ref[0]) + sh_ref[0]).astype(o_ref.dtype)


def _lnmod(x2, sc, sh, L):
    m, d = x2.shape
    tm = 512
    per_b = L // tm
    return pl.pallas_call(
        _lnmod_kernel,
        grid=(m // tm,),
        in_specs=[pl.BlockSpec((tm, d), lambda i: (i, 0)),
                  pl.BlockSpec((1, 1, d), lambda i: (i // per_b, 0, 0)),
                  pl.BlockSpec((1, 1, d), lambda i: (i // per_b, 0, 0))],
        out_specs=pl.BlockSpec((tm, d), lambda i: (i, 0)),
        out_shape=jax.ShapeDtypeStruct((m, d), bf16),
        compiler_params=_cparams(("parallel",)),
    )(x2, sc, sh)


def _mm_kernel(a_ref, w_ref, o_ref):
    o_ref[...] = jnp.dot(a_ref[...], w_ref[...], preferred_element_type=f32).astype(o_ref.dtype)


def _mm(a, w, l, tm, tn, out_dtype, col0=0, n=None):
    m, k = a.shape
    n = w.shape[2] if n is None else n
    cb = col0 // tn
    return pl.pallas_call(
        _mm_kernel,
        grid=(n // tn, m // tm),
        in_specs=[pl.BlockSpec((tm, k), lambda j, i: (i, 0)),
                  pl.BlockSpec((None, k, tn), lambda j, i: (l, 0, cb + j))],
        out_specs=pl.BlockSpec((tm, tn), lambda j, i: (i, j)),
        out_shape=jax.ShapeDtypeStruct((m, n), out_dtype),
        compiler_params=_cparams(("parallel", "parallel")),
    )(a, w)


def _proj_t_kernel(w_ref, a_ref, o_ref):
    n, d = o_ref.shape[1], o_ref.shape[3]
    res = _dot_nt(w_ref[...], a_ref[...])
    for i in range(n):
        o_ref[0, i, 0] = res[i * d:(i + 1) * d, :].astype(o_ref.dtype)


def _proj_t(a, w_t, l, row0, n, d, tk, B, L):
    k = a.shape[1]
    rows = n * d
    nc = L // tk
    return pl.pallas_call(
        _proj_t_kernel,
        grid=(B, nc),
        in_specs=[pl.BlockSpec((None, rows, k), lambda b, c: (l, row0 // rows, 0)),
                  pl.BlockSpec((tk, k), lambda b, c: (b * nc + c, 0))],
        out_specs=pl.BlockSpec((1, n, 1, d, tk), lambda b, c: (b, 0, c, 0, 0)),
        out_shape=jax.ShapeDtypeStruct((B, n, nc, d, tk), bf16),
        compiler_params=_cparams(("parallel", "parallel")),
    )(w_t, a)


def _rope_heads(x, c, s1, s2, half):
    return x * c + pltpu.roll(x, half, 1) * s1 + pltpu.roll(x, 128 - half, 1) * s2


def _mm_rope_kernel(a_ref, w_ref, c_ref, s1_ref, s2_ref, o_ref):
    acc = jnp.dot(a_ref[...], w_ref[...], preferred_element_type=f32)
    c, s1, s2 = c_ref[...], s1_ref[...], s2_ref[...]
    for h in range(acc.shape[1] // 128):
        sl = slice(h * 128, (h + 1) * 128)
        o_ref[:, sl] = _rope_heads(acc[:, sl], c, s1, s2, 16).astype(o_ref.dtype)


def _mm_rope(a, w, l, n, tabs, L, tm, tn):
    m, k = a.shape
    per_b = L // tm
    tspec = pl.BlockSpec((tm, 128), lambda j, i: (i % per_b, 0))
    return pl.pallas_call(
        _mm_rope_kernel,
        grid=(n // tn, m // tm),
        in_specs=[pl.BlockSpec((tm, k), lambda j, i: (i, 0)),
                  pl.BlockSpec((None, k, tn), lambda j, i: (l, 0, j)),
                  tspec, tspec, tspec],
        out_specs=pl.BlockSpec((tm, tn), lambda j, i: (i, j)),
        out_shape=jax.ShapeDtypeStruct((m, n), bf16),
        compiler_params=_cparams(("parallel", "parallel")),
    )(a, w, *tabs)


def _prep_kernel(aq_ref, iq_ref, bq_ref, cq_ref, ck_ref, bk_ref, alat_ref, ikw_ref, bg_ref, alg_ref,
                 c_ref, s1_ref, s2_ref, c6_ref, s16_ref, s26_ref,
                 aq_o, iq_o, bq_o, cq_o, ck_o, bk_o, alat_o, ikw_o, g_o):
    c, s1, s2 = c_ref[...], s1_ref[...], s2_ref[...]
    c6, s16, s26 = c6_ref[...], s16_ref[...], s26_ref[...]
    scale = HEAD_DIM ** -0.5 * _LOG2E

    def rope_all(src, dst, mult, tabs, half):
        for h in range(src.shape[1] // 128):
            sl = slice(h * 128, (h + 1) * 128)
            y = _rope_heads(src[:, sl].astype(f32), *tabs, half)
            if mult != 1.0:
                y = y * mult
            dst[:, sl] = y.astype(dst.dtype)

    rope_all(aq_ref, aq_o, scale, (c, s1, s2), 16)
    rope_all(bq_ref, bq_o, scale, (c, s1, s2), 16)
    rope_all(cq_ref, cq_o, C_DIM ** -0.5 * _LOG2E, (c, s1, s2), 16)
    rope_all(ck_ref, ck_o, 1.0, (c, s1, s2), 16)
    rope_all(bk_ref, bk_o, 1.0, (c, s1, s2), 16)
    rope_all(iq_ref, iq_o, 1.0, (c6, s16, s26), 8)

    a = alat_ref[...].astype(f32)
    ms = jnp.mean(a * a, axis=-1, keepdims=True)
    alat_o[...] = (a * lax.rsqrt(ms + 1e-6) * alg_ref[...]).astype(alat_o.dtype)

    x = ikw_ref[...].astype(f32)
    lane = lax.broadcasted_iota(i32, x.shape, 1)
    isk = lane < IDX_DIM
    mu = jnp.sum(jnp.where(isk, x, 0.0), axis=-1, keepdims=True) * (1.0 / IDX_DIM)
    d = jnp.where(isk, x - mu, 0.0)
    var = jnp.sum(d * d, axis=-1, keepdims=True) * (1.0 / IDX_DIM)
    y = d * lax.rsqrt(var + 1e-5)
    yr = _rope_heads(y, c6, s16, s26, 8)
    ikw_o[...] = jnp.where(isk, yr, x * (IDX_DIM ** -0.5 * IDX_HEADS ** -0.5))

    g_o[...] = _sigmoid(bg_ref[...].astype(f32))


def _prep(h, a_lat_g, tabs128, tabs64, L):
    m = h.shape[0]
    tm = 256
    per_b = L // tm

    def hs(width, off):
        return pl.BlockSpec((tm, width), lambda i, _o=off // width: (i, _o))

    tspec = pl.BlockSpec((tm, 128), lambda i: (i % per_b, 0))

    def os(width):
        return pl.BlockSpec((tm, width), lambda i: (i, 0))

    outs = [(1024, bf16)] * 5 + [(768, bf16), (512, bf16), (128, f32), (128, f32)]
    return pl.pallas_call(
        _prep_kernel,
        grid=(m // tm,),
        in_specs=[hs(1024, _P_AQ), hs(1024, _P_IQ), hs(1024, _P_BQ), hs(1024, _P_CQ), hs(1024, _P_CK),
                  hs(768, _P_BK), hs(512, _P_ALAT), hs(128, _P_IKW), hs(128, _P_BG),
                  pl.BlockSpec((1, A_LATENT), lambda i: (0, 0))] + [tspec] * 6,
        out_specs=[os(w) for w, _ in outs],
        out_shape=[jax.ShapeDtypeStruct((m, w), dt) for w, dt in outs],
        compiler_params=_cparams(("parallel",)),
    )(h, h, h, h, h, h, h, h, h, a_lat_g.reshape(1, A_LATENT), *tabs128, *tabs64)


def _f32_order_key(x):
    b = int(np.float32(x).view(np.int32))
    return b ^ ((b >> 31) & 0x7FFFFFFF)


_KEY_NEG = _f32_order_key(NEG)


def _dsa_mask_kernel(ik_ref, iq_ref, iw_ref, o_ref, key_ref, qp_ref, j_ref, *, L, tq, ksel):
    j = pl.program_id(1)
    ck = 512
    nck = L // ck
    nc = ((j + 1) * tq + ck - 1) // ck
    n_out_i = L - nc * ck
    n_out = n_out_i.astype(f32)
    t_row = j * tq + lax.broadcasted_iota(i32, (1, tq), 1)

    for hp in range(IDX_HEADS // 2):
        for e in range(2):
            h = 2 * hp + e
            qp_ref[hp, e * tq:(e + 1) * tq, :] = iq_ref[0, :, h * IDX_DIM:(h + 1) * IDX_DIM]

    def score_chunk(c, carry):
        off = pl.multiple_of(c * ck, ck)
        ikc = ik_ref[0, pl.ds(off, ck), :]
        acc = jnp.zeros((ck, tq), f32)
        for hp in range(IDX_HEADS // 2):
            s2 = _dot_nt(ikc, qp_ref[hp])
            acc = acc + jnp.maximum(s2[:, :tq], 0.0) * iw_ref[0, 2 * hp:2 * hp + 1, :]
            acc = acc + jnp.maximum(s2[:, tq:], 0.0) * iw_ref[0, 2 * hp + 1:2 * hp + 2, :]
        acc = jnp.where(acc == 0.0, 0.0, acc)
        sidx = off + lax.broadcasted_iota(i32, (ck, tq), 0)
        sc = jnp.where(sidx <= t_row, acc, NEG)
        bits = lax.bitcast_convert_type(sc, i32)
        key_ref[pl.ds(off, ck), :] = bits ^ ((bits >> 31) & 0x7FFFFFFF)
        return carry

    lax.fori_loop(0, nc, score_chunk, 0)

    def count(pred_fn):
        def body(c, cnt):
            off = pl.multiple_of(c * ck, ck)
            k = key_ref[pl.ds(off, ck), :]
            sidx = off + lax.broadcasted_iota(i32, (ck, tq), 0)
            return cnt + jnp.sum(pred_fn(k, sidx).reshape(ck // 64, 64, tq), axis=0)
        part = lax.fori_loop(0, nc, body, jnp.zeros((64, tq), f32))
        return jnp.sum(part, axis=0, keepdims=True)

    kf = float(ksel)

    def bit_body(i, carry):
        thr, cnt_thr = carry
        cand = thr + lax.shift_left(jnp.int32(1), 31 - i)
        cnt = count(lambda k, s: jnp.where(k >= cand, 1.0, 0.0)) + jnp.where(_KEY_NEG >= cand, n_out, 0.0)
        ok = cnt >= kf
        return jnp.where(ok, cand, thr), jnp.where(ok, cnt, cnt_thr)

    thr, cnt_ge = lax.fori_loop(0, 32, bit_body, (jnp.full((1, tq), -2 ** 31, i32),
                                                   jnp.full((1, tq), float(L), f32)))
    cnt_gt = count(lambda k, s: jnp.where(k > thr, 1.0, 0.0)) + jnp.where(_KEY_NEG > thr, n_out, 0.0)
    need = kf - cnt_gt

    j_ref[...] = jnp.full((1, tq), L, i32)

    @pl.when(jnp.max(cnt_ge) > kf)
    def _():
        nbits = L.bit_length() - 1

        def jbit(i, cur):
            cand = cur | lax.shift_left(jnp.int32(1), nbits - 1 - i)
            f = count(lambda k, s: jnp.where(k == thr, jnp.where(s < cand, 1.0, 0.0), 0.0))
            f = f + jnp.where(thr == _KEY_NEG, jnp.clip(cand - nc * ck, 0, n_out_i).astype(f32), 0.0)
            return jnp.where(f < need, cand, cur)

        j_ref[...] = lax.fori_loop(0, nbits, jbit, jnp.zeros((1, tq), i32))

    jlast = j_ref[...]

    def write(c, carry):
        off = pl.multiple_of(c * ck, ck)
        k = key_ref[pl.ds(off, ck), :]
        sidx = off + lax.broadcasted_iota(i32, (ck, tq), 0)
        sel = jnp.where(k > thr, _INF, jnp.where(k == thr, jnp.where(sidx <= jlast, _INF, NEG), NEG))
        o_ref[0, pl.ds(off, ck), :] = jnp.where(sidx <= t_row, sel, NEG)
        return carry

    lax.fori_loop(0, nc, write, 0)

    def write_rest(c, carry):
        off = pl.multiple_of(c * ck, ck)
        o_ref[0, pl.ds(off, ck), :] = jnp.full((ck, tq), NEG, f32)
        return carry

    lax.fori_loop(nc, nck, write_rest, 0)


def _dsa_mask(ik, iq, iw_t):
    B, L, _ = iq.shape
    tq = 128
    ksel = min(DSA_TOPK, L // 4)
    return pl.pallas_call(
        functools.partial(_dsa_mask_kernel, L=L, tq=tq, ksel=ksel),
        grid=(B, L // tq),
        in_specs=[pl.BlockSpec((1, L, IDX_DIM), lambda b, j: (b, 0, 0)),
                  pl.BlockSpec((1, tq, IDX_HEADS * IDX_DIM), lambda b, j: (b, j, 0)),
                  pl.BlockSpec((1, IDX_HEADS, tq), lambda b, j: (b, 0, j))],
        out_specs=pl.BlockSpec((1, L, tq), lambda b, j: (b, 0, j)),
        out_shape=jax.ShapeDtypeStruct((B, L, L), f32),
        scratch_shapes=[pltpu.VMEM((L, tq), i32), pltpu.VMEM((IDX_HEADS // 2, 2 * tq, IDX_DIM), bf16),
                        pltpu.VMEM((1, tq), i32)],
        compiler_params=_cparams(("parallel", "parallel")),
    )(ik, iq, iw_t)


def _col_reduce(x, op):
    r, c = x.shape
    if r > 64:
        x = op(x.reshape(r // 64, 64, c), axis=0)
    return op(x, axis=0, keepdims=True)


def _flash_group(ss, cap, v_ts, m_refs, l_refs, acc_refs):
    n = len(ss)
    if cap is not None:
        ss = [jnp.minimum(s, cap) for s in ss]
    m_prev = [r[...] for r in m_refs]
    m_new = [jnp.maximum(m_prev[i], _col_reduce(ss[i], jnp.max)) for i in range(n)]
    alpha = [jnp.exp2(m_prev[i] - m_new[i]) for i in range(n)]
    ps = [jnp.exp2(ss[i] - m_new[i]) for i in range(n)]
    l_new = [alpha[i] * l_refs[i][...] + _col_reduce(ps[i], jnp.sum) for i in range(n)]
    pv = [jnp.dot(v_ts[i], ps[i].astype(bf16), preferred_element_type=f32) for i in range(n)]
    for i in range(n):
        acc_new = acc_refs[i][...] * alpha[i] + pv[i]
        if cap is not None:
            dead = m_new[i] <= NEG
            l_new[i] = jnp.where(dead, 0.0, l_new[i])
            acc_new = jnp.where(dead, 0.0, acc_new)
        l_refs[i][...] = l_new[i]
        acc_refs[i][...] = acc_new
        m_refs[i][...] = m_new[i]


def _flash_heads(score_fns, cap, v_ts, m_ref, l_ref, acc_ref, group):
    n = len(score_fns)
    groups = [list(range(g, min(g + group, n))) for g in range(0, n, group)]
    nxt = [score_fns[h]() for h in groups[0]]
    for gi, hs in enumerate(groups):
        ss = nxt
        if gi + 1 < len(groups):
            nxt = [score_fns[h]() for h in groups[gi + 1]]
        _flash_group(ss, cap, [v_ts[h] for h in hs], [m_ref.at[h] for h in hs],
                     [l_ref.at[h] for h in hs], [acc_ref.at[h] for h in hs])


def _flash_init(m_ref, l_ref, acc_ref):
    m_ref[...] = jnp.full(m_ref.shape, NEG, f32)
    l_ref[...] = jnp.zeros(l_ref.shape, f32)
    acc_ref[...] = jnp.zeros(acc_ref.shape, f32)


def _dsa_attn_kernel(q_ref, k_ref, vt_ref, mask_ref, o_ref, m_ref, l_ref, acc_ref, *, tq, tk, nkc):
    j = pl.program_id(1)
    c = pl.program_id(2)

    @pl.when(c == 0)
    def _():
        _flash_init(m_ref, l_ref, acc_ref)

    @pl.when(c * tk < (j + 1) * tq)
    def _():
        cap = mask_ref[0]

        def scores(h):
            sl = slice(h * HEAD_DIM, (h + 1) * HEAD_DIM)
            return lambda: _dot_nt(k_ref[0, :, sl], q_ref[0, :, sl])

        v_ts = [vt_ref[0, h, 0] for h in range(A_HEADS)]
        _flash_heads([scores(h) for h in range(A_HEADS)], cap, v_ts, m_ref, l_ref, acc_ref, _HEAD_GROUP)

    @pl.when(c == nkc - 1)
    def _():
        for h in range(A_HEADS):
            o = acc_ref[h] / jnp.maximum(l_ref[h], 1e-30)
            o_ref[0, :, h * HEAD_DIM:(h + 1) * HEAD_DIM] = o.T.astype(o_ref.dtype)


def _dsa_attn(q, k, v_t, mask_t):
    B, L, W = q.shape
    tq, tk = 256, 512
    nkc = L // tk

    def last(j):
        return ((j + 1) * tq - 1) // tk

    return pl.pallas_call(
        functools.partial(_dsa_attn_kernel, tq=tq, tk=tk, nkc=nkc),
        grid=(B, L // tq, nkc),
        in_specs=[pl.BlockSpec((1, tq, W), lambda b, j, c: (b, j, 0)),
                  pl.BlockSpec((1, tk, W), lambda b, j, c: (b, jnp.minimum(c, last(j)), 0)),
                  pl.BlockSpec((1, A_HEADS, 1, HEAD_DIM, tk),
                               lambda b, j, c: (b, 0, jnp.minimum(c, last(j)), 0, 0)),
                  pl.BlockSpec((1, tk, tq), lambda b, j, c: (b, jnp.minimum(c, last(j)), j))],
        out_specs=pl.BlockSpec((1, tq, W), lambda b, j, c: (b, j, 0)),
        out_shape=jax.ShapeDtypeStruct((B, L, W), bf16),
        scratch_shapes=[pltpu.VMEM((A_HEADS, 1, tq), f32), pltpu.VMEM((A_HEADS, 1, tq), f32),
                        pltpu.VMEM((A_HEADS, HEAD_DIM, tq), f32)],
        compiler_params=_cparams(("parallel", "parallel", "arbitrary")),
    )(q, k, v_t, mask_t)


def _nsa_cmp_kernel(x_ref, w1_ref, w2_ref, pe_ref, o_ref):
    x = x_ref[0, 0, 0]
    w1 = w1_ref[0]
    half = CMP_STRIDE * HEAD_DIM
    a = jnp.dot(x, w1[:half], preferred_element_type=f32)
    b = jnp.dot(x, w1[half:], preferred_element_type=f32)
    pe = jnp.dot(pe_ref[0], w1, preferred_element_type=f32)[0:1]
    n = a.shape[0]
    pre = a + pltpu.roll(b, n - 1, 0) + pe
    act = pre * _sigmoid(pre)
    o_ref[0, 0, 0] = jnp.dot(act.astype(bf16), w2_ref[0], preferred_element_type=f32).astype(o_ref.dtype)


def _nsa_compress(x, w1, w2, pe, l):
    B, _, G, n, wd = x.shape
    return pl.pallas_call(
        _nsa_cmp_kernel,
        grid=(B, 2, G),
        in_specs=[pl.BlockSpec((1, 1, 1, n, wd), lambda b, t, g: (b, t, g, 0, 0)),
                  pl.BlockSpec((None, 1, CMP_LEN * HEAD_DIM, HEAD_DIM), lambda b, t, g: (l, t, 0, 0)),
                  pl.BlockSpec((None, 1, HEAD_DIM, HEAD_DIM), lambda b, t, g: (l, t, 0, 0)),
                  pl.BlockSpec((None, 1, 8, CMP_LEN * HEAD_DIM), lambda b, t, g: (l, t, 0, 0))],
        out_specs=pl.BlockSpec((1, 1, 1, n, HEAD_DIM), lambda b, t, g: (b, t, g, 0, 0)),
        out_shape=jax.ShapeDtypeStruct((B, 2, G, n, HEAD_DIM), bf16),
        compiler_params=_cparams(("parallel", "parallel", "parallel")),
    )(x, w1, w2, pe)


def _nsa_attn_kernel(q_ref, kc_ref, vct_ref, ks_ref, vst_ref, kw_ref, vwt_ref, g_ref, cov_ref, o_ref,
                     m_ref, l_ref, acc_ref, out_ref, sel_ref, *, L, tq, tk, n_sel):
    j = pl.program_id(2)
    t_row = j * tq + lax.broadcasted_iota(i32, (1, tq), 1)
    n_cmp = kc_ref.shape[2]
    n_slc = L // SLC_LEN
    qs = [q_ref[0, :, hh * HEAD_DIM:(hh + 1) * HEAD_DIM] for hh in range(B_HPG)]

    def gate(hh, i):
        return g_ref[0, 0, hh * 3 + i:hh * 3 + i + 1, :]

    kc = kc_ref[0, 0]
    vct = vct_ref[0, 0]
    cend = lax.broadcasted_iota(i32, (n_cmp, tq), 0) * CMP_STRIDE + (CMP_LEN - 1)
    mc = cend <= t_row
    psum = jnp.zeros((n_cmp, tq), f32)
    for hh in range(B_HPG):
        s = jnp.where(mc, _dot_nt(kc, qs[hh]), NEG)
        mx = _col_reduce(s, jnp.max)
        p = jnp.where(mc, jnp.exp2(s - mx), 0.0)
        p = p / jnp.maximum(_col_reduce(p, jnp.sum), 1e-30)
        psum = psum + p
        out_ref[hh] = gate(hh, 0) * jnp.dot(vct, p.astype(bf16), preferred_element_type=f32)
    imp = jnp.dot(cov_ref[...], psum, preferred_element_type=f32, precision=lax.Precision.HIGHEST)
    blk = lax.broadcasted_iota(i32, (n_slc, tq), 0)
    cur = t_row >> (SLC_LEN.bit_length() - 1)
    forced = (blk == 0) | (blk == cur) | (blk == cur - 1)
    imp = jnp.where(forced, FORCE, jnp.where(blk <= cur, imp, NEG))
    rank = jnp.zeros((n_slc, tq), f32)
    for r in range(n_slc):
        row = imp[r:r + 1, :]
        rank = rank + jnp.where(row > imp, 1.0, jnp.where(row == imp, jnp.where(blk > r, 1.0, 0.0), 0.0))
    sel_ref[...] = jnp.where(rank < float(n_sel), _INF, NEG)

    def finish(i):
        for hh in range(B_HPG):
            o = acc_ref[hh] / jnp.maximum(l_ref[hh], 1e-30)
            out_ref[hh] = out_ref[hh] + gate(hh, i) * o

    def run_branch(k_ref, vt_ref, c_lo, c_hi, mask_fn):
        for hh in range(B_HPG):
            _flash_init(m_ref.at[hh], l_ref.at[hh], acc_ref.at[hh])

        def body(c, carry):
            off = pl.multiple_of(c * tk, tk)
            kch = k_ref[0, pl.ds(off, tk), :]
            vch = vt_ref[0, 0, c]
            sidx = off + lax.broadcasted_iota(i32, (tk, tq), 0)
            cap = mask_fn(c, sidx)
            fns = [functools.partial(_dot_nt, kch, qs[hh]) for hh in range(B_HPG)]
            _flash_heads(fns, cap, [vch] * B_HPG, m_ref, l_ref, acc_ref, _HEAD_GROUP)
            return carry

        lax.fori_loop(c_lo, c_hi, body, 0)

    bpc = tk // SLC_LEN

    def slc_mask(c, sidx):
        rows = [jnp.broadcast_to(sel_ref[pl.ds(c * bpc + b, 1), :], (SLC_LEN, tq)) for b in range(bpc)]
        return jnp.where(sidx <= t_row, jnp.concatenate(rows, axis=0), NEG)

    c_hi = ((j + 1) * tq + tk - 1) // tk
    run_branch(ks_ref, vst_ref, 0, c_hi, slc_mask)
    finish(1)

    def win_mask(c, sidx):
        d = t_row - sidx
        return jnp.where(d >= 0, jnp.where(d < WIN_LEN, _INF, NEG), NEG)

    c_lo = jnp.maximum(j * tq - (WIN_LEN - 1), 0) // tk
    run_branch(kw_ref, vwt_ref, c_lo, c_hi, win_mask)
    finish(2)

    for hh in range(B_HPG):
        o_ref[0, :, hh * HEAD_DIM:(hh + 1) * HEAD_DIM] = out_ref[hh].T.astype(o_ref.dtype)


def _nsa_attn(bq, kc, vc_t, bk, bv_tc, gates_t, cov_t):
    B, L, _ = bq.shape
    G = B_KV_GROUPS
    tq, tk = 256, 512
    n_cmp = kc.shape[2]
    n_slc = L // SLC_LEN
    n_sel = min(SLC_TOPN, n_slc)
    gw = B_HPG * HEAD_DIM
    nc = L // tk
    return pl.pallas_call(
        functools.partial(_nsa_attn_kernel, L=L, tq=tq, tk=tk, n_sel=n_sel),
        grid=(B, G, L // tq),
        in_specs=[pl.BlockSpec((1, tq, gw), lambda b, g, j: (b, j, g)),
                  pl.BlockSpec((1, 1, n_cmp, HEAD_DIM), lambda b, g, j: (b, g, 0, 0)),
                  pl.BlockSpec((1, 1, HEAD_DIM, n_cmp), lambda b, g, j: (b, g, 0, 0)),
                  pl.BlockSpec((1, L, HEAD_DIM), lambda b, g, j: (b, 0, 2 + g)),
                  pl.BlockSpec((1, 1, nc, HEAD_DIM, tk), lambda b, g, j: (b, g, 0, 0, 0)),
                  pl.BlockSpec((1, L, HEAD_DIM), lambda b, g, j: (b, 0, 4 + g)),
                  pl.BlockSpec((1, 1, nc, HEAD_DIM, tk), lambda b, g, j: (b, B_KV_GROUPS + g, 0, 0, 0)),
                  pl.BlockSpec((1, 1, 3 * B_HPG, tq), lambda b, g, j: (b, g, 0, j)),
                  pl.BlockSpec((n_slc, n_cmp), lambda b, g, j: (0, 0))],
        out_specs=pl.BlockSpec((1, tq, gw), lambda b, g, j: (b, j, g)),
        out_shape=jax.ShapeDtypeStruct((B, L, B_HEADS * HEAD_DIM), bf16),
        scratch_shapes=[pltpu.VMEM((B_HPG, 1, tq), f32), pltpu.VMEM((B_HPG, 1, tq), f32),
                        pltpu.VMEM((B_HPG, HEAD_DIM, tq), f32), pltpu.VMEM((B_HPG, HEAD_DIM, tq), f32),
                        pltpu.VMEM((n_slc, tq), f32)],
        compiler_params=_cparams(("parallel", "parallel", "parallel")),
    )(bq, kc, vc_t, bk, bv_tc, bk, bv_tc, gates_t, cov_t)


def _diff_attn_kernel(q_ref, k_ref, vt_ref, lam_ref, g_ref, o_ref, m_ref, l_ref, acc_ref,
                      *, tq, tk, lam_init):
    j = pl.program_id(2)
    t_row = j * tq + lax.broadcasted_iota(i32, (1, tq), 1)
    qs = [q_ref[0, :, mi * C_DIM:(mi + 1) * C_DIM] for mi in range(2)]
    for mi in range(2):
        _flash_init(m_ref.at[mi], l_ref.at[mi], acc_ref.at[mi])

    def chunk(c, masked):
        off = pl.multiple_of(c * tk, tk)
        vch = vt_ref[0, 0, c]
        cap = None
        if masked:
            sidx = off + lax.broadcasted_iota(i32, (tk, tq), 0)
            cap = jnp.where(sidx <= t_row, _INF, NEG)
        ss = [_dot_nt(k_ref[0, pl.ds(off, tk), mi * C_DIM:(mi + 1) * C_DIM], qs[mi]) for mi in range(2)]
        _flash_group(ss, cap, [vch] * 2, [m_ref.at[mi] for mi in range(2)],
                     [l_ref.at[mi] for mi in range(2)], [acc_ref.at[mi] for mi in range(2)])

    def body(c, carry):
        chunk(c, False)
        return carry

    c_diag = (j * tq) // tk
    lax.fori_loop(0, c_diag, body, 0)
    chunk(c_diag, True)

    lam = lam_ref[...]
    lam_val = (jnp.exp(jnp.sum(lam[0:1] * lam[1:2], axis=1, keepdims=True))
               - jnp.exp(jnp.sum(lam[2:3] * lam[3:4], axis=1, keepdims=True)) + lam_init)
    o = (acc_ref[0] / jnp.maximum(l_ref[0], 1e-30)
         - lam_val * (acc_ref[1] / jnp.maximum(l_ref[1], 1e-30)))
    ms = jnp.mean(o * o, axis=0, keepdims=True)
    y = o * lax.rsqrt(ms + 1e-6) * g_ref[...] * (1.0 - lam_init)
    o_ref[0] = y.T.astype(o_ref.dtype)


def _diff_attn(cq, ck, cv_tc, lam, g_col, lam_init):
    B, L, _ = cq.shape
    tq, tk = 256, cv_tc.shape[-1]
    hw = 2 * C_DIM
    nc = L // tk
    return pl.pallas_call(
        functools.partial(_diff_attn_kernel, tq=tq, tk=tk, lam_init=lam_init),
        grid=(B, C_HEADS, L // tq),
        in_specs=[pl.BlockSpec((1, tq, hw), lambda b, h, j: (b, j, h)),
                  pl.BlockSpec((1, L, hw), lambda b, h, j: (b, 0, h)),
                  pl.BlockSpec((1, 1, nc, hw, tk), lambda b, h, j: (b, h, 0, 0, 0)),
                  pl.BlockSpec((4, C_DIM), lambda b, h, j: (0, 0)),
                  pl.BlockSpec((hw, 1), lambda b, h, j: (0, 0))],
        out_specs=pl.BlockSpec((1, tq, hw), lambda b, h, j: (b, j, h)),
        out_shape=jax.ShapeDtypeStruct((B, L, C_HEADS * hw), bf16),
        scratch_shapes=[pltpu.VMEM((2, 1, tq), f32), pltpu.VMEM((2, 1, tq), f32),
                        pltpu.VMEM((2, hw, tq), f32)],
        compiler_params=_cparams(("parallel", "parallel", "parallel")),
    )(cq, ck, cv_tc, lam, g_col)


def _merge_kernel(ya_ref, yb_ref, yc_ref, w_ref, g0_ref, g1_ref, g2_ref, o_ref):
    acc = None
    for r, (y_ref, g_ref) in enumerate(((ya_ref, g0_ref), (yb_ref, g1_ref), (yc_ref, g2_ref))):
        br = jnp.dot(y_ref[...], w_ref[r], preferred_element_type=f32)
        t = _sigmoid(g_ref[...].astype(f32)) * br
        acc = t if acc is None else acc + t
    o_ref[...] = acc.astype(o_ref.dtype)


def _merge(ya, yb, yc, w_br, l, h):
    m, kw = ya.shape
    tm, tn = 512, 512
    npb = D_MODEL // tn
    yspec = pl.BlockSpec((tm, kw), lambda j, i: (i, 0))

    def gspec(r):
        return pl.BlockSpec((tm, tn), lambda j, i, _r=r: (i, _r * npb + j))

    return pl.pallas_call(
        _merge_kernel,
        grid=(npb, m // tm),
        in_specs=[yspec, yspec, yspec,
                  pl.BlockSpec((None, N_BRANCH, kw, tn), lambda j, i: (l, 0, 0, j)),
                  gspec(0), gspec(1), gspec(2)],
        out_specs=pl.BlockSpec((tm, tn), lambda j, i: (i, j)),
        out_shape=jax.ShapeDtypeStruct((m, D_MODEL), bf16),
        compiler_params=_cparams(("parallel", "parallel")),
    )(ya, yb, yc, w_br, h, h, h)


def _mm_res_ln_kernel(a_ref, w_ref, x_ref, gate_ref, lg_ref, lb_ref, sc_ref, sh_ref, xo_ref, *u_refs,
                      alpha, sub):
    tm = a_ref.shape[0]
    for r in range(tm // sub):
        rows = slice(r * sub, (r + 1) * sub)
        y = jnp.dot(a_ref[rows, :], w_ref[...], preferred_element_type=f32)
        z = alpha * x_ref[rows, :] + gate_ref[0] * y
        xn = _ln_rows(z, 1e-5) * lg_ref[...] + lb_ref[...]
        xo_ref[rows, :] = xn
        if u_refs:
            u_refs[0][rows, :] = (_ln_rows(xn, 1e-5) * (1.0 + sc_ref[0]) + sh_ref[0]).astype(bf16)


def _mm_res_ln(a, w, l, x2, gate, ln_g, ln_b, sc, sh, L, alpha, emit_u):
    m, kdim = a.shape
    d = w.shape[2]
    tm = _RES_LN_TM if kdim * d * 2 <= 16 * 1024 * 1024 else _RES_LN_SUB
    per_b = L // tm
    bspec = pl.BlockSpec((1, 1, d), lambda i: (i // per_b, 0, 0))
    vspec = pl.BlockSpec((1, d), lambda i: (0, 0))
    rspec = pl.BlockSpec((tm, d), lambda i: (i, 0))
    out_shape = [jax.ShapeDtypeStruct((m, d), f32)]
    out_specs = [rspec]
    if emit_u:
        out_shape.append(jax.ShapeDtypeStruct((m, d), bf16))
        out_specs.append(rspec)
    res = pl.pallas_call(
        functools.partial(_mm_res_ln_kernel, alpha=alpha, sub=_RES_LN_SUB),
        grid=(m // tm,),
        in_specs=[pl.BlockSpec((tm, kdim), lambda i: (i, 0)),
                  pl.BlockSpec((None, kdim, d), lambda i: (l, 0, 0), pipeline_mode=pl.Buffered(1)),
                  rspec, bspec, vspec, vspec, bspec, bspec],
        out_specs=out_specs,
        out_shape=out_shape,
        compiler_params=_cparams(("parallel",), 60 * 1024 * 1024),
    )(a, w, x2, gate, ln_g.reshape(1, d), ln_b.reshape(1, d), sc, sh)
    return res if emit_u else (res[0], None)


def _ffn_in_kernel(a_ref, wg_ref, wu_ref, o_ref):
    a = a_ref[...]
    g = jnp.dot(a, wg_ref[...], preferred_element_type=f32)
    u = jnp.dot(a, wu_ref[...], preferred_element_type=f32)
    o_ref[...] = (g * _sigmoid(g) * u).astype(o_ref.dtype)


def _ffn_in(a, w, l):
    m, k = a.shape
    tm, tn = 512, 512
    nb = D_FF // tn
    return pl.pallas_call(
        _ffn_in_kernel,
        grid=(nb, m // tm),
        in_specs=[pl.BlockSpec((tm, k), lambda j, i: (i, 0)),
                  pl.BlockSpec((None, k, tn), lambda j, i: (l, 0, j)),
                  pl.BlockSpec((None, k, tn), lambda j, i: (l, 0, nb + j))],
        out_specs=pl.BlockSpec((tm, tn), lambda j, i: (i, j)),
        out_shape=jax.ShapeDtypeStruct((m, D_FF), bf16),
        compiler_params=_cparams(("parallel", "parallel")),
    )(a, w, w)


def _rope_tables(L, d):
    r = d // 4
    half = r // 2
    inv = ROPE_THETA ** (-(jnp.arange(half, dtype=f32) * 2.0) / r)
    ang = jnp.arange(L).astype(f32)[:, None] * inv[None, :]
    cos, sin = jnp.cos(ang), jnp.sin(ang)
    z = jnp.zeros((L, d - r), f32)
    zh = jnp.zeros((L, half), f32)
    c = jnp.concatenate([cos, cos, jnp.ones((L, d - r), f32)], axis=1)
    s1 = jnp.concatenate([zh, sin, z], axis=1)
    s2 = jnp.concatenate([-sin, zh, z], axis=1)
    rep = 128 // d
    return tuple(jnp.tile(t, (1, rep)) for t in (c, s1, s2))


def _pack_w_in_kernel(w_ref, o_ref, ov_ref):
    def cp(src, width, dst):
        for o in range(0, width, 1024):
            wd = min(1024, width - o)
            o_ref[:, dst + o:dst + o + wd] = w_ref[src + o:src + o + wd, :].T.astype(bf16)

    def bkv(i, kv, g):
        return _O_BKV + ((i * 2 + kv) * B_KV_GROUPS + g) * HEAD_DIM

    cp(_O_GL, N_BRANCH * D_MODEL, _P_GL)
    for src, dst in ((_O_AQ, _P_AQ), (_O_IQ, _P_IQ), (_O_BQ, _P_BQ), (_O_CQ, _P_CQ), (_O_CK, _P_CK)):
        cp(src, 1024, dst)
    for i in range(3):
        for g in range(B_KV_GROUPS):
            cp(bkv(i, 0, g), HEAD_DIM, _P_BK + (i * B_KV_GROUPS + g) * HEAD_DIM)
    for g in range(B_KV_GROUPS):
        cp(bkv(0, 1, g), HEAD_DIM, _P_BVC + g * HEAD_DIM)
    cp(_O_ALAT, A_LATENT, _P_ALAT)
    ov_ref[_PV_CV:_PV_CV + 1024, :] = w_ref[_O_CV:_O_CV + 1024, :].astype(bf16)
    for i in (1, 2):
        for g in range(B_KV_GROUPS):
            dst = _PV_BV + ((i - 1) * B_KV_GROUPS + g) * HEAD_DIM
            ov_ref[dst:dst + HEAD_DIM, :] = w_ref[bkv(i, 1, g):bkv(i, 1, g) + HEAD_DIM, :].astype(bf16)
    lane = lax.broadcasted_iota(i32, (w_ref.shape[1], 128), 1)
    assert _O_IW == _O_IK + IDX_DIM and _O_IK % 8 == 0 and _O_BG % 8 == 0
    blk = w_ref[_O_IK:_O_IK + 128, :].T
    o_ref[:, _P_IKW:_P_IKW + 128] = jnp.where(lane < IDX_DIM + IDX_HEADS, blk, 0.0).astype(bf16)
    blk = w_ref[_O_BG:_O_BG + 128, :].T
    o_ref[:, _P_BG:_P_BG + 128] = jnp.where(lane < 3 * B_HEADS, blk, 0.0).astype(bf16)


def _pack_w_in(w_in):
    depth, k, n = w_in.shape
    w_t = jnp.swapaxes(w_in, 1, 2)
    tc = 128
    return pl.pallas_call(
        _pack_w_in_kernel,
        grid=(depth, k // tc),
        in_specs=[pl.BlockSpec((None, n, tc), lambda l, i: (l, 0, i))],
        out_specs=[pl.BlockSpec((None, tc, _P_TOT), lambda l, i: (l, i, 0)),
                   pl.BlockSpec((None, _PV_TOT, tc), lambda l, i: (l, 0, i))],
        out_shape=[jax.ShapeDtypeStruct((depth, k, _P_TOT), bf16),
                   jax.ShapeDtypeStruct((depth, _PV_TOT, k), bf16)],
        compiler_params=_cparams(("parallel", "parallel")),
    )(w_t)


def _cover_t(L):
    n_cmp_pad = L // CMP_STRIDE
    starts = np.arange(n_cmp_pad) * CMP_STRIDE
    slc_start = np.arange(L // SLC_LEN) * SLC_LEN
    cover = ((starts[:, None] < slc_start[None, :] + SLC_LEN)
             & (starts[:, None] + CMP_LEN > slc_start[None, :])).astype(np.float32)
    n_cmp = (L - CMP_LEN) // CMP_STRIDE + 1
    cover[n_cmp:] = 0.0
    return jnp.asarray(cover.T)


def _token_mixing(u, h, B, L, l, lw, tabs128, tabs64, cov_t, lam_init):
    M = B * L
    G = B_KV_GROUPS
    aq, iq, bq, cq, ck, bk, alat_n, ikw, gates = _prep(h, lw['a_lat_g'], tabs128, tabs64, L)

    aw = A_HEADS * HEAD_DIM
    ak = _mm_rope(alat_n, lw['a_up'], l, aw, tabs128, L, 512, aw)
    av_tc = _proj_t(alat_n, lw['a_up_vt'], l, 0, A_HEADS, HEAD_DIM, 512, B, L)
    ik = ikw[:, :IDX_DIM].astype(bf16).reshape(B, L, IDX_DIM)
    iw_t = ikw[:, IDX_DIM:IDX_DIM + IDX_HEADS].reshape(B, L, IDX_HEADS).transpose(0, 2, 1)
    mask_t = _dsa_mask(ik, iq.reshape(B, L, -1), iw_t)
    ya = _dsa_attn(aq.reshape(B, L, -1), ak.reshape(B, L, -1), av_tc, mask_t)

    bvc = h[:, _P_BVC:_P_BVC + G * HEAD_DIM].reshape(B, L, G, HEAD_DIM)
    bkc = bk[:, :G * HEAD_DIM].reshape(B, L, G, HEAD_DIM)
    n_row = L // CMP_STRIDE
    xcmp = jnp.stack([bkc, bvc], axis=1)
    xcmp = xcmp.transpose(0, 1, 3, 2, 4).reshape(B, 2, G, n_row, CMP_STRIDE * HEAD_DIM)
    cmp_out = _nsa_compress(xcmp, lw['cmp_w1'], lw['cmp_w2'], lw['cmp_pe'], l)
    kc = cmp_out[:, 0]
    vc_t = cmp_out[:, 1].transpose(0, 1, 3, 2)
    bv_tc = _proj_t(u, lw['w_vt'], l, _PV_BV, 2 * G, HEAD_DIM, 512, B, L)
    gates_t = gates[:, :3 * B_HEADS].reshape(B, L, G, 3 * B_HPG).transpose(0, 2, 3, 1)
    yb = _nsa_attn(bq.reshape(B, L, -1), kc, vc_t, bk.reshape(B, L, -1), bv_tc, gates_t, cov_t)

    cv_tc = _proj_t(u, lw['w_vt'], l, _PV_CV, C_HEADS, 2 * C_DIM, _DIFF_TK, B, L)
    yc = _diff_attn(cq.reshape(B, L, -1), ck.reshape(B, L, -1), cv_tc, lw['lam'],
                    lw['c_subln_g'].reshape(2 * C_DIM, 1), lam_init)

    return _merge(ya.reshape(M, -1), yb.reshape(M, -1), yc.reshape(M, -1), lw['w_br'], l, h)


def kernel(x, c, w_ada, b_ada, w_in, a_lat_g, a_up, cmp_w1, cmp_w2, cmp_pe, lam, c_subln_g, w_br, w_o,
           w_ffn_in, w_ffn_out, ln_g, ln_b):
    B, L, D = x.shape
    depth = w_ada.shape[0]
    M = B * L
    alpha = (2 * depth) ** 0.25

    c_pad = jnp.zeros((8, D), f32).at[:B].set(c)
    mod = _ada(c_pad, w_ada, b_ada)[:, :B]
    mods = [[mod[l, :, i * D:(i + 1) * D].reshape(B, 1, D) for i in range(6)] for l in range(depth)]

    tabs128 = _rope_tables(L, HEAD_DIM)
    tabs64 = _rope_tables(L, IDX_DIM)
    cov_t = _cover_t(L)

    w_in_p, w_vt = _pack_w_in(w_in)
    pe_flat = jnp.zeros((depth, 2, 8, CMP_LEN * HEAD_DIM), f32).at[:, :, 0].set(
        cmp_pe.reshape(depth, 2, CMP_LEN * HEAD_DIM)).astype(bf16)
    a_up_vt = jnp.swapaxes(a_up[:, :, A_HEADS * HEAD_DIM:], 1, 2).astype(bf16)
    wb = dict(a_up=a_up.astype(bf16), a_up_vt=a_up_vt, w_vt=w_vt, cmp_w1=cmp_w1.astype(bf16),
              cmp_w2=cmp_w2.astype(bf16), cmp_pe=pe_flat, w_br=w_br.astype(bf16))
    w_o_b = w_o.astype(bf16)
    w_fi_b = w_ffn_in.astype(bf16)
    w_fo_b = w_ffn_out.astype(bf16)

    x2 = x.reshape(M, D)
    u = _lnmod(x2, mods[0][1], mods[0][0], L)
    for l in range(depth):
        lam_init = 0.8 - 0.6 * math.exp(-0.3 * l)
        sh_a, sc_a, g_a, sh_f, sc_f, g_f = mods[l]
        lw = dict(wb, a_lat_g=a_lat_g[l], lam=lam[l], c_subln_g=c_subln_g[l])
        h = _mm(u, w_in_p, l, 512, 768, bf16)
        merged = _token_mixing(u, h, B, L, l, lw, tabs128, tabs64, cov_t, lam_init)
        x2, u = _mm_res_ln(merged, w_o_b, l, x2, g_a, ln_g[l, 0], ln_b[l, 0], sc_f, sh_f, L, alpha, True)
        f = _ffn_in(u, w_fi_b, l)
        last = l == depth - 1
        nsc, nsh = (sc_f, sh_f) if last else (mods[l + 1][1], mods[l + 1][0])
        x2, u = _mm_res_ln(f, w_fo_b, l, x2, g_f, ln_g[l, 1], ln_b[l, 1], nsc, nsh, L, alpha, not last)
    return x2.reshape(B, L, D)
```

```python
import functools
import math

import numpy as np
import jax
import jax.numpy as jnp
from jax import lax
from jax.experimental import pallas as pl
from jax.experimental.pallas import tpu as pltpu

f32 = jnp.float32
bf16 = jnp.bfloat16
i32 = jnp.int32

D_MODEL = 2048
HEAD_DIM = 128
ROPE_THETA = 500000.0
NEG = -1e30
FORCE = 1e6
A_HEADS = 8
A_LATENT = 512
IDX_HEADS = 16
IDX_DIM = 64
DSA_TOPK = 256
B_HEADS = 8
B_KV_GROUPS = 2
B_HPG = B_HEADS // B_KV_GROUPS
CMP_LEN = 32
CMP_STRIDE = 16
SLC_LEN = 64
SLC_TOPN = 16
WIN_LEN = 512
C_HEADS = 4
C_DIM = 128
BRANCH_W = A_HEADS * HEAD_DIM
N_BRANCH = 3
D_FF = int(math.ceil(8 * D_MODEL / 3 / 256)) * 256

_O_AQ = 0
_O_ALAT = _O_AQ + A_HEADS * HEAD_DIM
_O_IQ = _O_ALAT + A_LATENT
_O_IK = _O_IQ + IDX_HEADS * IDX_DIM
_O_IW = _O_IK + IDX_DIM
_O_BQ = _O_IW + IDX_HEADS
_O_BKV = _O_BQ + B_HEADS * HEAD_DIM
_O_BG = _O_BKV + 3 * 2 * B_KV_GROUPS * HEAD_DIM
_O_CQ = _O_BG + 3 * B_HEADS
_O_CK = _O_CQ + C_HEADS * 2 * C_DIM
_O_CV = _O_CK + C_HEADS * 2 * C_DIM
_O_GL = _O_CV + C_HEADS * 2 * C_DIM
_N_IN = _O_GL + N_BRANCH * D_MODEL

_P_GL = 0
_P_BK = 6144
_P_BVC = 6912
_P_AQ = 7168
_P_IQ = 8192
_P_BQ = 9216
_P_CQ = 10240
_P_CK = 11264
_P_ALAT = 12288
_P_IKW = 12800
_P_BG = 12928
_P_TOT = 13056
_PV_CV = 0
_PV_BV = 1024
_PV_TOT = 1536

_VMEM_LIMIT = 48 * 1024 * 1024
_LOG2E = 1.4426950408889634
_INF = float("inf")
_RES_LN_TM = 512
_RES_LN_SUB = 256
_DIFF_TK = 1024
_DIFF_TQ = 512
_HEAD_GROUP = 2


def _cparams(sem, vmem=_VMEM_LIMIT):
    return pltpu.CompilerParams(dimension_semantics=sem, vmem_limit_bytes=vmem)


def _sigmoid(x):
    return 1.0 / (1.0 + jnp.exp(-x))


def _dot_nt(a, b):
    return lax.dot_general(a, b, (((1,), (1,)), ((), ())), preferred_element_type=f32)


def _ada_kernel(c_ref, w_ref, b_ref, o_ref):
    c = c_ref[...]
    cs = c * _sigmoid(c)
    o_ref[0] = jnp.dot(cs, w_ref[0], preferred_element_type=f32,
                       precision=lax.Precision.HIGHEST) + b_ref[0]


def _ada(c_pad, w_ada, b_ada):
    depth, d, n = w_ada.shape
    tn = 512
    return pl.pallas_call(
        _ada_kernel,
        grid=(depth, n // tn),
        in_specs=[pl.BlockSpec((8, d), lambda l, j: (0, 0)),
                  pl.BlockSpec((1, d, tn), lambda l, j: (l, 0, j)),
                  pl.BlockSpec((1, 1, tn), lambda l, j: (l, 0, j))],
        out_specs=pl.BlockSpec((1, 8, tn), lambda l, j: (l, 0, j)),
        out_shape=jax.ShapeDtypeStruct((depth, 8, n), f32),
        compiler_params=_cparams(("parallel", "parallel")),
    )(c_pad, w_ada, b_ada.reshape(depth, 1, n))


def _ln_rows(x, eps):
    mu = jnp.mean(x, axis=-1, keepdims=True)
    d = x - mu
    var = jnp.mean(d * d, axis=-1, keepdims=True)
    return d * lax.rsqrt(var + eps)


def _lnmod_kernel(x_ref, sc_ref, sh_ref, o_ref):
    y = _ln_rows(x_ref[...], 1e-5)
    o_ref[...] = (y * (1.0 + sc_ref[0]) + sh_ref[0]).astype(o_ref.dtype)


def _lnmod(x2, sc, sh, L):
    m, d = x2.shape
    tm = 512
    per_b = L // tm
    return pl.pallas_call(
        _lnmod_kernel,
        grid=(m // tm,),
        in_specs=[pl.BlockSpec((tm, d), lambda i: (i, 0)),
                  pl.BlockSpec((1, 1, d), lambda i: (i // per_b, 0, 0)),
                  pl.BlockSpec((1, 1, d), lambda i: (i // per_b, 0, 0))],
        out_specs=pl.BlockSpec((tm, d), lambda i: (i, 0)),
        out_shape=jax.ShapeDtypeStruct((m, d), bf16),
        compiler_params=_cparams(("parallel",)),
    )(x2, sc, sh)


def _mm_kernel(a_ref, w_ref, o_ref):
    o_ref[...] = jnp.dot(a_ref[...], w_ref[...], preferred_element_type=f32).astype(o_ref.dtype)


def _mm(a, w, l, tm, tn, out_dtype, col0=0, n=None):
    m, k = a.shape
    n = w.shape[2] if n is None else n
    cb = col0 // tn
    return pl.pallas_call(
        _mm_kernel,
        grid=(n // tn, m // tm),
        in_specs=[pl.BlockSpec((tm, k), lambda j, i: (i, 0)),
                  pl.BlockSpec((None, k, tn), lambda j, i: (l, 0, cb + j))],
        out_specs=pl.BlockSpec((tm, tn), lambda j, i: (i, j)),
        out_shape=jax.ShapeDtypeStruct((m, n), out_dtype),
        compiler_params=_cparams(("parallel", "parallel")),
    )(a, w)


def _proj_t_kernel(w_ref, a_ref, o_ref):
    n, d = o_ref.shape[1], o_ref.shape[3]
    res = _dot_nt(w_ref[...], a_ref[...])
    for i in range(n):
        o_ref[0, i, 0] = res[i * d:(i + 1) * d, :].astype(o_ref.dtype)


def _proj_t(a, w_t, l, row0, n, d, tk, B, L):
    k = a.shape[1]
    rows = n * d
    nc = L // tk
    return pl.pallas_call(
        _proj_t_kernel,
        grid=(B, nc),
        in_specs=[pl.BlockSpec((None, rows, k), lambda b, c: (l, row0 // rows, 0)),
                  pl.BlockSpec((tk, k), lambda b, c: (b * nc + c, 0))],
        out_specs=pl.BlockSpec((1, n, 1, d, tk), lambda b, c: (b, 0, c, 0, 0)),
        out_shape=jax.ShapeDtypeStruct((B, n, nc, d, tk), bf16),
        compiler_params=_cparams(("parallel", "parallel")),
    )(w_t, a)


def _rope_heads(x, c, s1, s2, half):
    return x * c + pltpu.roll(x, half, 1) * s1 + pltpu.roll(x, 128 - half, 1) * s2


def _mm_rope_kernel(a_ref, w_ref, c_ref, s1_ref, s2_ref, o_ref):
    acc = jnp.dot(a_ref[...], w_ref[...], preferred_element_type=f32)
    c, s1, s2 = c_ref[...], s1_ref[...], s2_ref[...]
    for h in range(acc.shape[1] // 128):
        sl = slice(h * 128, (h + 1) * 128)
        o_ref[:, sl] = _rope_heads(acc[:, sl], c, s1, s2, 16).astype(o_ref.dtype)


def _mm_rope(a, w, l, n, tabs, L, tm, tn):
    m, k = a.shape
    per_b = L // tm
    tspec = pl.BlockSpec((tm, 128), lambda j, i: (i % per_b, 0))
    return pl.pallas_call(
        _mm_rope_kernel,
        grid=(n // tn, m // tm),
        in_specs=[pl.BlockSpec((tm, k), lambda j, i: (i, 0)),
                  pl.BlockSpec((None, k, tn), lambda j, i: (l, 0, j)),
                  tspec, tspec, tspec],
        out_specs=pl.BlockSpec((tm, tn), lambda j, i: (i, j)),
        out_shape=jax.ShapeDtypeStruct((m, n), bf16),
        compiler_params=_cparams(("parallel", "parallel")),
    )(a, w, *tabs)


def _prep_kernel(aq_ref, iq_ref, bq_ref, cq_ref, ck_ref, bk_ref, alat_ref, ikw_ref, bg_ref, alg_ref,
                 c_ref, s1_ref, s2_ref, c6_ref, s16_ref, s26_ref,
                 aq_o, iq_o, bq_o, cq_o, ck_o, bk_o, alat_o, ikw_o, g_o):
    c, s1, s2 = c_ref[...], s1_ref[...], s2_ref[...]
    c6, s16, s26 = c6_ref[...], s16_ref[...], s26_ref[...]
    scale = HEAD_DIM ** -0.5 * _LOG2E

    def rope_all(src, dst, mult, tabs, half):
        for h in range(src.shape[1] // 128):
            sl = slice(h * 128, (h + 1) * 128)
            y = _rope_heads(src[:, sl].astype(f32), *tabs, half)
            if mult != 1.0:
                y = y * mult
            dst[:, sl] = y.astype(dst.dtype)

    rope_all(aq_ref, aq_o, scale, (c, s1, s2), 16)
    rope_all(bq_ref, bq_o, scale, (c, s1, s2), 16)
    rope_all(cq_ref, cq_o, C_DIM ** -0.5 * _LOG2E, (c, s1, s2), 16)
    rope_all(ck_ref, ck_o, 1.0, (c, s1, s2), 16)
    rope_all(bk_ref, bk_o, 1.0, (c, s1, s2), 16)
    rope_all(iq_ref, iq_o, 1.0, (c6, s16, s26), 8)

    a = alat_ref[...].astype(f32)
    ms = jnp.mean(a * a, axis=-1, keepdims=True)
    alat_o[...] = (a * lax.rsqrt(ms + 1e-6) * alg_ref[...]).astype(alat_o.dtype)

    x = ikw_ref[...].astype(f32)
    lane = lax.broadcasted_iota(i32, x.shape, 1)
    isk = lane < IDX_DIM
    mu = jnp.sum(jnp.where(isk, x, 0.0), axis=-1, keepdims=True) * (1.0 / IDX_DIM)
    d = jnp.where(isk, x - mu, 0.0)
    var = jnp.sum(d * d, axis=-1, keepdims=True) * (1.0 / IDX_DIM)
    y = d * lax.rsqrt(var + 1e-5)
    yr = _rope_heads(y, c6, s16, s26, 8)
    ikw_o[...] = jnp.where(isk, yr, x * (IDX_DIM ** -0.5 * IDX_HEADS ** -0.5))

    g_o[...] = _sigmoid(bg_ref[...].astype(f32))


def _prep(h, a_lat_g, tabs128, tabs64, L):
    m = h.shape[0]
    tm = 256
    per_b = L // tm

    def hs(width, off):
        return pl.BlockSpec((tm, width), lambda i, _o=off // width: (i, _o))

    tspec = pl.BlockSpec((tm, 128), lambda i: (i % per_b, 0))

    def os(width):
        return pl.BlockSpec((tm, width), lambda i: (i, 0))

    outs = [(1024, bf16)] * 5 + [(768, bf16), (512, bf16), (128, f32), (128, f32)]
    return pl.pallas_call(
        _prep_kernel,
        grid=(m // tm,),
        in_specs=[hs(1024, _P_AQ), hs(1024, _P_IQ), hs(1024, _P_BQ), hs(1024, _P_CQ), hs(1024, _P_CK),
                  hs(768, _P_BK), hs(512, _P_ALAT), hs(128, _P_IKW), hs(128, _P_BG),
                  pl.BlockSpec((1, A_LATENT), lambda i: (0, 0))] + [tspec] * 6,
        out_specs=[os(w) for w, _ in outs],
        out_shape=[jax.ShapeDtypeStruct((m, w), dt) for w, dt in outs],
        compiler_params=_cparams(("parallel",)),
    )(h, h, h, h, h, h, h, h, h, a_lat_g.reshape(1, A_LATENT), *tabs128, *tabs64)


def _f32_order_key(x):
    b = int(np.float32(x).view(np.int32))
    return b ^ ((b >> 31) & 0x7FFFFFFF)


_KEY_NEG = _f32_order_key(NEG)


def _dsa_mask_kernel(ik_ref, iq_ref, iw_ref, o_ref, key_ref, qp_ref, j_ref, *, L, tq, ksel):
    j = pl.program_id(1)
    ck = 512
    nck = L // ck
    nc = ((j + 1) * tq + ck - 1) // ck
    n_out_i = L - nc * ck
    n_out = n_out_i.astype(f32)
    t_row = j * tq + lax.broadcasted_iota(i32, (1, tq), 1)

    for hp in range(IDX_HEADS // 2):
        for e in range(2):
            h = 2 * hp + e
            qp_ref[hp, e * tq:(e + 1) * tq, :] = iq_ref[0, :, h * IDX_DIM:(h + 1) * IDX_DIM]

    def score_chunk(c, carry):
        off = pl.multiple_of(c * ck, ck)
        ikc = ik_ref[0, pl.ds(off, ck), :]
        acc = jnp.zeros((ck, tq), f32)
        for hp in range(IDX_HEADS // 2):
            s2 = _dot_nt(ikc, qp_ref[hp])
            acc = acc + jnp.maximum(s2[:, :tq], 0.0) * iw_ref[0, 2 * hp:2 * hp + 1, :]
            acc = acc + jnp.maximum(s2[:, tq:], 0.0) * iw_ref[0, 2 * hp + 1:2 * hp + 2, :]
        acc = jnp.where(acc == 0.0, 0.0, acc)
        sidx = off + lax.broadcasted_iota(i32, (ck, tq), 0)
        key_ref[pl.ds(off, ck), :] = jnp.where(sidx <= t_row, acc, NEG)
        return carry

    lax.fori_loop(0, nc, score_chunk, 0)

    def as_f32(key):
        return lax.bitcast_convert_type(key ^ ((key >> 31) & 0x7FFFFFFF), f32)

    def count(pred_fn):
        def body(c, cnt):
            off = pl.multiple_of(c * ck, ck)
            k = key_ref[pl.ds(off, ck), :]
            sidx = off + lax.broadcasted_iota(i32, (ck, tq), 0)
            return cnt + jnp.sum(pred_fn(k, sidx).reshape(ck // 64, 64, tq), axis=0)
        part = lax.fori_loop(0, nc, body, jnp.zeros((64, tq), f32))
        return jnp.sum(part, axis=0, keepdims=True)

    kf = float(ksel)

    def bit_body(i, carry):
        thr, cnt_thr = carry
        cand = thr + lax.shift_left(jnp.int32(1), 31 - i)
        cand_f = as_f32(cand)
        cnt = count(lambda k, s: jnp.where(k >= cand_f, 1.0, 0.0)) + jnp.where(_KEY_NEG >= cand, n_out, 0.0)
        ok = cnt >= kf
        return jnp.where(ok, cand, thr), jnp.where(ok, cnt, cnt_thr)

    thr_key, cnt_ge = lax.fori_loop(0, 32, bit_body, (jnp.full((1, tq), -2 ** 31, i32),
                                                       jnp.full((1, tq), float(L), f32)))
    thr = as_f32(thr_key)
    cnt_gt = count(lambda k, s: jnp.where(k > thr, 1.0, 0.0)) + jnp.where(_KEY_NEG > thr_key, n_out, 0.0)
    need = kf - cnt_gt

    j_ref[...] = jnp.full((1, tq), L, i32)

    @pl.when(jnp.max(cnt_ge) > kf)
    def _():
        nbits = L.bit_length() - 1

        def jbit(i, cur):
            cand = cur | lax.shift_left(jnp.int32(1), nbits - 1 - i)
            f = count(lambda k, s: jnp.where(k == thr, jnp.where(s < cand, 1.0, 0.0), 0.0))
            f = f + jnp.where(thr_key == _KEY_NEG, jnp.clip(cand - nc * ck, 0, n_out_i).astype(f32), 0.0)
            return jnp.where(f < need, cand, cur)

        j_ref[...] = lax.fori_loop(0, nbits, jbit, jnp.zeros((1, tq), i32))

    jlast = j_ref[...]

    def write(c, carry):
        off = pl.multiple_of(c * ck, ck)
        k = key_ref[pl.ds(off, ck), :]
        sidx = off + lax.broadcasted_iota(i32, (ck, tq), 0)
        sel = jnp.where(k > thr, _INF, jnp.where(k == thr, jnp.where(sidx <= jlast, _INF, NEG), NEG))
        o_ref[0, pl.ds(off, ck), :] = jnp.where(sidx <= t_row, sel, NEG)
        return carry

    lax.fori_loop(0, nc, write, 0)

    def write_rest(c, carry):
        off = pl.multiple_of(c * ck, ck)
        o_ref[0, pl.ds(off, ck), :] = jnp.full((ck, tq), NEG, f32)
        return carry

    lax.fori_loop(nc, nck, write_rest, 0)


def _dsa_mask(ik, iq, iw_t):
    B, L, _ = iq.shape
    tq = 128
    ksel = min(DSA_TOPK, L // 4)
    return pl.pallas_call(
        functools.partial(_dsa_mask_kernel, L=L, tq=tq, ksel=ksel),
        grid=(B, L // tq),
        in_specs=[pl.BlockSpec((1, L, IDX_DIM), lambda b, j: (b, 0, 0)),
                  pl.BlockSpec((1, tq, IDX_HEADS * IDX_DIM), lambda b, j: (b, j, 0)),
                  pl.BlockSpec((1, IDX_HEADS, tq), lambda b, j: (b, 0, j))],
        out_specs=pl.BlockSpec((1, L, tq), lambda b, j: (b, 0, j)),
        out_shape=jax.ShapeDtypeStruct((B, L, L), f32),
        scratch_shapes=[pltpu.VMEM((L, tq), f32), pltpu.VMEM((IDX_HEADS // 2, 2 * tq, IDX_DIM), bf16),
                        pltpu.VMEM((1, tq), i32)],
        compiler_params=_cparams(("parallel", "parallel")),
    )(ik, iq, iw_t)


def _col_reduce(x, op):
    r, c = x.shape
    if r > 64:
        x = op(x.reshape(r // 64, 64, c), axis=0)
    return op(x, axis=0, keepdims=True)


def _flash_group(ss, cap, v_ts, m_refs, l_refs, acc_refs):
    n = len(ss)
    if cap is not None:
        ss = [jnp.minimum(s, cap) for s in ss]
    m_prev = [r[...] for r in m_refs]
    m_new = [jnp.maximum(m_prev[i], _col_reduce(ss[i], jnp.max)) for i in range(n)]
    alpha = [jnp.exp2(m_prev[i] - m_new[i]) for i in range(n)]
    ps = [jnp.exp2(ss[i] - m_new[i]) for i in range(n)]
    l_new = [alpha[i] * l_refs[i][...] + _col_reduce(ps[i], jnp.sum) for i in range(n)]
    pv = [jnp.dot(v_ts[i], ps[i].astype(bf16), preferred_element_type=f32) for i in range(n)]
    for i in range(n):
        acc_new = acc_refs[i][...] * alpha[i] + pv[i]
        if cap is not None:
            dead = m_new[i] <= NEG
            l_new[i] = jnp.where(dead, 0.0, l_new[i])
            acc_new = jnp.where(dead, 0.0, acc_new)
        l_refs[i][...] = l_new[i]
        acc_refs[i][...] = acc_new
        m_refs[i][...] = m_new[i]


def _flash_heads(score_fns, cap, v_ts, m_ref, l_ref, acc_ref, group):
    n = len(score_fns)
    groups = [list(range(g, min(g + group, n))) for g in range(0, n, group)]
    nxt = [score_fns[h]() for h in groups[0]]
    for gi, hs in enumerate(groups):
        ss = nxt
        if gi + 1 < len(groups):
            nxt = [score_fns[h]() for h in groups[gi + 1]]
        _flash_group(ss, cap, [v_ts[h] for h in hs], [m_ref.at[h] for h in hs],
                     [l_ref.at[h] for h in hs], [acc_ref.at[h] for h in hs])


def _flash_init(m_ref, l_ref, acc_ref):
    m_ref[...] = jnp.full(m_ref.shape, NEG, f32)
    l_ref[...] = jnp.zeros(l_ref.shape, f32)
    acc_ref[...] = jnp.zeros(acc_ref.shape, f32)


def _dsa_attn_kernel(q_ref, k_ref, vt_ref, mask_ref, o_ref, m_ref, l_ref, acc_ref, *, tq, tk, nkc):
    j = pl.program_id(1)
    c = pl.program_id(2)

    @pl.when(c == 0)
    def _():
        _flash_init(m_ref, l_ref, acc_ref)

    @pl.when(c * tk < (j + 1) * tq)
    def _():
        cap = mask_ref[0]

        def scores(h):
            sl = slice(h * HEAD_DIM, (h + 1) * HEAD_DIM)
            return lambda: _dot_nt(k_ref[0, :, sl], q_ref[0, :, sl])

        v_ts = [vt_ref[0, h, 0] for h in range(A_HEADS)]
        _flash_heads([scores(h) for h in range(A_HEADS)], cap, v_ts, m_ref, l_ref, acc_ref, _HEAD_GROUP)

    @pl.when(c == nkc - 1)
    def _():
        for h in range(A_HEADS):
            o = acc_ref[h] / jnp.maximum(l_ref[h], 1e-30)
            o_ref[0, :, h * HEAD_DIM:(h + 1) * HEAD_DIM] = o.T.astype(o_ref.dtype)


def _dsa_attn(q, k, v_t, mask_t):
    B, L, W = q.shape
    tq, tk = 256, 512
    nkc = L // tk

    def last(j):
        return ((j + 1) * tq - 1) // tk

    return pl.pallas_call(
        functools.partial(_dsa_attn_kernel, tq=tq, tk=tk, nkc=nkc),
        grid=(B, L // tq, nkc),
        in_specs=[pl.BlockSpec((1, tq, W), lambda b, j, c: (b, j, 0)),
                  pl.BlockSpec((1, tk, W), lambda b, j, c: (b, jnp.minimum(c, last(j)), 0)),
                  pl.BlockSpec((1, A_HEADS, 1, HEAD_DIM, tk),
                               lambda b, j, c: (b, 0, jnp.minimum(c, last(j)), 0, 0)),
                  pl.BlockSpec((1, tk, tq), lambda b, j, c: (b, jnp.minimum(c, last(j)), j))],
        out_specs=pl.BlockSpec((1, tq, W), lambda b, j, c: (b, j, 0)),
        out_shape=jax.ShapeDtypeStruct((B, L, W), bf16),
        scratch_shapes=[pltpu.VMEM((A_HEADS, 1, tq), f32), pltpu.VMEM((A_HEADS, 1, tq), f32),
                        pltpu.VMEM((A_HEADS, HEAD_DIM, tq), f32)],
        compiler_params=_cparams(("parallel", "parallel", "arbitrary")),
    )(q, k, v_t, mask_t)


def _nsa_cmp_kernel(x_ref, w1_ref, w2_ref, pe_ref, o_ref):
    x = x_ref[0, 0, 0]
    w1 = w1_ref[0]
    half = CMP_STRIDE * HEAD_DIM
    a = jnp.dot(x, w1[:half], preferred_element_type=f32)
    b = jnp.dot(x, w1[half:], preferred_element_type=f32)
    pe = jnp.dot(pe_ref[0], w1, preferred_element_type=f32)[0:1]
    n = a.shape[0]
    pre = a + pltpu.roll(b, n - 1, 0) + pe
    act = pre * _sigmoid(pre)
    o_ref[0, 0, 0] = jnp.dot(act.astype(bf16), w2_ref[0], preferred_element_type=f32).astype(o_ref.dtype)


def _nsa_compress(x, w1, w2, pe, l):
    B, _, G, n, wd = x.shape
    return pl.pallas_call(
        _nsa_cmp_kernel,
        grid=(B, 2, G),
        in_specs=[pl.BlockSpec((1, 1, 1, n, wd), lambda b, t, g: (b, t, g, 0, 0)),
                  pl.BlockSpec((None, 1, CMP_LEN * HEAD_DIM, HEAD_DIM), lambda b, t, g: (l, t, 0, 0)),
                  pl.BlockSpec((None, 1, HEAD_DIM, HEAD_DIM), lambda b, t, g: (l, t, 0, 0)),
                  pl.BlockSpec((None, 1, 8, CMP_LEN * HEAD_DIM), lambda b, t, g: (l, t, 0, 0))],
        out_specs=pl.BlockSpec((1, 1, 1, n, HEAD_DIM), lambda b, t, g: (b, t, g, 0, 0)),
        out_shape=jax.ShapeDtypeStruct((B, 2, G, n, HEAD_DIM), bf16),
        compiler_params=_cparams(("parallel", "parallel", "parallel")),
    )(x, w1, w2, pe)


def _nsa_attn_kernel(q_ref, kc_ref, vct_ref, ks_ref, vst_ref, kw_ref, vwt_ref, g_ref, cov_ref, o_ref,
                     m_ref, l_ref, acc_ref, out_ref, sel_ref, *, L, tq, tk, n_sel):
    j = pl.program_id(2)
    t_row = j * tq + lax.broadcasted_iota(i32, (1, tq), 1)
    n_cmp = kc_ref.shape[2]
    n_slc = L // SLC_LEN
    qs = [q_ref[0, :, hh * HEAD_DIM:(hh + 1) * HEAD_DIM] for hh in range(B_HPG)]

    def gate(hh, i):
        return g_ref[0, 0, hh * 3 + i:hh * 3 + i + 1, :]

    kc = kc_ref[0, 0]
    vct = vct_ref[0, 0]
    cend = lax.broadcasted_iota(i32, (n_cmp, tq), 0) * CMP_STRIDE + (CMP_LEN - 1)
    mc = cend <= t_row
    psum = jnp.zeros((n_cmp, tq), f32)
    for hh in range(B_HPG):
        s = jnp.where(mc, _dot_nt(kc, qs[hh]), NEG)
        mx = _col_reduce(s, jnp.max)
        p = jnp.where(mc, jnp.exp2(s - mx), 0.0)
        p = p / jnp.maximum(_col_reduce(p, jnp.sum), 1e-30)
        psum = psum + p
        out_ref[hh] = gate(hh, 0) * jnp.dot(vct, p.astype(bf16), preferred_element_type=f32)
    imp = jnp.dot(cov_ref[...], psum, preferred_element_type=f32, precision=lax.Precision.HIGHEST)
    blk = lax.broadcasted_iota(i32, (n_slc, tq), 0)
    cur = t_row >> (SLC_LEN.bit_length() - 1)
    forced = (blk == 0) | (blk == cur) | (blk == cur - 1)
    imp = jnp.where(forced, FORCE, jnp.where(blk <= cur, imp, NEG))
    rank = jnp.zeros((n_slc, tq), f32)
    for r in range(n_slc):
        row = imp[r:r + 1, :]
        rank = rank + jnp.where(row > imp, 1.0, jnp.where(row == imp, jnp.where(blk > r, 1.0, 0.0), 0.0))
    sel_ref[...] = jnp.where(rank < float(n_sel), _INF, NEG)

    def finish(i):
        for hh in range(B_HPG):
            o = acc_ref[hh] / jnp.maximum(l_ref[hh], 1e-30)
            out_ref[hh] = out_ref[hh] + gate(hh, i) * o

    def run_branch(k_ref, vt_ref, c_lo, c_hi, mask_fn):
        for hh in range(B_HPG):
            _flash_init(m_ref.at[hh], l_ref.at[hh], acc_ref.at[hh])

        def body(c, carry):
            off = pl.multiple_of(c * tk, tk)
            kch = k_ref[0, pl.ds(off, tk), :]
            vch = vt_ref[0, 0, c]
            sidx = off + lax.broadcasted_iota(i32, (tk, tq), 0)
            cap = mask_fn(c, sidx)
            fns = [functools.partial(_dot_nt, kch, qs[hh]) for hh in range(B_HPG)]
            _flash_heads(fns, cap, [vch] * B_HPG, m_ref, l_ref, acc_ref, _HEAD_GROUP)
            return carry

        lax.fori_loop(c_lo, c_hi, body, 0)

    bpc = tk // SLC_LEN

    def slc_mask(c, sidx):
        rows = [jnp.broadcast_to(sel_ref[pl.ds(c * bpc + b, 1), :], (SLC_LEN, tq)) for b in range(bpc)]
        return jnp.where(sidx <= t_row, jnp.concatenate(rows, axis=0), NEG)

    c_hi = ((j + 1) * tq + tk - 1) // tk
    run_branch(ks_ref, vst_ref, 0, c_hi, slc_mask)
    finish(1)

    def win_mask(c, sidx):
        d = t_row - sidx
        return jnp.where(d >= 0, jnp.where(d < WIN_LEN, _INF, NEG), NEG)

    c_lo = jnp.maximum(j * tq - (WIN_LEN - 1), 0) // tk
    run_branch(kw_ref, vwt_ref, c_lo, c_hi, win_mask)
    finish(2)

    for hh in range(B_HPG):
        o_ref[0, :, hh * HEAD_DIM:(hh + 1) * HEAD_DIM] = out_ref[hh].T.astype(o_ref.dtype)


def _nsa_attn(bq, kc, vc_t, bk, bv_tc, gates_t, cov_t):
    B, L, _ = bq.shape
    G = B_KV_GROUPS
    tq, tk = 256, 512
    n_cmp = kc.shape[2]
    n_slc = L // SLC_LEN
    n_sel = min(SLC_TOPN, n_slc)
    gw = B_HPG * HEAD_DIM
    nc = L // tk
    return pl.pallas_call(
        functools.partial(_nsa_attn_kernel, L=L, tq=tq, tk=tk, n_sel=n_sel),
        grid=(B, G, L // tq),
        in_specs=[pl.BlockSpec((1, tq, gw), lambda b, g, j: (b, j, g)),
                  pl.BlockSpec((1, 1, n_cmp, HEAD_DIM), lambda b, g, j: (b, g, 0, 0)),
                  pl.BlockSpec((1, 1, HEAD_DIM, n_cmp), lambda b, g, j: (b, g, 0, 0)),
                  pl.BlockSpec((1, L, HEAD_DIM), lambda b, g, j: (b, 0, 2 + g)),
                  pl.BlockSpec((1, 1, nc, HEAD_DIM, tk), lambda b, g, j: (b, g, 0, 0, 0)),
                  pl.BlockSpec((1, L, HEAD_DIM), lambda b, g, j: (b, 0, 4 + g)),
                  pl.BlockSpec((1, 1, nc, HEAD_DIM, tk), lambda b, g, j: (b, B_KV_GROUPS + g, 0, 0, 0)),
                  pl.BlockSpec((1, 1, 3 * B_HPG, tq), lambda b, g, j: (b, g, 0, j)),
                  pl.BlockSpec((n_slc, n_cmp), lambda b, g, j: (0, 0))],
        out_specs=pl.BlockSpec((1, tq, gw), lambda b, g, j: (b, j, g)),
        out_shape=jax.ShapeDtypeStruct((B, L, B_HEADS * HEAD_DIM), bf16),
        scratch_shapes=[pltpu.VMEM((B_HPG, 1, tq), f32), pltpu.VMEM((B_HPG, 1, tq), f32),
                        pltpu.VMEM((B_HPG, HEAD_DIM, tq), f32), pltpu.VMEM((B_HPG, HEAD_DIM, tq), f32),
                        pltpu.VMEM((n_slc, tq), f32)],
        compiler_params=_cparams(("parallel", "parallel", "parallel")),
    )(bq, kc, vc_t, bk, bv_tc, bk, bv_tc, gates_t, cov_t)


def _diff_attn_kernel(q_ref, k_ref, vt_ref, lam_ref, g_ref, o_ref, m_ref, l_ref, acc_ref,
                      *, tq, tk, lam_init):
    j = pl.program_id(2)
    t_row = j * tq + lax.broadcasted_iota(i32, (1, tq), 1)
    qs = [q_ref[0, :, mi * C_DIM:(mi + 1) * C_DIM] for mi in range(2)]
    for mi in range(2):
        _flash_init(m_ref.at[mi], l_ref.at[mi], acc_ref.at[mi])

    def chunk(c, masked):
        off = pl.multiple_of(c * tk, tk)
        vch = vt_ref[0, 0, c]
        cap = None
        if masked:
            sidx = off + lax.broadcasted_iota(i32, (tk, tq), 0)
            cap = jnp.where(sidx <= t_row, _INF, NEG)
        ss = [_dot_nt(k_ref[0, pl.ds(off, tk), mi * C_DIM:(mi + 1) * C_DIM], qs[mi]) for mi in range(2)]
        _flash_group(ss, cap, [vch] * 2, [m_ref.at[mi] for mi in range(2)],
                     [l_ref.at[mi] for mi in range(2)], [acc_ref.at[mi] for mi in range(2)])

    def body(c, carry):
        chunk(c, False)
        return carry

    c_diag = (j * tq) // tk
    lax.fori_loop(0, c_diag, body, 0)
    chunk(c_diag, True)

    lam = lam_ref[...]
    lam_val = (jnp.exp(jnp.sum(lam[0:1] * lam[1:2], axis=1, keepdims=True))
               - jnp.exp(jnp.sum(lam[2:3] * lam[3:4], axis=1, keepdims=True)) + lam_init)
    o = (acc_ref[0] / jnp.maximum(l_ref[0], 1e-30)
         - lam_val * (acc_ref[1] / jnp.maximum(l_ref[1], 1e-30)))
    ms = jnp.mean(o * o, axis=0, keepdims=True)
    y = o * lax.rsqrt(ms + 1e-6) * g_ref[...] * (1.0 - lam_init)
    o_ref[0] = y.T.astype(o_ref.dtype)


def _diff_attn(cq, ck, cv_tc, lam, g_col, lam_init):
    B, L, _ = cq.shape
    tq, tk = _DIFF_TQ, cv_tc.shape[-1]
    assert tk % tq == 0
    hw = 2 * C_DIM
    nc = L // tk
    return pl.pallas_call(
        functools.partial(_diff_attn_kernel, tq=tq, tk=tk, lam_init=lam_init),
        grid=(B, C_HEADS, L // tq),
        in_specs=[pl.BlockSpec((1, tq, hw), lambda b, h, j: (b, j, h)),
                  pl.BlockSpec((1, L, hw), lambda b, h, j: (b, 0, h)),
                  pl.BlockSpec((1, 1, nc, hw, tk), lambda b, h, j: (b, h, 0, 0, 0)),
                  pl.BlockSpec((4, C_DIM), lambda b, h, j: (0, 0)),
                  pl.BlockSpec((hw, 1), lambda b, h, j: (0, 0))],
        out_specs=pl.BlockSpec((1, tq, hw), lambda b, h, j: (b, j, h)),
        out_shape=jax.ShapeDtypeStruct((B, L, C_HEADS * hw), bf16),
        scratch_shapes=[pltpu.VMEM((2, 1, tq), f32), pltpu.VMEM((2, 1, tq), f32),
                        pltpu.VMEM((2, hw, tq), f32)],
        compiler_params=_cparams(("parallel", "parallel", "parallel")),
    )(cq, ck, cv_tc, lam, g_col)


def _merge_kernel(ya_ref, yb_ref, yc_ref, w_ref, g0_ref, g1_ref, g2_ref, o_ref):
    acc = None
    for r, (y_ref, g_ref) in enumerate(((ya_ref, g0_ref), (yb_ref, g1_ref), (yc_ref, g2_ref))):
        br = jnp.dot(y_ref[...], w_ref[r], preferred_element_type=f32)
        t = _sigmoid(g_ref[...].astype(f32)) * br
        acc = t if acc is None else acc + t
    o_ref[...] = acc.astype(o_ref.dtype)


def _merge(ya, yb, yc, w_br, l, h):
    m, kw = ya.shape
    tm, tn = 512, 512
    npb = D_MODEL // tn
    yspec = pl.BlockSpec((tm, kw), lambda j, i: (i, 0))

    def gspec(r):
        return pl.BlockSpec((tm, tn), lambda j, i, _r=r: (i, _r * npb + j))

    return pl.pallas_call(
        _merge_kernel,
        grid=(npb, m // tm),
        in_specs=[yspec, yspec, yspec,
                  pl.BlockSpec((None, N_BRANCH, kw, tn), lambda j, i: (l, 0, 0, j)),
                  gspec(0), gspec(1), gspec(2)],
        out_specs=pl.BlockSpec((tm, tn), lambda j, i: (i, j)),
        out_shape=jax.ShapeDtypeStruct((m, D_MODEL), bf16),
        compiler_params=_cparams(("parallel", "parallel")),
    )(ya, yb, yc, w_br, h, h, h)


def _mm_res_ln_kernel(a_ref, w_ref, x_ref, gate_ref, lg_ref, lb_ref, sc_ref, sh_ref, xo_ref, *u_refs,
                      alpha, sub):
    tm = a_ref.shape[0]
    for r in range(tm // sub):
        rows = slice(r * sub, (r + 1) * sub)
        y = jnp.dot(a_ref[rows, :], w_ref[...], preferred_element_type=f32)
        z = alpha * x_ref[rows, :] + gate_ref[0] * y
        xn = _ln_rows(z, 1e-5) * lg_ref[...] + lb_ref[...]
        xo_ref[rows, :] = xn
        if u_refs:
            u_refs[0][rows, :] = (_ln_rows(xn, 1e-5) * (1.0 + sc_ref[0]) + sh_ref[0]).astype(bf16)


def _mm_res_ln(a, w, l, x2, gate, ln_g, ln_b, sc, sh, L, alpha, emit_u):
    m, kdim = a.shape
    d = w.shape[2]
    tm = _RES_LN_TM if kdim * d * 2 <= 16 * 1024 * 1024 else _RES_LN_SUB
    per_b = L // tm
    bspec = pl.BlockSpec((1, 1, d), lambda i: (i // per_b, 0, 0))
    vspec = pl.BlockSpec((1, d), lambda i: (0, 0))
    rspec = pl.BlockSpec((tm, d), lambda i: (i, 0))
    out_shape = [jax.ShapeDtypeStruct((m, d), f32)]
    out_specs = [rspec]
    if emit_u:
        out_shape.append(jax.ShapeDtypeStruct((m, d), bf16))
        out_specs.append(rspec)
    res = pl.pallas_call(
        functools.partial(_mm_res_ln_kernel, alpha=alpha, sub=_RES_LN_SUB),
        grid=(m // tm,),
        in_specs=[pl.BlockSpec((tm, kdim), lambda i: (i, 0)),
                  pl.BlockSpec((None, kdim, d), lambda i: (l, 0, 0), pipeline_mode=pl.Buffered(1)),
                  rspec, bspec, vspec, vspec, bspec, bspec],
        out_specs=out_specs,
        out_shape=out_shape,
        compiler_params=_cparams(("parallel",), 60 * 1024 * 1024),
    )(a, w, x2, gate, ln_g.reshape(1, d), ln_b.reshape(1, d), sc, sh)
    return res if emit_u else (res[0], None)


def _ffn_in_kernel(a_ref, wg_ref, wu_ref, o_ref):
    a = a_ref[...]
    g = jnp.dot(a, wg_ref[...], preferred_element_type=f32)
    u = jnp.dot(a, wu_ref[...], preferred_element_type=f32)
    o_ref[...] = (g * _sigmoid(g) * u).astype(o_ref.dtype)


def _ffn_in(a, w, l):
    m, k = a.shape
    tm, tn = 512, 512
    nb = D_FF // tn
    return pl.pallas_call(
        _ffn_in_kernel,
        grid=(nb, m // tm),
        in_specs=[pl.BlockSpec((tm, k), lambda j, i: (i, 0)),
                  pl.BlockSpec((None, k, tn), lambda j, i: (l, 0, j)),
                  pl.BlockSpec((None, k, tn), lambda j, i: (l, 0, nb + j))],
        out_specs=pl.BlockSpec((tm, tn), lambda j, i: (i, j)),
        out_shape=jax.ShapeDtypeStruct((m, D_FF), bf16),
        compiler_params=_cparams(("parallel", "parallel")),
    )(a, w, w)


def _rope_tables(L, d):
    r = d // 4
    half = r // 2
    inv = ROPE_THETA ** (-(jnp.arange(half, dtype=f32) * 2.0) / r)
    ang = jnp.arange(L).astype(f32)[:, None] * inv[None, :]
    cos, sin = jnp.cos(ang), jnp.sin(ang)
    z = jnp.zeros((L, d - r), f32)
    zh = jnp.zeros((L, half), f32)
    c = jnp.concatenate([cos, cos, jnp.ones((L, d - r), f32)], axis=1)
    s1 = jnp.concatenate([zh, sin, z], axis=1)
    s2 = jnp.concatenate([-sin, zh, z], axis=1)
    rep = 128 // d
    return tuple(jnp.tile(t, (1, rep)) for t in (c, s1, s2))


def _pack_w_in_kernel(w_ref, o_ref, ov_ref):
    def cp(src, width, dst):
        for o in range(0, width, 1024):
            wd = min(1024, width - o)
            o_ref[:, dst + o:dst + o + wd] = w_ref[src + o:src + o + wd, :].T.astype(bf16)

    def bkv(i, kv, g):
        return _O_BKV + ((i * 2 + kv) * B_KV_GROUPS + g) * HEAD_DIM

    cp(_O_GL, N_BRANCH * D_MODEL, _P_GL)
    for src, dst in ((_O_AQ, _P_AQ), (_O_IQ, _P_IQ), (_O_BQ, _P_BQ), (_O_CQ, _P_CQ), (_O_CK, _P_CK)):
        cp(src, 1024, dst)
    for i in range(3):
        for g in range(B_KV_GROUPS):
            cp(bkv(i, 0, g), HEAD_DIM, _P_BK + (i * B_KV_GROUPS + g) * HEAD_DIM)
    for g in range(B_KV_GROUPS):
        cp(bkv(0, 1, g), HEAD_DIM, _P_BVC + g * HEAD_DIM)
    cp(_O_ALAT, A_LATENT, _P_ALAT)
    ov_ref[_PV_CV:_PV_CV + 1024, :] = w_ref[_O_CV:_O_CV + 1024, :].astype(bf16)
    for i in (1, 2):
        for g in range(B_KV_GROUPS):
            dst = _PV_BV + ((i - 1) * B_KV_GROUPS + g) * HEAD_DIM
            ov_ref[dst:dst + HEAD_DIM, :] = w_ref[bkv(i, 1, g):bkv(i, 1, g) + HEAD_DIM, :].astype(bf16)
    lane = lax.broadcasted_iota(i32, (w_ref.shape[1], 128), 1)
    assert _O_IW == _O_IK + IDX_DIM and _O_IK % 8 == 0 and _O_BG % 8 == 0
    blk = w_ref[_O_IK:_O_IK + 128, :].T
    o_ref[:, _P_IKW:_P_IKW + 128] = jnp.where(lane < IDX_DIM + IDX_HEADS, blk, 0.0).astype(bf16)
    blk = w_ref[_O_BG:_O_BG + 128, :].T
    o_ref[:, _P_BG:_P_BG + 128] = jnp.where(lane < 3 * B_HEADS, blk, 0.0).astype(bf16)


def _pack_w_in(w_in):
    depth, k, n = w_in.shape
    w_t = jnp.swapaxes(w_in, 1, 2)
    tc = 128
    return pl.pallas_call(
        _pack_w_in_kernel,
        grid=(depth, k // tc),
        in_specs=[pl.BlockSpec((None, n, tc), lambda l, i: (l, 0, i))],
        out_specs=[pl.BlockSpec((None, tc, _P_TOT), lambda l, i: (l, i, 0)),
                   pl.BlockSpec((None, _PV_TOT, tc), lambda l, i: (l, 0, i))],
        out_shape=[jax.ShapeDtypeStruct((depth, k, _P_TOT), bf16),
                   jax.ShapeDtypeStruct((depth, _PV_TOT, k), bf16)],
        compiler_params=_cparams(("parallel", "parallel")),
    )(w_t)


def _cover_t(L):
    n_cmp_pad = L // CMP_STRIDE
    starts = np.arange(n_cmp_pad) * CMP_STRIDE
    slc_start = np.arange(L // SLC_LEN) * SLC_LEN
    cover = ((starts[:, None] < slc_start[None, :] + SLC_LEN)
             & (starts[:, None] + CMP_LEN > slc_start[None, :])).astype(np.float32)
    n_cmp = (L - CMP_LEN) // CMP_STRIDE + 1
    cover[n_cmp:] = 0.0
    return jnp.asarray(cover.T)


def _token_mixing(u, h, B, L, l, lw, tabs128, tabs64, cov_t, lam_init):
    M = B * L
    G = B_KV_GROUPS
    aq, iq, bq, cq, ck, bk, alat_n, ikw, gates = _prep(h, lw['a_lat_g'], tabs128, tabs64, L)

    aw = A_HEADS * HEAD_DIM
    ak = _mm_rope(alat_n, lw['a_up'], l, aw, tabs128, L, 512, aw)
    av_tc = _proj_t(alat_n, lw['a_up_vt'], l, 0, A_HEADS, HEAD_DIM, 512, B, L)
    ik = ikw[:, :IDX_DIM].astype(bf16).reshape(B, L, IDX_DIM)
    iw_t = ikw[:, IDX_DIM:IDX_DIM + IDX_HEADS].reshape(B, L, IDX_HEADS).transpose(0, 2, 1)
    mask_t = _dsa_mask(ik, iq.reshape(B, L, -1), iw_t)
    ya = _dsa_attn(aq.reshape(B, L, -1), ak.reshape(B, L, -1), av_tc, mask_t)

    bvc = h[:, _P_BVC:_P_BVC + G * HEAD_DIM].reshape(B, L, G, HEAD_DIM)
    bkc = bk[:, :G * HEAD_DIM].reshape(B, L, G, HEAD_DIM)
    n_row = L // CMP_STRIDE
    xcmp = jnp.stack([bkc, bvc], axis=1)
    xcmp = xcmp.transpose(0, 1, 3, 2, 4).reshape(B, 2, G, n_row, CMP_STRIDE * HEAD_DIM)
    cmp_out = _nsa_compress(xcmp, lw['cmp_w1'], lw['cmp_w2'], lw['cmp_pe'], l)
    kc = cmp_out[:, 0]
    vc_t = cmp_out[:, 1].transpose(0, 1, 3, 2)
    bv_tc = _proj_t(u, lw['w_vt'], l, _PV_BV, 2 * G, HEAD_DIM, 512, B, L)
    gates_t = gates[:, :3 * B_HEADS].reshape(B, L, G, 3 * B_HPG).transpose(0, 2, 3, 1)
    yb = _nsa_attn(bq.reshape(B, L, -1), kc, vc_t, bk.reshape(B, L, -1), bv_tc, gates_t, cov_t)

    cv_tc = _proj_t(u, lw['w_vt'], l, _PV_CV, C_HEADS, 2 * C_DIM, _DIFF_TK, B, L)
    yc = _diff_attn(cq.reshape(B, L, -1), ck.reshape(B, L, -1), cv_tc, lw['lam'],
                    lw['c_subln_g'].reshape(2 * C_DIM, 1), lam_init)

    return _merge(ya.reshape(M, -1), yb.reshape(M, -1), yc.reshape(M, -1), lw['w_br'], l, h)


def kernel(x, c, w_ada, b_ada, w_in, a_lat_g, a_up, cmp_w1, cmp_w2, cmp_pe, lam, c_subln_g, w_br, w_o,
           w_ffn_in, w_ffn_out, ln_g, ln_b):
    B, L, D = x.shape
    depth = w_ada.shape[0]
    M = B * L
    alpha = (2 * depth) ** 0.25

    c_pad = jnp.zeros((8, D), f32).at[:B].set(c)
    mod = _ada(c_pad, w_ada, b_ada)[:, :B]
    mods = [[mod[l, :, i * D:(i + 1) * D].reshape(B, 1, D) for i in range(6)] for l in range(depth)]

    tabs128 = _rope_tables(L, HEAD_DIM)
    tabs64 = _rope_tables(L, IDX_DIM)
    cov_t = _cover_t(L)

    w_in_p, w_vt = _pack_w_in(w_in)
    pe_flat = jnp.zeros((depth, 2, 8, CMP_LEN * HEAD_DIM), f32).at[:, :, 0].set(
        cmp_pe.reshape(depth, 2, CMP_LEN * HEAD_DIM)).astype(bf16)
    a_up_vt = jnp.swapaxes(a_up[:, :, A_HEADS * HEAD_DIM:], 1, 2).astype(bf16)
    wb = dict(a_up=a_up.astype(bf16), a_up_vt=a_up_vt, w_vt=w_vt, cmp_w1=cmp_w1.astype(bf16),
              cmp_w2=cmp_w2.astype(bf16), cmp_pe=pe_flat, w_br=w_br.astype(bf16))
    w_o_b = w_o.astype(bf16)
    w_fi_b = w_ffn_in.astype(bf16)
    w_fo_b = w_ffn_out.astype(bf16)

    x2 = x.reshape(M, D)
    u = _lnmod(x2, mods[0][1], mods[0][0], L)
    for l in range(depth):
        lam_init = 0.8 - 0.6 * math.exp(-0.3 * l)
        sh_a, sc_a, g_a, sh_f, sc_f, g_f = mods[l]
        lw = dict(wb, a_lat_g=a_lat_g[l], lam=lam[l], c_subln_g=c_subln_g[l])
        h = _mm(u, w_in_p, l, 512, 768, bf16)
        merged = _token_mixing(u, h, B, L, l, lw, tabs128, tabs64, cov_t, lam_init)
        x2, u = _mm_res_ln(merged, w_o_b, l, x2, g_a, ln_g[l, 0], ln_b[l, 0], sc_f, sh_f, L, alpha, True)
        f = _ffn_in(u, w_fi_b, l)
        last = l == depth - 1
        nsc, nsh = (sc_f, sh_f) if last else (mods[l + 1][1], mods[l + 1][0])
        x2, u = _mm_res_ln(f, w_fo_b, l, x2, g_f, ln_g[l, 1], ln_b[l, 1], nsc, nsh, L, alpha, not last)
    return x2.reshape(B, L, D)
```

```python
import functools
import math

import numpy as np
import jax
import jax.numpy as jnp
from jax import lax
from jax.experimental import pallas as pl
from jax.experimental.pallas import tpu as pltpu

f32 = jnp.float32
bf16 = jnp.bfloat16
i32 = jnp.int32

D_MODEL = 2048
HEAD_DIM = 128
ROPE_THETA = 500000.0
NEG = -1e30
FORCE = 1e6
A_HEADS = 8
A_LATENT = 512
IDX_HEADS = 16
IDX_DIM = 64
DSA_TOPK = 256
B_HEADS = 8
B_KV_GROUPS = 2
B_HPG = B_HEADS // B_KV_GROUPS
CMP_LEN = 32
CMP_STRIDE = 16
SLC_LEN = 64
SLC_TOPN = 16
WIN_LEN = 512
C_HEADS = 4
C_DIM = 128
BRANCH_W = A_HEADS * HEAD_DIM
N_BRANCH = 3
D_FF = int(math.ceil(8 * D_MODEL / 3 / 256)) * 256

_O_AQ = 0
_O_ALAT = _O_AQ + A_HEADS * HEAD_DIM
_O_IQ = _O_ALAT + A_LATENT
_O_IK = _O_IQ + IDX_HEADS * IDX_DIM
_O_IW = _O_IK + IDX_DIM
_O_BQ = _O_IW + IDX_HEADS
_O_BKV = _O_BQ + B_HEADS * HEAD_DIM
_O_BG = _O_BKV + 3 * 2 * B_KV_GROUPS * HEAD_DIM
_O_CQ = _O_BG + 3 * B_HEADS
_O_CK = _O_CQ + C_HEADS * 2 * C_DIM
_O_CV = _O_CK + C_HEADS * 2 * C_DIM
_O_GL = _O_CV + C_HEADS * 2 * C_DIM
_N_IN = _O_GL + N_BRANCH * D_MODEL

_P_GL = 0
_P_BK = 6144
_P_BVC = 6912
_P_AQ = 7168
_P_IQ = 8192
_P_BQ = 9216
_P_CQ = 10240
_P_CK = 11264
_P_ALAT = 12288
_P_IKW = 12800
_P_BG = 12928
_P_TOT = 13056
_PV_CV = 0
_PV_BV = 1024
_PV_TOT = 1536

_VMEM_LIMIT = 48 * 1024 * 1024
_LOG2E = 1.4426950408889634
_INF = float("inf")
_RES_LN_TM = 512
_RES_LN_SUB = 256
_DIFF_TK = 1024
_DIFF_TQ = 512
_HEAD_GROUP = 2


def _cparams(sem, vmem=_VMEM_LIMIT):
    return pltpu.CompilerParams(dimension_semantics=sem, vmem_limit_bytes=vmem)


def _sigmoid(x):
    return 1.0 / (1.0 + jnp.exp(-x))


def _dot_nt(a, b):
    return lax.dot_general(a, b, (((1,), (1,)), ((), ())), preferred_element_type=f32)


def _ada_kernel(c_ref, w_ref, b_ref, o_ref):
    c = c_ref[...]
    cs = c * _sigmoid(c)
    o_ref[0] = jnp.dot(cs, w_ref[0], preferred_element_type=f32,
                       precision=lax.Precision.HIGHEST) + b_ref[0]


def _ada(c_pad, w_ada, b_ada):
    depth, d, n = w_ada.shape
    tn = 512
    return pl.pallas_call(
        _ada_kernel,
        grid=(depth, n // tn),
        in_specs=[pl.BlockSpec((8, d), lambda l, j: (0, 0)),
                  pl.BlockSpec((1, d, tn), lambda l, j: (l, 0, j)),
                  pl.BlockSpec((1, 1, tn), lambda l, j: (l, 0, j))],
        out_specs=pl.BlockSpec((1, 8, tn), lambda l, j: (l, 0, j)),
        out_shape=jax.ShapeDtypeStruct((depth, 8, n), f32),
        compiler_params=_cparams(("parallel", "parallel")),
    )(c_pad, w_ada, b_ada.reshape(depth, 1, n))


def _ln_rows(x, eps):
    mu = jnp.mean(x, axis=-1, keepdims=True)
    d = x - mu
    var = jnp.mean(d * d, axis=-1, keepdims=True)
    return d * lax.rsqrt(var + eps)


def _lnmod_kernel(x_ref, sc_ref, sh_ref, o_ref):
    y = _ln_rows(x_ref[...], 1e-5)
    o_ref[...] = (y * (1.0 + sc_ref[0]) + sh_ref[0]).astype(o_ref.dtype)


def _lnmod(x2, sc, sh, L):
    m, d = x2.shape
    tm = 512
    per_b = L // tm
    return pl.pallas_call(
        _lnmod_kernel,
        grid=(m // tm,),
        in_specs=[pl.BlockSpec((tm, d), lambda i: (i, 0)),
                  pl.BlockSpec((1, 1, d), lambda i: (i // per_b, 0, 0)),
                  pl.BlockSpec((1, 1, d), lambda i: (i // per_b, 0, 0))],
        out_specs=pl.BlockSpec((tm, d), lambda i: (i, 0)),
        out_shape=jax.ShapeDtypeStruct((m, d), bf16),
        compiler_params=_cparams(("parallel",)),
    )(x2, sc, sh)


def _mm_kernel(a_ref, w_ref, o_ref):
    o_ref[...] = jnp.dot(a_ref[...], w_ref[...], preferred_element_type=f32).astype(o_ref.dtype)


def _mm(a, w, l, tm, tn, out_dtype, col0=0, n=None):
    m, k = a.shape
    n = w.shape[2] if n is None else n
    cb = col0 // tn
    return pl.pallas_call(
        _mm_kernel,
        grid=(n // tn, m // tm),
        in_specs=[pl.BlockSpec((tm, k), lambda j, i: (i, 0)),
                  pl.BlockSpec((None, k, tn), lambda j, i: (l, 0, cb + j))],
        out_specs=pl.BlockSpec((tm, tn), lambda j, i: (i, j)),
        out_shape=jax.ShapeDtypeStruct((m, n), out_dtype),
        compiler_params=_cparams(("parallel", "parallel")),
    )(a, w)


def _proj_t_kernel(w_ref, a_ref, o_ref):
    n, d = o_ref.shape[1], o_ref.shape[3]
    res = _dot_nt(w_ref[...], a_ref[...])
    for i in range(n):
        o_ref[0, i, 0] = res[i * d:(i + 1) * d, :].astype(o_ref.dtype)


def _proj_t(a, w_t, l, row0, n, d, tk, B, L):
    k = a.shape[1]
    rows = n * d
    nc = L // tk
    return pl.pallas_call(
        _proj_t_kernel,
        grid=(B, nc),
        in_specs=[pl.BlockSpec((None, rows, k), lambda b, c: (l, row0 // rows, 0)),
                  pl.BlockSpec((tk, k), lambda b, c: (b * nc + c, 0))],
        out_specs=pl.BlockSpec((1, n, 1, d, tk), lambda b, c: (b, 0, c, 0, 0)),
        out_shape=jax.ShapeDtypeStruct((B, n, nc, d, tk), bf16),
        compiler_params=_cparams(("parallel", "parallel")),
    )(w_t, a)


def _rope_heads(x, c, s1, s2, half):
    return x * c + pltpu.roll(x, half, 1) * s1 + pltpu.roll(x, 128 - half, 1) * s2


def _mm_rope_kernel(a_ref, w_ref, c_ref, s1_ref, s2_ref, o_ref):
    acc = jnp.dot(a_ref[...], w_ref[...], preferred_element_type=f32)
    c, s1, s2 = c_ref[...], s1_ref[...], s2_ref[...]
    for h in range(acc.shape[1] // 128):
        sl = slice(h * 128, (h + 1) * 128)
        o_ref[:, sl] = _rope_heads(acc[:, sl], c, s1, s2, 16).astype(o_ref.dtype)


def _mm_rope(a, w, l, n, tabs, L, tm, tn):
    m, k = a.shape
    per_b = L // tm
    tspec = pl.BlockSpec((tm, 128), lambda j, i: (i % per_b, 0))
    return pl.pallas_call(
        _mm_rope_kernel,
        grid=(n // tn, m // tm),
        in_specs=[pl.BlockSpec((tm, k), lambda j, i: (i, 0)),
                  pl.BlockSpec((None, k, tn), lambda j, i: (l, 0, j)),
                  tspec, tspec, tspec],
        out_specs=pl.BlockSpec((tm, tn), lambda j, i: (i, j)),
        out_shape=jax.ShapeDtypeStruct((m, n), bf16),
        compiler_params=_cparams(("parallel", "parallel")),
    )(a, w, *tabs)


def _prep_kernel(aq_ref, iq_ref, bq_ref, cq_ref, ck_ref, bk_ref, alat_ref, ikw_ref, bg_ref, alg_ref,
                 c_ref, s1_ref, s2_ref, c6_ref, s16_ref, s26_ref,
                 aq_o, iq_o, bq_o, cq_o, ck_o, bk_o, alat_o, ikw_o, g_o):
    c, s1, s2 = c_ref[...], s1_ref[...], s2_ref[...]
    c6, s16, s26 = c6_ref[...], s16_ref[...], s26_ref[...]
    scale = HEAD_DIM ** -0.5 * _LOG2E

    def rope_all(src, dst, mult, tabs, half):
        for h in range(src.shape[1] // 128):
            sl = slice(h * 128, (h + 1) * 128)
            y = _rope_heads(src[:, sl].astype(f32), *tabs, half)
            if mult != 1.0:
                y = y * mult
            dst[:, sl] = y.astype(dst.dtype)

    rope_all(aq_ref, aq_o, scale, (c, s1, s2), 16)
    rope_all(bq_ref, bq_o, scale, (c, s1, s2), 16)
    rope_all(cq_ref, cq_o, C_DIM ** -0.5 * _LOG2E, (c, s1, s2), 16)
    rope_all(ck_ref, ck_o, 1.0, (c, s1, s2), 16)
    rope_all(bk_ref, bk_o, 1.0, (c, s1, s2), 16)
    rope_all(iq_ref, iq_o, 1.0, (c6, s16, s26), 8)

    a = alat_ref[...].astype(f32)
    ms = jnp.mean(a * a, axis=-1, keepdims=True)
    alat_o[...] = (a * lax.rsqrt(ms + 1e-6) * alg_ref[...]).astype(alat_o.dtype)

    x = ikw_ref[...].astype(f32)
    lane = lax.broadcasted_iota(i32, x.shape, 1)
    isk = lane < IDX_DIM
    mu = jnp.sum(jnp.where(isk, x, 0.0), axis=-1, keepdims=True) * (1.0 / IDX_DIM)
    d = jnp.where(isk, x - mu, 0.0)
    var = jnp.sum(d * d, axis=-1, keepdims=True) * (1.0 / IDX_DIM)
    y = d * lax.rsqrt(var + 1e-5)
    yr = _rope_heads(y, c6, s16, s26, 8)
    ikw_o[...] = jnp.where(isk, yr, x * (IDX_DIM ** -0.5 * IDX_HEADS ** -0.5))

    g_o[...] = _sigmoid(bg_ref[...].astype(f32))


def _prep(h, a_lat_g, tabs128, tabs64, L):
    m = h.shape[0]
    tm = 256
    per_b = L // tm

    def hs(width, off):
        return pl.BlockSpec((tm, width), lambda i, _o=off // width: (i, _o))

    tspec = pl.BlockSpec((tm, 128), lambda i: (i % per_b, 0))

    def os(width):
        return pl.BlockSpec((tm, width), lambda i: (i, 0))

    outs = [(1024, bf16)] * 5 + [(768, bf16), (512, bf16), (128, f32), (128, f32)]
    return pl.pallas_call(
        _prep_kernel,
        grid=(m // tm,),
        in_specs=[hs(1024, _P_AQ), hs(1024, _P_IQ), hs(1024, _P_BQ), hs(1024, _P_CQ), hs(1024, _P_CK),
                  hs(768, _P_BK), hs(512, _P_ALAT), hs(128, _P_IKW), hs(128, _P_BG),
                  pl.BlockSpec((1, A_LATENT), lambda i: (0, 0))] + [tspec] * 6,
        out_specs=[os(w) for w, _ in outs],
        out_shape=[jax.ShapeDtypeStruct((m, w), dt) for w, dt in outs],
        compiler_params=_cparams(("parallel",)),
    )(h, h, h, h, h, h, h, h, h, a_lat_g.reshape(1, A_LATENT), *tabs128, *tabs64)


def _f32_order_key(x):
    b = int(np.float32(x).view(np.int32))
    return b ^ ((b >> 31) & 0x7FFFFFFF)


_KEY_NEG = _f32_order_key(NEG)


def _dsa_mask_kernel(ik_ref, iq_ref, iw_ref, o_ref, key_ref, qp_ref, j_ref, *, L, tq, ksel):
    j = pl.program_id(1)
    ck = 512
    nck = L // ck
    nc = ((j + 1) * tq + ck - 1) // ck
    n_out_i = L - nc * ck
    n_out = n_out_i.astype(f32)
    t_row = j * tq + lax.broadcasted_iota(i32, (1, tq), 1)

    for hp in range(IDX_HEADS // 2):
        for e in range(2):
            h = 2 * hp + e
            qp_ref[hp, e * tq:(e + 1) * tq, :] = iq_ref[0, :, h * IDX_DIM:(h + 1) * IDX_DIM]

    def score_chunk(c, carry):
        off = pl.multiple_of(c * ck, ck)
        ikc = ik_ref[0, pl.ds(off, ck), :]
        acc = jnp.zeros((ck, tq), f32)
        for hp in range(IDX_HEADS // 2):
            s2 = _dot_nt(ikc, qp_ref[hp])
            acc = acc + jnp.maximum(s2[:, :tq], 0.0) * iw_ref[0, 2 * hp:2 * hp + 1, :]
            acc = acc + jnp.maximum(s2[:, tq:], 0.0) * iw_ref[0, 2 * hp + 1:2 * hp + 2, :]
        acc = jnp.where(acc == 0.0, 0.0, acc)
        sidx = off + lax.broadcasted_iota(i32, (ck, tq), 0)
        key_ref[pl.ds(off, ck), :] = jnp.where(sidx <= t_row, acc, NEG)
        return carry

    lax.fori_loop(0, nc, score_chunk, 0)

    def as_f32(key):
        return lax.bitcast_convert_type(key ^ ((key >> 31) & 0x7FFFFFFF), f32)

    def count(pred_fn):
        def body(c, cnt):
            off = pl.multiple_of(c * ck, ck)
            k = key_ref[pl.ds(off, ck), :]
            sidx = off + lax.broadcasted_iota(i32, (ck, tq), 0)
            return cnt + jnp.sum(pred_fn(k, sidx).reshape(ck // 64, 64, tq), axis=0)
        part = lax.fori_loop(0, nc, body, jnp.zeros((64, tq), f32))
        return jnp.sum(part, axis=0, keepdims=True)

    kf = float(ksel)

    def bit_body(i, carry):
        thr, cnt_thr = carry
        cand = thr + lax.shift_left(jnp.int32(1), 31 - i)
        cand_f = as_f32(cand)
        cnt = count(lambda k, s: jnp.where(k >= cand_f, 1.0, 0.0)) + jnp.where(_KEY_NEG >= cand, n_out, 0.0)
        ok = cnt >= kf
        return jnp.where(ok, cand, thr), jnp.where(ok, cnt, cnt_thr)

    thr_key, cnt_ge = lax.fori_loop(0, 32, bit_body, (jnp.full((1, tq), -2 ** 31, i32),
                                                       jnp.full((1, tq), float(L), f32)))
    thr = as_f32(thr_key)
    cnt_gt = count(lambda k, s: jnp.where(k > thr, 1.0, 0.0)) + jnp.where(_KEY_NEG > thr_key, n_out, 0.0)
    need = kf - cnt_gt

    j_ref[...] = jnp.full((1, tq), L, i32)

    @pl.when(jnp.max(cnt_ge) > kf)
    def _():
        nbits = L.bit_length() - 1

        def jbit(i, cur):
            cand = cur | lax.shift_left(jnp.int32(1), nbits - 1 - i)
            f = count(lambda k, s: jnp.where(k == thr, jnp.where(s < cand, 1.0, 0.0), 0.0))
            f = f + jnp.where(thr_key == _KEY_NEG, jnp.clip(cand - nc * ck, 0, n_out_i).astype(f32), 0.0)
            return jnp.where(f < need, cand, cur)

        j_ref[...] = lax.fori_loop(0, nbits, jbit, jnp.zeros((1, tq), i32))

    jlast = j_ref[...]

    def write(c, carry):
        off = pl.multiple_of(c * ck, ck)
        k = key_ref[pl.ds(off, ck), :]
        sidx = off + lax.broadcasted_iota(i32, (ck, tq), 0)
        sel = jnp.where(k > thr, _INF, jnp.where(k == thr, jnp.where(sidx <= jlast, _INF, NEG), NEG))
        o_ref[0, pl.ds(off, ck), :] = jnp.where(sidx <= t_row, sel, NEG)
        return carry

    lax.fori_loop(0, nc, write, 0)

    def write_rest(c, carry):
        off = pl.multiple_of(c * ck, ck)
        o_ref[0, pl.ds(off, ck), :] = jnp.full((ck, tq), NEG, f32)
        return carry

    lax.fori_loop(nc, nck, write_rest, 0)


def _dsa_mask(ik, iq, iw_t):
    B, L, _ = iq.shape
    tq = 128
    ksel = min(DSA_TOPK, L // 4)
    return pl.pallas_call(
        functools.partial(_dsa_mask_kernel, L=L, tq=tq, ksel=ksel),
        grid=(B, L // tq),
        in_specs=[pl.BlockSpec((1, L, IDX_DIM), lambda b, j: (b, 0, 0)),
                  pl.BlockSpec((1, tq, IDX_HEADS * IDX_DIM), lambda b, j: (b, j, 0)),
                  pl.BlockSpec((1, IDX_HEADS, tq), lambda b, j: (b, 0, j))],
        out_specs=pl.BlockSpec((1, L, tq), lambda b, j: (b, 0, j)),
        out_shape=jax.ShapeDtypeStruct((B, L, L), f32),
        scratch_shapes=[pltpu.VMEM((L, tq), f32), pltpu.VMEM((IDX_HEADS // 2, 2 * tq, IDX_DIM), bf16),
                        pltpu.VMEM((1, tq), i32)],
        compiler_params=_cparams(("parallel", "parallel")),
    )(ik, iq, iw_t)


def _col_reduce(x, op):
    r, c = x.shape
    if r > 64:
        x = op(x.reshape(r // 64, 64, c), axis=0)
    return op(x, axis=0, keepdims=True)


def _flash_group(ss, cap, v_ts, m_refs, l_refs, acc_refs):
    n = len(ss)
    if cap is not None:
        ss = [jnp.minimum(s, cap) for s in ss]
    m_prev = [r[...] for r in m_refs]
    m_new = [jnp.maximum(m_prev[i], _col_reduce(ss[i], jnp.max)) for i in range(n)]
    alpha = [jnp.exp2(m_prev[i] - m_new[i]) for i in range(n)]
    ps = [jnp.exp2(ss[i] - m_new[i]) for i in range(n)]
    l_new = [alpha[i] * l_refs[i][...] + _col_reduce(ps[i], jnp.sum) for i in range(n)]
    pv = [jnp.dot(v_ts[i], ps[i].astype(bf16), preferred_element_type=f32) for i in range(n)]
    for i in range(n):
        acc_new = acc_refs[i][...] * alpha[i] + pv[i]
        if cap is not None:
            dead = m_new[i] <= NEG
            l_new[i] = jnp.where(dead, 0.0, l_new[i])
            acc_new = jnp.where(dead, 0.0, acc_new)
        l_refs[i][...] = l_new[i]
        acc_refs[i][...] = acc_new
        m_refs[i][...] = m_new[i]


def _flash_heads(score_fns, cap, v_ts, m_ref, l_ref, acc_ref, group):
    n = len(score_fns)
    groups = [list(range(g, min(g + group, n))) for g in range(0, n, group)]
    nxt = [score_fns[h]() for h in groups[0]]
    for gi, hs in enumerate(groups):
        ss = nxt
        if gi + 1 < len(groups):
            nxt = [score_fns[h]() for h in groups[gi + 1]]
        _flash_group(ss, cap, [v_ts[h] for h in hs], [m_ref.at[h] for h in hs],
                     [l_ref.at[h] for h in hs], [acc_ref.at[h] for h in hs])


def _flash_init(m_ref, l_ref, acc_ref):
    m_ref[...] = jnp.full(m_ref.shape, NEG, f32)
    l_ref[...] = jnp.zeros(l_ref.shape, f32)
    acc_ref[...] = jnp.zeros(acc_ref.shape, f32)


def _dsa_attn_kernel(q_ref, k_ref, vt_ref, mask_ref, o_ref, m_ref, l_ref, acc_ref, *, tq, tk, nkc):
    j = pl.program_id(1)
    c = pl.program_id(2)

    @pl.when(c == 0)
    def _():
        _flash_init(m_ref, l_ref, acc_ref)

    @pl.when(c * tk < (j + 1) * tq)
    def _():
        cap = mask_ref[0]

        def scores(h):
            sl = slice(h * HEAD_DIM, (h + 1) * HEAD_DIM)
            return lambda: _dot_nt(k_ref[0, :, sl], q_ref[0, :, sl])

        v_ts = [vt_ref[0, h, 0] for h in range(A_HEADS)]
        _flash_heads([scores(h) for h in range(A_HEADS)], cap, v_ts, m_ref, l_ref, acc_ref, _HEAD_GROUP)

    @pl.when(c == nkc - 1)
    def _():
        for h in range(A_HEADS):
            o = acc_ref[h] / jnp.maximum(l_ref[h], 1e-30)
            o_ref[0, :, h * HEAD_DIM:(h + 1) * HEAD_DIM] = o.T.astype(o_ref.dtype)


def _dsa_attn(q, k, v_t, mask_t):
    B, L, W = q.shape
    tq, tk = 256, 512
    nkc = L // tk

    def last(j):
        return ((j + 1) * tq - 1) // tk

    return pl.pallas_call(
        functools.partial(_dsa_attn_kernel, tq=tq, tk=tk, nkc=nkc),
        grid=(B, L // tq, nkc),
        in_specs=[pl.BlockSpec((1, tq, W), lambda b, j, c: (b, j, 0)),
                  pl.BlockSpec((1, tk, W), lambda b, j, c: (b, jnp.minimum(c, last(j)), 0)),
                  pl.BlockSpec((1, A_HEADS, 1, HEAD_DIM, tk),
                               lambda b, j, c: (b, 0, jnp.minimum(c, last(j)), 0, 0)),
                  pl.BlockSpec((1, tk, tq), lambda b, j, c: (b, jnp.minimum(c, last(j)), j))],
        out_specs=pl.BlockSpec((1, tq, W), lambda b, j, c: (b, j, 0)),
        out_shape=jax.ShapeDtypeStruct((B, L, W), bf16),
        scratch_shapes=[pltpu.VMEM((A_HEADS, 1, tq), f32), pltpu.VMEM((A_HEADS, 1, tq), f32),
                        pltpu.VMEM((A_HEADS, HEAD_DIM, tq), f32)],
        compiler_params=_cparams(("parallel", "parallel", "arbitrary")),
    )(q, k, v_t, mask_t)


def _nsa_cmp_kernel(x_ref, w1_ref, w2_ref, pe_ref, o_ref):
    x = x_ref[0, 0, 0]
    w1 = w1_ref[0]
    half = CMP_STRIDE * HEAD_DIM
    a = jnp.dot(x, w1[:half], preferred_element_type=f32)
    b = jnp.dot(x, w1[half:], preferred_element_type=f32)
    pe = jnp.dot(pe_ref[0], w1, preferred_element_type=f32)[0:1]
    n = a.shape[0]
    pre = a + pltpu.roll(b, n - 1, 0) + pe
    act = pre * _sigmoid(pre)
    o_ref[0, 0, 0] = jnp.dot(act.astype(bf16), w2_ref[0], preferred_element_type=f32).astype(o_ref.dtype)


def _nsa_compress(x, w1, w2, pe, l):
    B, _, G, n, wd = x.shape
    return pl.pallas_call(
        _nsa_cmp_kernel,
        grid=(B, 2, G),
        in_specs=[pl.BlockSpec((1, 1, 1, n, wd), lambda b, t, g: (b, t, g, 0, 0)),
                  pl.BlockSpec((None, 1, CMP_LEN * HEAD_DIM, HEAD_DIM), lambda b, t, g: (l, t, 0, 0)),
                  pl.BlockSpec((None, 1, HEAD_DIM, HEAD_DIM), lambda b, t, g: (l, t, 0, 0)),
                  pl.BlockSpec((None, 1, 8, CMP_LEN * HEAD_DIM), lambda b, t, g: (l, t, 0, 0))],
        out_specs=pl.BlockSpec((1, 1, 1, n, HEAD_DIM), lambda b, t, g: (b, t, g, 0, 0)),
        out_shape=jax.ShapeDtypeStruct((B, 2, G, n, HEAD_DIM), bf16),
        compiler_params=_cparams(("parallel", "parallel", "parallel")),
    )(x, w1, w2, pe)


def _nsa_attn_kernel(q_ref, kc_ref, vct_ref, ks_ref, vst_ref, kw_ref, vwt_ref, g_ref, cov_ref, o_ref,
                     m_ref, l_ref, acc_ref, out_ref, sel_ref, *, L, tq, tk, n_sel):
    j = pl.program_id(2)
    t_row = j * tq + lax.broadcasted_iota(i32, (1, tq), 1)
    n_cmp = kc_ref.shape[2]
    n_slc = L // SLC_LEN
    qs = [q_ref[0, :, hh * HEAD_DIM:(hh + 1) * HEAD_DIM] for hh in range(B_HPG)]

    def gate(hh, i):
        return g_ref[0, 0, hh * 3 + i:hh * 3 + i + 1, :]

    kc = kc_ref[0, 0]
    vct = vct_ref[0, 0]
    cend = lax.broadcasted_iota(i32, (n_cmp, tq), 0) * CMP_STRIDE + (CMP_LEN - 1)
    mc = cend <= t_row
    ss = [jnp.where(mc, _dot_nt(kc, qs[hh]), NEG) for hh in range(B_HPG)]
    mxs = [_col_reduce(s, jnp.max) for s in ss]
    ps = [jnp.where(mc, jnp.exp2(ss[hh] - mxs[hh]), 0.0) for hh in range(B_HPG)]
    inv = [1.0 / jnp.maximum(_col_reduce(p, jnp.sum), 1e-30) for p in ps]
    pvs = [jnp.dot(vct, p.astype(bf16), preferred_element_type=f32) for p in ps]
    psum = ps[0] * inv[0]
    for hh in range(1, B_HPG):
        psum = psum + ps[hh] * inv[hh]
    for hh in range(B_HPG):
        out_ref[hh] = (gate(hh, 0) * inv[hh]) * pvs[hh]
    imp = jnp.dot(cov_ref[...], psum, preferred_element_type=f32, precision=lax.Precision.HIGHEST)
    blk = lax.broadcasted_iota(i32, (n_slc, tq), 0)
    cur = t_row >> (SLC_LEN.bit_length() - 1)
    forced = (blk == 0) | (blk == cur) | (blk == cur - 1)
    imp = jnp.where(forced, FORCE, jnp.where(blk <= cur, imp, NEG))
    rank = jnp.zeros((n_slc, tq), f32)
    for r in range(n_slc):
        row = imp[r:r + 1, :]
        rank = rank + jnp.where(row > imp, 1.0, jnp.where(row == imp, jnp.where(blk > r, 1.0, 0.0), 0.0))
    sel_ref[...] = jnp.where(rank < float(n_sel), _INF, NEG)

    def finish(i):
        for hh in range(B_HPG):
            o = acc_ref[hh] / jnp.maximum(l_ref[hh], 1e-30)
            out_ref[hh] = out_ref[hh] + gate(hh, i) * o

    def run_branch(k_ref, vt_ref, c_lo, c_hi, mask_fn):
        for hh in range(B_HPG):
            _flash_init(m_ref.at[hh], l_ref.at[hh], acc_ref.at[hh])

        def body(c, carry):
            off = pl.multiple_of(c * tk, tk)
            kch = k_ref[0, pl.ds(off, tk), :]
            vch = vt_ref[0, 0, c]
            sidx = off + lax.broadcasted_iota(i32, (tk, tq), 0)
            cap = mask_fn(c, sidx)
            fns = [functools.partial(_dot_nt, kch, qs[hh]) for hh in range(B_HPG)]
            _flash_heads(fns, cap, [vch] * B_HPG, m_ref, l_ref, acc_ref, _HEAD_GROUP)
            return carry

        lax.fori_loop(c_lo, c_hi, body, 0)

    bpc = tk // SLC_LEN

    def slc_mask(c, sidx):
        rows = [jnp.broadcast_to(sel_ref[pl.ds(c * bpc + b, 1), :], (SLC_LEN, tq)) for b in range(bpc)]
        return jnp.where(sidx <= t_row, jnp.concatenate(rows, axis=0), NEG)

    c_hi = ((j + 1) * tq + tk - 1) // tk
    run_branch(ks_ref, vst_ref, 0, c_hi, slc_mask)
    finish(1)

    def win_mask(c, sidx):
        d = t_row - sidx
        return jnp.where(d >= 0, jnp.where(d < WIN_LEN, _INF, NEG), NEG)

    c_lo = jnp.maximum(j * tq - (WIN_LEN - 1), 0) // tk
    run_branch(kw_ref, vwt_ref, c_lo, c_hi, win_mask)
    finish(2)

    for hh in range(B_HPG):
        o_ref[0, :, hh * HEAD_DIM:(hh + 1) * HEAD_DIM] = out_ref[hh].T.astype(o_ref.dtype)


def _nsa_attn(bq, kc, vc_t, bk, bv_tc, gates_t, cov_t):
    B, L, _ = bq.shape
    G = B_KV_GROUPS
    tq, tk = 256, 512
    n_cmp = kc.shape[2]
    n_slc = L // SLC_LEN
    n_sel = min(SLC_TOPN, n_slc)
    gw = B_HPG * HEAD_DIM
    nc = L // tk
    return pl.pallas_call(
        functools.partial(_nsa_attn_kernel, L=L, tq=tq, tk=tk, n_sel=n_sel),
        grid=(B, G, L // tq),
        in_specs=[pl.BlockSpec((1, tq, gw), lambda b, g, j: (b, j, g)),
                  pl.BlockSpec((1, 1, n_cmp, HEAD_DIM), lambda b, g, j: (b, g, 0, 0)),
                  pl.BlockSpec((1, 1, HEAD_DIM, n_cmp), lambda b, g, j: (b, g, 0, 0)),
                  pl.BlockSpec((1, L, HEAD_DIM), lambda b, g, j: (b, 0, 2 + g)),
                  pl.BlockSpec((1, 1, nc, HEAD_DIM, tk), lambda b, g, j: (b, g, 0, 0, 0)),
                  pl.BlockSpec((1, L, HEAD_DIM), lambda b, g, j: (b, 0, 4 + g)),
                  pl.BlockSpec((1, 1, nc, HEAD_DIM, tk), lambda b, g, j: (b, B_KV_GROUPS + g, 0, 0, 0)),
                  pl.BlockSpec((1, 1, 3 * B_HPG, tq), lambda b, g, j: (b, g, 0, j)),
                  pl.BlockSpec((n_slc, n_cmp), lambda b, g, j: (0, 0))],
        out_specs=pl.BlockSpec((1, tq, gw), lambda b, g, j: (b, j, g)),
        out_shape=jax.ShapeDtypeStruct((B, L, B_HEADS * HEAD_DIM), bf16),
        scratch_shapes=[pltpu.VMEM((B_HPG, 1, tq), f32), pltpu.VMEM((B_HPG, 1, tq), f32),
                        pltpu.VMEM((B_HPG, HEAD_DIM, tq), f32), pltpu.VMEM((B_HPG, HEAD_DIM, tq), f32),
                        pltpu.VMEM((n_slc, tq), f32)],
        compiler_params=_cparams(("parallel", "parallel", "parallel")),
    )(bq, kc, vc_t, bk, bv_tc, bk, bv_tc, gates_t, cov_t)


def _diff_attn_kernel(q_ref, k_ref, vt_ref, lam_ref, g_ref, o_ref, m_ref, l_ref, acc_ref,
                      *, tq, tk, lam_init):
    j = pl.program_id(2)
    t_row = j * tq + lax.broadcasted_iota(i32, (1, tq), 1)
    qs = [q_ref[0, :, mi * C_DIM:(mi + 1) * C_DIM] for mi in range(2)]
    for mi in range(2):
        _flash_init(m_ref.at[mi], l_ref.at[mi], acc_ref.at[mi])

    def chunk(c, masked):
        off = pl.multiple_of(c * tk, tk)
        vch = vt_ref[0, 0, c]
        cap = None
        if masked:
            sidx = off + lax.broadcasted_iota(i32, (tk, tq), 0)
            cap = jnp.where(sidx <= t_row, _INF, NEG)
        ss = [_dot_nt(k_ref[0, pl.ds(off, tk), mi * C_DIM:(mi + 1) * C_DIM], qs[mi]) for mi in range(2)]
        _flash_group(ss, cap, [vch] * 2, [m_ref.at[mi] for mi in range(2)],
                     [l_ref.at[mi] for mi in range(2)], [acc_ref.at[mi] for mi in range(2)])

    def body(c, carry):
        chunk(c, False)
        return carry

    c_diag = (j * tq) // tk
    lax.fori_loop(0, c_diag, body, 0)
    chunk(c_diag, True)

    lam = lam_ref[...]
    lam_val = (jnp.exp(jnp.sum(lam[0:1] * lam[1:2], axis=1, keepdims=True))
               - jnp.exp(jnp.sum(lam[2:3] * lam[3:4], axis=1, keepdims=True)) + lam_init)
    o = (acc_ref[0] / jnp.maximum(l_ref[0], 1e-30)
         - lam_val * (acc_ref[1] / jnp.maximum(l_ref[1], 1e-30)))
    ms = jnp.mean(o * o, axis=0, keepdims=True)
    y = o * lax.rsqrt(ms + 1e-6) * g_ref[...] * (1.0 - lam_init)
    o_ref[0] = y.T.astype(o_ref.dtype)


def _diff_attn(cq, ck, cv_tc, lam, g_col, lam_init):
    B, L, _ = cq.shape
    tq, tk = _DIFF_TQ, cv_tc.shape[-1]
    assert tk % tq == 0
    hw = 2 * C_DIM
    nc = L // tk
    return pl.pallas_call(
        functools.partial(_diff_attn_kernel, tq=tq, tk=tk, lam_init=lam_init),
        grid=(B, C_HEADS, L // tq),
        in_specs=[pl.BlockSpec((1, tq, hw), lambda b, h, j: (b, j, h)),
                  pl.BlockSpec((1, L, hw), lambda b, h, j: (b, 0, h)),
                  pl.BlockSpec((1, 1, nc, hw, tk), lambda b, h, j: (b, h, 0, 0, 0)),
                  pl.BlockSpec((4, C_DIM), lambda b, h, j: (0, 0)),
                  pl.BlockSpec((hw, 1), lambda b, h, j: (0, 0))],
        out_specs=pl.BlockSpec((1, tq, hw), lambda b, h, j: (b, j, h)),
        out_shape=jax.ShapeDtypeStruct((B, L, C_HEADS * hw), bf16),
        scratch_shapes=[pltpu.VMEM((2, 1, tq), f32), pltpu.VMEM((2, 1, tq), f32),
                        pltpu.VMEM((2, hw, tq), f32)],
        compiler_params=_cparams(("parallel", "parallel", "parallel")),
    )(cq, ck, cv_tc, lam, g_col)


def _merge_kernel(ya_ref, yb_ref, yc_ref, w_ref, g0_ref, g1_ref, g2_ref, o_ref, wb_ref):
    @pl.when(pl.program_id(1) == 0)
    def _():
        wb_ref[...] = w_ref[...].astype(bf16)

    acc = None
    for r, (y_ref, g_ref) in enumerate(((ya_ref, g0_ref), (yb_ref, g1_ref), (yc_ref, g2_ref))):
        br = jnp.dot(y_ref[...], wb_ref[r], preferred_element_type=f32)
        t = _sigmoid(g_ref[...].astype(f32)) * br
        acc = t if acc is None else acc + t
    o_ref[...] = acc.astype(o_ref.dtype)


def _merge(ya, yb, yc, w_br, l, h):
    m, kw = ya.shape
    tm, tn = 512, 512
    npb = D_MODEL // tn
    yspec = pl.BlockSpec((tm, kw), lambda j, i: (i, 0))

    def gspec(r):
        return pl.BlockSpec((tm, tn), lambda j, i, _r=r: (i, _r * npb + j))

    return pl.pallas_call(
        _merge_kernel,
        grid=(npb, m // tm),
        in_specs=[yspec, yspec, yspec,
                  pl.BlockSpec((None, N_BRANCH, kw, tn), lambda j, i: (l, 0, 0, j)),
                  gspec(0), gspec(1), gspec(2)],
        out_specs=pl.BlockSpec((tm, tn), lambda j, i: (i, j)),
        out_shape=jax.ShapeDtypeStruct((m, D_MODEL), bf16),
        scratch_shapes=[pltpu.VMEM((N_BRANCH, kw, tn), bf16)],
        compiler_params=_cparams(("parallel", "arbitrary")),
    )(ya, yb, yc, w_br, h, h, h)


def _mm_res_ln_kernel(a_ref, w_ref, x_ref, gate_ref, lg_ref, lb_ref, sc_ref, sh_ref, xo_ref, *u_refs,
                      alpha, sub):
    tm = a_ref.shape[0]
    for r in range(tm // sub):
        rows = slice(r * sub, (r + 1) * sub)
        y = jnp.dot(a_ref[rows, :], w_ref[...], preferred_element_type=f32)
        z = alpha * x_ref[rows, :] + gate_ref[0] * y
        xn = _ln_rows(z, 1e-5) * lg_ref[...] + lb_ref[...]
        xo_ref[rows, :] = xn
        if u_refs:
            u_refs[0][rows, :] = (_ln_rows(xn, 1e-5) * (1.0 + sc_ref[0]) + sh_ref[0]).astype(bf16)


def _mm_res_ln(a, w, l, x2, gate, ln_g, ln_b, sc, sh, L, alpha, emit_u):
    m, kdim = a.shape
    d = w.shape[2]
    tm = _RES_LN_TM if kdim * d * 2 <= 16 * 1024 * 1024 else _RES_LN_SUB
    per_b = L // tm
    bspec = pl.BlockSpec((1, 1, d), lambda i: (i // per_b, 0, 0))
    vspec = pl.BlockSpec((1, d), lambda i: (0, 0))
    rspec = pl.BlockSpec((tm, d), lambda i: (i, 0))
    out_shape = [jax.ShapeDtypeStruct((m, d), f32)]
    out_specs = [rspec]
    if emit_u:
        out_shape.append(jax.ShapeDtypeStruct((m, d), bf16))
        out_specs.append(rspec)
    res = pl.pallas_call(
        functools.partial(_mm_res_ln_kernel, alpha=alpha, sub=_RES_LN_SUB),
        grid=(m // tm,),
        in_specs=[pl.BlockSpec((tm, kdim), lambda i: (i, 0)),
                  pl.BlockSpec((None, kdim, d), lambda i: (l, 0, 0), pipeline_mode=pl.Buffered(1)),
                  rspec, bspec, vspec, vspec, bspec, bspec],
        out_specs=out_specs,
        out_shape=out_shape,
        compiler_params=_cparams(("parallel",), 60 * 1024 * 1024),
    )(a, w, x2, gate, ln_g.reshape(1, d), ln_b.reshape(1, d), sc, sh)
    return res if emit_u else (res[0], None)


def _ffn_in_kernel(a_ref, wg_ref, wu_ref, o_ref, wgb_ref, wub_ref):
    @pl.when(pl.program_id(1) == 0)
    def _():
        wgb_ref[...] = wg_ref[...].astype(bf16)
        wub_ref[...] = wu_ref[...].astype(bf16)

    a = a_ref[...]
    g = jnp.dot(a, wgb_ref[...], preferred_element_type=f32)
    u = jnp.dot(a, wub_ref[...], preferred_element_type=f32)
    o_ref[...] = (g * _sigmoid(g) * u).astype(o_ref.dtype)


def _ffn_in(a, w, l):
    m, k = a.shape
    tm, tn = 512, 512
    nb = D_FF // tn
    return pl.pallas_call(
        _ffn_in_kernel,
        grid=(nb, m // tm),
        in_specs=[pl.BlockSpec((tm, k), lambda j, i: (i, 0)),
                  pl.BlockSpec((None, k, tn), lambda j, i: (l, 0, j)),
                  pl.BlockSpec((None, k, tn), lambda j, i: (l, 0, nb + j))],
        out_specs=pl.BlockSpec((tm, tn), lambda j, i: (i, j)),
        out_shape=jax.ShapeDtypeStruct((m, D_FF), bf16),
        scratch_shapes=[pltpu.VMEM((k, tn), bf16), pltpu.VMEM((k, tn), bf16)],
        compiler_params=_cparams(("parallel", "arbitrary")),
    )(a, w, w)


def _rope_tables(L, d):
    r = d // 4
    half = r // 2
    inv = ROPE_THETA ** (-(jnp.arange(half, dtype=f32) * 2.0) / r)
    ang = jnp.arange(L).astype(f32)[:, None] * inv[None, :]
    cos, sin = jnp.cos(ang), jnp.sin(ang)
    z = jnp.zeros((L, d - r), f32)
    zh = jnp.zeros((L, half), f32)
    c = jnp.concatenate([cos, cos, jnp.ones((L, d - r), f32)], axis=1)
    s1 = jnp.concatenate([zh, sin, z], axis=1)
    s2 = jnp.concatenate([-sin, zh, z], axis=1)
    rep = 128 // d
    return tuple(jnp.tile(t, (1, rep)) for t in (c, s1, s2))


def _pack_w_in_kernel(w_ref, o_ref, ov_ref):
    def cp(src, width, dst):
        for o in range(0, width, 1024):
            wd = min(1024, width - o)
            o_ref[:, dst + o:dst + o + wd] = w_ref[src + o:src + o + wd, :].T.astype(bf16)

    def bkv(i, kv, g):
        return _O_BKV + ((i * 2 + kv) * B_KV_GROUPS + g) * HEAD_DIM

    cp(_O_GL, N_BRANCH * D_MODEL, _P_GL)
    for src, dst in ((_O_AQ, _P_AQ), (_O_IQ, _P_IQ), (_O_BQ, _P_BQ), (_O_CQ, _P_CQ), (_O_CK, _P_CK)):
        cp(src, 1024, dst)
    for i in range(3):
        for g in range(B_KV_GROUPS):
            cp(bkv(i, 0, g), HEAD_DIM, _P_BK + (i * B_KV_GROUPS + g) * HEAD_DIM)
    for g in range(B_KV_GROUPS):
        cp(bkv(0, 1, g), HEAD_DIM, _P_BVC + g * HEAD_DIM)
    cp(_O_ALAT, A_LATENT, _P_ALAT)
    ov_ref[_PV_CV:_PV_CV + 1024, :] = w_ref[_O_CV:_O_CV + 1024, :].astype(bf16)
    for i in (1, 2):
        for g in range(B_KV_GROUPS):
            dst = _PV_BV + ((i - 1) * B_KV_GROUPS + g) * HEAD_DIM
            ov_ref[dst:dst + HEAD_DIM, :] = w_ref[bkv(i, 1, g):bkv(i, 1, g) + HEAD_DIM, :].astype(bf16)
    lane = lax.broadcasted_iota(i32, (w_ref.shape[1], 128), 1)
    assert _O_IW == _O_IK + IDX_DIM and _O_IK % 8 == 0 and _O_BG % 8 == 0
    blk = w_ref[_O_IK:_O_IK + 128, :].T
    o_ref[:, _P_IKW:_P_IKW + 128] = jnp.where(lane < IDX_DIM + IDX_HEADS, blk, 0.0).astype(bf16)
    blk = w_ref[_O_BG:_O_BG + 128, :].T
    o_ref[:, _P_BG:_P_BG + 128] = jnp.where(lane < 3 * B_HEADS, blk, 0.0).astype(bf16)


def _pack_w_in(w_in):
    depth, k, n = w_in.shape
    w_t = jnp.swapaxes(w_in, 1, 2)
    tc = 128
    return pl.pallas_call(
        _pack_w_in_kernel,
        grid=(depth, k // tc),
        in_specs=[pl.BlockSpec((None, n, tc), lambda l, i: (l, 0, i))],
        out_specs=[pl.BlockSpec((None, tc, _P_TOT), lambda l, i: (l, i, 0)),
                   pl.BlockSpec((None, _PV_TOT, tc), lambda l, i: (l, 0, i))],
        out_shape=[jax.ShapeDtypeStruct((depth, k, _P_TOT), bf16),
                   jax.ShapeDtypeStruct((depth, _PV_TOT, k), bf16)],
        compiler_params=_cparams(("parallel", "parallel")),
    )(w_t)


def _cover_t(L):
    n_cmp_pad = L // CMP_STRIDE
    starts = np.arange(n_cmp_pad) * CMP_STRIDE
    slc_start = np.arange(L // SLC_LEN) * SLC_LEN
    cover = ((starts[:, None] < slc_start[None, :] + SLC_LEN)
             & (starts[:, None] + CMP_LEN > slc_start[None, :])).astype(np.float32)
    n_cmp = (L - CMP_LEN) // CMP_STRIDE + 1
    cover[n_cmp:] = 0.0
    return jnp.asarray(cover.T)


def _token_mixing(u, h, B, L, l, lw, tabs128, tabs64, cov_t, lam_init):
    M = B * L
    G = B_KV_GROUPS
    aq, iq, bq, cq, ck, bk, alat_n, ikw, gates = _prep(h, lw['a_lat_g'], tabs128, tabs64, L)

    aw = A_HEADS * HEAD_DIM
    ak = _mm_rope(alat_n, lw['a_up'], l, aw, tabs128, L, 512, aw)
    av_tc = _proj_t(alat_n, lw['a_up_vt'], l, 0, A_HEADS, HEAD_DIM, 512, B, L)
    ik = ikw[:, :IDX_DIM].astype(bf16).reshape(B, L, IDX_DIM)
    iw_t = ikw[:, IDX_DIM:IDX_DIM + IDX_HEADS].reshape(B, L, IDX_HEADS).transpose(0, 2, 1)
    mask_t = _dsa_mask(ik, iq.reshape(B, L, -1), iw_t)
    ya = _dsa_attn(aq.reshape(B, L, -1), ak.reshape(B, L, -1), av_tc, mask_t)

    bvc = h[:, _P_BVC:_P_BVC + G * HEAD_DIM].reshape(B, L, G, HEAD_DIM)
    bkc = bk[:, :G * HEAD_DIM].reshape(B, L, G, HEAD_DIM)
    n_row = L // CMP_STRIDE
    xcmp = jnp.stack([bkc, bvc], axis=1)
    xcmp = xcmp.transpose(0, 1, 3, 2, 4).reshape(B, 2, G, n_row, CMP_STRIDE * HEAD_DIM)
    cmp_out = _nsa_compress(xcmp, lw['cmp_w1'], lw['cmp_w2'], lw['cmp_pe'], l)
    kc = cmp_out[:, 0]
    vc_t = cmp_out[:, 1].transpose(0, 1, 3, 2)
    bv_tc = _proj_t(u, lw['w_vt'], l, _PV_BV, 2 * G, HEAD_DIM, 512, B, L)
    gates_t = gates[:, :3 * B_HEADS].reshape(B, L, G, 3 * B_HPG).transpose(0, 2, 3, 1)
    yb = _nsa_attn(bq.reshape(B, L, -1), kc, vc_t, bk.reshape(B, L, -1), bv_tc, gates_t, cov_t)

    cv_tc = _proj_t(u, lw['w_vt'], l, _PV_CV, C_HEADS, 2 * C_DIM, _DIFF_TK, B, L)
    yc = _diff_attn(cq.reshape(B, L, -1), ck.reshape(B, L, -1), cv_tc, lw['lam'],
                    lw['c_subln_g'].reshape(2 * C_DIM, 1), lam_init)

    return _merge(ya.reshape(M, -1), yb.reshape(M, -1), yc.reshape(M, -1), lw['w_br'], l, h)


def kernel(x, c, w_ada, b_ada, w_in, a_lat_g, a_up, cmp_w1, cmp_w2, cmp_pe, lam, c_subln_g, w_br, w_o,
           w_ffn_in, w_ffn_out, ln_g, ln_b):
    B, L, D = x.shape
    depth = w_ada.shape[0]
    M = B * L
    alpha = (2 * depth) ** 0.25

    c_pad = jnp.zeros((8, D), f32).at[:B].set(c)
    mod = _ada(c_pad, w_ada, b_ada)[:, :B]
    mods = [[mod[l, :, i * D:(i + 1) * D].reshape(B, 1, D) for i in range(6)] for l in range(depth)]

    tabs128 = _rope_tables(L, HEAD_DIM)
    tabs64 = _rope_tables(L, IDX_DIM)
    cov_t = _cover_t(L)

    w_in_p, w_vt = _pack_w_in(w_in)
    pe_flat = jnp.zeros((depth, 2, 8, CMP_LEN * HEAD_DIM), f32).at[:, :, 0].set(
        cmp_pe.reshape(depth, 2, CMP_LEN * HEAD_DIM)).astype(bf16)
    a_up_vt = jnp.swapaxes(a_up[:, :, A_HEADS * HEAD_DIM:], 1, 2).astype(bf16)
    wb = dict(a_up=a_up.astype(bf16), a_up_vt=a_up_vt, w_vt=w_vt, cmp_w1=cmp_w1.astype(bf16),
              cmp_w2=cmp_w2.astype(bf16), cmp_pe=pe_flat, w_br=w_br)
    w_o_b = w_o.astype(bf16)
    w_fo_b = w_ffn_out.astype(bf16)

    x2 = x.reshape(M, D)
    u = _lnmod(x2, mods[0][1], mods[0][0], L)
    for l in range(depth):
        lam_init = 0.8 - 0.6 * math.exp(-0.3 * l)
        sh_a, sc_a, g_a, sh_f, sc_f, g_f = mods[l]
        lw = dict(wb, a_lat_g=a_lat_g[l], lam=lam[l], c_subln_g=c_subln_g[l])
        h = _mm(u, w_in_p, l, 512, 768, bf16)
        merged = _token_mixing(u, h, B, L, l, lw, tabs128, tabs64, cov_t, lam_init)
        x2, u = _mm_res_ln(merged, w_o_b, l, x2, g_a, ln_g[l, 0], ln_b[l, 0], sc_f, sh_f, L, alpha, True)
        f = _ffn_in(u, w_ffn_in, l)
        last = l == depth - 1
        nsc, nsh = (sc_f, sh_f) if last else (mods[l + 1][1], mods[l + 1][0])
        x2, u = _mm_res_ln(f, w_fo_b, l, x2, g_f, ln_g[l, 1], ln_b[l, 1], nsc, nsh, L, alpha, not last)
    return x2.reshape(B, L, D)
```

```python
import functools
import math

import numpy as np
import jax
import jax.numpy as jnp
from jax import lax
from jax.experimental import pallas as pl
from jax.experimental.pallas import tpu as pltpu

f32 = jnp.float32
bf16 = jnp.bfloat16
i32 = jnp.int32

D_MODEL = 2048
HEAD_DIM = 128
ROPE_THETA = 500000.0
NEG = -1e30
FORCE = 1e6
A_HEADS = 8
A_LATENT = 512
IDX_HEADS = 16
IDX_DIM = 64
DSA_TOPK = 256
B_HEADS = 8
B_KV_GROUPS = 2
B_HPG = B_HEADS // B_KV_GROUPS
CMP_LEN = 32
CMP_STRIDE = 16
SLC_LEN = 64
SLC_TOPN = 16
WIN_LEN = 512
C_HEADS = 4
C_DIM = 128
BRANCH_W = A_HEADS * HEAD_DIM
N_BRANCH = 3
D_FF = int(math.ceil(8 * D_MODEL / 3 / 256)) * 256

_O_AQ = 0
_O_ALAT = _O_AQ + A_HEADS * HEAD_DIM
_O_IQ = _O_ALAT + A_LATENT
_O_IK = _O_IQ + IDX_HEADS * IDX_DIM
_O_IW = _O_IK + IDX_DIM
_O_BQ = _O_IW + IDX_HEADS
_O_BKV = _O_BQ + B_HEADS * HEAD_DIM
_O_BG = _O_BKV + 3 * 2 * B_KV_GROUPS * HEAD_DIM
_O_CQ = _O_BG + 3 * B_HEADS
_O_CK = _O_CQ + C_HEADS * 2 * C_DIM
_O_CV = _O_CK + C_HEADS * 2 * C_DIM
_O_GL = _O_CV + C_HEADS * 2 * C_DIM
_N_IN = _O_GL + N_BRANCH * D_MODEL

_P_GL = 0
_P_BK = 6144
_P_BVC = 6912
_P_AQ = 7168
_P_IQ = 8192
_P_BQ = 9216
_P_CQ = 10240
_P_CK = 11264
_P_ALAT = 12288
_P_IKW = 12800
_P_BG = 12928
_P_TOT = 13056
_PV_CV = 0
_PV_BV = 1024
_PV_TOT = 1536

_VMEM_LIMIT = 48 * 1024 * 1024
_LOG2E = 1.4426950408889634
_INF = float("inf")
_RES_LN_TM = 512
_RES_LN_SUB = 256
_DIFF_TK = 1024
_DIFF_TQ = 512
_HEAD_GROUP = 2


def _cparams(sem, vmem=_VMEM_LIMIT):
    return pltpu.CompilerParams(dimension_semantics=sem, vmem_limit_bytes=vmem)


def _sigmoid(x):
    return 1.0 / (1.0 + jnp.exp(-x))


def _dot_nt(a, b):
    return lax.dot_general(a, b, (((1,), (1,)), ((), ())), preferred_element_type=f32)


def _ada_kernel(c_ref, w_ref, b_ref, o_ref):
    c = c_ref[...]
    cs = c * _sigmoid(c)
    o_ref[0] = jnp.dot(cs, w_ref[0], preferred_element_type=f32,
                       precision=lax.Precision.HIGHEST) + b_ref[0]


def _ada(c_pad, w_ada, b_ada):
    depth, d, n = w_ada.shape
    tn = 512
    return pl.pallas_call(
        _ada_kernel,
        grid=(depth, n // tn),
        in_specs=[pl.BlockSpec((8, d), lambda l, j: (0, 0)),
                  pl.BlockSpec((1, d, tn), lambda l, j: (l, 0, j)),
                  pl.BlockSpec((1, 1, tn), lambda l, j: (l, 0, j))],
        out_specs=pl.BlockSpec((1, 8, tn), lambda l, j: (l, 0, j)),
        out_shape=jax.ShapeDtypeStruct((depth, 8, n), f32),
        compiler_params=_cparams(("parallel", "parallel")),
    )(c_pad, w_ada, b_ada.reshape(depth, 1, n))


def _ln_rows(x, eps):
    mu = jnp.mean(x, axis=-1, keepdims=True)
    d = x - mu
    var = jnp.mean(d * d, axis=-1, keepdims=True)
    return d * lax.rsqrt(var + eps)


def _lnmod_kernel(x_ref, sc_ref, sh_ref, o_ref):
    y = _ln_rows(x_ref[...], 1e-5)
    o_ref[...] = (y * (1.0 + sc_ref[0]) + sh_ref[0]).astype(o_ref.dtype)


def _lnmod(x2, sc, sh, L):
    m, d = x2.shape
    tm = 512
    per_b = L // tm
    return pl.pallas_call(
        _lnmod_kernel,
        grid=(m // tm,),
        in_specs=[pl.BlockSpec((tm, d), lambda i: (i, 0)),
                  pl.BlockSpec((1, 1, d), lambda i: (i // per_b, 0, 0)),
                  pl.BlockSpec((1, 1, d), lambda i: (i // per_b, 0, 0))],
        out_specs=pl.BlockSpec((tm, d), lambda i: (i, 0)),
        out_shape=jax.ShapeDtypeStruct((m, d), bf16),
        compiler_params=_cparams(("parallel",)),
    )(x2, sc, sh)


def _mm_kernel(a_ref, w_ref, o_ref):
    o_ref[...] = jnp.dot(a_ref[...], w_ref[...], preferred_element_type=f32).astype(o_ref.dtype)


def _mm(a, w, l, tm, tn, out_dtype):
    m, k = a.shape
    n = w.shape[2]
    return pl.pallas_call(
        _mm_kernel,
        grid=(n // tn, m // tm),
        in_specs=[pl.BlockSpec((tm, k), lambda j, i: (i, 0)),
                  pl.BlockSpec((None, k, tn), lambda j, i: (l, 0, j))],
        out_specs=pl.BlockSpec((tm, tn), lambda j, i: (i, j)),
        out_shape=jax.ShapeDtypeStruct((m, n), out_dtype),
        compiler_params=_cparams(("parallel", "parallel")),
    )(a, w)


def _proj_t_kernel(w_ref, a_ref, o_ref):
    n, d = o_ref.shape[1], o_ref.shape[3]
    res = _dot_nt(w_ref[...], a_ref[...])
    for i in range(n):
        o_ref[0, i, 0] = res[i * d:(i + 1) * d, :].astype(o_ref.dtype)


def _proj_t(a, w_t, l, row0, n, d, tk, B, L):
    k = a.shape[1]
    rows = n * d
    nc = L // tk
    return pl.pallas_call(
        _proj_t_kernel,
        grid=(B, nc),
        in_specs=[pl.BlockSpec((None, rows, k), lambda b, c: (l, row0 // rows, 0)),
                  pl.BlockSpec((tk, k), lambda b, c: (b * nc + c, 0))],
        out_specs=pl.BlockSpec((1, n, 1, d, tk), lambda b, c: (b, 0, c, 0, 0)),
        out_shape=jax.ShapeDtypeStruct((B, n, nc, d, tk), bf16),
        compiler_params=_cparams(("parallel", "parallel")),
    )(w_t, a)


def _rope_heads(x, c, s1, s2, half):
    return x * c + pltpu.roll(x, half, 1) * s1 + pltpu.roll(x, 128 - half, 1) * s2


def _mm_rope_kernel(a_ref, w_ref, c_ref, s1_ref, s2_ref, o_ref):
    acc = jnp.dot(a_ref[...], w_ref[...], preferred_element_type=f32)
    c, s1, s2 = c_ref[...], s1_ref[...], s2_ref[...]
    for h in range(acc.shape[1] // 128):
        sl = slice(h * 128, (h + 1) * 128)
        o_ref[:, sl] = _rope_heads(acc[:, sl], c, s1, s2, 16).astype(o_ref.dtype)


def _mm_rope(a, w, l, n, tabs, L, tm, tn):
    m, k = a.shape
    per_b = L // tm
    tspec = pl.BlockSpec((tm, 128), lambda j, i: (i % per_b, 0))
    return pl.pallas_call(
        _mm_rope_kernel,
        grid=(n // tn, m // tm),
        in_specs=[pl.BlockSpec((tm, k), lambda j, i: (i, 0)),
                  pl.BlockSpec((None, k, tn), lambda j, i: (l, 0, j)),
                  tspec, tspec, tspec],
        out_specs=pl.BlockSpec((tm, tn), lambda j, i: (i, j)),
        out_shape=jax.ShapeDtypeStruct((m, n), bf16),
        compiler_params=_cparams(("parallel", "parallel")),
    )(a, w, *tabs)


def _prep_kernel(aq_ref, iq_ref, bq_ref, cq_ref, ck_ref, bk_ref, alat_ref, ikw_ref, bg_ref, alg_ref,
                 c_ref, s1_ref, s2_ref, c6_ref, s16_ref, s26_ref,
                 aq_o, iq_o, bq_o, cq_o, ck_o, bk_o, alat_o, ikw_o, g_o):
    c, s1, s2 = c_ref[...], s1_ref[...], s2_ref[...]
    c6, s16, s26 = c6_ref[...], s16_ref[...], s26_ref[...]
    scale = HEAD_DIM ** -0.5 * _LOG2E

    def rope_all(src, dst, mult, tabs, half):
        for h in range(src.shape[1] // 128):
            sl = slice(h * 128, (h + 1) * 128)
            y = _rope_heads(src[:, sl].astype(f32), *tabs, half)
            if mult != 1.0:
                y = y * mult
            dst[:, sl] = y.astype(dst.dtype)

    rope_all(aq_ref, aq_o, scale, (c, s1, s2), 16)
    rope_all(bq_ref, bq_o, scale, (c, s1, s2), 16)
    rope_all(cq_ref, cq_o, C_DIM ** -0.5 * _LOG2E, (c, s1, s2), 16)
    rope_all(ck_ref, ck_o, 1.0, (c, s1, s2), 16)
    rope_all(bk_ref, bk_o, 1.0, (c, s1, s2), 16)
    rope_all(iq_ref, iq_o, 1.0, (c6, s16, s26), 8)

    a = alat_ref[...].astype(f32)
    ms = jnp.mean(a * a, axis=-1, keepdims=True)
    alat_o[...] = (a * lax.rsqrt(ms + 1e-6) * alg_ref[...]).astype(alat_o.dtype)

    x = ikw_ref[...].astype(f32)
    lane = lax.broadcasted_iota(i32, x.shape, 1)
    isk = lane < IDX_DIM
    mu = jnp.sum(jnp.where(isk, x, 0.0), axis=-1, keepdims=True) * (1.0 / IDX_DIM)
    d = jnp.where(isk, x - mu, 0.0)
    var = jnp.sum(d * d, axis=-1, keepdims=True) * (1.0 / IDX_DIM)
    y = d * lax.rsqrt(var + 1e-5)
    yr = _rope_heads(y, c6, s16, s26, 8)
    ikw_o[...] = jnp.where(isk, yr, x * (IDX_DIM ** -0.5 * IDX_HEADS ** -0.5))

    g_o[...] = _sigmoid(bg_ref[...].astype(f32))


def _prep(h, a_lat_g, tabs128, tabs64, L):
    m = h.shape[0]
    tm = 256
    per_b = L // tm

    def hs(width, off):
        return pl.BlockSpec((tm, width), lambda i, _o=off // width: (i, _o))

    tspec = pl.BlockSpec((tm, 128), lambda i: (i % per_b, 0))

    def os(width):
        return pl.BlockSpec((tm, width), lambda i: (i, 0))

    outs = [(1024, bf16)] * 5 + [(768, bf16), (512, bf16), (128, f32), (128, f32)]
    return pl.pallas_call(
        _prep_kernel,
        grid=(m // tm,),
        in_specs=[hs(1024, _P_AQ), hs(1024, _P_IQ), hs(1024, _P_BQ), hs(1024, _P_CQ), hs(1024, _P_CK),
                  hs(768, _P_BK), hs(512, _P_ALAT), hs(128, _P_IKW), hs(128, _P_BG),
                  pl.BlockSpec((1, A_LATENT), lambda i: (0, 0))] + [tspec] * 6,
        out_specs=[os(w) for w, _ in outs],
        out_shape=[jax.ShapeDtypeStruct((m, w), dt) for w, dt in outs],
        compiler_params=_cparams(("parallel",)),
    )(h, h, h, h, h, h, h, h, h, a_lat_g.reshape(1, A_LATENT), *tabs128, *tabs64)


def _f32_order_key(x):
    b = int(np.float32(x).view(np.int32))
    return b ^ ((b >> 31) & 0x7FFFFFFF)


_KEY_NEG = _f32_order_key(NEG)


def _dsa_mask_kernel(ik_ref, iq_ref, iw_ref, o_ref, key_ref, qp_ref, j_ref, *, L, tq, ksel):
    j = pl.program_id(1)
    ck = 512
    nck = L // ck
    nc = ((j + 1) * tq + ck - 1) // ck
    n_out_i = L - nc * ck
    n_out = n_out_i.astype(f32)
    t_row = j * tq + lax.broadcasted_iota(i32, (1, tq), 1)

    for hp in range(IDX_HEADS // 2):
        for e in range(2):
            h = 2 * hp + e
            qp_ref[hp, e * tq:(e + 1) * tq, :] = iq_ref[0, :, h * IDX_DIM:(h + 1) * IDX_DIM]

    def score_chunk(c, carry):
        off = pl.multiple_of(c * ck, ck)
        ikc = ik_ref[0, pl.ds(off, ck), :]
        acc = jnp.zeros((ck, tq), f32)
        for hp in range(IDX_HEADS // 2):
            s2 = _dot_nt(ikc, qp_ref[hp])
            acc = acc + jnp.maximum(s2[:, :tq], 0.0) * iw_ref[0, 2 * hp:2 * hp + 1, :]
            acc = acc + jnp.maximum(s2[:, tq:], 0.0) * iw_ref[0, 2 * hp + 1:2 * hp + 2, :]
        acc = jnp.where(acc == 0.0, 0.0, acc)
        sidx = off + lax.broadcasted_iota(i32, (ck, tq), 0)
        key_ref[pl.ds(off, ck), :] = jnp.where(sidx <= t_row, acc, NEG)
        return carry

    lax.fori_loop(0, nc, score_chunk, 0)

    def as_f32(key):
        return lax.bitcast_convert_type(key ^ ((key >> 31) & 0x7FFFFFFF), f32)

    def count(pred_fn):
        def body(c, cnt):
            off = pl.multiple_of(c * ck, ck)
            k = key_ref[pl.ds(off, ck), :]
            sidx = off + lax.broadcasted_iota(i32, (ck, tq), 0)
            return cnt + jnp.sum(pred_fn(k, sidx).reshape(ck // 64, 64, tq), axis=0)
        part = lax.fori_loop(0, nc, body, jnp.zeros((64, tq), f32))
        return jnp.sum(part, axis=0, keepdims=True)

    kf = float(ksel)

    def bit_body(i, carry):
        thr, cnt_thr = carry
        cand = thr + lax.shift_left(jnp.int32(1), 31 - i)
        cand_f = as_f32(cand)
        cnt = count(lambda k, s: jnp.where(k >= cand_f, 1.0, 0.0)) + jnp.where(_KEY_NEG >= cand, n_out, 0.0)
        ok = cnt >= kf
        return jnp.where(ok, cand, thr), jnp.where(ok, cnt, cnt_thr)

    thr_key, cnt_ge = lax.fori_loop(0, 32, bit_body, (jnp.full((1, tq), -2 ** 31, i32),
                                                       jnp.full((1, tq), float(L), f32)))
    thr = as_f32(thr_key)
    cnt_gt = count(lambda k, s: jnp.where(k > thr, 1.0, 0.0)) + jnp.where(_KEY_NEG > thr_key, n_out, 0.0)
    need = kf - cnt_gt

    j_ref[...] = jnp.full((1, tq), L, i32)

    @pl.when(jnp.max(cnt_ge) > kf)
    def _():
        nbits = L.bit_length() - 1

        def jbit(i, cur):
            cand = cur | lax.shift_left(jnp.int32(1), nbits - 1 - i)
            f = count(lambda k, s: jnp.where(k == thr, jnp.where(s < cand, 1.0, 0.0), 0.0))
            f = f + jnp.where(thr_key == _KEY_NEG, jnp.clip(cand - nc * ck, 0, n_out_i).astype(f32), 0.0)
            return jnp.where(f < need, cand, cur)

        j_ref[...] = lax.fori_loop(0, nbits, jbit, jnp.zeros((1, tq), i32))

    jlast = j_ref[...]

    def write(c, carry):
        off = pl.multiple_of(c * ck, ck)
        k = key_ref[pl.ds(off, ck), :]
        sidx = off + lax.broadcasted_iota(i32, (ck, tq), 0)
        sel = jnp.where(k > thr, _INF, jnp.where(k == thr, jnp.where(sidx <= jlast, _INF, NEG), NEG))
        o_ref[0, pl.ds(off, ck), :] = jnp.where(sidx <= t_row, sel, NEG)
        return carry

    lax.fori_loop(0, nc, write, 0)

    def write_rest(c, carry):
        off = pl.multiple_of(c * ck, ck)
        o_ref[0, pl.ds(off, ck), :] = jnp.full((ck, tq), NEG, f32)
        return carry

    lax.fori_loop(nc, nck, write_rest, 0)


def _dsa_mask(ik, iq, iw_t):
    B, L, _ = iq.shape
    tq = 128
    ksel = min(DSA_TOPK, L // 4)
    return pl.pallas_call(
        functools.partial(_dsa_mask_kernel, L=L, tq=tq, ksel=ksel),
        grid=(B, L // tq),
        in_specs=[pl.BlockSpec((1, L, IDX_DIM), lambda b, j: (b, 0, 0)),
                  pl.BlockSpec((1, tq, IDX_HEADS * IDX_DIM), lambda b, j: (b, j, 0)),
                  pl.BlockSpec((1, IDX_HEADS, tq), lambda b, j: (b, 0, j))],
        out_specs=pl.BlockSpec((1, L, tq), lambda b, j: (b, 0, j)),
        out_shape=jax.ShapeDtypeStruct((B, L, L), f32),
        scratch_shapes=[pltpu.VMEM((L, tq), f32), pltpu.VMEM((IDX_HEADS // 2, 2 * tq, IDX_DIM), bf16),
                        pltpu.VMEM((1, tq), i32)],
        compiler_params=_cparams(("parallel", "parallel")),
    )(ik, iq, iw_t)


def _col_reduce(x, op):
    r, c = x.shape
    if r > 64:
        x = op(x.reshape(r // 64, 64, c), axis=0)
    return op(x, axis=0, keepdims=True)


def _flash_group(ss, cap, v_ts, m_refs, l_refs, acc_refs):
    n = len(ss)
    if cap is not None:
        ss = [jnp.minimum(s, cap) for s in ss]
    m_prev = [r[...] for r in m_refs]
    m_new = [jnp.maximum(m_prev[i], _col_reduce(ss[i], jnp.max)) for i in range(n)]
    alpha = [jnp.exp2(m_prev[i] - m_new[i]) for i in range(n)]
    ps = [jnp.exp2(ss[i] - m_new[i]) for i in range(n)]
    l_new = [alpha[i] * l_refs[i][...] + _col_reduce(ps[i], jnp.sum) for i in range(n)]
    pv = [jnp.dot(v_ts[i], ps[i].astype(bf16), preferred_element_type=f32) for i in range(n)]
    for i in range(n):
        acc_new = acc_refs[i][...] * alpha[i] + pv[i]
        if cap is not None:
            dead = m_new[i] <= NEG
            l_new[i] = jnp.where(dead, 0.0, l_new[i])
            acc_new = jnp.where(dead, 0.0, acc_new)
        l_refs[i][...] = l_new[i]
        acc_refs[i][...] = acc_new
        m_refs[i][...] = m_new[i]


def _flash_heads(score_fns, cap, v_ts, m_ref, l_ref, acc_ref, group):
    n = len(score_fns)
    groups = [list(range(g, min(g + group, n))) for g in range(0, n, group)]
    nxt = [score_fns[h]() for h in groups[0]]
    for gi, hs in enumerate(groups):
        ss = nxt
        if gi + 1 < len(groups):
            nxt = [score_fns[h]() for h in groups[gi + 1]]
        _flash_group(ss, cap, [v_ts[h] for h in hs], [m_ref.at[h] for h in hs],
                     [l_ref.at[h] for h in hs], [acc_ref.at[h] for h in hs])


def _flash_init(m_ref, l_ref, acc_ref):
    m_ref[...] = jnp.full(m_ref.shape, NEG, f32)
    l_ref[...] = jnp.zeros(l_ref.shape, f32)
    acc_ref[...] = jnp.zeros(acc_ref.shape, f32)


def _dsa_attn_kernel(q_ref, k_ref, vt_ref, mask_ref, o_ref, m_ref, l_ref, acc_ref, *, tq, tk, nkc):
    j = pl.program_id(1)
    c = pl.program_id(2)

    @pl.when(c == 0)
    def _():
        _flash_init(m_ref, l_ref, acc_ref)

    @pl.when(c * tk < (j + 1) * tq)
    def _():
        cap = mask_ref[0]

        def scores(h):
            sl = slice(h * HEAD_DIM, (h + 1) * HEAD_DIM)
            return lambda: _dot_nt(k_ref[0, :, sl], q_ref[0, :, sl])

        v_ts = [vt_ref[0, h, 0] for h in range(A_HEADS)]
        _flash_heads([scores(h) for h in range(A_HEADS)], cap, v_ts, m_ref, l_ref, acc_ref, _HEAD_GROUP)

    @pl.when(c == nkc - 1)
    def _():
        for h in range(A_HEADS):
            o = acc_ref[h] / jnp.maximum(l_ref[h], 1e-30)
            o_ref[0, :, h * HEAD_DIM:(h + 1) * HEAD_DIM] = o.T.astype(o_ref.dtype)


def _dsa_attn(q, k, v_t, mask_t):
    B, L, W = q.shape
    tq, tk = 512, 512
    nkc = L // tk

    def last(j):
        return ((j + 1) * tq - 1) // tk

    return pl.pallas_call(
        functools.partial(_dsa_attn_kernel, tq=tq, tk=tk, nkc=nkc),
        grid=(B, L // tq, nkc),
        in_specs=[pl.BlockSpec((1, tq, W), lambda b, j, c: (b, j, 0)),
                  pl.BlockSpec((1, tk, W), lambda b, j, c: (b, jnp.minimum(c, last(j)), 0)),
                  pl.BlockSpec((1, A_HEADS, 1, HEAD_DIM, tk),
                               lambda b, j, c: (b, 0, jnp.minimum(c, last(j)), 0, 0)),
                  pl.BlockSpec((1, tk, tq), lambda b, j, c: (b, jnp.minimum(c, last(j)), j))],
        out_specs=pl.BlockSpec((1, tq, W), lambda b, j, c: (b, j, 0)),
        out_shape=jax.ShapeDtypeStruct((B, L, W), bf16),
        scratch_shapes=[pltpu.VMEM((A_HEADS, 1, tq), f32), pltpu.VMEM((A_HEADS, 1, tq), f32),
                        pltpu.VMEM((A_HEADS, HEAD_DIM, tq), f32)],
        compiler_params=_cparams(("parallel", "parallel", "arbitrary")),
    )(q, k, v_t, mask_t)


def _nsa_cmp_kernel(x_ref, w1_ref, w2_ref, pe_ref, o_ref):
    x = x_ref[0, 0, 0]
    w1 = w1_ref[0]
    half = CMP_STRIDE * HEAD_DIM
    a = jnp.dot(x, w1[:half], preferred_element_type=f32)
    b = jnp.dot(x, w1[half:], preferred_element_type=f32)
    pe = jnp.dot(pe_ref[0], w1, preferred_element_type=f32)[0:1]
    n = a.shape[0]
    pre = a + pltpu.roll(b, n - 1, 0) + pe
    act = pre * _sigmoid(pre)
    o_ref[0, 0, 0] = jnp.dot(act.astype(bf16), w2_ref[0], preferred_element_type=f32).astype(o_ref.dtype)


def _nsa_compress(x, w1, w2, pe, l):
    B, _, G, n, wd = x.shape
    return pl.pallas_call(
        _nsa_cmp_kernel,
        grid=(B, 2, G),
        in_specs=[pl.BlockSpec((1, 1, 1, n, wd), lambda b, t, g: (b, t, g, 0, 0)),
                  pl.BlockSpec((None, 1, CMP_LEN * HEAD_DIM, HEAD_DIM), lambda b, t, g: (l, t, 0, 0)),
                  pl.BlockSpec((None, 1, HEAD_DIM, HEAD_DIM), lambda b, t, g: (l, t, 0, 0)),
                  pl.BlockSpec((None, 1, 8, CMP_LEN * HEAD_DIM), lambda b, t, g: (l, t, 0, 0))],
        out_specs=pl.BlockSpec((1, 1, 1, n, HEAD_DIM), lambda b, t, g: (b, t, g, 0, 0)),
        out_shape=jax.ShapeDtypeStruct((B, 2, G, n, HEAD_DIM), bf16),
        compiler_params=_cparams(("parallel", "parallel", "parallel")),
    )(x, w1, w2, pe)


def _nsa_attn_kernel(q_ref, kc_ref, vct_ref, ks_ref, vst_ref, kw_ref, vwt_ref, g_ref, cov_ref, o_ref,
                     m_ref, l_ref, acc_ref, out_ref, sel_ref, *, L, tq, tk, n_sel):
    j = pl.program_id(2)
    t_row = j * tq + lax.broadcasted_iota(i32, (1, tq), 1)
    n_cmp = kc_ref.shape[2]
    n_slc = L // SLC_LEN
    qs = [q_ref[0, :, hh * HEAD_DIM:(hh + 1) * HEAD_DIM] for hh in range(B_HPG)]

    def gate(hh, i):
        return g_ref[0, 0, hh * 3 + i:hh * 3 + i + 1, :]

    kc = kc_ref[0, 0]
    vct = vct_ref[0, 0]
    cend = lax.broadcasted_iota(i32, (n_cmp, tq), 0) * CMP_STRIDE + (CMP_LEN - 1)
    mc = cend <= t_row
    ss = [jnp.where(mc, _dot_nt(kc, qs[hh]), NEG) for hh in range(B_HPG)]
    mxs = [_col_reduce(s, jnp.max) for s in ss]
    ps = [jnp.where(mc, jnp.exp2(ss[hh] - mxs[hh]), 0.0) for hh in range(B_HPG)]
    inv = [1.0 / jnp.maximum(_col_reduce(p, jnp.sum), 1e-30) for p in ps]
    pvs = [jnp.dot(vct, p.astype(bf16), preferred_element_type=f32) for p in ps]
    psum = ps[0] * inv[0]
    for hh in range(1, B_HPG):
        psum = psum + ps[hh] * inv[hh]
    for hh in range(B_HPG):
        out_ref[hh] = (gate(hh, 0) * inv[hh]) * pvs[hh]
    imp = jnp.dot(cov_ref[...], psum, preferred_element_type=f32, precision=lax.Precision.HIGHEST)
    blk = lax.broadcasted_iota(i32, (n_slc, tq), 0)
    cur = t_row >> (SLC_LEN.bit_length() - 1)
    forced = (blk == 0) | (blk == cur) | (blk == cur - 1)
    imp = jnp.where(forced, FORCE, jnp.where(blk <= cur, imp, NEG))
    rank = jnp.zeros((n_slc, tq), f32)
    for r in range(n_slc):
        row = imp[r:r + 1, :]
        rank = rank + jnp.where(row > imp, 1.0, jnp.where(row == imp, jnp.where(blk > r, 1.0, 0.0), 0.0))
    sel_ref[...] = jnp.where(rank < float(n_sel), _INF, NEG)

    def finish(i):
        for hh in range(B_HPG):
            o = acc_ref[hh] / jnp.maximum(l_ref[hh], 1e-30)
            out_ref[hh] = out_ref[hh] + gate(hh, i) * o

    def run_branch(k_ref, vt_ref, c_lo, c_hi, mask_fn):
        for hh in range(B_HPG):
            _flash_init(m_ref.at[hh], l_ref.at[hh], acc_ref.at[hh])

        def body(c, carry):
            off = pl.multiple_of(c * tk, tk)
            kch = k_ref[0, pl.ds(off, tk), :]
            vch = vt_ref[0, 0, c]
            sidx = off + lax.broadcasted_iota(i32, (tk, tq), 0)
            cap = mask_fn(c, sidx)
            fns = [functools.partial(_dot_nt, kch, qs[hh]) for hh in range(B_HPG)]
            _flash_heads(fns, cap, [vch] * B_HPG, m_ref, l_ref, acc_ref, _HEAD_GROUP)
            return carry

        lax.fori_loop(c_lo, c_hi, body, 0)

    bpc = tk // SLC_LEN

    def slc_mask(c, sidx):
        rows = [jnp.broadcast_to(sel_ref[pl.ds(c * bpc + b, 1), :], (SLC_LEN, tq)) for b in range(bpc)]
        return jnp.where(sidx <= t_row, jnp.concatenate(rows, axis=0), NEG)

    c_hi = ((j + 1) * tq + tk - 1) // tk
    run_branch(ks_ref, vst_ref, 0, c_hi, slc_mask)
    finish(1)

    def win_mask(c, sidx):
        d = t_row - sidx
        return jnp.where(d >= 0, jnp.where(d < WIN_LEN, _INF, NEG), NEG)

    c_lo = jnp.maximum(j * tq - (WIN_LEN - 1), 0) // tk
    run_branch(kw_ref, vwt_ref, c_lo, c_hi, win_mask)
    finish(2)

    for hh in range(B_HPG):
        o_ref[0, :, hh * HEAD_DIM:(hh + 1) * HEAD_DIM] = out_ref[hh].T.astype(o_ref.dtype)


def _nsa_attn(bq, kc, vc_t, bk, bv_tc, gates_t, cov_t):
    B, L, _ = bq.shape
    G = B_KV_GROUPS
    tq, tk = 256, 512
    n_cmp = kc.shape[2]
    n_slc = L // SLC_LEN
    n_sel = min(SLC_TOPN, n_slc)
    gw = B_HPG * HEAD_DIM
    nc = L // tk
    return pl.pallas_call(
        functools.partial(_nsa_attn_kernel, L=L, tq=tq, tk=tk, n_sel=n_sel),
        grid=(B, G, L // tq),
        in_specs=[pl.BlockSpec((1, tq, gw), lambda b, g, j: (b, j, g)),
                  pl.BlockSpec((1, 1, n_cmp, HEAD_DIM), lambda b, g, j: (b, g, 0, 0)),
                  pl.BlockSpec((1, 1, HEAD_DIM, n_cmp), lambda b, g, j: (b, g, 0, 0)),
                  pl.BlockSpec((1, L, HEAD_DIM), lambda b, g, j: (b, 0, 2 + g)),
                  pl.BlockSpec((1, 1, nc, HEAD_DIM, tk), lambda b, g, j: (b, g, 0, 0, 0)),
                  pl.BlockSpec((1, L, HEAD_DIM), lambda b, g, j: (b, 0, 4 + g)),
                  pl.BlockSpec((1, 1, nc, HEAD_DIM, tk), lambda b, g, j: (b, B_KV_GROUPS + g, 0, 0, 0)),
                  pl.BlockSpec((1, 1, 3 * B_HPG, tq), lambda b, g, j: (b, g, 0, j)),
                  pl.BlockSpec((n_slc, n_cmp), lambda b, g, j: (0, 0))],
        out_specs=pl.BlockSpec((1, tq, gw), lambda b, g, j: (b, j, g)),
        out_shape=jax.ShapeDtypeStruct((B, L, B_HEADS * HEAD_DIM), bf16),
        scratch_shapes=[pltpu.VMEM((B_HPG, 1, tq), f32), pltpu.VMEM((B_HPG, 1, tq), f32),
                        pltpu.VMEM((B_HPG, HEAD_DIM, tq), f32), pltpu.VMEM((B_HPG, HEAD_DIM, tq), f32),
                        pltpu.VMEM((n_slc, tq), f32)],
        compiler_params=_cparams(("parallel", "parallel", "parallel")),
    )(bq, kc, vc_t, bk, bv_tc, bk, bv_tc, gates_t, cov_t)


def _diff_attn_kernel(q_ref, k_ref, vt_ref, lam_ref, g_ref, o_ref, m_ref, l_ref, acc_ref,
                      *, tq, tk, lam_init):
    j = pl.program_id(2)
    t_row = j * tq + lax.broadcasted_iota(i32, (1, tq), 1)
    qs = [q_ref[0, :, mi * C_DIM:(mi + 1) * C_DIM] for mi in range(2)]
    for mi in range(2):
        _flash_init(m_ref.at[mi], l_ref.at[mi], acc_ref.at[mi])

    def chunk(c, masked):
        off = pl.multiple_of(c * tk, tk)
        vch = vt_ref[0, 0, c]
        cap = None
        if masked:
            sidx = off + lax.broadcasted_iota(i32, (tk, tq), 0)
            cap = jnp.where(sidx <= t_row, _INF, NEG)
        ss = [_dot_nt(k_ref[0, pl.ds(off, tk), mi * C_DIM:(mi + 1) * C_DIM], qs[mi]) for mi in range(2)]
        _flash_group(ss, cap, [vch] * 2, [m_ref.at[mi] for mi in range(2)],
                     [l_ref.at[mi] for mi in range(2)], [acc_ref.at[mi] for mi in range(2)])

    def body(c, carry):
        chunk(c, False)
        return carry

    c_diag = (j * tq) // tk
    lax.fori_loop(0, c_diag, body, 0)
    chunk(c_diag, True)

    lam = lam_ref[...]
    lam_val = (jnp.exp(jnp.sum(lam[0:1] * lam[1:2], axis=1, keepdims=True))
               - jnp.exp(jnp.sum(lam[2:3] * lam[3:4], axis=1, keepdims=True)) + lam_init)
    o = (acc_ref[0] / jnp.maximum(l_ref[0], 1e-30)
         - lam_val * (acc_ref[1] / jnp.maximum(l_ref[1], 1e-30)))
    ms = jnp.mean(o * o, axis=0, keepdims=True)
    y = o * lax.rsqrt(ms + 1e-6) * g_ref[...] * (1.0 - lam_init)
    o_ref[0] = y.T.astype(o_ref.dtype)


def _diff_attn(cq, ck, cv_tc, lam, g_col, lam_init):
    B, L, _ = cq.shape
    tq, tk = _DIFF_TQ, cv_tc.shape[-1]
    assert tk % tq == 0
    hw = 2 * C_DIM
    nc = L // tk
    return pl.pallas_call(
        functools.partial(_diff_attn_kernel, tq=tq, tk=tk, lam_init=lam_init),
        grid=(B, C_HEADS, L // tq),
        in_specs=[pl.BlockSpec((1, tq, hw), lambda b, h, j: (b, j, h)),
                  pl.BlockSpec((1, L, hw), lambda b, h, j: (b, 0, h)),
                  pl.BlockSpec((1, 1, nc, hw, tk), lambda b, h, j: (b, h, 0, 0, 0)),
                  pl.BlockSpec((4, C_DIM), lambda b, h, j: (0, 0)),
                  pl.BlockSpec((hw, 1), lambda b, h, j: (0, 0))],
        out_specs=pl.BlockSpec((1, tq, hw), lambda b, h, j: (b, j, h)),
        out_shape=jax.ShapeDtypeStruct((B, L, C_HEADS * hw), bf16),
        scratch_shapes=[pltpu.VMEM((2, 1, tq), f32), pltpu.VMEM((2, 1, tq), f32),
                        pltpu.VMEM((2, hw, tq), f32)],
        compiler_params=_cparams(("parallel", "parallel", "parallel")),
    )(cq, ck, cv_tc, lam, g_col)


def _merge_kernel(ya_ref, yb_ref, yc_ref, w_ref, g0_ref, g1_ref, g2_ref, o_ref, wb_ref):
    @pl.when(pl.program_id(1) == 0)
    def _():
        wb_ref[...] = w_ref[...].astype(bf16)

    acc = None
    for r, (y_ref, g_ref) in enumerate(((ya_ref, g0_ref), (yb_ref, g1_ref), (yc_ref, g2_ref))):
        br = jnp.dot(y_ref[...], wb_ref[r], preferred_element_type=f32)
        t = _sigmoid(g_ref[...].astype(f32)) * br
        acc = t if acc is None else acc + t
    o_ref[...] = acc.astype(o_ref.dtype)


def _merge(ya, yb, yc, w_br, l, h):
    m, kw = ya.shape
    tm, tn = 512, 512
    npb = D_MODEL // tn
    yspec = pl.BlockSpec((tm, kw), lambda j, i: (i, 0))

    def gspec(r):
        return pl.BlockSpec((tm, tn), lambda j, i, _r=r: (i, _r * npb + j))

    return pl.pallas_call(
        _merge_kernel,
        grid=(npb, m // tm),
        in_specs=[yspec, yspec, yspec,
                  pl.BlockSpec((None, N_BRANCH, kw, tn), lambda j, i: (l, 0, 0, j)),
                  gspec(0), gspec(1), gspec(2)],
        out_specs=pl.BlockSpec((tm, tn), lambda j, i: (i, j)),
        out_shape=jax.ShapeDtypeStruct((m, D_MODEL), bf16),
        scratch_shapes=[pltpu.VMEM((N_BRANCH, kw, tn), bf16)],
        compiler_params=_cparams(("parallel", "arbitrary")),
    )(ya, yb, yc, w_br, h, h, h)


def _mm_res_ln_kernel(a_ref, w_ref, x_ref, gate_ref, lg_ref, lb_ref, sc_ref, sh_ref, xo_ref, *u_refs,
                      alpha, sub):
    tm = a_ref.shape[0]
    for r in range(tm // sub):
        rows = slice(r * sub, (r + 1) * sub)
        y = jnp.dot(a_ref[rows, :], w_ref[...], preferred_element_type=f32)
        z = alpha * x_ref[rows, :] + gate_ref[0] * y
        xn = _ln_rows(z, 1e-5) * lg_ref[...] + lb_ref[...]
        xo_ref[rows, :] = xn
        if u_refs:
            u_refs[0][rows, :] = (_ln_rows(xn, 1e-5) * (1.0 + sc_ref[0]) + sh_ref[0]).astype(bf16)


def _mm_res_ln(a, w, l, x2, gate, ln_g, ln_b, sc, sh, L, alpha, emit_u):
    m, kdim = a.shape
    d = w.shape[2]
    tm = _RES_LN_TM if kdim * d * 2 <= 16 * 1024 * 1024 else _RES_LN_SUB
    per_b = L // tm
    bspec = pl.BlockSpec((1, 1, d), lambda i: (i // per_b, 0, 0))
    vspec = pl.BlockSpec((1, d), lambda i: (0, 0))
    rspec = pl.BlockSpec((tm, d), lambda i: (i, 0))
    out_shape = [jax.ShapeDtypeStruct((m, d), f32)]
    out_specs = [rspec]
    if emit_u:
        out_shape.append(jax.ShapeDtypeStruct((m, d), bf16))
        out_specs.append(rspec)
    res = pl.pallas_call(
        functools.partial(_mm_res_ln_kernel, alpha=alpha, sub=_RES_LN_SUB),
        grid=(m // tm,),
        in_specs=[pl.BlockSpec((tm, kdim), lambda i: (i, 0)),
                  pl.BlockSpec((None, kdim, d), lambda i: (l, 0, 0), pipeline_mode=pl.Buffered(1)),
                  rspec, bspec, vspec, vspec, bspec, bspec],
        out_specs=out_specs,
        out_shape=out_shape,
        compiler_params=_cparams(("parallel",), 60 * 1024 * 1024),
    )(a, w, x2, gate, ln_g.reshape(1, d), ln_b.reshape(1, d), sc, sh)
    return res if emit_u else (res[0], None)


def _ffn_in_kernel(a_ref, wg_ref, wu_ref, o_ref, wgb_ref, wub_ref):
    @pl.when(pl.program_id(1) == 0)
    def _():
        wgb_ref[...] = wg_ref[...].astype(bf16)
        wub_ref[...] = wu_ref[...].astype(bf16)

    a = a_ref[...]
    g = jnp.dot(a, wgb_ref[...], preferred_element_type=f32)
    u = jnp.dot(a, wub_ref[...], preferred_element_type=f32)
    o_ref[...] = (g * _sigmoid(g) * u).astype(o_ref.dtype)


def _ffn_in(a, w, l):
    m, k = a.shape
    tm, tn = 512, 512
    nb = D_FF // tn
    return pl.pallas_call(
        _ffn_in_kernel,
        grid=(nb, m // tm),
        in_specs=[pl.BlockSpec((tm, k), lambda j, i: (i, 0)),
                  pl.BlockSpec((None, k, tn), lambda j, i: (l, 0, j)),
                  pl.BlockSpec((None, k, tn), lambda j, i: (l, 0, nb + j))],
        out_specs=pl.BlockSpec((tm, tn), lambda j, i: (i, j)),
        out_shape=jax.ShapeDtypeStruct((m, D_FF), bf16),
        scratch_shapes=[pltpu.VMEM((k, tn), bf16), pltpu.VMEM((k, tn), bf16)],
        compiler_params=_cparams(("parallel", "arbitrary")),
    )(a, w, w)


def _rope_tables(L, d):
    r = d // 4
    half = r // 2
    inv = ROPE_THETA ** (-(jnp.arange(half, dtype=f32) * 2.0) / r)
    ang = jnp.arange(L).astype(f32)[:, None] * inv[None, :]
    cos, sin = jnp.cos(ang), jnp.sin(ang)
    z = jnp.zeros((L, d - r), f32)
    zh = jnp.zeros((L, half), f32)
    c = jnp.concatenate([cos, cos, jnp.ones((L, d - r), f32)], axis=1)
    s1 = jnp.concatenate([zh, sin, z], axis=1)
    s2 = jnp.concatenate([-sin, zh, z], axis=1)
    rep = 128 // d
    return tuple(jnp.tile(t, (1, rep)) for t in (c, s1, s2))


def _pack_w_in_kernel(w_ref, o_ref, ov_ref):
    def cp(src, width, dst):
        for o in range(0, width, 1024):
            wd = min(1024, width - o)
            o_ref[:, dst + o:dst + o + wd] = w_ref[src + o:src + o + wd, :].T.astype(bf16)

    def bkv(i, kv, g):
        return _O_BKV + ((i * 2 + kv) * B_KV_GROUPS + g) * HEAD_DIM

    cp(_O_GL, N_BRANCH * D_MODEL, _P_GL)
    for src, dst in ((_O_AQ, _P_AQ), (_O_IQ, _P_IQ), (_O_BQ, _P_BQ), (_O_CQ, _P_CQ), (_O_CK, _P_CK)):
        cp(src, 1024, dst)
    for i in range(3):
        for g in range(B_KV_GROUPS):
            cp(bkv(i, 0, g), HEAD_DIM, _P_BK + (i * B_KV_GROUPS + g) * HEAD_DIM)
    for g in range(B_KV_GROUPS):
        cp(bkv(0, 1, g), HEAD_DIM, _P_BVC + g * HEAD_DIM)
    cp(_O_ALAT, A_LATENT, _P_ALAT)
    ov_ref[_PV_CV:_PV_CV + 1024, :] = w_ref[_O_CV:_O_CV + 1024, :].astype(bf16)
    for i in (1, 2):
        for g in range(B_KV_GROUPS):
            dst = _PV_BV + ((i - 1) * B_KV_GROUPS + g) * HEAD_DIM
            ov_ref[dst:dst + HEAD_DIM, :] = w_ref[bkv(i, 1, g):bkv(i, 1, g) + HEAD_DIM, :].astype(bf16)
    lane = lax.broadcasted_iota(i32, (w_ref.shape[1], 128), 1)
    assert _O_IW == _O_IK + IDX_DIM and _O_IK % 8 == 0 and _O_BG % 8 == 0
    blk = w_ref[_O_IK:_O_IK + 128, :].T
    o_ref[:, _P_IKW:_P_IKW + 128] = jnp.where(lane < IDX_DIM + IDX_HEADS, blk, 0.0).astype(bf16)
    blk = w_ref[_O_BG:_O_BG + 128, :].T
    o_ref[:, _P_BG:_P_BG + 128] = jnp.where(lane < 3 * B_HEADS, blk, 0.0).astype(bf16)


def _pack_w_in(w_in):
    depth, k, n = w_in.shape
    w_t = jnp.swapaxes(w_in, 1, 2)
    tc = 128
    return pl.pallas_call(
        _pack_w_in_kernel,
        grid=(depth, k // tc),
        in_specs=[pl.BlockSpec((None, n, tc), lambda l, i: (l, 0, i))],
        out_specs=[pl.BlockSpec((None, tc, _P_TOT), lambda l, i: (l, i, 0)),
                   pl.BlockSpec((None, _PV_TOT, tc), lambda l, i: (l, 0, i))],
        out_shape=[jax.ShapeDtypeStruct((depth, k, _P_TOT), bf16),
                   jax.ShapeDtypeStruct((depth, _PV_TOT, k), bf16)],
        compiler_params=_cparams(("parallel", "parallel")),
    )(w_t)


def _cover_t(L):
    n_cmp_pad = L // CMP_STRIDE
    starts = np.arange(n_cmp_pad) * CMP_STRIDE
    slc_start = np.arange(L // SLC_LEN) * SLC_LEN
    cover = ((starts[:, None] < slc_start[None, :] + SLC_LEN)
             & (starts[:, None] + CMP_LEN > slc_start[None, :])).astype(np.float32)
    n_cmp = (L - CMP_LEN) // CMP_STRIDE + 1
    cover[n_cmp:] = 0.0
    return jnp.asarray(cover.T)


def _token_mixing(u, h, B, L, l, lw, tabs128, tabs64, cov_t, lam_init):
    M = B * L
    G = B_KV_GROUPS
    aq, iq, bq, cq, ck, bk, alat_n, ikw, gates = _prep(h, lw['a_lat_g'], tabs128, tabs64, L)

    aw = A_HEADS * HEAD_DIM
    ak = _mm_rope(alat_n, lw['a_up'], l, aw, tabs128, L, 512, aw)
    av_tc = _proj_t(alat_n, lw['a_up_vt'], l, 0, A_HEADS, HEAD_DIM, 512, B, L)
    ik = ikw[:, :IDX_DIM].astype(bf16).reshape(B, L, IDX_DIM)
    iw_t = ikw[:, IDX_DIM:IDX_DIM + IDX_HEADS].reshape(B, L, IDX_HEADS).transpose(0, 2, 1)
    mask_t = _dsa_mask(ik, iq.reshape(B, L, -1), iw_t)
    ya = _dsa_attn(aq.reshape(B, L, -1), ak.reshape(B, L, -1), av_tc, mask_t)

    bvc = h[:, _P_BVC:_P_BVC + G * HEAD_DIM].reshape(B, L, G, HEAD_DIM)
    bkc = bk[:, :G * HEAD_DIM].reshape(B, L, G, HEAD_DIM)
    n_row = L // CMP_STRIDE
    xcmp = jnp.stack([bkc, bvc], axis=1)
    xcmp = xcmp.transpose(0, 1, 3, 2, 4).reshape(B, 2, G, n_row, CMP_STRIDE * HEAD_DIM)
    cmp_out = _nsa_compress(xcmp, lw['cmp_w1'], lw['cmp_w2'], lw['cmp_pe'], l)
    kc = cmp_out[:, 0]
    vc_t = cmp_out[:, 1].transpose(0, 1, 3, 2)
    bv_tc = _proj_t(u, lw['w_vt'], l, _PV_BV, 2 * G, HEAD_DIM, 512, B, L)
    gates_t = gates[:, :3 * B_HEADS].reshape(B, L, G, 3 * B_HPG).transpose(0, 2, 3, 1)
    yb = _nsa_attn(bq.reshape(B, L, -1), kc, vc_t, bk.reshape(B, L, -1), bv_tc, gates_t, cov_t)

    cv_tc = _proj_t(u, lw['w_vt'], l, _PV_CV, C_HEADS, 2 * C_DIM, _DIFF_TK, B, L)
    yc = _diff_attn(cq.reshape(B, L, -1), ck.reshape(B, L, -1), cv_tc, lw['lam'],
                    lw['c_subln_g'].reshape(2 * C_DIM, 1), lam_init)

    return _merge(ya.reshape(M, -1), yb.reshape(M, -1), yc.reshape(M, -1), lw['w_br'], l, h)


def kernel(x, c, w_ada, b_ada, w_in, a_lat_g, a_up, cmp_w1, cmp_w2, cmp_pe, lam, c_subln_g, w_br, w_o,
           w_ffn_in, w_ffn_out, ln_g, ln_b):
    B, L, D = x.shape
    depth = w_ada.shape[0]
    M = B * L
    alpha = (2 * depth) ** 0.25

    c_pad = jnp.zeros((8, D), f32).at[:B].set(c)
    mod = _ada(c_pad, w_ada, b_ada)[:, :B]
    mods = [[mod[l, :, i * D:(i + 1) * D].reshape(B, 1, D) for i in range(6)] for l in range(depth)]

    tabs128 = _rope_tables(L, HEAD_DIM)
    tabs64 = _rope_tables(L, IDX_DIM)
    cov_t = _cover_t(L)

    w_in_p, w_vt = _pack_w_in(w_in)
    pe_flat = jnp.zeros((depth, 2, 8, CMP_LEN * HEAD_DIM), f32).at[:, :, 0].set(
        cmp_pe.reshape(depth, 2, CMP_LEN * HEAD_DIM)).astype(bf16)
    a_up_vt = jnp.swapaxes(a_up[:, :, A_HEADS * HEAD_DIM:], 1, 2).astype(bf16)
    wb = dict(a_up=a_up.astype(bf16), a_up_vt=a_up_vt, w_vt=w_vt, cmp_w1=cmp_w1.astype(bf16),
              cmp_w2=cmp_w2.astype(bf16), cmp_pe=pe_flat, w_br=w_br)
    w_o_b = w_o.astype(bf16)
    w_fo_b = w_ffn_out.astype(bf16)

    x2 = x.reshape(M, D)
    u = _lnmod(x2, mods[0][1], mods[0][0], L)
    for l in range(depth):
        lam_init = 0.8 - 0.6 * math.exp(-0.3 * l)
        sh_a, sc_a, g_a, sh_f, sc_f, g_f = mods[l]
        lw = dict(wb, a_lat_g=a_lat_g[l], lam=lam[l], c_subln_g=c_subln_g[l])
        h = _mm(u, w_in_p, l, 512, 768, bf16)
        merged = _token_mixing(u, h, B, L, l, lw, tabs128, tabs64, cov_t, lam_init)
        x2, u = _mm_res_ln(merged, w_o_b, l, x2, g_a, ln_g[l, 0], ln_b[l, 0], sc_f, sh_f, L, alpha, True)
        f = _ffn_in(u, w_ffn_in, l)
        last = l == depth - 1
        nsc, nsh = (sc_f, sh_f) if last else (mods[l + 1][1], mods[l + 1][0])
        x2, u = _mm_res_ln(f, w_fo_b, l, x2, g_f, ln_g[l, 1], ln_b[l, 1], nsc, nsh, L, alpha, not last)
    return x2.reshape(B, L, D)
```

```python
import functools
import math

import numpy as np
import jax
import jax.numpy as jnp
from jax import lax
from jax.experimental import pallas as pl
from jax.experimental.pallas import tpu as pltpu

f32 = jnp.float32
bf16 = jnp.bfloat16
i32 = jnp.int32

D_MODEL = 2048
HEAD_DIM = 128
ROPE_THETA = 500000.0
NEG = -1e30
FORCE = 1e6
A_HEADS = 8
A_LATENT = 512
IDX_HEADS = 16
IDX_DIM = 64
DSA_TOPK = 256
B_HEADS = 8
B_KV_GROUPS = 2
B_HPG = B_HEADS // B_KV_GROUPS
CMP_LEN = 32
CMP_STRIDE = 16
SLC_LEN = 64
SLC_TOPN = 16
WIN_LEN = 512
C_HEADS = 4
C_DIM = 128
BRANCH_W = A_HEADS * HEAD_DIM
N_BRANCH = 3
D_FF = int(math.ceil(8 * D_MODEL / 3 / 256)) * 256

_O_AQ = 0
_O_ALAT = _O_AQ + A_HEADS * HEAD_DIM
_O_IQ = _O_ALAT + A_LATENT
_O_IK = _O_IQ + IDX_HEADS * IDX_DIM
_O_IW = _O_IK + IDX_DIM
_O_BQ = _O_IW + IDX_HEADS
_O_BKV = _O_BQ + B_HEADS * HEAD_DIM
_O_BG = _O_BKV + 3 * 2 * B_KV_GROUPS * HEAD_DIM
_O_CQ = _O_BG + 3 * B_HEADS
_O_CK = _O_CQ + C_HEADS * 2 * C_DIM
_O_CV = _O_CK + C_HEADS * 2 * C_DIM
_O_GL = _O_CV + C_HEADS * 2 * C_DIM
_N_IN = _O_GL + N_BRANCH * D_MODEL

_P_GL = 0
_P_BK = 6144
_P_BVC = 6912
_P_AQ = 7168
_P_IQ = 8192
_P_BQ = 9216
_P_CQ = 10240
_P_CK = 11264
_P_ALAT = 12288
_P_IKW = 12800
_P_BG = 12928
_P_TOT = 13056
_PV_CV = 0
_PV_BV = 1024
_PV_TOT = 1536

_VMEM_LIMIT = 48 * 1024 * 1024
_LOG2E = 1.4426950408889634
_INF = float("inf")
_RES_LN_TM = 512
_RES_LN_SUB = 256
_DIFF_TK = 1024
_DIFF_TQ = 512
_HEAD_GROUP = 2


def _cparams(sem, vmem=_VMEM_LIMIT):
    return pltpu.CompilerParams(dimension_semantics=sem, vmem_limit_bytes=vmem)


def _sigmoid(x):
    return 1.0 / (1.0 + jnp.exp(-x))


def _dot_nt(a, b):
    return lax.dot_general(a, b, (((1,), (1,)), ((), ())), preferred_element_type=f32)


def _ada_kernel(c_ref, w_ref, b_ref, o_ref):
    c = c_ref[...]
    cs = c * _sigmoid(c)
    o_ref[0] = jnp.dot(cs, w_ref[0], preferred_element_type=f32,
                       precision=lax.Precision.HIGHEST) + b_ref[0]


def _ada(c_pad, w_ada, b_ada):
    depth, d, n = w_ada.shape
    tn = 512
    return pl.pallas_call(
        _ada_kernel,
        grid=(depth, n // tn),
        in_specs=[pl.BlockSpec((8, d), lambda l, j: (0, 0)),
                  pl.BlockSpec((1, d, tn), lambda l, j: (l, 0, j)),
                  pl.BlockSpec((1, 1, tn), lambda l, j: (l, 0, j))],
        out_specs=pl.BlockSpec((1, 8, tn), lambda l, j: (l, 0, j)),
        out_shape=jax.ShapeDtypeStruct((depth, 8, n), f32),
        compiler_params=_cparams(("parallel", "parallel")),
    )(c_pad, w_ada, b_ada.reshape(depth, 1, n))


def _ln_rows(x, eps):
    mu = jnp.mean(x, axis=-1, keepdims=True)
    d = x - mu
    var = jnp.mean(d * d, axis=-1, keepdims=True)
    return d * lax.rsqrt(var + eps)


def _lnmod_kernel(x_ref, sc_ref, sh_ref, o_ref):
    y = _ln_rows(x_ref[...], 1e-5)
    o_ref[...] = (y * (1.0 + sc_ref[0]) + sh_ref[0]).astype(o_ref.dtype)


def _lnmod(x2, sc, sh, L):
    m, d = x2.shape
    tm = 512
    per_b = L // tm
    return pl.pallas_call(
        _lnmod_kernel,
        grid=(m // tm,),
        in_specs=[pl.BlockSpec((tm, d), lambda i: (i, 0)),
                  pl.BlockSpec((1, 1, d), lambda i: (i // per_b, 0, 0)),
                  pl.BlockSpec((1, 1, d), lambda i: (i // per_b, 0, 0))],
        out_specs=pl.BlockSpec((tm, d), lambda i: (i, 0)),
        out_shape=jax.ShapeDtypeStruct((m, d), bf16),
        compiler_params=_cparams(("parallel",)),
    )(x2, sc, sh)


def _mm_kernel(a_ref, w_ref, o_ref):
    o_ref[...] = jnp.dot(a_ref[...], w_ref[...], preferred_element_type=f32).astype(o_ref.dtype)


def _mm(a, w, l, tm, tn, out_dtype):
    m, k = a.shape
    n = w.shape[2]
    return pl.pallas_call(
        _mm_kernel,
        grid=(n // tn, m // tm),
        in_specs=[pl.BlockSpec((tm, k), lambda j, i: (i, 0)),
                  pl.BlockSpec((None, k, tn), lambda j, i: (l, 0, j))],
        out_specs=pl.BlockSpec((tm, tn), lambda j, i: (i, j)),
        out_shape=jax.ShapeDtypeStruct((m, n), out_dtype),
        compiler_params=_cparams(("parallel", "parallel")),
    )(a, w)


def _proj_t_kernel(w_ref, a_ref, o_ref):
    n, d = o_ref.shape[1], o_ref.shape[3]
    res = _dot_nt(w_ref[...], a_ref[...])
    for i in range(n):
        o_ref[0, i, 0] = res[i * d:(i + 1) * d, :].astype(o_ref.dtype)


def _proj_t(a, w_t, l, row0, n, d, tk, B, L):
    k = a.shape[1]
    rows = n * d
    nc = L // tk
    return pl.pallas_call(
        _proj_t_kernel,
        grid=(B, nc),
        in_specs=[pl.BlockSpec((None, rows, k), lambda b, c: (l, row0 // rows, 0)),
                  pl.BlockSpec((tk, k), lambda b, c: (b * nc + c, 0))],
        out_specs=pl.BlockSpec((1, n, 1, d, tk), lambda b, c: (b, 0, c, 0, 0)),
        out_shape=jax.ShapeDtypeStruct((B, n, nc, d, tk), bf16),
        compiler_params=_cparams(("parallel", "parallel")),
    )(w_t, a)


def _rope_heads(x, c, s1, s2, half):
    return x * c + pltpu.roll(x, half, 1) * s1 + pltpu.roll(x, 128 - half, 1) * s2


def _mm_rope_kernel(a_ref, w_ref, c_ref, s1_ref, s2_ref, o_ref):
    acc = jnp.dot(a_ref[...], w_ref[...], preferred_element_type=f32)
    c, s1, s2 = c_ref[...], s1_ref[...], s2_ref[...]
    for h in range(acc.shape[1] // 128):
        sl = slice(h * 128, (h + 1) * 128)
        o_ref[:, sl] = _rope_heads(acc[:, sl], c, s1, s2, 16).astype(o_ref.dtype)


def _mm_rope(a, w, l, n, tabs, L, tm, tn):
    m, k = a.shape
    per_b = L // tm
    tspec = pl.BlockSpec((tm, 128), lambda j, i: (i % per_b, 0))
    return pl.pallas_call(
        _mm_rope_kernel,
        grid=(n // tn, m // tm),
        in_specs=[pl.BlockSpec((tm, k), lambda j, i: (i, 0)),
                  pl.BlockSpec((None, k, tn), lambda j, i: (l, 0, j)),
                  tspec, tspec, tspec],
        out_specs=pl.BlockSpec((tm, tn), lambda j, i: (i, j)),
        out_shape=jax.ShapeDtypeStruct((m, n), bf16),
        compiler_params=_cparams(("parallel", "parallel")),
    )(a, w, *tabs)


def _prep_kernel(aq_ref, iq_ref, bq_ref, cq_ref, ck_ref, bk_ref, alat_ref, ikw_ref, bg_ref, alg_ref,
                 c_ref, s1_ref, s2_ref, c6_ref, s16_ref, s26_ref,
                 aq_o, iq_o, bq_o, cq_o, ck_o, bk_o, alat_o, ikw_o, g_o):
    c, s1, s2 = c_ref[...], s1_ref[...], s2_ref[...]
    c6, s16, s26 = c6_ref[...], s16_ref[...], s26_ref[...]
    scale = HEAD_DIM ** -0.5 * _LOG2E

    def rope_all(src, dst, mult, tabs, half):
        for h in range(src.shape[1] // 128):
            sl = slice(h * 128, (h + 1) * 128)
            y = _rope_heads(src[:, sl].astype(f32), *tabs, half)
            if mult != 1.0:
                y = y * mult
            dst[:, sl] = y.astype(dst.dtype)

    rope_all(aq_ref, aq_o, scale, (c, s1, s2), 16)
    rope_all(bq_ref, bq_o, scale, (c, s1, s2), 16)
    rope_all(cq_ref, cq_o, C_DIM ** -0.5 * _LOG2E, (c, s1, s2), 16)
    rope_all(ck_ref, ck_o, 1.0, (c, s1, s2), 16)
    rope_all(bk_ref, bk_o, 1.0, (c, s1, s2), 16)
    rope_all(iq_ref, iq_o, 1.0, (c6, s16, s26), 8)

    a = alat_ref[...].astype(f32)
    ms = jnp.mean(a * a, axis=-1, keepdims=True)
    alat_o[...] = (a * lax.rsqrt(ms + 1e-6) * alg_ref[...]).astype(alat_o.dtype)

    x = ikw_ref[...].astype(f32)
    lane = lax.broadcasted_iota(i32, x.shape, 1)
    isk = lane < IDX_DIM
    mu = jnp.sum(jnp.where(isk, x, 0.0), axis=-1, keepdims=True) * (1.0 / IDX_DIM)
    d = jnp.where(isk, x - mu, 0.0)
    var = jnp.sum(d * d, axis=-1, keepdims=True) * (1.0 / IDX_DIM)
    y = d * lax.rsqrt(var + 1e-5)
    yr = _rope_heads(y, c6, s16, s26, 8)
    ikw_o[...] = jnp.where(isk, yr, x * (IDX_DIM ** -0.5 * IDX_HEADS ** -0.5))

    g_o[...] = _sigmoid(bg_ref[...].astype(f32))


def _prep(h, a_lat_g, tabs128, tabs64, L):
    m = h.shape[0]
    tm = 256
    per_b = L // tm

    def hs(width, off):
        return pl.BlockSpec((tm, width), lambda i, _o=off // width: (i, _o))

    tspec = pl.BlockSpec((tm, 128), lambda i: (i % per_b, 0))

    def os(width):
        return pl.BlockSpec((tm, width), lambda i: (i, 0))

    outs = [(1024, bf16)] * 5 + [(768, bf16), (512, bf16), (128, f32), (128, f32)]
    return pl.pallas_call(
        _prep_kernel,
        grid=(m // tm,),
        in_specs=[hs(1024, _P_AQ), hs(1024, _P_IQ), hs(1024, _P_BQ), hs(1024, _P_CQ), hs(1024, _P_CK),
                  hs(768, _P_BK), hs(512, _P_ALAT), hs(128, _P_IKW), hs(128, _P_BG),
                  pl.BlockSpec((1, A_LATENT), lambda i: (0, 0))] + [tspec] * 6,
        out_specs=[os(w) for w, _ in outs],
        out_shape=[jax.ShapeDtypeStruct((m, w), dt) for w, dt in outs],
        compiler_params=_cparams(("parallel",)),
    )(h, h, h, h, h, h, h, h, h, a_lat_g.reshape(1, A_LATENT), *tabs128, *tabs64)


def _f32_order_key(x):
    b = int(np.float32(x).view(np.int32))
    return b ^ ((b >> 31) & 0x7FFFFFFF)


_KEY_NEG = _f32_order_key(NEG)


def _dsa_mask_kernel(ik_ref, iq_ref, iw_ref, o_ref, key_ref, qp_ref, j_ref, *, L, tq, ksel):
    j = pl.program_id(1)
    ck = 512
    nck = L // ck
    nc = ((j + 1) * tq + ck - 1) // ck
    n_out_i = L - nc * ck
    n_out = n_out_i.astype(f32)
    t_row = j * tq + lax.broadcasted_iota(i32, (1, tq), 1)

    for hp in range(IDX_HEADS // 2):
        for e in range(2):
            h = 2 * hp + e
            qp_ref[hp, e * tq:(e + 1) * tq, :] = iq_ref[0, :, h * IDX_DIM:(h + 1) * IDX_DIM]

    def score_chunk(c, carry):
        off = pl.multiple_of(c * ck, ck)
        ikc = ik_ref[0, pl.ds(off, ck), :]
        acc = jnp.zeros((ck, tq), f32)
        for hp in range(IDX_HEADS // 2):
            s2 = _dot_nt(ikc, qp_ref[hp])
            acc = acc + jnp.maximum(s2[:, :tq], 0.0) * iw_ref[0, 2 * hp:2 * hp + 1, :]
            acc = acc + jnp.maximum(s2[:, tq:], 0.0) * iw_ref[0, 2 * hp + 1:2 * hp + 2, :]
        acc = jnp.where(acc == 0.0, 0.0, acc)
        sidx = off + lax.broadcasted_iota(i32, (ck, tq), 0)
        key_ref[pl.ds(off, ck), :] = jnp.where(sidx <= t_row, acc, NEG)
        return carry

    lax.fori_loop(0, nc, score_chunk, 0)

    def as_f32(key):
        return lax.bitcast_convert_type(key ^ ((key >> 31) & 0x7FFFFFFF), f32)

    def count(pred_fn):
        def body(c, cnt):
            off = pl.multiple_of(c * ck, ck)
            k = key_ref[pl.ds(off, ck), :]
            sidx = off + lax.broadcasted_iota(i32, (ck, tq), 0)
            return cnt + jnp.sum(pred_fn(k, sidx).reshape(ck // 64, 64, tq), axis=0)
        part = lax.fori_loop(0, nc, body, jnp.zeros((64, tq), f32))
        return jnp.sum(part, axis=0, keepdims=True)

    kf = float(ksel)

    def bit_body(i, carry):
        thr, cnt_thr = carry
        cand = thr + lax.shift_left(jnp.int32(1), 31 - i)
        cand_f = as_f32(cand)
        cnt = count(lambda k, s: jnp.where(k >= cand_f, 1.0, 0.0)) + jnp.where(_KEY_NEG >= cand, n_out, 0.0)
        ok = cnt >= kf
        return jnp.where(ok, cand, thr), jnp.where(ok, cnt, cnt_thr)

    thr_key, cnt_ge = lax.fori_loop(0, 32, bit_body, (jnp.full((1, tq), -2 ** 31, i32),
                                                       jnp.full((1, tq), float(L), f32)))
    thr = as_f32(thr_key)
    cnt_gt = count(lambda k, s: jnp.where(k > thr, 1.0, 0.0)) + jnp.where(_KEY_NEG > thr_key, n_out, 0.0)
    need = kf - cnt_gt

    j_ref[...] = jnp.full((1, tq), L, i32)

    @pl.when(jnp.max(cnt_ge) > kf)
    def _():
        nbits = L.bit_length() - 1

        def jbit(i, cur):
            cand = cur | lax.shift_left(jnp.int32(1), nbits - 1 - i)
            f = count(lambda k, s: jnp.where(k == thr, jnp.where(s < cand, 1.0, 0.0), 0.0))
            f = f + jnp.where(thr_key == _KEY_NEG, jnp.clip(cand - nc * ck, 0, n_out_i).astype(f32), 0.0)
            return jnp.where(f < need, cand, cur)

        j_ref[...] = lax.fori_loop(0, nbits, jbit, jnp.zeros((1, tq), i32))

    jlast = j_ref[...]

    def write(c, carry):
        off = pl.multiple_of(c * ck, ck)
        k = key_ref[pl.ds(off, ck), :]
        sidx = off + lax.broadcasted_iota(i32, (ck, tq), 0)
        sel = jnp.where(k > thr, _INF, jnp.where(k == thr, jnp.where(sidx <= jlast, _INF, NEG), NEG))
        o_ref[0, pl.ds(off, ck), :] = jnp.where(sidx <= t_row, sel, NEG)
        return carry

    lax.fori_loop(0, nc, write, 0)

    def write_rest(c, carry):
        off = pl.multiple_of(c * ck, ck)
        o_ref[0, pl.ds(off, ck), :] = jnp.full((ck, tq), NEG, f32)
        return carry

    lax.fori_loop(nc, nck, write_rest, 0)


def _dsa_mask(ik, iq, iw_t):
    B, L, _ = iq.shape
    tq = 128
    ksel = min(DSA_TOPK, L // 4)
    return pl.pallas_call(
        functools.partial(_dsa_mask_kernel, L=L, tq=tq, ksel=ksel),
        grid=(B, L // tq),
        in_specs=[pl.BlockSpec((1, L, IDX_DIM), lambda b, j: (b, 0, 0)),
                  pl.BlockSpec((1, tq, IDX_HEADS * IDX_DIM), lambda b, j: (b, j, 0)),
                  pl.BlockSpec((1, IDX_HEADS, tq), lambda b, j: (b, 0, j))],
        out_specs=pl.BlockSpec((1, L, tq), lambda b, j: (b, 0, j)),
        out_shape=jax.ShapeDtypeStruct((B, L, L), f32),
        scratch_shapes=[pltpu.VMEM((L, tq), f32), pltpu.VMEM((IDX_HEADS // 2, 2 * tq, IDX_DIM), bf16),
                        pltpu.VMEM((1, tq), i32)],
        compiler_params=_cparams(("parallel", "parallel")),
    )(ik, iq, iw_t)


def _col_reduce(x, op):
    r, c = x.shape
    if r > 64:
        x = op(x.reshape(r // 64, 64, c), axis=0)
    return op(x, axis=0, keepdims=True)


def _flash_group(ss, cap, v_ts, m_refs, l_refs, acc_refs):
    n = len(ss)
    if cap is not None:
        ss = [jnp.minimum(s, cap) for s in ss]
    m_prev = [r[...] for r in m_refs]
    m_new = [jnp.maximum(m_prev[i], _col_reduce(ss[i], jnp.max)) for i in range(n)]
    alpha = [jnp.exp2(m_prev[i] - m_new[i]) for i in range(n)]
    ps = [jnp.exp2(ss[i] - m_new[i]) for i in range(n)]
    l_new = [alpha[i] * l_refs[i][...] + _col_reduce(ps[i], jnp.sum) for i in range(n)]
    pv = [jnp.dot(v_ts[i], ps[i].astype(bf16), preferred_element_type=f32) for i in range(n)]
    for i in range(n):
        acc_new = acc_refs[i][...] * alpha[i] + pv[i]
        if cap is not None:
            dead = m_new[i] <= NEG
            l_new[i] = jnp.where(dead, 0.0, l_new[i])
            acc_new = jnp.where(dead, 0.0, acc_new)
        l_refs[i][...] = l_new[i]
        acc_refs[i][...] = acc_new
        m_refs[i][...] = m_new[i]


def _flash_heads(score_fns, cap, v_ts, m_ref, l_ref, acc_ref, group):
    n = len(score_fns)
    groups = [list(range(g, min(g + group, n))) for g in range(0, n, group)]
    nxt = [score_fns[h]() for h in groups[0]]
    for gi, hs in enumerate(groups):
        ss = nxt
        if gi + 1 < len(groups):
            nxt = [score_fns[h]() for h in groups[gi + 1]]
        _flash_group(ss, cap, [v_ts[h] for h in hs], [m_ref.at[h] for h in hs],
                     [l_ref.at[h] for h in hs], [acc_ref.at[h] for h in hs])


def _flash_init(m_ref, l_ref, acc_ref):
    m_ref[...] = jnp.full(m_ref.shape, NEG, f32)
    l_ref[...] = jnp.zeros(l_ref.shape, f32)
    acc_ref[...] = jnp.zeros(acc_ref.shape, f32)


def _dsa_attn_kernel(q_ref, k_ref, vt_ref, mask_ref, o_ref, m_ref, l_ref, acc_ref, *, tq, tk, nkc):
    j = pl.program_id(1)
    c = pl.program_id(2)

    @pl.when(c == 0)
    def _():
        _flash_init(m_ref, l_ref, acc_ref)

    @pl.when(c * tk < (j + 1) * tq)
    def _():
        cap = mask_ref[0]

        def scores(h):
            sl = slice(h * HEAD_DIM, (h + 1) * HEAD_DIM)
            return lambda: _dot_nt(k_ref[0, :, sl], q_ref[0, :, sl])

        v_ts = [vt_ref[0, h, 0] for h in range(A_HEADS)]
        _flash_heads([scores(h) for h in range(A_HEADS)], cap, v_ts, m_ref, l_ref, acc_ref, _HEAD_GROUP)

    @pl.when(c == nkc - 1)
    def _():
        for h in range(A_HEADS):
            o = acc_ref[h] / jnp.maximum(l_ref[h], 1e-30)
            o_ref[0, :, h * HEAD_DIM:(h + 1) * HEAD_DIM] = o.T.astype(o_ref.dtype)


def _dsa_attn(q, k, v_t, mask_t):
    B, L, W = q.shape
    tq, tk = 512, 512
    nkc = L // tk

    def last(j):
        return ((j + 1) * tq - 1) // tk

    return pl.pallas_call(
        functools.partial(_dsa_attn_kernel, tq=tq, tk=tk, nkc=nkc),
        grid=(B, L // tq, nkc),
        in_specs=[pl.BlockSpec((1, tq, W), lambda b, j, c: (b, j, 0)),
                  pl.BlockSpec((1, tk, W), lambda b, j, c: (b, jnp.minimum(c, last(j)), 0)),
                  pl.BlockSpec((1, A_HEADS, 1, HEAD_DIM, tk),
                               lambda b, j, c: (b, 0, jnp.minimum(c, last(j)), 0, 0)),
                  pl.BlockSpec((1, tk, tq), lambda b, j, c: (b, jnp.minimum(c, last(j)), j))],
        out_specs=pl.BlockSpec((1, tq, W), lambda b, j, c: (b, j, 0)),
        out_shape=jax.ShapeDtypeStruct((B, L, W), bf16),
        scratch_shapes=[pltpu.VMEM((A_HEADS, 1, tq), f32), pltpu.VMEM((A_HEADS, 1, tq), f32),
                        pltpu.VMEM((A_HEADS, HEAD_DIM, tq), f32)],
        compiler_params=_cparams(("parallel", "parallel", "arbitrary")),
    )(q, k, v_t, mask_t)


def _nsa_cmp_kernel(x_ref, w1_ref, w2_ref, pe_ref, o_ref):
    x = x_ref[0, 0, 0]
    w1 = w1_ref[0]
    half = CMP_STRIDE * HEAD_DIM
    a = jnp.dot(x, w1[:half], preferred_element_type=f32)
    b = jnp.dot(x, w1[half:], preferred_element_type=f32)
    pe = jnp.dot(pe_ref[0], w1, preferred_element_type=f32)[0:1]
    n = a.shape[0]
    pre = a + pltpu.roll(b, n - 1, 0) + pe
    act = pre * _sigmoid(pre)
    o_ref[0, 0, 0] = jnp.dot(act.astype(bf16), w2_ref[0], preferred_element_type=f32).astype(o_ref.dtype)


def _nsa_compress(x, w1, w2, pe, l):
    B, _, G, n, wd = x.shape
    return pl.pallas_call(
        _nsa_cmp_kernel,
        grid=(B, 2, G),
        in_specs=[pl.BlockSpec((1, 1, 1, n, wd), lambda b, t, g: (b, t, g, 0, 0)),
                  pl.BlockSpec((None, 1, CMP_LEN * HEAD_DIM, HEAD_DIM), lambda b, t, g: (l, t, 0, 0)),
                  pl.BlockSpec((None, 1, HEAD_DIM, HEAD_DIM), lambda b, t, g: (l, t, 0, 0)),
                  pl.BlockSpec((None, 1, 8, CMP_LEN * HEAD_DIM), lambda b, t, g: (l, t, 0, 0))],
        out_specs=pl.BlockSpec((1, 1, 1, n, HEAD_DIM), lambda b, t, g: (b, t, g, 0, 0)),
        out_shape=jax.ShapeDtypeStruct((B, 2, G, n, HEAD_DIM), bf16),
        compiler_params=_cparams(("parallel", "parallel", "parallel")),
    )(x, w1, w2, pe)


def _nsa_attn_kernel(q_ref, kc_ref, vct_ref, ks_ref, vst_ref, kw_ref, vwt_ref, g_ref, cov_ref, o_ref,
                     m_ref, l_ref, acc_ref, out_ref, sel_ref, *, L, tq, tk, n_sel):
    j = pl.program_id(2)
    t_row = j * tq + lax.broadcasted_iota(i32, (1, tq), 1)
    n_cmp = kc_ref.shape[2]
    n_slc = L // SLC_LEN
    qs = [q_ref[0, :, hh * HEAD_DIM:(hh + 1) * HEAD_DIM] for hh in range(B_HPG)]

    def gate(hh, i):
        return g_ref[0, 0, hh * 3 + i:hh * 3 + i + 1, :]

    kc = kc_ref[0, 0]
    vct = vct_ref[0, 0]
    cend = lax.broadcasted_iota(i32, (n_cmp, tq), 0) * CMP_STRIDE + (CMP_LEN - 1)
    mc = cend <= t_row
    ss = [jnp.where(mc, _dot_nt(kc, qs[hh]), NEG) for hh in range(B_HPG)]
    mxs = [_col_reduce(s, jnp.max) for s in ss]
    ps = [jnp.where(mc, jnp.exp2(ss[hh] - mxs[hh]), 0.0) for hh in range(B_HPG)]
    inv = [1.0 / jnp.maximum(_col_reduce(p, jnp.sum), 1e-30) for p in ps]
    pvs = [jnp.dot(vct, p.astype(bf16), preferred_element_type=f32) for p in ps]
    psum = ps[0] * inv[0]
    for hh in range(1, B_HPG):
        psum = psum + ps[hh] * inv[hh]
    for hh in range(B_HPG):
        out_ref[hh] = (gate(hh, 0) * inv[hh]) * pvs[hh]
    imp = jnp.dot(cov_ref[...], psum, preferred_element_type=f32, precision=lax.Precision.HIGHEST)
    blk = lax.broadcasted_iota(i32, (n_slc, tq), 0)
    cur = t_row >> (SLC_LEN.bit_length() - 1)
    forced = (blk == 0) | (blk == cur) | (blk == cur - 1)
    imp = jnp.where(forced, FORCE, jnp.where(blk <= cur, imp, NEG))
    rank = jnp.zeros((n_slc, tq), f32)
    for r in range(n_slc):
        row = imp[r:r + 1, :]
        rank = rank + jnp.where(row > imp, 1.0, jnp.where(row == imp, jnp.where(blk > r, 1.0, 0.0), 0.0))
    sel_ref[...] = jnp.where(rank < float(n_sel), _INF, NEG)

    def finish(i):
        for hh in range(B_HPG):
            o = acc_ref[hh] / jnp.maximum(l_ref[hh], 1e-30)
            out_ref[hh] = out_ref[hh] + gate(hh, i) * o

    def run_branch(k_ref, vt_ref, c_lo, c_hi, mask_fn):
        for hh in range(B_HPG):
            _flash_init(m_ref.at[hh], l_ref.at[hh], acc_ref.at[hh])

        def body(c, carry):
            off = pl.multiple_of(c * tk, tk)
            kch = k_ref[0, pl.ds(off, tk), :]
            vch = vt_ref[0, 0, c]
            sidx = off + lax.broadcasted_iota(i32, (tk, tq), 0)
            cap = mask_fn(c, sidx)
            fns = [functools.partial(_dot_nt, kch, qs[hh]) for hh in range(B_HPG)]
            _flash_heads(fns, cap, [vch] * B_HPG, m_ref, l_ref, acc_ref, _HEAD_GROUP)
            return carry

        lax.fori_loop(c_lo, c_hi, body, 0)

    bpc = tk // SLC_LEN

    def slc_mask(c, sidx):
        rows = [jnp.broadcast_to(sel_ref[pl.ds(c * bpc + b, 1), :], (SLC_LEN, tq)) for b in range(bpc)]
        return jnp.where(sidx <= t_row, jnp.concatenate(rows, axis=0), NEG)

    c_hi = ((j + 1) * tq + tk - 1) // tk
    run_branch(ks_ref, vst_ref, 0, c_hi, slc_mask)
    finish(1)

    def win_mask(c, sidx):
        d = t_row - sidx
        return jnp.where(d >= 0, jnp.where(d < WIN_LEN, _INF, NEG), NEG)

    c_lo = jnp.maximum(j * tq - (WIN_LEN - 1), 0) // tk
    run_branch(kw_ref, vwt_ref, c_lo, c_hi, win_mask)
    finish(2)

    for hh in range(B_HPG):
        o_ref[0, :, hh * HEAD_DIM:(hh + 1) * HEAD_DIM] = out_ref[hh].T.astype(o_ref.dtype)


def _nsa_attn(bq, kc, vc_t, bk, bv_tc, gates_t, cov_t):
    B, L, _ = bq.shape
    G = B_KV_GROUPS
    tq, tk = 512, 512
    n_cmp = kc.shape[2]
    n_slc = L // SLC_LEN
    n_sel = min(SLC_TOPN, n_slc)
    gw = B_HPG * HEAD_DIM
    nc = L // tk
    return pl.pallas_call(
        functools.partial(_nsa_attn_kernel, L=L, tq=tq, tk=tk, n_sel=n_sel),
        grid=(B, G, L // tq),
        in_specs=[pl.BlockSpec((1, tq, gw), lambda b, g, j: (b, j, g)),
                  pl.BlockSpec((1, 1, n_cmp, HEAD_DIM), lambda b, g, j: (b, g, 0, 0)),
                  pl.BlockSpec((1, 1, HEAD_DIM, n_cmp), lambda b, g, j: (b, g, 0, 0)),
                  pl.BlockSpec((1, L, HEAD_DIM), lambda b, g, j: (b, 0, 2 + g)),
                  pl.BlockSpec((1, 1, nc, HEAD_DIM, tk), lambda b, g, j: (b, g, 0, 0, 0)),
                  pl.BlockSpec((1, L, HEAD_DIM), lambda b, g, j: (b, 0, 4 + g)),
                  pl.BlockSpec((1, 1, nc, HEAD_DIM, tk), lambda b, g, j: (b, B_KV_GROUPS + g, 0, 0, 0)),
                  pl.BlockSpec((1, 1, 3 * B_HPG, tq), lambda b, g, j: (b, g, 0, j)),
                  pl.BlockSpec((n_slc, n_cmp), lambda b, g, j: (0, 0))],
        out_specs=pl.BlockSpec((1, tq, gw), lambda b, g, j: (b, j, g)),
        out_shape=jax.ShapeDtypeStruct((B, L, B_HEADS * HEAD_DIM), bf16),
        scratch_shapes=[pltpu.VMEM((B_HPG, 1, tq), f32), pltpu.VMEM((B_HPG, 1, tq), f32),
                        pltpu.VMEM((B_HPG, HEAD_DIM, tq), f32), pltpu.VMEM((B_HPG, HEAD_DIM, tq), f32),
                        pltpu.VMEM((n_slc, tq), f32)],
        compiler_params=_cparams(("parallel", "parallel", "parallel")),
    )(bq, kc, vc_t, bk, bv_tc, bk, bv_tc, gates_t, cov_t)


def _diff_attn_kernel(q_ref, k_ref, vt_ref, lam_ref, g_ref, o_ref, m_ref, l_ref, acc_ref,
                      *, tq, tk, lam_init):
    j = pl.program_id(2)
    t_row = j * tq + lax.broadcasted_iota(i32, (1, tq), 1)
    qs = [q_ref[0, :, mi * C_DIM:(mi + 1) * C_DIM] for mi in range(2)]
    for mi in range(2):
        _flash_init(m_ref.at[mi], l_ref.at[mi], acc_ref.at[mi])

    def chunk(c, lo, width, masked):
        off = pl.multiple_of(c * tk + lo, tq)
        vch = vt_ref[0, 0, c, :, lo:lo + width]
        cap = None
        if masked:
            sidx = off + lax.broadcasted_iota(i32, (width, tq), 0)
            cap = jnp.where(sidx <= t_row, _INF, NEG)
        ss = [_dot_nt(k_ref[0, pl.ds(off, width), mi * C_DIM:(mi + 1) * C_DIM], qs[mi]) for mi in range(2)]
        _flash_group(ss, cap, [vch] * 2, [m_ref.at[mi] for mi in range(2)],
                     [l_ref.at[mi] for mi in range(2)], [acc_ref.at[mi] for mi in range(2)])

    def body(c, carry):
        chunk(c, 0, tk, False)
        return carry

    c_diag = (j * tq) // tk
    lax.fori_loop(0, c_diag, body, 0)

    @pl.when(j % 2 == 0)
    def _():
        chunk(c_diag, 0, tq, True)

    @pl.when(j % 2 == 1)
    def _():
        chunk(c_diag, 0, tq, False)
        chunk(c_diag, tq, tq, True)

    lam = lam_ref[...]
    lam_val = (jnp.exp(jnp.sum(lam[0:1] * lam[1:2], axis=1, keepdims=True))
               - jnp.exp(jnp.sum(lam[2:3] * lam[3:4], axis=1, keepdims=True)) + lam_init)
    o = (acc_ref[0] / jnp.maximum(l_ref[0], 1e-30)
         - lam_val * (acc_ref[1] / jnp.maximum(l_ref[1], 1e-30)))
    ms = jnp.mean(o * o, axis=0, keepdims=True)
    y = o * lax.rsqrt(ms + 1e-6) * g_ref[...] * (1.0 - lam_init)
    o_ref[0] = y.T.astype(o_ref.dtype)


def _diff_attn(cq, ck, cv_tc, lam, g_col, lam_init):
    B, L, _ = cq.shape
    tq, tk = _DIFF_TQ, cv_tc.shape[-1]
    assert tk == 2 * tq
    hw = 2 * C_DIM
    nc = L // tk
    return pl.pallas_call(
        functools.partial(_diff_attn_kernel, tq=tq, tk=tk, lam_init=lam_init),
        grid=(B, C_HEADS, L // tq),
        in_specs=[pl.BlockSpec((1, tq, hw), lambda b, h, j: (b, j, h)),
                  pl.BlockSpec((1, L, hw), lambda b, h, j: (b, 0, h)),
                  pl.BlockSpec((1, 1, nc, hw, tk), lambda b, h, j: (b, h, 0, 0, 0)),
                  pl.BlockSpec((4, C_DIM), lambda b, h, j: (0, 0)),
                  pl.BlockSpec((hw, 1), lambda b, h, j: (0, 0))],
        out_specs=pl.BlockSpec((1, tq, hw), lambda b, h, j: (b, j, h)),
        out_shape=jax.ShapeDtypeStruct((B, L, C_HEADS * hw), bf16),
        scratch_shapes=[pltpu.VMEM((2, 1, tq), f32), pltpu.VMEM((2, 1, tq), f32),
                        pltpu.VMEM((2, hw, tq), f32)],
        compiler_params=_cparams(("parallel", "parallel", "parallel")),
    )(cq, ck, cv_tc, lam, g_col)


def _merge_kernel(ya_ref, yb_ref, yc_ref, w_ref, g0_ref, g1_ref, g2_ref, o_ref, wb_ref):
    @pl.when(pl.program_id(1) == 0)
    def _():
        wb_ref[...] = w_ref[...].astype(bf16)

    acc = None
    for r, (y_ref, g_ref) in enumerate(((ya_ref, g0_ref), (yb_ref, g1_ref), (yc_ref, g2_ref))):
        br = jnp.dot(y_ref[...], wb_ref[r], preferred_element_type=f32)
        t = _sigmoid(g_ref[...].astype(f32)) * br
        acc = t if acc is None else acc + t
    o_ref[...] = acc.astype(o_ref.dtype)


def _merge(ya, yb, yc, w_br, l, h):
    m, kw = ya.shape
    tm, tn = 512, 512
    npb = D_MODEL // tn
    yspec = pl.BlockSpec((tm, kw), lambda j, i: (i, 0))

    def gspec(r):
        return pl.BlockSpec((tm, tn), lambda j, i, _r=r: (i, _r * npb + j))

    return pl.pallas_call(
        _merge_kernel,
        grid=(npb, m // tm),
        in_specs=[yspec, yspec, yspec,
                  pl.BlockSpec((None, N_BRANCH, kw, tn), lambda j, i: (l, 0, 0, j)),
                  gspec(0), gspec(1), gspec(2)],
        out_specs=pl.BlockSpec((tm, tn), lambda j, i: (i, j)),
        out_shape=jax.ShapeDtypeStruct((m, D_MODEL), bf16),
        scratch_shapes=[pltpu.VMEM((N_BRANCH, kw, tn), bf16)],
        compiler_params=_cparams(("parallel", "arbitrary")),
    )(ya, yb, yc, w_br, h, h, h)


def _mm_res_ln_kernel(a_ref, w_ref, x_ref, gate_ref, lg_ref, lb_ref, sc_ref, sh_ref, xo_ref, *u_refs,
                      alpha, sub):
    tm = a_ref.shape[0]
    for r in range(tm // sub):
        rows = slice(r * sub, (r + 1) * sub)
        y = jnp.dot(a_ref[rows, :], w_ref[...], preferred_element_type=f32)
        z = alpha * x_ref[rows, :] + gate_ref[0] * y
        xn = _ln_rows(z, 1e-5) * lg_ref[...] + lb_ref[...]
        xo_ref[rows, :] = xn
        if u_refs:
            u_refs[0][rows, :] = (_ln_rows(xn, 1e-5) * (1.0 + sc_ref[0]) + sh_ref[0]).astype(bf16)


def _mm_res_ln(a, w, l, x2, gate, ln_g, ln_b, sc, sh, L, alpha, emit_u):
    m, kdim = a.shape
    d = w.shape[2]
    tm = _RES_LN_TM if kdim * d * 2 <= 16 * 1024 * 1024 else _RES_LN_SUB
    per_b = L // tm
    bspec = pl.BlockSpec((1, 1, d), lambda i: (i // per_b, 0, 0))
    vspec = pl.BlockSpec((1, d), lambda i: (0, 0))
    rspec = pl.BlockSpec((tm, d), lambda i: (i, 0))
    out_shape = [jax.ShapeDtypeStruct((m, d), f32)]
    out_specs = [rspec]
    if emit_u:
        out_shape.append(jax.ShapeDtypeStruct((m, d), bf16))
        out_specs.append(rspec)
    res = pl.pallas_call(
        functools.partial(_mm_res_ln_kernel, alpha=alpha, sub=_RES_LN_SUB),
        grid=(m // tm,),
        in_specs=[pl.BlockSpec((tm, kdim), lambda i: (i, 0)),
                  pl.BlockSpec((None, kdim, d), lambda i: (l, 0, 0), pipeline_mode=pl.Buffered(1)),
                  rspec, bspec, vspec, vspec, bspec, bspec],
        out_specs=out_specs,
        out_shape=out_shape,
        compiler_params=_cparams(("parallel",), 60 * 1024 * 1024),
    )(a, w, x2, gate, ln_g.reshape(1, d), ln_b.reshape(1, d), sc, sh)
    return res if emit_u else (res[0], None)


def _ffn_in_kernel(a_ref, wg_ref, wu_ref, o_ref, wgb_ref, wub_ref):
    @pl.when(pl.program_id(1) == 0)
    def _():
        wgb_ref[...] = wg_ref[...].astype(bf16)
        wub_ref[...] = wu_ref[...].astype(bf16)

    a = a_ref[...]
    g = jnp.dot(a, wgb_ref[...], preferred_element_type=f32)
    u = jnp.dot(a, wub_ref[...], preferred_element_type=f32)
    o_ref[...] = (g * _sigmoid(g) * u).astype(o_ref.dtype)


def _ffn_in(a, w, l):
    m, k = a.shape
    tm, tn = 512, 512
    nb = D_FF // tn
    return pl.pallas_call(
        _ffn_in_kernel,
        grid=(nb, m // tm),
        in_specs=[pl.BlockSpec((tm, k), lambda j, i: (i, 0)),
                  pl.BlockSpec((None, k, tn), lambda j, i: (l, 0, j)),
                  pl.BlockSpec((None, k, tn), lambda j, i: (l, 0, nb + j))],
        out_specs=pl.BlockSpec((tm, tn), lambda j, i: (i, j)),
        out_shape=jax.ShapeDtypeStruct((m, D_FF), bf16),
        scratch_shapes=[pltpu.VMEM((k, tn), bf16), pltpu.VMEM((k, tn), bf16)],
        compiler_params=_cparams(("parallel", "arbitrary")),
    )(a, w, w)


def _rope_tables(L, d):
    r = d // 4
    half = r // 2
    inv = ROPE_THETA ** (-(jnp.arange(half, dtype=f32) * 2.0) / r)
    ang = jnp.arange(L).astype(f32)[:, None] * inv[None, :]
    cos, sin = jnp.cos(ang), jnp.sin(ang)
    z = jnp.zeros((L, d - r), f32)
    zh = jnp.zeros((L, half), f32)
    c = jnp.concatenate([cos, cos, jnp.ones((L, d - r), f32)], axis=1)
    s1 = jnp.concatenate([zh, sin, z], axis=1)
    s2 = jnp.concatenate([-sin, zh, z], axis=1)
    rep = 128 // d
    return tuple(jnp.tile(t, (1, rep)) for t in (c, s1, s2))


def _pack_w_in_kernel(w_ref, o_ref, ov_ref):
    def cp(src, width, dst):
        for o in range(0, width, 1024):
            wd = min(1024, width - o)
            o_ref[:, dst + o:dst + o + wd] = w_ref[src + o:src + o + wd, :].T.astype(bf16)

    def bkv(i, kv, g):
        return _O_BKV + ((i * 2 + kv) * B_KV_GROUPS + g) * HEAD_DIM

    cp(_O_GL, N_BRANCH * D_MODEL, _P_GL)
    for src, dst in ((_O_AQ, _P_AQ), (_O_IQ, _P_IQ), (_O_BQ, _P_BQ), (_O_CQ, _P_CQ), (_O_CK, _P_CK)):
        cp(src, 1024, dst)
    for i in range(3):
        for g in range(B_KV_GROUPS):
            cp(bkv(i, 0, g), HEAD_DIM, _P_BK + (i * B_KV_GROUPS + g) * HEAD_DIM)
    for g in range(B_KV_GROUPS):
        cp(bkv(0, 1, g), HEAD_DIM, _P_BVC + g * HEAD_DIM)
    cp(_O_ALAT, A_LATENT, _P_ALAT)
    ov_ref[_PV_CV:_PV_CV + 1024, :] = w_ref[_O_CV:_O_CV + 1024, :].astype(bf16)
    for i in (1, 2):
        for g in range(B_KV_GROUPS):
            dst = _PV_BV + ((i - 1) * B_KV_GROUPS + g) * HEAD_DIM
            ov_ref[dst:dst + HEAD_DIM, :] = w_ref[bkv(i, 1, g):bkv(i, 1, g) + HEAD_DIM, :].astype(bf16)
    lane = lax.broadcasted_iota(i32, (w_ref.shape[1], 128), 1)
    assert _O_IW == _O_IK + IDX_DIM and _O_IK % 8 == 0 and _O_BG % 8 == 0
    blk = w_ref[_O_IK:_O_IK + 128, :].T
    o_ref[:, _P_IKW:_P_IKW + 128] = jnp.where(lane < IDX_DIM + IDX_HEADS, blk, 0.0).astype(bf16)
    blk = w_ref[_O_BG:_O_BG + 128, :].T
    o_ref[:, _P_BG:_P_BG + 128] = jnp.where(lane < 3 * B_HEADS, blk, 0.0).astype(bf16)


def _pack_w_in(w_in):
    depth, k, n = w_in.shape
    w_t = jnp.swapaxes(w_in, 1, 2)
    tc = 128
    return pl.pallas_call(
        _pack_w_in_kernel,
        grid=(depth, k // tc),
        in_specs=[pl.BlockSpec((None, n, tc), lambda l, i: (l, 0, i))],
        out_specs=[pl.BlockSpec((None, tc, _P_TOT), lambda l, i: (l, i, 0)),
                   pl.BlockSpec((None, _PV_TOT, tc), lambda l, i: (l, 0, i))],
        out_shape=[jax.ShapeDtypeStruct((depth, k, _P_TOT), bf16),
                   jax.ShapeDtypeStruct((depth, _PV_TOT, k), bf16)],
        compiler_params=_cparams(("parallel", "parallel")),
    )(w_t)


def _cover_t(L):
    n_cmp_pad = L // CMP_STRIDE
    starts = np.arange(n_cmp_pad) * CMP_STRIDE
    slc_start = np.arange(L // SLC_LEN) * SLC_LEN
    cover = ((starts[:, None] < slc_start[None, :] + SLC_LEN)
             & (starts[:, None] + CMP_LEN > slc_start[None, :])).astype(np.float32)
    n_cmp = (L - CMP_LEN) // CMP_STRIDE + 1
    cover[n_cmp:] = 0.0
    return jnp.asarray(cover.T)


def _token_mixing(u, h, B, L, l, lw, tabs128, tabs64, cov_t, lam_init):
    M = B * L
    G = B_KV_GROUPS
    aq, iq, bq, cq, ck, bk, alat_n, ikw, gates = _prep(h, lw['a_lat_g'], tabs128, tabs64, L)

    aw = A_HEADS * HEAD_DIM
    ak = _mm_rope(alat_n, lw['a_up'], l, aw, tabs128, L, 512, aw)
    av_tc = _proj_t(alat_n, lw['a_up_vt'], l, 0, A_HEADS, HEAD_DIM, 512, B, L)
    ik = ikw[:, :IDX_DIM].astype(bf16).reshape(B, L, IDX_DIM)
    iw_t = ikw[:, IDX_DIM:IDX_DIM + IDX_HEADS].reshape(B, L, IDX_HEADS).transpose(0, 2, 1)
    mask_t = _dsa_mask(ik, iq.reshape(B, L, -1), iw_t)
    ya = _dsa_attn(aq.reshape(B, L, -1), ak.reshape(B, L, -1), av_tc, mask_t)

    bvc = h[:, _P_BVC:_P_BVC + G * HEAD_DIM].reshape(B, L, G, HEAD_DIM)
    bkc = bk[:, :G * HEAD_DIM].reshape(B, L, G, HEAD_DIM)
    n_row = L // CMP_STRIDE
    xcmp = jnp.stack([bkc, bvc], axis=1)
    xcmp = xcmp.transpose(0, 1, 3, 2, 4).reshape(B, 2, G, n_row, CMP_STRIDE * HEAD_DIM)
    cmp_out = _nsa_compress(xcmp, lw['cmp_w1'], lw['cmp_w2'], lw['cmp_pe'], l)
    kc = cmp_out[:, 0]
    vc_t = cmp_out[:, 1].transpose(0, 1, 3, 2)
    bv_tc = _proj_t(u, lw['w_vt'], l, _PV_BV, 2 * G, HEAD_DIM, 512, B, L)
    gates_t = gates[:, :3 * B_HEADS].reshape(B, L, G, 3 * B_HPG).transpose(0, 2, 3, 1)
    yb = _nsa_attn(bq.reshape(B, L, -1), kc, vc_t, bk.reshape(B, L, -1), bv_tc, gates_t, cov_t)

    cv_tc = _proj_t(u, lw['w_vt'], l, _PV_CV, C_HEADS, 2 * C_DIM, _DIFF_TK, B, L)
    yc = _diff_attn(cq.reshape(B, L, -1), ck.reshape(B, L, -1), cv_tc, lw['lam'],
                    lw['c_subln_g'].reshape(2 * C_DIM, 1), lam_init)

    return _merge(ya.reshape(M, -1), yb.reshape(M, -1), yc.reshape(M, -1), lw['w_br'], l, h)


def kernel(x, c, w_ada, b_ada, w_in, a_lat_g, a_up, cmp_w1, cmp_w2, cmp_pe, lam, c_subln_g, w_br, w_o,
           w_ffn_in, w_ffn_out, ln_g, ln_b):
    B, L, D = x.shape
    depth = w_ada.shape[0]
    M = B * L
    alpha = (2 * depth) ** 0.25

    c_pad = jnp.zeros((8, D), f32).at[:B].set(c)
    mod = _ada(c_pad, w_ada, b_ada)[:, :B]
    mods = [[mod[l, :, i * D:(i + 1) * D].reshape(B, 1, D) for i in range(6)] for l in range(depth)]

    tabs128 = _rope_tables(L, HEAD_DIM)
    tabs64 = _rope_tables(L, IDX_DIM)
    cov_t = _cover_t(L)

    w_in_p, w_vt = _pack_w_in(w_in)
    pe_flat = jnp.zeros((depth, 2, 8, CMP_LEN * HEAD_DIM), f32).at[:, :, 0].set(
        cmp_pe.reshape(depth, 2, CMP_LEN * HEAD_DIM)).astype(bf16)
    a_up_vt = jnp.swapaxes(a_up[:, :, A_HEADS * HEAD_DIM:], 1, 2).astype(bf16)
    wb = dict(a_up=a_up.astype(bf16), a_up_vt=a_up_vt, w_vt=w_vt, cmp_w1=cmp_w1.astype(bf16),
              cmp_w2=cmp_w2.astype(bf16), cmp_pe=pe_flat, w_br=w_br)
    w_o_b = w_o.astype(bf16)
    w_fo_b = w_ffn_out.astype(bf16)

    x2 = x.reshape(M, D)
    u = _lnmod(x2, mods[0][1], mods[0][0], L)
    for l in range(depth):
        lam_init = 0.8 - 0.6 * math.exp(-0.3 * l)
        sh_a, sc_a, g_a, sh_f, sc_f, g_f = mods[l]
        lw = dict(wb, a_lat_g=a_lat_g[l], lam=lam[l], c_subln_g=c_subln_g[l])
        h = _mm(u, w_in_p, l, 512, 768, bf16)
        merged = _token_mixing(u, h, B, L, l, lw, tabs128, tabs64, cov_t, lam_init)
        x2, u = _mm_res_ln(merged, w_o_b, l, x2, g_a, ln_g[l, 0], ln_b[l, 0], sc_f, sh_f, L, alpha, True)
        f = _ffn_in(u, w_ffn_in, l)
        last = l == depth - 1
        nsc, nsh = (sc_f, sh_f) if last else (mods[l + 1][1], mods[l + 1][0])
        x2, u = _mm_res_ln(f, w_fo_b, l, x2, g_f, ln_g[l, 1], ln_b[l, 1], nsc, nsh, L, alpha, not last)
    return x2.reshape(B, L, D)
```

```python
import functools
import math

import numpy as np
import jax
import jax.numpy as jnp
from jax import lax
from jax.experimental import pallas as pl
from jax.experimental.pallas import tpu as pltpu

f32 = jnp.float32
bf16 = jnp.bfloat16
i32 = jnp.int32

D_MODEL = 2048
HEAD_DIM = 128
ROPE_THETA = 500000.0
NEG = -1e30
FORCE = 1e6
A_HEADS = 8
A_LATENT = 512
IDX_HEADS = 16
IDX_DIM = 64
DSA_TOPK = 256
B_HEADS = 8
B_KV_GROUPS = 2
B_HPG = B_HEADS // B_KV_GROUPS
CMP_LEN = 32
CMP_STRIDE = 16
SLC_LEN = 64
SLC_TOPN = 16
WIN_LEN = 512
C_HEADS = 4
C_DIM = 128
BRANCH_W = A_HEADS * HEAD_DIM
N_BRANCH = 3
D_FF = int(math.ceil(8 * D_MODEL / 3 / 256)) * 256

_O_AQ = 0
_O_ALAT = _O_AQ + A_HEADS * HEAD_DIM
_O_IQ = _O_ALAT + A_LATENT
_O_IK = _O_IQ + IDX_HEADS * IDX_DIM
_O_IW = _O_IK + IDX_DIM
_O_BQ = _O_IW + IDX_HEADS
_O_BKV = _O_BQ + B_HEADS * HEAD_DIM
_O_BG = _O_BKV + 3 * 2 * B_KV_GROUPS * HEAD_DIM
_O_CQ = _O_BG + 3 * B_HEADS
_O_CK = _O_CQ + C_HEADS * 2 * C_DIM
_O_CV = _O_CK + C_HEADS * 2 * C_DIM
_O_GL = _O_CV + C_HEADS * 2 * C_DIM
_N_IN = _O_GL + N_BRANCH * D_MODEL

_P_GL = 0
_P_BK = 6144
_P_BVC = 6912
_P_AQ = 7168
_P_IQ = 8192
_P_BQ = 9216
_P_CQ = 10240
_P_CK = 11264
_P_ALAT = 12288
_P_IKW = 12800
_P_BG = 12928
_P_TOT = 13056
_PV_CV = 0
_PV_BV = 1024
_PV_TOT = 1536

_VMEM_LIMIT = 48 * 1024 * 1024
_LOG2E = 1.4426950408889634
_INF = float("inf")
_RES_LN_TM = 512
_RES_LN_SUB = 256
_DIFF_TK = 1024
_DIFF_TQ = 512
_HEAD_GROUP = 2


def _cparams(sem, vmem=_VMEM_LIMIT):
    return pltpu.CompilerParams(dimension_semantics=sem, vmem_limit_bytes=vmem)


def _sigmoid(x):
    return 1.0 / (1.0 + jnp.exp(-x))


def _dot_nt(a, b):
    return lax.dot_general(a, b, (((1,), (1,)), ((), ())), preferred_element_type=f32)


def _ada_kernel(c_ref, w_ref, b_ref, o_ref):
    c = c_ref[...]
    cs = c * _sigmoid(c)
    o_ref[0] = jnp.dot(cs, w_ref[0], preferred_element_type=f32,
                       precision=lax.Precision.HIGHEST) + b_ref[0]


def _ada(c_pad, w_ada, b_ada):
    depth, d, n = w_ada.shape
    tn = 512
    return pl.pallas_call(
        _ada_kernel,
        grid=(depth, n // tn),
        in_specs=[pl.BlockSpec((8, d), lambda l, j: (0, 0)),
                  pl.BlockSpec((1, d, tn), lambda l, j: (l, 0, j)),
                  pl.BlockSpec((1, 1, tn), lambda l, j: (l, 0, j))],
        out_specs=pl.BlockSpec((1, 8, tn), lambda l, j: (l, 0, j)),
        out_shape=jax.ShapeDtypeStruct((depth, 8, n), f32),
        compiler_params=_cparams(("parallel", "parallel")),
    )(c_pad, w_ada, b_ada.reshape(depth, 1, n))


def _ln_rows(x, eps):
    mu = jnp.mean(x, axis=-1, keepdims=True)
    d = x - mu
    var = jnp.mean(d * d, axis=-1, keepdims=True)
    return d * lax.rsqrt(var + eps)


def _lnmod_kernel(x_ref, sc_ref, sh_ref, o_ref):
    y = _ln_rows(x_ref[...], 1e-5)
    o_ref[...] = (y * (1.0 + sc_ref[0]) + sh_ref[0]).astype(o_ref.dtype)


def _lnmod(x2, sc, sh, L):
    m, d = x2.shape
    tm = 512
    per_b = L // tm
    return pl.pallas_call(
        _lnmod_kernel,
        grid=(m // tm,),
        in_specs=[pl.BlockSpec((tm, d), lambda i: (i, 0)),
                  pl.BlockSpec((1, 1, d), lambda i: (i // per_b, 0, 0)),
                  pl.BlockSpec((1, 1, d), lambda i: (i // per_b, 0, 0))],
        out_specs=pl.BlockSpec((tm, d), lambda i: (i, 0)),
        out_shape=jax.ShapeDtypeStruct((m, d), bf16),
        compiler_params=_cparams(("parallel",)),
    )(x2, sc, sh)


def _mm_kernel(a_ref, w_ref, o_ref):
    o_ref[...] = jnp.dot(a_ref[...], w_ref[...], preferred_element_type=f32).astype(o_ref.dtype)


def _mm(a, w, l, tm, tn, out_dtype):
    m, k = a.shape
    n = w.shape[2]
    return pl.pallas_call(
        _mm_kernel,
        grid=(n // tn, m // tm),
        in_specs=[pl.BlockSpec((tm, k), lambda j, i: (i, 0)),
                  pl.BlockSpec((None, k, tn), lambda j, i: (l, 0, j))],
        out_specs=pl.BlockSpec((tm, tn), lambda j, i: (i, j)),
        out_shape=jax.ShapeDtypeStruct((m, n), out_dtype),
        compiler_params=_cparams(("parallel", "parallel")),
    )(a, w)


def _proj_t_kernel(w_ref, a_ref, o_ref):
    n, d = o_ref.shape[1], o_ref.shape[3]
    res = _dot_nt(w_ref[...], a_ref[...])
    for i in range(n):
        o_ref[0, i, 0] = res[i * d:(i + 1) * d, :].astype(o_ref.dtype)


def _proj_t(a, w_t, l, row0, n, d, tk, B, L):
    k = a.shape[1]
    rows = n * d
    nc = L // tk
    return pl.pallas_call(
        _proj_t_kernel,
        grid=(B, nc),
        in_specs=[pl.BlockSpec((None, rows, k), lambda b, c: (l, row0 // rows, 0)),
                  pl.BlockSpec((tk, k), lambda b, c: (b * nc + c, 0))],
        out_specs=pl.BlockSpec((1, n, 1, d, tk), lambda b, c: (b, 0, c, 0, 0)),
        out_shape=jax.ShapeDtypeStruct((B, n, nc, d, tk), bf16),
        compiler_params=_cparams(("parallel", "parallel")),
    )(w_t, a)


def _rope_heads(x, c, s1, s2, half):
    return x * c + pltpu.roll(x, half, 1) * s1 + pltpu.roll(x, 128 - half, 1) * s2


def _mm_rope_kernel(a_ref, w_ref, c_ref, s1_ref, s2_ref, o_ref):
    acc = jnp.dot(a_ref[...], w_ref[...], preferred_element_type=f32)
    c, s1, s2 = c_ref[...], s1_ref[...], s2_ref[...]
    for h in range(acc.shape[1] // 128):
        sl = slice(h * 128, (h + 1) * 128)
        o_ref[:, sl] = _rope_heads(acc[:, sl], c, s1, s2, 16).astype(o_ref.dtype)


def _mm_rope(a, w, l, n, tabs, L, tm, tn):
    m, k = a.shape
    per_b = L // tm
    tspec = pl.BlockSpec((tm, 128), lambda j, i: (i % per_b, 0))
    return pl.pallas_call(
        _mm_rope_kernel,
        grid=(n // tn, m // tm),
        in_specs=[pl.BlockSpec((tm, k), lambda j, i: (i, 0)),
                  pl.BlockSpec((None, k, tn), lambda j, i: (l, 0, j)),
                  tspec, tspec, tspec],
        out_specs=pl.BlockSpec((tm, tn), lambda j, i: (i, j)),
        out_shape=jax.ShapeDtypeStruct((m, n), bf16),
        compiler_params=_cparams(("parallel", "parallel")),
    )(a, w, *tabs)


def _prep_kernel(aq_ref, iq_ref, bq_ref, cq_ref, ck_ref, bk_ref, alat_ref, ikw_ref, bg_ref, alg_ref,
                 c_ref, s1_ref, s2_ref, c6_ref, s16_ref, s26_ref,
                 aq_o, iq_o, bq_o, cq_o, ck_o, bk_o, alat_o, ikw_o, g_o):
    c, s1, s2 = c_ref[...], s1_ref[...], s2_ref[...]
    c6, s16, s26 = c6_ref[...], s16_ref[...], s26_ref[...]
    scale = HEAD_DIM ** -0.5 * _LOG2E

    def rope_all(src, dst, mult, tabs, half):
        for h in range(src.shape[1] // 128):
            sl = slice(h * 128, (h + 1) * 128)
            y = _rope_heads(src[:, sl].astype(f32), *tabs, half)
            if mult != 1.0:
                y = y * mult
            dst[:, sl] = y.astype(dst.dtype)

    rope_all(aq_ref, aq_o, scale, (c, s1, s2), 16)
    rope_all(bq_ref, bq_o, scale, (c, s1, s2), 16)
    rope_all(cq_ref, cq_o, C_DIM ** -0.5 * _LOG2E, (c, s1, s2), 16)
    rope_all(ck_ref, ck_o, 1.0, (c, s1, s2), 16)
    rope_all(bk_ref, bk_o, 1.0, (c, s1, s2), 16)
    rope_all(iq_ref, iq_o, 1.0, (c6, s16, s26), 8)

    a = alat_ref[...].astype(f32)
    ms = jnp.mean(a * a, axis=-1, keepdims=True)
    alat_o[...] = (a * lax.rsqrt(ms + 1e-6) * alg_ref[...]).astype(alat_o.dtype)

    x = ikw_ref[...].astype(f32)
    lane = lax.broadcasted_iota(i32, x.shape, 1)
    isk = lane < IDX_DIM
    mu = jnp.sum(jnp.where(isk, x, 0.0), axis=-1, keepdims=True) * (1.0 / IDX_DIM)
    d = jnp.where(isk, x - mu, 0.0)
    var = jnp.sum(d * d, axis=-1, keepdims=True) * (1.0 / IDX_DIM)
    y = d * lax.rsqrt(var + 1e-5)
    yr = _rope_heads(y, c6, s16, s26, 8)
    ikw_o[...] = jnp.where(isk, yr, x * (IDX_DIM ** -0.5 * IDX_HEADS ** -0.5))

    g_o[...] = _sigmoid(bg_ref[...].astype(f32))


def _prep(h, a_lat_g, tabs128, tabs64, L):
    m = h.shape[0]
    tm = 256
    per_b = L // tm

    def hs(width, off):
        return pl.BlockSpec((tm, width), lambda i, _o=off // width: (i, _o))

    tspec = pl.BlockSpec((tm, 128), lambda i: (i % per_b, 0))

    def os(width):
        return pl.BlockSpec((tm, width), lambda i: (i, 0))

    outs = [(1024, bf16)] * 5 + [(768, bf16), (512, bf16), (128, f32), (128, f32)]
    return pl.pallas_call(
        _prep_kernel,
        grid=(m // tm,),
        in_specs=[hs(1024, _P_AQ), hs(1024, _P_IQ), hs(1024, _P_BQ), hs(1024, _P_CQ), hs(1024, _P_CK),
                  hs(768, _P_BK), hs(512, _P_ALAT), hs(128, _P_IKW), hs(128, _P_BG),
                  pl.BlockSpec((1, A_LATENT), lambda i: (0, 0))] + [tspec] * 6,
        out_specs=[os(w) for w, _ in outs],
        out_shape=[jax.ShapeDtypeStruct((m, w), dt) for w, dt in outs],
        compiler_params=_cparams(("parallel",)),
    )(h, h, h, h, h, h, h, h, h, a_lat_g.reshape(1, A_LATENT), *tabs128, *tabs64)


def _f32_order_key(x):
    b = int(np.float32(x).view(np.int32))
    return b ^ ((b >> 31) & 0x7FFFFFFF)


_KEY_NEG = _f32_order_key(NEG)


def _dsa_mask_kernel(ik_ref, iq_ref, iw_ref, o_ref, key_ref, qp_ref, j_ref, *, L, tq, ksel):
    j = pl.program_id(1)
    ck = 512
    nck = L // ck
    nc = ((j + 1) * tq + ck - 1) // ck
    n_out_i = L - nc * ck
    n_out = n_out_i.astype(f32)
    t_row = j * tq + lax.broadcasted_iota(i32, (1, tq), 1)

    for hp in range(IDX_HEADS // 2):
        for e in range(2):
            h = 2 * hp + e
            qp_ref[hp, e * tq:(e + 1) * tq, :] = iq_ref[0, :, h * IDX_DIM:(h + 1) * IDX_DIM]

    def score_chunk(c, carry):
        off = pl.multiple_of(c * ck, ck)
        ikc = ik_ref[0, pl.ds(off, ck), :]
        acc = jnp.zeros((ck, tq), f32)
        for hp in range(IDX_HEADS // 2):
            s2 = _dot_nt(ikc, qp_ref[hp])
            acc = acc + jnp.maximum(s2[:, :tq], 0.0) * iw_ref[0, 2 * hp:2 * hp + 1, :]
            acc = acc + jnp.maximum(s2[:, tq:], 0.0) * iw_ref[0, 2 * hp + 1:2 * hp + 2, :]
        acc = jnp.where(acc == 0.0, 0.0, acc)
        sidx = off + lax.broadcasted_iota(i32, (ck, tq), 0)
        key_ref[pl.ds(off, ck), :] = jnp.where(sidx <= t_row, acc, NEG)
        return carry

    lax.fori_loop(0, nc, score_chunk, 0)

    def as_f32(key):
        return lax.bitcast_convert_type(key ^ ((key >> 31) & 0x7FFFFFFF), f32)

    def count(pred_fn):
        def body(c, cnt):
            off = pl.multiple_of(c * ck, ck)
            k = key_ref[pl.ds(off, ck), :]
            sidx = off + lax.broadcasted_iota(i32, (ck, tq), 0)
            return cnt + jnp.sum(pred_fn(k, sidx).reshape(ck // 64, 64, tq), axis=0)
        part = lax.fori_loop(0, nc, body, jnp.zeros((64, tq), f32))
        return jnp.sum(part, axis=0, keepdims=True)

    kf = float(ksel)

    def bit_body(i, carry):
        thr, cnt_thr = carry
        cand = thr + lax.shift_left(jnp.int32(1), 31 - i)
        cand_f = as_f32(cand)
        cnt = count(lambda k, s: jnp.where(k >= cand_f, 1.0, 0.0)) + jnp.where(_KEY_NEG >= cand, n_out, 0.0)
        ok = cnt >= kf
        return jnp.where(ok, cand, thr), jnp.where(ok, cnt, cnt_thr)

    thr_key, cnt_ge = lax.fori_loop(0, 32, bit_body, (jnp.full((1, tq), -2 ** 31, i32),
                                                       jnp.full((1, tq), float(L), f32)))
    thr = as_f32(thr_key)
    cnt_gt = count(lambda k, s: jnp.where(k > thr, 1.0, 0.0)) + jnp.where(_KEY_NEG > thr_key, n_out, 0.0)
    need = kf - cnt_gt

    j_ref[...] = jnp.full((1, tq), L, i32)

    @pl.when(jnp.max(cnt_ge) > kf)
    def _():
        nbits = L.bit_length() - 1

        def jbit(i, cur):
            cand = cur | lax.shift_left(jnp.int32(1), nbits - 1 - i)
            f = count(lambda k, s: jnp.where(k == thr, jnp.where(s < cand, 1.0, 0.0), 0.0))
            f = f + jnp.where(thr_key == _KEY_NEG, jnp.clip(cand - nc * ck, 0, n_out_i).astype(f32), 0.0)
            return jnp.where(f < need, cand, cur)

        j_ref[...] = lax.fori_loop(0, nbits, jbit, jnp.zeros((1, tq), i32))

    jlast = j_ref[...]

    def write(c, carry):
        off = pl.multiple_of(c * ck, ck)
        k = key_ref[pl.ds(off, ck), :]
        sidx = off + lax.broadcasted_iota(i32, (ck, tq), 0)
        sel = jnp.where(k > thr, _INF, jnp.where(k == thr, jnp.where(sidx <= jlast, _INF, NEG), NEG))
        o_ref[0, pl.ds(off, ck), :] = jnp.where(sidx <= t_row, sel, NEG)
        return carry

    lax.fori_loop(0, nc, write, 0)

    def write_rest(c, carry):
        off = pl.multiple_of(c * ck, ck)
        o_ref[0, pl.ds(off, ck), :] = jnp.full((ck, tq), NEG, f32)
        return carry

    lax.fori_loop(nc, nck, write_rest, 0)


def _dsa_mask(ik, iq, iw_t):
    B, L, _ = iq.shape
    tq = 128
    ksel = min(DSA_TOPK, L // 4)
    return pl.pallas_call(
        functools.partial(_dsa_mask_kernel, L=L, tq=tq, ksel=ksel),
        grid=(B, L // tq),
        in_specs=[pl.BlockSpec((1, L, IDX_DIM), lambda b, j: (b, 0, 0)),
                  pl.BlockSpec((1, tq, IDX_HEADS * IDX_DIM), lambda b, j: (b, j, 0)),
                  pl.BlockSpec((1, IDX_HEADS, tq), lambda b, j: (b, 0, j))],
        out_specs=pl.BlockSpec((1, L, tq), lambda b, j: (b, 0, j)),
        out_shape=jax.ShapeDtypeStruct((B, L, L), f32),
        scratch_shapes=[pltpu.VMEM((L, tq), f32), pltpu.VMEM((IDX_HEADS // 2, 2 * tq, IDX_DIM), bf16),
                        pltpu.VMEM((1, tq), i32)],
        compiler_params=_cparams(("parallel", "parallel")),
    )(ik, iq, iw_t)


def _col_reduce(x, op):
    r, c = x.shape
    slab = 8 * max(1, 2048 // c)
    if r > slab and r % slab == 0:
        x = op(x.reshape(r // slab, slab, c), axis=0)
    return op(x, axis=0, keepdims=True)


def _flash_group(ss, cap, v_ts, m_refs, l_refs, acc_refs):
    ss, mloc = _flash_mask_max(ss, cap)
    _flash_update(ss, mloc, cap is not None, v_ts, m_refs, l_refs, acc_refs)


def _flash_mask_max(ss, cap):
    if cap is not None:
        ss = [jnp.minimum(s, cap) for s in ss]
    return ss, [_col_reduce(s, jnp.max) for s in ss]


def _flash_update(ss, mloc, masked, v_ts, m_refs, l_refs, acc_refs):
    n = len(ss)
    m_prev = [r[...] for r in m_refs]
    m_new = [jnp.maximum(m_prev[i], mloc[i]) for i in range(n)]
    alpha = [jnp.exp2(m_prev[i] - m_new[i]) for i in range(n)]
    ps = [jnp.exp2(ss[i] - m_new[i]) for i in range(n)]
    l_new = [alpha[i] * l_refs[i][...] + _col_reduce(ps[i], jnp.sum) for i in range(n)]
    pv = [jnp.dot(v_ts[i], ps[i].astype(bf16), preferred_element_type=f32) for i in range(n)]
    for i in range(n):
        acc_new = acc_refs[i][...] * alpha[i] + pv[i]
        if masked:
            dead = m_new[i] <= NEG
            l_new[i] = jnp.where(dead, 0.0, l_new[i])
            acc_new = jnp.where(dead, 0.0, acc_new)
        l_refs[i][...] = l_new[i]
        acc_refs[i][...] = acc_new
        m_refs[i][...] = m_new[i]


def _flash_heads(score_fns, cap, v_ts, m_ref, l_ref, acc_ref, group):
    n = len(score_fns)
    groups = [list(range(g, min(g + group, n))) for g in range(0, n, group)]
    nxt = _flash_mask_max([score_fns[h]() for h in groups[0]], cap)
    for gi, hs in enumerate(groups):
        ss, mloc = nxt
        if gi + 1 < len(groups):
            nxt = _flash_mask_max([score_fns[h]() for h in groups[gi + 1]], cap)
        _flash_update(ss, mloc, cap is not None, [v_ts[h] for h in hs], [m_ref.at[h] for h in hs],
                      [l_ref.at[h] for h in hs], [acc_ref.at[h] for h in hs])


def _flash_init(m_ref, l_ref, acc_ref):
    m_ref[...] = jnp.full(m_ref.shape, NEG, f32)
    l_ref[...] = jnp.zeros(l_ref.shape, f32)
    acc_ref[...] = jnp.zeros(acc_ref.shape, f32)


def _dsa_attn_kernel(q_ref, k_ref, vt_ref, mask_ref, o_ref, m_ref, l_ref, acc_ref, *, tq, tk, nkc):
    j = pl.program_id(1)
    c = pl.program_id(2)

    @pl.when(c == 0)
    def _():
        _flash_init(m_ref, l_ref, acc_ref)

    @pl.when(c * tk < (j + 1) * tq)
    def _():
        cap = mask_ref[0]

        def scores(h):
            sl = slice(h * HEAD_DIM, (h + 1) * HEAD_DIM)
            return lambda: _dot_nt(k_ref[0, :, sl], q_ref[0, :, sl])

        v_ts = [vt_ref[0, h, 0] for h in range(A_HEADS)]
        _flash_heads([scores(h) for h in range(A_HEADS)], cap, v_ts, m_ref, l_ref, acc_ref, _HEAD_GROUP)

    @pl.when(c == nkc - 1)
    def _():
        for h in range(A_HEADS):
            o = acc_ref[h] / jnp.maximum(l_ref[h], 1e-30)
            o_ref[0, :, h * HEAD_DIM:(h + 1) * HEAD_DIM] = o.T.astype(o_ref.dtype)


def _dsa_attn(q, k, v_t, mask_t):
    B, L, W = q.shape
    tq, tk = 512, 512
    nkc = L // tk

    def last(j):
        return ((j + 1) * tq - 1) // tk

    return pl.pallas_call(
        functools.partial(_dsa_attn_kernel, tq=tq, tk=tk, nkc=nkc),
        grid=(B, L // tq, nkc),
        in_specs=[pl.BlockSpec((1, tq, W), lambda b, j, c: (b, j, 0)),
                  pl.BlockSpec((1, tk, W), lambda b, j, c: (b, jnp.minimum(c, last(j)), 0)),
                  pl.BlockSpec((1, A_HEADS, 1, HEAD_DIM, tk),
                               lambda b, j, c: (b, 0, jnp.minimum(c, last(j)), 0, 0)),
                  pl.BlockSpec((1, tk, tq), lambda b, j, c: (b, jnp.minimum(c, last(j)), j))],
        out_specs=pl.BlockSpec((1, tq, W), lambda b, j, c: (b, j, 0)),
        out_shape=jax.ShapeDtypeStruct((B, L, W), bf16),
        scratch_shapes=[pltpu.VMEM((A_HEADS, 1, tq), f32), pltpu.VMEM((A_HEADS, 1, tq), f32),
                        pltpu.VMEM((A_HEADS, HEAD_DIM, tq), f32)],
        compiler_params=_cparams(("parallel", "parallel", "arbitrary")),
    )(q, k, v_t, mask_t)


def _nsa_cmp_kernel(x_ref, w1_ref, w2_ref, pe_ref, o_ref):
    x = x_ref[0, 0, 0]
    w1 = w1_ref[0]
    half = CMP_STRIDE * HEAD_DIM
    a = jnp.dot(x, w1[:half], preferred_element_type=f32)
    b = jnp.dot(x, w1[half:], preferred_element_type=f32)
    pe = jnp.dot(pe_ref[0], w1, preferred_element_type=f32)[0:1]
    n = a.shape[0]
    pre = a + pltpu.roll(b, n - 1, 0) + pe
    act = pre * _sigmoid(pre)
    o_ref[0, 0, 0] = jnp.dot(act.astype(bf16), w2_ref[0], preferred_element_type=f32).astype(o_ref.dtype)


def _nsa_compress(x, w1, w2, pe, l):
    B, _, G, n, wd = x.shape
    return pl.pallas_call(
        _nsa_cmp_kernel,
        grid=(B, 2, G),
        in_specs=[pl.BlockSpec((1, 1, 1, n, wd), lambda b, t, g: (b, t, g, 0, 0)),
                  pl.BlockSpec((None, 1, CMP_LEN * HEAD_DIM, HEAD_DIM), lambda b, t, g: (l, t, 0, 0)),
                  pl.BlockSpec((None, 1, HEAD_DIM, HEAD_DIM), lambda b, t, g: (l, t, 0, 0)),
                  pl.BlockSpec((None, 1, 8, CMP_LEN * HEAD_DIM), lambda b, t, g: (l, t, 0, 0))],
        out_specs=pl.BlockSpec((1, 1, 1, n, HEAD_DIM), lambda b, t, g: (b, t, g, 0, 0)),
        out_shape=jax.ShapeDtypeStruct((B, 2, G, n, HEAD_DIM), bf16),
        compiler_params=_cparams(("parallel", "parallel", "parallel")),
    )(x, w1, w2, pe)


def _nsa_attn_kernel(q_ref, kc_ref, vct_ref, ks_ref, vst_ref, kw_ref, vwt_ref, g_ref, cov_ref, o_ref,
                     m_ref, l_ref, acc_ref, out_ref, sel_ref, imp_ref, rank_ref, *, L, tq, tk, n_sel):
    j = pl.program_id(2)
    t_row = j * tq + lax.broadcasted_iota(i32, (1, tq), 1)
    n_cmp = kc_ref.shape[2]
    n_slc = L // SLC_LEN
    qs = [q_ref[0, :, hh * HEAD_DIM:(hh + 1) * HEAD_DIM] for hh in range(B_HPG)]

    def gate(hh, i):
        return g_ref[0, 0, hh * 3 + i:hh * 3 + i + 1, :]

    kc = kc_ref[0, 0]
    vct = vct_ref[0, 0]
    cend = lax.broadcasted_iota(i32, (n_cmp, tq), 0) * CMP_STRIDE + (CMP_LEN - 1)
    mc = cend <= t_row
    ss = [jnp.where(mc, _dot_nt(kc, qs[hh]), NEG) for hh in range(B_HPG)]
    mxs = [_col_reduce(s, jnp.max) for s in ss]
    ps = [jnp.where(mc, jnp.exp2(ss[hh] - mxs[hh]), 0.0) for hh in range(B_HPG)]
    inv = [1.0 / jnp.maximum(_col_reduce(p, jnp.sum), 1e-30) for p in ps]
    pvs = [jnp.dot(vct, p.astype(bf16), preferred_element_type=f32) for p in ps]
    psum = ps[0] * inv[0]
    for hh in range(1, B_HPG):
        psum = psum + ps[hh] * inv[hh]
    for hh in range(B_HPG):
        out_ref[hh] = (gate(hh, 0) * inv[hh]) * pvs[hh]
    imp = jnp.dot(cov_ref[...], psum, preferred_element_type=f32, precision=lax.Precision.HIGHEST)
    blk = lax.broadcasted_iota(i32, (n_slc, tq), 0)
    cur = t_row >> (SLC_LEN.bit_length() - 1)
    forced = (blk == 0) | (blk == cur) | (blk == cur - 1)
    imp_ref[...] = jnp.where(forced, FORCE, jnp.where(blk <= cur, imp, NEG))
    rank_ref[...] = jnp.zeros((n_slc, tq), f32)
    rows_per_step = 8
    for r0 in range(0, n_slc, rows_per_step):
        @pl.when(r0 * SLC_LEN < (j + 1) * tq)
        def _():
            imp_all = imp_ref[...]
            rank = rank_ref[...]
            for r in range(r0, r0 + rows_per_step):
                row = imp_ref[r:r + 1, :]
                rank = rank + jnp.where(row > imp_all, 1.0,
                                        jnp.where(row == imp_all, jnp.where(blk > r, 1.0, 0.0), 0.0))
            rank_ref[...] = rank
    sel_ref[...] = jnp.where(rank_ref[...] < float(n_sel), _INF, NEG)

    def finish(i):
        for hh in range(B_HPG):
            o = acc_ref[hh] / jnp.maximum(l_ref[hh], 1e-30)
            out_ref[hh] = out_ref[hh] + gate(hh, i) * o

    def run_branch(k_ref, vt_ref, c_lo, c_hi, mask_fn):
        for hh in range(B_HPG):
            _flash_init(m_ref.at[hh], l_ref.at[hh], acc_ref.at[hh])

        def body(c, carry):
            off = pl.multiple_of(c * tk, tk)
            kch = k_ref[0, pl.ds(off, tk), :]
            vch = vt_ref[0, 0, c]
            sidx = off + lax.broadcasted_iota(i32, (tk, tq), 0)
            cap = mask_fn(c, sidx)
            fns = [functools.partial(_dot_nt, kch, qs[hh]) for hh in range(B_HPG)]
            _flash_heads(fns, cap, [vch] * B_HPG, m_ref, l_ref, acc_ref, _HEAD_GROUP)
            return carry

        lax.fori_loop(c_lo, c_hi, body, 0)

    bpc = tk // SLC_LEN

    def slc_mask(c, sidx):
        rows = [jnp.broadcast_to(sel_ref[pl.ds(c * bpc + b, 1), :], (SLC_LEN, tq)) for b in range(bpc)]
        return jnp.where(sidx <= t_row, jnp.concatenate(rows, axis=0), NEG)

    c_hi = ((j + 1) * tq + tk - 1) // tk
    run_branch(ks_ref, vst_ref, 0, c_hi, slc_mask)
    finish(1)

    def win_mask(c, sidx):
        d = t_row - sidx
        return jnp.where(d >= 0, jnp.where(d < WIN_LEN, _INF, NEG), NEG)

    c_lo = jnp.maximum(j * tq - (WIN_LEN - 1), 0) // tk
    run_branch(kw_ref, vwt_ref, c_lo, c_hi, win_mask)
    finish(2)

    for hh in range(B_HPG):
        o_ref[0, :, hh * HEAD_DIM:(hh + 1) * HEAD_DIM] = out_ref[hh].T.astype(o_ref.dtype)


def _nsa_attn(bq, kc, vc_t, bk, bv_tc, gates_t, cov_t):
    B, L, _ = bq.shape
    G = B_KV_GROUPS
    tq, tk = 512, 512
    n_cmp = kc.shape[2]
    n_slc = L // SLC_LEN
    n_sel = min(SLC_TOPN, n_slc)
    gw = B_HPG * HEAD_DIM
    nc = L // tk
    return pl.pallas_call(
        functools.partial(_nsa_attn_kernel, L=L, tq=tq, tk=tk, n_sel=n_sel),
        grid=(B, G, L // tq),
        in_specs=[pl.BlockSpec((1, tq, gw), lambda b, g, j: (b, j, g)),
                  pl.BlockSpec((1, 1, n_cmp, HEAD_DIM), lambda b, g, j: (b, g, 0, 0)),
                  pl.BlockSpec((1, 1, HEAD_DIM, n_cmp), lambda b, g, j: (b, g, 0, 0)),
                  pl.BlockSpec((1, L, HEAD_DIM), lambda b, g, j: (b, 0, 2 + g)),
                  pl.BlockSpec((1, 1, nc, HEAD_DIM, tk), lambda b, g, j: (b, g, 0, 0, 0)),
                  pl.BlockSpec((1, L, HEAD_DIM), lambda b, g, j: (b, 0, 4 + g)),
                  pl.BlockSpec((1, 1, nc, HEAD_DIM, tk), lambda b, g, j: (b, B_KV_GROUPS + g, 0, 0, 0)),
                  pl.BlockSpec((1, 1, 3 * B_HPG, tq), lambda b, g, j: (b, g, 0, j)),
                  pl.BlockSpec((n_slc, n_cmp), lambda b, g, j: (0, 0))],
        out_specs=pl.BlockSpec((1, tq, gw), lambda b, g, j: (b, j, g)),
        out_shape=jax.ShapeDtypeStruct((B, L, B_HEADS * HEAD_DIM), bf16),
        scratch_shapes=[pltpu.VMEM((B_HPG, 1, tq), f32), pltpu.VMEM((B_HPG, 1, tq), f32),
                        pltpu.VMEM((B_HPG, HEAD_DIM, tq), f32), pltpu.VMEM((B_HPG, HEAD_DIM, tq), f32),
                        pltpu.VMEM((n_slc, tq), f32), pltpu.VMEM((n_slc, tq), f32),
                        pltpu.VMEM((n_slc, tq), f32)],
        compiler_params=_cparams(("parallel", "parallel", "parallel")),
    )(bq, kc, vc_t, bk, bv_tc, bk, bv_tc, gates_t, cov_t)


def _diff_attn_kernel(q_ref, k_ref, vt_ref, lam_ref, g_ref, o_ref, m_ref, l_ref, acc_ref,
                      *, tq, tk, lam_init):
    j = pl.program_id(2)
    t_row = j * tq + lax.broadcasted_iota(i32, (1, tq), 1)
    qs = [q_ref[0, :, mi * C_DIM:(mi + 1) * C_DIM] for mi in range(2)]
    for mi in range(2):
        _flash_init(m_ref.at[mi], l_ref.at[mi], acc_ref.at[mi])

    def chunk(c, lo, width, masked):
        off = pl.multiple_of(c * tk + lo, tq)
        vch = vt_ref[0, 0, c, :, lo:lo + width]
        cap = None
        if masked:
            sidx = off + lax.broadcasted_iota(i32, (width, tq), 0)
            cap = jnp.where(sidx <= t_row, _INF, NEG)
        ss = [_dot_nt(k_ref[0, pl.ds(off, width), mi * C_DIM:(mi + 1) * C_DIM], qs[mi]) for mi in range(2)]
        _flash_group(ss, cap, [vch] * 2, [m_ref.at[mi] for mi in range(2)],
                     [l_ref.at[mi] for mi in range(2)], [acc_ref.at[mi] for mi in range(2)])

    def body(c, carry):
        chunk(c, 0, tk, False)
        return carry

    c_diag = (j * tq) // tk
    lax.fori_loop(0, c_diag, body, 0)

    @pl.when(j % 2 == 0)
    def _():
        chunk(c_diag, 0, tq, True)

    @pl.when(j % 2 == 1)
    def _():
        chunk(c_diag, 0, tq, False)
        chunk(c_diag, tq, tq, True)

    lam = lam_ref[...]
    lam_val = (jnp.exp(jnp.sum(lam[0:1] * lam[1:2], axis=1, keepdims=True))
               - jnp.exp(jnp.sum(lam[2:3] * lam[3:4], axis=1, keepdims=True)) + lam_init)
    o = (acc_ref[0] / jnp.maximum(l_ref[0], 1e-30)
         - lam_val * (acc_ref[1] / jnp.maximum(l_ref[1], 1e-30)))
    ms = jnp.mean(o * o, axis=0, keepdims=True)
    y = o * lax.rsqrt(ms + 1e-6) * g_ref[...] * (1.0 - lam_init)
    o_ref[0] = y.T.astype(o_ref.dtype)


def _diff_attn(cq, ck, cv_tc, lam, g_col, lam_init):
    B, L, _ = cq.shape
    tq, tk = _DIFF_TQ, cv_tc.shape[-1]
    assert tk == 2 * tq
    hw = 2 * C_DIM
    nc = L // tk
    return pl.pallas_call(
        functools.partial(_diff_attn_kernel, tq=tq, tk=tk, lam_init=lam_init),
        grid=(B, C_HEADS, L // tq),
        in_specs=[pl.BlockSpec((1, tq, hw), lambda b, h, j: (b, j, h)),
                  pl.BlockSpec((1, L, hw), lambda b, h, j: (b, 0, h)),
                  pl.BlockSpec((1, 1, nc, hw, tk), lambda b, h, j: (b, h, 0, 0, 0)),
                  pl.BlockSpec((4, C_DIM), lambda b, h, j: (0, 0)),
                  pl.BlockSpec((hw, 1), lambda b, h, j: (0, 0))],
        out_specs=pl.BlockSpec((1, tq, hw), lambda b, h, j: (b, j, h)),
        out_shape=jax.ShapeDtypeStruct((B, L, C_HEADS * hw), bf16),
        scratch_shapes=[pltpu.VMEM((2, 1, tq), f32), pltpu.VMEM((2, 1, tq), f32),
                        pltpu.VMEM((2, hw, tq), f32)],
        compiler_params=_cparams(("parallel", "parallel", "parallel")),
    )(cq, ck, cv_tc, lam, g_col)


def _merge_kernel(ya_ref, yb_ref, yc_ref, w_ref, g0_ref, g1_ref, g2_ref, o_ref, wb_ref):
    @pl.when(pl.program_id(1) == 0)
    def _():
        wb_ref[...] = w_ref[...].astype(bf16)

    acc = None
    for r, (y_ref, g_ref) in enumerate(((ya_ref, g0_ref), (yb_ref, g1_ref), (yc_ref, g2_ref))):
        br = jnp.dot(y_ref[...], wb_ref[r], preferred_element_type=f32)
        t = _sigmoid(g_ref[...].astype(f32)) * br
        acc = t if acc is None else acc + t
    o_ref[...] = acc.astype(o_ref.dtype)


def _merge(ya, yb, yc, w_br, l, h):
    m, kw = ya.shape
    tm, tn = 512, 512
    npb = D_MODEL // tn
    yspec = pl.BlockSpec((tm, kw), lambda j, i: (i, 0))

    def gspec(r):
        return pl.BlockSpec((tm, tn), lambda j, i, _r=r: (i, _r * npb + j))

    return pl.pallas_call(
        _merge_kernel,
        grid=(npb, m // tm),
        in_specs=[yspec, yspec, yspec,
                  pl.BlockSpec((None, N_BRANCH, kw, tn), lambda j, i: (l, 0, 0, j)),
                  gspec(0), gspec(1), gspec(2)],
        out_specs=pl.BlockSpec((tm, tn), lambda j, i: (i, j)),
        out_shape=jax.ShapeDtypeStruct((m, D_MODEL), bf16),
        scratch_shapes=[pltpu.VMEM((N_BRANCH, kw, tn), bf16)],
        compiler_params=_cparams(("parallel", "arbitrary")),
    )(ya, yb, yc, w_br, h, h, h)


def _mm_res_ln_kernel(a_ref, w_ref, x_ref, gate_ref, lg_ref, lb_ref, sc_ref, sh_ref, xo_ref, *u_refs,
                      alpha, sub):
    tm = a_ref.shape[0]
    for r in range(tm // sub):
        rows = slice(r * sub, (r + 1) * sub)
        y = jnp.dot(a_ref[rows, :], w_ref[...], preferred_element_type=f32)
        z = alpha * x_ref[rows, :] + gate_ref[0] * y
        xn = _ln_rows(z, 1e-5) * lg_ref[...] + lb_ref[...]
        xo_ref[rows, :] = xn
        if u_refs:
            u_refs[0][rows, :] = (_ln_rows(xn, 1e-5) * (1.0 + sc_ref[0]) + sh_ref[0]).astype(bf16)


def _mm_res_ln(a, w, l, x2, gate, ln_g, ln_b, sc, sh, L, alpha, emit_u):
    m, kdim = a.shape
    d = w.shape[2]
    tm = _RES_LN_TM if kdim * d * 2 <= 16 * 1024 * 1024 else _RES_LN_SUB
    per_b = L // tm
    bspec = pl.BlockSpec((1, 1, d), lambda i: (i // per_b, 0, 0))
    vspec = pl.BlockSpec((1, d), lambda i: (0, 0))
    rspec = pl.BlockSpec((tm, d), lambda i: (i, 0))
    out_shape = [jax.ShapeDtypeStruct((m, d), f32)]
    out_specs = [rspec]
    if emit_u:
        out_shape.append(jax.ShapeDtypeStruct((m, d), bf16))
        out_specs.append(rspec)
    res = pl.pallas_call(
        functools.partial(_mm_res_ln_kernel, alpha=alpha, sub=_RES_LN_SUB),
        grid=(m // tm,),
        in_specs=[pl.BlockSpec((tm, kdim), lambda i: (i, 0)),
                  pl.BlockSpec((None, kdim, d), lambda i: (l, 0, 0), pipeline_mode=pl.Buffered(1)),
                  rspec, bspec, vspec, vspec, bspec, bspec],
        out_specs=out_specs,
        out_shape=out_shape,
        compiler_params=_cparams(("parallel",), 60 * 1024 * 1024),
    )(a, w, x2, gate, ln_g.reshape(1, d), ln_b.reshape(1, d), sc, sh)
    return res if emit_u else (res[0], None)


def _ffn_in_kernel(a_ref, wg_ref, wu_ref, o_ref, wgb_ref, wub_ref):
    @pl.when(pl.program_id(1) == 0)
    def _():
        wgb_ref[...] = wg_ref[...].astype(bf16)
        wub_ref[...] = wu_ref[...].astype(bf16)

    a = a_ref[...]
    g = jnp.dot(a, wgb_ref[...], preferred_element_type=f32)
    u = jnp.dot(a, wub_ref[...], preferred_element_type=f32)
    o_ref[...] = (g * _sigmoid(g) * u).astype(o_ref.dtype)


def _ffn_in(a, w, l):
    m, k = a.shape
    tm, tn = 512, 512
    nb = D_FF // tn
    return pl.pallas_call(
        _ffn_in_kernel,
        grid=(nb, m // tm),
        in_specs=[pl.BlockSpec((tm, k), lambda j, i: (i, 0)),
                  pl.BlockSpec((None, k, tn), lambda j, i: (l, 0, j)),
                  pl.BlockSpec((None, k, tn), lambda j, i: (l, 0, nb + j))],
        out_specs=pl.BlockSpec((tm, tn), lambda j, i: (i, j)),
        out_shape=jax.ShapeDtypeStruct((m, D_FF), bf16),
        scratch_shapes=[pltpu.VMEM((k, tn), bf16), pltpu.VMEM((k, tn), bf16)],
        compiler_params=_cparams(("parallel", "arbitrary")),
    )(a, w, w)


def _rope_tables(L, d):
    r = d // 4
    half = r // 2
    inv = ROPE_THETA ** (-(jnp.arange(half, dtype=f32) * 2.0) / r)
    ang = jnp.arange(L).astype(f32)[:, None] * inv[None, :]
    cos, sin = jnp.cos(ang), jnp.sin(ang)
    z = jnp.zeros((L, d - r), f32)
    zh = jnp.zeros((L, half), f32)
    c = jnp.concatenate([cos, cos, jnp.ones((L, d - r), f32)], axis=1)
    s1 = jnp.concatenate([zh, sin, z], axis=1)
    s2 = jnp.concatenate([-sin, zh, z], axis=1)
    rep = 128 // d
    return tuple(jnp.tile(t, (1, rep)) for t in (c, s1, s2))


def _pack_w_in_kernel(w_ref, o_ref, ov_ref):
    def cp(src, width, dst):
        for o in range(0, width, 1024):
            wd = min(1024, width - o)
            o_ref[:, dst + o:dst + o + wd] = w_ref[src + o:src + o + wd, :].T.astype(bf16)

    def bkv(i, kv, g):
        return _O_BKV + ((i * 2 + kv) * B_KV_GROUPS + g) * HEAD_DIM

    cp(_O_GL, N_BRANCH * D_MODEL, _P_GL)
    for src, dst in ((_O_AQ, _P_AQ), (_O_IQ, _P_IQ), (_O_BQ, _P_BQ), (_O_CQ, _P_CQ), (_O_CK, _P_CK)):
        cp(src, 1024, dst)
    for i in range(3):
        for g in range(B_KV_GROUPS):
            cp(bkv(i, 0, g), HEAD_DIM, _P_BK + (i * B_KV_GROUPS + g) * HEAD_DIM)
    for g in range(B_KV_GROUPS):
        cp(bkv(0, 1, g), HEAD_DIM, _P_BVC + g * HEAD_DIM)
    cp(_O_ALAT, A_LATENT, _P_ALAT)
    ov_ref[_PV_CV:_PV_CV + 1024, :] = w_ref[_O_CV:_O_CV + 1024, :].astype(bf16)
    for i in (1, 2):
        for g in range(B_KV_GROUPS):
            dst = _PV_BV + ((i - 1) * B_KV_GROUPS + g) * HEAD_DIM
            ov_ref[dst:dst + HEAD_DIM, :] = w_ref[bkv(i, 1, g):bkv(i, 1, g) + HEAD_DIM, :].astype(bf16)
    lane = lax.broadcasted_iota(i32, (w_ref.shape[1], 128), 1)
    assert _O_IW == _O_IK + IDX_DIM and _O_IK % 8 == 0 and _O_BG % 8 == 0
    blk = w_ref[_O_IK:_O_IK + 128, :].T
    o_ref[:, _P_IKW:_P_IKW + 128] = jnp.where(lane < IDX_DIM + IDX_HEADS, blk, 0.0).astype(bf16)
    blk = w_ref[_O_BG:_O_BG + 128, :].T
    o_ref[:, _P_BG:_P_BG + 128] = jnp.where(lane < 3 * B_HEADS, blk, 0.0).astype(bf16)


def _pack_w_in(w_in):
    depth, k, n = w_in.shape
    w_t = jnp.swapaxes(w_in, 1, 2)
    tc = 128
    return pl.pallas_call(
        _pack_w_in_kernel,
        grid=(depth, k // tc),
        in_specs=[pl.BlockSpec((None, n, tc), lambda l, i: (l, 0, i))],
        out_specs=[pl.BlockSpec((None, tc, _P_TOT), lambda l, i: (l, i, 0)),
                   pl.BlockSpec((None, _PV_TOT, tc), lambda l, i: (l, 0, i))],
        out_shape=[jax.ShapeDtypeStruct((depth, k, _P_TOT), bf16),
                   jax.ShapeDtypeStruct((depth, _PV_TOT, k), bf16)],
        compiler_params=_cparams(("parallel", "parallel")),
    )(w_t)


def _cover_t(L):
    n_cmp_pad = L // CMP_STRIDE
    starts = np.arange(n_cmp_pad) * CMP_STRIDE
    slc_start = np.arange(L // SLC_LEN) * SLC_LEN
    cover = ((starts[:, None] < slc_start[None, :] + SLC_LEN)
             & (starts[:, None] + CMP_LEN > slc_start[None, :])).astype(np.float32)
    n_cmp = (L - CMP_LEN) // CMP_STRIDE + 1
    cover[n_cmp:] = 0.0
    return jnp.asarray(cover.T)


def _token_mixing(u, h, B, L, l, lw, tabs128, tabs64, cov_t, lam_init):
    M = B * L
    G = B_KV_GROUPS
    aq, iq, bq, cq, ck, bk, alat_n, ikw, gates = _prep(h, lw['a_lat_g'], tabs128, tabs64, L)

    aw = A_HEADS * HEAD_DIM
    ak = _mm_rope(alat_n, lw['a_up'], l, aw, tabs128, L, 512, aw)
    av_tc = _proj_t(alat_n, lw['a_up_vt'], l, 0, A_HEADS, HEAD_DIM, 512, B, L)
    ik = ikw[:, :IDX_DIM].astype(bf16).reshape(B, L, IDX_DIM)
    iw_t = ikw[:, IDX_DIM:IDX_DIM + IDX_HEADS].reshape(B, L, IDX_HEADS).transpose(0, 2, 1)
    mask_t = _dsa_mask(ik, iq.reshape(B, L, -1), iw_t)
    ya = _dsa_attn(aq.reshape(B, L, -1), ak.reshape(B, L, -1), av_tc, mask_t)

    bvc = h[:, _P_BVC:_P_BVC + G * HEAD_DIM].reshape(B, L, G, HEAD_DIM)
    bkc = bk[:, :G * HEAD_DIM].reshape(B, L, G, HEAD_DIM)
    n_row = L // CMP_STRIDE
    xcmp = jnp.stack([bkc, bvc], axis=1)
    xcmp = xcmp.transpose(0, 1, 3, 2, 4).reshape(B, 2, G, n_row, CMP_STRIDE * HEAD_DIM)
    cmp_out = _nsa_compress(xcmp, lw['cmp_w1'], lw['cmp_w2'], lw['cmp_pe'], l)
    kc = cmp_out[:, 0]
    vc_t = cmp_out[:, 1].transpose(0, 1, 3, 2)
    bv_tc = _proj_t(u, lw['w_vt'], l, _PV_BV, 2 * G, HEAD_DIM, 512, B, L)
    gates_t = gates[:, :3 * B_HEADS].reshape(B, L, G, 3 * B_HPG).transpose(0, 2, 3, 1)
    yb = _nsa_attn(bq.reshape(B, L, -1), kc, vc_t, bk.reshape(B, L, -1), bv_tc, gates_t, cov_t)

    cv_tc = _proj_t(u, lw['w_vt'], l, _PV_CV, C_HEADS, 2 * C_DIM, _DIFF_TK, B, L)
    yc = _diff_attn(cq.reshape(B, L, -1), ck.reshape(B, L, -1), cv_tc, lw['lam'],
                    lw['c_subln_g'].reshape(2 * C_DIM, 1), lam_init)

    return _merge(ya.reshape(M, -1), yb.reshape(M, -1), yc.reshape(M, -1), lw['w_br'], l, h)


def kernel(x, c, w_ada, b_ada, w_in, a_lat_g, a_up, cmp_w1, cmp_w2, cmp_pe, lam, c_subln_g, w_br, w_o,
           w_ffn_in, w_ffn_out, ln_g, ln_b):
    B, L, D = x.shape
    depth = w_ada.shape[0]
    M = B * L
    alpha = (2 * depth) ** 0.25

    c_pad = jnp.zeros((8, D), f32).at[:B].set(c)
    mod = _ada(c_pad, w_ada, b_ada)[:, :B]
    mods = [[mod[l, :, i * D:(i + 1) * D].reshape(B, 1, D) for i in range(6)] for l in range(depth)]

    tabs128 = _rope_tables(L, HEAD_DIM)
    tabs64 = _rope_tables(L, IDX_DIM)
    cov_t = _cover_t(L)

    w_in_p, w_vt = _pack_w_in(w_in)
    pe_flat = jnp.zeros((depth, 2, 8, CMP_LEN * HEAD_DIM), f32).at[:, :, 0].set(
        cmp_pe.reshape(depth, 2, CMP_LEN * HEAD_DIM)).astype(bf16)
    a_up_vt = jnp.swapaxes(a_up[:, :, A_HEADS * HEAD_DIM:], 1, 2).astype(bf16)
    wb = dict(a_up=a_up.astype(bf16), a_up_vt=a_up_vt, w_vt=w_vt, cmp_w1=cmp_w1.astype(bf16),
              cmp_w2=cmp_w2.astype(bf16), cmp_pe=pe_flat, w_br=w_br)
    w_o_b = w_o.astype(bf16)
    w_fo_b = w_ffn_out.astype(bf16)

    x2 = x.reshape(M, D)
    u = _lnmod(x2, mods[0][1], mods[0][0], L)
    for l in range(depth):
        lam_init = 0.8 - 0.6 * math.exp(-0.3 * l)
        sh_a, sc_a, g_a, sh_f, sc_f, g_f = mods[l]
        lw = dict(wb, a_lat_g=a_lat_g[l], lam=lam[l], c_subln_g=c_subln_g[l])
        h = _mm(u, w_in_p, l, 512, 768, bf16)
        merged = _token_mixing(u, h, B, L, l, lw, tabs128, tabs64, cov_t, lam_init)
        x2, u = _mm_res_ln(merged, w_o_b, l, x2, g_a, ln_g[l, 0], ln_b[l, 0], sc_f, sh_f, L, alpha, True)
        f = _ffn_in(u, w_ffn_in, l)
        last = l == depth - 1
        nsc, nsh = (sc_f, sh_f) if last else (mods[l + 1][1], mods[l + 1][0])
        x2, u = _mm_res_ln(f, w_fo_b, l, x2, g_f, ln_g[l, 1], ln_b[l, 1], nsc, nsh, L, alpha, not last)
    return x2.reshape(B, L, D)
```

```python
import functools
import math

import numpy as np
import jax
import jax.numpy as jnp
from jax import lax
from jax.experimental import pallas as pl
from jax.experimental.pallas import tpu as pltpu

f32 = jnp.float32
bf16 = jnp.bfloat16
i32 = jnp.int32

D_MODEL = 2048
HEAD_DIM = 128
ROPE_THETA = 500000.0
NEG = -1e30
FORCE = 1e6
A_HEADS = 8
A_LATENT = 512
IDX_HEADS = 16
IDX_DIM = 64
DSA_TOPK = 256
B_HEADS = 8
B_KV_GROUPS = 2
B_HPG = B_HEADS // B_KV_GROUPS
CMP_LEN = 32
CMP_STRIDE = 16
SLC_LEN = 64
SLC_TOPN = 16
WIN_LEN = 512
C_HEADS = 4
C_DIM = 128
BRANCH_W = A_HEADS * HEAD_DIM
N_BRANCH = 3
D_FF = int(math.ceil(8 * D_MODEL / 3 / 256)) * 256

_O_AQ = 0
_O_ALAT = _O_AQ + A_HEADS * HEAD_DIM
_O_IQ = _O_ALAT + A_LATENT
_O_IK = _O_IQ + IDX_HEADS * IDX_DIM
_O_IW = _O_IK + IDX_DIM
_O_BQ = _O_IW + IDX_HEADS
_O_BKV = _O_BQ + B_HEADS * HEAD_DIM
_O_BG = _O_BKV + 3 * 2 * B_KV_GROUPS * HEAD_DIM
_O_CQ = _O_BG + 3 * B_HEADS
_O_CK = _O_CQ + C_HEADS * 2 * C_DIM
_O_CV = _O_CK + C_HEADS * 2 * C_DIM
_O_GL = _O_CV + C_HEADS * 2 * C_DIM
_N_IN = _O_GL + N_BRANCH * D_MODEL

_P_GL = 0
_P_BK = 6144
_P_BVC = 6912
_P_AQ = 7168
_P_IQ = 8192
_P_BQ = 9216
_P_CQ = 10240
_P_CK = 11264
_P_ALAT = 12288
_P_IKW = 12800
_P_BG = 12928
_P_TOT = 13056
_PV_CV = 0
_PV_BV = 1024
_PV_TOT = 1536

_VMEM_LIMIT = 48 * 1024 * 1024
_LOG2E = 1.4426950408889634
_INF = float("inf")
_ONES = 16
_RES_LN_TM = 512
_RES_LN_SUB = 256
_DIFF_TK = 1024
_DIFF_TQ = 512
_HEAD_GROUP = 2


def _cparams(sem, vmem=_VMEM_LIMIT):
    return pltpu.CompilerParams(dimension_semantics=sem, vmem_limit_bytes=vmem)


def _sigmoid(x):
    return 1.0 / (1.0 + jnp.exp(-x))


def _dot_nt(a, b):
    return lax.dot_general(a, b, (((1,), (1,)), ((), ())), preferred_element_type=f32)


def _ada_kernel(c_ref, w_ref, b_ref, o_ref):
    c = c_ref[...]
    cs = c * _sigmoid(c)
    o_ref[0] = jnp.dot(cs, w_ref[0], preferred_element_type=f32,
                       precision=lax.Precision.HIGHEST) + b_ref[0]


def _ada(c_pad, w_ada, b_ada):
    depth, d, n = w_ada.shape
    tn = 512
    return pl.pallas_call(
        _ada_kernel,
        grid=(depth, n // tn),
        in_specs=[pl.BlockSpec((8, d), lambda l, j: (0, 0)),
                  pl.BlockSpec((1, d, tn), lambda l, j: (l, 0, j)),
                  pl.BlockSpec((1, 1, tn), lambda l, j: (l, 0, j))],
        out_specs=pl.BlockSpec((1, 8, tn), lambda l, j: (l, 0, j)),
        out_shape=jax.ShapeDtypeStruct((depth, 8, n), f32),
        compiler_params=_cparams(("parallel", "parallel")),
    )(c_pad, w_ada, b_ada.reshape(depth, 1, n))


def _ln_rows(x, eps):
    mu = jnp.mean(x, axis=-1, keepdims=True)
    d = x - mu
    var = jnp.mean(d * d, axis=-1, keepdims=True)
    return d * lax.rsqrt(var + eps)


def _lnmod_kernel(x_ref, sc_ref, sh_ref, o_ref):
    y = _ln_rows(x_ref[...], 1e-5)
    o_ref[...] = (y * (1.0 + sc_ref[0]) + sh_ref[0]).astype(o_ref.dtype)


def _lnmod(x2, sc, sh, L):
    m, d = x2.shape
    tm = 512
    per_b = L // tm
    return pl.pallas_call(
        _lnmod_kernel,
        grid=(m // tm,),
        in_specs=[pl.BlockSpec((tm, d), lambda i: (i, 0)),
                  pl.BlockSpec((1, 1, d), lambda i: (i // per_b, 0, 0)),
                  pl.BlockSpec((1, 1, d), lambda i: (i // per_b, 0, 0))],
        out_specs=pl.BlockSpec((tm, d), lambda i: (i, 0)),
        out_shape=jax.ShapeDtypeStruct((m, d), bf16),
        compiler_params=_cparams(("parallel",)),
    )(x2, sc, sh)


def _mm_kernel(a_ref, w_ref, o_ref):
    o_ref[...] = jnp.dot(a_ref[...], w_ref[...], preferred_element_type=f32).astype(o_ref.dtype)


def _mm(a, w, l, tm, tn, out_dtype):
    m, k = a.shape
    n = w.shape[2]
    return pl.pallas_call(
        _mm_kernel,
        grid=(n // tn, m // tm),
        in_specs=[pl.BlockSpec((tm, k), lambda j, i: (i, 0)),
                  pl.BlockSpec((None, k, tn), lambda j, i: (l, 0, j))],
        out_specs=pl.BlockSpec((tm, tn), lambda j, i: (i, j)),
        out_shape=jax.ShapeDtypeStruct((m, n), out_dtype),
        compiler_params=_cparams(("parallel", "parallel")),
    )(a, w)


def _proj_t_kernel(w_ref, a_ref, o_ref):
    n, d = o_ref.shape[1], o_ref.shape[3] - _ONES
    tk = o_ref.shape[4]
    res = _dot_nt(w_ref[...], a_ref[...])
    for i in range(n):
        o_ref[0, i, 0, :d, :] = res[i * d:(i + 1) * d, :].astype(o_ref.dtype)
        o_ref[0, i, 0, d:, :] = jnp.ones((_ONES, tk), o_ref.dtype)


def _proj_t(a, w_t, l, row0, n, d, tk, B, L):
    k = a.shape[1]
    rows = n * d
    nc = L // tk
    return pl.pallas_call(
        _proj_t_kernel,
        grid=(B, nc),
        in_specs=[pl.BlockSpec((None, rows, k), lambda b, c: (l, row0 // rows, 0)),
                  pl.BlockSpec((tk, k), lambda b, c: (b * nc + c, 0))],
        out_specs=pl.BlockSpec((1, n, 1, d + _ONES, tk), lambda b, c: (b, 0, c, 0, 0)),
        out_shape=jax.ShapeDtypeStruct((B, n, nc, d + _ONES, tk), bf16),
        compiler_params=_cparams(("parallel", "parallel")),
    )(w_t, a)


def _rope_heads(x, c, s1, s2, half):
    return x * c + pltpu.roll(x, half, 1) * s1 + pltpu.roll(x, 128 - half, 1) * s2


def _mm_rope_kernel(a_ref, w_ref, c_ref, s1_ref, s2_ref, o_ref):
    acc = jnp.dot(a_ref[...], w_ref[...], preferred_element_type=f32)
    c, s1, s2 = c_ref[...], s1_ref[...], s2_ref[...]
    for h in range(acc.shape[1] // 128):
        sl = slice(h * 128, (h + 1) * 128)
        o_ref[:, sl] = _rope_heads(acc[:, sl], c, s1, s2, 16).astype(o_ref.dtype)


def _mm_rope(a, w, l, n, tabs, L, tm, tn):
    m, k = a.shape
    per_b = L // tm
    tspec = pl.BlockSpec((tm, 128), lambda j, i: (i % per_b, 0))
    return pl.pallas_call(
        _mm_rope_kernel,
        grid=(n // tn, m // tm),
        in_specs=[pl.BlockSpec((tm, k), lambda j, i: (i, 0)),
                  pl.BlockSpec((None, k, tn), lambda j, i: (l, 0, j)),
                  tspec, tspec, tspec],
        out_specs=pl.BlockSpec((tm, tn), lambda j, i: (i, j)),
        out_shape=jax.ShapeDtypeStruct((m, n), bf16),
        compiler_params=_cparams(("parallel", "parallel")),
    )(a, w, *tabs)


def _prep_kernel(aq_ref, iq_ref, bq_ref, cq_ref, ck_ref, bk_ref, alat_ref, ikw_ref, bg_ref, alg_ref,
                 c_ref, s1_ref, s2_ref, c6_ref, s16_ref, s26_ref,
                 aq_o, iq_o, bq_o, cq_o, ck_o, bk_o, alat_o, ikw_o, g_o):
    c, s1, s2 = c_ref[...], s1_ref[...], s2_ref[...]
    c6, s16, s26 = c6_ref[...], s16_ref[...], s26_ref[...]
    scale = HEAD_DIM ** -0.5 * _LOG2E

    def rope_all(src, dst, mult, tabs, half):
        for h in range(src.shape[1] // 128):
            sl = slice(h * 128, (h + 1) * 128)
            y = _rope_heads(src[:, sl].astype(f32), *tabs, half)
            if mult != 1.0:
                y = y * mult
            dst[:, sl] = y.astype(dst.dtype)

    rope_all(aq_ref, aq_o, scale, (c, s1, s2), 16)
    rope_all(bq_ref, bq_o, scale, (c, s1, s2), 16)
    rope_all(cq_ref, cq_o, C_DIM ** -0.5 * _LOG2E, (c, s1, s2), 16)
    rope_all(ck_ref, ck_o, 1.0, (c, s1, s2), 16)
    rope_all(bk_ref, bk_o, 1.0, (c, s1, s2), 16)
    rope_all(iq_ref, iq_o, 1.0, (c6, s16, s26), 8)

    a = alat_ref[...].astype(f32)
    ms = jnp.mean(a * a, axis=-1, keepdims=True)
    alat_o[...] = (a * lax.rsqrt(ms + 1e-6) * alg_ref[...]).astype(alat_o.dtype)

    x = ikw_ref[...].astype(f32)
    lane = lax.broadcasted_iota(i32, x.shape, 1)
    isk = lane < IDX_DIM
    mu = jnp.sum(jnp.where(isk, x, 0.0), axis=-1, keepdims=True) * (1.0 / IDX_DIM)
    d = jnp.where(isk, x - mu, 0.0)
    var = jnp.sum(d * d, axis=-1, keepdims=True) * (1.0 / IDX_DIM)
    y = d * lax.rsqrt(var + 1e-5)
    yr = _rope_heads(y, c6, s16, s26, 8)
    ikw_o[...] = jnp.where(isk, yr, x * (IDX_DIM ** -0.5 * IDX_HEADS ** -0.5))

    g_o[...] = _sigmoid(bg_ref[...].astype(f32))


def _prep(h, a_lat_g, tabs128, tabs64, L):
    m = h.shape[0]
    tm = 256
    per_b = L // tm

    def hs(width, off):
        return pl.BlockSpec((tm, width), lambda i, _o=off // width: (i, _o))

    tspec = pl.BlockSpec((tm, 128), lambda i: (i % per_b, 0))

    def os(width):
        return pl.BlockSpec((tm, width), lambda i: (i, 0))

    outs = [(1024, bf16)] * 5 + [(768, bf16), (512, bf16), (128, f32), (128, f32)]
    return pl.pallas_call(
        _prep_kernel,
        grid=(m // tm,),
        in_specs=[hs(1024, _P_AQ), hs(1024, _P_IQ), hs(1024, _P_BQ), hs(1024, _P_CQ), hs(1024, _P_CK),
                  hs(768, _P_BK), hs(512, _P_ALAT), hs(128, _P_IKW), hs(128, _P_BG),
                  pl.BlockSpec((1, A_LATENT), lambda i: (0, 0))] + [tspec] * 6,
        out_specs=[os(w) for w, _ in outs],
        out_shape=[jax.ShapeDtypeStruct((m, w), dt) for w, dt in outs],
        compiler_params=_cparams(("parallel",)),
    )(h, h, h, h, h, h, h, h, h, a_lat_g.reshape(1, A_LATENT), *tabs128, *tabs64)


def _f32_order_key(x):
    b = int(np.float32(x).view(np.int32))
    return b ^ ((b >> 31) & 0x7FFFFFFF)


_KEY_NEG = _f32_order_key(NEG)


def _dsa_mask_kernel(ik_ref, iq_ref, iw_ref, o_ref, key_ref, qp_ref, j_ref, *, L, tq, ksel):
    j = pl.program_id(1)
    ck = 512
    nck = L // ck
    nc = ((j + 1) * tq + ck - 1) // ck
    n_out_i = L - nc * ck
    n_out = n_out_i.astype(f32)
    t_row = j * tq + lax.broadcasted_iota(i32, (1, tq), 1)

    for hp in range(IDX_HEADS // 2):
        for e in range(2):
            h = 2 * hp + e
            qp_ref[hp, e * tq:(e + 1) * tq, :] = iq_ref[0, :, h * IDX_DIM:(h + 1) * IDX_DIM]

    def score_chunk(c, carry):
        off = pl.multiple_of(c * ck, ck)
        ikc = ik_ref[0, pl.ds(off, ck), :]
        acc = jnp.zeros((ck, tq), f32)
        for hp in range(IDX_HEADS // 2):
            s2 = _dot_nt(ikc, qp_ref[hp])
            acc = acc + jnp.maximum(s2[:, :tq], 0.0) * iw_ref[0, 2 * hp:2 * hp + 1, :]
            acc = acc + jnp.maximum(s2[:, tq:], 0.0) * iw_ref[0, 2 * hp + 1:2 * hp + 2, :]
        acc = jnp.where(acc == 0.0, 0.0, acc)
        sidx = off + lax.broadcasted_iota(i32, (ck, tq), 0)
        key_ref[pl.ds(off, ck), :] = jnp.where(sidx <= t_row, acc, NEG)
        return carry

    lax.fori_loop(0, nc, score_chunk, 0)

    def as_f32(key):
        return lax.bitcast_convert_type(key ^ ((key >> 31) & 0x7FFFFFFF), f32)

    def count(pred_fn):
        def body(c, cnt):
            off = pl.multiple_of(c * ck, ck)
            k = key_ref[pl.ds(off, ck), :]
            sidx = off + lax.broadcasted_iota(i32, (ck, tq), 0)
            return cnt + jnp.sum(pred_fn(k, sidx).reshape(ck // 64, 64, tq), axis=0)
        part = lax.fori_loop(0, nc, body, jnp.zeros((64, tq), f32))
        return jnp.sum(part, axis=0, keepdims=True)

    kf = float(ksel)

    def bit_body(i, carry):
        thr, cnt_thr = carry
        cand = thr + lax.shift_left(jnp.int32(1), 31 - i)
        cand_f = as_f32(cand)
        cnt = count(lambda k, s: jnp.where(k >= cand_f, 1.0, 0.0)) + jnp.where(_KEY_NEG >= cand, n_out, 0.0)
        ok = cnt >= kf
        return jnp.where(ok, cand, thr), jnp.where(ok, cnt, cnt_thr)

    thr_key, cnt_ge = lax.fori_loop(0, 32, bit_body, (jnp.full((1, tq), -2 ** 31, i32),
                                                       jnp.full((1, tq), float(L), f32)))
    thr = as_f32(thr_key)
    cnt_gt = count(lambda k, s: jnp.where(k > thr, 1.0, 0.0)) + jnp.where(_KEY_NEG > thr_key, n_out, 0.0)
    need = kf - cnt_gt

    j_ref[...] = jnp.full((1, tq), L, i32)

    @pl.when(jnp.max(cnt_ge) > kf)
    def _():
        nbits = L.bit_length() - 1

        def jbit(i, cur):
            cand = cur | lax.shift_left(jnp.int32(1), nbits - 1 - i)
            f = count(lambda k, s: jnp.where(k == thr, jnp.where(s < cand, 1.0, 0.0), 0.0))
            f = f + jnp.where(thr_key == _KEY_NEG, jnp.clip(cand - nc * ck, 0, n_out_i).astype(f32), 0.0)
            return jnp.where(f < need, cand, cur)

        j_ref[...] = lax.fori_loop(0, nbits, jbit, jnp.zeros((1, tq), i32))

    jlast = j_ref[...]

    def write(c, carry):
        off = pl.multiple_of(c * ck, ck)
        k = key_ref[pl.ds(off, ck), :]
        sidx = off + lax.broadcasted_iota(i32, (ck, tq), 0)
        sel = jnp.where(k > thr, _INF, jnp.where(k == thr, jnp.where(sidx <= jlast, _INF, NEG), NEG))
        o_ref[0, pl.ds(off, ck), :] = jnp.where(sidx <= t_row, sel, NEG)
        return carry

    lax.fori_loop(0, nc, write, 0)

    def write_rest(c, carry):
        off = pl.multiple_of(c * ck, ck)
        o_ref[0, pl.ds(off, ck), :] = jnp.full((ck, tq), NEG, f32)
        return carry

    lax.fori_loop(nc, nck, write_rest, 0)


def _dsa_mask(ik, iq, iw_t):
    B, L, _ = iq.shape
    tq = 128
    ksel = min(DSA_TOPK, L // 4)
    return pl.pallas_call(
        functools.partial(_dsa_mask_kernel, L=L, tq=tq, ksel=ksel),
        grid=(B, L // tq),
        in_specs=[pl.BlockSpec((1, L, IDX_DIM), lambda b, j: (b, 0, 0)),
                  pl.BlockSpec((1, tq, IDX_HEADS * IDX_DIM), lambda b, j: (b, j, 0)),
                  pl.BlockSpec((1, IDX_HEADS, tq), lambda b, j: (b, 0, j))],
        out_specs=pl.BlockSpec((1, L, tq), lambda b, j: (b, 0, j)),
        out_shape=jax.ShapeDtypeStruct((B, L, L), f32),
        scratch_shapes=[pltpu.VMEM((L, tq), f32), pltpu.VMEM((IDX_HEADS // 2, 2 * tq, IDX_DIM), bf16),
                        pltpu.VMEM((1, tq), i32)],
        compiler_params=_cparams(("parallel", "parallel")),
    )(ik, iq, iw_t)


def _col_reduce(x, op):
    r, c = x.shape
    if r > 64:
        x = op(x.reshape(r // 64, 64, c), axis=0)
    return op(x, axis=0, keepdims=True)


def _flash_group(ss, cap, v_ts, m_refs, acc_refs):
    ss, mloc = _flash_mask_max(ss, cap)
    _flash_update(ss, mloc, cap is not None, v_ts, m_refs, acc_refs)


def _flash_mask_max(ss, cap):
    if cap is not None:
        ss = [jnp.minimum(s, cap) for s in ss]
    return ss, [_col_reduce(s, jnp.max) for s in ss]


def _flash_update(ss, mloc, masked, v_ts, m_refs, acc_refs):
    n = len(ss)
    m_prev = [r[...] for r in m_refs]
    m_new = [jnp.maximum(m_prev[i], mloc[i]) for i in range(n)]
    alpha = [jnp.exp2(m_prev[i] - m_new[i]) for i in range(n)]
    ps = [jnp.exp2((ss[i] - m_new[i]).astype(bf16)) for i in range(n)]
    pv = [jnp.dot(v_ts[i], ps[i], preferred_element_type=f32) for i in range(n)]
    for i in range(n):
        acc_new = acc_refs[i][...] * alpha[i] + pv[i]
        if masked:
            acc_new = jnp.where(m_new[i] <= NEG, 0.0, acc_new)
        acc_refs[i][...] = acc_new
        m_refs[i][...] = m_new[i]


def _flash_heads(score_fns, cap, v_ts, m_ref, acc_ref, group):
    n = len(score_fns)
    groups = [list(range(g, min(g + group, n))) for g in range(0, n, group)]
    nxt = _flash_mask_max([score_fns[h]() for h in groups[0]], cap)
    for gi, hs in enumerate(groups):
        ss, mloc = nxt
        if gi + 1 < len(groups):
            nxt = _flash_mask_max([score_fns[h]() for h in groups[gi + 1]], cap)
        _flash_update(ss, mloc, cap is not None, [v_ts[h] for h in hs], [m_ref.at[h] for h in hs],
                      [acc_ref.at[h] for h in hs])


def _flash_init(m_ref, acc_ref):
    m_ref[...] = jnp.full(m_ref.shape, NEG, f32)
    acc_ref[...] = jnp.zeros(acc_ref.shape, f32)


def _flash_result(acc, d):
    return acc[:d] / jnp.maximum(acc[d:d + 1], 1e-30)


def _dsa_attn_kernel(q_ref, k_ref, vt_ref, mask_ref, o_ref, m_ref, acc_ref, *, tq, tk, nkc):
    j = pl.program_id(1)
    c = pl.program_id(2)

    @pl.when(c == 0)
    def _():
        _flash_init(m_ref, acc_ref)

    @pl.when(c * tk < (j + 1) * tq)
    def _():
        cap = mask_ref[0]

        def scores(h):
            sl = slice(h * HEAD_DIM, (h + 1) * HEAD_DIM)
            return lambda: _dot_nt(k_ref[0, :, sl], q_ref[0, :, sl])

        v_ts = [vt_ref[0, h, 0] for h in range(A_HEADS)]
        _flash_heads([scores(h) for h in range(A_HEADS)], cap, v_ts, m_ref, acc_ref, _HEAD_GROUP)

    @pl.when(c == nkc - 1)
    def _():
        for h in range(A_HEADS):
            o = _flash_result(acc_ref[h], HEAD_DIM)
            o_ref[0, :, h * HEAD_DIM:(h + 1) * HEAD_DIM] = o.T.astype(o_ref.dtype)


def _dsa_attn(q, k, v_t, mask_t):
    B, L, W = q.shape
    tq, tk = 512, 512
    nkc = L // tk

    def last(j):
        return ((j + 1) * tq - 1) // tk

    return pl.pallas_call(
        functools.partial(_dsa_attn_kernel, tq=tq, tk=tk, nkc=nkc),
        grid=(B, L // tq, nkc),
        in_specs=[pl.BlockSpec((1, tq, W), lambda b, j, c: (b, j, 0)),
                  pl.BlockSpec((1, tk, W), lambda b, j, c: (b, jnp.minimum(c, last(j)), 0)),
                  pl.BlockSpec((1, A_HEADS, 1, HEAD_DIM + _ONES, tk),
                               lambda b, j, c: (b, 0, jnp.minimum(c, last(j)), 0, 0)),
                  pl.BlockSpec((1, tk, tq), lambda b, j, c: (b, jnp.minimum(c, last(j)), j))],
        out_specs=pl.BlockSpec((1, tq, W), lambda b, j, c: (b, j, 0)),
        out_shape=jax.ShapeDtypeStruct((B, L, W), bf16),
        scratch_shapes=[pltpu.VMEM((A_HEADS, 1, tq), f32),
                        pltpu.VMEM((A_HEADS, HEAD_DIM + _ONES, tq), f32)],
        compiler_params=_cparams(("parallel", "parallel", "arbitrary")),
    )(q, k, v_t, mask_t)


def _nsa_cmp_kernel(x_ref, w1_ref, w2_ref, pe_ref, o_ref):
    x = x_ref[0, 0, 0]
    w1 = w1_ref[0]
    half = CMP_STRIDE * HEAD_DIM
    a = jnp.dot(x, w1[:half], preferred_element_type=f32)
    b = jnp.dot(x, w1[half:], preferred_element_type=f32)
    pe = jnp.dot(pe_ref[0], w1, preferred_element_type=f32)[0:1]
    n = a.shape[0]
    pre = a + pltpu.roll(b, n - 1, 0) + pe
    act = pre * _sigmoid(pre)
    o_ref[0, 0, 0] = jnp.dot(act.astype(bf16), w2_ref[0], preferred_element_type=f32).astype(o_ref.dtype)


def _nsa_compress(x, w1, w2, pe, l):
    B, _, G, n, wd = x.shape
    return pl.pallas_call(
        _nsa_cmp_kernel,
        grid=(B, 2, G),
        in_specs=[pl.BlockSpec((1, 1, 1, n, wd), lambda b, t, g: (b, t, g, 0, 0)),
                  pl.BlockSpec((None, 1, CMP_LEN * HEAD_DIM, HEAD_DIM), lambda b, t, g: (l, t, 0, 0)),
                  pl.BlockSpec((None, 1, HEAD_DIM, HEAD_DIM), lambda b, t, g: (l, t, 0, 0)),
                  pl.BlockSpec((None, 1, 8, CMP_LEN * HEAD_DIM), lambda b, t, g: (l, t, 0, 0))],
        out_specs=pl.BlockSpec((1, 1, 1, n, HEAD_DIM), lambda b, t, g: (b, t, g, 0, 0)),
        out_shape=jax.ShapeDtypeStruct((B, 2, G, n, HEAD_DIM), bf16),
        compiler_params=_cparams(("parallel", "parallel", "parallel")),
    )(x, w1, w2, pe)


def _nsa_attn_kernel(q_ref, kc_ref, vct_ref, ks_ref, vst_ref, kw_ref, vwt_ref, g_ref, cov_ref, o_ref,
                     m_ref, acc_ref, out_ref, sel_ref, *, L, tq, tk, n_sel):
    j = pl.program_id(2)
    t_row = j * tq + lax.broadcasted_iota(i32, (1, tq), 1)
    n_cmp = kc_ref.shape[2]
    n_slc = L // SLC_LEN
    qs = [q_ref[0, :, hh * HEAD_DIM:(hh + 1) * HEAD_DIM] for hh in range(B_HPG)]

    def gate(hh, i):
        return g_ref[0, 0, hh * 3 + i:hh * 3 + i + 1, :]

    kc = kc_ref[0, 0]
    vct = vct_ref[0, 0]
    cend = lax.broadcasted_iota(i32, (n_cmp, tq), 0) * CMP_STRIDE + (CMP_LEN - 1)
    mc = cend <= t_row
    ss = [jnp.where(mc, _dot_nt(kc, qs[hh]), NEG) for hh in range(B_HPG)]
    mxs = [_col_reduce(s, jnp.max) for s in ss]
    ps = [jnp.where(mc, jnp.exp2(ss[hh] - mxs[hh]), 0.0) for hh in range(B_HPG)]
    inv = [1.0 / jnp.maximum(_col_reduce(p, jnp.sum), 1e-30) for p in ps]
    pvs = [jnp.dot(vct, p.astype(bf16), preferred_element_type=f32) for p in ps]
    psum = ps[0] * inv[0]
    for hh in range(1, B_HPG):
        psum = psum + ps[hh] * inv[hh]
    for hh in range(B_HPG):
        out_ref[hh] = (gate(hh, 0) * inv[hh]) * pvs[hh]
    imp = jnp.dot(cov_ref[...], psum, preferred_element_type=f32, precision=lax.Precision.HIGHEST)
    blk = lax.broadcasted_iota(i32, (n_slc, tq), 0)
    cur = t_row >> (SLC_LEN.bit_length() - 1)
    forced = (blk == 0) | (blk == cur) | (blk == cur - 1)
    imp = jnp.where(forced, FORCE, jnp.where(blk <= cur, imp, NEG))
    rank = jnp.zeros((n_slc, tq), f32)
    for r in range(n_slc):
        row = imp[r:r + 1, :]
        rank = rank + jnp.where(row > imp, 1.0, jnp.where(row == imp, jnp.where(blk > r, 1.0, 0.0), 0.0))
    sel_ref[...] = jnp.where(rank < float(n_sel), _INF, NEG)

    def finish(i):
        for hh in range(B_HPG):
            out_ref[hh] = out_ref[hh] + gate(hh, i) * _flash_result(acc_ref[hh], HEAD_DIM)

    def run_branch(k_ref, vt_ref, c_lo, c_hi, mask_fn):
        _flash_init(m_ref, acc_ref)

        def body(c, carry):
            off = pl.multiple_of(c * tk, tk)
            kch = k_ref[0, pl.ds(off, tk), :]
            vch = vt_ref[0, 0, c]
            sidx = off + lax.broadcasted_iota(i32, (tk, tq), 0)
            cap = mask_fn(c, sidx)
            fns = [functools.partial(_dot_nt, kch, qs[hh]) for hh in range(B_HPG)]
            _flash_heads(fns, cap, [vch] * B_HPG, m_ref, acc_ref, _HEAD_GROUP)
            return carry

        lax.fori_loop(c_lo, c_hi, body, 0)

    bpc = tk // SLC_LEN

    def slc_mask(c, sidx):
        rows = [jnp.broadcast_to(sel_ref[pl.ds(c * bpc + b, 1), :], (SLC_LEN, tq)) for b in range(bpc)]
        return jnp.where(sidx <= t_row, jnp.concatenate(rows, axis=0), NEG)

    c_hi = ((j + 1) * tq + tk - 1) // tk
    run_branch(ks_ref, vst_ref, 0, c_hi, slc_mask)
    finish(1)

    def win_mask(c, sidx):
        d = t_row - sidx
        return jnp.where(d >= 0, jnp.where(d < WIN_LEN, _INF, NEG), NEG)

    c_lo = jnp.maximum(j * tq - (WIN_LEN - 1), 0) // tk
    run_branch(kw_ref, vwt_ref, c_lo, c_hi, win_mask)
    finish(2)

    for hh in range(B_HPG):
        o_ref[0, :, hh * HEAD_DIM:(hh + 1) * HEAD_DIM] = out_ref[hh].T.astype(o_ref.dtype)


def _nsa_attn(bq, kc, vc_t, bk, bv_tc, gates_t, cov_t):
    B, L, _ = bq.shape
    G = B_KV_GROUPS
    tq, tk = 512, 512
    n_cmp = kc.shape[2]
    n_slc = L // SLC_LEN
    n_sel = min(SLC_TOPN, n_slc)
    gw = B_HPG * HEAD_DIM
    nc = L // tk
    return pl.pallas_call(
        functools.partial(_nsa_attn_kernel, L=L, tq=tq, tk=tk, n_sel=n_sel),
        grid=(B, G, L // tq),
        in_specs=[pl.BlockSpec((1, tq, gw), lambda b, g, j: (b, j, g)),
                  pl.BlockSpec((1, 1, n_cmp, HEAD_DIM), lambda b, g, j: (b, g, 0, 0)),
                  pl.BlockSpec((1, 1, HEAD_DIM, n_cmp), lambda b, g, j: (b, g, 0, 0)),
                  pl.BlockSpec((1, L, HEAD_DIM), lambda b, g, j: (b, 0, 2 + g)),
                  pl.BlockSpec((1, 1, nc, HEAD_DIM + _ONES, tk), lambda b, g, j: (b, g, 0, 0, 0)),
                  pl.BlockSpec((1, L, HEAD_DIM), lambda b, g, j: (b, 0, 4 + g)),
                  pl.BlockSpec((1, 1, nc, HEAD_DIM + _ONES, tk),
                               lambda b, g, j: (b, B_KV_GROUPS + g, 0, 0, 0)),
                  pl.BlockSpec((1, 1, 3 * B_HPG, tq), lambda b, g, j: (b, g, 0, j)),
                  pl.BlockSpec((n_slc, n_cmp), lambda b, g, j: (0, 0))],
        out_specs=pl.BlockSpec((1, tq, gw), lambda b, g, j: (b, j, g)),
        out_shape=jax.ShapeDtypeStruct((B, L, B_HEADS * HEAD_DIM), bf16),
        scratch_shapes=[pltpu.VMEM((B_HPG, 1, tq), f32),
                        pltpu.VMEM((B_HPG, HEAD_DIM + _ONES, tq), f32),
                        pltpu.VMEM((B_HPG, HEAD_DIM, tq), f32), pltpu.VMEM((n_slc, tq), f32)],
        compiler_params=_cparams(("parallel", "parallel", "parallel")),
    )(bq, kc, vc_t, bk, bv_tc, bk, bv_tc, gates_t, cov_t)


def _diff_attn_kernel(q_ref, k_ref, vt_ref, lam_ref, g_ref, o_ref, m_ref, acc_ref,
                      *, tq, tk, lam_init):
    j = pl.program_id(2)
    t_row = j * tq + lax.broadcasted_iota(i32, (1, tq), 1)
    qs = [q_ref[0, :, mi * C_DIM:(mi + 1) * C_DIM] for mi in range(2)]
    _flash_init(m_ref, acc_ref)

    def chunk(c, lo, width, masked):
        off = pl.multiple_of(c * tk + lo, tq)
        vch = vt_ref[0, 0, c, :, lo:lo + width]
        cap = None
        if masked:
            sidx = off + lax.broadcasted_iota(i32, (width, tq), 0)
            cap = jnp.where(sidx <= t_row, _INF, NEG)
        ss = [_dot_nt(k_ref[0, pl.ds(off, width), mi * C_DIM:(mi + 1) * C_DIM], qs[mi]) for mi in range(2)]
        _flash_group(ss, cap, [vch] * 2, [m_ref.at[mi] for mi in range(2)],
                     [acc_ref.at[mi] for mi in range(2)])

    def body(c, carry):
        chunk(c, 0, tk, False)
        return carry

    c_diag = (j * tq) // tk
    lax.fori_loop(0, c_diag, body, 0)

    @pl.when(j % 2 == 0)
    def _():
        chunk(c_diag, 0, tq, True)

    @pl.when(j % 2 == 1)
    def _():
        chunk(c_diag, 0, tq, False)
        chunk(c_diag, tq, tq, True)

    lam = lam_ref[...]
    lam_val = (jnp.exp(jnp.sum(lam[0:1] * lam[1:2], axis=1, keepdims=True))
               - jnp.exp(jnp.sum(lam[2:3] * lam[3:4], axis=1, keepdims=True)) + lam_init)
    o = _flash_result(acc_ref[0], 2 * C_DIM) - lam_val * _flash_result(acc_ref[1], 2 * C_DIM)
    ms = jnp.mean(o * o, axis=0, keepdims=True)
    y = o * lax.rsqrt(ms + 1e-6) * g_ref[...] * (1.0 - lam_init)
    o_ref[0] = y.T.astype(o_ref.dtype)


def _diff_attn(cq, ck, cv_tc, lam, g_col, lam_init):
    B, L, _ = cq.shape
    tq, tk = _DIFF_TQ, cv_tc.shape[-1]
    assert tk == 2 * tq
    hw = 2 * C_DIM
    nc = L // tk
    return pl.pallas_call(
        functools.partial(_diff_attn_kernel, tq=tq, tk=tk, lam_init=lam_init),
        grid=(B, C_HEADS, L // tq),
        in_specs=[pl.BlockSpec((1, tq, hw), lambda b, h, j: (b, j, h)),
                  pl.BlockSpec((1, L, hw), lambda b, h, j: (b, 0, h)),
                  pl.BlockSpec((1, 1, nc, hw + _ONES, tk), lambda b, h, j: (b, h, 0, 0, 0)),
                  pl.BlockSpec((4, C_DIM), lambda b, h, j: (0, 0)),
                  pl.BlockSpec((hw, 1), lambda b, h, j: (0, 0))],
        out_specs=pl.BlockSpec((1, tq, hw), lambda b, h, j: (b, j, h)),
        out_shape=jax.ShapeDtypeStruct((B, L, C_HEADS * hw), bf16),
        scratch_shapes=[pltpu.VMEM((2, 1, tq), f32), pltpu.VMEM((2, hw + _ONES, tq), f32)],
        compiler_params=_cparams(("parallel", "parallel", "parallel")),
    )(cq, ck, cv_tc, lam, g_col)


def _merge_kernel(ya_ref, yb_ref, yc_ref, w_ref, g0_ref, g1_ref, g2_ref, o_ref, wb_ref):
    @pl.when(pl.program_id(1) == 0)
    def _():
        wb_ref[...] = w_ref[...].astype(bf16)

    acc = None
    for r, (y_ref, g_ref) in enumerate(((ya_ref, g0_ref), (yb_ref, g1_ref), (yc_ref, g2_ref))):
        br = jnp.dot(y_ref[...], wb_ref[r], preferred_element_type=f32)
        t = _sigmoid(g_ref[...].astype(f32)) * br
        acc = t if acc is None else acc + t
    o_ref[...] = acc.astype(o_ref.dtype)


def _merge(ya, yb, yc, w_br, l, h):
    m, kw = ya.shape
    tm, tn = 512, 512
    npb = D_MODEL // tn
    yspec = pl.BlockSpec((tm, kw), lambda j, i: (i, 0))

    def gspec(r):
        return pl.BlockSpec((tm, tn), lambda j, i, _r=r: (i, _r * npb + j))

    return pl.pallas_call(
        _merge_kernel,
        grid=(npb, m // tm),
        in_specs=[yspec, yspec, yspec,
                  pl.BlockSpec((None, N_BRANCH, kw, tn), lambda j, i: (l, 0, 0, j)),
                  gspec(0), gspec(1), gspec(2)],
        out_specs=pl.BlockSpec((tm, tn), lambda j, i: (i, j)),
        out_shape=jax.ShapeDtypeStruct((m, D_MODEL), bf16),
        scratch_shapes=[pltpu.VMEM((N_BRANCH, kw, tn), bf16)],
        compiler_params=_cparams(("parallel", "arbitrary")),
    )(ya, yb, yc, w_br, h, h, h)


def _mm_res_ln_kernel(a_ref, w_ref, x_ref, gate_ref, lg_ref, lb_ref, sc_ref, sh_ref, xo_ref, *u_refs,
                      alpha, sub):
    tm = a_ref.shape[0]
    for r in range(tm // sub):
        rows = slice(r * sub, (r + 1) * sub)
        y = jnp.dot(a_ref[rows, :], w_ref[...], preferred_element_type=f32)
        z = alpha * x_ref[rows, :] + gate_ref[0] * y
        xn = _ln_rows(z, 1e-5) * lg_ref[...] + lb_ref[...]
        xo_ref[rows, :] = xn
        if u_refs:
            u_refs[0][rows, :] = (_ln_rows(xn, 1e-5) * (1.0 + sc_ref[0]) + sh_ref[0]).astype(bf16)


def _mm_res_ln(a, w, l, x2, gate, ln_g, ln_b, sc, sh, L, alpha, emit_u):
    m, kdim = a.shape
    d = w.shape[2]
    tm = _RES_LN_TM if kdim * d * 2 <= 16 * 1024 * 1024 else _RES_LN_SUB
    per_b = L // tm
    bspec = pl.BlockSpec((1, 1, d), lambda i: (i // per_b, 0, 0))
    vspec = pl.BlockSpec((1, d), lambda i: (0, 0))
    rspec = pl.BlockSpec((tm, d), lambda i: (i, 0))
    out_shape = [jax.ShapeDtypeStruct((m, d), f32)]
    out_specs = [rspec]
    if emit_u:
        out_shape.append(jax.ShapeDtypeStruct((m, d), bf16))
        out_specs.append(rspec)
    res = pl.pallas_call(
        functools.partial(_mm_res_ln_kernel, alpha=alpha, sub=_RES_LN_SUB),
        grid=(m // tm,),
        in_specs=[pl.BlockSpec((tm, kdim), lambda i: (i, 0)),
                  pl.BlockSpec((None, kdim, d), lambda i: (l, 0, 0), pipeline_mode=pl.Buffered(1)),
                  rspec, bspec, vspec, vspec, bspec, bspec],
        out_specs=out_specs,
        out_shape=out_shape,
        compiler_params=_cparams(("parallel",), 60 * 1024 * 1024),
    )(a, w, x2, gate, ln_g.reshape(1, d), ln_b.reshape(1, d), sc, sh)
    return res if emit_u else (res[0], None)


def _ffn_in_kernel(a_ref, wg_ref, wu_ref, o_ref, wgb_ref, wub_ref):
    @pl.when(pl.program_id(1) == 0)
    def _():
        wgb_ref[...] = wg_ref[...].astype(bf16)
        wub_ref[...] = wu_ref[...].astype(bf16)

    a = a_ref[...]
    g = jnp.dot(a, wgb_ref[...], preferred_element_type=f32)
    u = jnp.dot(a, wub_ref[...], preferred_element_type=f32)
    o_ref[...] = (g * _sigmoid(g) * u).astype(o_ref.dtype)


def _ffn_in(a, w, l):
    m, k = a.shape
    tm, tn = 512, 512
    nb = D_FF // tn
    return pl.pallas_call(
        _ffn_in_kernel,
        grid=(nb, m // tm),
        in_specs=[pl.BlockSpec((tm, k), lambda j, i: (i, 0)),
                  pl.BlockSpec((None, k, tn), lambda j, i: (l, 0, j)),
                  pl.BlockSpec((None, k, tn), lambda j, i: (l, 0, nb + j))],
        out_specs=pl.BlockSpec((tm, tn), lambda j, i: (i, j)),
        out_shape=jax.ShapeDtypeStruct((m, D_FF), bf16),
        scratch_shapes=[pltpu.VMEM((k, tn), bf16), pltpu.VMEM((k, tn), bf16)],
        compiler_params=_cparams(("parallel", "arbitrary")),
    )(a, w, w)


def _rope_tables(L, d):
    r = d // 4
    half = r // 2
    inv = ROPE_THETA ** (-(jnp.arange(half, dtype=f32) * 2.0) / r)
    ang = jnp.arange(L).astype(f32)[:, None] * inv[None, :]
    cos, sin = jnp.cos(ang), jnp.sin(ang)
    z = jnp.zeros((L, d - r), f32)
    zh = jnp.zeros((L, half), f32)
    c = jnp.concatenate([cos, cos, jnp.ones((L, d - r), f32)], axis=1)
    s1 = jnp.concatenate([zh, sin, z], axis=1)
    s2 = jnp.concatenate([-sin, zh, z], axis=1)
    rep = 128 // d
    return tuple(jnp.tile(t, (1, rep)) for t in (c, s1, s2))


def _pack_w_in_kernel(w_ref, o_ref, ov_ref):
    def cp(src, width, dst):
        for o in range(0, width, 1024):
            wd = min(1024, width - o)
            o_ref[:, dst + o:dst + o + wd] = w_ref[src + o:src + o + wd, :].T.astype(bf16)

    def bkv(i, kv, g):
        return _O_BKV + ((i * 2 + kv) * B_KV_GROUPS + g) * HEAD_DIM

    cp(_O_GL, N_BRANCH * D_MODEL, _P_GL)
    for src, dst in ((_O_AQ, _P_AQ), (_O_IQ, _P_IQ), (_O_BQ, _P_BQ), (_O_CQ, _P_CQ), (_O_CK, _P_CK)):
        cp(src, 1024, dst)
    for i in range(3):
        for g in range(B_KV_GROUPS):
            cp(bkv(i, 0, g), HEAD_DIM, _P_BK + (i * B_KV_GROUPS + g) * HEAD_DIM)
    for g in range(B_KV_GROUPS):
        cp(bkv(0, 1, g), HEAD_DIM, _P_BVC + g * HEAD_DIM)
    cp(_O_ALAT, A_LATENT, _P_ALAT)
    ov_ref[_PV_CV:_PV_CV + 1024, :] = w_ref[_O_CV:_O_CV + 1024, :].astype(bf16)
    for i in (1, 2):
        for g in range(B_KV_GROUPS):
            dst = _PV_BV + ((i - 1) * B_KV_GROUPS + g) * HEAD_DIM
            ov_ref[dst:dst + HEAD_DIM, :] = w_ref[bkv(i, 1, g):bkv(i, 1, g) + HEAD_DIM, :].astype(bf16)
    lane = lax.broadcasted_iota(i32, (w_ref.shape[1], 128), 1)
    assert _O_IW == _O_IK + IDX_DIM and _O_IK % 8 == 0 and _O_BG % 8 == 0
    blk = w_ref[_O_IK:_O_IK + 128, :].T
    o_ref[:, _P_IKW:_P_IKW + 128] = jnp.where(lane < IDX_DIM + IDX_HEADS, blk, 0.0).astype(bf16)
    blk = w_ref[_O_BG:_O_BG + 128, :].T
    o_ref[:, _P_BG:_P_BG + 128] = jnp.where(lane < 3 * B_HEADS, blk, 0.0).astype(bf16)


def _pack_w_in(w_in):
    depth, k, n = w_in.shape
    w_t = jnp.swapaxes(w_in, 1, 2)
    tc = 128
    return pl.pallas_call(
        _pack_w_in_kernel,
        grid=(depth, k // tc),
        in_specs=[pl.BlockSpec((None, n, tc), lambda l, i: (l, 0, i))],
        out_specs=[pl.BlockSpec((None, tc, _P_TOT), lambda l, i: (l, i, 0)),
                   pl.BlockSpec((None, _PV_TOT, tc), lambda l, i: (l, 0, i))],
        out_shape=[jax.ShapeDtypeStruct((depth, k, _P_TOT), bf16),
                   jax.ShapeDtypeStruct((depth, _PV_TOT, k), bf16)],
        compiler_params=_cparams(("parallel", "parallel")),
    )(w_t)


def _cover_t(L):
    n_cmp_pad = L // CMP_STRIDE
    starts = np.arange(n_cmp_pad) * CMP_STRIDE
    slc_start = np.arange(L // SLC_LEN) * SLC_LEN
    cover = ((starts[:, None] < slc_start[None, :] + SLC_LEN)
             & (starts[:, None] + CMP_LEN > slc_start[None, :])).astype(np.float32)
    n_cmp = (L - CMP_LEN) // CMP_STRIDE + 1
    cover[n_cmp:] = 0.0
    return jnp.asarray(cover.T)


def _token_mixing(u, h, B, L, l, lw, tabs128, tabs64, cov_t, lam_init):
    M = B * L
    G = B_KV_GROUPS
    aq, iq, bq, cq, ck, bk, alat_n, ikw, gates = _prep(h, lw['a_lat_g'], tabs128, tabs64, L)

    aw = A_HEADS * HEAD_DIM
    ak = _mm_rope(alat_n, lw['a_up'], l, aw, tabs128, L, 512, aw)
    av_tc = _proj_t(alat_n, lw['a_up_vt'], l, 0, A_HEADS, HEAD_DIM, 512, B, L)
    ik = ikw[:, :IDX_DIM].astype(bf16).reshape(B, L, IDX_DIM)
    iw_t = ikw[:, IDX_DIM:IDX_DIM + IDX_HEADS].reshape(B, L, IDX_HEADS).transpose(0, 2, 1)
    mask_t = _dsa_mask(ik, iq.reshape(B, L, -1), iw_t)
    ya = _dsa_attn(aq.reshape(B, L, -1), ak.reshape(B, L, -1), av_tc, mask_t)

    bvc = h[:, _P_BVC:_P_BVC + G * HEAD_DIM].reshape(B, L, G, HEAD_DIM)
    bkc = bk[:, :G * HEAD_DIM].reshape(B, L, G, HEAD_DIM)
    n_row = L // CMP_STRIDE
    xcmp = jnp.stack([bkc, bvc], axis=1)
    xcmp = xcmp.transpose(0, 1, 3, 2, 4).reshape(B, 2, G, n_row, CMP_STRIDE * HEAD_DIM)
    cmp_out = _nsa_compress(xcmp, lw['cmp_w1'], lw['cmp_w2'], lw['cmp_pe'], l)
    kc = cmp_out[:, 0]
    vc_t = cmp_out[:, 1].transpose(0, 1, 3, 2)
    bv_tc = _proj_t(u, lw['w_vt'], l, _PV_BV, 2 * G, HEAD_DIM, 512, B, L)
    gates_t = gates[:, :3 * B_HEADS].reshape(B, L, G, 3 * B_HPG).transpose(0, 2, 3, 1)
    yb = _nsa_attn(bq.reshape(B, L, -1), kc, vc_t, bk.reshape(B, L, -1), bv_tc, gates_t, cov_t)

    cv_tc = _proj_t(u, lw['w_vt'], l, _PV_CV, C_HEADS, 2 * C_DIM, _DIFF_TK, B, L)
    yc = _diff_attn(cq.reshape(B, L, -1), ck.reshape(B, L, -1), cv_tc, lw['lam'],
                    lw['c_subln_g'].reshape(2 * C_DIM, 1), lam_init)

    return _merge(ya.reshape(M, -1), yb.reshape(M, -1), yc.reshape(M, -1), lw['w_br'], l, h)


def kernel(x, c, w_ada, b_ada, w_in, a_lat_g, a_up, cmp_w1, cmp_w2, cmp_pe, lam, c_subln_g, w_br, w_o,
           w_ffn_in, w_ffn_out, ln_g, ln_b):
    B, L, D = x.shape
    depth = w_ada.shape[0]
    M = B * L
    alpha = (2 * depth) ** 0.25

    c_pad = jnp.zeros((8, D), f32).at[:B].set(c)
    mod = _ada(c_pad, w_ada, b_ada)[:, :B]
    mods = [[mod[l, :, i * D:(i + 1) * D].reshape(B, 1, D) for i in range(6)] for l in range(depth)]

    tabs128 = _rope_tables(L, HEAD_DIM)
    tabs64 = _rope_tables(L, IDX_DIM)
    cov_t = _cover_t(L)

    w_in_p, w_vt = _pack_w_in(w_in)
    pe_flat = jnp.zeros((depth, 2, 8, CMP_LEN * HEAD_DIM), f32).at[:, :, 0].set(
        cmp_pe.reshape(depth, 2, CMP_LEN * HEAD_DIM)).astype(bf16)
    a_up_vt = jnp.swapaxes(a_up[:, :, A_HEADS * HEAD_DIM:], 1, 2).astype(bf16)
    wb = dict(a_up=a_up.astype(bf16), a_up_vt=a_up_vt, w_vt=w_vt, cmp_w1=cmp_w1.astype(bf16),
              cmp_w2=cmp_w2.astype(bf16), cmp_pe=pe_flat, w_br=w_br)
    w_o_b = w_o.astype(bf16)
    w_fo_b = w_ffn_out.astype(bf16)

    x2 = x.reshape(M, D)
    u = _lnmod(x2, mods[0][1], mods[0][0], L)
    for l in range(depth):
        lam_init = 0.8 - 0.6 * math.exp(-0.3 * l)
        sh_a, sc_a, g_a, sh_f, sc_f, g_f = mods[l]
        lw = dict(wb, a_lat_g=a_lat_g[l], lam=lam[l], c_subln_g=c_subln_g[l])
        h = _mm(u, w_in_p, l, 512, 768, bf16)
        merged = _token_mixing(u, h, B, L, l, lw, tabs128, tabs64, cov_t, lam_init)
        x2, u = _mm_res_ln(merged, w_o_b, l, x2, g_a, ln_g[l, 0], ln_b[l, 0], sc_f, sh_f, L, alpha, True)
        f = _ffn_in(u, w_ffn_in, l)
        last = l == depth - 1
        nsc, nsh = (sc_f, sh_f) if last else (mods[l + 1][1], mods[l + 1][0])
        x2, u = _mm_res_ln(f, w_fo_b, l, x2, g_f, ln_g[l, 1], ln_b[l, 1], nsc, nsh, L, alpha, not last)
    return x2.reshape(B, L, D)
```

```python
import functools
import math

import numpy as np
import jax
import jax.numpy as jnp
from jax import lax
from jax.experimental import pallas as pl
from jax.experimental.pallas import tpu as pltpu

f32 = jnp.float32
bf16 = jnp.bfloat16
i32 = jnp.int32

D_MODEL = 2048
HEAD_DIM = 128
ROPE_THETA = 500000.0
NEG = -1e30
FORCE = 1e6
A_HEADS = 8
A_LATENT = 512
IDX_HEADS = 16
IDX_DIM = 64
DSA_TOPK = 256
B_HEADS = 8
B_KV_GROUPS = 2
B_HPG = B_HEADS // B_KV_GROUPS
CMP_LEN = 32
CMP_STRIDE = 16
SLC_LEN = 64
SLC_TOPN = 16
WIN_LEN = 512
C_HEADS = 4
C_DIM = 128
BRANCH_W = A_HEADS * HEAD_DIM
N_BRANCH = 3
D_FF = int(math.ceil(8 * D_MODEL / 3 / 256)) * 256

_O_AQ = 0
_O_ALAT = _O_AQ + A_HEADS * HEAD_DIM
_O_IQ = _O_ALAT + A_LATENT
_O_IK = _O_IQ + IDX_HEADS * IDX_DIM
_O_IW = _O_IK + IDX_DIM
_O_BQ = _O_IW + IDX_HEADS
_O_BKV = _O_BQ + B_HEADS * HEAD_DIM
_O_BG = _O_BKV + 3 * 2 * B_KV_GROUPS * HEAD_DIM
_O_CQ = _O_BG + 3 * B_HEADS
_O_CK = _O_CQ + C_HEADS * 2 * C_DIM
_O_CV = _O_CK + C_HEADS * 2 * C_DIM
_O_GL = _O_CV + C_HEADS * 2 * C_DIM
_N_IN = _O_GL + N_BRANCH * D_MODEL

_P_GL = 0
_P_BK = 6144
_P_BVC = 6912
_P_AQ = 7168
_P_IQ = 8192
_P_BQ = 9216
_P_CQ = 10240
_P_CK = 11264
_P_ALAT = 12288
_P_IKW = 12800
_P_BG = 12928
_P_TOT = 13056
_PV_CV = 0
_PV_BV = 1024
_PV_TOT = 1536

_VMEM_LIMIT = 48 * 1024 * 1024
_LOG2E = 1.4426950408889634
_INF = float("inf")
_ONES = 16
_RES_LN_TM = 512
_RES_LN_SUB = 256
_DIFF_TK = 1024
_DIFF_TQ = 512
_HEAD_GROUP = 2


def _cparams(sem, vmem=_VMEM_LIMIT):
    return pltpu.CompilerParams(dimension_semantics=sem, vmem_limit_bytes=vmem)


def _sigmoid(x):
    return 1.0 / (1.0 + jnp.exp(-x))


def _dot_nt(a, b):
    return lax.dot_general(a, b, (((1,), (1,)), ((), ())), preferred_element_type=f32)


def _ada_kernel(c_ref, w_ref, b_ref, o_ref):
    c = c_ref[...]
    cs = c * _sigmoid(c)
    o_ref[0] = jnp.dot(cs, w_ref[0], preferred_element_type=f32,
                       precision=lax.Precision.HIGHEST) + b_ref[0]


def _ada(c_pad, w_ada, b_ada):
    depth, d, n = w_ada.shape
    tn = 512
    return pl.pallas_call(
        _ada_kernel,
        grid=(depth, n // tn),
        in_specs=[pl.BlockSpec((8, d), lambda l, j: (0, 0)),
                  pl.BlockSpec((1, d, tn), lambda l, j: (l, 0, j)),
                  pl.BlockSpec((1, 1, tn), lambda l, j: (l, 0, j))],
        out_specs=pl.BlockSpec((1, 8, tn), lambda l, j: (l, 0, j)),
        out_shape=jax.ShapeDtypeStruct((depth, 8, n), f32),
        compiler_params=_cparams(("parallel", "parallel")),
    )(c_pad, w_ada, b_ada.reshape(depth, 1, n))


def _ln_rows(x, eps):
    mu = jnp.mean(x, axis=-1, keepdims=True)
    d = x - mu
    var = jnp.mean(d * d, axis=-1, keepdims=True)
    return d * lax.rsqrt(var + eps)


def _lnmod_kernel(x_ref, sc_ref, sh_ref, o_ref):
    y = _ln_rows(x_ref[...], 1e-5)
    o_ref[...] = (y * (1.0 + sc_ref[0]) + sh_ref[0]).astype(o_ref.dtype)


def _lnmod(x2, sc, sh, L):
    m, d = x2.shape
    tm = 512
    per_b = L // tm
    return pl.pallas_call(
        _lnmod_kernel,
        grid=(m // tm,),
        in_specs=[pl.BlockSpec((tm, d), lambda i: (i, 0)),
                  pl.BlockSpec((1, 1, d), lambda i: (i // per_b, 0, 0)),
                  pl.BlockSpec((1, 1, d), lambda i: (i // per_b, 0, 0))],
        out_specs=pl.BlockSpec((tm, d), lambda i: (i, 0)),
        out_shape=jax.ShapeDtypeStruct((m, d), bf16),
        compiler_params=_cparams(("parallel",)),
    )(x2, sc, sh)


def _mm_kernel(a_ref, w_ref, o_ref):
    o_ref[...] = jnp.dot(a_ref[...], w_ref[...], preferred_element_type=f32).astype(o_ref.dtype)


def _mm(a, w, l, tm, tn, out_dtype):
    m, k = a.shape
    n = w.shape[2]
    return pl.pallas_call(
        _mm_kernel,
        grid=(n // tn, m // tm),
        in_specs=[pl.BlockSpec((tm, k), lambda j, i: (i, 0)),
                  pl.BlockSpec((None, k, tn), lambda j, i: (l, 0, j))],
        out_specs=pl.BlockSpec((tm, tn), lambda j, i: (i, j)),
        out_shape=jax.ShapeDtypeStruct((m, n), out_dtype),
        compiler_params=_cparams(("parallel", "parallel")),
    )(a, w)


def _proj_t_kernel(w_ref, a_ref, o_ref):
    n, d = o_ref.shape[1], o_ref.shape[3] - _ONES
    tk = o_ref.shape[4]
    res = _dot_nt(w_ref[...], a_ref[...])
    for i in range(n):
        o_ref[0, i, 0, :d, :] = res[i * d:(i + 1) * d, :].astype(o_ref.dtype)
        o_ref[0, i, 0, d:, :] = jnp.ones((_ONES, tk), o_ref.dtype)


def _proj_t(a, w_t, l, row0, n, d, tk, B, L):
    k = a.shape[1]
    rows = n * d
    nc = L // tk
    return pl.pallas_call(
        _proj_t_kernel,
        grid=(B, nc),
        in_specs=[pl.BlockSpec((None, rows, k), lambda b, c: (l, row0 // rows, 0)),
                  pl.BlockSpec((tk, k), lambda b, c: (b * nc + c, 0))],
        out_specs=pl.BlockSpec((1, n, 1, d + _ONES, tk), lambda b, c: (b, 0, c, 0, 0)),
        out_shape=jax.ShapeDtypeStruct((B, n, nc, d + _ONES, tk), bf16),
        compiler_params=_cparams(("parallel", "parallel")),
    )(w_t, a)


def _rope_low(shape, d, half):
    lane = lax.broadcasted_iota(i32, shape, 1)
    return (lane & (d - 1)) < half


def _rope_heads(x, c, s, low, half):
    partner = jnp.where(low, pltpu.roll(x, 128 - half, 1), pltpu.roll(x, half, 1))
    return x * c + partner * s


def _mm_rope_kernel(a_ref, w_ref, c_ref, s_ref, o_ref):
    acc = jnp.dot(a_ref[...], w_ref[...], preferred_element_type=f32)
    c, s = c_ref[...], s_ref[...]
    low = _rope_low(c.shape, HEAD_DIM, HEAD_DIM // 8)
    for h in range(acc.shape[1] // 128):
        sl = slice(h * 128, (h + 1) * 128)
        o_ref[:, sl] = _rope_heads(acc[:, sl], c, s, low, HEAD_DIM // 8).astype(o_ref.dtype)


def _mm_rope(a, w, l, n, tabs, L, tm, tn):
    m, k = a.shape
    per_b = L // tm
    tspec = pl.BlockSpec((tm, 128), lambda j, i: (i % per_b, 0))
    return pl.pallas_call(
        _mm_rope_kernel,
        grid=(n // tn, m // tm),
        in_specs=[pl.BlockSpec((tm, k), lambda j, i: (i, 0)),
                  pl.BlockSpec((None, k, tn), lambda j, i: (l, 0, j)),
                  tspec, tspec],
        out_specs=pl.BlockSpec((tm, tn), lambda j, i: (i, j)),
        out_shape=jax.ShapeDtypeStruct((m, n), bf16),
        compiler_params=_cparams(("parallel", "parallel")),
    )(a, w, *tabs)


def _prep_kernel(aq_ref, iq_ref, bq_ref, cq_ref, ck_ref, bk_ref, alat_ref, ikw_ref, bg_ref, alg_ref,
                 c_ref, s_ref, cq_ref_t, sq_ref_t, c6_ref, s6_ref,
                 aq_o, iq_o, bq_o, cq_o, ck_o, bk_o, alat_o, ikw_o, g_o):
    rows = 64
    plain = (c_ref, s_ref, _rope_low((rows, 128), HEAD_DIM, HEAD_DIM // 8), HEAD_DIM // 8)
    scaled = (cq_ref_t, sq_ref_t) + plain[2:]
    idx = (c6_ref, s6_ref, _rope_low((rows, 128), IDX_DIM, IDX_DIM // 8), IDX_DIM // 8)

    def rope(x, tabs, rs):
        return _rope_heads(x, tabs[0][rs, :], tabs[1][rs, :], tabs[2], tabs[3])

    def rope_all(src, dst, tabs):
        for r in range(src.shape[0] // rows):
            rs = slice(r * rows, (r + 1) * rows)
            for h in range(src.shape[1] // 128):
                sl = slice(h * 128, (h + 1) * 128)
                dst[rs, sl] = rope(src[rs, sl].astype(f32), tabs, rs).astype(dst.dtype)

    rope_all(aq_ref, aq_o, scaled)
    rope_all(bq_ref, bq_o, scaled)
    rope_all(cq_ref, cq_o, scaled)
    rope_all(ck_ref, ck_o, plain)
    rope_all(bk_ref, bk_o, plain)
    rope_all(iq_ref, iq_o, idx)

    a = alat_ref[...].astype(f32)
    ms = jnp.mean(a * a, axis=-1, keepdims=True)
    alat_o[...] = (a * lax.rsqrt(ms + 1e-6) * alg_ref[...]).astype(alat_o.dtype)

    isk = lax.broadcasted_iota(i32, (rows, 128), 1) < IDX_DIM
    for r in range(ikw_ref.shape[0] // rows):
        rs = slice(r * rows, (r + 1) * rows)
        x = ikw_ref[rs, :].astype(f32)
        mu = jnp.sum(jnp.where(isk, x, 0.0), axis=-1, keepdims=True) * (1.0 / IDX_DIM)
        d = jnp.where(isk, x - mu, 0.0)
        var = jnp.sum(d * d, axis=-1, keepdims=True) * (1.0 / IDX_DIM)
        y = d * lax.rsqrt(var + 1e-5)
        ikw_o[rs, :] = jnp.where(isk, rope(y, idx, rs), x * (IDX_DIM ** -0.5 * IDX_HEADS ** -0.5))

    g_o[...] = _sigmoid(bg_ref[...].astype(f32))


def _prep(h, a_lat_g, tabs, L):
    m = h.shape[0]
    tm = 512
    per_b = L // tm

    def hs(width, off):
        return pl.BlockSpec((tm, width), lambda i, _o=off // width: (i, _o))

    tspec = pl.BlockSpec((tm, 128), lambda i: (i % per_b, 0))

    def os(width):
        return pl.BlockSpec((tm, width), lambda i: (i, 0))

    outs = [(1024, bf16)] * 5 + [(768, bf16), (512, bf16), (128, f32), (128, f32)]
    return pl.pallas_call(
        _prep_kernel,
        grid=(m // tm,),
        in_specs=[hs(1024, _P_AQ), hs(1024, _P_IQ), hs(1024, _P_BQ), hs(1024, _P_CQ), hs(1024, _P_CK),
                  hs(768, _P_BK), hs(512, _P_ALAT), hs(128, _P_IKW), hs(128, _P_BG),
                  pl.BlockSpec((1, A_LATENT), lambda i: (0, 0))] + [tspec] * 6,
        out_specs=[os(w) for w, _ in outs],
        out_shape=[jax.ShapeDtypeStruct((m, w), dt) for w, dt in outs],
        compiler_params=_cparams(("parallel",)),
    )(h, h, h, h, h, h, h, h, h, a_lat_g.reshape(1, A_LATENT), *tabs['plain'], *tabs['query'], *tabs['idx'])


def _f32_order_key(x):
    b = int(np.float32(x).view(np.int32))
    return b ^ ((b >> 31) & 0x7FFFFFFF)


_KEY_NEG = _f32_order_key(NEG)


def _dsa_mask_kernel(ik_ref, iq_ref, iw_ref, o_ref, key_ref, qp_ref, j_ref, *, L, tq, ksel):
    j = pl.program_id(1)
    ck = 512
    nck = L // ck
    nc = ((j + 1) * tq + ck - 1) // ck
    n_out_i = L - nc * ck
    n_out = n_out_i.astype(f32)
    t_row = j * tq + lax.broadcasted_iota(i32, (1, tq), 1)

    for hp in range(IDX_HEADS // 2):
        for e in range(2):
            h = 2 * hp + e
            qp_ref[hp, e * tq:(e + 1) * tq, :] = iq_ref[0, :, h * IDX_DIM:(h + 1) * IDX_DIM]

    def score_chunk(c, carry):
        off = pl.multiple_of(c * ck, ck)
        ikc = ik_ref[0, pl.ds(off, ck), :]
        acc = jnp.zeros((ck, tq), f32)
        for hp in range(IDX_HEADS // 2):
            s2 = _dot_nt(ikc, qp_ref[hp])
            acc = acc + jnp.maximum(s2[:, :tq], 0.0) * iw_ref[0, 2 * hp:2 * hp + 1, :]
            acc = acc + jnp.maximum(s2[:, tq:], 0.0) * iw_ref[0, 2 * hp + 1:2 * hp + 2, :]
        acc = jnp.where(acc == 0.0, 0.0, acc)
        sidx = off + lax.broadcasted_iota(i32, (ck, tq), 0)
        key_ref[pl.ds(off, ck), :] = jnp.where(sidx <= t_row, acc, NEG)
        return carry

    lax.fori_loop(0, nc, score_chunk, 0)

    def as_f32(key):
        return lax.bitcast_convert_type(key ^ ((key >> 31) & 0x7FFFFFFF), f32)

    def count(pred_fn):
        def body(c, cnt):
            off = pl.multiple_of(c * ck, ck)
            k = key_ref[pl.ds(off, ck), :]
            sidx = off + lax.broadcasted_iota(i32, (ck, tq), 0)
            return cnt + jnp.sum(pred_fn(k, sidx).reshape(ck // 64, 64, tq), axis=0)
        part = lax.fori_loop(0, nc, body, jnp.zeros((64, tq), f32))
        return jnp.sum(part, axis=0, keepdims=True)

    kf = float(ksel)

    def bit_body(i, carry):
        thr, cnt_thr = carry
        cand = thr + lax.shift_left(jnp.int32(1), 31 - i)
        cand_f = as_f32(cand)
        cnt = count(lambda k, s: jnp.where(k >= cand_f, 1.0, 0.0)) + jnp.where(_KEY_NEG >= cand, n_out, 0.0)
        ok = cnt >= kf
        return jnp.where(ok, cand, thr), jnp.where(ok, cnt, cnt_thr)

    thr_key, cnt_ge = lax.fori_loop(0, 32, bit_body, (jnp.full((1, tq), -2 ** 31, i32),
                                                       jnp.full((1, tq), float(L), f32)))
    thr = as_f32(thr_key)
    cnt_gt = count(lambda k, s: jnp.where(k > thr, 1.0, 0.0)) + jnp.where(_KEY_NEG > thr_key, n_out, 0.0)
    need = kf - cnt_gt

    j_ref[...] = jnp.full((1, tq), L, i32)

    @pl.when(jnp.max(cnt_ge) > kf)
    def _():
        nbits = L.bit_length() - 1

        def jbit(i, cur):
            cand = cur | lax.shift_left(jnp.int32(1), nbits - 1 - i)
            f = count(lambda k, s: jnp.where(k == thr, jnp.where(s < cand, 1.0, 0.0), 0.0))
            f = f + jnp.where(thr_key == _KEY_NEG, jnp.clip(cand - nc * ck, 0, n_out_i).astype(f32), 0.0)
            return jnp.where(f < need, cand, cur)

        j_ref[...] = lax.fori_loop(0, nbits, jbit, jnp.zeros((1, tq), i32))

    jlast = j_ref[...]

    def write(c, carry):
        off = pl.multiple_of(c * ck, ck)
        k = key_ref[pl.ds(off, ck), :]
        sidx = off + lax.broadcasted_iota(i32, (ck, tq), 0)
        sel = jnp.where(k > thr, _INF, jnp.where(k == thr, jnp.where(sidx <= jlast, _INF, NEG), NEG))
        o_ref[0, pl.ds(off, ck), :] = jnp.where(sidx <= t_row, sel, NEG)
        return carry

    lax.fori_loop(0, nc, write, 0)

    def write_rest(c, carry):
        off = pl.multiple_of(c * ck, ck)
        o_ref[0, pl.ds(off, ck), :] = jnp.full((ck, tq), NEG, f32)
        return carry

    lax.fori_loop(nc, nck, write_rest, 0)


def _dsa_mask(ik, iq, iw_t):
    B, L, _ = iq.shape
    tq = 128
    ksel = min(DSA_TOPK, L // 4)
    return pl.pallas_call(
        functools.partial(_dsa_mask_kernel, L=L, tq=tq, ksel=ksel),
        grid=(B, L // tq),
        in_specs=[pl.BlockSpec((1, L, IDX_DIM), lambda b, j: (b, 0, 0)),
                  pl.BlockSpec((1, tq, IDX_HEADS * IDX_DIM), lambda b, j: (b, j, 0)),
                  pl.BlockSpec((1, IDX_HEADS, tq), lambda b, j: (b, 0, j))],
        out_specs=pl.BlockSpec((1, L, tq), lambda b, j: (b, 0, j)),
        out_shape=jax.ShapeDtypeStruct((B, L, L), f32),
        scratch_shapes=[pltpu.VMEM((L, tq), f32), pltpu.VMEM((IDX_HEADS // 2, 2 * tq, IDX_DIM), bf16),
                        pltpu.VMEM((1, tq), i32)],
        compiler_params=_cparams(("parallel", "parallel")),
    )(ik, iq, iw_t)


def _col_reduce(x, op):
    r, c = x.shape
    if r > 64:
        x = op(x.reshape(r // 64, 64, c), axis=0)
    return op(x, axis=0, keepdims=True)


def _flash_group(ss, cap, v_ts, m_refs, acc_refs):
    ss, mloc = _flash_mask_max(ss, cap)
    _flash_update(ss, mloc, cap is not None, v_ts, m_refs, acc_refs)


def _flash_mask_max(ss, cap):
    if cap is not None:
        ss = [jnp.minimum(s, cap) for s in ss]
    return ss, [_col_reduce(s, jnp.max) for s in ss]


def _flash_update(ss, mloc, masked, v_ts, m_refs, acc_refs):
    n = len(ss)
    m_prev = [r[...] for r in m_refs]
    m_new = [jnp.maximum(m_prev[i], mloc[i]) for i in range(n)]
    alpha = [jnp.exp2(m_prev[i] - m_new[i]) for i in range(n)]
    ps = [jnp.exp2((ss[i] - m_new[i]).astype(bf16)) for i in range(n)]
    pv = [jnp.dot(v_ts[i], ps[i], preferred_element_type=f32) for i in range(n)]
    for i in range(n):
        acc_new = acc_refs[i][...] * alpha[i] + pv[i]
        if masked:
            acc_new = jnp.where(m_new[i] <= NEG, 0.0, acc_new)
        acc_refs[i][...] = acc_new
        m_refs[i][...] = m_new[i]


def _flash_heads(score_fns, cap, v_ts, m_ref, acc_ref, group):
    n = len(score_fns)
    groups = [list(range(g, min(g + group, n))) for g in range(0, n, group)]
    nxt = _flash_mask_max([score_fns[h]() for h in groups[0]], cap)
    for gi, hs in enumerate(groups):
        ss, mloc = nxt
        if gi + 1 < len(groups):
            nxt = _flash_mask_max([score_fns[h]() for h in groups[gi + 1]], cap)
        _flash_update(ss, mloc, cap is not None, [v_ts[h] for h in hs], [m_ref.at[h] for h in hs],
                      [acc_ref.at[h] for h in hs])


def _flash_init(m_ref, acc_ref):
    m_ref[...] = jnp.full(m_ref.shape, NEG, f32)
    acc_ref[...] = jnp.zeros(acc_ref.shape, f32)


def _flash_result(acc, d):
    return acc[:d] / jnp.maximum(acc[d:d + 1], 1e-30)


def _dsa_attn_kernel(q_ref, k_ref, vt_ref, mask_ref, o_ref, m_ref, acc_ref, *, tq, tk, nkc):
    j = pl.program_id(1)
    c = pl.program_id(2)

    @pl.when(c == 0)
    def _():
        _flash_init(m_ref, acc_ref)

    @pl.when(c * tk < (j + 1) * tq)
    def _():
        cap = mask_ref[0]

        def scores(h):
            sl = slice(h * HEAD_DIM, (h + 1) * HEAD_DIM)
            return lambda: _dot_nt(k_ref[0, :, sl], q_ref[0, :, sl])

        v_ts = [vt_ref[0, h, 0] for h in range(A_HEADS)]
        _flash_heads([scores(h) for h in range(A_HEADS)], cap, v_ts, m_ref, acc_ref, _HEAD_GROUP)

    @pl.when(c == nkc - 1)
    def _():
        for h in range(A_HEADS):
            o = _flash_result(acc_ref[h], HEAD_DIM)
            o_ref[0, :, h * HEAD_DIM:(h + 1) * HEAD_DIM] = o.T.astype(o_ref.dtype)


def _dsa_attn(q, k, v_t, mask_t):
    B, L, W = q.shape
    tq, tk = 512, 512
    nkc = L // tk

    def last(j):
        return ((j + 1) * tq - 1) // tk

    return pl.pallas_call(
        functools.partial(_dsa_attn_kernel, tq=tq, tk=tk, nkc=nkc),
        grid=(B, L // tq, nkc),
        in_specs=[pl.BlockSpec((1, tq, W), lambda b, j, c: (b, j, 0)),
                  pl.BlockSpec((1, tk, W), lambda b, j, c: (b, jnp.minimum(c, last(j)), 0)),
                  pl.BlockSpec((1, A_HEADS, 1, HEAD_DIM + _ONES, tk),
                               lambda b, j, c: (b, 0, jnp.minimum(c, last(j)), 0, 0)),
                  pl.BlockSpec((1, tk, tq), lambda b, j, c: (b, jnp.minimum(c, last(j)), j))],
        out_specs=pl.BlockSpec((1, tq, W), lambda b, j, c: (b, j, 0)),
        out_shape=jax.ShapeDtypeStruct((B, L, W), bf16),
        scratch_shapes=[pltpu.VMEM((A_HEADS, 1, tq), f32),
                        pltpu.VMEM((A_HEADS, HEAD_DIM + _ONES, tq), f32)],
        compiler_params=_cparams(("parallel", "parallel", "arbitrary")),
    )(q, k, v_t, mask_t)


def _nsa_cmp_kernel(x_ref, w1_ref, w2_ref, pe_ref, o_ref):
    x = x_ref[0, 0, 0]
    w1 = w1_ref[0]
    half = CMP_STRIDE * HEAD_DIM
    a = jnp.dot(x, w1[:half], preferred_element_type=f32)
    b = jnp.dot(x, w1[half:], preferred_element_type=f32)
    pe = jnp.dot(pe_ref[0], w1, preferred_element_type=f32)[0:1]
    n = a.shape[0]
    pre = a + pltpu.roll(b, n - 1, 0) + pe
    act = pre * _sigmoid(pre)
    o_ref[0, 0, 0] = jnp.dot(act.astype(bf16), w2_ref[0], preferred_element_type=f32).astype(o_ref.dtype)


def _nsa_compress(x, w1, w2, pe, l):
    B, _, G, n, wd = x.shape
    return pl.pallas_call(
        _nsa_cmp_kernel,
        grid=(B, 2, G),
        in_specs=[pl.BlockSpec((1, 1, 1, n, wd), lambda b, t, g: (b, t, g, 0, 0)),
                  pl.BlockSpec((None, 1, CMP_LEN * HEAD_DIM, HEAD_DIM), lambda b, t, g: (l, t, 0, 0)),
                  pl.BlockSpec((None, 1, HEAD_DIM, HEAD_DIM), lambda b, t, g: (l, t, 0, 0)),
                  pl.BlockSpec((None, 1, 8, CMP_LEN * HEAD_DIM), lambda b, t, g: (l, t, 0, 0))],
        out_specs=pl.BlockSpec((1, 1, 1, n, HEAD_DIM), lambda b, t, g: (b, t, g, 0, 0)),
        out_shape=jax.ShapeDtypeStruct((B, 2, G, n, HEAD_DIM), bf16),
        compiler_params=_cparams(("parallel", "parallel", "parallel")),
    )(x, w1, w2, pe)


def _nsa_attn_kernel(q_ref, kc_ref, vct_ref, ks_ref, vst_ref, kw_ref, vwt_ref, g_ref, cov_ref, o_ref,
                     m_ref, acc_ref, out_ref, sel_ref, *, L, tq, tk, n_sel):
    j = pl.program_id(2)
    t_row = j * tq + lax.broadcasted_iota(i32, (1, tq), 1)
    n_cmp = kc_ref.shape[2]
    n_slc = L // SLC_LEN
    qs = [q_ref[0, :, hh * HEAD_DIM:(hh + 1) * HEAD_DIM] for hh in range(B_HPG)]

    def gate(hh, i):
        return g_ref[0, 0, hh * 3 + i:hh * 3 + i + 1, :]

    kc = kc_ref[0, 0]
    vct = vct_ref[0, 0]
    cend = lax.broadcasted_iota(i32, (n_cmp, tq), 0) * CMP_STRIDE + (CMP_LEN - 1)
    mc = cend <= t_row
    ss = [jnp.where(mc, _dot_nt(kc, qs[hh]), NEG) for hh in range(B_HPG)]
    mxs = [_col_reduce(s, jnp.max) for s in ss]
    ps = [jnp.where(mc, jnp.exp2(ss[hh] - mxs[hh]), 0.0) for hh in range(B_HPG)]
    inv = [1.0 / jnp.maximum(_col_reduce(p, jnp.sum), 1e-30) for p in ps]
    pvs = [jnp.dot(vct, p.astype(bf16), preferred_element_type=f32) for p in ps]
    psum = ps[0] * inv[0]
    for hh in range(1, B_HPG):
        psum = psum + ps[hh] * inv[hh]
    for hh in range(B_HPG):
        out_ref[hh] = (gate(hh, 0) * inv[hh]) * pvs[hh]
    imp = jnp.dot(cov_ref[...], psum, preferred_element_type=f32, precision=lax.Precision.HIGHEST)
    blk = lax.broadcasted_iota(i32, (n_slc, tq), 0)
    cur = t_row >> (SLC_LEN.bit_length() - 1)
    forced = (blk == 0) | (blk == cur) | (blk == cur - 1)
    imp = jnp.where(forced, FORCE, jnp.where(blk <= cur, imp, NEG))
    rank = jnp.zeros((n_slc, tq), f32)
    for r in range(n_slc):
        row = imp[r:r + 1, :]
        rank = rank + jnp.where(row > imp, 1.0, jnp.where(row == imp, jnp.where(blk > r, 1.0, 0.0), 0.0))
    sel_ref[...] = jnp.where(rank < float(n_sel), _INF, NEG)

    def finish(i):
        for hh in range(B_HPG):
            out_ref[hh] = out_ref[hh] + gate(hh, i) * _flash_result(acc_ref[hh], HEAD_DIM)

    def run_branch(k_ref, vt_ref, c_lo, c_hi, mask_fn):
        _flash_init(m_ref, acc_ref)

        def body(c, carry):
            off = pl.multiple_of(c * tk, tk)
            kch = k_ref[0, pl.ds(off, tk), :]
            vch = vt_ref[0, 0, c]
            sidx = off + lax.broadcasted_iota(i32, (tk, tq), 0)
            cap = mask_fn(c, sidx)
            fns = [functools.partial(_dot_nt, kch, qs[hh]) for hh in range(B_HPG)]
            _flash_heads(fns, cap, [vch] * B_HPG, m_ref, acc_ref, _HEAD_GROUP)
            return carry

        lax.fori_loop(c_lo, c_hi, body, 0)

    bpc = tk // SLC_LEN

    def slc_mask(c, sidx):
        rows = [jnp.broadcast_to(sel_ref[pl.ds(c * bpc + b, 1), :], (SLC_LEN, tq)) for b in range(bpc)]
        return jnp.where(sidx <= t_row, jnp.concatenate(rows, axis=0), NEG)

    c_hi = ((j + 1) * tq + tk - 1) // tk
    run_branch(ks_ref, vst_ref, 0, c_hi, slc_mask)
    finish(1)

    def win_mask(c, sidx):
        d = t_row - sidx
        return jnp.where(d >= 0, jnp.where(d < WIN_LEN, _INF, NEG), NEG)

    c_lo = jnp.maximum(j * tq - (WIN_LEN - 1), 0) // tk
    run_branch(kw_ref, vwt_ref, c_lo, c_hi, win_mask)
    finish(2)

    for hh in range(B_HPG):
        o_ref[0, :, hh * HEAD_DIM:(hh + 1) * HEAD_DIM] = out_ref[hh].T.astype(o_ref.dtype)


def _nsa_attn(bq, kc, vc_t, bk, bv_tc, gates_t, cov_t):
    B, L, _ = bq.shape
    G = B_KV_GROUPS
    tq, tk = 512, 512
    n_cmp = kc.shape[2]
    n_slc = L // SLC_LEN
    n_sel = min(SLC_TOPN, n_slc)
    gw = B_HPG * HEAD_DIM
    nc = L // tk
    return pl.pallas_call(
        functools.partial(_nsa_attn_kernel, L=L, tq=tq, tk=tk, n_sel=n_sel),
        grid=(B, G, L // tq),
        in_specs=[pl.BlockSpec((1, tq, gw), lambda b, g, j: (b, j, g)),
                  pl.BlockSpec((1, 1, n_cmp, HEAD_DIM), lambda b, g, j: (b, g, 0, 0)),
                  pl.BlockSpec((1, 1, HEAD_DIM, n_cmp), lambda b, g, j: (b, g, 0, 0)),
                  pl.BlockSpec((1, L, HEAD_DIM), lambda b, g, j: (b, 0, 2 + g)),
                  pl.BlockSpec((1, 1, nc, HEAD_DIM + _ONES, tk), lambda b, g, j: (b, g, 0, 0, 0)),
                  pl.BlockSpec((1, L, HEAD_DIM), lambda b, g, j: (b, 0, 4 + g)),
                  pl.BlockSpec((1, 1, nc, HEAD_DIM + _ONES, tk),
                               lambda b, g, j: (b, B_KV_GROUPS + g, 0, 0, 0)),
                  pl.BlockSpec((1, 1, 3 * B_HPG, tq), lambda b, g, j: (b, g, 0, j)),
                  pl.BlockSpec((n_slc, n_cmp), lambda b, g, j: (0, 0))],
        out_specs=pl.BlockSpec((1, tq, gw), lambda b, g, j: (b, j, g)),
        out_shape=jax.ShapeDtypeStruct((B, L, B_HEADS * HEAD_DIM), bf16),
        scratch_shapes=[pltpu.VMEM((B_HPG, 1, tq), f32),
                        pltpu.VMEM((B_HPG, HEAD_DIM + _ONES, tq), f32),
                        pltpu.VMEM((B_HPG, HEAD_DIM, tq), f32), pltpu.VMEM((n_slc, tq), f32)],
        compiler_params=_cparams(("parallel", "parallel", "parallel")),
    )(bq, kc, vc_t, bk, bv_tc, bk, bv_tc, gates_t, cov_t)


def _diff_attn_kernel(q_ref, k_ref, vt_ref, lam_ref, g_ref, o_ref, m_ref, acc_ref,
                      *, tq, tk, lam_init):
    j = pl.program_id(2)
    t_row = j * tq + lax.broadcasted_iota(i32, (1, tq), 1)
    qs = [q_ref[0, :, mi * C_DIM:(mi + 1) * C_DIM] for mi in range(2)]
    _flash_init(m_ref, acc_ref)

    def chunk(c, lo, width, masked):
        off = pl.multiple_of(c * tk + lo, tq)
        vch = vt_ref[0, 0, c, :, lo:lo + width]
        cap = None
        if masked:
            sidx = off + lax.broadcasted_iota(i32, (width, tq), 0)
            cap = jnp.where(sidx <= t_row, _INF, NEG)
        ss = [_dot_nt(k_ref[0, pl.ds(off, width), mi * C_DIM:(mi + 1) * C_DIM], qs[mi]) for mi in range(2)]
        _flash_group(ss, cap, [vch] * 2, [m_ref.at[mi] for mi in range(2)],
                     [acc_ref.at[mi] for mi in range(2)])

    def body(c, carry):
        chunk(c, 0, tk, False)
        return carry

    c_diag = (j * tq) // tk
    lax.fori_loop(0, c_diag, body, 0)

    @pl.when(j % 2 == 0)
    def _():
        chunk(c_diag, 0, tq, True)

    @pl.when(j % 2 == 1)
    def _():
        chunk(c_diag, 0, tq, False)
        chunk(c_diag, tq, tq, True)

    lam = lam_ref[...]
    lam_val = (jnp.exp(jnp.sum(lam[0:1] * lam[1:2], axis=1, keepdims=True))
               - jnp.exp(jnp.sum(lam[2:3] * lam[3:4], axis=1, keepdims=True)) + lam_init)
    o = _flash_result(acc_ref[0], 2 * C_DIM) - lam_val * _flash_result(acc_ref[1], 2 * C_DIM)
    ms = jnp.mean(o * o, axis=0, keepdims=True)
    y = o * lax.rsqrt(ms + 1e-6) * g_ref[...] * (1.0 - lam_init)
    o_ref[0] = y.T.astype(o_ref.dtype)


def _diff_attn(cq, ck, cv_tc, lam, g_col, lam_init):
    B, L, _ = cq.shape
    tq, tk = _DIFF_TQ, cv_tc.shape[-1]
    assert tk == 2 * tq
    hw = 2 * C_DIM
    nc = L // tk
    return pl.pallas_call(
        functools.partial(_diff_attn_kernel, tq=tq, tk=tk, lam_init=lam_init),
        grid=(B, C_HEADS, L // tq),
        in_specs=[pl.BlockSpec((1, tq, hw), lambda b, h, j: (b, j, h)),
                  pl.BlockSpec((1, L, hw), lambda b, h, j: (b, 0, h)),
                  pl.BlockSpec((1, 1, nc, hw + _ONES, tk), lambda b, h, j: (b, h, 0, 0, 0)),
                  pl.BlockSpec((4, C_DIM), lambda b, h, j: (0, 0)),
                  pl.BlockSpec((hw, 1), lambda b, h, j: (0, 0))],
        out_specs=pl.BlockSpec((1, tq, hw), lambda b, h, j: (b, j, h)),
        out_shape=jax.ShapeDtypeStruct((B, L, C_HEADS * hw), bf16),
        scratch_shapes=[pltpu.VMEM((2, 1, tq), f32), pltpu.VMEM((2, hw + _ONES, tq), f32)],
        compiler_params=_cparams(("parallel", "parallel", "parallel")),
    )(cq, ck, cv_tc, lam, g_col)


def _merge_kernel(ya_ref, yb_ref, yc_ref, w_ref, g0_ref, g1_ref, g2_ref, o_ref, wb_ref):
    @pl.when(pl.program_id(1) == 0)
    def _():
        wb_ref[...] = w_ref[...].astype(bf16)

    acc = None
    for r, (y_ref, g_ref) in enumerate(((ya_ref, g0_ref), (yb_ref, g1_ref), (yc_ref, g2_ref))):
        br = jnp.dot(y_ref[...], wb_ref[r], preferred_element_type=f32)
        t = _sigmoid(g_ref[...].astype(f32)) * br
        acc = t if acc is None else acc + t
    o_ref[...] = acc.astype(o_ref.dtype)


def _merge(ya, yb, yc, w_br, l, h):
    m, kw = ya.shape
    tm, tn = 512, 512
    npb = D_MODEL // tn
    yspec = pl.BlockSpec((tm, kw), lambda j, i: (i, 0))

    def gspec(r):
        return pl.BlockSpec((tm, tn), lambda j, i, _r=r: (i, _r * npb + j))

    return pl.pallas_call(
        _merge_kernel,
        grid=(npb, m // tm),
        in_specs=[yspec, yspec, yspec,
                  pl.BlockSpec((None, N_BRANCH, kw, tn), lambda j, i: (l, 0, 0, j)),
                  gspec(0), gspec(1), gspec(2)],
        out_specs=pl.BlockSpec((tm, tn), lambda j, i: (i, j)),
        out_shape=jax.ShapeDtypeStruct((m, D_MODEL), bf16),
        scratch_shapes=[pltpu.VMEM((N_BRANCH, kw, tn), bf16)],
        compiler_params=_cparams(("parallel", "arbitrary")),
    )(ya, yb, yc, w_br, h, h, h)


def _mm_res_ln_kernel(a_ref, w_ref, x_ref, gate_ref, lg_ref, lb_ref, sc_ref, sh_ref, xo_ref, *u_refs,
                      alpha, sub):
    tm = a_ref.shape[0]
    for r in range(tm // sub):
        rows = slice(r * sub, (r + 1) * sub)
        y = jnp.dot(a_ref[rows, :], w_ref[...], preferred_element_type=f32)
        z = alpha * x_ref[rows, :] + gate_ref[0] * y
        xn = _ln_rows(z, 1e-5) * lg_ref[...] + lb_ref[...]
        xo_ref[rows, :] = xn
        if u_refs:
            u_refs[0][rows, :] = (_ln_rows(xn, 1e-5) * (1.0 + sc_ref[0]) + sh_ref[0]).astype(bf16)


def _mm_res_ln(a, w, l, x2, gate, ln_g, ln_b, sc, sh, L, alpha, emit_u):
    m, kdim = a.shape
    d = w.shape[2]
    tm = _RES_LN_TM if kdim * d * 2 <= 16 * 1024 * 1024 else _RES_LN_SUB
    per_b = L // tm
    bspec = pl.BlockSpec((1, 1, d), lambda i: (i // per_b, 0, 0))
    vspec = pl.BlockSpec((1, d), lambda i: (0, 0))
    rspec = pl.BlockSpec((tm, d), lambda i: (i, 0))
    out_shape = [jax.ShapeDtypeStruct((m, d), f32)]
    out_specs = [rspec]
    if emit_u:
        out_shape.append(jax.ShapeDtypeStruct((m, d), bf16))
        out_specs.append(rspec)
    res = pl.pallas_call(
        functools.partial(_mm_res_ln_kernel, alpha=alpha, sub=_RES_LN_SUB),
        grid=(m // tm,),
        in_specs=[pl.BlockSpec((tm, kdim), lambda i: (i, 0)),
                  pl.BlockSpec((None, kdim, d), lambda i: (l, 0, 0), pipeline_mode=pl.Buffered(1)),
                  rspec, bspec, vspec, vspec, bspec, bspec],
        out_specs=out_specs,
        out_shape=out_shape,
        compiler_params=_cparams(("parallel",), 60 * 1024 * 1024),
    )(a, w, x2, gate, ln_g.reshape(1, d), ln_b.reshape(1, d), sc, sh)
    return res if emit_u else (res[0], None)


def _ffn_in_kernel(a_ref, wg_ref, wu_ref, o_ref, wgb_ref, wub_ref):
    @pl.when(pl.program_id(1) == 0)
    def _():
        wgb_ref[...] = wg_ref[...].astype(bf16)
        wub_ref[...] = wu_ref[...].astype(bf16)

    a = a_ref[...]
    g = jnp.dot(a, wgb_ref[...], preferred_element_type=f32)
    u = jnp.dot(a, wub_ref[...], preferred_element_type=f32)
    o_ref[...] = (g * _sigmoid(g) * u).astype(o_ref.dtype)


def _ffn_in(a, w, l):
    m, k = a.shape
    tm, tn = 512, 512
    nb = D_FF // tn
    return pl.pallas_call(
        _ffn_in_kernel,
        grid=(nb, m // tm),
        in_specs=[pl.BlockSpec((tm, k), lambda j, i: (i, 0)),
                  pl.BlockSpec((None, k, tn), lambda j, i: (l, 0, j)),
                  pl.BlockSpec((None, k, tn), lambda j, i: (l, 0, nb + j))],
        out_specs=pl.BlockSpec((tm, tn), lambda j, i: (i, j)),
        out_shape=jax.ShapeDtypeStruct((m, D_FF), bf16),
        scratch_shapes=[pltpu.VMEM((k, tn), bf16), pltpu.VMEM((k, tn), bf16)],
        compiler_params=_cparams(("parallel", "arbitrary")),
    )(a, w, w)


def _rope_tables(L, d, mult=1.0):
    r = d // 4
    half = r // 2
    inv = ROPE_THETA ** (-(jnp.arange(half, dtype=f32) * 2.0) / r)
    ang = jnp.arange(L).astype(f32)[:, None] * inv[None, :]
    cos, sin = jnp.cos(ang), jnp.sin(ang)
    c = jnp.concatenate([cos, cos, jnp.ones((L, d - r), f32)], axis=1)
    s = jnp.concatenate([-sin, sin, jnp.zeros((L, d - r), f32)], axis=1)
    rep = 128 // d
    return tuple(jnp.tile(t * mult, (1, rep)) for t in (c, s))


def _pack_w_in_kernel(w_ref, o_ref, ov_ref):
    def cp(src, width, dst):
        for o in range(0, width, 1024):
            wd = min(1024, width - o)
            o_ref[:, dst + o:dst + o + wd] = w_ref[src + o:src + o + wd, :].T.astype(bf16)

    def bkv(i, kv, g):
        return _O_BKV + ((i * 2 + kv) * B_KV_GROUPS + g) * HEAD_DIM

    cp(_O_GL, N_BRANCH * D_MODEL, _P_GL)
    for src, dst in ((_O_AQ, _P_AQ), (_O_IQ, _P_IQ), (_O_BQ, _P_BQ), (_O_CQ, _P_CQ), (_O_CK, _P_CK)):
        cp(src, 1024, dst)
    for i in range(3):
        for g in range(B_KV_GROUPS):
            cp(bkv(i, 0, g), HEAD_DIM, _P_BK + (i * B_KV_GROUPS + g) * HEAD_DIM)
    for g in range(B_KV_GROUPS):
        cp(bkv(0, 1, g), HEAD_DIM, _P_BVC + g * HEAD_DIM)
    cp(_O_ALAT, A_LATENT, _P_ALAT)
    ov_ref[_PV_CV:_PV_CV + 1024, :] = w_ref[_O_CV:_O_CV + 1024, :].astype(bf16)
    for i in (1, 2):
        for g in range(B_KV_GROUPS):
            dst = _PV_BV + ((i - 1) * B_KV_GROUPS + g) * HEAD_DIM
            ov_ref[dst:dst + HEAD_DIM, :] = w_ref[bkv(i, 1, g):bkv(i, 1, g) + HEAD_DIM, :].astype(bf16)
    lane = lax.broadcasted_iota(i32, (w_ref.shape[1], 128), 1)
    assert _O_IW == _O_IK + IDX_DIM and _O_IK % 8 == 0 and _O_BG % 8 == 0
    blk = w_ref[_O_IK:_O_IK + 128, :].T
    o_ref[:, _P_IKW:_P_IKW + 128] = jnp.where(lane < IDX_DIM + IDX_HEADS, blk, 0.0).astype(bf16)
    blk = w_ref[_O_BG:_O_BG + 128, :].T
    o_ref[:, _P_BG:_P_BG + 128] = jnp.where(lane < 3 * B_HEADS, blk, 0.0).astype(bf16)


def _pack_w_in(w_in):
    depth, k, n = w_in.shape
    w_t = jnp.swapaxes(w_in, 1, 2)
    tc = 128
    return pl.pallas_call(
        _pack_w_in_kernel,
        grid=(depth, k // tc),
        in_specs=[pl.BlockSpec((None, n, tc), lambda l, i: (l, 0, i))],
        out_specs=[pl.BlockSpec((None, tc, _P_TOT), lambda l, i: (l, i, 0)),
                   pl.BlockSpec((None, _PV_TOT, tc), lambda l, i: (l, 0, i))],
        out_shape=[jax.ShapeDtypeStruct((depth, k, _P_TOT), bf16),
                   jax.ShapeDtypeStruct((depth, _PV_TOT, k), bf16)],
        compiler_params=_cparams(("parallel", "parallel")),
    )(w_t)


def _cover_t(L):
    n_cmp_pad = L // CMP_STRIDE
    starts = np.arange(n_cmp_pad) * CMP_STRIDE
    slc_start = np.arange(L // SLC_LEN) * SLC_LEN
    cover = ((starts[:, None] < slc_start[None, :] + SLC_LEN)
             & (starts[:, None] + CMP_LEN > slc_start[None, :])).astype(np.float32)
    n_cmp = (L - CMP_LEN) // CMP_STRIDE + 1
    cover[n_cmp:] = 0.0
    return jnp.asarray(cover.T)


def _token_mixing(u, h, B, L, l, lw, tabs, cov_t, lam_init):
    M = B * L
    G = B_KV_GROUPS
    aq, iq, bq, cq, ck, bk, alat_n, ikw, gates = _prep(h, lw['a_lat_g'], tabs, L)

    aw = A_HEADS * HEAD_DIM
    ak = _mm_rope(alat_n, lw['a_up'], l, aw, tabs['plain'], L, 512, aw)
    av_tc = _proj_t(alat_n, lw['a_up_vt'], l, 0, A_HEADS, HEAD_DIM, 512, B, L)
    ik = ikw[:, :IDX_DIM].astype(bf16).reshape(B, L, IDX_DIM)
    iw_t = ikw[:, IDX_DIM:IDX_DIM + IDX_HEADS].reshape(B, L, IDX_HEADS).transpose(0, 2, 1)
    mask_t = _dsa_mask(ik, iq.reshape(B, L, -1), iw_t)
    ya = _dsa_attn(aq.reshape(B, L, -1), ak.reshape(B, L, -1), av_tc, mask_t)

    bvc = h[:, _P_BVC:_P_BVC + G * HEAD_DIM].reshape(B, L, G, HEAD_DIM)
    bkc = bk[:, :G * HEAD_DIM].reshape(B, L, G, HEAD_DIM)
    n_row = L // CMP_STRIDE
    xcmp = jnp.stack([bkc, bvc], axis=1)
    xcmp = xcmp.transpose(0, 1, 3, 2, 4).reshape(B, 2, G, n_row, CMP_STRIDE * HEAD_DIM)
    cmp_out = _nsa_compress(xcmp, lw['cmp_w1'], lw['cmp_w2'], lw['cmp_pe'], l)
    kc = cmp_out[:, 0]
    vc_t = cmp_out[:, 1].transpose(0, 1, 3, 2)
    bv_tc = _proj_t(u, lw['w_vt'], l, _PV_BV, 2 * G, HEAD_DIM, 512, B, L)
    gates_t = gates[:, :3 * B_HEADS].reshape(B, L, G, 3 * B_HPG).transpose(0, 2, 3, 1)
    yb = _nsa_attn(bq.reshape(B, L, -1), kc, vc_t, bk.reshape(B, L, -1), bv_tc, gates_t, cov_t)

    cv_tc = _proj_t(u, lw['w_vt'], l, _PV_CV, C_HEADS, 2 * C_DIM, _DIFF_TK, B, L)
    yc = _diff_attn(cq.reshape(B, L, -1), ck.reshape(B, L, -1), cv_tc, lw['lam'],
                    lw['c_subln_g'].reshape(2 * C_DIM, 1), lam_init)

    return _merge(ya.reshape(M, -1), yb.reshape(M, -1), yc.reshape(M, -1), lw['w_br'], l, h)


def kernel(x, c, w_ada, b_ada, w_in, a_lat_g, a_up, cmp_w1, cmp_w2, cmp_pe, lam, c_subln_g, w_br, w_o,
           w_ffn_in, w_ffn_out, ln_g, ln_b):
    B, L, D = x.shape
    depth = w_ada.shape[0]
    M = B * L
    alpha = (2 * depth) ** 0.25

    c_pad = jnp.zeros((8, D), f32).at[:B].set(c)
    mod = _ada(c_pad, w_ada, b_ada)[:, :B]
    mods = [[mod[l, :, i * D:(i + 1) * D].reshape(B, 1, D) for i in range(6)] for l in range(depth)]

    assert HEAD_DIM == C_DIM
    tabs = dict(plain=_rope_tables(L, HEAD_DIM), query=_rope_tables(L, HEAD_DIM, HEAD_DIM ** -0.5 * _LOG2E),
                idx=_rope_tables(L, IDX_DIM))
    cov_t = _cover_t(L)

    w_in_p, w_vt = _pack_w_in(w_in)
    pe_flat = jnp.zeros((depth, 2, 8, CMP_LEN * HEAD_DIM), f32).at[:, :, 0].set(
        cmp_pe.reshape(depth, 2, CMP_LEN * HEAD_DIM)).astype(bf16)
    a_up_vt = jnp.swapaxes(a_up[:, :, A_HEADS * HEAD_DIM:], 1, 2).astype(bf16)
    wb = dict(a_up=a_up.astype(bf16), a_up_vt=a_up_vt, w_vt=w_vt, cmp_w1=cmp_w1.astype(bf16),
              cmp_w2=cmp_w2.astype(bf16), cmp_pe=pe_flat, w_br=w_br)
    w_o_b = w_o.astype(bf16)
    w_fo_b = w_ffn_out.astype(bf16)

    x2 = x.reshape(M, D)
    u = _lnmod(x2, mods[0][1], mods[0][0], L)
    for l in range(depth):
        lam_init = 0.8 - 0.6 * math.exp(-0.3 * l)
        sh_a, sc_a, g_a, sh_f, sc_f, g_f = mods[l]
        lw = dict(wb, a_lat_g=a_lat_g[l], lam=lam[l], c_subln_g=c_subln_g[l])
        h = _mm(u, w_in_p, l, 512, 768, bf16)
        merged = _token_mixing(u, h, B, L, l, lw, tabs, cov_t, lam_init)
        x2, u = _mm_res_ln(merged, w_o_b, l, x2, g_a, ln_g[l, 0], ln_b[l, 0], sc_f, sh_f, L, alpha, True)
        f = _ffn_in(u, w_ffn_in, l)
        last = l == depth - 1
        nsc, nsh = (sc_f, sh_f) if last else (mods[l + 1][1], mods[l + 1][0])
        x2, u = _mm_res_ln(f, w_fo_b, l, x2, g_f, ln_g[l, 1], ln_b[l, 1], nsc, nsh, L, alpha, not last)
    return x2.reshape(B, L, D)
```

```python
import functools
import math

import numpy as np
import jax
import jax.numpy as jnp
from jax import lax
from jax.experimental import pallas as pl
from jax.experimental.pallas import tpu as pltpu

f32 = jnp.float32
bf16 = jnp.bfloat16
i32 = jnp.int32

D_MODEL = 2048
HEAD_DIM = 128
ROPE_THETA = 500000.0
NEG = -1e30
FORCE = 1e6
A_HEADS = 8
A_LATENT = 512
IDX_HEADS = 16
IDX_DIM = 64
DSA_TOPK = 256
B_HEADS = 8
B_KV_GROUPS = 2
B_HPG = B_HEADS // B_KV_GROUPS
CMP_LEN = 32
CMP_STRIDE = 16
SLC_LEN = 64
SLC_TOPN = 16
WIN_LEN = 512
C_HEADS = 4
C_DIM = 128
BRANCH_W = A_HEADS * HEAD_DIM
N_BRANCH = 3
D_FF = int(math.ceil(8 * D_MODEL / 3 / 256)) * 256

_O_AQ = 0
_O_ALAT = _O_AQ + A_HEADS * HEAD_DIM
_O_IQ = _O_ALAT + A_LATENT
_O_IK = _O_IQ + IDX_HEADS * IDX_DIM
_O_IW = _O_IK + IDX_DIM
_O_BQ = _O_IW + IDX_HEADS
_O_BKV = _O_BQ + B_HEADS * HEAD_DIM
_O_BG = _O_BKV + 3 * 2 * B_KV_GROUPS * HEAD_DIM
_O_CQ = _O_BG + 3 * B_HEADS
_O_CK = _O_CQ + C_HEADS * 2 * C_DIM
_O_CV = _O_CK + C_HEADS * 2 * C_DIM
_O_GL = _O_CV + C_HEADS * 2 * C_DIM
_N_IN = _O_GL + N_BRANCH * D_MODEL

_P_GL = 0
_P_BK = 6144
_P_BVC = 6912
_P_AQ = 7168
_P_IQ = 8192
_P_BQ = 9216
_P_CQ = 10240
_P_CK = 11264
_P_ALAT = 12288
_P_IKW = 12800
_P_BG = 12928
_P_TOT = 13056
_PV_CV = 0
_PV_BV = 1024
_PV_TOT = 1536

_VMEM_LIMIT = 48 * 1024 * 1024
_LOG2E = 1.4426950408889634
_INF = float("inf")
_ONES = 16
_RES_LN_TM = 512
_RES_LN_SUB = 256
_DIFF_TK = 1024
_DIFF_TQ = 512
_HEAD_GROUP = 2


def _cparams(sem, vmem=_VMEM_LIMIT):
    return pltpu.CompilerParams(dimension_semantics=sem, vmem_limit_bytes=vmem)


def _sigmoid(x):
    return 1.0 / (1.0 + jnp.exp(-x))


def _dot_nt(a, b):
    return lax.dot_general(a, b, (((1,), (1,)), ((), ())), preferred_element_type=f32)


def _ada_kernel(c_ref, w_ref, b_ref, o_ref):
    c = c_ref[...]
    cs = c * _sigmoid(c)
    o_ref[0] = jnp.dot(cs, w_ref[0], preferred_element_type=f32,
                       precision=lax.Precision.HIGHEST) + b_ref[0]


def _ada(c_pad, w_ada, b_ada):
    depth, d, n = w_ada.shape
    tn = 512
    return pl.pallas_call(
        _ada_kernel,
        grid=(depth, n // tn),
        in_specs=[pl.BlockSpec((8, d), lambda l, j: (0, 0)),
                  pl.BlockSpec((1, d, tn), lambda l, j: (l, 0, j)),
                  pl.BlockSpec((1, 1, tn), lambda l, j: (l, 0, j))],
        out_specs=pl.BlockSpec((1, 8, tn), lambda l, j: (l, 0, j)),
        out_shape=jax.ShapeDtypeStruct((depth, 8, n), f32),
        compiler_params=_cparams(("parallel", "parallel")),
    )(c_pad, w_ada, b_ada.reshape(depth, 1, n))


def _ln_rows(x, eps):
    mu = jnp.mean(x, axis=-1, keepdims=True)
    d = x - mu
    var = jnp.mean(d * d, axis=-1, keepdims=True)
    return d * lax.rsqrt(var + eps)


def _lnmod_kernel(x_ref, sc_ref, sh_ref, o_ref):
    y = _ln_rows(x_ref[...], 1e-5)
    o_ref[...] = (y * (1.0 + sc_ref[0]) + sh_ref[0]).astype(o_ref.dtype)


def _lnmod(x2, sc, sh, L):
    m, d = x2.shape
    tm = 512
    per_b = L // tm
    return pl.pallas_call(
        _lnmod_kernel,
        grid=(m // tm,),
        in_specs=[pl.BlockSpec((tm, d), lambda i: (i, 0)),
                  pl.BlockSpec((1, 1, d), lambda i: (i // per_b, 0, 0)),
                  pl.BlockSpec((1, 1, d), lambda i: (i // per_b, 0, 0))],
        out_specs=pl.BlockSpec((tm, d), lambda i: (i, 0)),
        out_shape=jax.ShapeDtypeStruct((m, d), bf16),
        compiler_params=_cparams(("parallel",)),
    )(x2, sc, sh)


def _mm_kernel(a_ref, w_ref, o_ref):
    o_ref[...] = jnp.dot(a_ref[...], w_ref[...], preferred_element_type=f32).astype(o_ref.dtype)


def _mm(a, w, l, tm, tn, out_dtype):
    m, k = a.shape
    n = w.shape[2]
    return pl.pallas_call(
        _mm_kernel,
        grid=(n // tn, m // tm),
        in_specs=[pl.BlockSpec((tm, k), lambda j, i: (i, 0)),
                  pl.BlockSpec((None, k, tn), lambda j, i: (l, 0, j))],
        out_specs=pl.BlockSpec((tm, tn), lambda j, i: (i, j)),
        out_shape=jax.ShapeDtypeStruct((m, n), out_dtype),
        compiler_params=_cparams(("parallel", "parallel")),
    )(a, w)


def _proj_t_kernel(w_ref, a_ref, o_ref):
    n, d = o_ref.shape[1], o_ref.shape[3] - _ONES
    tk = o_ref.shape[4]
    res = _dot_nt(w_ref[...], a_ref[...])
    for i in range(n):
        o_ref[0, i, 0, :d, :] = res[i * d:(i + 1) * d, :].astype(o_ref.dtype)
        o_ref[0, i, 0, d:, :] = jnp.ones((_ONES, tk), o_ref.dtype)


def _proj_t(a, w_t, l, row0, n, d, tk, B, L):
    k = a.shape[1]
    rows = n * d
    nc = L // tk
    return pl.pallas_call(
        _proj_t_kernel,
        grid=(B, nc),
        in_specs=[pl.BlockSpec((None, rows, k), lambda b, c: (l, row0 // rows, 0)),
                  pl.BlockSpec((tk, k), lambda b, c: (b * nc + c, 0))],
        out_specs=pl.BlockSpec((1, n, 1, d + _ONES, tk), lambda b, c: (b, 0, c, 0, 0)),
        out_shape=jax.ShapeDtypeStruct((B, n, nc, d + _ONES, tk), bf16),
        compiler_params=_cparams(("parallel", "parallel")),
    )(w_t, a)


def _rope_low(shape, d, half):
    lane = lax.broadcasted_iota(i32, shape, 1)
    return (lane & (d - 1)) < half


def _rope_heads(x, c, s, low, half):
    partner = jnp.where(low, pltpu.roll(x, 128 - half, 1), pltpu.roll(x, half, 1))
    return x * c + partner * s


def _mm_rope_kernel(a_ref, w_ref, c_ref, s_ref, o_ref):
    acc = jnp.dot(a_ref[...], w_ref[...], preferred_element_type=f32)
    c, s = c_ref[...], s_ref[...]
    low = _rope_low(c.shape, HEAD_DIM, HEAD_DIM // 8)
    for h in range(acc.shape[1] // 128):
        sl = slice(h * 128, (h + 1) * 128)
        o_ref[:, sl] = _rope_heads(acc[:, sl], c, s, low, HEAD_DIM // 8).astype(o_ref.dtype)


def _mm_rope(a, w, l, n, tabs, L, tm, tn):
    m, k = a.shape
    per_b = L // tm
    tspec = pl.BlockSpec((tm, 128), lambda j, i: (i % per_b, 0))
    return pl.pallas_call(
        _mm_rope_kernel,
        grid=(n // tn, m // tm),
        in_specs=[pl.BlockSpec((tm, k), lambda j, i: (i, 0)),
                  pl.BlockSpec((None, k, tn), lambda j, i: (l, 0, j)),
                  tspec, tspec],
        out_specs=pl.BlockSpec((tm, tn), lambda j, i: (i, j)),
        out_shape=jax.ShapeDtypeStruct((m, n), bf16),
        compiler_params=_cparams(("parallel", "parallel")),
    )(a, w, *tabs)


def _prep_kernel(aq_ref, iq_ref, bq_ref, cq_ref, ck_ref, bk_ref, alat_ref, ikw_ref, bg_ref, alg_ref,
                 c_ref, s_ref, cq_ref_t, sq_ref_t, c6_ref, s6_ref,
                 aq_o, iq_o, bq_o, cq_o, ck_o, bk_o, alat_o, ikw_o, g_o):
    rows = 64
    plain = (c_ref, s_ref, _rope_low((rows, 128), HEAD_DIM, HEAD_DIM // 8), HEAD_DIM // 8)
    scaled = (cq_ref_t, sq_ref_t) + plain[2:]
    idx = (c6_ref, s6_ref, _rope_low((rows, 128), IDX_DIM, IDX_DIM // 8), IDX_DIM // 8)

    def rope(x, tabs, rs):
        return _rope_heads(x, tabs[0][rs, :], tabs[1][rs, :], tabs[2], tabs[3])

    def rope_all(src, dst, tabs):
        for r in range(src.shape[0] // rows):
            rs = slice(r * rows, (r + 1) * rows)
            for h in range(src.shape[1] // 128):
                sl = slice(h * 128, (h + 1) * 128)
                dst[rs, sl] = rope(src[rs, sl].astype(f32), tabs, rs).astype(dst.dtype)

    rope_all(aq_ref, aq_o, scaled)
    rope_all(bq_ref, bq_o, scaled)
    rope_all(cq_ref, cq_o, scaled)
    rope_all(ck_ref, ck_o, plain)
    rope_all(bk_ref, bk_o, plain)
    rope_all(iq_ref, iq_o, idx)

    a = alat_ref[...].astype(f32)
    ms = jnp.mean(a * a, axis=-1, keepdims=True)
    alat_o[...] = (a * lax.rsqrt(ms + 1e-6) * alg_ref[...]).astype(alat_o.dtype)

    isk = lax.broadcasted_iota(i32, (rows, 128), 1) < IDX_DIM
    for r in range(ikw_ref.shape[0] // rows):
        rs = slice(r * rows, (r + 1) * rows)
        x = ikw_ref[rs, :].astype(f32)
        mu = jnp.sum(jnp.where(isk, x, 0.0), axis=-1, keepdims=True) * (1.0 / IDX_DIM)
        d = jnp.where(isk, x - mu, 0.0)
        var = jnp.sum(d * d, axis=-1, keepdims=True) * (1.0 / IDX_DIM)
        y = d * lax.rsqrt(var + 1e-5)
        ikw_o[rs, :] = jnp.where(isk, rope(y, idx, rs), x * (IDX_DIM ** -0.5 * IDX_HEADS ** -0.5))

    g_o[...] = _sigmoid(bg_ref[...].astype(f32))


def _prep(h, a_lat_g, tabs, L):
    m = h.shape[0]
    tm = 512
    per_b = L // tm

    def hs(width, off):
        return pl.BlockSpec((tm, width), lambda i, _o=off // width: (i, _o))

    tspec = pl.BlockSpec((tm, 128), lambda i: (i % per_b, 0))

    def os(width):
        return pl.BlockSpec((tm, width), lambda i: (i, 0))

    outs = [(1024, bf16)] * 5 + [(768, bf16), (512, bf16), (128, f32), (128, f32)]
    return pl.pallas_call(
        _prep_kernel,
        grid=(m // tm,),
        in_specs=[hs(1024, _P_AQ), hs(1024, _P_IQ), hs(1024, _P_BQ), hs(1024, _P_CQ), hs(1024, _P_CK),
                  hs(768, _P_BK), hs(512, _P_ALAT), hs(128, _P_IKW), hs(128, _P_BG),
                  pl.BlockSpec((1, A_LATENT), lambda i: (0, 0))] + [tspec] * 6,
        out_specs=[os(w) for w, _ in outs],
        out_shape=[jax.ShapeDtypeStruct((m, w), dt) for w, dt in outs],
        compiler_params=_cparams(("parallel",)),
    )(h, h, h, h, h, h, h, h, h, a_lat_g.reshape(1, A_LATENT), *tabs['plain'], *tabs['query'], *tabs['idx'])


def _f32_order_key(x):
    b = int(np.float32(x).view(np.int32))
    return b ^ ((b >> 31) & 0x7FFFFFFF)


_KEY_NEG = _f32_order_key(NEG)


def _dsa_mask_kernel(ik_ref, iq_ref, iw_ref, o_ref, key_ref, qp_ref, j_ref, *, L, tq, ksel):
    j = pl.program_id(1)
    ck = 512
    nck = L // ck
    nc = ((j + 1) * tq + ck - 1) // ck
    n_out_i = L - nc * ck
    n_out = n_out_i.astype(f32)
    t_row = j * tq + lax.broadcasted_iota(i32, (1, tq), 1)

    for hp in range(IDX_HEADS // 2):
        for e in range(2):
            h = 2 * hp + e
            qp_ref[hp, e * tq:(e + 1) * tq, :] = iq_ref[0, :, h * IDX_DIM:(h + 1) * IDX_DIM]

    def score_chunk(c, carry):
        off = pl.multiple_of(c * ck, ck)
        ikc = ik_ref[0, pl.ds(off, ck), :]
        acc = jnp.zeros((ck, tq), f32)
        for hp in range(IDX_HEADS // 2):
            s2 = _dot_nt(ikc, qp_ref[hp])
            acc = acc + jnp.maximum(s2[:, :tq], 0.0) * iw_ref[0, 2 * hp:2 * hp + 1, :]
            acc = acc + jnp.maximum(s2[:, tq:], 0.0) * iw_ref[0, 2 * hp + 1:2 * hp + 2, :]
        acc = jnp.where(acc == 0.0, 0.0, acc)
        sidx = off + lax.broadcasted_iota(i32, (ck, tq), 0)
        key_ref[pl.ds(off, ck), :] = jnp.where(sidx <= t_row, acc, NEG)
        return carry

    lax.fori_loop(0, nc, score_chunk, 0)

    def as_f32(key):
        return lax.bitcast_convert_type(key ^ ((key >> 31) & 0x7FFFFFFF), f32)

    def count(pred_fn):
        def body(c, cnt):
            off = pl.multiple_of(c * ck, ck)
            k = key_ref[pl.ds(off, ck), :]
            sidx = off + lax.broadcasted_iota(i32, (ck, tq), 0)
            return cnt + jnp.sum(pred_fn(k, sidx).reshape(ck // 64, 64, tq), axis=0)
        part = lax.fori_loop(0, nc, body, jnp.zeros((64, tq), f32))
        return jnp.sum(part, axis=0, keepdims=True)

    kf = float(ksel)

    def bit_body(i, carry):
        thr, cnt_thr = carry
        cand = thr + lax.shift_left(jnp.int32(1), 31 - i)
        cand_f = as_f32(cand)
        cnt = count(lambda k, s: jnp.where(k >= cand_f, 1.0, 0.0)) + jnp.where(_KEY_NEG >= cand, n_out, 0.0)
        ok = cnt >= kf
        return jnp.where(ok, cand, thr), jnp.where(ok, cnt, cnt_thr)

    thr_key, cnt_ge = lax.fori_loop(0, 32, bit_body, (jnp.full((1, tq), -2 ** 31, i32),
                                                       jnp.full((1, tq), float(L), f32)))
    thr = as_f32(thr_key)
    cnt_gt = count(lambda k, s: jnp.where(k > thr, 1.0, 0.0)) + jnp.where(_KEY_NEG > thr_key, n_out, 0.0)
    need = kf - cnt_gt

    j_ref[...] = jnp.full((1, tq), L, i32)

    @pl.when(jnp.max(cnt_ge) > kf)
    def _():
        nbits = L.bit_length() - 1

        def jbit(i, cur):
            cand = cur | lax.shift_left(jnp.int32(1), nbits - 1 - i)
            f = count(lambda k, s: jnp.where(k == thr, jnp.where(s < cand, 1.0, 0.0), 0.0))
            f = f + jnp.where(thr_key == _KEY_NEG, jnp.clip(cand - nc * ck, 0, n_out_i).astype(f32), 0.0)
            return jnp.where(f < need, cand, cur)

        j_ref[...] = lax.fori_loop(0, nbits, jbit, jnp.zeros((1, tq), i32))

    jlast = j_ref[...]

    def write(c, carry):
        off = pl.multiple_of(c * ck, ck)
        k = key_ref[pl.ds(off, ck), :]
        sidx = off + lax.broadcasted_iota(i32, (ck, tq), 0)
        sel = jnp.where(k > thr, _INF, jnp.where(k == thr, jnp.where(sidx <= jlast, _INF, NEG), NEG))
        o_ref[0, pl.ds(off, ck), :] = jnp.where(sidx <= t_row, sel, NEG)
        return carry

    lax.fori_loop(0, nc, write, 0)

    def write_rest(c, carry):
        off = pl.multiple_of(c * ck, ck)
        o_ref[0, pl.ds(off, ck), :] = jnp.full((ck, tq), NEG, f32)
        return carry

    lax.fori_loop(nc, nck, write_rest, 0)


def _dsa_mask(ik, iq, iw_t):
    B, L, _ = iq.shape
    tq = 128
    ksel = min(DSA_TOPK, L // 4)
    return pl.pallas_call(
        functools.partial(_dsa_mask_kernel, L=L, tq=tq, ksel=ksel),
        grid=(B, L // tq),
        in_specs=[pl.BlockSpec((1, L, IDX_DIM), lambda b, j: (b, 0, 0)),
                  pl.BlockSpec((1, tq, IDX_HEADS * IDX_DIM), lambda b, j: (b, j, 0)),
                  pl.BlockSpec((1, IDX_HEADS, tq), lambda b, j: (b, 0, j))],
        out_specs=pl.BlockSpec((1, L, tq), lambda b, j: (b, 0, j)),
        out_shape=jax.ShapeDtypeStruct((B, L, L), f32),
        scratch_shapes=[pltpu.VMEM((L, tq), f32), pltpu.VMEM((IDX_HEADS // 2, 2 * tq, IDX_DIM), bf16),
                        pltpu.VMEM((1, tq), i32)],
        compiler_params=_cparams(("parallel", "parallel")),
    )(ik, iq, iw_t)


def _col_reduce(x, op):
    r, c = x.shape
    if r > 64:
        x = op(x.reshape(r // 64, 64, c), axis=0)
    return op(x, axis=0, keepdims=True)


def _flash_group(ss, cap, v_ts, m_refs, acc_refs):
    ss, mloc = _flash_mask_max(ss, cap)
    _flash_update(ss, mloc, cap is not None, v_ts, m_refs, acc_refs)


def _flash_mask_max(ss, cap):
    if cap is not None:
        ss = [jnp.minimum(s, cap) for s in ss]
    return ss, [_col_reduce(s, jnp.max) for s in ss]


def _flash_update(ss, mloc, masked, v_ts, m_refs, acc_refs):
    n = len(ss)
    m_prev = [r[...] for r in m_refs]
    m_new = [jnp.maximum(m_prev[i], mloc[i]) for i in range(n)]
    alpha = [jnp.exp2(m_prev[i] - m_new[i]) for i in range(n)]
    ps = [jnp.exp2((ss[i] - m_new[i]).astype(bf16)) for i in range(n)]
    pv = [jnp.dot(v_ts[i], ps[i], preferred_element_type=f32) for i in range(n)]
    for i in range(n):
        acc_new = acc_refs[i][...] * alpha[i] + pv[i]
        if masked:
            acc_new = jnp.where(m_new[i] <= NEG, 0.0, acc_new)
        acc_refs[i][...] = acc_new
        m_refs[i][...] = m_new[i]


def _flash_heads(score_fns, cap, v_ts, m_ref, acc_ref, group):
    n = len(score_fns)
    groups = [list(range(g, min(g + group, n))) for g in range(0, n, group)]
    nxt = _flash_mask_max([score_fns[h]() for h in groups[0]], cap)
    for gi, hs in enumerate(groups):
        ss, mloc = nxt
        if gi + 1 < len(groups):
            nxt = _flash_mask_max([score_fns[h]() for h in groups[gi + 1]], cap)
        _flash_update(ss, mloc, cap is not None, [v_ts[h] for h in hs], [m_ref.at[h] for h in hs],
                      [acc_ref.at[h] for h in hs])


def _flash_init(m_ref, acc_ref):
    m_ref[...] = jnp.full(m_ref.shape, NEG, f32)
    acc_ref[...] = jnp.zeros(acc_ref.shape, f32)


def _flash_result(acc, d):
    return acc[:d] / jnp.maximum(acc[d:d + 1], 1e-30)


def _dsa_attn_kernel(q_ref, k_ref, vt_ref, mask_ref, o_ref, m_ref, acc_ref, *, tq, tk, nkc):
    j = pl.program_id(1)
    c = pl.program_id(2)

    @pl.when(c == 0)
    def _():
        _flash_init(m_ref, acc_ref)

    @pl.when(c * tk < (j + 1) * tq)
    def _():
        cap = mask_ref[0]

        def scores(h):
            sl = slice(h * HEAD_DIM, (h + 1) * HEAD_DIM)
            return lambda: _dot_nt(k_ref[0, :, sl], q_ref[0, :, sl])

        v_ts = [vt_ref[0, h, 0] for h in range(A_HEADS)]
        _flash_heads([scores(h) for h in range(A_HEADS)], cap, v_ts, m_ref, acc_ref, _HEAD_GROUP)

    @pl.when(c == nkc - 1)
    def _():
        for h in range(A_HEADS):
            o = _flash_result(acc_ref[h], HEAD_DIM)
            o_ref[0, :, h * HEAD_DIM:(h + 1) * HEAD_DIM] = o.T.astype(o_ref.dtype)


def _dsa_attn(q, k, v_t, mask_t):
    B, L, W = q.shape
    tq, tk = 512, 512
    nkc = L // tk

    def last(j):
        return ((j + 1) * tq - 1) // tk

    return pl.pallas_call(
        functools.partial(_dsa_attn_kernel, tq=tq, tk=tk, nkc=nkc),
        grid=(B, L // tq, nkc),
        in_specs=[pl.BlockSpec((1, tq, W), lambda b, j, c: (b, j, 0)),
                  pl.BlockSpec((1, tk, W), lambda b, j, c: (b, jnp.minimum(c, last(j)), 0)),
                  pl.BlockSpec((1, A_HEADS, 1, HEAD_DIM + _ONES, tk),
                               lambda b, j, c: (b, 0, jnp.minimum(c, last(j)), 0, 0)),
                  pl.BlockSpec((1, tk, tq), lambda b, j, c: (b, jnp.minimum(c, last(j)), j))],
        out_specs=pl.BlockSpec((1, tq, W), lambda b, j, c: (b, j, 0)),
        out_shape=jax.ShapeDtypeStruct((B, L, W), bf16),
        scratch_shapes=[pltpu.VMEM((A_HEADS, 1, tq), f32),
                        pltpu.VMEM((A_HEADS, HEAD_DIM + _ONES, tq), f32)],
        compiler_params=_cparams(("parallel", "parallel", "arbitrary")),
    )(q, k, v_t, mask_t)


def _nsa_cmp_kernel(x_ref, w1_ref, w2_ref, pe_ref, o_ref):
    x = x_ref[0, 0, 0]
    w1 = w1_ref[0]
    half = CMP_STRIDE * HEAD_DIM
    a = jnp.dot(x, w1[:half], preferred_element_type=f32)
    b = jnp.dot(x, w1[half:], preferred_element_type=f32)
    pe = jnp.dot(pe_ref[0], w1, preferred_element_type=f32)[0:1]
    n = a.shape[0]
    pre = a + pltpu.roll(b, n - 1, 0) + pe
    act = pre * _sigmoid(pre)
    o_ref[0, 0, 0] = jnp.dot(act.astype(bf16), w2_ref[0], preferred_element_type=f32).astype(o_ref.dtype)


def _nsa_compress(x, w1, w2, pe, l):
    B, _, G, n, wd = x.shape
    return pl.pallas_call(
        _nsa_cmp_kernel,
        grid=(B, 2, G),
        in_specs=[pl.BlockSpec((1, 1, 1, n, wd), lambda b, t, g: (b, t, g, 0, 0)),
                  pl.BlockSpec((None, 1, CMP_LEN * HEAD_DIM, HEAD_DIM), lambda b, t, g: (l, t, 0, 0)),
                  pl.BlockSpec((None, 1, HEAD_DIM, HEAD_DIM), lambda b, t, g: (l, t, 0, 0)),
                  pl.BlockSpec((None, 1, 8, CMP_LEN * HEAD_DIM), lambda b, t, g: (l, t, 0, 0))],
        out_specs=pl.BlockSpec((1, 1, 1, n, HEAD_DIM), lambda b, t, g: (b, t, g, 0, 0)),
        out_shape=jax.ShapeDtypeStruct((B, 2, G, n, HEAD_DIM), bf16),
        compiler_params=_cparams(("parallel", "parallel", "parallel")),
    )(x, w1, w2, pe)


def _nsa_attn_kernel(q_ref, kc_ref, vct_ref, ks_ref, vst_ref, kw_ref, vwt_ref, g_ref, cov_ref, o_ref,
                     m_ref, acc_ref, out_ref, sel_ref, *, L, tq, tk, n_sel):
    j = pl.program_id(2)
    t_row = j * tq + lax.broadcasted_iota(i32, (1, tq), 1)
    n_cmp = kc_ref.shape[2]
    n_slc = L // SLC_LEN
    qs = [q_ref[0, :, hh * HEAD_DIM:(hh + 1) * HEAD_DIM] for hh in range(B_HPG)]

    def gate(hh, i):
        return g_ref[0, 0, hh * 3 + i:hh * 3 + i + 1, :]

    kc = kc_ref[0, 0]
    vct = vct_ref[0, 0]
    cend = lax.broadcasted_iota(i32, (n_cmp, tq), 0) * CMP_STRIDE + (CMP_LEN - 1)
    mc = cend <= t_row
    ss = [jnp.where(mc, _dot_nt(kc, qs[hh]), NEG) for hh in range(B_HPG)]
    mxs = [_col_reduce(s, jnp.max) for s in ss]
    ps = [jnp.where(mc, jnp.exp2(ss[hh] - mxs[hh]), 0.0) for hh in range(B_HPG)]
    inv = [1.0 / jnp.maximum(_col_reduce(p, jnp.sum), 1e-30) for p in ps]
    pvs = [jnp.dot(vct, p.astype(bf16), preferred_element_type=f32) for p in ps]
    psum = ps[0] * inv[0]
    for hh in range(1, B_HPG):
        psum = psum + ps[hh] * inv[hh]
    for hh in range(B_HPG):
        out_ref[hh] = (gate(hh, 0) * inv[hh]) * pvs[hh]
    imp = jnp.dot(cov_ref[...], psum, preferred_element_type=f32, precision=lax.Precision.HIGHEST)
    blk = lax.broadcasted_iota(i32, (n_slc, tq), 0)
    cur = t_row >> (SLC_LEN.bit_length() - 1)
    forced = (blk == 0) | (blk == cur) | (blk == cur - 1)
    imp = jnp.where(forced, FORCE, jnp.where(blk <= cur, imp, NEG))
    rank = jnp.zeros((n_slc, tq), f32)
    for r in range(n_slc):
        row = imp[r:r + 1, :]
        rank = rank + jnp.where(row > imp, 1.0, jnp.where(row == imp, jnp.where(blk > r, 1.0, 0.0), 0.0))
    sel_ref[...] = jnp.where(rank < float(n_sel), _INF, NEG)

    def finish(i):
        for hh in range(B_HPG):
            out_ref[hh] = out_ref[hh] + gate(hh, i) * _flash_result(acc_ref[hh], HEAD_DIM)

    def run_branch(k_ref, vt_ref, c_lo, c_hi, mask_fn):
        _flash_init(m_ref, acc_ref)

        def body(c, carry):
            off = pl.multiple_of(c * tk, tk)
            kch = k_ref[0, pl.ds(off, tk), :]
            vch = vt_ref[0, 0, c]
            sidx = off + lax.broadcasted_iota(i32, (tk, tq), 0)
            cap = mask_fn(c, sidx)
            fns = [functools.partial(_dot_nt, kch, qs[hh]) for hh in range(B_HPG)]
            _flash_heads(fns, cap, [vch] * B_HPG, m_ref, acc_ref, _HEAD_GROUP)
            return carry

        lax.fori_loop(c_lo, c_hi, body, 0)

    bpc = tk // SLC_LEN

    def slc_mask(c, sidx):
        rows = [jnp.broadcast_to(sel_ref[pl.ds(c * bpc + b, 1), :], (SLC_LEN, tq)) for b in range(bpc)]
        return jnp.where(sidx <= t_row, jnp.concatenate(rows, axis=0), NEG)

    c_hi = ((j + 1) * tq + tk - 1) // tk
    run_branch(ks_ref, vst_ref, 0, c_hi, slc_mask)
    finish(1)

    def win_mask(c, sidx):
        d = t_row - sidx
        return jnp.where(d >= 0, jnp.where(d < WIN_LEN, _INF, NEG), NEG)

    c_lo = jnp.maximum(j * tq - (WIN_LEN - 1), 0) // tk
    run_branch(kw_ref, vwt_ref, c_lo, c_hi, win_mask)
    finish(2)

    for hh in range(B_HPG):
        o_ref[0, :, hh * HEAD_DIM:(hh + 1) * HEAD_DIM] = out_ref[hh].T.astype(o_ref.dtype)


def _nsa_attn(bq, kc, vc_t, bk, bv_tc, gates_t, cov_t):
    B, L, _ = bq.shape
    G = B_KV_GROUPS
    tq, tk = 512, 512
    n_cmp = kc.shape[2]
    n_slc = L // SLC_LEN
    n_sel = min(SLC_TOPN, n_slc)
    gw = B_HPG * HEAD_DIM
    nc = L // tk
    return pl.pallas_call(
        functools.partial(_nsa_attn_kernel, L=L, tq=tq, tk=tk, n_sel=n_sel),
        grid=(B, G, L // tq),
        in_specs=[pl.BlockSpec((1, tq, gw), lambda b, g, j: (b, j, g)),
                  pl.BlockSpec((1, 1, n_cmp, HEAD_DIM), lambda b, g, j: (b, g, 0, 0)),
                  pl.BlockSpec((1, 1, HEAD_DIM, n_cmp), lambda b, g, j: (b, g, 0, 0)),
                  pl.BlockSpec((1, L, HEAD_DIM), lambda b, g, j: (b, 0, 2 + g)),
                  pl.BlockSpec((1, 1, nc, HEAD_DIM + _ONES, tk), lambda b, g, j: (b, g, 0, 0, 0)),
                  pl.BlockSpec((1, L, HEAD_DIM), lambda b, g, j: (b, 0, 4 + g)),
                  pl.BlockSpec((1, 1, nc, HEAD_DIM + _ONES, tk),
                               lambda b, g, j: (b, B_KV_GROUPS + g, 0, 0, 0)),
                  pl.BlockSpec((1, 1, 3 * B_HPG, tq), lambda b, g, j: (b, g, 0, j)),
                  pl.BlockSpec((n_slc, n_cmp), lambda b, g, j: (0, 0))],
        out_specs=pl.BlockSpec((1, tq, gw), lambda b, g, j: (b, j, g)),
        out_shape=jax.ShapeDtypeStruct((B, L, B_HEADS * HEAD_DIM), bf16),
        scratch_shapes=[pltpu.VMEM((B_HPG, 1, tq), f32),
                        pltpu.VMEM((B_HPG, HEAD_DIM + _ONES, tq), f32),
                        pltpu.VMEM((B_HPG, HEAD_DIM, tq), f32), pltpu.VMEM((n_slc, tq), f32)],
        compiler_params=_cparams(("parallel", "parallel", "parallel")),
    )(bq, kc, vc_t, bk, bv_tc, bk, bv_tc, gates_t, cov_t)


def _diff_attn_kernel(q_ref, k_ref, vt_ref, lam_ref, g_ref, o_ref, m_ref, acc_ref,
                      *, tq, tk, lam_init):
    j = pl.program_id(2)
    t_row = j * tq + lax.broadcasted_iota(i32, (1, tq), 1)
    qs = [q_ref[0, :, mi * C_DIM:(mi + 1) * C_DIM] for mi in range(2)]
    _flash_init(m_ref, acc_ref)

    def chunk(c, lo, width, masked):
        off = pl.multiple_of(c * tk + lo, tq)
        vch = vt_ref[0, 0, c, :, lo:lo + width]
        cap = None
        if masked:
            sidx = off + lax.broadcasted_iota(i32, (width, tq), 0)
            cap = jnp.where(sidx <= t_row, _INF, NEG)
        ss = [_dot_nt(k_ref[0, pl.ds(off, width), mi * C_DIM:(mi + 1) * C_DIM], qs[mi]) for mi in range(2)]
        _flash_group(ss, cap, [vch] * 2, [m_ref.at[mi] for mi in range(2)],
                     [acc_ref.at[mi] for mi in range(2)])

    def body(c, carry):
        chunk(c, 0, tk, False)
        return carry

    c_diag = (j * tq) // tk
    lax.fori_loop(0, c_diag, body, 0)

    @pl.when(j % 2 == 0)
    def _():
        chunk(c_diag, 0, tq, True)

    @pl.when(j % 2 == 1)
    def _():
        chunk(c_diag, 0, tq, False)
        chunk(c_diag, tq, tq, True)

    lam = lam_ref[...]
    lam_val = (jnp.exp(jnp.sum(lam[0:1] * lam[1:2], axis=1, keepdims=True))
               - jnp.exp(jnp.sum(lam[2:3] * lam[3:4], axis=1, keepdims=True)) + lam_init)
    o = _flash_result(acc_ref[0], 2 * C_DIM) - lam_val * _flash_result(acc_ref[1], 2 * C_DIM)
    ms = jnp.mean(o * o, axis=0, keepdims=True)
    y = o * lax.rsqrt(ms + 1e-6) * g_ref[...] * (1.0 - lam_init)
    o_ref[0] = y.T.astype(o_ref.dtype)


def _diff_attn(cq, ck, cv_tc, lam, g_col, lam_init):
    B, L, _ = cq.shape
    tq, tk = _DIFF_TQ, cv_tc.shape[-1]
    assert tk == 2 * tq
    hw = 2 * C_DIM
    nc = L // tk
    return pl.pallas_call(
        functools.partial(_diff_attn_kernel, tq=tq, tk=tk, lam_init=lam_init),
        grid=(B, C_HEADS, L // tq),
        in_specs=[pl.BlockSpec((1, tq, hw), lambda b, h, j: (b, j, h)),
                  pl.BlockSpec((1, L, hw), lambda b, h, j: (b, 0, h)),
                  pl.BlockSpec((1, 1, nc, hw + _ONES, tk), lambda b, h, j: (b, h, 0, 0, 0)),
                  pl.BlockSpec((4, C_DIM), lambda b, h, j: (0, 0)),
                  pl.BlockSpec((hw, 1), lambda b, h, j: (0, 0))],
        out_specs=pl.BlockSpec((1, tq, hw), lambda b, h, j: (b, j, h)),
        out_shape=jax.ShapeDtypeStruct((B, L, C_HEADS * hw), bf16),
        scratch_shapes=[pltpu.VMEM((2, 1, tq), f32), pltpu.VMEM((2, hw + _ONES, tq), f32)],
        compiler_params=_cparams(("parallel", "parallel", "parallel")),
    )(cq, ck, cv_tc, lam, g_col)


def _merge_kernel(ya_ref, yb_ref, yc_ref, w_ref, g0_ref, g1_ref, g2_ref, o_ref, wb_ref):
    @pl.when(pl.program_id(1) == 0)
    def _():
        wb_ref[...] = w_ref[...].astype(bf16)

    acc = None
    for r, (y_ref, g_ref) in enumerate(((ya_ref, g0_ref), (yb_ref, g1_ref), (yc_ref, g2_ref))):
        br = jnp.dot(y_ref[...], wb_ref[r], preferred_element_type=f32)
        t = _sigmoid(g_ref[...].astype(f32)) * br
        acc = t if acc is None else acc + t
    o_ref[...] = acc.astype(o_ref.dtype)


def _merge(ya, yb, yc, w_br, l, h):
    m, kw = ya.shape
    tm, tn = 1024, 512
    npb = D_MODEL // tn
    yspec = pl.BlockSpec((tm, kw), lambda j, i: (i, 0))

    def gspec(r):
        return pl.BlockSpec((tm, tn), lambda j, i, _r=r: (i, _r * npb + j))

    return pl.pallas_call(
        _merge_kernel,
        grid=(npb, m // tm),
        in_specs=[yspec, yspec, yspec,
                  pl.BlockSpec((None, N_BRANCH, kw, tn), lambda j, i: (l, 0, 0, j)),
                  gspec(0), gspec(1), gspec(2)],
        out_specs=pl.BlockSpec((tm, tn), lambda j, i: (i, j)),
        out_shape=jax.ShapeDtypeStruct((m, D_MODEL), bf16),
        scratch_shapes=[pltpu.VMEM((N_BRANCH, kw, tn), bf16)],
        compiler_params=_cparams(("parallel", "arbitrary")),
    )(ya, yb, yc, w_br, h, h, h)


def _mm_res_ln_kernel(a_ref, w_ref, x_ref, gate_ref, lg_ref, lb_ref, sc_ref, sh_ref, xo_ref, *u_refs,
                      alpha, sub):
    tm = a_ref.shape[0]
    for r in range(tm // sub):
        rows = slice(r * sub, (r + 1) * sub)
        y = jnp.dot(a_ref[rows, :], w_ref[...], preferred_element_type=f32)
        z = alpha * x_ref[rows, :] + gate_ref[0] * y
        xn = _ln_rows(z, 1e-5) * lg_ref[...] + lb_ref[...]
        xo_ref[rows, :] = xn
        if u_refs:
            u_refs[0][rows, :] = (_ln_rows(xn, 1e-5) * (1.0 + sc_ref[0]) + sh_ref[0]).astype(bf16)


def _mm_res_ln(a, w, l, x2, gate, ln_g, ln_b, sc, sh, L, alpha, emit_u):
    m, kdim = a.shape
    d = w.shape[2]
    tm = _RES_LN_TM if kdim * d * 2 <= 16 * 1024 * 1024 else _RES_LN_SUB
    per_b = L // tm
    bspec = pl.BlockSpec((1, 1, d), lambda i: (i // per_b, 0, 0))
    vspec = pl.BlockSpec((1, d), lambda i: (0, 0))
    rspec = pl.BlockSpec((tm, d), lambda i: (i, 0))
    out_shape = [jax.ShapeDtypeStruct((m, d), f32)]
    out_specs = [rspec]
    if emit_u:
        out_shape.append(jax.ShapeDtypeStruct((m, d), bf16))
        out_specs.append(rspec)
    res = pl.pallas_call(
        functools.partial(_mm_res_ln_kernel, alpha=alpha, sub=_RES_LN_SUB),
        grid=(m // tm,),
        in_specs=[pl.BlockSpec((tm, kdim), lambda i: (i, 0)),
                  pl.BlockSpec((None, kdim, d), lambda i: (l, 0, 0), pipeline_mode=pl.Buffered(1)),
                  rspec, bspec, vspec, vspec, bspec, bspec],
        out_specs=out_specs,
        out_shape=out_shape,
        compiler_params=_cparams(("parallel",), 60 * 1024 * 1024),
    )(a, w, x2, gate, ln_g.reshape(1, d), ln_b.reshape(1, d), sc, sh)
    return res if emit_u else (res[0], None)


def _ffn_in_kernel(a_ref, wg_ref, wu_ref, o_ref, wgb_ref, wub_ref):
    @pl.when(pl.program_id(1) == 0)
    def _():
        wgb_ref[...] = wg_ref[...].astype(bf16)
        wub_ref[...] = wu_ref[...].astype(bf16)

    a = a_ref[...]
    g = jnp.dot(a, wgb_ref[...], preferred_element_type=f32)
    u = jnp.dot(a, wub_ref[...], preferred_element_type=f32)
    o_ref[...] = (g * _sigmoid(g) * u).astype(o_ref.dtype)


def _ffn_in(a, w, l):
    m, k = a.shape
    tm, tn = 1024, 512
    nb = D_FF // tn
    return pl.pallas_call(
        _ffn_in_kernel,
        grid=(nb, m // tm),
        in_specs=[pl.BlockSpec((tm, k), lambda j, i: (i, 0)),
                  pl.BlockSpec((None, k, tn), lambda j, i: (l, 0, j)),
                  pl.BlockSpec((None, k, tn), lambda j, i: (l, 0, nb + j))],
        out_specs=pl.BlockSpec((tm, tn), lambda j, i: (i, j)),
        out_shape=jax.ShapeDtypeStruct((m, D_FF), bf16),
        scratch_shapes=[pltpu.VMEM((k, tn), bf16), pltpu.VMEM((k, tn), bf16)],
        compiler_params=_cparams(("parallel", "arbitrary")),
    )(a, w, w)


def _rope_tables(L, d, mult=1.0):
    r = d // 4
    half = r // 2
    inv = ROPE_THETA ** (-(jnp.arange(half, dtype=f32) * 2.0) / r)
    ang = jnp.arange(L).astype(f32)[:, None] * inv[None, :]
    cos, sin = jnp.cos(ang), jnp.sin(ang)
    c = jnp.concatenate([cos, cos, jnp.ones((L, d - r), f32)], axis=1)
    s = jnp.concatenate([-sin, sin, jnp.zeros((L, d - r), f32)], axis=1)
    rep = 128 // d
    return tuple(jnp.tile(t * mult, (1, rep)) for t in (c, s))


def _pack_w_in_kernel(w_ref, o_ref, ov_ref):
    def cp(src, width, dst):
        for o in range(0, width, 1024):
            wd = min(1024, width - o)
            o_ref[:, dst + o:dst + o + wd] = w_ref[src + o:src + o + wd, :].T.astype(bf16)

    def bkv(i, kv, g):
        return _O_BKV + ((i * 2 + kv) * B_KV_GROUPS + g) * HEAD_DIM

    cp(_O_GL, N_BRANCH * D_MODEL, _P_GL)
    for src, dst in ((_O_AQ, _P_AQ), (_O_IQ, _P_IQ), (_O_BQ, _P_BQ), (_O_CQ, _P_CQ), (_O_CK, _P_CK)):
        cp(src, 1024, dst)
    for i in range(3):
        for g in range(B_KV_GROUPS):
            cp(bkv(i, 0, g), HEAD_DIM, _P_BK + (i * B_KV_GROUPS + g) * HEAD_DIM)
    for g in range(B_KV_GROUPS):
        cp(bkv(0, 1, g), HEAD_DIM, _P_BVC + g * HEAD_DIM)
    cp(_O_ALAT, A_LATENT, _P_ALAT)
    ov_ref[_PV_CV:_PV_CV + 1024, :] = w_ref[_O_CV:_O_CV + 1024, :].astype(bf16)
    for i in (1, 2):
        for g in range(B_KV_GROUPS):
            dst = _PV_BV + ((i - 1) * B_KV_GROUPS + g) * HEAD_DIM
            ov_ref[dst:dst + HEAD_DIM, :] = w_ref[bkv(i, 1, g):bkv(i, 1, g) + HEAD_DIM, :].astype(bf16)
    lane = lax.broadcasted_iota(i32, (w_ref.shape[1], 128), 1)
    assert _O_IW == _O_IK + IDX_DIM and _O_IK % 8 == 0 and _O_BG % 8 == 0
    blk = w_ref[_O_IK:_O_IK + 128, :].T
    o_ref[:, _P_IKW:_P_IKW + 128] = jnp.where(lane < IDX_DIM + IDX_HEADS, blk, 0.0).astype(bf16)
    blk = w_ref[_O_BG:_O_BG + 128, :].T
    o_ref[:, _P_BG:_P_BG + 128] = jnp.where(lane < 3 * B_HEADS, blk, 0.0).astype(bf16)


def _pack_w_in(w_in):
    depth, k, n = w_in.shape
    w_t = jnp.swapaxes(w_in, 1, 2)
    tc = 128
    return pl.pallas_call(
        _pack_w_in_kernel,
        grid=(depth, k // tc),
        in_specs=[pl.BlockSpec((None, n, tc), lambda l, i: (l, 0, i))],
        out_specs=[pl.BlockSpec((None, tc, _P_TOT), lambda l, i: (l, i, 0)),
                   pl.BlockSpec((None, _PV_TOT, tc), lambda l, i: (l, 0, i))],
        out_shape=[jax.ShapeDtypeStruct((depth, k, _P_TOT), bf16),
                   jax.ShapeDtypeStruct((depth, _PV_TOT, k), bf16)],
        compiler_params=_cparams(("parallel", "parallel")),
    )(w_t)


def _cover_t(L):
    n_cmp_pad = L // CMP_STRIDE
    starts = np.arange(n_cmp_pad) * CMP_STRIDE
    slc_start = np.arange(L // SLC_LEN) * SLC_LEN
    cover = ((starts[:, None] < slc_start[None, :] + SLC_LEN)
             & (starts[:, None] + CMP_LEN > slc_start[None, :])).astype(np.float32)
    n_cmp = (L - CMP_LEN) // CMP_STRIDE + 1
    cover[n_cmp:] = 0.0
    return jnp.asarray(cover.T)


def _token_mixing(u, h, B, L, l, lw, tabs, cov_t, lam_init):
    M = B * L
    G = B_KV_GROUPS
    aq, iq, bq, cq, ck, bk, alat_n, ikw, gates = _prep(h, lw['a_lat_g'], tabs, L)

    aw = A_HEADS * HEAD_DIM
    ak = _mm_rope(alat_n, lw['a_up'], l, aw, tabs['plain'], L, 512, aw)
    av_tc = _proj_t(alat_n, lw['a_up_vt'], l, 0, A_HEADS, HEAD_DIM, 512, B, L)
    ik = ikw[:, :IDX_DIM].astype(bf16).reshape(B, L, IDX_DIM)
    iw_t = ikw[:, IDX_DIM:IDX_DIM + IDX_HEADS].reshape(B, L, IDX_HEADS).transpose(0, 2, 1)
    mask_t = _dsa_mask(ik, iq.reshape(B, L, -1), iw_t)
    ya = _dsa_attn(aq.reshape(B, L, -1), ak.reshape(B, L, -1), av_tc, mask_t)

    bvc = h[:, _P_BVC:_P_BVC + G * HEAD_DIM].reshape(B, L, G, HEAD_DIM)
    bkc = bk[:, :G * HEAD_DIM].reshape(B, L, G, HEAD_DIM)
    n_row = L // CMP_STRIDE
    xcmp = jnp.stack([bkc, bvc], axis=1)
    xcmp = xcmp.transpose(0, 1, 3, 2, 4).reshape(B, 2, G, n_row, CMP_STRIDE * HEAD_DIM)
    cmp_out = _nsa_compress(xcmp, lw['cmp_w1'], lw['cmp_w2'], lw['cmp_pe'], l)
    kc = cmp_out[:, 0]
    vc_t = cmp_out[:, 1].transpose(0, 1, 3, 2)
    bv_tc = _proj_t(u, lw['w_vt'], l, _PV_BV, 2 * G, HEAD_DIM, 512, B, L)
    gates_t = gates[:, :3 * B_HEADS].reshape(B, L, G, 3 * B_HPG).transpose(0, 2, 3, 1)
    yb = _nsa_attn(bq.reshape(B, L, -1), kc, vc_t, bk.reshape(B, L, -1), bv_tc, gates_t, cov_t)

    cv_tc = _proj_t(u, lw['w_vt'], l, _PV_CV, C_HEADS, 2 * C_DIM, _DIFF_TK, B, L)
    yc = _diff_attn(cq.reshape(B, L, -1), ck.reshape(B, L, -1), cv_tc, lw['lam'],
                    lw['c_subln_g'].reshape(2 * C_DIM, 1), lam_init)

    return _merge(ya.reshape(M, -1), yb.reshape(M, -1), yc.reshape(M, -1), lw['w_br'], l, h)


def kernel(x, c, w_ada, b_ada, w_in, a_lat_g, a_up, cmp_w1, cmp_w2, cmp_pe, lam, c_subln_g, w_br, w_o,
           w_ffn_in, w_ffn_out, ln_g, ln_b):
    B, L, D = x.shape
    depth = w_ada.shape[0]
    M = B * L
    alpha = (2 * depth) ** 0.25

    c_pad = jnp.zeros((8, D), f32).at[:B].set(c)
    mod = _ada(c_pad, w_ada, b_ada)[:, :B]
    mods = [[mod[l, :, i * D:(i + 1) * D].reshape(B, 1, D) for i in range(6)] for l in range(depth)]

    assert HEAD_DIM == C_DIM
    tabs = dict(plain=_rope_tables(L, HEAD_DIM), query=_rope_tables(L, HEAD_DIM, HEAD_DIM ** -0.5 * _LOG2E),
                idx=_rope_tables(L, IDX_DIM))
    cov_t = _cover_t(L)

    w_in_p, w_vt = _pack_w_in(w_in)
    pe_flat = jnp.zeros((depth, 2, 8, CMP_LEN * HEAD_DIM), f32).at[:, :, 0].set(
        cmp_pe.reshape(depth, 2, CMP_LEN * HEAD_DIM)).astype(bf16)
    a_up_vt = jnp.swapaxes(a_up[:, :, A_HEADS * HEAD_DIM:], 1, 2).astype(bf16)
    wb = dict(a_up=a_up.astype(bf16), a_up_vt=a_up_vt, w_vt=w_vt, cmp_w1=cmp_w1.astype(bf16),
              cmp_w2=cmp_w2.astype(bf16), cmp_pe=pe_flat, w_br=w_br)
    w_o_b = w_o.astype(bf16)
    w_fo_b = w_ffn_out.astype(bf16)

    x2 = x.reshape(M, D)
    u = _lnmod(x2, mods[0][1], mods[0][0], L)
    for l in range(depth):
        lam_init = 0.8 - 0.6 * math.exp(-0.3 * l)
        sh_a, sc_a, g_a, sh_f, sc_f, g_f = mods[l]
        lw = dict(wb, a_lat_g=a_lat_g[l], lam=lam[l], c_subln_g=c_subln_g[l])
        h = _mm(u, w_in_p, l, 1024, 768, bf16)
        merged = _token_mixing(u, h, B, L, l, lw, tabs, cov_t, lam_init)
        x2, u = _mm_res_ln(merged, w_o_b, l, x2, g_a, ln_g[l, 0], ln_b[l, 0], sc_f, sh_f, L, alpha, True)
        f = _ffn_in(u, w_ffn_in, l)
        last = l == depth - 1
        nsc, nsh = (sc_f, sh_f) if last else (mods[l + 1][1], mods[l + 1][0])
        x2, u = _mm_res_ln(f, w_fo_b, l, x2, g_f, ln_g[l, 1], ln_b[l, 1], nsc, nsh, L, alpha, not last)
    return x2.reshape(B, L, D)
```

```python
import functools
import math

import numpy as np
import jax
import jax.numpy as jnp
from jax import lax
from jax.experimental import pallas as pl
from jax.experimental.pallas import tpu as pltpu

f32 = jnp.float32
bf16 = jnp.bfloat16
i32 = jnp.int32

D_MODEL = 2048
HEAD_DIM = 128
ROPE_THETA = 500000.0
NEG = -1e30
FORCE = 1e6
A_HEADS = 8
A_LATENT = 512
IDX_HEADS = 16
IDX_DIM = 64
DSA_TOPK = 256
B_HEADS = 8
B_KV_GROUPS = 2
B_HPG = B_HEADS // B_KV_GROUPS
CMP_LEN = 32
CMP_STRIDE = 16
SLC_LEN = 64
SLC_TOPN = 16
WIN_LEN = 512
C_HEADS = 4
C_DIM = 128
BRANCH_W = A_HEADS * HEAD_DIM
N_BRANCH = 3
D_FF = int(math.ceil(8 * D_MODEL / 3 / 256)) * 256

_O_AQ = 0
_O_ALAT = _O_AQ + A_HEADS * HEAD_DIM
_O_IQ = _O_ALAT + A_LATENT
_O_IK = _O_IQ + IDX_HEADS * IDX_DIM
_O_IW = _O_IK + IDX_DIM
_O_BQ = _O_IW + IDX_HEADS
_O_BKV = _O_BQ + B_HEADS * HEAD_DIM
_O_BG = _O_BKV + 3 * 2 * B_KV_GROUPS * HEAD_DIM
_O_CQ = _O_BG + 3 * B_HEADS
_O_CK = _O_CQ + C_HEADS * 2 * C_DIM
_O_CV = _O_CK + C_HEADS * 2 * C_DIM
_O_GL = _O_CV + C_HEADS * 2 * C_DIM
_N_IN = _O_GL + N_BRANCH * D_MODEL

_P_GL = 0
_P_BK = 6144
_P_BVC = 6912
_P_AQ = 7168
_P_IQ = 8192
_P_BQ = 9216
_P_CQ = 10240
_P_CK = 11264
_P_ALAT = 12288
_P_IKW = 12800
_P_BG = 12928
_P_TOT = 13056
_PV_CV = 0
_PV_BV = 1024
_PV_TOT = 1536

_VMEM_LIMIT = 48 * 1024 * 1024
_LOG2E = 1.4426950408889634
_INF = float("inf")
_ONES = 16
_RES_LN_TM = 512
_RES_LN_SUB = 256
_DIFF_TK = 1024
_DIFF_TQ = 512
_HEAD_GROUP = 2


def _cparams(sem, vmem=_VMEM_LIMIT):
    return pltpu.CompilerParams(dimension_semantics=sem, vmem_limit_bytes=vmem)


def _sigmoid(x):
    return 1.0 / (1.0 + jnp.exp(-x))


def _dot_nt(a, b):
    return lax.dot_general(a, b, (((1,), (1,)), ((), ())), preferred_element_type=f32)


def _ada_kernel(c_ref, w_ref, b_ref, o_ref):
    c = c_ref[...]
    cs = c * _sigmoid(c)
    o_ref[0] = jnp.dot(cs, w_ref[0], preferred_element_type=f32,
                       precision=lax.Precision.HIGHEST) + b_ref[0]


def _ada(c_pad, w_ada, b_ada):
    depth, d, n = w_ada.shape
    tn = 512
    return pl.pallas_call(
        _ada_kernel,
        grid=(depth, n // tn),
        in_specs=[pl.BlockSpec((8, d), lambda l, j: (0, 0)),
                  pl.BlockSpec((1, d, tn), lambda l, j: (l, 0, j)),
                  pl.BlockSpec((1, 1, tn), lambda l, j: (l, 0, j))],
        out_specs=pl.BlockSpec((1, 8, tn), lambda l, j: (l, 0, j)),
        out_shape=jax.ShapeDtypeStruct((depth, 8, n), f32),
        compiler_params=_cparams(("parallel", "parallel")),
    )(c_pad, w_ada, b_ada.reshape(depth, 1, n))


def _ln_rows(x, eps):
    mu = jnp.mean(x, axis=-1, keepdims=True)
    d = x - mu
    var = jnp.mean(d * d, axis=-1, keepdims=True)
    return d * lax.rsqrt(var + eps)


def _lnmod_kernel(x_ref, sc_ref, sh_ref, o_ref):
    y = _ln_rows(x_ref[...], 1e-5)
    o_ref[...] = (y * (1.0 + sc_ref[0]) + sh_ref[0]).astype(o_ref.dtype)


def _lnmod(x2, sc, sh, L):
    m, d = x2.shape
    tm = 512
    per_b = L // tm
    return pl.pallas_call(
        _lnmod_kernel,
        grid=(m // tm,),
        in_specs=[pl.BlockSpec((tm, d), lambda i: (i, 0)),
                  pl.BlockSpec((1, 1, d), lambda i: (i // per_b, 0, 0)),
                  pl.BlockSpec((1, 1, d), lambda i: (i // per_b, 0, 0))],
        out_specs=pl.BlockSpec((tm, d), lambda i: (i, 0)),
        out_shape=jax.ShapeDtypeStruct((m, d), bf16),
        compiler_params=_cparams(("parallel",)),
    )(x2, sc, sh)


def _mm_kernel(a_ref, w_ref, o_ref):
    o_ref[...] = jnp.dot(a_ref[...], w_ref[...], preferred_element_type=f32).astype(o_ref.dtype)


def _mm(a, w, l, tm, tn, out_dtype):
    m, k = a.shape
    n = w.shape[2]
    return pl.pallas_call(
        _mm_kernel,
        grid=(n // tn, m // tm),
        in_specs=[pl.BlockSpec((tm, k), lambda j, i: (i, 0)),
                  pl.BlockSpec((None, k, tn), lambda j, i: (l, 0, j))],
        out_specs=pl.BlockSpec((tm, tn), lambda j, i: (i, j)),
        out_shape=jax.ShapeDtypeStruct((m, n), out_dtype),
        compiler_params=_cparams(("parallel", "parallel")),
    )(a, w)


def _proj_t_kernel(w_ref, a_ref, o_ref):
    n, d = o_ref.shape[1], o_ref.shape[3] - _ONES
    tk = o_ref.shape[4]
    res = _dot_nt(w_ref[...], a_ref[...])
    for i in range(n):
        o_ref[0, i, 0, :d, :] = res[i * d:(i + 1) * d, :].astype(o_ref.dtype)
        o_ref[0, i, 0, d:, :] = jnp.ones((_ONES, tk), o_ref.dtype)


def _proj_t(a, w_t, l, row0, n, d, tk, B, L):
    k = a.shape[1]
    rows = n * d
    nc = L // tk
    return pl.pallas_call(
        _proj_t_kernel,
        grid=(B, nc),
        in_specs=[pl.BlockSpec((None, rows, k), lambda b, c: (l, row0 // rows, 0)),
                  pl.BlockSpec((tk, k), lambda b, c: (b * nc + c, 0))],
        out_specs=pl.BlockSpec((1, n, 1, d + _ONES, tk), lambda b, c: (b, 0, c, 0, 0)),
        out_shape=jax.ShapeDtypeStruct((B, n, nc, d + _ONES, tk), bf16),
        compiler_params=_cparams(("parallel", "parallel")),
    )(w_t, a)


def _rope_low(shape, d, half):
    lane = lax.broadcasted_iota(i32, shape, 1)
    return (lane & (d - 1)) < half


def _rope_heads(x, c, s, low, half):
    partner = jnp.where(low, pltpu.roll(x, 128 - half, 1), pltpu.roll(x, half, 1))
    return x * c + partner * s


def _mm_rope_kernel(a_ref, w_ref, c_ref, s_ref, o_ref):
    acc = jnp.dot(a_ref[...], w_ref[...], preferred_element_type=f32)
    c, s = c_ref[...], s_ref[...]
    low = _rope_low(c.shape, HEAD_DIM, HEAD_DIM // 8)
    for h in range(acc.shape[1] // 128):
        sl = slice(h * 128, (h + 1) * 128)
        o_ref[:, sl] = _rope_heads(acc[:, sl], c, s, low, HEAD_DIM // 8).astype(o_ref.dtype)


def _mm_rope(a, w, l, n, tabs, L, tm, tn):
    m, k = a.shape
    per_b = L // tm
    tspec = pl.BlockSpec((tm, 128), lambda j, i: (i % per_b, 0))
    return pl.pallas_call(
        _mm_rope_kernel,
        grid=(n // tn, m // tm),
        in_specs=[pl.BlockSpec((tm, k), lambda j, i: (i, 0)),
                  pl.BlockSpec((None, k, tn), lambda j, i: (l, 0, j)),
                  tspec, tspec],
        out_specs=pl.BlockSpec((tm, tn), lambda j, i: (i, j)),
        out_shape=jax.ShapeDtypeStruct((m, n), bf16),
        compiler_params=_cparams(("parallel", "parallel")),
    )(a, w, *tabs)


def _prep_kernel(aq_ref, iq_ref, bq_ref, cq_ref, ck_ref, bk_ref, alat_ref, ikw_ref, bg_ref, alg_ref,
                 c_ref, s_ref, cq_ref_t, sq_ref_t, c6_ref, s6_ref,
                 aq_o, iq_o, bq_o, cq_o, ck_o, bk_o, alat_o, ikw_o, g_o):
    rows = 64
    plain = (c_ref, s_ref, _rope_low((rows, 128), HEAD_DIM, HEAD_DIM // 8), HEAD_DIM // 8)
    scaled = (cq_ref_t, sq_ref_t) + plain[2:]
    idx = (c6_ref, s6_ref, _rope_low((rows, 128), IDX_DIM, IDX_DIM // 8), IDX_DIM // 8)

    def rope(x, tabs, rs):
        return _rope_heads(x, tabs[0][rs, :], tabs[1][rs, :], tabs[2], tabs[3])

    def rope_all(src, dst, tabs):
        for r in range(src.shape[0] // rows):
            rs = slice(r * rows, (r + 1) * rows)
            for h in range(src.shape[1] // 128):
                sl = slice(h * 128, (h + 1) * 128)
                dst[rs, sl] = rope(src[rs, sl].astype(f32), tabs, rs).astype(dst.dtype)

    rope_all(aq_ref, aq_o, scaled)
    rope_all(bq_ref, bq_o, scaled)
    rope_all(cq_ref, cq_o, scaled)
    rope_all(ck_ref, ck_o, plain)
    rope_all(bk_ref, bk_o, plain)
    rope_all(iq_ref, iq_o, idx)

    a = alat_ref[...].astype(f32)
    ms = jnp.mean(a * a, axis=-1, keepdims=True)
    alat_o[...] = (a * lax.rsqrt(ms + 1e-6) * alg_ref[...]).astype(alat_o.dtype)

    isk = lax.broadcasted_iota(i32, (rows, 128), 1) < IDX_DIM
    for r in range(ikw_ref.shape[0] // rows):
        rs = slice(r * rows, (r + 1) * rows)
        x = ikw_ref[rs, :].astype(f32)
        mu = jnp.sum(jnp.where(isk, x, 0.0), axis=-1, keepdims=True) * (1.0 / IDX_DIM)
        d = jnp.where(isk, x - mu, 0.0)
        var = jnp.sum(d * d, axis=-1, keepdims=True) * (1.0 / IDX_DIM)
        y = d * lax.rsqrt(var + 1e-5)
        ikw_o[rs, :] = jnp.where(isk, rope(y, idx, rs), x * (IDX_DIM ** -0.5 * IDX_HEADS ** -0.5))

    g_o[...] = _sigmoid(bg_ref[...].astype(f32))


def _prep(h, a_lat_g, tabs, L):
    m = h.shape[0]
    tm = 512
    per_b = L // tm

    def hs(width, off):
        return pl.BlockSpec((tm, width), lambda i, _o=off // width: (i, _o))

    tspec = pl.BlockSpec((tm, 128), lambda i: (i % per_b, 0))

    def os(width):
        return pl.BlockSpec((tm, width), lambda i: (i, 0))

    outs = [(1024, bf16)] * 5 + [(768, bf16), (512, bf16), (128, f32), (128, f32)]
    return pl.pallas_call(
        _prep_kernel,
        grid=(m // tm,),
        in_specs=[hs(1024, _P_AQ), hs(1024, _P_IQ), hs(1024, _P_BQ), hs(1024, _P_CQ), hs(1024, _P_CK),
                  hs(768, _P_BK), hs(512, _P_ALAT), hs(128, _P_IKW), hs(128, _P_BG),
                  pl.BlockSpec((1, A_LATENT), lambda i: (0, 0))] + [tspec] * 6,
        out_specs=[os(w) for w, _ in outs],
        out_shape=[jax.ShapeDtypeStruct((m, w), dt) for w, dt in outs],
        compiler_params=_cparams(("parallel",)),
    )(h, h, h, h, h, h, h, h, h, a_lat_g.reshape(1, A_LATENT), *tabs['plain'], *tabs['query'], *tabs['idx'])


def _f32_order_key(x):
    b = int(np.float32(x).view(np.int32))
    return b ^ ((b >> 31) & 0x7FFFFFFF)


_KEY_NEG = _f32_order_key(NEG)


def _dsa_mask_kernel(ik_ref, iq_ref, iw_ref, o_ref, key_ref, qp_ref, j_ref, *, L, tq, ksel):
    j = pl.program_id(1)
    ck = 512
    nck = L // ck
    nc = ((j + 1) * tq + ck - 1) // ck
    n_out_i = L - nc * ck
    n_out = n_out_i.astype(f32)
    t_row = j * tq + lax.broadcasted_iota(i32, (1, tq), 1)

    for hp in range(IDX_HEADS // 2):
        for e in range(2):
            h = 2 * hp + e
            qp_ref[hp, e * tq:(e + 1) * tq, :] = iq_ref[0, :, h * IDX_DIM:(h + 1) * IDX_DIM]

    def score_chunk(c, carry):
        off = pl.multiple_of(c * ck, ck)
        ikc = ik_ref[0, pl.ds(off, ck), :]
        acc = jnp.zeros((ck, tq), f32)
        for hp in range(IDX_HEADS // 2):
            s2 = _dot_nt(ikc, qp_ref[hp])
            acc = acc + jnp.maximum(s2[:, :tq], 0.0) * iw_ref[0, 2 * hp:2 * hp + 1, :]
            acc = acc + jnp.maximum(s2[:, tq:], 0.0) * iw_ref[0, 2 * hp + 1:2 * hp + 2, :]
        acc = jnp.where(acc == 0.0, 0.0, acc)
        sidx = off + lax.broadcasted_iota(i32, (ck, tq), 0)
        key_ref[pl.ds(off, ck), :] = jnp.where(sidx <= t_row, acc, NEG)
        return carry

    lax.fori_loop(0, nc, score_chunk, 0)

    def as_f32(key):
        return lax.bitcast_convert_type(key ^ ((key >> 31) & 0x7FFFFFFF), f32)

    def count(pred_fn):
        def body(c, cnt):
            off = pl.multiple_of(c * ck, ck)
            k = key_ref[pl.ds(off, ck), :]
            sidx = off + lax.broadcasted_iota(i32, (ck, tq), 0)
            return cnt + jnp.sum(pred_fn(k, sidx).reshape(ck // 64, 64, tq), axis=0)
        part = lax.fori_loop(0, nc, body, jnp.zeros((64, tq), f32))
        return jnp.sum(part, axis=0, keepdims=True)

    kf = float(ksel)

    def bit_body(i, carry):
        thr, cnt_thr = carry
        cand = thr + lax.shift_left(jnp.int32(1), 31 - i)
        cand_f = as_f32(cand)
        cnt = count(lambda k, s: jnp.where(k >= cand_f, 1.0, 0.0)) + jnp.where(_KEY_NEG >= cand, n_out, 0.0)
        ok = cnt >= kf
        return jnp.where(ok, cand, thr), jnp.where(ok, cnt, cnt_thr)

    thr_key, cnt_ge = lax.fori_loop(0, 32, bit_body, (jnp.full((1, tq), -2 ** 31, i32),
                                                       jnp.full((1, tq), float(L), f32)))
    thr = as_f32(thr_key)

    j_ref[...] = jnp.full((1, tq), L, i32)

    @pl.when(jnp.max(cnt_ge) > kf)
    def _():
        nbits = L.bit_length() - 1
        cnt_gt = count(lambda k, s: jnp.where(k > thr, 1.0, 0.0)) + jnp.where(_KEY_NEG > thr_key, n_out, 0.0)
        need = kf - cnt_gt

        def jbit(i, cur):
            cand = cur | lax.shift_left(jnp.int32(1), nbits - 1 - i)
            f = count(lambda k, s: jnp.where(k == thr, jnp.where(s < cand, 1.0, 0.0), 0.0))
            f = f + jnp.where(thr_key == _KEY_NEG, jnp.clip(cand - nc * ck, 0, n_out_i).astype(f32), 0.0)
            return jnp.where(f < need, cand, cur)

        j_ref[...] = lax.fori_loop(0, nbits, jbit, jnp.zeros((1, tq), i32))

    jlast = j_ref[...]

    def write(c, carry):
        off = pl.multiple_of(c * ck, ck)
        k = key_ref[pl.ds(off, ck), :]
        sidx = off + lax.broadcasted_iota(i32, (ck, tq), 0)
        sel = jnp.where(k > thr, _INF, jnp.where(k == thr, jnp.where(sidx <= jlast, _INF, NEG), NEG))
        o_ref[0, pl.ds(off, ck), :] = jnp.where(sidx <= t_row, sel, NEG)
        return carry

    lax.fori_loop(0, nc, write, 0)

    def write_rest(c, carry):
        off = pl.multiple_of(c * ck, ck)
        o_ref[0, pl.ds(off, ck), :] = jnp.full((ck, tq), NEG, f32)
        return carry

    lax.fori_loop(nc, nck, write_rest, 0)


def _dsa_mask(ik, iq, iw_t):
    B, L, _ = iq.shape
    tq = 128
    ksel = min(DSA_TOPK, L // 4)
    return pl.pallas_call(
        functools.partial(_dsa_mask_kernel, L=L, tq=tq, ksel=ksel),
        grid=(B, L // tq),
        in_specs=[pl.BlockSpec((1, L, IDX_DIM), lambda b, j: (b, 0, 0)),
                  pl.BlockSpec((1, tq, IDX_HEADS * IDX_DIM), lambda b, j: (b, j, 0)),
                  pl.BlockSpec((1, IDX_HEADS, tq), lambda b, j: (b, 0, j))],
        out_specs=pl.BlockSpec((1, L, tq), lambda b, j: (b, 0, j)),
        out_shape=jax.ShapeDtypeStruct((B, L, L), f32),
        scratch_shapes=[pltpu.VMEM((L, tq), f32), pltpu.VMEM((IDX_HEADS // 2, 2 * tq, IDX_DIM), bf16),
                        pltpu.VMEM((1, tq), i32)],
        compiler_params=_cparams(("parallel", "parallel")),
    )(ik, iq, iw_t)


def _col_reduce(x, op):
    r, c = x.shape
    if r > 64:
        x = op(x.reshape(r // 64, 64, c), axis=0)
    return op(x, axis=0, keepdims=True)


def _flash_group(ss, cap, v_ts, m_refs, acc_refs):
    ss, mloc = _flash_mask_max(ss, cap)
    _flash_update(ss, mloc, cap is not None, v_ts, m_refs, acc_refs)


def _flash_mask_max(ss, cap):
    if cap is not None:
        ss = [jnp.minimum(s, cap) for s in ss]
    return ss, [_col_reduce(s, jnp.max) for s in ss]


def _flash_update(ss, mloc, masked, v_ts, m_refs, acc_refs):
    n = len(ss)
    m_prev = [r[...] for r in m_refs]
    m_new = [jnp.maximum(m_prev[i], mloc[i]) for i in range(n)]
    alpha = [jnp.exp2(m_prev[i] - m_new[i]) for i in range(n)]
    ps = [jnp.exp2((ss[i] - m_new[i]).astype(bf16)) for i in range(n)]
    pv = [jnp.dot(v_ts[i], ps[i], preferred_element_type=f32) for i in range(n)]
    for i in range(n):
        acc_new = acc_refs[i][...] * alpha[i] + pv[i]
        if masked:
            acc_new = jnp.where(m_new[i] <= NEG, 0.0, acc_new)
        acc_refs[i][...] = acc_new
        m_refs[i][...] = m_new[i]


def _flash_heads(score_fns, cap, v_ts, m_ref, acc_ref, group):
    n = len(score_fns)
    groups = [list(range(g, min(g + group, n))) for g in range(0, n, group)]
    nxt = _flash_mask_max([score_fns[h]() for h in groups[0]], cap)
    for gi, hs in enumerate(groups):
        ss, mloc = nxt
        if gi + 1 < len(groups):
            nxt = _flash_mask_max([score_fns[h]() for h in groups[gi + 1]], cap)
        _flash_update(ss, mloc, cap is not None, [v_ts[h] for h in hs], [m_ref.at[h] for h in hs],
                      [acc_ref.at[h] for h in hs])


def _flash_init(m_ref, acc_ref):
    m_ref[...] = jnp.full(m_ref.shape, NEG, f32)
    acc_ref[...] = jnp.zeros(acc_ref.shape, f32)


def _flash_result(acc, d):
    return acc[:d] / jnp.maximum(acc[d:d + 1], 1e-30)


def _dsa_attn_kernel(jm_ref, cm_ref, lm_ref, q_ref, k_ref, vt_ref, mask_ref, o_ref, m_ref, acc_ref):
    s = pl.program_id(1)

    @pl.when(cm_ref[s] == 0)
    def _():
        _flash_init(m_ref, acc_ref)

    cap = mask_ref[0]

    def scores(h):
        sl = slice(h * HEAD_DIM, (h + 1) * HEAD_DIM)
        return lambda: _dot_nt(k_ref[0, :, sl], q_ref[0, :, sl])

    v_ts = [vt_ref[0, h, 0] for h in range(A_HEADS)]
    _flash_heads([scores(h) for h in range(A_HEADS)], cap, v_ts, m_ref, acc_ref, _HEAD_GROUP)

    @pl.when(lm_ref[s] == 1)
    def _():
        for h in range(A_HEADS):
            o = _flash_result(acc_ref[h], HEAD_DIM)
            o_ref[0, :, h * HEAD_DIM:(h + 1) * HEAD_DIM] = o.T.astype(o_ref.dtype)


def _dsa_attn(q, k, v_t, mask_t):
    B, L, W = q.shape
    tq, tk = 512, 512
    pairs = [(j, c) for j in range(L // tq) for c in range(((j + 1) * tq - 1) // tk + 1)]
    jm = jnp.asarray([p[0] for p in pairs], i32)
    cm = jnp.asarray([p[1] for p in pairs], i32)
    lm = jnp.asarray([int(i + 1 == len(pairs) or pairs[i + 1][0] != p[0]) for i, p in enumerate(pairs)], i32)
    return pl.pallas_call(
        _dsa_attn_kernel,
        grid_spec=pltpu.PrefetchScalarGridSpec(
            num_scalar_prefetch=3,
            grid=(B, len(pairs)),
            in_specs=[pl.BlockSpec((1, tq, W), lambda b, s, jm, cm, lm: (b, jm[s], 0)),
                      pl.BlockSpec((1, tk, W), lambda b, s, jm, cm, lm: (b, cm[s], 0)),
                      pl.BlockSpec((1, A_HEADS, 1, HEAD_DIM + _ONES, tk),
                                   lambda b, s, jm, cm, lm: (b, 0, cm[s], 0, 0)),
                      pl.BlockSpec((1, tk, tq), lambda b, s, jm, cm, lm: (b, cm[s], jm[s]))],
            out_specs=pl.BlockSpec((1, tq, W), lambda b, s, jm, cm, lm: (b, jm[s], 0)),
            scratch_shapes=[pltpu.VMEM((A_HEADS, 1, tq), f32),
                            pltpu.VMEM((A_HEADS, HEAD_DIM + _ONES, tq), f32)]),
        out_shape=jax.ShapeDtypeStruct((B, L, W), bf16),
        compiler_params=_cparams(("parallel", "arbitrary")),
    )(jm, cm, lm, q, k, v_t, mask_t)


def _nsa_cmp_kernel(x_ref, w1_ref, w2_ref, pe_ref, o_ref):
    x = x_ref[0, 0, 0]
    w1 = w1_ref[0]
    half = CMP_STRIDE * HEAD_DIM
    a = jnp.dot(x, w1[:half], preferred_element_type=f32)
    b = jnp.dot(x, w1[half:], preferred_element_type=f32)
    pe = jnp.dot(pe_ref[0], w1, preferred_element_type=f32)[0:1]
    n = a.shape[0]
    pre = a + pltpu.roll(b, n - 1, 0) + pe
    act = pre * _sigmoid(pre)
    o_ref[0, 0, 0] = jnp.dot(act.astype(bf16), w2_ref[0], preferred_element_type=f32).astype(o_ref.dtype)


def _nsa_compress(x, w1, w2, pe, l):
    B, _, G, n, wd = x.shape
    return pl.pallas_call(
        _nsa_cmp_kernel,
        grid=(B, 2, G),
        in_specs=[pl.BlockSpec((1, 1, 1, n, wd), lambda b, t, g: (b, t, g, 0, 0)),
                  pl.BlockSpec((None, 1, CMP_LEN * HEAD_DIM, HEAD_DIM), lambda b, t, g: (l, t, 0, 0)),
                  pl.BlockSpec((None, 1, HEAD_DIM, HEAD_DIM), lambda b, t, g: (l, t, 0, 0)),
                  pl.BlockSpec((None, 1, 8, CMP_LEN * HEAD_DIM), lambda b, t, g: (l, t, 0, 0))],
        out_specs=pl.BlockSpec((1, 1, 1, n, HEAD_DIM), lambda b, t, g: (b, t, g, 0, 0)),
        out_shape=jax.ShapeDtypeStruct((B, 2, G, n, HEAD_DIM), bf16),
        compiler_params=_cparams(("parallel", "parallel", "parallel")),
    )(x, w1, w2, pe)


def _nsa_attn_kernel(q_ref, kc_ref, vct_ref, ks_ref, vst_ref, kw_ref, vwt_ref, g_ref, cov_ref, o_ref,
                     m_ref, acc_ref, out_ref, sel_ref, *, L, tq, tk, n_sel):
    j = pl.program_id(2)
    t_row = j * tq + lax.broadcasted_iota(i32, (1, tq), 1)
    n_cmp = kc_ref.shape[2]
    n_slc = L // SLC_LEN
    qs = [q_ref[0, :, hh * HEAD_DIM:(hh + 1) * HEAD_DIM] for hh in range(B_HPG)]

    def gate(hh, i):
        return g_ref[0, 0, hh * 3 + i:hh * 3 + i + 1, :]

    kc = kc_ref[0, 0]
    vct = vct_ref[0, 0]
    cend = lax.broadcasted_iota(i32, (n_cmp, tq), 0) * CMP_STRIDE + (CMP_LEN - 1)
    mc = cend <= t_row
    ss = [jnp.where(mc, _dot_nt(kc, qs[hh]), NEG) for hh in range(B_HPG)]
    mxs = [_col_reduce(s, jnp.max) for s in ss]
    ps = [jnp.where(mc, jnp.exp2(ss[hh] - mxs[hh]), 0.0) for hh in range(B_HPG)]
    inv = [1.0 / jnp.maximum(_col_reduce(p, jnp.sum), 1e-30) for p in ps]
    pvs = [jnp.dot(vct, p.astype(bf16), preferred_element_type=f32) for p in ps]
    psum = ps[0] * inv[0]
    for hh in range(1, B_HPG):
        psum = psum + ps[hh] * inv[hh]
    for hh in range(B_HPG):
        out_ref[hh] = (gate(hh, 0) * inv[hh]) * pvs[hh]
    imp = jnp.dot(cov_ref[...], psum, preferred_element_type=f32, precision=lax.Precision.HIGHEST)
    blk = lax.broadcasted_iota(i32, (n_slc, tq), 0)
    cur = t_row >> (SLC_LEN.bit_length() - 1)
    forced = (blk == 0) | (blk == cur) | (blk == cur - 1)
    imp = jnp.where(forced, FORCE, jnp.where(blk <= cur, imp, NEG))
    rank = jnp.zeros((n_slc, tq), f32)
    for r in range(n_slc):
        row = imp[r:r + 1, :]
        rank = rank + jnp.where(row > imp, 1.0, jnp.where(row == imp, jnp.where(blk > r, 1.0, 0.0), 0.0))
    sel_ref[...] = jnp.where(rank < float(n_sel), _INF, NEG)

    def finish(i):
        for hh in range(B_HPG):
            out_ref[hh] = out_ref[hh] + gate(hh, i) * _flash_result(acc_ref[hh], HEAD_DIM)

    def run_branch(k_ref, vt_ref, c_lo, c_hi, mask_fn):
        _flash_init(m_ref, acc_ref)

        def body(c, carry):
            off = pl.multiple_of(c * tk, tk)
            kch = k_ref[0, pl.ds(off, tk), :]
            vch = vt_ref[0, 0, c]
            sidx = off + lax.broadcasted_iota(i32, (tk, tq), 0)
            cap = mask_fn(c, sidx)
            fns = [functools.partial(_dot_nt, kch, qs[hh]) for hh in range(B_HPG)]
            _flash_heads(fns, cap, [vch] * B_HPG, m_ref, acc_ref, _HEAD_GROUP)
            return carry

        lax.fori_loop(c_lo, c_hi, body, 0)

    bpc = tk // SLC_LEN

    def slc_mask(c, sidx):
        rows = [jnp.broadcast_to(sel_ref[pl.ds(c * bpc + b, 1), :], (SLC_LEN, tq)) for b in range(bpc)]
        return jnp.where(sidx <= t_row, jnp.concatenate(rows, axis=0), NEG)

    c_hi = ((j + 1) * tq + tk - 1) // tk
    run_branch(ks_ref, vst_ref, 0, c_hi, slc_mask)
    finish(1)

    def win_mask(c, sidx):
        d = t_row - sidx
        return jnp.where(d >= 0, jnp.where(d < WIN_LEN, _INF, NEG), NEG)

    c_lo = jnp.maximum(j * tq - (WIN_LEN - 1), 0) // tk
    run_branch(kw_ref, vwt_ref, c_lo, c_hi, win_mask)
    finish(2)

    for hh in range(B_HPG):
        o_ref[0, :, hh * HEAD_DIM:(hh + 1) * HEAD_DIM] = out_ref[hh].T.astype(o_ref.dtype)


def _nsa_attn(bq, kc, vc_t, bk, bv_tc, gates_t, cov_t):
    B, L, _ = bq.shape
    G = B_KV_GROUPS
    tq, tk = 512, 512
    n_cmp = kc.shape[2]
    n_slc = L // SLC_LEN
    n_sel = min(SLC_TOPN, n_slc)
    gw = B_HPG * HEAD_DIM
    nc = L // tk
    return pl.pallas_call(
        functools.partial(_nsa_attn_kernel, L=L, tq=tq, tk=tk, n_sel=n_sel),
        grid=(B, G, L // tq),
        in_specs=[pl.BlockSpec((1, tq, gw), lambda b, g, j: (b, j, g)),
                  pl.BlockSpec((1, 1, n_cmp, HEAD_DIM), lambda b, g, j: (b, g, 0, 0)),
                  pl.BlockSpec((1, 1, HEAD_DIM, n_cmp), lambda b, g, j: (b, g, 0, 0)),
                  pl.BlockSpec((1, L, HEAD_DIM), lambda b, g, j: (b, 0, 2 + g)),
                  pl.BlockSpec((1, 1, nc, HEAD_DIM + _ONES, tk), lambda b, g, j: (b, g, 0, 0, 0)),
                  pl.BlockSpec((1, L, HEAD_DIM), lambda b, g, j: (b, 0, 4 + g)),
                  pl.BlockSpec((1, 1, nc, HEAD_DIM + _ONES, tk),
                               lambda b, g, j: (b, B_KV_GROUPS + g, 0, 0, 0)),
                  pl.BlockSpec((1, 1, 3 * B_HPG, tq), lambda b, g, j: (b, g, 0, j)),
                  pl.BlockSpec((n_slc, n_cmp), lambda b, g, j: (0, 0))],
        out_specs=pl.BlockSpec((1, tq, gw), lambda b, g, j: (b, j, g)),
        out_shape=jax.ShapeDtypeStruct((B, L, B_HEADS * HEAD_DIM), bf16),
        scratch_shapes=[pltpu.VMEM((B_HPG, 1, tq), f32),
                        pltpu.VMEM((B_HPG, HEAD_DIM + _ONES, tq), f32),
                        pltpu.VMEM((B_HPG, HEAD_DIM, tq), f32), pltpu.VMEM((n_slc, tq), f32)],
        compiler_params=_cparams(("parallel", "parallel", "parallel")),
    )(bq, kc, vc_t, bk, bv_tc, bk, bv_tc, gates_t, cov_t)


def _diff_attn_kernel(q_ref, k_ref, vt_ref, lam_ref, g_ref, o_ref, m_ref, acc_ref,
                      *, tq, tk, lam_init):
    j = pl.program_id(2)
    t_row = j * tq + lax.broadcasted_iota(i32, (1, tq), 1)
    qs = [q_ref[0, :, mi * C_DIM:(mi + 1) * C_DIM] for mi in range(2)]
    _flash_init(m_ref, acc_ref)

    def chunk(c, lo, width, masked):
        off = pl.multiple_of(c * tk + lo, tq)
        vch = vt_ref[0, 0, c, :, lo:lo + width]
        cap = None
        if masked:
            sidx = off + lax.broadcasted_iota(i32, (width, tq), 0)
            cap = jnp.where(sidx <= t_row, _INF, NEG)
        ss = [_dot_nt(k_ref[0, pl.ds(off, width), mi * C_DIM:(mi + 1) * C_DIM], qs[mi]) for mi in range(2)]
        _flash_group(ss, cap, [vch] * 2, [m_ref.at[mi] for mi in range(2)],
                     [acc_ref.at[mi] for mi in range(2)])

    def body(c, carry):
        chunk(c, 0, tk, False)
        return carry

    c_diag = (j * tq) // tk
    lax.fori_loop(0, c_diag, body, 0)

    @pl.when(j % 2 == 0)
    def _():
        chunk(c_diag, 0, tq, True)

    @pl.when(j % 2 == 1)
    def _():
        chunk(c_diag, 0, tq, False)
        chunk(c_diag, tq, tq, True)

    lam = lam_ref[...]
    lam_val = (jnp.exp(jnp.sum(lam[0:1] * lam[1:2], axis=1, keepdims=True))
               - jnp.exp(jnp.sum(lam[2:3] * lam[3:4], axis=1, keepdims=True)) + lam_init)
    o = _flash_result(acc_ref[0], 2 * C_DIM) - lam_val * _flash_result(acc_ref[1], 2 * C_DIM)
    ms = jnp.mean(o * o, axis=0, keepdims=True)
    y = o * lax.rsqrt(ms + 1e-6) * g_ref[...] * (1.0 - lam_init)
    o_ref[0] = y.T.astype(o_ref.dtype)


def _diff_attn(cq, ck, cv_tc, lam, g_col, lam_init):
    B, L, _ = cq.shape
    tq, tk = _DIFF_TQ, cv_tc.shape[-1]
    assert tk == 2 * tq
    hw = 2 * C_DIM
    nc = L // tk
    return pl.pallas_call(
        functools.partial(_diff_attn_kernel, tq=tq, tk=tk, lam_init=lam_init),
        grid=(B, C_HEADS, L // tq),
        in_specs=[pl.BlockSpec((1, tq, hw), lambda b, h, j: (b, j, h)),
                  pl.BlockSpec((1, L, hw), lambda b, h, j: (b, 0, h)),
                  pl.BlockSpec((1, 1, nc, hw + _ONES, tk), lambda b, h, j: (b, h, 0, 0, 0)),
                  pl.BlockSpec((4, C_DIM), lambda b, h, j: (0, 0)),
                  pl.BlockSpec((hw, 1), lambda b, h, j: (0, 0))],
        out_specs=pl.BlockSpec((1, tq, hw), lambda b, h, j: (b, j, h)),
        out_shape=jax.ShapeDtypeStruct((B, L, C_HEADS * hw), bf16),
        scratch_shapes=[pltpu.VMEM((2, 1, tq), f32), pltpu.VMEM((2, hw + _ONES, tq), f32)],
        compiler_params=_cparams(("parallel", "parallel", "parallel")),
    )(cq, ck, cv_tc, lam, g_col)


def _merge_kernel(ya_ref, yb_ref, yc_ref, w_ref, g0_ref, g1_ref, g2_ref, o_ref, wb_ref):
    @pl.when(pl.program_id(1) == 0)
    def _():
        wb_ref[...] = w_ref[...].astype(bf16)

    acc = None
    for r, (y_ref, g_ref) in enumerate(((ya_ref, g0_ref), (yb_ref, g1_ref), (yc_ref, g2_ref))):
        br = jnp.dot(y_ref[...], wb_ref[r], preferred_element_type=f32)
        t = _sigmoid(g_ref[...].astype(f32)) * br
        acc = t if acc is None else acc + t
    o_ref[...] = acc.astype(o_ref.dtype)


def _merge(ya, yb, yc, w_br, l, h):
    m, kw = ya.shape
    tm, tn = 1024, 512
    npb = D_MODEL // tn
    yspec = pl.BlockSpec((tm, kw), lambda j, i: (i, 0))

    def gspec(r):
        return pl.BlockSpec((tm, tn), lambda j, i, _r=r: (i, _r * npb + j))

    return pl.pallas_call(
        _merge_kernel,
        grid=(npb, m // tm),
        in_specs=[yspec, yspec, yspec,
                  pl.BlockSpec((None, N_BRANCH, kw, tn), lambda j, i: (l, 0, 0, j)),
                  gspec(0), gspec(1), gspec(2)],
        out_specs=pl.BlockSpec((tm, tn), lambda j, i: (i, j)),
        out_shape=jax.ShapeDtypeStruct((m, D_MODEL), bf16),
        scratch_shapes=[pltpu.VMEM((N_BRANCH, kw, tn), bf16)],
        compiler_params=_cparams(("parallel", "arbitrary")),
    )(ya, yb, yc, w_br, h, h, h)


def _mm_res_ln_kernel(a_ref, w_ref, x_ref, gate_ref, lg_ref, lb_ref, sc_ref, sh_ref, xo_ref, *u_refs,
                      alpha, sub):
    tm = a_ref.shape[0]
    for r in range(tm // sub):
        rows = slice(r * sub, (r + 1) * sub)
        y = jnp.dot(a_ref[rows, :], w_ref[...], preferred_element_type=f32)
        z = alpha * x_ref[rows, :] + gate_ref[0] * y
        xn = _ln_rows(z, 1e-5) * lg_ref[...] + lb_ref[...]
        xo_ref[rows, :] = xn
        if u_refs:
            u_refs[0][rows, :] = (_ln_rows(xn, 1e-5) * (1.0 + sc_ref[0]) + sh_ref[0]).astype(bf16)


def _mm_res_ln(a, w, l, x2, gate, ln_g, ln_b, sc, sh, L, alpha, emit_u):
    m, kdim = a.shape
    d = w.shape[2]
    tm = _RES_LN_TM if kdim * d * 2 <= 16 * 1024 * 1024 else _RES_LN_SUB
    per_b = L // tm
    bspec = pl.BlockSpec((1, 1, d), lambda i: (i // per_b, 0, 0))
    vspec = pl.BlockSpec((1, d), lambda i: (0, 0))
    rspec = pl.BlockSpec((tm, d), lambda i: (i, 0))
    out_shape = [jax.ShapeDtypeStruct((m, d), f32)]
    out_specs = [rspec]
    if emit_u:
        out_shape.append(jax.ShapeDtypeStruct((m, d), bf16))
        out_specs.append(rspec)
    res = pl.pallas_call(
        functools.partial(_mm_res_ln_kernel, alpha=alpha, sub=_RES_LN_SUB),
        grid=(m // tm,),
        in_specs=[pl.BlockSpec((tm, kdim), lambda i: (i, 0)),
                  pl.BlockSpec((None, kdim, d), lambda i: (l, 0, 0), pipeline_mode=pl.Buffered(1)),
                  rspec, bspec, vspec, vspec, bspec, bspec],
        out_specs=out_specs,
        out_shape=out_shape,
        compiler_params=_cparams(("parallel",), 60 * 1024 * 1024),
    )(a, w, x2, gate, ln_g.reshape(1, d), ln_b.reshape(1, d), sc, sh)
    return res if emit_u else (res[0], None)


def _ffn_in_kernel(a_ref, wg_ref, wu_ref, o_ref, wgb_ref, wub_ref):
    @pl.when(pl.program_id(1) == 0)
    def _():
        wgb_ref[...] = wg_ref[...].astype(bf16)
        wub_ref[...] = wu_ref[...].astype(bf16)

    a = a_ref[...]
    g = jnp.dot(a, wgb_ref[...], preferred_element_type=f32)
    u = jnp.dot(a, wub_ref[...], preferred_element_type=f32)
    o_ref[...] = (g * _sigmoid(g) * u).astype(o_ref.dtype)


def _ffn_in(a, w, l):
    m, k = a.shape
    tm, tn = 1024, 512
    nb = D_FF // tn
    return pl.pallas_call(
        _ffn_in_kernel,
        grid=(nb, m // tm),
        in_specs=[pl.BlockSpec((tm, k), lambda j, i: (i, 0)),
                  pl.BlockSpec((None, k, tn), lambda j, i: (l, 0, j)),
                  pl.BlockSpec((None, k, tn), lambda j, i: (l, 0, nb + j))],
        out_specs=pl.BlockSpec((tm, tn), lambda j, i: (i, j)),
        out_shape=jax.ShapeDtypeStruct((m, D_FF), bf16),
        scratch_shapes=[pltpu.VMEM((k, tn), bf16), pltpu.VMEM((k, tn), bf16)],
        compiler_params=_cparams(("parallel", "arbitrary")),
    )(a, w, w)


def _rope_tables(L, d, mult=1.0):
    r = d // 4
    half = r // 2
    inv = ROPE_THETA ** (-(jnp.arange(half, dtype=f32) * 2.0) / r)
    ang = jnp.arange(L).astype(f32)[:, None] * inv[None, :]
    cos, sin = jnp.cos(ang), jnp.sin(ang)
    c = jnp.concatenate([cos, cos, jnp.ones((L, d - r), f32)], axis=1)
    s = jnp.concatenate([-sin, sin, jnp.zeros((L, d - r), f32)], axis=1)
    rep = 128 // d
    return tuple(jnp.tile(t * mult, (1, rep)) for t in (c, s))


def _pack_w_in_kernel(w_ref, o_ref, ov_ref):
    def cp(src, width, dst):
        for o in range(0, width, 1024):
            wd = min(1024, width - o)
            o_ref[:, dst + o:dst + o + wd] = w_ref[src + o:src + o + wd, :].T.astype(bf16)

    def bkv(i, kv, g):
        return _O_BKV + ((i * 2 + kv) * B_KV_GROUPS + g) * HEAD_DIM

    cp(_O_GL, N_BRANCH * D_MODEL, _P_GL)
    for src, dst in ((_O_AQ, _P_AQ), (_O_IQ, _P_IQ), (_O_BQ, _P_BQ), (_O_CQ, _P_CQ), (_O_CK, _P_CK)):
        cp(src, 1024, dst)
    for i in range(3):
        for g in range(B_KV_GROUPS):
            cp(bkv(i, 0, g), HEAD_DIM, _P_BK + (i * B_KV_GROUPS + g) * HEAD_DIM)
    for g in range(B_KV_GROUPS):
        cp(bkv(0, 1, g), HEAD_DIM, _P_BVC + g * HEAD_DIM)
    cp(_O_ALAT, A_LATENT, _P_ALAT)
    ov_ref[_PV_CV:_PV_CV + 1024, :] = w_ref[_O_CV:_O_CV + 1024, :].astype(bf16)
    for i in (1, 2):
        for g in range(B_KV_GROUPS):
            dst = _PV_BV + ((i - 1) * B_KV_GROUPS + g) * HEAD_DIM
            ov_ref[dst:dst + HEAD_DIM, :] = w_ref[bkv(i, 1, g):bkv(i, 1, g) + HEAD_DIM, :].astype(bf16)
    lane = lax.broadcasted_iota(i32, (w_ref.shape[1], 128), 1)
    assert _O_IW == _O_IK + IDX_DIM and _O_IK % 8 == 0 and _O_BG % 8 == 0
    blk = w_ref[_O_IK:_O_IK + 128, :].T
    o_ref[:, _P_IKW:_P_IKW + 128] = jnp.where(lane < IDX_DIM + IDX_HEADS, blk, 0.0).astype(bf16)
    blk = w_ref[_O_BG:_O_BG + 128, :].T
    o_ref[:, _P_BG:_P_BG + 128] = jnp.where(lane < 3 * B_HEADS, blk, 0.0).astype(bf16)


def _pack_w_in(w_in):
    depth, k, n = w_in.shape
    w_t = jnp.swapaxes(w_in, 1, 2)
    tc = 128
    return pl.pallas_call(
        _pack_w_in_kernel,
        grid=(depth, k // tc),
        in_specs=[pl.BlockSpec((None, n, tc), lambda l, i: (l, 0, i))],
        out_specs=[pl.BlockSpec((None, tc, _P_TOT), lambda l, i: (l, i, 0)),
                   pl.BlockSpec((None, _PV_TOT, tc), lambda l, i: (l, 0, i))],
        out_shape=[jax.ShapeDtypeStruct((depth, k, _P_TOT), bf16),
                   jax.ShapeDtypeStruct((depth, _PV_TOT, k), bf16)],
        compiler_params=_cparams(("parallel", "parallel")),
    )(w_t)


def _cover_t(L):
    n_cmp_pad = L // CMP_STRIDE
    starts = np.arange(n_cmp_pad) * CMP_STRIDE
    slc_start = np.arange(L // SLC_LEN) * SLC_LEN
    cover = ((starts[:, None] < slc_start[None, :] + SLC_LEN)
             & (starts[:, None] + CMP_LEN > slc_start[None, :])).astype(np.float32)
    n_cmp = (L - CMP_LEN) // CMP_STRIDE + 1
    cover[n_cmp:] = 0.0
    return jnp.asarray(cover.T)


def _token_mixing(u, h, B, L, l, lw, tabs, cov_t, lam_init):
    M = B * L
    G = B_KV_GROUPS
    aq, iq, bq, cq, ck, bk, alat_n, ikw, gates = _prep(h, lw['a_lat_g'], tabs, L)

    aw = A_HEADS * HEAD_DIM
    ak = _mm_rope(alat_n, lw['a_up'], l, aw, tabs['plain'], L, 512, aw)
    av_tc = _proj_t(alat_n, lw['a_up_vt'], l, 0, A_HEADS, HEAD_DIM, 512, B, L)
    ik = ikw[:, :IDX_DIM].astype(bf16).reshape(B, L, IDX_DIM)
    iw_t = ikw[:, IDX_DIM:IDX_DIM + IDX_HEADS].reshape(B, L, IDX_HEADS).transpose(0, 2, 1)
    mask_t = _dsa_mask(ik, iq.reshape(B, L, -1), iw_t)
    ya = _dsa_attn(aq.reshape(B, L, -1), ak.reshape(B, L, -1), av_tc, mask_t)

    bvc = h[:, _P_BVC:_P_BVC + G * HEAD_DIM].reshape(B, L, G, HEAD_DIM)
    bkc = bk[:, :G * HEAD_DIM].reshape(B, L, G, HEAD_DIM)
    n_row = L // CMP_STRIDE
    xcmp = jnp.stack([bkc, bvc], axis=1)
    xcmp = xcmp.transpose(0, 1, 3, 2, 4).reshape(B, 2, G, n_row, CMP_STRIDE * HEAD_DIM)
    cmp_out = _nsa_compress(xcmp, lw['cmp_w1'], lw['cmp_w2'], lw['cmp_pe'], l)
    kc = cmp_out[:, 0]
    vc_t = cmp_out[:, 1].transpose(0, 1, 3, 2)
    bv_tc = _proj_t(u, lw['w_vt'], l, _PV_BV, 2 * G, HEAD_DIM, 512, B, L)
    gates_t = gates[:, :3 * B_HEADS].reshape(B, L, G, 3 * B_HPG).transpose(0, 2, 3, 1)
    yb = _nsa_attn(bq.reshape(B, L, -1), kc, vc_t, bk.reshape(B, L, -1), bv_tc, gates_t, cov_t)

    cv_tc = _proj_t(u, lw['w_vt'], l, _PV_CV, C_HEADS, 2 * C_DIM, _DIFF_TK, B, L)
    yc = _diff_attn(cq.reshape(B, L, -1), ck.reshape(B, L, -1), cv_tc, lw['lam'],
                    lw['c_subln_g'].reshape(2 * C_DIM, 1), lam_init)

    return _merge(ya.reshape(M, -1), yb.reshape(M, -1), yc.reshape(M, -1), lw['w_br'], l, h)


def kernel(x, c, w_ada, b_ada, w_in, a_lat_g, a_up, cmp_w1, cmp_w2, cmp_pe, lam, c_subln_g, w_br, w_o,
           w_ffn_in, w_ffn_out, ln_g, ln_b):
    B, L, D = x.shape
    depth = w_ada.shape[0]
    M = B * L
    alpha = (2 * depth) ** 0.25

    c_pad = jnp.zeros((8, D), f32).at[:B].set(c)
    mod = _ada(c_pad, w_ada, b_ada)[:, :B]
    mods = [[mod[l, :, i * D:(i + 1) * D].reshape(B, 1, D) for i in range(6)] for l in range(depth)]

    assert HEAD_DIM == C_DIM
    tabs = dict(plain=_rope_tables(L, HEAD_DIM), query=_rope_tables(L, HEAD_DIM, HEAD_DIM ** -0.5 * _LOG2E),
                idx=_rope_tables(L, IDX_DIM))
    cov_t = _cover_t(L)

    w_in_p, w_vt = _pack_w_in(w_in)
    pe_flat = jnp.zeros((depth, 2, 8, CMP_LEN * HEAD_DIM), f32).at[:, :, 0].set(
        cmp_pe.reshape(depth, 2, CMP_LEN * HEAD_DIM)).astype(bf16)
    a_up_vt = jnp.swapaxes(a_up[:, :, A_HEADS * HEAD_DIM:], 1, 2).astype(bf16)
    wb = dict(a_up=a_up.astype(bf16), a_up_vt=a_up_vt, w_vt=w_vt, cmp_w1=cmp_w1.astype(bf16),
              cmp_w2=cmp_w2.astype(bf16), cmp_pe=pe_flat, w_br=w_br)
    w_o_b = w_o.astype(bf16)
    w_fo_b = w_ffn_out.astype(bf16)

    x2 = x.reshape(M, D)
    u = _lnmod(x2, mods[0][1], mods[0][0], L)
    for l in range(depth):
        lam_init = 0.8 - 0.6 * math.exp(-0.3 * l)
        sh_a, sc_a, g_a, sh_f, sc_f, g_f = mods[l]
        lw = dict(wb, a_lat_g=a_lat_g[l], lam=lam[l], c_subln_g=c_subln_g[l])
        h = _mm(u, w_in_p, l, 1024, 768, bf16)
        merged = _token_mixing(u, h, B, L, l, lw, tabs, cov_t, lam_init)
        x2, u = _mm_res_ln(merged, w_o_b, l, x2, g_a, ln_g[l, 0], ln_b[l, 0], sc_f, sh_f, L, alpha, True)
        f = _ffn_in(u, w_ffn_in, l)
        last = l == depth - 1
        nsc, nsh = (sc_f, sh_f) if last else (mods[l + 1][1], mods[l + 1][0])
        x2, u = _mm_res_ln(f, w_fo_b, l, x2, g_f, ln_g[l, 1], ln_b[l, 1], nsc, nsh, L, alpha, not last)
    return x2.reshape(B, L, D)
```

```python
import functools
import math

import numpy as np
import jax
import jax.numpy as jnp
from jax import lax
from jax.experimental import pallas as pl
from jax.experimental.pallas import tpu as pltpu

f32 = jnp.float32
bf16 = jnp.bfloat16
i32 = jnp.int32

D_MODEL = 2048
HEAD_DIM = 128
ROPE_THETA = 500000.0
NEG = -1e30
FORCE = 1e6
A_HEADS = 8
A_LATENT = 512
IDX_HEADS = 16
IDX_DIM = 64
DSA_TOPK = 256
B_HEADS = 8
B_KV_GROUPS = 2
B_HPG = B_HEADS // B_KV_GROUPS
CMP_LEN = 32
CMP_STRIDE = 16
SLC_LEN = 64
SLC_TOPN = 16
WIN_LEN = 512
C_HEADS = 4
C_DIM = 128
BRANCH_W = A_HEADS * HEAD_DIM
N_BRANCH = 3
D_FF = int(math.ceil(8 * D_MODEL / 3 / 256)) * 256

_O_AQ = 0
_O_ALAT = _O_AQ + A_HEADS * HEAD_DIM
_O_IQ = _O_ALAT + A_LATENT
_O_IK = _O_IQ + IDX_HEADS * IDX_DIM
_O_IW = _O_IK + IDX_DIM
_O_BQ = _O_IW + IDX_HEADS
_O_BKV = _O_BQ + B_HEADS * HEAD_DIM
_O_BG = _O_BKV + 3 * 2 * B_KV_GROUPS * HEAD_DIM
_O_CQ = _O_BG + 3 * B_HEADS
_O_CK = _O_CQ + C_HEADS * 2 * C_DIM
_O_CV = _O_CK + C_HEADS * 2 * C_DIM
_O_GL = _O_CV + C_HEADS * 2 * C_DIM
_N_IN = _O_GL + N_BRANCH * D_MODEL

_P_GL = 0
_P_BK = 6144
_P_BVC = 6912
_P_AQ = 7168
_P_IQ = 8192
_P_BQ = 9216
_P_CQ = 10240
_P_CK = 11264
_P_ALAT = 12288
_P_IKW = 12800
_P_BG = 12928
_P_TOT = 13056
_PV_CV = 0
_PV_BV = 1024
_PV_TOT = 1536

_VMEM_LIMIT = 48 * 1024 * 1024
_LOG2E = 1.4426950408889634
_INF = float("inf")
_ONES = 16
_RES_LN_TM = 512
_RES_LN_SUB = 256
_DIFF_TK = 1024
_DIFF_TQ = 512
_HEAD_GROUP = 2


def _cparams(sem, vmem=_VMEM_LIMIT):
    return pltpu.CompilerParams(dimension_semantics=sem, vmem_limit_bytes=vmem)


def _sigmoid(x):
    return 1.0 / (1.0 + jnp.exp(-x))


def _dot_nt(a, b):
    return lax.dot_general(a, b, (((1,), (1,)), ((), ())), preferred_element_type=f32)


def _ada_kernel(c_ref, w_ref, b_ref, o_ref):
    c = c_ref[...]
    cs = c * _sigmoid(c)
    o_ref[0] = jnp.dot(cs, w_ref[0], preferred_element_type=f32,
                       precision=lax.Precision.HIGHEST) + b_ref[0]


def _ada(c_pad, w_ada, b_ada):
    depth, d, n = w_ada.shape
    tn = 512
    return pl.pallas_call(
        _ada_kernel,
        grid=(depth, n // tn),
        in_specs=[pl.BlockSpec((8, d), lambda l, j: (0, 0)),
                  pl.BlockSpec((1, d, tn), lambda l, j: (l, 0, j)),
                  pl.BlockSpec((1, 1, tn), lambda l, j: (l, 0, j))],
        out_specs=pl.BlockSpec((1, 8, tn), lambda l, j: (l, 0, j)),
        out_shape=jax.ShapeDtypeStruct((depth, 8, n), f32),
        compiler_params=_cparams(("parallel", "parallel")),
    )(c_pad, w_ada, b_ada.reshape(depth, 1, n))


def _ln_rows(x, eps):
    mu = jnp.mean(x, axis=-1, keepdims=True)
    d = x - mu
    var = jnp.mean(d * d, axis=-1, keepdims=True)
    return d * lax.rsqrt(var + eps)


def _lnmod_kernel(x_ref, sc_ref, sh_ref, o_ref):
    y = _ln_rows(x_ref[...], 1e-5)
    o_ref[...] = (y * (1.0 + sc_ref[0]) + sh_ref[0]).astype(o_ref.dtype)


def _lnmod(x2, sc, sh, L):
    m, d = x2.shape
    tm = 512
    per_b = L // tm
    return pl.pallas_call(
        _lnmod_kernel,
        grid=(m // tm,),
        in_specs=[pl.BlockSpec((tm, d), lambda i: (i, 0)),
                  pl.BlockSpec((1, 1, d), lambda i: (i // per_b, 0, 0)),
                  pl.BlockSpec((1, 1, d), lambda i: (i // per_b, 0, 0))],
        out_specs=pl.BlockSpec((tm, d), lambda i: (i, 0)),
        out_shape=jax.ShapeDtypeStruct((m, d), bf16),
        compiler_params=_cparams(("parallel",)),
    )(x2, sc, sh)


def _mm_kernel(a_ref, w_ref, o_ref):
    o_ref[...] = jnp.dot(a_ref[...], w_ref[...], preferred_element_type=f32).astype(o_ref.dtype)


def _mm(a, w, l, tm, tn, out_dtype):
    m, k = a.shape
    n = w.shape[2]
    return pl.pallas_call(
        _mm_kernel,
        grid=(n // tn, m // tm),
        in_specs=[pl.BlockSpec((tm, k), lambda j, i: (i, 0)),
                  pl.BlockSpec((None, k, tn), lambda j, i: (l, 0, j))],
        out_specs=pl.BlockSpec((tm, tn), lambda j, i: (i, j)),
        out_shape=jax.ShapeDtypeStruct((m, n), out_dtype),
        compiler_params=_cparams(("parallel", "parallel")),
    )(a, w)


def _proj_t_kernel(w_ref, a_ref, o_ref):
    n, d = o_ref.shape[1], o_ref.shape[3] - _ONES
    tk = o_ref.shape[4]
    res = _dot_nt(w_ref[...], a_ref[...])
    for i in range(n):
        o_ref[0, i, 0, :d, :] = res[i * d:(i + 1) * d, :].astype(o_ref.dtype)
        o_ref[0, i, 0, d:, :] = jnp.ones((_ONES, tk), o_ref.dtype)


def _proj_t(a, w_t, l, row0, n, d, tk, B, L):
    k = a.shape[1]
    rows = n * d
    nc = L // tk
    return pl.pallas_call(
        _proj_t_kernel,
        grid=(B, nc),
        in_specs=[pl.BlockSpec((None, rows, k), lambda b, c: (l, row0 // rows, 0)),
                  pl.BlockSpec((tk, k), lambda b, c: (b * nc + c, 0))],
        out_specs=pl.BlockSpec((1, n, 1, d + _ONES, tk), lambda b, c: (b, 0, c, 0, 0)),
        out_shape=jax.ShapeDtypeStruct((B, n, nc, d + _ONES, tk), bf16),
        compiler_params=_cparams(("parallel", "parallel")),
    )(w_t, a)


def _rope_low(shape, d, half):
    lane = lax.broadcasted_iota(i32, shape, 1)
    return (lane & (d - 1)) < half


def _rope_heads(x, c, s, low, half):
    partner = jnp.where(low, pltpu.roll(x, 128 - half, 1), pltpu.roll(x, half, 1))
    return x * c + partner * s


def _mm_rope_kernel(a_ref, w_ref, c_ref, s_ref, o_ref):
    acc = jnp.dot(a_ref[...], w_ref[...], preferred_element_type=f32)
    c, s = c_ref[...], s_ref[...]
    low = _rope_low(c.shape, HEAD_DIM, HEAD_DIM // 8)
    for h in range(acc.shape[1] // 128):
        sl = slice(h * 128, (h + 1) * 128)
        o_ref[:, sl] = _rope_heads(acc[:, sl], c, s, low, HEAD_DIM // 8).astype(o_ref.dtype)


def _mm_rope(a, w, l, n, tabs, L, tm, tn):
    m, k = a.shape
    per_b = L // tm
    tspec = pl.BlockSpec((tm, 128), lambda j, i: (i % per_b, 0))
    return pl.pallas_call(
        _mm_rope_kernel,
        grid=(n // tn, m // tm),
        in_specs=[pl.BlockSpec((tm, k), lambda j, i: (i, 0)),
                  pl.BlockSpec((None, k, tn), lambda j, i: (l, 0, j)),
                  tspec, tspec],
        out_specs=pl.BlockSpec((tm, tn), lambda j, i: (i, j)),
        out_shape=jax.ShapeDtypeStruct((m, n), bf16),
        compiler_params=_cparams(("parallel", "parallel")),
    )(a, w, *tabs)


def _prep_kernel(aq_ref, iq_ref, bq_ref, cq_ref, ck_ref, bk_ref, alat_ref, ikw_ref, bg_ref, alg_ref,
                 c_ref, s_ref, cq_ref_t, sq_ref_t, c6_ref, s6_ref,
                 aq_o, iq_o, bq_o, cq_o, ck_o, bk_o, alat_o, ikw_o, g_o):
    rows = 64
    plain = (c_ref, s_ref, _rope_low((rows, 128), HEAD_DIM, HEAD_DIM // 8), HEAD_DIM // 8)
    scaled = (cq_ref_t, sq_ref_t) + plain[2:]
    idx = (c6_ref, s6_ref, _rope_low((rows, 128), IDX_DIM, IDX_DIM // 8), IDX_DIM // 8)

    def rope(x, tabs, rs):
        return _rope_heads(x, tabs[0][rs, :], tabs[1][rs, :], tabs[2], tabs[3])

    def rope_all(src, dst, tabs):
        for r in range(src.shape[0] // rows):
            rs = slice(r * rows, (r + 1) * rows)
            for h in range(src.shape[1] // 128):
                sl = slice(h * 128, (h + 1) * 128)
                dst[rs, sl] = rope(src[rs, sl].astype(f32), tabs, rs).astype(dst.dtype)

    rope_all(aq_ref, aq_o, scaled)
    rope_all(bq_ref, bq_o, scaled)
    rope_all(cq_ref, cq_o, scaled)
    rope_all(ck_ref, ck_o, plain)
    rope_all(bk_ref, bk_o, plain)
    rope_all(iq_ref, iq_o, idx)

    a = alat_ref[...].astype(f32)
    ms = jnp.mean(a * a, axis=-1, keepdims=True)
    alat_o[...] = (a * lax.rsqrt(ms + 1e-6) * alg_ref[...]).astype(alat_o.dtype)

    isk = lax.broadcasted_iota(i32, (rows, 128), 1) < IDX_DIM
    for r in range(ikw_ref.shape[0] // rows):
        rs = slice(r * rows, (r + 1) * rows)
        x = ikw_ref[rs, :].astype(f32)
        mu = jnp.sum(jnp.where(isk, x, 0.0), axis=-1, keepdims=True) * (1.0 / IDX_DIM)
        d = jnp.where(isk, x - mu, 0.0)
        var = jnp.sum(d * d, axis=-1, keepdims=True) * (1.0 / IDX_DIM)
        y = d * lax.rsqrt(var + 1e-5)
        ikw_o[rs, :] = jnp.where(isk, rope(y, idx, rs), x * (IDX_DIM ** -0.5 * IDX_HEADS ** -0.5))

    g_o[...] = _sigmoid(bg_ref[...].astype(f32))


def _prep(h, a_lat_g, tabs, L):
    m = h.shape[0]
    tm = 512
    per_b = L // tm

    def hs(width, off):
        return pl.BlockSpec((tm, width), lambda i, _o=off // width: (i, _o))

    tspec = pl.BlockSpec((tm, 128), lambda i: (i % per_b, 0))

    def os(width):
        return pl.BlockSpec((tm, width), lambda i: (i, 0))

    outs = [(1024, bf16)] * 5 + [(768, bf16), (512, bf16), (128, f32), (128, f32)]
    return pl.pallas_call(
        _prep_kernel,
        grid=(m // tm,),
        in_specs=[hs(1024, _P_AQ), hs(1024, _P_IQ), hs(1024, _P_BQ), hs(1024, _P_CQ), hs(1024, _P_CK),
                  hs(768, _P_BK), hs(512, _P_ALAT), hs(128, _P_IKW), hs(128, _P_BG),
                  pl.BlockSpec((1, A_LATENT), lambda i: (0, 0))] + [tspec] * 6,
        out_specs=[os(w) for w, _ in outs],
        out_shape=[jax.ShapeDtypeStruct((m, w), dt) for w, dt in outs],
        compiler_params=_cparams(("parallel",)),
    )(h, h, h, h, h, h, h, h, h, a_lat_g.reshape(1, A_LATENT), *tabs['plain'], *tabs['query'], *tabs['idx'])


def _f32_order_key(x):
    b = int(np.float32(x).view(np.int32))
    return b ^ ((b >> 31) & 0x7FFFFFFF)


_KEY_NEG = _f32_order_key(NEG)


def _dsa_mask_kernel(ik_ref, iq_ref, iw_ref, o_ref, key_ref, qp_ref, j_ref, *, L, tq, ksel):
    j = pl.program_id(1)
    ck = 512
    nck = L // ck
    nc = ((j + 1) * tq + ck - 1) // ck
    n_out_i = L - nc * ck
    n_out = n_out_i.astype(f32)
    t_row = j * tq + lax.broadcasted_iota(i32, (1, tq), 1)

    for hp in range(IDX_HEADS // 2):
        for e in range(2):
            h = 2 * hp + e
            qp_ref[hp, e * tq:(e + 1) * tq, :] = iq_ref[0, :, h * IDX_DIM:(h + 1) * IDX_DIM]

    def score_chunk(c, carry):
        off = pl.multiple_of(c * ck, ck)
        ikc = ik_ref[0, pl.ds(off, ck), :]
        acc = jnp.zeros((ck, tq), f32)
        for hp in range(IDX_HEADS // 2):
            s2 = _dot_nt(ikc, qp_ref[hp])
            acc = acc + jnp.maximum(s2[:, :tq], 0.0) * iw_ref[0, 2 * hp:2 * hp + 1, :]
            acc = acc + jnp.maximum(s2[:, tq:], 0.0) * iw_ref[0, 2 * hp + 1:2 * hp + 2, :]
        acc = jnp.where(acc == 0.0, 0.0, acc)
        sidx = off + lax.broadcasted_iota(i32, (ck, tq), 0)
        key_ref[pl.ds(off, ck), :] = jnp.where(sidx <= t_row, acc, NEG)
        return carry

    def score_pair(p, carry):
        score_chunk(2 * p, carry)
        return score_chunk(2 * p + 1, carry)

    lax.fori_loop(0, nc // 2, score_pair, 0)

    @pl.when(nc % 2 == 1)
    def _():
        score_chunk(nc - 1, 0)

    def as_f32(key):
        return lax.bitcast_convert_type(key ^ ((key >> 31) & 0x7FFFFFFF), f32)

    def count(pred_fn):
        def body(c, cnt):
            off = pl.multiple_of(c * ck, ck)
            k = key_ref[pl.ds(off, ck), :]
            sidx = off + lax.broadcasted_iota(i32, (ck, tq), 0)
            return cnt + jnp.sum(pred_fn(k, sidx).reshape(ck // 64, 64, tq), axis=0)
        part = lax.fori_loop(0, nc, body, jnp.zeros((64, tq), f32))
        return jnp.sum(part, axis=0, keepdims=True)

    kf = float(ksel)

    def bit_body(i, carry):
        thr, cnt_thr = carry
        cand = thr + lax.shift_left(jnp.int32(1), 31 - i)
        cand_f = as_f32(cand)
        cnt = count(lambda k, s: jnp.where(k >= cand_f, 1.0, 0.0)) + jnp.where(_KEY_NEG >= cand, n_out, 0.0)
        ok = cnt >= kf
        return jnp.where(ok, cand, thr), jnp.where(ok, cnt, cnt_thr)

    thr_key, cnt_ge = lax.fori_loop(0, 32, bit_body, (jnp.full((1, tq), -2 ** 31, i32),
                                                       jnp.full((1, tq), float(L), f32)))
    thr = as_f32(thr_key)

    j_ref[...] = jnp.full((1, tq), L, i32)

    @pl.when(jnp.max(cnt_ge) > kf)
    def _():
        nbits = L.bit_length() - 1
        cnt_gt = count(lambda k, s: jnp.where(k > thr, 1.0, 0.0)) + jnp.where(_KEY_NEG > thr_key, n_out, 0.0)
        need = kf - cnt_gt

        def jbit(i, cur):
            cand = cur | lax.shift_left(jnp.int32(1), nbits - 1 - i)
            f = count(lambda k, s: jnp.where(k == thr, jnp.where(s < cand, 1.0, 0.0), 0.0))
            f = f + jnp.where(thr_key == _KEY_NEG, jnp.clip(cand - nc * ck, 0, n_out_i).astype(f32), 0.0)
            return jnp.where(f < need, cand, cur)

        j_ref[...] = lax.fori_loop(0, nbits, jbit, jnp.zeros((1, tq), i32))

    jlast = j_ref[...]

    def write(c, carry):
        off = pl.multiple_of(c * ck, ck)
        k = key_ref[pl.ds(off, ck), :]
        sidx = off + lax.broadcasted_iota(i32, (ck, tq), 0)
        sel = jnp.where(k > thr, _INF, jnp.where(k == thr, jnp.where(sidx <= jlast, _INF, NEG), NEG))
        o_ref[0, pl.ds(off, ck), :] = jnp.where(sidx <= t_row, sel, NEG)
        return carry

    lax.fori_loop(0, nc, write, 0)

    def write_rest(c, carry):
        off = pl.multiple_of(c * ck, ck)
        o_ref[0, pl.ds(off, ck), :] = jnp.full((ck, tq), NEG, f32)
        return carry

    lax.fori_loop(nc, nck, write_rest, 0)


def _dsa_mask(ik, iq, iw_t):
    B, L, _ = iq.shape
    tq = 128
    ksel = min(DSA_TOPK, L // 4)
    return pl.pallas_call(
        functools.partial(_dsa_mask_kernel, L=L, tq=tq, ksel=ksel),
        grid=(B, L // tq),
        in_specs=[pl.BlockSpec((1, L, IDX_DIM), lambda b, j: (b, 0, 0)),
                  pl.BlockSpec((1, tq, IDX_HEADS * IDX_DIM), lambda b, j: (b, j, 0)),
                  pl.BlockSpec((1, IDX_HEADS, tq), lambda b, j: (b, 0, j))],
        out_specs=pl.BlockSpec((1, L, tq), lambda b, j: (b, 0, j)),
        out_shape=jax.ShapeDtypeStruct((B, L, L), f32),
        scratch_shapes=[pltpu.VMEM((L, tq), f32), pltpu.VMEM((IDX_HEADS // 2, 2 * tq, IDX_DIM), bf16),
                        pltpu.VMEM((1, tq), i32)],
        compiler_params=_cparams(("parallel", "parallel")),
    )(ik, iq, iw_t)


def _col_reduce(x, op):
    r, c = x.shape
    if r > 64:
        x = op(x.reshape(r // 64, 64, c), axis=0)
    return op(x, axis=0, keepdims=True)


def _flash_group(ss, cap, v_ts, m_refs, acc_refs):
    ss, mloc = _flash_mask_max(ss, cap)
    _flash_update(ss, mloc, cap is not None, v_ts, m_refs, acc_refs)


def _flash_mask_max(ss, cap):
    if cap is not None:
        ss = [jnp.minimum(s, cap) for s in ss]
    return ss, [_col_reduce(s, jnp.max) for s in ss]


def _flash_update(ss, mloc, masked, v_ts, m_refs, acc_refs):
    n = len(ss)
    m_prev = [r[...] for r in m_refs]
    m_new = [jnp.maximum(m_prev[i], mloc[i]) for i in range(n)]
    alpha = [jnp.exp2(m_prev[i] - m_new[i]) for i in range(n)]
    ps = [jnp.exp2((ss[i] - m_new[i]).astype(bf16)) for i in range(n)]
    pv = [jnp.dot(v_ts[i], ps[i], preferred_element_type=f32) for i in range(n)]
    for i in range(n):
        acc_new = acc_refs[i][...] * alpha[i] + pv[i]
        if masked:
            acc_new = jnp.where(m_new[i] <= NEG, 0.0, acc_new)
        acc_refs[i][...] = acc_new
        m_refs[i][...] = m_new[i]


def _flash_heads(score_fns, cap, v_ts, m_ref, acc_ref, group):
    n = len(score_fns)
    groups = [list(range(g, min(g + group, n))) for g in range(0, n, group)]
    nxt = _flash_mask_max([score_fns[h]() for h in groups[0]], cap)
    for gi, hs in enumerate(groups):
        ss, mloc = nxt
        if gi + 1 < len(groups):
            nxt = _flash_mask_max([score_fns[h]() for h in groups[gi + 1]], cap)
        _flash_update(ss, mloc, cap is not None, [v_ts[h] for h in hs], [m_ref.at[h] for h in hs],
                      [acc_ref.at[h] for h in hs])


def _flash_init(m_ref, acc_ref):
    m_ref[...] = jnp.full(m_ref.shape, NEG, f32)
    acc_ref[...] = jnp.zeros(acc_ref.shape, f32)


def _flash_result(acc, d):
    return acc[:d] / jnp.maximum(acc[d:d + 1], 1e-30)


def _dsa_attn_kernel(jm_ref, cm_ref, lm_ref, q_ref, k_ref, vt_ref, mask_ref, o_ref, m_ref, acc_ref):
    s = pl.program_id(1)

    @pl.when(cm_ref[s] == 0)
    def _():
        _flash_init(m_ref, acc_ref)

    cap = mask_ref[0]

    def scores(h):
        sl = slice(h * HEAD_DIM, (h + 1) * HEAD_DIM)
        return lambda: _dot_nt(k_ref[0, :, sl], q_ref[0, :, sl])

    v_ts = [vt_ref[0, h, 0] for h in range(A_HEADS)]
    _flash_heads([scores(h) for h in range(A_HEADS)], cap, v_ts, m_ref, acc_ref, _HEAD_GROUP)

    @pl.when(lm_ref[s] == 1)
    def _():
        for h in range(A_HEADS):
            o = _flash_result(acc_ref[h], HEAD_DIM)
            o_ref[0, :, h * HEAD_DIM:(h + 1) * HEAD_DIM] = o.T.astype(o_ref.dtype)


def _dsa_attn(q, k, v_t, mask_t):
    B, L, W = q.shape
    tq, tk = 512, 512
    pairs = [(j, c) for j in range(L // tq) for c in range(((j + 1) * tq - 1) // tk + 1)]
    jm = jnp.asarray([p[0] for p in pairs], i32)
    cm = jnp.asarray([p[1] for p in pairs], i32)
    lm = jnp.asarray([int(i + 1 == len(pairs) or pairs[i + 1][0] != p[0]) for i, p in enumerate(pairs)], i32)
    return pl.pallas_call(
        _dsa_attn_kernel,
        grid_spec=pltpu.PrefetchScalarGridSpec(
            num_scalar_prefetch=3,
            grid=(B, len(pairs)),
            in_specs=[pl.BlockSpec((1, tq, W), lambda b, s, jm, cm, lm: (b, jm[s], 0)),
                      pl.BlockSpec((1, tk, W), lambda b, s, jm, cm, lm: (b, cm[s], 0)),
                      pl.BlockSpec((1, A_HEADS, 1, HEAD_DIM + _ONES, tk),
                                   lambda b, s, jm, cm, lm: (b, 0, cm[s], 0, 0)),
                      pl.BlockSpec((1, tk, tq), lambda b, s, jm, cm, lm: (b, cm[s], jm[s]))],
            out_specs=pl.BlockSpec((1, tq, W), lambda b, s, jm, cm, lm: (b, jm[s], 0)),
            scratch_shapes=[pltpu.VMEM((A_HEADS, 1, tq), f32),
                            pltpu.VMEM((A_HEADS, HEAD_DIM + _ONES, tq), f32)]),
        out_shape=jax.ShapeDtypeStruct((B, L, W), bf16),
        compiler_params=_cparams(("parallel", "arbitrary")),
    )(jm, cm, lm, q, k, v_t, mask_t)


def _nsa_cmp_kernel(x_ref, w1_ref, w2_ref, pe_ref, o_ref):
    x = x_ref[0, 0, 0]
    w1 = w1_ref[0]
    half = CMP_STRIDE * HEAD_DIM
    a = jnp.dot(x, w1[:half], preferred_element_type=f32)
    b = jnp.dot(x, w1[half:], preferred_element_type=f32)
    pe = jnp.dot(pe_ref[0], w1, preferred_element_type=f32)[0:1]
    n = a.shape[0]
    pre = a + pltpu.roll(b, n - 1, 0) + pe
    act = pre * _sigmoid(pre)
    o_ref[0, 0, 0] = jnp.dot(act.astype(bf16), w2_ref[0], preferred_element_type=f32).astype(o_ref.dtype)


def _nsa_compress(x, w1, w2, pe, l):
    B, _, G, n, wd = x.shape
    return pl.pallas_call(
        _nsa_cmp_kernel,
        grid=(B, 2, G),
        in_specs=[pl.BlockSpec((1, 1, 1, n, wd), lambda b, t, g: (b, t, g, 0, 0)),
                  pl.BlockSpec((None, 1, CMP_LEN * HEAD_DIM, HEAD_DIM), lambda b, t, g: (l, t, 0, 0)),
                  pl.BlockSpec((None, 1, HEAD_DIM, HEAD_DIM), lambda b, t, g: (l, t, 0, 0)),
                  pl.BlockSpec((None, 1, 8, CMP_LEN * HEAD_DIM), lambda b, t, g: (l, t, 0, 0))],
        out_specs=pl.BlockSpec((1, 1, 1, n, HEAD_DIM), lambda b, t, g: (b, t, g, 0, 0)),
        out_shape=jax.ShapeDtypeStruct((B, 2, G, n, HEAD_DIM), bf16),
        compiler_params=_cparams(("parallel", "parallel", "parallel")),
    )(x, w1, w2, pe)


def _nsa_attn_kernel(q_ref, kc_ref, vct_ref, ks_ref, vst_ref, kw_ref, vwt_ref, g_ref, cov_ref, o_ref,
                     m_ref, acc_ref, out_ref, sel_ref, *, L, tq, tk, n_sel):
    j = pl.program_id(2)
    t_row = j * tq + lax.broadcasted_iota(i32, (1, tq), 1)
    n_cmp = kc_ref.shape[2]
    n_slc = L // SLC_LEN
    qs = [q_ref[0, :, hh * HEAD_DIM:(hh + 1) * HEAD_DIM] for hh in range(B_HPG)]

    def gate(hh, i):
        return g_ref[0, 0, hh * 3 + i:hh * 3 + i + 1, :]

    kc = kc_ref[0, 0]
    vct = vct_ref[0, 0]
    cend = lax.broadcasted_iota(i32, (n_cmp, tq), 0) * CMP_STRIDE + (CMP_LEN - 1)
    mc = cend <= t_row
    ss = [jnp.where(mc, _dot_nt(kc, qs[hh]), NEG) for hh in range(B_HPG)]
    mxs = [_col_reduce(s, jnp.max) for s in ss]
    ps = [jnp.where(mc, jnp.exp2(ss[hh] - mxs[hh]), 0.0) for hh in range(B_HPG)]
    inv = [1.0 / jnp.maximum(_col_reduce(p, jnp.sum), 1e-30) for p in ps]
    pvs = [jnp.dot(vct, p.astype(bf16), preferred_element_type=f32) for p in ps]
    psum = ps[0] * inv[0]
    for hh in range(1, B_HPG):
        psum = psum + ps[hh] * inv[hh]
    for hh in range(B_HPG):
        out_ref[hh] = (gate(hh, 0) * inv[hh]) * pvs[hh]
    imp = jnp.dot(cov_ref[...], psum, preferred_element_type=f32, precision=lax.Precision.HIGHEST)
    blk = lax.broadcasted_iota(i32, (n_slc, tq), 0)
    cur = t_row >> (SLC_LEN.bit_length() - 1)
    forced = (blk == 0) | (blk == cur) | (blk == cur - 1)
    imp = jnp.where(forced, FORCE, jnp.where(blk <= cur, imp, NEG))
    rank = jnp.zeros((n_slc, tq), f32)
    for r in range(n_slc):
        row = imp[r:r + 1, :]
        rank = rank + jnp.where(row > imp, 1.0, jnp.where(row == imp, jnp.where(blk > r, 1.0, 0.0), 0.0))
    sel_ref[...] = jnp.where(rank < float(n_sel), _INF, NEG)

    def finish(i):
        for hh in range(B_HPG):
            out_ref[hh] = out_ref[hh] + gate(hh, i) * _flash_result(acc_ref[hh], HEAD_DIM)

    def run_branch(k_ref, vt_ref, c_lo, c_hi, mask_fn):
        _flash_init(m_ref, acc_ref)

        def body(c, carry):
            off = pl.multiple_of(c * tk, tk)
            kch = k_ref[0, pl.ds(off, tk), :]
            vch = vt_ref[0, 0, c]
            sidx = off + lax.broadcasted_iota(i32, (tk, tq), 0)
            cap = mask_fn(c, sidx)
            fns = [functools.partial(_dot_nt, kch, qs[hh]) for hh in range(B_HPG)]
            _flash_heads(fns, cap, [vch] * B_HPG, m_ref, acc_ref, _HEAD_GROUP)
            return carry

        lax.fori_loop(c_lo, c_hi, body, 0)

    bpc = tk // SLC_LEN

    def slc_mask(c, sidx):
        rows = [jnp.broadcast_to(sel_ref[pl.ds(c * bpc + b, 1), :], (SLC_LEN, tq)) for b in range(bpc)]
        return jnp.where(sidx <= t_row, jnp.concatenate(rows, axis=0), NEG)

    c_hi = ((j + 1) * tq + tk - 1) // tk
    run_branch(ks_ref, vst_ref, 0, c_hi, slc_mask)
    finish(1)

    def win_mask(c, sidx):
        d = t_row - sidx
        return jnp.where(d >= 0, jnp.where(d < WIN_LEN, _INF, NEG), NEG)

    c_lo = jnp.maximum(j * tq - (WIN_LEN - 1), 0) // tk
    run_branch(kw_ref, vwt_ref, c_lo, c_hi, win_mask)
    finish(2)

    for hh in range(B_HPG):
        o_ref[0, :, hh * HEAD_DIM:(hh + 1) * HEAD_DIM] = out_ref[hh].T.astype(o_ref.dtype)


def _nsa_attn(bq, kc, vc_t, bk, bv_tc, gates_t, cov_t):
    B, L, _ = bq.shape
    G = B_KV_GROUPS
    tq, tk = 512, 512
    n_cmp = kc.shape[2]
    n_slc = L // SLC_LEN
    n_sel = min(SLC_TOPN, n_slc)
    gw = B_HPG * HEAD_DIM
    nc = L // tk
    return pl.pallas_call(
        functools.partial(_nsa_attn_kernel, L=L, tq=tq, tk=tk, n_sel=n_sel),
        grid=(B, G, L // tq),
        in_specs=[pl.BlockSpec((1, tq, gw), lambda b, g, j: (b, j, g)),
                  pl.BlockSpec((1, 1, n_cmp, HEAD_DIM), lambda b, g, j: (b, g, 0, 0)),
                  pl.BlockSpec((1, 1, HEAD_DIM, n_cmp), lambda b, g, j: (b, g, 0, 0)),
                  pl.BlockSpec((1, L, HEAD_DIM), lambda b, g, j: (b, 0, 2 + g)),
                  pl.BlockSpec((1, 1, nc, HEAD_DIM + _ONES, tk), lambda b, g, j: (b, g, 0, 0, 0)),
                  pl.BlockSpec((1, L, HEAD_DIM), lambda b, g, j: (b, 0, 4 + g)),
                  pl.BlockSpec((1, 1, nc, HEAD_DIM + _ONES, tk),
                               lambda b, g, j: (b, B_KV_GROUPS + g, 0, 0, 0)),
                  pl.BlockSpec((1, 1, 3 * B_HPG, tq), lambda b, g, j: (b, g, 0, j)),
                  pl.BlockSpec((n_slc, n_cmp), lambda b, g, j: (0, 0))],
        out_specs=pl.BlockSpec((1, tq, gw), lambda b, g, j: (b, j, g)),
        out_shape=jax.ShapeDtypeStruct((B, L, B_HEADS * HEAD_DIM), bf16),
        scratch_shapes=[pltpu.VMEM((B_HPG, 1, tq), f32),
                        pltpu.VMEM((B_HPG, HEAD_DIM + _ONES, tq), f32),
                        pltpu.VMEM((B_HPG, HEAD_DIM, tq), f32), pltpu.VMEM((n_slc, tq), f32)],
        compiler_params=_cparams(("parallel", "parallel", "parallel")),
    )(bq, kc, vc_t, bk, bv_tc, bk, bv_tc, gates_t, cov_t)


def _diff_attn_kernel(q_ref, k_ref, vt_ref, lam_ref, g_ref, o_ref, m_ref, acc_ref,
                      *, tq, tk, lam_init):
    j = pl.program_id(2)
    t_row = j * tq + lax.broadcasted_iota(i32, (1, tq), 1)
    qs = [q_ref[0, :, mi * C_DIM:(mi + 1) * C_DIM] for mi in range(2)]
    _flash_init(m_ref, acc_ref)

    def chunk(c, lo, width, masked):
        off = pl.multiple_of(c * tk + lo, tq)
        vch = vt_ref[0, 0, c, :, lo:lo + width]
        cap = None
        if masked:
            sidx = off + lax.broadcasted_iota(i32, (width, tq), 0)
            cap = jnp.where(sidx <= t_row, _INF, NEG)
        ss = [_dot_nt(k_ref[0, pl.ds(off, width), mi * C_DIM:(mi + 1) * C_DIM], qs[mi]) for mi in range(2)]
        _flash_group(ss, cap, [vch] * 2, [m_ref.at[mi] for mi in range(2)],
                     [acc_ref.at[mi] for mi in range(2)])

    def body(c, carry):
        chunk(c, 0, tk, False)
        return carry

    c_diag = (j * tq) // tk
    lax.fori_loop(0, c_diag, body, 0)

    @pl.when(j % 2 == 0)
    def _():
        chunk(c_diag, 0, tq, True)

    @pl.when(j % 2 == 1)
    def _():
        chunk(c_diag, 0, tq, False)
        chunk(c_diag, tq, tq, True)

    lam = lam_ref[...]
    lam_val = (jnp.exp(jnp.sum(lam[0:1] * lam[1:2], axis=1, keepdims=True))
               - jnp.exp(jnp.sum(lam[2:3] * lam[3:4], axis=1, keepdims=True)) + lam_init)
    o = _flash_result(acc_ref[0], 2 * C_DIM) - lam_val * _flash_result(acc_ref[1], 2 * C_DIM)
    ms = jnp.mean(o * o, axis=0, keepdims=True)
    y = o * lax.rsqrt(ms + 1e-6) * g_ref[...] * (1.0 - lam_init)
    o_ref[0] = y.T.astype(o_ref.dtype)


def _diff_attn(cq, ck, cv_tc, lam, g_col, lam_init):
    B, L, _ = cq.shape
    tq, tk = _DIFF_TQ, cv_tc.shape[-1]
    assert tk == 2 * tq
    hw = 2 * C_DIM
    nc = L // tk
    return pl.pallas_call(
        functools.partial(_diff_attn_kernel, tq=tq, tk=tk, lam_init=lam_init),
        grid=(B, C_HEADS, L // tq),
        in_specs=[pl.BlockSpec((1, tq, hw), lambda b, h, j: (b, j, h)),
                  pl.BlockSpec((1, L, hw), lambda b, h, j: (b, 0, h)),
                  pl.BlockSpec((1, 1, nc, hw + _ONES, tk), lambda b, h, j: (b, h, 0, 0, 0)),
                  pl.BlockSpec((4, C_DIM), lambda b, h, j: (0, 0)),
                  pl.BlockSpec((hw, 1), lambda b, h, j: (0, 0))],
        out_specs=pl.BlockSpec((1, tq, hw), lambda b, h, j: (b, j, h)),
        out_shape=jax.ShapeDtypeStruct((B, L, C_HEADS * hw), bf16),
        scratch_shapes=[pltpu.VMEM((2, 1, tq), f32), pltpu.VMEM((2, hw + _ONES, tq), f32)],
        compiler_params=_cparams(("parallel", "parallel", "parallel")),
    )(cq, ck, cv_tc, lam, g_col)


def _merge_kernel(ya_ref, yb_ref, yc_ref, w_ref, g0_ref, g1_ref, g2_ref, o_ref, wb_ref):
    @pl.when(pl.program_id(1) == 0)
    def _():
        wb_ref[...] = w_ref[...].astype(bf16)

    acc = None
    for r, (y_ref, g_ref) in enumerate(((ya_ref, g0_ref), (yb_ref, g1_ref), (yc_ref, g2_ref))):
        br = jnp.dot(y_ref[...], wb_ref[r], preferred_element_type=f32)
        t = _sigmoid(g_ref[...].astype(f32)) * br
        acc = t if acc is None else acc + t
    o_ref[...] = acc.astype(o_ref.dtype)


def _merge(ya, yb, yc, w_br, l, h):
    m, kw = ya.shape
    tm, tn = 1024, 512
    npb = D_MODEL // tn
    yspec = pl.BlockSpec((tm, kw), lambda j, i: (i, 0))

    def gspec(r):
        return pl.BlockSpec((tm, tn), lambda j, i, _r=r: (i, _r * npb + j))

    return pl.pallas_call(
        _merge_kernel,
        grid=(npb, m // tm),
        in_specs=[yspec, yspec, yspec,
                  pl.BlockSpec((None, N_BRANCH, kw, tn), lambda j, i: (l, 0, 0, j)),
                  gspec(0), gspec(1), gspec(2)],
        out_specs=pl.BlockSpec((tm, tn), lambda j, i: (i, j)),
        out_shape=jax.ShapeDtypeStruct((m, D_MODEL), bf16),
        scratch_shapes=[pltpu.VMEM((N_BRANCH, kw, tn), bf16)],
        compiler_params=_cparams(("parallel", "arbitrary")),
    )(ya, yb, yc, w_br, h, h, h)


def _mm_res_ln_kernel(a_ref, w_ref, x_ref, gate_ref, lg_ref, lb_ref, sc_ref, sh_ref, xo_ref, *u_refs,
                      alpha, sub):
    tm = a_ref.shape[0]
    for r in range(tm // sub):
        rows = slice(r * sub, (r + 1) * sub)
        y = jnp.dot(a_ref[rows, :], w_ref[...], preferred_element_type=f32)
        z = alpha * x_ref[rows, :] + gate_ref[0] * y
        xn = _ln_rows(z, 1e-5) * lg_ref[...] + lb_ref[...]
        xo_ref[rows, :] = xn
        if u_refs:
            u_refs[0][rows, :] = (_ln_rows(xn, 1e-5) * (1.0 + sc_ref[0]) + sh_ref[0]).astype(bf16)


def _mm_res_ln(a, w, l, x2, gate, ln_g, ln_b, sc, sh, L, alpha, emit_u):
    m, kdim = a.shape
    d = w.shape[2]
    tm = _RES_LN_TM if kdim * d * 2 <= 16 * 1024 * 1024 else _RES_LN_SUB
    per_b = L // tm
    bspec = pl.BlockSpec((1, 1, d), lambda i: (i // per_b, 0, 0))
    vspec = pl.BlockSpec((1, d), lambda i: (0, 0))
    rspec = pl.BlockSpec((tm, d), lambda i: (i, 0))
    out_shape = [jax.ShapeDtypeStruct((m, d), f32)]
    out_specs = [rspec]
    if emit_u:
        out_shape.append(jax.ShapeDtypeStruct((m, d), bf16))
        out_specs.append(rspec)
    res = pl.pallas_call(
        functools.partial(_mm_res_ln_kernel, alpha=alpha, sub=_RES_LN_SUB),
        grid=(m // tm,),
        in_specs=[pl.BlockSpec((tm, kdim), lambda i: (i, 0)),
                  pl.BlockSpec((None, kdim, d), lambda i: (l, 0, 0), pipeline_mode=pl.Buffered(1)),
                  rspec, bspec, vspec, vspec, bspec, bspec],
        out_specs=out_specs,
        out_shape=out_shape,
        compiler_params=_cparams(("parallel",), 60 * 1024 * 1024),
    )(a, w, x2, gate, ln_g.reshape(1, d), ln_b.reshape(1, d), sc, sh)
    return res if emit_u else (res[0], None)


def _ffn_in_kernel(a_ref, wg_ref, wu_ref, o_ref, wgb_ref, wub_ref):
    @pl.when(pl.program_id(1) == 0)
    def _():
        wgb_ref[...] = wg_ref[...].astype(bf16)
        wub_ref[...] = wu_ref[...].astype(bf16)

    a = a_ref[...]
    g = jnp.dot(a, wgb_ref[...], preferred_element_type=f32)
    u = jnp.dot(a, wub_ref[...], preferred_element_type=f32)
    o_ref[...] = (g * _sigmoid(g) * u).astype(o_ref.dtype)


def _ffn_in(a, w, l):
    m, k = a.shape
    tm, tn = 1024, 512
    nb = D_FF // tn
    return pl.pallas_call(
        _ffn_in_kernel,
        grid=(nb, m // tm),
        in_specs=[pl.BlockSpec((tm, k), lambda j, i: (i, 0)),
                  pl.BlockSpec((None, k, tn), lambda j, i: (l, 0, j)),
                  pl.BlockSpec((None, k, tn), lambda j, i: (l, 0, nb + j))],
        out_specs=pl.BlockSpec((tm, tn), lambda j, i: (i, j)),
        out_shape=jax.ShapeDtypeStruct((m, D_FF), bf16),
        scratch_shapes=[pltpu.VMEM((k, tn), bf16), pltpu.VMEM((k, tn), bf16)],
        compiler_params=_cparams(("parallel", "arbitrary")),
    )(a, w, w)


def _rope_tables(L, d, mult=1.0):
    r = d // 4
    half = r // 2
    inv = ROPE_THETA ** (-(jnp.arange(half, dtype=f32) * 2.0) / r)
    ang = jnp.arange(L).astype(f32)[:, None] * inv[None, :]
    cos, sin = jnp.cos(ang), jnp.sin(ang)
    c = jnp.concatenate([cos, cos, jnp.ones((L, d - r), f32)], axis=1)
    s = jnp.concatenate([-sin, sin, jnp.zeros((L, d - r), f32)], axis=1)
    rep = 128 // d
    return tuple(jnp.tile(t * mult, (1, rep)) for t in (c, s))


def _pack_w_in_kernel(w_ref, o_ref, ov_ref):
    def cp(src, width, dst):
        for o in range(0, width, 1024):
            wd = min(1024, width - o)
            o_ref[:, dst + o:dst + o + wd] = w_ref[src + o:src + o + wd, :].T.astype(bf16)

    def bkv(i, kv, g):
        return _O_BKV + ((i * 2 + kv) * B_KV_GROUPS + g) * HEAD_DIM

    cp(_O_GL, N_BRANCH * D_MODEL, _P_GL)
    for src, dst in ((_O_AQ, _P_AQ), (_O_IQ, _P_IQ), (_O_BQ, _P_BQ), (_O_CQ, _P_CQ), (_O_CK, _P_CK)):
        cp(src, 1024, dst)
    for i in range(3):
        for g in range(B_KV_GROUPS):
            cp(bkv(i, 0, g), HEAD_DIM, _P_BK + (i * B_KV_GROUPS + g) * HEAD_DIM)
    for g in range(B_KV_GROUPS):
        cp(bkv(0, 1, g), HEAD_DIM, _P_BVC + g * HEAD_DIM)
    cp(_O_ALAT, A_LATENT, _P_ALAT)
    ov_ref[_PV_CV:_PV_CV + 1024, :] = w_ref[_O_CV:_O_CV + 1024, :].astype(bf16)
    for i in (1, 2):
        for g in range(B_KV_GROUPS):
            dst = _PV_BV + ((i - 1) * B_KV_GROUPS + g) * HEAD_DIM
            ov_ref[dst:dst + HEAD_DIM, :] = w_ref[bkv(i, 1, g):bkv(i, 1, g) + HEAD_DIM, :].astype(bf16)
    lane = lax.broadcasted_iota(i32, (w_ref.shape[1], 128), 1)
    assert _O_IW == _O_IK + IDX_DIM and _O_IK % 8 == 0 and _O_BG % 8 == 0
    blk = w_ref[_O_IK:_O_IK + 128, :].T
    o_ref[:, _P_IKW:_P_IKW + 128] = jnp.where(lane < IDX_DIM + IDX_HEADS, blk, 0.0).astype(bf16)
    blk = w_ref[_O_BG:_O_BG + 128, :].T
    o_ref[:, _P_BG:_P_BG + 128] = jnp.where(lane < 3 * B_HEADS, blk, 0.0).astype(bf16)


def _pack_w_in(w_in):
    depth, k, n = w_in.shape
    w_t = jnp.swapaxes(w_in, 1, 2)
    tc = 128
    return pl.pallas_call(
        _pack_w_in_kernel,
        grid=(depth, k // tc),
        in_specs=[pl.BlockSpec((None, n, tc), lambda l, i: (l, 0, i))],
        out_specs=[pl.BlockSpec((None, tc, _P_TOT), lambda l, i: (l, i, 0)),
                   pl.BlockSpec((None, _PV_TOT, tc), lambda l, i: (l, 0, i))],
        out_shape=[jax.ShapeDtypeStruct((depth, k, _P_TOT), bf16),
                   jax.ShapeDtypeStruct((depth, _PV_TOT, k), bf16)],
        compiler_params=_cparams(("parallel", "parallel")),
    )(w_t)


def _cover_t(L):
    n_cmp_pad = L // CMP_STRIDE
    starts = np.arange(n_cmp_pad) * CMP_STRIDE
    slc_start = np.arange(L // SLC_LEN) * SLC_LEN
    cover = ((starts[:, None] < slc_start[None, :] + SLC_LEN)
             & (starts[:, None] + CMP_LEN > slc_start[None, :])).astype(np.float32)
    n_cmp = (L - CMP_LEN) // CMP_STRIDE + 1
    cover[n_cmp:] = 0.0
    return jnp.asarray(cover.T)


def _token_mixing(u, h, B, L, l, lw, tabs, cov_t, lam_init):
    M = B * L
    G = B_KV_GROUPS
    aq, iq, bq, cq, ck, bk, alat_n, ikw, gates = _prep(h, lw['a_lat_g'], tabs, L)

    aw = A_HEADS * HEAD_DIM
    ak = _mm_rope(alat_n, lw['a_up'], l, aw, tabs['plain'], L, 512, aw)
    av_tc = _proj_t(alat_n, lw['a_up_vt'], l, 0, A_HEADS, HEAD_DIM, 512, B, L)
    ik = ikw[:, :IDX_DIM].astype(bf16).reshape(B, L, IDX_DIM)
    iw_t = ikw[:, IDX_DIM:IDX_DIM + IDX_HEADS].reshape(B, L, IDX_HEADS).transpose(0, 2, 1)
    mask_t = _dsa_mask(ik, iq.reshape(B, L, -1), iw_t)
    ya = _dsa_attn(aq.reshape(B, L, -1), ak.reshape(B, L, -1), av_tc, mask_t)

    bvc = h[:, _P_BVC:_P_BVC + G * HEAD_DIM].reshape(B, L, G, HEAD_DIM)
    bkc = bk[:, :G * HEAD_DIM].reshape(B, L, G, HEAD_DIM)
    n_row = L // CMP_STRIDE
    xcmp = jnp.stack([bkc, bvc], axis=1)
    xcmp = xcmp.transpose(0, 1, 3, 2, 4).reshape(B, 2, G, n_row, CMP_STRIDE * HEAD_DIM)
    cmp_out = _nsa_compress(xcmp, lw['cmp_w1'], lw['cmp_w2'], lw['cmp_pe'], l)
    kc = cmp_out[:, 0]
    vc_t = cmp_out[:, 1].transpose(0, 1, 3, 2)
    bv_tc = _proj_t(u, lw['w_vt'], l, _PV_BV, 2 * G, HEAD_DIM, 512, B, L)
    gates_t = gates[:, :3 * B_HEADS].reshape(B, L, G, 3 * B_HPG).transpose(0, 2, 3, 1)
    yb = _nsa_attn(bq.reshape(B, L, -1), kc, vc_t, bk.reshape(B, L, -1), bv_tc, gates_t, cov_t)

    cv_tc = _proj_t(u, lw['w_vt'], l, _PV_CV, C_HEADS, 2 * C_DIM, _DIFF_TK, B, L)
    yc = _diff_attn(cq.reshape(B, L, -1), ck.reshape(B, L, -1), cv_tc, lw['lam'],
                    lw['c_subln_g'].reshape(2 * C_DIM, 1), lam_init)

    return _merge(ya.reshape(M, -1), yb.reshape(M, -1), yc.reshape(M, -1), lw['w_br'], l, h)


def kernel(x, c, w_ada, b_ada, w_in, a_lat_g, a_up, cmp_w1, cmp_w2, cmp_pe, lam, c_subln_g, w_br, w_o,
           w_ffn_in, w_ffn_out, ln_g, ln_b):
    B, L, D = x.shape
    depth = w_ada.shape[0]
    M = B * L
    alpha = (2 * depth) ** 0.25

    c_pad = jnp.zeros((8, D), f32).at[:B].set(c)
    mod = _ada(c_pad, w_ada, b_ada)[:, :B]
    mods = [[mod[l, :, i * D:(i + 1) * D].reshape(B, 1, D) for i in range(6)] for l in range(depth)]

    assert HEAD_DIM == C_DIM
    tabs = dict(plain=_rope_tables(L, HEAD_DIM), query=_rope_tables(L, HEAD_DIM, HEAD_DIM ** -0.5 * _LOG2E),
                idx=_rope_tables(L, IDX_DIM))
    cov_t = _cover_t(L)

    w_in_p, w_vt = _pack_w_in(w_in)
    pe_flat = jnp.zeros((depth, 2, 8, CMP_LEN * HEAD_DIM), f32).at[:, :, 0].set(
        cmp_pe.reshape(depth, 2, CMP_LEN * HEAD_DIM)).astype(bf16)
    a_up_vt = jnp.swapaxes(a_up[:, :, A_HEADS * HEAD_DIM:], 1, 2).astype(bf16)
    wb = dict(a_up=a_up.astype(bf16), a_up_vt=a_up_vt, w_vt=w_vt, cmp_w1=cmp_w1.astype(bf16),
              cmp_w2=cmp_w2.astype(bf16), cmp_pe=pe_flat, w_br=w_br)
    w_o_b = w_o.astype(bf16)
    w_fo_b = w_ffn_out.astype(bf16)

    x2 = x.reshape(M, D)
    u = _lnmod(x2, mods[0][1], mods[0][0], L)
    for l in range(depth):
        lam_init = 0.8 - 0.6 * math.exp(-0.3 * l)
        sh_a, sc_a, g_a, sh_f, sc_f, g_f = mods[l]
        lw = dict(wb, a_lat_g=a_lat_g[l], lam=lam[l], c_subln_g=c_subln_g[l])
        h = _mm(u, w_in_p, l, 1024, 768, bf16)
        merged = _token_mixing(u, h, B, L, l, lw, tabs, cov_t, lam_init)
        x2, u = _mm_res_ln(merged, w_o_b, l, x2, g_a, ln_g[l, 0], ln_b[l, 0], sc_f, sh_f, L, alpha, True)
        f = _ffn_in(u, w_ffn_in, l)
        last = l == depth - 1
        nsc, nsh = (sc_f, sh_f) if last else (mods[l + 1][1], mods[l + 1][0])
        x2, u = _mm_res_ln(f, w_fo_b, l, x2, g_f, ln_g[l, 1], ln_b[l, 1], nsc, nsh, L, alpha, not last)
    return x2.reshape(B, L, D)
```

```python
import functools
import math

import numpy as np
import jax
import jax.numpy as jnp
from jax import lax
from jax.experimental import pallas as pl
from jax.experimental.pallas import tpu as pltpu

f32 = jnp.float32
bf16 = jnp.bfloat16
i32 = jnp.int32

D_MODEL = 2048
HEAD_DIM = 128
ROPE_THETA = 500000.0
NEG = -1e30
FORCE = 1e6
A_HEADS = 8
A_LATENT = 512
IDX_HEADS = 16
IDX_DIM = 64
DSA_TOPK = 256
B_HEADS = 8
B_KV_GROUPS = 2
B_HPG = B_HEADS // B_KV_GROUPS
CMP_LEN = 32
CMP_STRIDE = 16
SLC_LEN = 64
SLC_TOPN = 16
WIN_LEN = 512
C_HEADS = 4
C_DIM = 128
BRANCH_W = A_HEADS * HEAD_DIM
N_BRANCH = 3
D_FF = int(math.ceil(8 * D_MODEL / 3 / 256)) * 256

_O_AQ = 0
_O_ALAT = _O_AQ + A_HEADS * HEAD_DIM
_O_IQ = _O_ALAT + A_LATENT
_O_IK = _O_IQ + IDX_HEADS * IDX_DIM
_O_IW = _O_IK + IDX_DIM
_O_BQ = _O_IW + IDX_HEADS
_O_BKV = _O_BQ + B_HEADS * HEAD_DIM
_O_BG = _O_BKV + 3 * 2 * B_KV_GROUPS * HEAD_DIM
_O_CQ = _O_BG + 3 * B_HEADS
_O_CK = _O_CQ + C_HEADS * 2 * C_DIM
_O_CV = _O_CK + C_HEADS * 2 * C_DIM
_O_GL = _O_CV + C_HEADS * 2 * C_DIM
_N_IN = _O_GL + N_BRANCH * D_MODEL

_P_GL = 0
_P_BK = 6144
_P_BVC = 6912
_P_AQ = 7168
_P_IQ = 8192
_P_BQ = 9216
_P_CQ = 10240
_P_CK = 11264
_P_ALAT = 12288
_P_IKW = 12800
_P_BG = 12928
_P_TOT = 13056
_PV_CV = 0
_PV_BV = 1024
_PV_TOT = 1536

_VMEM_LIMIT = 48 * 1024 * 1024
_LOG2E = 1.4426950408889634
_INF = float("inf")
_ONES = 16
_RES_LN_TM = 512
_RES_LN_SUB = 256
_DIFF_TK = 1024
_DIFF_TQ = 512
_HEAD_GROUP = 2


def _cparams(sem, vmem=_VMEM_LIMIT):
    return pltpu.CompilerParams(dimension_semantics=sem, vmem_limit_bytes=vmem)


def _sigmoid(x):
    return 1.0 / (1.0 + jnp.exp(-x))


def _dot_nt(a, b):
    return lax.dot_general(a, b, (((1,), (1,)), ((), ())), preferred_element_type=f32)


def _ada_kernel(c_ref, w_ref, b_ref, o_ref):
    c = c_ref[...]
    cs = c * _sigmoid(c)
    o_ref[0] = jnp.dot(cs, w_ref[0], preferred_element_type=f32,
                       precision=lax.Precision.HIGHEST) + b_ref[0]


def _ada(c_pad, w_ada, b_ada):
    depth, d, n = w_ada.shape
    tn = 512
    return pl.pallas_call(
        _ada_kernel,
        grid=(depth, n // tn),
        in_specs=[pl.BlockSpec((8, d), lambda l, j: (0, 0)),
                  pl.BlockSpec((1, d, tn), lambda l, j: (l, 0, j)),
                  pl.BlockSpec((1, 1, tn), lambda l, j: (l, 0, j))],
        out_specs=pl.BlockSpec((1, 8, tn), lambda l, j: (l, 0, j)),
        out_shape=jax.ShapeDtypeStruct((depth, 8, n), f32),
        compiler_params=_cparams(("parallel", "parallel")),
    )(c_pad, w_ada, b_ada.reshape(depth, 1, n))


def _ln_rows(x, eps):
    mu = jnp.mean(x, axis=-1, keepdims=True)
    d = x - mu
    var = jnp.mean(d * d, axis=-1, keepdims=True)
    return d * lax.rsqrt(var + eps)


def _lnmod_kernel(x_ref, sc_ref, sh_ref, o_ref):
    y = _ln_rows(x_ref[...], 1e-5)
    o_ref[...] = (y * (1.0 + sc_ref[0]) + sh_ref[0]).astype(o_ref.dtype)


def _lnmod(x2, sc, sh, L):
    m, d = x2.shape
    tm = 512
    per_b = L // tm
    return pl.pallas_call(
        _lnmod_kernel,
        grid=(m // tm,),
        in_specs=[pl.BlockSpec((tm, d), lambda i: (i, 0)),
                  pl.BlockSpec((1, 1, d), lambda i: (i // per_b, 0, 0)),
                  pl.BlockSpec((1, 1, d), lambda i: (i // per_b, 0, 0))],
        out_specs=pl.BlockSpec((tm, d), lambda i: (i, 0)),
        out_shape=jax.ShapeDtypeStruct((m, d), bf16),
        compiler_params=_cparams(("parallel",)),
    )(x2, sc, sh)


def _mm_kernel(a_ref, w_ref, o_ref):
    o_ref[...] = jnp.dot(a_ref[...], w_ref[...], preferred_element_type=f32).astype(o_ref.dtype)


def _mm(a, w, l, tm, tn, out_dtype):
    m, k = a.shape
    n = w.shape[2]
    return pl.pallas_call(
        _mm_kernel,
        grid=(n // tn, m // tm),
        in_specs=[pl.BlockSpec((tm, k), lambda j, i: (i, 0)),
                  pl.BlockSpec((None, k, tn), lambda j, i: (l, 0, j))],
        out_specs=pl.BlockSpec((tm, tn), lambda j, i: (i, j)),
        out_shape=jax.ShapeDtypeStruct((m, n), out_dtype),
        compiler_params=_cparams(("parallel", "parallel")),
    )(a, w)


def _proj_t_kernel(w_ref, a_ref, o_ref):
    n, d = o_ref.shape[1], o_ref.shape[3] - _ONES
    tk = o_ref.shape[4]
    res = _dot_nt(w_ref[...], a_ref[...])
    for i in range(n):
        o_ref[0, i, 0, :d, :] = res[i * d:(i + 1) * d, :].astype(o_ref.dtype)
        o_ref[0, i, 0, d:, :] = jnp.ones((_ONES, tk), o_ref.dtype)


def _proj_t(a, w_t, l, row0, n, d, tk, B, L):
    k = a.shape[1]
    rows = n * d
    nc = L // tk
    return pl.pallas_call(
        _proj_t_kernel,
        grid=(B, nc),
        in_specs=[pl.BlockSpec((None, rows, k), lambda b, c: (l, row0 // rows, 0)),
                  pl.BlockSpec((tk, k), lambda b, c: (b * nc + c, 0))],
        out_specs=pl.BlockSpec((1, n, 1, d + _ONES, tk), lambda b, c: (b, 0, c, 0, 0)),
        out_shape=jax.ShapeDtypeStruct((B, n, nc, d + _ONES, tk), bf16),
        compiler_params=_cparams(("parallel", "parallel")),
    )(w_t, a)


def _rope_low(shape, d, half):
    lane = lax.broadcasted_iota(i32, shape, 1)
    return (lane & (d - 1)) < half


def _rope_heads(x, c, s, low, half):
    partner = jnp.where(low, pltpu.roll(x, 128 - half, 1), pltpu.roll(x, half, 1))
    return x * c + partner * s


def _mm_rope_kernel(a_ref, w_ref, c_ref, s_ref, o_ref):
    acc = jnp.dot(a_ref[...], w_ref[...], preferred_element_type=f32)
    c, s = c_ref[...], s_ref[...]
    low = _rope_low(c.shape, HEAD_DIM, HEAD_DIM // 8)
    for h in range(acc.shape[1] // 128):
        sl = slice(h * 128, (h + 1) * 128)
        o_ref[:, sl] = _rope_heads(acc[:, sl], c, s, low, HEAD_DIM // 8).astype(o_ref.dtype)


def _mm_rope(a, w, l, n, tabs, L, tm, tn):
    m, k = a.shape
    per_b = L // tm
    tspec = pl.BlockSpec((tm, 128), lambda j, i: (i % per_b, 0))
    return pl.pallas_call(
        _mm_rope_kernel,
        grid=(n // tn, m // tm),
        in_specs=[pl.BlockSpec((tm, k), lambda j, i: (i, 0)),
                  pl.BlockSpec((None, k, tn), lambda j, i: (l, 0, j)),
                  tspec, tspec],
        out_specs=pl.BlockSpec((tm, tn), lambda j, i: (i, j)),
        out_shape=jax.ShapeDtypeStruct((m, n), bf16),
        compiler_params=_cparams(("parallel", "parallel")),
    )(a, w, *tabs)


def _prep_kernel(aq_ref, iq_ref, bq_ref, cq_ref, ck_ref, bk_ref, alat_ref, ikw_ref, bg_ref, alg_ref,
                 c_ref, s_ref, cq_ref_t, sq_ref_t, c6_ref, s6_ref, p128_ref, p64_ref,
                 aq_o, iq_o, bq_o, cq_o, ck_o, bk_o, alat_o, ikw_o, g_o):
    rows = 64

    def rope_all(src, dst, c_t, s_t, p_ref):
        for blk in range(src.shape[1] // 256):
            cols = slice(blk * 256, (blk + 1) * 256)
            partner = jnp.dot(src[:, cols], p_ref[...], preferred_element_type=f32)
            for r in range(src.shape[0] // rows):
                rs = slice(r * rows, (r + 1) * rows)
                c, s = c_t[rs, :], s_t[rs, :]
                for half in range(2):
                    sl = slice(blk * 256 + half * 128, blk * 256 + (half + 1) * 128)
                    y = src[rs, sl].astype(f32) * c + partner[rs, half * 128:(half + 1) * 128] * s
                    dst[rs, sl] = y.astype(dst.dtype)

    rope_all(aq_ref, aq_o, cq_ref_t, sq_ref_t, p128_ref)
    rope_all(bq_ref, bq_o, cq_ref_t, sq_ref_t, p128_ref)
    rope_all(cq_ref, cq_o, cq_ref_t, sq_ref_t, p128_ref)
    rope_all(ck_ref, ck_o, c_ref, s_ref, p128_ref)
    rope_all(bk_ref, bk_o, c_ref, s_ref, p128_ref)
    rope_all(iq_ref, iq_o, c6_ref, s6_ref, p64_ref)

    idx = (c6_ref, s6_ref, _rope_low((rows, 128), IDX_DIM, IDX_DIM // 8), IDX_DIM // 8)

    def rope(x, tabs, rs):
        return _rope_heads(x, tabs[0][rs, :], tabs[1][rs, :], tabs[2], tabs[3])

    a = alat_ref[...].astype(f32)
    ms = jnp.mean(a * a, axis=-1, keepdims=True)
    alat_o[...] = (a * lax.rsqrt(ms + 1e-6) * alg_ref[...]).astype(alat_o.dtype)

    isk = lax.broadcasted_iota(i32, (rows, 128), 1) < IDX_DIM
    for r in range(ikw_ref.shape[0] // rows):
        rs = slice(r * rows, (r + 1) * rows)
        x = ikw_ref[rs, :].astype(f32)
        mu = jnp.sum(jnp.where(isk, x, 0.0), axis=-1, keepdims=True) * (1.0 / IDX_DIM)
        d = jnp.where(isk, x - mu, 0.0)
        var = jnp.sum(d * d, axis=-1, keepdims=True) * (1.0 / IDX_DIM)
        y = d * lax.rsqrt(var + 1e-5)
        ikw_o[rs, :] = jnp.where(isk, rope(y, idx, rs), x * (IDX_DIM ** -0.5 * IDX_HEADS ** -0.5))

    g_o[...] = _sigmoid(bg_ref[...].astype(f32))


def _prep(h, a_lat_g, tabs, L):
    m = h.shape[0]
    tm = 512
    per_b = L // tm

    def hs(width, off):
        return pl.BlockSpec((tm, width), lambda i, _o=off // width: (i, _o))

    tspec = pl.BlockSpec((tm, 128), lambda i: (i % per_b, 0))
    pspec = pl.BlockSpec((256, 256), lambda i: (0, 0))

    def os(width):
        return pl.BlockSpec((tm, width), lambda i: (i, 0))

    outs = [(1024, bf16)] * 5 + [(768, bf16), (512, bf16), (128, f32), (128, f32)]
    return pl.pallas_call(
        _prep_kernel,
        grid=(m // tm,),
        in_specs=[hs(1024, _P_AQ), hs(1024, _P_IQ), hs(1024, _P_BQ), hs(1024, _P_CQ), hs(1024, _P_CK),
                  hs(768, _P_BK), hs(512, _P_ALAT), hs(128, _P_IKW), hs(128, _P_BG),
                  pl.BlockSpec((1, A_LATENT), lambda i: (0, 0))] + [tspec] * 6 + [pspec, pspec],
        out_specs=[os(w) for w, _ in outs],
        out_shape=[jax.ShapeDtypeStruct((m, w), dt) for w, dt in outs],
        compiler_params=_cparams(("parallel",)),
    )(h, h, h, h, h, h, h, h, h, a_lat_g.reshape(1, A_LATENT), *tabs['plain'], *tabs['query'], *tabs['idx'],
      _rope_perm(HEAD_DIM), _rope_perm(IDX_DIM))


def _f32_order_key(x):
    b = int(np.float32(x).view(np.int32))
    return b ^ ((b >> 31) & 0x7FFFFFFF)


_KEY_NEG = _f32_order_key(NEG)


def _dsa_mask_kernel(ik_ref, iq_ref, iw_ref, o_ref, key_ref, qp_ref, j_ref, *, L, tq, ksel):
    j = pl.program_id(1)
    ck = 512
    nck = L // ck
    nc = ((j + 1) * tq + ck - 1) // ck
    n_out_i = L - nc * ck
    n_out = n_out_i.astype(f32)
    t_row = j * tq + lax.broadcasted_iota(i32, (1, tq), 1)

    for hp in range(IDX_HEADS // 2):
        for e in range(2):
            h = 2 * hp + e
            qp_ref[hp, e * tq:(e + 1) * tq, :] = iq_ref[0, :, h * IDX_DIM:(h + 1) * IDX_DIM]

    def score_chunk(c, carry):
        off = pl.multiple_of(c * ck, ck)
        ikc = ik_ref[0, pl.ds(off, ck), :]
        acc = jnp.zeros((ck, tq), f32)
        for hp in range(IDX_HEADS // 2):
            s2 = _dot_nt(ikc, qp_ref[hp])
            acc = acc + jnp.maximum(s2[:, :tq], 0.0) * iw_ref[0, 2 * hp:2 * hp + 1, :]
            acc = acc + jnp.maximum(s2[:, tq:], 0.0) * iw_ref[0, 2 * hp + 1:2 * hp + 2, :]
        acc = jnp.where(acc == 0.0, 0.0, acc)
        sidx = off + lax.broadcasted_iota(i32, (ck, tq), 0)
        key_ref[pl.ds(off, ck), :] = jnp.where(sidx <= t_row, acc, NEG)
        return carry

    def score_pair(p, carry):
        score_chunk(2 * p, carry)
        return score_chunk(2 * p + 1, carry)

    lax.fori_loop(0, nc // 2, score_pair, 0)

    @pl.when(nc % 2 == 1)
    def _():
        score_chunk(nc - 1, 0)

    def as_f32(key):
        return lax.bitcast_convert_type(key ^ ((key >> 31) & 0x7FFFFFFF), f32)

    def count(pred_fn):
        def body(c, cnt):
            off = pl.multiple_of(c * ck, ck)
            k = key_ref[pl.ds(off, ck), :]
            sidx = off + lax.broadcasted_iota(i32, (ck, tq), 0)
            return cnt + jnp.sum(pred_fn(k, sidx).reshape(ck // 64, 64, tq), axis=0)
        part = lax.fori_loop(0, nc, body, jnp.zeros((64, tq), f32))
        return jnp.sum(part, axis=0, keepdims=True)

    kf = float(ksel)

    def bit_body(i, carry):
        thr, cnt_thr = carry
        cand = thr + lax.shift_left(jnp.int32(1), 31 - i)
        cand_f = as_f32(cand)
        cnt = count(lambda k, s: jnp.where(k >= cand_f, 1.0, 0.0)) + jnp.where(_KEY_NEG >= cand, n_out, 0.0)
        ok = cnt >= kf
        return jnp.where(ok, cand, thr), jnp.where(ok, cnt, cnt_thr)

    thr_key, cnt_ge = lax.fori_loop(0, 32, bit_body, (jnp.full((1, tq), -2 ** 31, i32),
                                                       jnp.full((1, tq), float(L), f32)))
    thr = as_f32(thr_key)

    j_ref[...] = jnp.full((1, tq), L, i32)

    @pl.when(jnp.max(cnt_ge) > kf)
    def _():
        nbits = L.bit_length() - 1
        cnt_gt = count(lambda k, s: jnp.where(k > thr, 1.0, 0.0)) + jnp.where(_KEY_NEG > thr_key, n_out, 0.0)
        need = kf - cnt_gt

        def jbit(i, cur):
            cand = cur | lax.shift_left(jnp.int32(1), nbits - 1 - i)
            f = count(lambda k, s: jnp.where(k == thr, jnp.where(s < cand, 1.0, 0.0), 0.0))
            f = f + jnp.where(thr_key == _KEY_NEG, jnp.clip(cand - nc * ck, 0, n_out_i).astype(f32), 0.0)
            return jnp.where(f < need, cand, cur)

        j_ref[...] = lax.fori_loop(0, nbits, jbit, jnp.zeros((1, tq), i32))

    jlast = j_ref[...]

    def write(c, carry):
        off = pl.multiple_of(c * ck, ck)
        k = key_ref[pl.ds(off, ck), :]
        sidx = off + lax.broadcasted_iota(i32, (ck, tq), 0)
        sel = jnp.where(k > thr, _INF, jnp.where(k == thr, jnp.where(sidx <= jlast, _INF, NEG), NEG))
        o_ref[0, pl.ds(off, ck), :] = jnp.where(sidx <= t_row, sel, NEG)
        return carry

    lax.fori_loop(0, nc, write, 0)

    def write_rest(c, carry):
        off = pl.multiple_of(c * ck, ck)
        o_ref[0, pl.ds(off, ck), :] = jnp.full((ck, tq), NEG, f32)
        return carry

    lax.fori_loop(nc, nck, write_rest, 0)


def _dsa_mask(ik, iq, iw_t):
    B, L, _ = iq.shape
    tq = 128
    ksel = min(DSA_TOPK, L // 4)
    return pl.pallas_call(
        functools.partial(_dsa_mask_kernel, L=L, tq=tq, ksel=ksel),
        grid=(B, L // tq),
        in_specs=[pl.BlockSpec((1, L, IDX_DIM), lambda b, j: (b, 0, 0)),
                  pl.BlockSpec((1, tq, IDX_HEADS * IDX_DIM), lambda b, j: (b, j, 0)),
                  pl.BlockSpec((1, IDX_HEADS, tq), lambda b, j: (b, 0, j))],
        out_specs=pl.BlockSpec((1, L, tq), lambda b, j: (b, 0, j)),
        out_shape=jax.ShapeDtypeStruct((B, L, L), f32),
        scratch_shapes=[pltpu.VMEM((L, tq), f32), pltpu.VMEM((IDX_HEADS // 2, 2 * tq, IDX_DIM), bf16),
                        pltpu.VMEM((1, tq), i32)],
        compiler_params=_cparams(("parallel", "parallel")),
    )(ik, iq, iw_t)


def _col_reduce(x, op):
    r, c = x.shape
    if r > 64:
        x = op(x.reshape(r // 64, 64, c), axis=0)
    return op(x, axis=0, keepdims=True)


def _flash_group(ss, cap, v_ts, m_refs, acc_refs):
    ss, mloc = _flash_mask_max(ss, cap)
    _flash_update(ss, mloc, cap is not None, v_ts, m_refs, acc_refs)


def _flash_mask_max(ss, cap):
    if cap is not None:
        ss = [jnp.minimum(s, cap) for s in ss]
    return ss, [_col_reduce(s, jnp.max) for s in ss]


def _flash_update(ss, mloc, masked, v_ts, m_refs, acc_refs):
    n = len(ss)
    m_prev = [r[...] for r in m_refs]
    m_new = [jnp.maximum(m_prev[i], mloc[i]) for i in range(n)]
    alpha = [jnp.exp2(m_prev[i] - m_new[i]) for i in range(n)]
    ps = [jnp.exp2((ss[i] - m_new[i]).astype(bf16)) for i in range(n)]
    pv = [jnp.dot(v_ts[i], ps[i], preferred_element_type=f32) for i in range(n)]
    for i in range(n):
        acc_new = acc_refs[i][...] * alpha[i] + pv[i]
        if masked:
            acc_new = jnp.where(m_new[i] <= NEG, 0.0, acc_new)
        acc_refs[i][...] = acc_new
        m_refs[i][...] = m_new[i]


def _flash_heads(score_fns, cap, v_ts, m_ref, acc_ref, group):
    n = len(score_fns)
    groups = [list(range(g, min(g + group, n))) for g in range(0, n, group)]
    nxt = _flash_mask_max([score_fns[h]() for h in groups[0]], cap)
    for gi, hs in enumerate(groups):
        ss, mloc = nxt
        if gi + 1 < len(groups):
            nxt = _flash_mask_max([score_fns[h]() for h in groups[gi + 1]], cap)
        _flash_update(ss, mloc, cap is not None, [v_ts[h] for h in hs], [m_ref.at[h] for h in hs],
                      [acc_ref.at[h] for h in hs])


def _flash_init(m_ref, acc_ref):
    m_ref[...] = jnp.full(m_ref.shape, NEG, f32)
    acc_ref[...] = jnp.zeros(acc_ref.shape, f32)


def _flash_result(acc, d):
    return acc[:d] / jnp.maximum(acc[d:d + 1], 1e-30)


def _dsa_attn_kernel(jm_ref, cm_ref, lm_ref, q_ref, k_ref, vt_ref, mask_ref, o_ref, m_ref, acc_ref):
    s = pl.program_id(1)

    @pl.when(cm_ref[s] == 0)
    def _():
        _flash_init(m_ref, acc_ref)

    cap = mask_ref[0]

    def scores(h):
        sl = slice(h * HEAD_DIM, (h + 1) * HEAD_DIM)
        return lambda: _dot_nt(k_ref[0, :, sl], q_ref[0, :, sl])

    v_ts = [vt_ref[0, h, 0] for h in range(A_HEADS)]
    _flash_heads([scores(h) for h in range(A_HEADS)], cap, v_ts, m_ref, acc_ref, _HEAD_GROUP)

    @pl.when(lm_ref[s] == 1)
    def _():
        for h in range(A_HEADS):
            o = _flash_result(acc_ref[h], HEAD_DIM)
            o_ref[0, :, h * HEAD_DIM:(h + 1) * HEAD_DIM] = o.T.astype(o_ref.dtype)


def _dsa_attn(q, k, v_t, mask_t):
    B, L, W = q.shape
    tq, tk = 512, 512
    pairs = [(j, c) for j in range(L // tq) for c in range(((j + 1) * tq - 1) // tk + 1)]
    jm = jnp.asarray([p[0] for p in pairs], i32)
    cm = jnp.asarray([p[1] for p in pairs], i32)
    lm = jnp.asarray([int(i + 1 == len(pairs) or pairs[i + 1][0] != p[0]) for i, p in enumerate(pairs)], i32)
    return pl.pallas_call(
        _dsa_attn_kernel,
        grid_spec=pltpu.PrefetchScalarGridSpec(
            num_scalar_prefetch=3,
            grid=(B, len(pairs)),
            in_specs=[pl.BlockSpec((1, tq, W), lambda b, s, jm, cm, lm: (b, jm[s], 0)),
                      pl.BlockSpec((1, tk, W), lambda b, s, jm, cm, lm: (b, cm[s], 0)),
                      pl.BlockSpec((1, A_HEADS, 1, HEAD_DIM + _ONES, tk),
                                   lambda b, s, jm, cm, lm: (b, 0, cm[s], 0, 0)),
                      pl.BlockSpec((1, tk, tq), lambda b, s, jm, cm, lm: (b, cm[s], jm[s]))],
            out_specs=pl.BlockSpec((1, tq, W), lambda b, s, jm, cm, lm: (b, jm[s], 0)),
            scratch_shapes=[pltpu.VMEM((A_HEADS, 1, tq), f32),
                            pltpu.VMEM((A_HEADS, HEAD_DIM + _ONES, tq), f32)]),
        out_shape=jax.ShapeDtypeStruct((B, L, W), bf16),
        compiler_params=_cparams(("parallel", "arbitrary")),
    )(jm, cm, lm, q, k, v_t, mask_t)


def _nsa_cmp_kernel(x_ref, w1_ref, w2_ref, pe_ref, o_ref):
    x = x_ref[0, 0, 0]
    w1 = w1_ref[0]
    half = CMP_STRIDE * HEAD_DIM
    a = jnp.dot(x, w1[:half], preferred_element_type=f32)
    b = jnp.dot(x, w1[half:], preferred_element_type=f32)
    pe = jnp.dot(pe_ref[0], w1, preferred_element_type=f32)[0:1]
    n = a.shape[0]
    pre = a + pltpu.roll(b, n - 1, 0) + pe
    act = pre * _sigmoid(pre)
    o_ref[0, 0, 0] = jnp.dot(act.astype(bf16), w2_ref[0], preferred_element_type=f32).astype(o_ref.dtype)


def _nsa_compress(x, w1, w2, pe, l):
    B, _, G, n, wd = x.shape
    return pl.pallas_call(
        _nsa_cmp_kernel,
        grid=(B, 2, G),
        in_specs=[pl.BlockSpec((1, 1, 1, n, wd), lambda b, t, g: (b, t, g, 0, 0)),
                  pl.BlockSpec((None, 1, CMP_LEN * HEAD_DIM, HEAD_DIM), lambda b, t, g: (l, t, 0, 0)),
                  pl.BlockSpec((None, 1, HEAD_DIM, HEAD_DIM), lambda b, t, g: (l, t, 0, 0)),
                  pl.BlockSpec((None, 1, 8, CMP_LEN * HEAD_DIM), lambda b, t, g: (l, t, 0, 0))],
        out_specs=pl.BlockSpec((1, 1, 1, n, HEAD_DIM), lambda b, t, g: (b, t, g, 0, 0)),
        out_shape=jax.ShapeDtypeStruct((B, 2, G, n, HEAD_DIM), bf16),
        compiler_params=_cparams(("parallel", "parallel", "parallel")),
    )(x, w1, w2, pe)


def _nsa_attn_kernel(q_ref, kc_ref, vct_ref, ks_ref, vst_ref, kw_ref, vwt_ref, g_ref, cov_ref, o_ref,
                     m_ref, acc_ref, out_ref, sel_ref, *, L, tq, tk, n_sel):
    j = pl.program_id(2)
    t_row = j * tq + lax.broadcasted_iota(i32, (1, tq), 1)
    n_cmp = kc_ref.shape[2]
    n_slc = L // SLC_LEN
    qs = [q_ref[0, :, hh * HEAD_DIM:(hh + 1) * HEAD_DIM] for hh in range(B_HPG)]

    def gate(hh, i):
        return g_ref[0, 0, hh * 3 + i:hh * 3 + i + 1, :]

    kc = kc_ref[0, 0]
    vct = vct_ref[0, 0]
    cend = lax.broadcasted_iota(i32, (n_cmp, tq), 0) * CMP_STRIDE + (CMP_LEN - 1)
    mc = cend <= t_row
    ss = [jnp.where(mc, _dot_nt(kc, qs[hh]), NEG) for hh in range(B_HPG)]
    mxs = [_col_reduce(s, jnp.max) for s in ss]
    ps = [jnp.where(mc, jnp.exp2(ss[hh] - mxs[hh]), 0.0) for hh in range(B_HPG)]
    inv = [1.0 / jnp.maximum(_col_reduce(p, jnp.sum), 1e-30) for p in ps]
    pvs = [jnp.dot(vct, p.astype(bf16), preferred_element_type=f32) for p in ps]
    psum = ps[0] * inv[0]
    for hh in range(1, B_HPG):
        psum = psum + ps[hh] * inv[hh]
    for hh in range(B_HPG):
        out_ref[hh] = (gate(hh, 0) * inv[hh]) * pvs[hh]
    imp = jnp.dot(cov_ref[...], psum, preferred_element_type=f32, precision=lax.Precision.HIGHEST)
    blk = lax.broadcasted_iota(i32, (n_slc, tq), 0)
    cur = t_row >> (SLC_LEN.bit_length() - 1)
    forced = (blk == 0) | (blk == cur) | (blk == cur - 1)
    imp = jnp.where(forced, FORCE, jnp.where(blk <= cur, imp, NEG))
    rank = jnp.zeros((n_slc, tq), f32)
    for r in range(n_slc):
        row = imp[r:r + 1, :]
        rank = rank + jnp.where(row > imp, 1.0, jnp.where(row == imp, jnp.where(blk > r, 1.0, 0.0), 0.0))
    sel_ref[...] = jnp.where(rank < float(n_sel), _INF, NEG)

    def finish(i):
        for hh in range(B_HPG):
            out_ref[hh] = out_ref[hh] + gate(hh, i) * _flash_result(acc_ref[hh], HEAD_DIM)

    def run_branch(k_ref, vt_ref, c_lo, c_hi, mask_fn):
        _flash_init(m_ref, acc_ref)

        def body(c, carry):
            off = pl.multiple_of(c * tk, tk)
            kch = k_ref[0, pl.ds(off, tk), :]
            vch = vt_ref[0, 0, c]
            sidx = off + lax.broadcasted_iota(i32, (tk, tq), 0)
            cap = mask_fn(c, sidx)
            fns = [functools.partial(_dot_nt, kch, qs[hh]) for hh in range(B_HPG)]
            _flash_heads(fns, cap, [vch] * B_HPG, m_ref, acc_ref, _HEAD_GROUP)
            return carry

        lax.fori_loop(c_lo, c_hi, body, 0)

    bpc = tk // SLC_LEN

    def slc_mask(c, sidx):
        rows = [jnp.broadcast_to(sel_ref[pl.ds(c * bpc + b, 1), :], (SLC_LEN, tq)) for b in range(bpc)]
        return jnp.where(sidx <= t_row, jnp.concatenate(rows, axis=0), NEG)

    c_hi = ((j + 1) * tq + tk - 1) // tk
    run_branch(ks_ref, vst_ref, 0, c_hi, slc_mask)
    finish(1)

    def win_mask(c, sidx):
        d = t_row - sidx
        return jnp.where(d >= 0, jnp.where(d < WIN_LEN, _INF, NEG), NEG)

    c_lo = jnp.maximum(j * tq - (WIN_LEN - 1), 0) // tk
    run_branch(kw_ref, vwt_ref, c_lo, c_hi, win_mask)
    finish(2)

    for hh in range(B_HPG):
        o_ref[0, :, hh * HEAD_DIM:(hh + 1) * HEAD_DIM] = out_ref[hh].T.astype(o_ref.dtype)


def _nsa_attn(bq, kc, vc_t, bk, bv_tc, gates_t, cov_t):
    B, L, _ = bq.shape
    G = B_KV_GROUPS
    tq, tk = 512, 512
    n_cmp = kc.shape[2]
    n_slc = L // SLC_LEN
    n_sel = min(SLC_TOPN, n_slc)
    gw = B_HPG * HEAD_DIM
    nc = L // tk
    return pl.pallas_call(
        functools.partial(_nsa_attn_kernel, L=L, tq=tq, tk=tk, n_sel=n_sel),
        grid=(B, G, L // tq),
        in_specs=[pl.BlockSpec((1, tq, gw), lambda b, g, j: (b, j, g)),
                  pl.BlockSpec((1, 1, n_cmp, HEAD_DIM), lambda b, g, j: (b, g, 0, 0)),
                  pl.BlockSpec((1, 1, HEAD_DIM, n_cmp), lambda b, g, j: (b, g, 0, 0)),
                  pl.BlockSpec((1, L, HEAD_DIM), lambda b, g, j: (b, 0, 2 + g)),
                  pl.BlockSpec((1, 1, nc, HEAD_DIM + _ONES, tk), lambda b, g, j: (b, g, 0, 0, 0)),
                  pl.BlockSpec((1, L, HEAD_DIM), lambda b, g, j: (b, 0, 4 + g)),
                  pl.BlockSpec((1, 1, nc, HEAD_DIM + _ONES, tk),
                               lambda b, g, j: (b, B_KV_GROUPS + g, 0, 0, 0)),
                  pl.BlockSpec((1, 1, 3 * B_HPG, tq), lambda b, g, j: (b, g, 0, j)),
                  pl.BlockSpec((n_slc, n_cmp), lambda b, g, j: (0, 0))],
        out_specs=pl.BlockSpec((1, tq, gw), lambda b, g, j: (b, j, g)),
        out_shape=jax.ShapeDtypeStruct((B, L, B_HEADS * HEAD_DIM), bf16),
        scratch_shapes=[pltpu.VMEM((B_HPG, 1, tq), f32),
                        pltpu.VMEM((B_HPG, HEAD_DIM + _ONES, tq), f32),
                        pltpu.VMEM((B_HPG, HEAD_DIM, tq), f32), pltpu.VMEM((n_slc, tq), f32)],
        compiler_params=_cparams(("parallel", "parallel", "parallel")),
    )(bq, kc, vc_t, bk, bv_tc, bk, bv_tc, gates_t, cov_t)


def _diff_attn_kernel(q_ref, k_ref, vt_ref, lam_ref, g_ref, o_ref, m_ref, acc_ref,
                      *, tq, tk, lam_init):
    j = pl.program_id(2)
    t_row = j * tq + lax.broadcasted_iota(i32, (1, tq), 1)
    qs = [q_ref[0, :, mi * C_DIM:(mi + 1) * C_DIM] for mi in range(2)]
    _flash_init(m_ref, acc_ref)

    def chunk(c, lo, width, masked):
        off = pl.multiple_of(c * tk + lo, tq)
        vch = vt_ref[0, 0, c, :, lo:lo + width]
        cap = None
        if masked:
            sidx = off + lax.broadcasted_iota(i32, (width, tq), 0)
            cap = jnp.where(sidx <= t_row, _INF, NEG)
        ss = [_dot_nt(k_ref[0, pl.ds(off, width), mi * C_DIM:(mi + 1) * C_DIM], qs[mi]) for mi in range(2)]
        _flash_group(ss, cap, [vch] * 2, [m_ref.at[mi] for mi in range(2)],
                     [acc_ref.at[mi] for mi in range(2)])

    def body(c, carry):
        chunk(c, 0, tk, False)
        return carry

    c_diag = (j * tq) // tk
    lax.fori_loop(0, c_diag, body, 0)

    @pl.when(j % 2 == 0)
    def _():
        chunk(c_diag, 0, tq, True)

    @pl.when(j % 2 == 1)
    def _():
        chunk(c_diag, 0, tq, False)
        chunk(c_diag, tq, tq, True)

    lam = lam_ref[...]
    lam_val = (jnp.exp(jnp.sum(lam[0:1] * lam[1:2], axis=1, keepdims=True))
               - jnp.exp(jnp.sum(lam[2:3] * lam[3:4], axis=1, keepdims=True)) + lam_init)
    o = _flash_result(acc_ref[0], 2 * C_DIM) - lam_val * _flash_result(acc_ref[1], 2 * C_DIM)
    ms = jnp.mean(o * o, axis=0, keepdims=True)
    y = o * lax.rsqrt(ms + 1e-6) * g_ref[...] * (1.0 - lam_init)
    o_ref[0] = y.T.astype(o_ref.dtype)


def _diff_attn(cq, ck, cv_tc, lam, g_col, lam_init):
    B, L, _ = cq.shape
    tq, tk = _DIFF_TQ, cv_tc.shape[-1]
    assert tk == 2 * tq
    hw = 2 * C_DIM
    nc = L // tk
    return pl.pallas_call(
        functools.partial(_diff_attn_kernel, tq=tq, tk=tk, lam_init=lam_init),
        grid=(B, C_HEADS, L // tq),
        in_specs=[pl.BlockSpec((1, tq, hw), lambda b, h, j: (b, j, h)),
                  pl.BlockSpec((1, L, hw), lambda b, h, j: (b, 0, h)),
                  pl.BlockSpec((1, 1, nc, hw + _ONES, tk), lambda b, h, j: (b, h, 0, 0, 0)),
                  pl.BlockSpec((4, C_DIM), lambda b, h, j: (0, 0)),
                  pl.BlockSpec((hw, 1), lambda b, h, j: (0, 0))],
        out_specs=pl.BlockSpec((1, tq, hw), lambda b, h, j: (b, j, h)),
        out_shape=jax.ShapeDtypeStruct((B, L, C_HEADS * hw), bf16),
        scratch_shapes=[pltpu.VMEM((2, 1, tq), f32), pltpu.VMEM((2, hw + _ONES, tq), f32)],
        compiler_params=_cparams(("parallel", "parallel", "parallel")),
    )(cq, ck, cv_tc, lam, g_col)


def _merge_kernel(ya_ref, yb_ref, yc_ref, w_ref, g0_ref, g1_ref, g2_ref, o_ref, wb_ref):
    @pl.when(pl.program_id(1) == 0)
    def _():
        wb_ref[...] = w_ref[...].astype(bf16)

    acc = None
    for r, (y_ref, g_ref) in enumerate(((ya_ref, g0_ref), (yb_ref, g1_ref), (yc_ref, g2_ref))):
        br = jnp.dot(y_ref[...], wb_ref[r], preferred_element_type=f32)
        t = _sigmoid(g_ref[...].astype(f32)) * br
        acc = t if acc is None else acc + t
    o_ref[...] = acc.astype(o_ref.dtype)


def _merge(ya, yb, yc, w_br, l, h):
    m, kw = ya.shape
    tm, tn = 1024, 512
    npb = D_MODEL // tn
    yspec = pl.BlockSpec((tm, kw), lambda j, i: (i, 0))

    def gspec(r):
        return pl.BlockSpec((tm, tn), lambda j, i, _r=r: (i, _r * npb + j))

    return pl.pallas_call(
        _merge_kernel,
        grid=(npb, m // tm),
        in_specs=[yspec, yspec, yspec,
                  pl.BlockSpec((None, N_BRANCH, kw, tn), lambda j, i: (l, 0, 0, j)),
                  gspec(0), gspec(1), gspec(2)],
        out_specs=pl.BlockSpec((tm, tn), lambda j, i: (i, j)),
        out_shape=jax.ShapeDtypeStruct((m, D_MODEL), bf16),
        scratch_shapes=[pltpu.VMEM((N_BRANCH, kw, tn), bf16)],
        compiler_params=_cparams(("parallel", "arbitrary")),
    )(ya, yb, yc, w_br, h, h, h)


def _mm_res_ln_kernel(a_ref, w_ref, x_ref, gate_ref, lg_ref, lb_ref, sc_ref, sh_ref, xo_ref, *u_refs,
                      alpha, sub):
    tm = a_ref.shape[0]
    for r in range(tm // sub):
        rows = slice(r * sub, (r + 1) * sub)
        y = jnp.dot(a_ref[rows, :], w_ref[...], preferred_element_type=f32)
        z = alpha * x_ref[rows, :] + gate_ref[0] * y
        xn = _ln_rows(z, 1e-5) * lg_ref[...] + lb_ref[...]
        xo_ref[rows, :] = xn
        if u_refs:
            u_refs[0][rows, :] = (_ln_rows(xn, 1e-5) * (1.0 + sc_ref[0]) + sh_ref[0]).astype(bf16)


def _mm_res_ln(a, w, l, x2, gate, ln_g, ln_b, sc, sh, L, alpha, emit_u):
    m, kdim = a.shape
    d = w.shape[2]
    tm = _RES_LN_TM if kdim * d * 2 <= 16 * 1024 * 1024 else _RES_LN_SUB
    per_b = L // tm
    bspec = pl.BlockSpec((1, 1, d), lambda i: (i // per_b, 0, 0))
    vspec = pl.BlockSpec((1, d), lambda i: (0, 0))
    rspec = pl.BlockSpec((tm, d), lambda i: (i, 0))
    out_shape = [jax.ShapeDtypeStruct((m, d), f32)]
    out_specs = [rspec]
    if emit_u:
        out_shape.append(jax.ShapeDtypeStruct((m, d), bf16))
        out_specs.append(rspec)
    res = pl.pallas_call(
        functools.partial(_mm_res_ln_kernel, alpha=alpha, sub=_RES_LN_SUB),
        grid=(m // tm,),
        in_specs=[pl.BlockSpec((tm, kdim), lambda i: (i, 0)),
                  pl.BlockSpec((None, kdim, d), lambda i: (l, 0, 0), pipeline_mode=pl.Buffered(1)),
                  rspec, bspec, vspec, vspec, bspec, bspec],
        out_specs=out_specs,
        out_shape=out_shape,
        compiler_params=_cparams(("parallel",), 60 * 1024 * 1024),
    )(a, w, x2, gate, ln_g.reshape(1, d), ln_b.reshape(1, d), sc, sh)
    return res if emit_u else (res[0], None)


def _ffn_in_kernel(a_ref, wg_ref, wu_ref, o_ref, wgb_ref, wub_ref):
    @pl.when(pl.program_id(1) == 0)
    def _():
        wgb_ref[...] = wg_ref[...].astype(bf16)
        wub_ref[...] = wu_ref[...].astype(bf16)

    a = a_ref[...]
    g = jnp.dot(a, wgb_ref[...], preferred_element_type=f32)
    u = jnp.dot(a, wub_ref[...], preferred_element_type=f32)
    o_ref[...] = (g * _sigmoid(g) * u).astype(o_ref.dtype)


def _ffn_in(a, w, l):
    m, k = a.shape
    tm, tn = 1024, 512
    nb = D_FF // tn
    return pl.pallas_call(
        _ffn_in_kernel,
        grid=(nb, m // tm),
        in_specs=[pl.BlockSpec((tm, k), lambda j, i: (i, 0)),
                  pl.BlockSpec((None, k, tn), lambda j, i: (l, 0, j)),
                  pl.BlockSpec((None, k, tn), lambda j, i: (l, 0, nb + j))],
        out_specs=pl.BlockSpec((tm, tn), lambda j, i: (i, j)),
        out_shape=jax.ShapeDtypeStruct((m, D_FF), bf16),
        scratch_shapes=[pltpu.VMEM((k, tn), bf16), pltpu.VMEM((k, tn), bf16)],
        compiler_params=_cparams(("parallel", "arbitrary")),
    )(a, w, w)


def _rope_tables(L, d, mult=1.0):
    r = d // 4
    half = r // 2
    inv = ROPE_THETA ** (-(jnp.arange(half, dtype=f32) * 2.0) / r)
    ang = jnp.arange(L).astype(f32)[:, None] * inv[None, :]
    cos, sin = jnp.cos(ang), jnp.sin(ang)
    c = jnp.concatenate([cos, cos, jnp.ones((L, d - r), f32)], axis=1)
    s = jnp.concatenate([-sin, sin, jnp.zeros((L, d - r), f32)], axis=1)
    rep = 128 // d
    return tuple(jnp.tile(t * mult, (1, rep)) for t in (c, s))


def _rope_perm(d):
    half = d // 8
    p = np.zeros((256, 256), np.float32)
    for i in range(256):
        li = i % d
        if li < half:
            p[i + half, i] = 1.0
        elif li < 2 * half:
            p[i - half, i] = 1.0
    return jnp.asarray(p, bf16)


def _pack_w_in_kernel(w_ref, o_ref, ov_ref):
    def cp(src, width, dst):
        for o in range(0, width, 1024):
            wd = min(1024, width - o)
            o_ref[:, dst + o:dst + o + wd] = w_ref[src + o:src + o + wd, :].T.astype(bf16)

    def bkv(i, kv, g):
        return _O_BKV + ((i * 2 + kv) * B_KV_GROUPS + g) * HEAD_DIM

    cp(_O_GL, N_BRANCH * D_MODEL, _P_GL)
    for src, dst in ((_O_AQ, _P_AQ), (_O_IQ, _P_IQ), (_O_BQ, _P_BQ), (_O_CQ, _P_CQ), (_O_CK, _P_CK)):
        cp(src, 1024, dst)
    for i in range(3):
        for g in range(B_KV_GROUPS):
            cp(bkv(i, 0, g), HEAD_DIM, _P_BK + (i * B_KV_GROUPS + g) * HEAD_DIM)
    for g in range(B_KV_GROUPS):
        cp(bkv(0, 1, g), HEAD_DIM, _P_BVC + g * HEAD_DIM)
    cp(_O_ALAT, A_LATENT, _P_ALAT)
    ov_ref[_PV_CV:_PV_CV + 1024, :] = w_ref[_O_CV:_O_CV + 1024, :].astype(bf16)
    for i in (1, 2):
        for g in range(B_KV_GROUPS):
            dst = _PV_BV + ((i - 1) * B_KV_GROUPS + g) * HEAD_DIM
            ov_ref[dst:dst + HEAD_DIM, :] = w_ref[bkv(i, 1, g):bkv(i, 1, g) + HEAD_DIM, :].astype(bf16)
    lane = lax.broadcasted_iota(i32, (w_ref.shape[1], 128), 1)
    assert _O_IW == _O_IK + IDX_DIM and _O_IK % 8 == 0 and _O_BG % 8 == 0
    blk = w_ref[_O_IK:_O_IK + 128, :].T
    o_ref[:, _P_IKW:_P_IKW + 128] = jnp.where(lane < IDX_DIM + IDX_HEADS, blk, 0.0).astype(bf16)
    blk = w_ref[_O_BG:_O_BG + 128, :].T
    o_ref[:, _P_BG:_P_BG + 128] = jnp.where(lane < 3 * B_HEADS, blk, 0.0).astype(bf16)


def _pack_w_in(w_in):
    depth, k, n = w_in.shape
    w_t = jnp.swapaxes(w_in, 1, 2)
    tc = 128
    return pl.pallas_call(
        _pack_w_in_kernel,
        grid=(depth, k // tc),
        in_specs=[pl.BlockSpec((None, n, tc), lambda l, i: (l, 0, i))],
        out_specs=[pl.BlockSpec((None, tc, _P_TOT), lambda l, i: (l, i, 0)),
                   pl.BlockSpec((None, _PV_TOT, tc), lambda l, i: (l, 0, i))],
        out_shape=[jax.ShapeDtypeStruct((depth, k, _P_TOT), bf16),
                   jax.ShapeDtypeStruct((depth, _PV_TOT, k), bf16)],
        compiler_params=_cparams(("parallel", "parallel")),
    )(w_t)


def _cover_t(L):
    n_cmp_pad = L // CMP_STRIDE
    starts = np.arange(n_cmp_pad) * CMP_STRIDE
    slc_start = np.arange(L // SLC_LEN) * SLC_LEN
    cover = ((starts[:, None] < slc_start[None, :] + SLC_LEN)
             & (starts[:, None] + CMP_LEN > slc_start[None, :])).astype(np.float32)
    n_cmp = (L - CMP_LEN) // CMP_STRIDE + 1
    cover[n_cmp:] = 0.0
    return jnp.asarray(cover.T)


def _token_mixing(u, h, B, L, l, lw, tabs, cov_t, lam_init):
    M = B * L
    G = B_KV_GROUPS
    aq, iq, bq, cq, ck, bk, alat_n, ikw, gates = _prep(h, lw['a_lat_g'], tabs, L)

    aw = A_HEADS * HEAD_DIM
    ak = _mm_rope(alat_n, lw['a_up'], l, aw, tabs['plain'], L, 512, aw)
    av_tc = _proj_t(alat_n, lw['a_up_vt'], l, 0, A_HEADS, HEAD_DIM, 512, B, L)
    ik = ikw[:, :IDX_DIM].astype(bf16).reshape(B, L, IDX_DIM)
    iw_t = ikw[:, IDX_DIM:IDX_DIM + IDX_HEADS].reshape(B, L, IDX_HEADS).transpose(0, 2, 1)
    mask_t = _dsa_mask(ik, iq.reshape(B, L, -1), iw_t)
    ya = _dsa_attn(aq.reshape(B, L, -1), ak.reshape(B, L, -1), av_tc, mask_t)

    bvc = h[:, _P_BVC:_P_BVC + G * HEAD_DIM].reshape(B, L, G, HEAD_DIM)
    bkc = bk[:, :G * HEAD_DIM].reshape(B, L, G, HEAD_DIM)
    n_row = L // CMP_STRIDE
    xcmp = jnp.stack([bkc, bvc], axis=1)
    xcmp = xcmp.transpose(0, 1, 3, 2, 4).reshape(B, 2, G, n_row, CMP_STRIDE * HEAD_DIM)
    cmp_out = _nsa_compress(xcmp, lw['cmp_w1'], lw['cmp_w2'], lw['cmp_pe'], l)
    kc = cmp_out[:, 0]
    vc_t = cmp_out[:, 1].transpose(0, 1, 3, 2)
    bv_tc = _proj_t(u, lw['w_vt'], l, _PV_BV, 2 * G, HEAD_DIM, 512, B, L)
    gates_t = gates[:, :3 * B_HEADS].reshape(B, L, G, 3 * B_HPG).transpose(0, 2, 3, 1)
    yb = _nsa_attn(bq.reshape(B, L, -1), kc, vc_t, bk.reshape(B, L, -1), bv_tc, gates_t, cov_t)

    cv_tc = _proj_t(u, lw['w_vt'], l, _PV_CV, C_HEADS, 2 * C_DIM, _DIFF_TK, B, L)
    yc = _diff_attn(cq.reshape(B, L, -1), ck.reshape(B, L, -1), cv_tc, lw['lam'],
                    lw['c_subln_g'].reshape(2 * C_DIM, 1), lam_init)

    return _merge(ya.reshape(M, -1), yb.reshape(M, -1), yc.reshape(M, -1), lw['w_br'], l, h)


def kernel(x, c, w_ada, b_ada, w_in, a_lat_g, a_up, cmp_w1, cmp_w2, cmp_pe, lam, c_subln_g, w_br, w_o,
           w_ffn_in, w_ffn_out, ln_g, ln_b):
    B, L, D = x.shape
    depth = w_ada.shape[0]
    M = B * L
    alpha = (2 * depth) ** 0.25

    c_pad = jnp.zeros((8, D), f32).at[:B].set(c)
    mod = _ada(c_pad, w_ada, b_ada)[:, :B]
    mods = [[mod[l, :, i * D:(i + 1) * D].reshape(B, 1, D) for i in range(6)] for l in range(depth)]

    assert HEAD_DIM == C_DIM
    tabs = dict(plain=_rope_tables(L, HEAD_DIM), query=_rope_tables(L, HEAD_DIM, HEAD_DIM ** -0.5 * _LOG2E),
                idx=_rope_tables(L, IDX_DIM))
    cov_t = _cover_t(L)

    w_in_p, w_vt = _pack_w_in(w_in)
    pe_flat = jnp.zeros((depth, 2, 8, CMP_LEN * HEAD_DIM), f32).at[:, :, 0].set(
        cmp_pe.reshape(depth, 2, CMP_LEN * HEAD_DIM)).astype(bf16)
    a_up_vt = jnp.swapaxes(a_up[:, :, A_HEADS * HEAD_DIM:], 1, 2).astype(bf16)
    wb = dict(a_up=a_up.astype(bf16), a_up_vt=a_up_vt, w_vt=w_vt, cmp_w1=cmp_w1.astype(bf16),
              cmp_w2=cmp_w2.astype(bf16), cmp_pe=pe_flat, w_br=w_br)
    w_o_b = w_o.astype(bf16)
    w_fo_b = w_ffn_out.astype(bf16)

    x2 = x.reshape(M, D)
    u = _lnmod(x2, mods[0][1], mods[0][0], L)
    for l in range(depth):
        lam_init = 0.8 - 0.6 * math.exp(-0.3 * l)
        sh_a, sc_a, g_a, sh_f, sc_f, g_f = mods[l]
        lw = dict(wb, a_lat_g=a_lat_g[l], lam=lam[l], c_subln_g=c_subln_g[l])
        h = _mm(u, w_in_p, l, 1024, 768, bf16)
        merged = _token_mixing(u, h, B, L, l, lw, tabs, cov_t, lam_init)
        x2, u = _mm_res_ln(merged, w_o_b, l, x2, g_a, ln_g[l, 0], ln_b[l, 0], sc_f, sh_f, L, alpha, True)
        f = _ffn_in(u, w_ffn_in, l)
        last = l == depth - 1
        nsc, nsh = (sc_f, sh_f) if last else (mods[l + 1][1], mods[l + 1][0])
        x2, u = _mm_res_ln(f, w_fo_b, l, x2, g_f, ln_g[l, 1], ln_b[l, 1], nsc, nsh, L, alpha, not last)
    return x2.reshape(B, L, D)
```

```python
import functools
import math

import numpy as np
import jax
import jax.numpy as jnp
from jax import lax
from jax.experimental import pallas as pl
from jax.experimental.pallas import tpu as pltpu

f32 = jnp.float32
bf16 = jnp.bfloat16
i32 = jnp.int32

D_MODEL = 2048
HEAD_DIM = 128
ROPE_THETA = 500000.0
NEG = -1e30
FORCE = 1e6
A_HEADS = 8
A_LATENT = 512
IDX_HEADS = 16
IDX_DIM = 64
DSA_TOPK = 256
B_HEADS = 8
B_KV_GROUPS = 2
B_HPG = B_HEADS // B_KV_GROUPS
CMP_LEN = 32
CMP_STRIDE = 16
SLC_LEN = 64
SLC_TOPN = 16
WIN_LEN = 512
C_HEADS = 4
C_DIM = 128
BRANCH_W = A_HEADS * HEAD_DIM
N_BRANCH = 3
D_FF = int(math.ceil(8 * D_MODEL / 3 / 256)) * 256

_O_AQ = 0
_O_ALAT = _O_AQ + A_HEADS * HEAD_DIM
_O_IQ = _O_ALAT + A_LATENT
_O_IK = _O_IQ + IDX_HEADS * IDX_DIM
_O_IW = _O_IK + IDX_DIM
_O_BQ = _O_IW + IDX_HEADS
_O_BKV = _O_BQ + B_HEADS * HEAD_DIM
_O_BG = _O_BKV + 3 * 2 * B_KV_GROUPS * HEAD_DIM
_O_CQ = _O_BG + 3 * B_HEADS
_O_CK = _O_CQ + C_HEADS * 2 * C_DIM
_O_CV = _O_CK + C_HEADS * 2 * C_DIM
_O_GL = _O_CV + C_HEADS * 2 * C_DIM
_N_IN = _O_GL + N_BRANCH * D_MODEL

_P_GL = 0
_P_BK = 6144
_P_BVC = 6912
_P_AQ = 7168
_P_IQ = 8192
_P_BQ = 9216
_P_CQ = 10240
_P_CK = 11264
_P_ALAT = 12288
_P_IKW = 12800
_P_BG = 12928
_P_TOT = 13056
_PV_CV = 0
_PV_BV = 1024
_PV_TOT = 1536

_VMEM_LIMIT = 48 * 1024 * 1024
_LOG2E = 1.4426950408889634
_INF = float("inf")
_ONES = 16
_RES_LN_TM = 512
_RES_LN_SUB = 256
_RADIX_FIRST = 26
_DIFF_TK = 1024
_DIFF_TQ = 512
_HEAD_GROUP = 2


def _cparams(sem, vmem=_VMEM_LIMIT):
    return pltpu.CompilerParams(dimension_semantics=sem, vmem_limit_bytes=vmem)


def _sigmoid(x):
    return 1.0 / (1.0 + jnp.exp(-x))


def _dot_nt(a, b):
    return lax.dot_general(a, b, (((1,), (1,)), ((), ())), preferred_element_type=f32)


def _ada_kernel(c_ref, w_ref, b_ref, o_ref):
    c = c_ref[...]
    cs = c * _sigmoid(c)
    o_ref[0] = jnp.dot(cs, w_ref[0], preferred_element_type=f32,
                       precision=lax.Precision.HIGHEST) + b_ref[0]


def _ada(c_pad, w_ada, b_ada):
    depth, d, n = w_ada.shape
    tn = 512
    return pl.pallas_call(
        _ada_kernel,
        grid=(depth, n // tn),
        in_specs=[pl.BlockSpec((8, d), lambda l, j: (0, 0)),
                  pl.BlockSpec((1, d, tn), lambda l, j: (l, 0, j)),
                  pl.BlockSpec((1, 1, tn), lambda l, j: (l, 0, j))],
        out_specs=pl.BlockSpec((1, 8, tn), lambda l, j: (l, 0, j)),
        out_shape=jax.ShapeDtypeStruct((depth, 8, n), f32),
        compiler_params=_cparams(("parallel", "parallel")),
    )(c_pad, w_ada, b_ada.reshape(depth, 1, n))


def _ln_rows(x, eps):
    mu = jnp.mean(x, axis=-1, keepdims=True)
    d = x - mu
    var = jnp.mean(d * d, axis=-1, keepdims=True)
    return d * lax.rsqrt(var + eps)


def _lnmod_kernel(x_ref, sc_ref, sh_ref, o_ref):
    y = _ln_rows(x_ref[...], 1e-5)
    o_ref[...] = (y * (1.0 + sc_ref[0]) + sh_ref[0]).astype(o_ref.dtype)


def _lnmod(x2, sc, sh, L):
    m, d = x2.shape
    tm = 512
    per_b = L // tm
    return pl.pallas_call(
        _lnmod_kernel,
        grid=(m // tm,),
        in_specs=[pl.BlockSpec((tm, d), lambda i: (i, 0)),
                  pl.BlockSpec((1, 1, d), lambda i: (i // per_b, 0, 0)),
                  pl.BlockSpec((1, 1, d), lambda i: (i // per_b, 0, 0))],
        out_specs=pl.BlockSpec((tm, d), lambda i: (i, 0)),
        out_shape=jax.ShapeDtypeStruct((m, d), bf16),
        compiler_params=_cparams(("parallel",)),
    )(x2, sc, sh)


def _mm_kernel(a_ref, w_ref, o_ref):
    o_ref[...] = jnp.dot(a_ref[...], w_ref[...], preferred_element_type=f32).astype(o_ref.dtype)


def _mm(a, w, l, tm, tn, out_dtype):
    m, k = a.shape
    n = w.shape[2]
    return pl.pallas_call(
        _mm_kernel,
        grid=(n // tn, m // tm),
        in_specs=[pl.BlockSpec((tm, k), lambda j, i: (i, 0)),
                  pl.BlockSpec((None, k, tn), lambda j, i: (l, 0, j))],
        out_specs=pl.BlockSpec((tm, tn), lambda j, i: (i, j)),
        out_shape=jax.ShapeDtypeStruct((m, n), out_dtype),
        compiler_params=_cparams(("parallel", "parallel")),
    )(a, w)


def _proj_t_kernel(w_ref, a_ref, o_ref):
    n, d = o_ref.shape[1], o_ref.shape[3] - _ONES
    tk = o_ref.shape[4]
    res = _dot_nt(w_ref[...], a_ref[...])
    for i in range(n):
        o_ref[0, i, 0, :d, :] = res[i * d:(i + 1) * d, :].astype(o_ref.dtype)
        o_ref[0, i, 0, d:, :] = jnp.ones((_ONES, tk), o_ref.dtype)


def _proj_t(a, w_t, l, row0, n, d, tk, B, L):
    k = a.shape[1]
    rows = n * d
    nc = L // tk
    return pl.pallas_call(
        _proj_t_kernel,
        grid=(B, nc),
        in_specs=[pl.BlockSpec((None, rows, k), lambda b, c: (l, row0 // rows, 0)),
                  pl.BlockSpec((tk, k), lambda b, c: (b * nc + c, 0))],
        out_specs=pl.BlockSpec((1, n, 1, d + _ONES, tk), lambda b, c: (b, 0, c, 0, 0)),
        out_shape=jax.ShapeDtypeStruct((B, n, nc, d + _ONES, tk), bf16),
        compiler_params=_cparams(("parallel", "parallel")),
    )(w_t, a)


def _rope_low(shape, d, half):
    lane = lax.broadcasted_iota(i32, shape, 1)
    return (lane & (d - 1)) < half


def _rope_heads(x, c, s, low, half):
    partner = jnp.where(low, pltpu.roll(x, 128 - half, 1), pltpu.roll(x, half, 1))
    return x * c + partner * s


def _mm_rope_kernel(a_ref, w_ref, c_ref, s_ref, o_ref):
    acc = jnp.dot(a_ref[...], w_ref[...], preferred_element_type=f32)
    c, s = c_ref[...], s_ref[...]
    low = _rope_low(c.shape, HEAD_DIM, HEAD_DIM // 8)
    for h in range(acc.shape[1] // 128):
        sl = slice(h * 128, (h + 1) * 128)
        o_ref[:, sl] = _rope_heads(acc[:, sl], c, s, low, HEAD_DIM // 8).astype(o_ref.dtype)


def _mm_rope(a, w, l, n, tabs, L, tm, tn):
    m, k = a.shape
    per_b = L // tm
    tspec = pl.BlockSpec((tm, 128), lambda j, i: (i % per_b, 0))
    return pl.pallas_call(
        _mm_rope_kernel,
        grid=(n // tn, m // tm),
        in_specs=[pl.BlockSpec((tm, k), lambda j, i: (i, 0)),
                  pl.BlockSpec((None, k, tn), lambda j, i: (l, 0, j)),
                  tspec, tspec],
        out_specs=pl.BlockSpec((tm, tn), lambda j, i: (i, j)),
        out_shape=jax.ShapeDtypeStruct((m, n), bf16),
        compiler_params=_cparams(("parallel", "parallel")),
    )(a, w, *tabs)


def _prep_kernel(aq_ref, iq_ref, bq_ref, cq_ref, ck_ref, bk_ref, alat_ref, ikw_ref, bg_ref, alg_ref,
                 c_ref, s_ref, cq_ref_t, sq_ref_t, c6_ref, s6_ref, p128_ref, p64_ref,
                 aq_o, iq_o, bq_o, cq_o, ck_o, bk_o, alat_o, ikw_o, g_o):
    rows = 64

    def rope_all(src, dst, c_t, s_t, p_ref):
        for blk in range(src.shape[1] // 256):
            cols = slice(blk * 256, (blk + 1) * 256)
            partner = jnp.dot(src[:, cols], p_ref[...], preferred_element_type=f32)
            for r in range(src.shape[0] // rows):
                rs = slice(r * rows, (r + 1) * rows)
                c, s = c_t[rs, :], s_t[rs, :]
                for half in range(2):
                    sl = slice(blk * 256 + half * 128, blk * 256 + (half + 1) * 128)
                    y = src[rs, sl].astype(f32) * c + partner[rs, half * 128:(half + 1) * 128] * s
                    dst[rs, sl] = y.astype(dst.dtype)

    rope_all(aq_ref, aq_o, cq_ref_t, sq_ref_t, p128_ref)
    rope_all(bq_ref, bq_o, cq_ref_t, sq_ref_t, p128_ref)
    rope_all(cq_ref, cq_o, cq_ref_t, sq_ref_t, p128_ref)
    rope_all(ck_ref, ck_o, c_ref, s_ref, p128_ref)
    rope_all(bk_ref, bk_o, c_ref, s_ref, p128_ref)
    rope_all(iq_ref, iq_o, c6_ref, s6_ref, p64_ref)

    idx = (c6_ref, s6_ref, _rope_low((rows, 128), IDX_DIM, IDX_DIM // 8), IDX_DIM // 8)

    def rope(x, tabs, rs):
        return _rope_heads(x, tabs[0][rs, :], tabs[1][rs, :], tabs[2], tabs[3])

    a = alat_ref[...].astype(f32)
    ms = jnp.mean(a * a, axis=-1, keepdims=True)
    alat_o[...] = (a * lax.rsqrt(ms + 1e-6) * alg_ref[...]).astype(alat_o.dtype)

    isk = lax.broadcasted_iota(i32, (rows, 128), 1) < IDX_DIM
    for r in range(ikw_ref.shape[0] // rows):
        rs = slice(r * rows, (r + 1) * rows)
        x = ikw_ref[rs, :].astype(f32)
        mu = jnp.sum(jnp.where(isk, x, 0.0), axis=-1, keepdims=True) * (1.0 / IDX_DIM)
        d = jnp.where(isk, x - mu, 0.0)
        var = jnp.sum(d * d, axis=-1, keepdims=True) * (1.0 / IDX_DIM)
        y = d * lax.rsqrt(var + 1e-5)
        ikw_o[rs, :] = jnp.where(isk, rope(y, idx, rs), x * (IDX_DIM ** -0.5 * IDX_HEADS ** -0.5))

    g_o[...] = _sigmoid(bg_ref[...].astype(f32))


def _prep(h, a_lat_g, tabs, L):
    m = h.shape[0]
    tm = 512
    per_b = L // tm

    def hs(width, off):
        return pl.BlockSpec((tm, width), lambda i, _o=off // width: (i, _o))

    tspec = pl.BlockSpec((tm, 128), lambda i: (i % per_b, 0))
    pspec = pl.BlockSpec((256, 256), lambda i: (0, 0))

    def os(width):
        return pl.BlockSpec((tm, width), lambda i: (i, 0))

    outs = [(1024, bf16)] * 5 + [(768, bf16), (512, bf16), (128, f32), (128, f32)]
    return pl.pallas_call(
        _prep_kernel,
        grid=(m // tm,),
        in_specs=[hs(1024, _P_AQ), hs(1024, _P_IQ), hs(1024, _P_BQ), hs(1024, _P_CQ), hs(1024, _P_CK),
                  hs(768, _P_BK), hs(512, _P_ALAT), hs(128, _P_IKW), hs(128, _P_BG),
                  pl.BlockSpec((1, A_LATENT), lambda i: (0, 0))] + [tspec] * 6 + [pspec, pspec],
        out_specs=[os(w) for w, _ in outs],
        out_shape=[jax.ShapeDtypeStruct((m, w), dt) for w, dt in outs],
        compiler_params=_cparams(("parallel",)),
    )(h, h, h, h, h, h, h, h, h, a_lat_g.reshape(1, A_LATENT), *tabs['plain'], *tabs['query'], *tabs['idx'],
      _rope_perm(HEAD_DIM), _rope_perm(IDX_DIM))


def _f32_order_key(x):
    b = int(np.float32(x).view(np.int32))
    return b ^ ((b >> 31) & 0x7FFFFFFF)


_KEY_NEG = _f32_order_key(NEG)


def _dsa_mask_kernel(ik_ref, iq_ref, iw_ref, o_ref, key_ref, qp_ref, j_ref, thr_ref, cnt_ref,
                     *, L, tq, ksel):
    j = pl.program_id(1)
    ck = 512
    nck = L // ck
    nc = ((j + 1) * tq + ck - 1) // ck
    n_out_i = L - nc * ck
    n_out = n_out_i.astype(f32)
    t_row = j * tq + lax.broadcasted_iota(i32, (1, tq), 1)

    for hp in range(IDX_HEADS // 2):
        for e in range(2):
            h = 2 * hp + e
            qp_ref[hp, e * tq:(e + 1) * tq, :] = iq_ref[0, :, h * IDX_DIM:(h + 1) * IDX_DIM]

    def score_chunk(c, carry):
        off = pl.multiple_of(c * ck, ck)
        ikc = ik_ref[0, pl.ds(off, ck), :]
        acc = jnp.zeros((ck, tq), f32)
        for hp in range(IDX_HEADS // 2):
            s2 = _dot_nt(ikc, qp_ref[hp])
            acc = acc + jnp.maximum(s2[:, :tq], 0.0) * iw_ref[0, 2 * hp:2 * hp + 1, :]
            acc = acc + jnp.maximum(s2[:, tq:], 0.0) * iw_ref[0, 2 * hp + 1:2 * hp + 2, :]
        acc = jnp.where(acc == 0.0, 0.0, acc)
        sidx = off + lax.broadcasted_iota(i32, (ck, tq), 0)
        key_ref[pl.ds(off, ck), :] = jnp.where(sidx <= t_row, acc, NEG)
        return carry

    def score_pair(p, carry):
        score_chunk(2 * p, carry)
        return score_chunk(2 * p + 1, carry)

    lax.fori_loop(0, nc // 2, score_pair, 0)

    @pl.when(nc % 2 == 1)
    def _():
        score_chunk(nc - 1, 0)

    def as_f32(key):
        return lax.bitcast_convert_type(key ^ ((key >> 31) & 0x7FFFFFFF), f32)

    def count(pred_fn):
        def body(c, cnt):
            off = pl.multiple_of(c * ck, ck)
            k = key_ref[pl.ds(off, ck), :]
            sidx = off + lax.broadcasted_iota(i32, (ck, tq), 0)
            return cnt + jnp.sum(pred_fn(k, sidx).reshape(ck // 64, 64, tq), axis=0)
        part = lax.fori_loop(0, nc, body, jnp.zeros((64, tq), f32))
        return jnp.sum(part, axis=0, keepdims=True)

    kf = float(ksel)

    def bit_body(i, carry):
        thr, cnt_thr = carry
        cand = thr + lax.shift_left(jnp.int32(1), 31 - i)
        cand_f = as_f32(cand)
        cnt = count(lambda k, s: jnp.where(k >= cand_f, 1.0, 0.0)) + jnp.where(_KEY_NEG >= cand, n_out, 0.0)
        ok = cnt >= kf
        return jnp.where(ok, cand, thr), jnp.where(ok, cnt, cnt_thr)

    thr_key, cnt_ge = lax.fori_loop(0, _RADIX_FIRST, bit_body, (jnp.full((1, tq), -2 ** 31, i32),
                                                                 jnp.full((1, tq), float(L), f32)))
    thr_ref[...] = thr_key
    cnt_ref[...] = cnt_ge

    @pl.when(jnp.max(jnp.abs(cnt_ge - kf)) > 0.0)
    def _():
        t, c = lax.fori_loop(_RADIX_FIRST, 32, bit_body, (thr_key, cnt_ge))
        thr_ref[...] = t
        cnt_ref[...] = c

    thr_key, cnt_ge = thr_ref[...], cnt_ref[...]
    thr = as_f32(thr_key)

    j_ref[...] = jnp.full((1, tq), L, i32)

    @pl.when(jnp.max(cnt_ge) > kf)
    def _():
        nbits = L.bit_length() - 1
        cnt_gt = count(lambda k, s: jnp.where(k > thr, 1.0, 0.0)) + jnp.where(_KEY_NEG > thr_key, n_out, 0.0)
        need = kf - cnt_gt

        def jbit(i, cur):
            cand = cur | lax.shift_left(jnp.int32(1), nbits - 1 - i)
            f = count(lambda k, s: jnp.where(k == thr, jnp.where(s < cand, 1.0, 0.0), 0.0))
            f = f + jnp.where(thr_key == _KEY_NEG, jnp.clip(cand - nc * ck, 0, n_out_i).astype(f32), 0.0)
            return jnp.where(f < need, cand, cur)

        j_ref[...] = lax.fori_loop(0, nbits, jbit, jnp.zeros((1, tq), i32))

    jlast = j_ref[...]

    def write(c, carry):
        off = pl.multiple_of(c * ck, ck)
        k = key_ref[pl.ds(off, ck), :]
        sidx = off + lax.broadcasted_iota(i32, (ck, tq), 0)
        sel = jnp.where(k > thr, _INF, jnp.where(k == thr, jnp.where(sidx <= jlast, _INF, NEG), NEG))
        o_ref[0, pl.ds(off, ck), :] = jnp.where(sidx <= t_row, sel, NEG)
        return carry

    lax.fori_loop(0, nc, write, 0)

    def write_rest(c, carry):
        off = pl.multiple_of(c * ck, ck)
        o_ref[0, pl.ds(off, ck), :] = jnp.full((ck, tq), NEG, f32)
        return carry

    lax.fori_loop(nc, nck, write_rest, 0)


def _dsa_mask(ik, iq, iw_t):
    B, L, _ = iq.shape
    tq = 128
    ksel = min(DSA_TOPK, L // 4)
    return pl.pallas_call(
        functools.partial(_dsa_mask_kernel, L=L, tq=tq, ksel=ksel),
        grid=(B, L // tq),
        in_specs=[pl.BlockSpec((1, L, IDX_DIM), lambda b, j: (b, 0, 0)),
                  pl.BlockSpec((1, tq, IDX_HEADS * IDX_DIM), lambda b, j: (b, j, 0)),
                  pl.BlockSpec((1, IDX_HEADS, tq), lambda b, j: (b, 0, j))],
        out_specs=pl.BlockSpec((1, L, tq), lambda b, j: (b, 0, j)),
        out_shape=jax.ShapeDtypeStruct((B, L, L), f32),
        scratch_shapes=[pltpu.VMEM((L, tq), f32), pltpu.VMEM((IDX_HEADS // 2, 2 * tq, IDX_DIM), bf16),
                        pltpu.VMEM((1, tq), i32), pltpu.VMEM((1, tq), i32), pltpu.VMEM((1, tq), f32)],
        compiler_params=_cparams(("parallel", "parallel")),
    )(ik, iq, iw_t)


def _col_reduce(x, op):
    r, c = x.shape
    if r > 64:
        x = op(x.reshape(r // 64, 64, c), axis=0)
    return op(x, axis=0, keepdims=True)


def _flash_group(ss, cap, v_ts, m_refs, acc_refs):
    ss, mloc = _flash_mask_max(ss, cap)
    _flash_update(ss, mloc, cap is not None, v_ts, m_refs, acc_refs)


def _flash_mask_max(ss, cap):
    if cap is not None:
        ss = [jnp.minimum(s, cap) for s in ss]
    return ss, [_col_reduce(s, jnp.max) for s in ss]


def _flash_update(ss, mloc, masked, v_ts, m_refs, acc_refs):
    n = len(ss)
    m_prev = [r[...] for r in m_refs]
    m_new = [jnp.maximum(m_prev[i], mloc[i]) for i in range(n)]
    alpha = [jnp.exp2(m_prev[i] - m_new[i]) for i in range(n)]
    ps = [jnp.exp2((ss[i] - m_new[i]).astype(bf16)) for i in range(n)]
    pv = [jnp.dot(v_ts[i], ps[i], preferred_element_type=f32) for i in range(n)]
    for i in range(n):
        acc_new = acc_refs[i][...] * alpha[i] + pv[i]
        if masked:
            acc_new = jnp.where(m_new[i] <= NEG, 0.0, acc_new)
        acc_refs[i][...] = acc_new
        m_refs[i][...] = m_new[i]


def _flash_heads(score_fns, cap, v_ts, m_ref, acc_ref, group):
    n = len(score_fns)
    groups = [list(range(g, min(g + group, n))) for g in range(0, n, group)]
    nxt = _flash_mask_max([score_fns[h]() for h in groups[0]], cap)
    for gi, hs in enumerate(groups):
        ss, mloc = nxt
        if gi + 1 < len(groups):
            nxt = _flash_mask_max([score_fns[h]() for h in groups[gi + 1]], cap)
        _flash_update(ss, mloc, cap is not None, [v_ts[h] for h in hs], [m_ref.at[h] for h in hs],
                      [acc_ref.at[h] for h in hs])


def _flash_init(m_ref, acc_ref):
    m_ref[...] = jnp.full(m_ref.shape, NEG, f32)
    acc_ref[...] = jnp.zeros(acc_ref.shape, f32)


def _flash_result(acc, d):
    return acc[:d] / jnp.maximum(acc[d:d + 1], 1e-30)


def _dsa_attn_kernel(jm_ref, cm_ref, lm_ref, q_ref, k_ref, vt_ref, mask_ref, o_ref, m_ref, acc_ref):
    s = pl.program_id(1)

    @pl.when(cm_ref[s] == 0)
    def _():
        _flash_init(m_ref, acc_ref)

    cap = mask_ref[0]

    def scores(h):
        sl = slice(h * HEAD_DIM, (h + 1) * HEAD_DIM)
        return lambda: _dot_nt(k_ref[0, :, sl], q_ref[0, :, sl])

    v_ts = [vt_ref[0, h, 0] for h in range(A_HEADS)]
    _flash_heads([scores(h) for h in range(A_HEADS)], cap, v_ts, m_ref, acc_ref, _HEAD_GROUP)

    @pl.when(lm_ref[s] == 1)
    def _():
        for h in range(A_HEADS):
            o = _flash_result(acc_ref[h], HEAD_DIM)
            o_ref[0, :, h * HEAD_DIM:(h + 1) * HEAD_DIM] = o.T.astype(o_ref.dtype)


def _dsa_attn(q, k, v_t, mask_t):
    B, L, W = q.shape
    tq, tk = 512, 512
    pairs = [(j, c) for j in range(L // tq) for c in range(((j + 1) * tq - 1) // tk + 1)]
    jm = jnp.asarray([p[0] for p in pairs], i32)
    cm = jnp.asarray([p[1] for p in pairs], i32)
    lm = jnp.asarray([int(i + 1 == len(pairs) or pairs[i + 1][0] != p[0]) for i, p in enumerate(pairs)], i32)
    return pl.pallas_call(
        _dsa_attn_kernel,
        grid_spec=pltpu.PrefetchScalarGridSpec(
            num_scalar_prefetch=3,
            grid=(B, len(pairs)),
            in_specs=[pl.BlockSpec((1, tq, W), lambda b, s, jm, cm, lm: (b, jm[s], 0)),
                      pl.BlockSpec((1, tk, W), lambda b, s, jm, cm, lm: (b, cm[s], 0)),
                      pl.BlockSpec((1, A_HEADS, 1, HEAD_DIM + _ONES, tk),
                                   lambda b, s, jm, cm, lm: (b, 0, cm[s], 0, 0)),
                      pl.BlockSpec((1, tk, tq), lambda b, s, jm, cm, lm: (b, cm[s], jm[s]))],
            out_specs=pl.BlockSpec((1, tq, W), lambda b, s, jm, cm, lm: (b, jm[s], 0)),
            scratch_shapes=[pltpu.VMEM((A_HEADS, 1, tq), f32),
                            pltpu.VMEM((A_HEADS, HEAD_DIM + _ONES, tq), f32)]),
        out_shape=jax.ShapeDtypeStruct((B, L, W), bf16),
        compiler_params=_cparams(("parallel", "arbitrary")),
    )(jm, cm, lm, q, k, v_t, mask_t)


def _nsa_cmp_kernel(x_ref, w1_ref, w2_ref, pe_ref, o_ref):
    x = x_ref[0, 0, 0]
    w1 = w1_ref[0]
    half = CMP_STRIDE * HEAD_DIM
    a = jnp.dot(x, w1[:half], preferred_element_type=f32)
    b = jnp.dot(x, w1[half:], preferred_element_type=f32)
    pe = jnp.dot(pe_ref[0], w1, preferred_element_type=f32)[0:1]
    n = a.shape[0]
    pre = a + pltpu.roll(b, n - 1, 0) + pe
    act = pre * _sigmoid(pre)
    o_ref[0, 0, 0] = jnp.dot(act.astype(bf16), w2_ref[0], preferred_element_type=f32).astype(o_ref.dtype)


def _nsa_compress(x, w1, w2, pe, l):
    B, _, G, n, wd = x.shape
    return pl.pallas_call(
        _nsa_cmp_kernel,
        grid=(B, 2, G),
        in_specs=[pl.BlockSpec((1, 1, 1, n, wd), lambda b, t, g: (b, t, g, 0, 0)),
                  pl.BlockSpec((None, 1, CMP_LEN * HEAD_DIM, HEAD_DIM), lambda b, t, g: (l, t, 0, 0)),
                  pl.BlockSpec((None, 1, HEAD_DIM, HEAD_DIM), lambda b, t, g: (l, t, 0, 0)),
                  pl.BlockSpec((None, 1, 8, CMP_LEN * HEAD_DIM), lambda b, t, g: (l, t, 0, 0))],
        out_specs=pl.BlockSpec((1, 1, 1, n, HEAD_DIM), lambda b, t, g: (b, t, g, 0, 0)),
        out_shape=jax.ShapeDtypeStruct((B, 2, G, n, HEAD_DIM), bf16),
        compiler_params=_cparams(("parallel", "parallel", "parallel")),
    )(x, w1, w2, pe)


def _nsa_attn_kernel(q_ref, kc_ref, vct_ref, ks_ref, vst_ref, kw_ref, vwt_ref, g_ref, cov_ref, o_ref,
                     m_ref, acc_ref, out_ref, sel_ref, *, L, tq, tk, n_sel):
    j = pl.program_id(2)
    t_row = j * tq + lax.broadcasted_iota(i32, (1, tq), 1)
    n_cmp = kc_ref.shape[2]
    n_slc = L // SLC_LEN
    qs = [q_ref[0, :, hh * HEAD_DIM:(hh + 1) * HEAD_DIM] for hh in range(B_HPG)]

    def gate(hh, i):
        return g_ref[0, 0, hh * 3 + i:hh * 3 + i + 1, :]

    kc = kc_ref[0, 0]
    vct = vct_ref[0, 0]
    cend = lax.broadcasted_iota(i32, (n_cmp, tq), 0) * CMP_STRIDE + (CMP_LEN - 1)
    mc = cend <= t_row
    ss = [jnp.where(mc, _dot_nt(kc, qs[hh]), NEG) for hh in range(B_HPG)]
    mxs = [_col_reduce(s, jnp.max) for s in ss]
    ps = [jnp.where(mc, jnp.exp2(ss[hh] - mxs[hh]), 0.0) for hh in range(B_HPG)]
    inv = [1.0 / jnp.maximum(_col_reduce(p, jnp.sum), 1e-30) for p in ps]
    pvs = [jnp.dot(vct, p.astype(bf16), preferred_element_type=f32) for p in ps]
    psum = ps[0] * inv[0]
    for hh in range(1, B_HPG):
        psum = psum + ps[hh] * inv[hh]
    for hh in range(B_HPG):
        out_ref[hh] = (gate(hh, 0) * inv[hh]) * pvs[hh]
    imp = jnp.dot(cov_ref[...], psum, preferred_element_type=f32, precision=lax.Precision.HIGHEST)
    blk = lax.broadcasted_iota(i32, (n_slc, tq), 0)
    cur = t_row >> (SLC_LEN.bit_length() - 1)
    forced = (blk == 0) | (blk == cur) | (blk == cur - 1)
    imp = jnp.where(forced, FORCE, jnp.where(blk <= cur, imp, NEG))
    rank = jnp.zeros((n_slc, tq), f32)
    for r in range(n_slc):
        row = imp[r:r + 1, :]
        rank = rank + jnp.where(row > imp, 1.0, jnp.where(row == imp, jnp.where(blk > r, 1.0, 0.0), 0.0))
    sel_ref[...] = jnp.where(rank < float(n_sel), _INF, NEG)

    def finish(i):
        for hh in range(B_HPG):
            out_ref[hh] = out_ref[hh] + gate(hh, i) * _flash_result(acc_ref[hh], HEAD_DIM)

    def run_branch(k_ref, vt_ref, c_lo, c_hi, mask_fn):
        _flash_init(m_ref, acc_ref)

        def body(c, carry):
            off = pl.multiple_of(c * tk, tk)
            kch = k_ref[0, pl.ds(off, tk), :]
            vch = vt_ref[0, 0, c]
            sidx = off + lax.broadcasted_iota(i32, (tk, tq), 0)
            cap = mask_fn(c, sidx)
            fns = [functools.partial(_dot_nt, kch, qs[hh]) for hh in range(B_HPG)]
            _flash_heads(fns, cap, [vch] * B_HPG, m_ref, acc_ref, _HEAD_GROUP)
            return carry

        lax.fori_loop(c_lo, c_hi, body, 0)

    bpc = tk // SLC_LEN

    def slc_mask(c, sidx):
        rows = [jnp.broadcast_to(sel_ref[pl.ds(c * bpc + b, 1), :], (SLC_LEN, tq)) for b in range(bpc)]
        return jnp.where(sidx <= t_row, jnp.concatenate(rows, axis=0), NEG)

    c_hi = ((j + 1) * tq + tk - 1) // tk
    run_branch(ks_ref, vst_ref, 0, c_hi, slc_mask)
    finish(1)

    def win_mask(c, sidx):
        d = t_row - sidx
        return jnp.where(d >= 0, jnp.where(d < WIN_LEN, _INF, NEG), NEG)

    c_lo = jnp.maximum(j * tq - (WIN_LEN - 1), 0) // tk
    run_branch(kw_ref, vwt_ref, c_lo, c_hi, win_mask)
    finish(2)

    for hh in range(B_HPG):
        o_ref[0, :, hh * HEAD_DIM:(hh + 1) * HEAD_DIM] = out_ref[hh].T.astype(o_ref.dtype)


def _nsa_attn(bq, kc, vc_t, bk, bv_tc, gates_t, cov_t):
    B, L, _ = bq.shape
    G = B_KV_GROUPS
    tq, tk = 512, 512
    n_cmp = kc.shape[2]
    n_slc = L // SLC_LEN
    n_sel = min(SLC_TOPN, n_slc)
    gw = B_HPG * HEAD_DIM
    nc = L // tk
    return pl.pallas_call(
        functools.partial(_nsa_attn_kernel, L=L, tq=tq, tk=tk, n_sel=n_sel),
        grid=(B, G, L // tq),
        in_specs=[pl.BlockSpec((1, tq, gw), lambda b, g, j: (b, j, g)),
                  pl.BlockSpec((1, 1, n_cmp, HEAD_DIM), lambda b, g, j: (b, g, 0, 0)),
                  pl.BlockSpec((1, 1, HEAD_DIM, n_cmp), lambda b, g, j: (b, g, 0, 0)),
                  pl.BlockSpec((1, L, HEAD_DIM), lambda b, g, j: (b, 0, 2 + g)),
                  pl.BlockSpec((1, 1, nc, HEAD_DIM + _ONES, tk), lambda b, g, j: (b, g, 0, 0, 0)),
                  pl.BlockSpec((1, L, HEAD_DIM), lambda b, g, j: (b, 0, 4 + g)),
                  pl.BlockSpec((1, 1, nc, HEAD_DIM + _ONES, tk),
                               lambda b, g, j: (b, B_KV_GROUPS + g, 0, 0, 0)),
                  pl.BlockSpec((1, 1, 3 * B_HPG, tq), lambda b, g, j: (b, g, 0, j)),
                  pl.BlockSpec((n_slc, n_cmp), lambda b, g, j: (0, 0))],
        out_specs=pl.BlockSpec((1, tq, gw), lambda b, g, j: (b, j, g)),
        out_shape=jax.ShapeDtypeStruct((B, L, B_HEADS * HEAD_DIM), bf16),
        scratch_shapes=[pltpu.VMEM((B_HPG, 1, tq), f32),
                        pltpu.VMEM((B_HPG, HEAD_DIM + _ONES, tq), f32),
                        pltpu.VMEM((B_HPG, HEAD_DIM, tq), f32), pltpu.VMEM((n_slc, tq), f32)],
        compiler_params=_cparams(("parallel", "parallel", "parallel")),
    )(bq, kc, vc_t, bk, bv_tc, bk, bv_tc, gates_t, cov_t)


def _diff_attn_kernel(q_ref, k_ref, vt_ref, lam_ref, g_ref, o_ref, m_ref, acc_ref,
                      *, tq, tk, lam_init):
    j = pl.program_id(2)
    t_row = j * tq + lax.broadcasted_iota(i32, (1, tq), 1)
    qs = [q_ref[0, :, mi * C_DIM:(mi + 1) * C_DIM] for mi in range(2)]
    _flash_init(m_ref, acc_ref)

    def chunk(c, lo, width, masked):
        off = pl.multiple_of(c * tk + lo, tq)
        vch = vt_ref[0, 0, c, :, lo:lo + width]
        cap = None
        if masked:
            sidx = off + lax.broadcasted_iota(i32, (width, tq), 0)
            cap = jnp.where(sidx <= t_row, _INF, NEG)
        ss = [_dot_nt(k_ref[0, pl.ds(off, width), mi * C_DIM:(mi + 1) * C_DIM], qs[mi]) for mi in range(2)]
        _flash_group(ss, cap, [vch] * 2, [m_ref.at[mi] for mi in range(2)],
                     [acc_ref.at[mi] for mi in range(2)])

    def body(c, carry):
        chunk(c, 0, tk, False)
        return carry

    c_diag = (j * tq) // tk
    lax.fori_loop(0, c_diag, body, 0)

    @pl.when(j % 2 == 0)
    def _():
        chunk(c_diag, 0, tq, True)

    @pl.when(j % 2 == 1)
    def _():
        chunk(c_diag, 0, tq, False)
        chunk(c_diag, tq, tq, True)

    lam = lam_ref[...]
    lam_val = (jnp.exp(jnp.sum(lam[0:1] * lam[1:2], axis=1, keepdims=True))
               - jnp.exp(jnp.sum(lam[2:3] * lam[3:4], axis=1, keepdims=True)) + lam_init)
    o = _flash_result(acc_ref[0], 2 * C_DIM) - lam_val * _flash_result(acc_ref[1], 2 * C_DIM)
    ms = jnp.mean(o * o, axis=0, keepdims=True)
    y = o * lax.rsqrt(ms + 1e-6) * g_ref[...] * (1.0 - lam_init)
    o_ref[0] = y.T.astype(o_ref.dtype)


def _diff_attn(cq, ck, cv_tc, lam, g_col, lam_init):
    B, L, _ = cq.shape
    tq, tk = _DIFF_TQ, cv_tc.shape[-1]
    assert tk == 2 * tq
    hw = 2 * C_DIM
    nc = L // tk
    return pl.pallas_call(
        functools.partial(_diff_attn_kernel, tq=tq, tk=tk, lam_init=lam_init),
        grid=(B, C_HEADS, L // tq),
        in_specs=[pl.BlockSpec((1, tq, hw), lambda b, h, j: (b, j, h)),
                  pl.BlockSpec((1, L, hw), lambda b, h, j: (b, 0, h)),
                  pl.BlockSpec((1, 1, nc, hw + _ONES, tk), lambda b, h, j: (b, h, 0, 0, 0)),
                  pl.BlockSpec((4, C_DIM), lambda b, h, j: (0, 0)),
                  pl.BlockSpec((hw, 1), lambda b, h, j: (0, 0))],
        out_specs=pl.BlockSpec((1, tq, hw), lambda b, h, j: (b, j, h)),
        out_shape=jax.ShapeDtypeStruct((B, L, C_HEADS * hw), bf16),
        scratch_shapes=[pltpu.VMEM((2, 1, tq), f32), pltpu.VMEM((2, hw + _ONES, tq), f32)],
        compiler_params=_cparams(("parallel", "parallel", "parallel")),
    )(cq, ck, cv_tc, lam, g_col)


def _merge_kernel(ya_ref, yb_ref, yc_ref, w_ref, g0_ref, g1_ref, g2_ref, o_ref, wb_ref):
    @pl.when(pl.program_id(1) == 0)
    def _():
        wb_ref[...] = w_ref[...].astype(bf16)

    acc = None
    for r, (y_ref, g_ref) in enumerate(((ya_ref, g0_ref), (yb_ref, g1_ref), (yc_ref, g2_ref))):
        br = jnp.dot(y_ref[...], wb_ref[r], preferred_element_type=f32)
        t = _sigmoid(g_ref[...].astype(f32)) * br
        acc = t if acc is None else acc + t
    o_ref[...] = acc.astype(o_ref.dtype)


def _merge(ya, yb, yc, w_br, l, h):
    m, kw = ya.shape
    tm, tn = 1024, 512
    npb = D_MODEL // tn
    yspec = pl.BlockSpec((tm, kw), lambda j, i: (i, 0))

    def gspec(r):
        return pl.BlockSpec((tm, tn), lambda j, i, _r=r: (i, _r * npb + j))

    return pl.pallas_call(
        _merge_kernel,
        grid=(npb, m // tm),
        in_specs=[yspec, yspec, yspec,
                  pl.BlockSpec((None, N_BRANCH, kw, tn), lambda j, i: (l, 0, 0, j)),
                  gspec(0), gspec(1), gspec(2)],
        out_specs=pl.BlockSpec((tm, tn), lambda j, i: (i, j)),
        out_shape=jax.ShapeDtypeStruct((m, D_MODEL), bf16),
        scratch_shapes=[pltpu.VMEM((N_BRANCH, kw, tn), bf16)],
        compiler_params=_cparams(("parallel", "arbitrary")),
    )(ya, yb, yc, w_br, h, h, h)


def _mm_res_ln_kernel(a_ref, w_ref, x_ref, gate_ref, lg_ref, lb_ref, sc_ref, sh_ref, xo_ref, *u_refs,
                      alpha, sub):
    tm = a_ref.shape[0]
    for r in range(tm // sub):
        rows = slice(r * sub, (r + 1) * sub)
        y = jnp.dot(a_ref[rows, :], w_ref[...], preferred_element_type=f32)
        z = alpha * x_ref[rows, :] + gate_ref[0] * y
        xn = _ln_rows(z, 1e-5) * lg_ref[...] + lb_ref[...]
        xo_ref[rows, :] = xn
        if u_refs:
            u_refs[0][rows, :] = (_ln_rows(xn, 1e-5) * (1.0 + sc_ref[0]) + sh_ref[0]).astype(bf16)


def _mm_res_ln(a, w, l, x2, gate, ln_g, ln_b, sc, sh, L, alpha, emit_u):
    m, kdim = a.shape
    d = w.shape[2]
    tm = _RES_LN_TM if kdim * d * 2 <= 16 * 1024 * 1024 else _RES_LN_SUB
    per_b = L // tm
    bspec = pl.BlockSpec((1, 1, d), lambda i: (i // per_b, 0, 0))
    vspec = pl.BlockSpec((1, d), lambda i: (0, 0))
    rspec = pl.BlockSpec((tm, d), lambda i: (i, 0))
    out_shape = [jax.ShapeDtypeStruct((m, d), f32)]
    out_specs = [rspec]
    if emit_u:
        out_shape.append(jax.ShapeDtypeStruct((m, d), bf16))
        out_specs.append(rspec)
    res = pl.pallas_call(
        functools.partial(_mm_res_ln_kernel, alpha=alpha, sub=_RES_LN_SUB),
        grid=(m // tm,),
        in_specs=[pl.BlockSpec((tm, kdim), lambda i: (i, 0)),
                  pl.BlockSpec((None, kdim, d), lambda i: (l, 0, 0), pipeline_mode=pl.Buffered(1)),
                  rspec, bspec, vspec, vspec, bspec, bspec],
        out_specs=out_specs,
        out_shape=out_shape,
        compiler_params=_cparams(("parallel",), 60 * 1024 * 1024),
    )(a, w, x2, gate, ln_g.reshape(1, d), ln_b.reshape(1, d), sc, sh)
    return res if emit_u else (res[0], None)


def _ffn_in_kernel(a_ref, wg_ref, wu_ref, o_ref, wgb_ref, wub_ref):
    @pl.when(pl.program_id(1) == 0)
    def _():
        wgb_ref[...] = wg_ref[...].astype(bf16)
        wub_ref[...] = wu_ref[...].astype(bf16)

    a = a_ref[...]
    g = jnp.dot(a, wgb_ref[...], preferred_element_type=f32)
    u = jnp.dot(a, wub_ref[...], preferred_element_type=f32)
    o_ref[...] = (g * _sigmoid(g) * u).astype(o_ref.dtype)


def _ffn_in(a, w, l):
    m, k = a.shape
    tm, tn = 1024, 512
    nb = D_FF // tn
    return pl.pallas_call(
        _ffn_in_kernel,
        grid=(nb, m // tm),
        in_specs=[pl.BlockSpec((tm, k), lambda j, i: (i, 0)),
                  pl.BlockSpec((None, k, tn), lambda j, i: (l, 0, j)),
                  pl.BlockSpec((None, k, tn), lambda j, i: (l, 0, nb + j))],
        out_specs=pl.BlockSpec((tm, tn), lambda j, i: (i, j)),
        out_shape=jax.ShapeDtypeStruct((m, D_FF), bf16),
        scratch_shapes=[pltpu.VMEM((k, tn), bf16), pltpu.VMEM((k, tn), bf16)],
        compiler_params=_cparams(("parallel", "arbitrary")),
    )(a, w, w)


def _rope_tables(L, d, mult=1.0):
    r = d // 4
    half = r // 2
    inv = ROPE_THETA ** (-(jnp.arange(half, dtype=f32) * 2.0) / r)
    ang = jnp.arange(L).astype(f32)[:, None] * inv[None, :]
    cos, sin = jnp.cos(ang), jnp.sin(ang)
    c = jnp.concatenate([cos, cos, jnp.ones((L, d - r), f32)], axis=1)
    s = jnp.concatenate([-sin, sin, jnp.zeros((L, d - r), f32)], axis=1)
    rep = 128 // d
    return tuple(jnp.tile(t * mult, (1, rep)) for t in (c, s))


def _rope_perm(d):
    half = d // 8
    p = np.zeros((256, 256), np.float32)
    for i in range(256):
        li = i % d
        if li < half:
            p[i + half, i] = 1.0
        elif li < 2 * half:
            p[i - half, i] = 1.0
    return jnp.asarray(p, bf16)


def _pack_w_in_kernel(w_ref, o_ref, ov_ref):
    def cp(src, width, dst):
        for o in range(0, width, 1024):
            wd = min(1024, width - o)
            o_ref[:, dst + o:dst + o + wd] = w_ref[src + o:src + o + wd, :].T.astype(bf16)

    def bkv(i, kv, g):
        return _O_BKV + ((i * 2 + kv) * B_KV_GROUPS + g) * HEAD_DIM

    cp(_O_GL, N_BRANCH * D_MODEL, _P_GL)
    for src, dst in ((_O_AQ, _P_AQ), (_O_IQ, _P_IQ), (_O_BQ, _P_BQ), (_O_CQ, _P_CQ), (_O_CK, _P_CK)):
        cp(src, 1024, dst)
    for i in range(3):
        for g in range(B_KV_GROUPS):
            cp(bkv(i, 0, g), HEAD_DIM, _P_BK + (i * B_KV_GROUPS + g) * HEAD_DIM)
    for g in range(B_KV_GROUPS):
        cp(bkv(0, 1, g), HEAD_DIM, _P_BVC + g * HEAD_DIM)
    cp(_O_ALAT, A_LATENT, _P_ALAT)
    ov_ref[_PV_CV:_PV_CV + 1024, :] = w_ref[_O_CV:_O_CV + 1024, :].astype(bf16)
    for i in (1, 2):
        for g in range(B_KV_GROUPS):
            dst = _PV_BV + ((i - 1) * B_KV_GROUPS + g) * HEAD_DIM
            ov_ref[dst:dst + HEAD_DIM, :] = w_ref[bkv(i, 1, g):bkv(i, 1, g) + HEAD_DIM, :].astype(bf16)
    lane = lax.broadcasted_iota(i32, (w_ref.shape[1], 128), 1)
    assert _O_IW == _O_IK + IDX_DIM and _O_IK % 8 == 0 and _O_BG % 8 == 0
    blk = w_ref[_O_IK:_O_IK + 128, :].T
    o_ref[:, _P_IKW:_P_IKW + 128] = jnp.where(lane < IDX_DIM + IDX_HEADS, blk, 0.0).astype(bf16)
    blk = w_ref[_O_BG:_O_BG + 128, :].T
    o_ref[:, _P_BG:_P_BG + 128] = jnp.where(lane < 3 * B_HEADS, blk, 0.0).astype(bf16)


def _pack_w_in(w_in):
    depth, k, n = w_in.shape
    w_t = jnp.swapaxes(w_in, 1, 2)
    tc = 128
    return pl.pallas_call(
        _pack_w_in_kernel,
        grid=(depth, k // tc),
        in_specs=[pl.BlockSpec((None, n, tc), lambda l, i: (l, 0, i))],
        out_specs=[pl.BlockSpec((None, tc, _P_TOT), lambda l, i: (l, i, 0)),
                   pl.BlockSpec((None, _PV_TOT, tc), lambda l, i: (l, 0, i))],
        out_shape=[jax.ShapeDtypeStruct((depth, k, _P_TOT), bf16),
                   jax.ShapeDtypeStruct((depth, _PV_TOT, k), bf16)],
        compiler_params=_cparams(("parallel", "parallel")),
    )(w_t)


def _cover_t(L):
    n_cmp_pad = L // CMP_STRIDE
    starts = np.arange(n_cmp_pad) * CMP_STRIDE
    slc_start = np.arange(L // SLC_LEN) * SLC_LEN
    cover = ((starts[:, None] < slc_start[None, :] + SLC_LEN)
             & (starts[:, None] + CMP_LEN > slc_start[None, :])).astype(np.float32)
    n_cmp = (L - CMP_LEN) // CMP_STRIDE + 1
    cover[n_cmp:] = 0.0
    return jnp.asarray(cover.T)


def _token_mixing(u, h, B, L, l, lw, tabs, cov_t, lam_init):
    M = B * L
    G = B_KV_GROUPS
    aq, iq, bq, cq, ck, bk, alat_n, ikw, gates = _prep(h, lw['a_lat_g'], tabs, L)

    aw = A_HEADS * HEAD_DIM
    ak = _mm_rope(alat_n, lw['a_up'], l, aw, tabs['plain'], L, 512, aw)
    av_tc = _proj_t(alat_n, lw['a_up_vt'], l, 0, A_HEADS, HEAD_DIM, 512, B, L)
    ik = ikw[:, :IDX_DIM].astype(bf16).reshape(B, L, IDX_DIM)
    iw_t = ikw[:, IDX_DIM:IDX_DIM + IDX_HEADS].reshape(B, L, IDX_HEADS).transpose(0, 2, 1)
    mask_t = _dsa_mask(ik, iq.reshape(B, L, -1), iw_t)
    ya = _dsa_attn(aq.reshape(B, L, -1), ak.reshape(B, L, -1), av_tc, mask_t)

    bvc = h[:, _P_BVC:_P_BVC + G * HEAD_DIM].reshape(B, L, G, HEAD_DIM)
    bkc = bk[:, :G * HEAD_DIM].reshape(B, L, G, HEAD_DIM)
    n_row = L // CMP_STRIDE
    xcmp = jnp.stack([bkc, bvc], axis=1)
    xcmp = xcmp.transpose(0, 1, 3, 2, 4).reshape(B, 2, G, n_row, CMP_STRIDE * HEAD_DIM)
    cmp_out = _nsa_compress(xcmp, lw['cmp_w1'], lw['cmp_w2'], lw['cmp_pe'], l)
    kc = cmp_out[:, 0]
    vc_t = cmp_out[:, 1].transpose(0, 1, 3, 2)
    bv_tc = _proj_t(u, lw['w_vt'], l, _PV_BV, 2 * G, HEAD_DIM, 512, B, L)
    gates_t = gates[:, :3 * B_HEADS].reshape(B, L, G, 3 * B_HPG).transpose(0, 2, 3, 1)
    yb = _nsa_attn(bq.reshape(B, L, -1), kc, vc_t, bk.reshape(B, L, -1), bv_tc, gates_t, cov_t)

    cv_tc = _proj_t(u, lw['w_vt'], l, _PV_CV, C_HEADS, 2 * C_DIM, _DIFF_TK, B, L)
    yc = _diff_attn(cq.reshape(B, L, -1), ck.reshape(B, L, -1), cv_tc, lw['lam'],
                    lw['c_subln_g'].reshape(2 * C_DIM, 1), lam_init)

    return _merge(ya.reshape(M, -1), yb.reshape(M, -1), yc.reshape(M, -1), lw['w_br'], l, h)


def kernel(x, c, w_ada, b_ada, w_in, a_lat_g, a_up, cmp_w1, cmp_w2, cmp_pe, lam, c_subln_g, w_br, w_o,
           w_ffn_in, w_ffn_out, ln_g, ln_b):
    B, L, D = x.shape
    depth = w_ada.shape[0]
    M = B * L
    alpha = (2 * depth) ** 0.25

    c_pad = jnp.zeros((8, D), f32).at[:B].set(c)
    mod = _ada(c_pad, w_ada, b_ada)[:, :B]
    mods = [[mod[l, :, i * D:(i + 1) * D].reshape(B, 1, D) for i in range(6)] for l in range(depth)]

    assert HEAD_DIM == C_DIM
    tabs = dict(plain=_rope_tables(L, HEAD_DIM), query=_rope_tables(L, HEAD_DIM, HEAD_DIM ** -0.5 * _LOG2E),
                idx=_rope_tables(L, IDX_DIM))
    cov_t = _cover_t(L)

    w_in_p, w_vt = _pack_w_in(w_in)
    pe_flat = jnp.zeros((depth, 2, 8, CMP_LEN * HEAD_DIM), f32).at[:, :, 0].set(
        cmp_pe.reshape(depth, 2, CMP_LEN * HEAD_DIM)).astype(bf16)
    a_up_vt = jnp.swapaxes(a_up[:, :, A_HEADS * HEAD_DIM:], 1, 2).astype(bf16)
    wb = dict(a_up=a_up.astype(bf16), a_up_vt=a_up_vt, w_vt=w_vt, cmp_w1=cmp_w1.astype(bf16),
              cmp_w2=cmp_w2.astype(bf16), cmp_pe=pe_flat, w_br=w_br)
    w_o_b = w_o.astype(bf16)
    w_fo_b = w_ffn_out.astype(bf16)

    x2 = x.reshape(M, D)
    u = _lnmod(x2, mods[0][1], mods[0][0], L)
    for l in range(depth):
        lam_init = 0.8 - 0.6 * math.exp(-0.3 * l)
        sh_a, sc_a, g_a, sh_f, sc_f, g_f = mods[l]
        lw = dict(wb, a_lat_g=a_lat_g[l], lam=lam[l], c_subln_g=c_subln_g[l])
        h = _mm(u, w_in_p, l, 1024, 768, bf16)
        merged = _token_mixing(u, h, B, L, l, lw, tabs, cov_t, lam_init)
        x2, u = _mm_res_ln(merged, w_o_b, l, x2, g_a, ln_g[l, 0], ln_b[l, 0], sc_f, sh_f, L, alpha, True)
        f = _ffn_in(u, w_ffn_in, l)
        last = l == depth - 1
        nsc, nsh = (sc_f, sh_f) if last else (mods[l + 1][1], mods[l + 1][0])
        x2, u = _mm_res_ln(f, w_fo_b, l, x2, g_f, ln_g[l, 1], ln_b[l, 1], nsc, nsh, L, alpha, not last)
    return x2.reshape(B, L, D)
```

```python
import functools
import math

import numpy as np
import jax
import jax.numpy as jnp
from jax import lax
from jax.experimental import pallas as pl
from jax.experimental.pallas import tpu as pltpu

f32 = jnp.float32
bf16 = jnp.bfloat16
i32 = jnp.int32

D_MODEL = 2048
HEAD_DIM = 128
ROPE_THETA = 500000.0
NEG = -1e30
FORCE = 1e6
A_HEADS = 8
A_LATENT = 512
IDX_HEADS = 16
IDX_DIM = 64
DSA_TOPK = 256
B_HEADS = 8
B_KV_GROUPS = 2
B_HPG = B_HEADS // B_KV_GROUPS
CMP_LEN = 32
CMP_STRIDE = 16
SLC_LEN = 64
SLC_TOPN = 16
WIN_LEN = 512
C_HEADS = 4
C_DIM = 128
BRANCH_W = A_HEADS * HEAD_DIM
N_BRANCH = 3
D_FF = int(math.ceil(8 * D_MODEL / 3 / 256)) * 256

_O_AQ = 0
_O_ALAT = _O_AQ + A_HEADS * HEAD_DIM
_O_IQ = _O_ALAT + A_LATENT
_O_IK = _O_IQ + IDX_HEADS * IDX_DIM
_O_IW = _O_IK + IDX_DIM
_O_BQ = _O_IW + IDX_HEADS
_O_BKV = _O_BQ + B_HEADS * HEAD_DIM
_O_BG = _O_BKV + 3 * 2 * B_KV_GROUPS * HEAD_DIM
_O_CQ = _O_BG + 3 * B_HEADS
_O_CK = _O_CQ + C_HEADS * 2 * C_DIM
_O_CV = _O_CK + C_HEADS * 2 * C_DIM
_O_GL = _O_CV + C_HEADS * 2 * C_DIM
_N_IN = _O_GL + N_BRANCH * D_MODEL

_P_GL = 0
_P_BK = 6144
_P_BVC = 6912
_P_AQ = 7168
_P_IQ = 8192
_P_BQ = 9216
_P_CQ = 10240
_P_CK = 11264
_P_ALAT = 12288
_P_IKW = 12800
_P_BG = 12928
_P_TOT = 13056
_PV_CV = 0
_PV_BV = 1024
_PV_TOT = 1536

_VMEM_LIMIT = 48 * 1024 * 1024
_LOG2E = 1.4426950408889634
_INF = float("inf")
_ONES = 16
_RES_LN_TM = 512
_RES_LN_SUB = 256
_RADIX_FIRST = 26
_DIFF_TK = 1024
_DIFF_TQ = 512
_HEAD_GROUP = 2


def _cparams(sem, vmem=_VMEM_LIMIT):
    return pltpu.CompilerParams(dimension_semantics=sem, vmem_limit_bytes=vmem)


def _sigmoid(x):
    return 1.0 / (1.0 + jnp.exp(-x))


def _dot_nt(a, b):
    return lax.dot_general(a, b, (((1,), (1,)), ((), ())), preferred_element_type=f32)


def _ada_kernel(c_ref, w_ref, b_ref, o_ref):
    c = c_ref[...]
    cs = c * _sigmoid(c)
    o_ref[0] = jnp.dot(cs, w_ref[0], preferred_element_type=f32,
                       precision=lax.Precision.HIGHEST) + b_ref[0]


def _ada(c_pad, w_ada, b_ada):
    depth, d, n = w_ada.shape
    tn = 512
    return pl.pallas_call(
        _ada_kernel,
        grid=(depth, n // tn),
        in_specs=[pl.BlockSpec((8, d), lambda l, j: (0, 0)),
                  pl.BlockSpec((1, d, tn), lambda l, j: (l, 0, j)),
                  pl.BlockSpec((1, 1, tn), lambda l, j: (l, 0, j))],
        out_specs=pl.BlockSpec((1, 8, tn), lambda l, j: (l, 0, j)),
        out_shape=jax.ShapeDtypeStruct((depth, 8, n), f32),
        compiler_params=_cparams(("parallel", "parallel")),
    )(c_pad, w_ada, b_ada.reshape(depth, 1, n))


def _ln_rows(x, eps):
    mu = jnp.mean(x, axis=-1, keepdims=True)
    d = x - mu
    var = jnp.mean(d * d, axis=-1, keepdims=True)
    return d * lax.rsqrt(var + eps)


def _lnmod_kernel(x_ref, sc_ref, sh_ref, o_ref):
    y = _ln_rows(x_ref[...], 1e-5)
    o_ref[...] = (y * (1.0 + sc_ref[0]) + sh_ref[0]).astype(o_ref.dtype)


def _lnmod(x2, sc, sh, L):
    m, d = x2.shape
    tm = 512
    per_b = L // tm
    return pl.pallas_call(
        _lnmod_kernel,
        grid=(m // tm,),
        in_specs=[pl.BlockSpec((tm, d), lambda i: (i, 0)),
                  pl.BlockSpec((1, 1, d), lambda i: (i // per_b, 0, 0)),
                  pl.BlockSpec((1, 1, d), lambda i: (i // per_b, 0, 0))],
        out_specs=pl.BlockSpec((tm, d), lambda i: (i, 0)),
        out_shape=jax.ShapeDtypeStruct((m, d), bf16),
        compiler_params=_cparams(("parallel",)),
    )(x2, sc, sh)


def _mm_kernel(a_ref, w_ref, o_ref):
    o_ref[...] = jnp.dot(a_ref[...], w_ref[...], preferred_element_type=f32).astype(o_ref.dtype)


def _mm(a, w, l, tm, tn, out_dtype):
    m, k = a.shape
    n = w.shape[2]
    return pl.pallas_call(
        _mm_kernel,
        grid=(n // tn, m // tm),
        in_specs=[pl.BlockSpec((tm, k), lambda j, i: (i, 0)),
                  pl.BlockSpec((None, k, tn), lambda j, i: (l, 0, j))],
        out_specs=pl.BlockSpec((tm, tn), lambda j, i: (i, j)),
        out_shape=jax.ShapeDtypeStruct((m, n), out_dtype),
        compiler_params=_cparams(("parallel", "parallel")),
    )(a, w)


def _proj_t_kernel(w_ref, a_ref, *o_refs, row0s):
    res = _dot_nt(w_ref[...], a_ref[...])
    for o_ref, row0 in zip(o_refs, row0s):
        _, n, reps, dd, tk = o_ref.shape
        d = dd - _ONES
        for i in range(n):
            for r in range(reps):
                o_ref[0, i, r, :d, :] = res[row0 + i * d:row0 + (i + 1) * d,
                                            r * tk:(r + 1) * tk].astype(o_ref.dtype)
                o_ref[0, i, r, d:, :] = jnp.ones((_ONES, tk), o_ref.dtype)


def _proj_t(a, w_t, l, outs, tm, B, L):
    rows, k = w_t.shape[1], a.shape[1]
    nc = L // tm
    return pl.pallas_call(
        functools.partial(_proj_t_kernel, row0s=tuple(o[0] for o in outs)),
        grid=(B, nc),
        in_specs=[pl.BlockSpec((None, rows, k), lambda b, c: (l, 0, 0)),
                  pl.BlockSpec((tm, k), lambda b, c: (b * nc + c, 0))],
        out_specs=[pl.BlockSpec((1, n, tm // tk, d + _ONES, tk), lambda b, c: (b, 0, c, 0, 0))
                   for _, n, d, tk in outs],
        out_shape=[jax.ShapeDtypeStruct((B, n, L // tk, d + _ONES, tk), bf16) for _, n, d, tk in outs],
        compiler_params=_cparams(("parallel", "parallel")),
    )(w_t, a)


def _rope_low(shape, d, half):
    lane = lax.broadcasted_iota(i32, shape, 1)
    return (lane & (d - 1)) < half


def _rope_heads(x, c, s, low, half):
    partner = jnp.where(low, pltpu.roll(x, 128 - half, 1), pltpu.roll(x, half, 1))
    return x * c + partner * s


def _mm_rope_kernel(a_ref, w_ref, c_ref, s_ref, o_ref):
    acc = jnp.dot(a_ref[...], w_ref[...], preferred_element_type=f32)
    c, s = c_ref[...], s_ref[...]
    low = _rope_low(c.shape, HEAD_DIM, HEAD_DIM // 8)
    for h in range(acc.shape[1] // 128):
        sl = slice(h * 128, (h + 1) * 128)
        o_ref[:, sl] = _rope_heads(acc[:, sl], c, s, low, HEAD_DIM // 8).astype(o_ref.dtype)


def _mm_rope(a, w, l, n, tabs, L, tm, tn):
    m, k = a.shape
    per_b = L // tm
    tspec = pl.BlockSpec((tm, 128), lambda j, i: (i % per_b, 0))
    return pl.pallas_call(
        _mm_rope_kernel,
        grid=(n // tn, m // tm),
        in_specs=[pl.BlockSpec((tm, k), lambda j, i: (i, 0)),
                  pl.BlockSpec((None, k, tn), lambda j, i: (l, 0, j)),
                  tspec, tspec],
        out_specs=pl.BlockSpec((tm, tn), lambda j, i: (i, j)),
        out_shape=jax.ShapeDtypeStruct((m, n), bf16),
        compiler_params=_cparams(("parallel", "parallel")),
    )(a, w, *tabs)


def _prep_kernel(aq_ref, iq_ref, bq_ref, cq_ref, ck_ref, bk_ref, alat_ref, ikw_ref, bg_ref, alg_ref,
                 c_ref, s_ref, cq_ref_t, sq_ref_t, c6_ref, s6_ref, p128_ref, p64_ref,
                 aq_o, iq_o, bq_o, cq_o, ck_o, bk_o, alat_o, ikw_o, g_o):
    rows = 64

    def rope_all(src, dst, c_t, s_t, p_ref):
        for blk in range(src.shape[1] // 256):
            cols = slice(blk * 256, (blk + 1) * 256)
            partner = jnp.dot(src[:, cols], p_ref[...], preferred_element_type=f32)
            for r in range(src.shape[0] // rows):
                rs = slice(r * rows, (r + 1) * rows)
                c, s = c_t[rs, :], s_t[rs, :]
                for half in range(2):
                    sl = slice(blk * 256 + half * 128, blk * 256 + (half + 1) * 128)
                    y = src[rs, sl].astype(f32) * c + partner[rs, half * 128:(half + 1) * 128] * s
                    dst[rs, sl] = y.astype(dst.dtype)

    rope_all(aq_ref, aq_o, cq_ref_t, sq_ref_t, p128_ref)
    rope_all(bq_ref, bq_o, cq_ref_t, sq_ref_t, p128_ref)
    rope_all(cq_ref, cq_o, cq_ref_t, sq_ref_t, p128_ref)
    rope_all(ck_ref, ck_o, c_ref, s_ref, p128_ref)
    rope_all(bk_ref, bk_o, c_ref, s_ref, p128_ref)
    rope_all(iq_ref, iq_o, c6_ref, s6_ref, p64_ref)

    idx = (c6_ref, s6_ref, _rope_low((rows, 128), IDX_DIM, IDX_DIM // 8), IDX_DIM // 8)

    def rope(x, tabs, rs):
        return _rope_heads(x, tabs[0][rs, :], tabs[1][rs, :], tabs[2], tabs[3])

    a = alat_ref[...].astype(f32)
    ms = jnp.mean(a * a, axis=-1, keepdims=True)
    alat_o[...] = (a * lax.rsqrt(ms + 1e-6) * alg_ref[...]).astype(alat_o.dtype)

    isk = lax.broadcasted_iota(i32, (rows, 128), 1) < IDX_DIM
    for r in range(ikw_ref.shape[0] // rows):
        rs = slice(r * rows, (r + 1) * rows)
        x = ikw_ref[rs, :].astype(f32)
        mu = jnp.sum(jnp.where(isk, x, 0.0), axis=-1, keepdims=True) * (1.0 / IDX_DIM)
        d = jnp.where(isk, x - mu, 0.0)
        var = jnp.sum(d * d, axis=-1, keepdims=True) * (1.0 / IDX_DIM)
        y = d * lax.rsqrt(var + 1e-5)
        ikw_o[rs, :] = jnp.where(isk, rope(y, idx, rs), x * (IDX_DIM ** -0.5 * IDX_HEADS ** -0.5))

    g_o[...] = _sigmoid(bg_ref[...].astype(f32))


def _prep(h, a_lat_g, tabs, L):
    m = h.shape[0]
    tm = 512
    per_b = L // tm

    def hs(width, off):
        return pl.BlockSpec((tm, width), lambda i, _o=off // width: (i, _o))

    tspec = pl.BlockSpec((tm, 128), lambda i: (i % per_b, 0))
    pspec = pl.BlockSpec((256, 256), lambda i: (0, 0))

    def os(width):
        return pl.BlockSpec((tm, width), lambda i: (i, 0))

    outs = [(1024, bf16)] * 5 + [(768, bf16), (512, bf16), (128, f32), (128, f32)]
    return pl.pallas_call(
        _prep_kernel,
        grid=(m // tm,),
        in_specs=[hs(1024, _P_AQ), hs(1024, _P_IQ), hs(1024, _P_BQ), hs(1024, _P_CQ), hs(1024, _P_CK),
                  hs(768, _P_BK), hs(512, _P_ALAT), hs(128, _P_IKW), hs(128, _P_BG),
                  pl.BlockSpec((1, A_LATENT), lambda i: (0, 0))] + [tspec] * 6 + [pspec, pspec],
        out_specs=[os(w) for w, _ in outs],
        out_shape=[jax.ShapeDtypeStruct((m, w), dt) for w, dt in outs],
        compiler_params=_cparams(("parallel",)),
    )(h, h, h, h, h, h, h, h, h, a_lat_g.reshape(1, A_LATENT), *tabs['plain'], *tabs['query'], *tabs['idx'],
      _rope_perm(HEAD_DIM), _rope_perm(IDX_DIM))


def _f32_order_key(x):
    b = int(np.float32(x).view(np.int32))
    return b ^ ((b >> 31) & 0x7FFFFFFF)


_KEY_NEG = _f32_order_key(NEG)


def _dsa_mask_kernel(ik_ref, iq_ref, iw_ref, o_ref, key_ref, qp_ref, j_ref, thr_ref, cnt_ref,
                     *, L, tq, ksel):
    j = pl.program_id(1)
    ck = 512
    nck = L // ck
    nc = ((j + 1) * tq + ck - 1) // ck
    n_out_i = L - nc * ck
    n_out = n_out_i.astype(f32)
    t_row = j * tq + lax.broadcasted_iota(i32, (1, tq), 1)

    for hp in range(IDX_HEADS // 2):
        for e in range(2):
            h = 2 * hp + e
            qp_ref[hp, e * tq:(e + 1) * tq, :] = iq_ref[0, :, h * IDX_DIM:(h + 1) * IDX_DIM]

    def score_chunk(c, carry):
        off = pl.multiple_of(c * ck, ck)
        ikc = ik_ref[0, pl.ds(off, ck), :]
        acc = jnp.zeros((ck, tq), f32)
        for hp in range(IDX_HEADS // 2):
            s2 = _dot_nt(ikc, qp_ref[hp])
            acc = acc + jnp.maximum(s2[:, :tq], 0.0) * iw_ref[0, 2 * hp:2 * hp + 1, :]
            acc = acc + jnp.maximum(s2[:, tq:], 0.0) * iw_ref[0, 2 * hp + 1:2 * hp + 2, :]
        acc = jnp.where(acc == 0.0, 0.0, acc)
        sidx = off + lax.broadcasted_iota(i32, (ck, tq), 0)
        key_ref[pl.ds(off, ck), :] = jnp.where(sidx <= t_row, acc, NEG)
        return carry

    def score_pair(p, carry):
        score_chunk(2 * p, carry)
        return score_chunk(2 * p + 1, carry)

    lax.fori_loop(0, nc // 2, score_pair, 0)

    @pl.when(nc % 2 == 1)
    def _():
        score_chunk(nc - 1, 0)

    def as_f32(key):
        return lax.bitcast_convert_type(key ^ ((key >> 31) & 0x7FFFFFFF), f32)

    def count(pred_fn):
        def body(c, cnt):
            off = pl.multiple_of(c * ck, ck)
            k = key_ref[pl.ds(off, ck), :]
            sidx = off + lax.broadcasted_iota(i32, (ck, tq), 0)
            return cnt + jnp.sum(pred_fn(k, sidx).reshape(ck // 64, 64, tq), axis=0)
        part = lax.fori_loop(0, nc, body, jnp.zeros((64, tq), f32))
        return jnp.sum(part, axis=0, keepdims=True)

    kf = float(ksel)

    def bit_body(i, carry):
        thr, cnt_thr = carry
        cand = thr + lax.shift_left(jnp.int32(1), 31 - i)
        cand_f = as_f32(cand)
        cnt = count(lambda k, s: jnp.where(k >= cand_f, 1.0, 0.0)) + jnp.where(_KEY_NEG >= cand, n_out, 0.0)
        ok = cnt >= kf
        return jnp.where(ok, cand, thr), jnp.where(ok, cnt, cnt_thr)

    thr_key, cnt_ge = lax.fori_loop(0, _RADIX_FIRST, bit_body, (jnp.full((1, tq), -2 ** 31, i32),
                                                                 jnp.full((1, tq), float(L), f32)))
    thr_ref[...] = thr_key
    cnt_ref[...] = cnt_ge

    @pl.when(jnp.max(jnp.abs(cnt_ge - kf)) > 0.0)
    def _():
        t, c = lax.fori_loop(_RADIX_FIRST, 32, bit_body, (thr_key, cnt_ge))
        thr_ref[...] = t
        cnt_ref[...] = c

    thr_key, cnt_ge = thr_ref[...], cnt_ref[...]
    thr = as_f32(thr_key)

    j_ref[...] = jnp.full((1, tq), L, i32)

    @pl.when(jnp.max(cnt_ge) > kf)
    def _():
        nbits = L.bit_length() - 1
        cnt_gt = count(lambda k, s: jnp.where(k > thr, 1.0, 0.0)) + jnp.where(_KEY_NEG > thr_key, n_out, 0.0)
        need = kf - cnt_gt

        def jbit(i, cur):
            cand = cur | lax.shift_left(jnp.int32(1), nbits - 1 - i)
            f = count(lambda k, s: jnp.where(k == thr, jnp.where(s < cand, 1.0, 0.0), 0.0))
            f = f + jnp.where(thr_key == _KEY_NEG, jnp.clip(cand - nc * ck, 0, n_out_i).astype(f32), 0.0)
            return jnp.where(f < need, cand, cur)

        j_ref[...] = lax.fori_loop(0, nbits, jbit, jnp.zeros((1, tq), i32))

    jlast = j_ref[...]

    def write(c, carry):
        off = pl.multiple_of(c * ck, ck)
        k = key_ref[pl.ds(off, ck), :]
        sidx = off + lax.broadcasted_iota(i32, (ck, tq), 0)
        sel = jnp.where(k > thr, _INF, jnp.where(k == thr, jnp.where(sidx <= jlast, _INF, NEG), NEG))
        o_ref[0, pl.ds(off, ck), :] = jnp.where(sidx <= t_row, sel, NEG)
        return carry

    lax.fori_loop(0, nc, write, 0)

    def write_rest(c, carry):
        off = pl.multiple_of(c * ck, ck)
        o_ref[0, pl.ds(off, ck), :] = jnp.full((ck, tq), NEG, f32)
        return carry

    lax.fori_loop(nc, nck, write_rest, 0)


def _dsa_mask(ik, iq, iw_t):
    B, L, _ = iq.shape
    tq = 128
    ksel = min(DSA_TOPK, L // 4)
    return pl.pallas_call(
        functools.partial(_dsa_mask_kernel, L=L, tq=tq, ksel=ksel),
        grid=(B, L // tq),
        in_specs=[pl.BlockSpec((1, L, IDX_DIM), lambda b, j: (b, 0, 0)),
                  pl.BlockSpec((1, tq, IDX_HEADS * IDX_DIM), lambda b, j: (b, j, 0)),
                  pl.BlockSpec((1, IDX_HEADS, tq), lambda b, j: (b, 0, j))],
        out_specs=pl.BlockSpec((1, L, tq), lambda b, j: (b, 0, j)),
        out_shape=jax.ShapeDtypeStruct((B, L, L), f32),
        scratch_shapes=[pltpu.VMEM((L, tq), f32), pltpu.VMEM((IDX_HEADS // 2, 2 * tq, IDX_DIM), bf16),
                        pltpu.VMEM((1, tq), i32), pltpu.VMEM((1, tq), i32), pltpu.VMEM((1, tq), f32)],
        compiler_params=_cparams(("parallel", "parallel")),
    )(ik, iq, iw_t)


def _col_reduce(x, op):
    r, c = x.shape
    if r > 64:
        x = op(x.reshape(r // 64, 64, c), axis=0)
    return op(x, axis=0, keepdims=True)


def _flash_group(ss, cap, v_ts, m_refs, acc_refs):
    ss, mloc = _flash_mask_max(ss, cap)
    _flash_update(ss, mloc, cap is not None, v_ts, m_refs, acc_refs)


def _flash_mask_max(ss, cap):
    if cap is not None:
        ss = [jnp.minimum(s, cap) for s in ss]
    return ss, [_col_reduce(s, jnp.max) for s in ss]


def _flash_update(ss, mloc, masked, v_ts, m_refs, acc_refs):
    n = len(ss)
    m_prev = [r[...] for r in m_refs]
    m_new = [jnp.maximum(m_prev[i], mloc[i]) for i in range(n)]
    alpha = [jnp.exp2(m_prev[i] - m_new[i]) for i in range(n)]
    ps = [jnp.exp2((ss[i] - m_new[i]).astype(bf16)) for i in range(n)]
    pv = [jnp.dot(v_ts[i], ps[i], preferred_element_type=f32) for i in range(n)]
    for i in range(n):
        acc_new = acc_refs[i][...] * alpha[i] + pv[i]
        if masked:
            acc_new = jnp.where(m_new[i] <= NEG, 0.0, acc_new)
        acc_refs[i][...] = acc_new
        m_refs[i][...] = m_new[i]


def _flash_heads(score_fns, cap, v_ts, m_ref, acc_ref, group):
    n = len(score_fns)
    groups = [list(range(g, min(g + group, n))) for g in range(0, n, group)]
    nxt = _flash_mask_max([score_fns[h]() for h in groups[0]], cap)
    for gi, hs in enumerate(groups):
        ss, mloc = nxt
        if gi + 1 < len(groups):
            nxt = _flash_mask_max([score_fns[h]() for h in groups[gi + 1]], cap)
        _flash_update(ss, mloc, cap is not None, [v_ts[h] for h in hs], [m_ref.at[h] for h in hs],
                      [acc_ref.at[h] for h in hs])


def _flash_init(m_ref, acc_ref):
    m_ref[...] = jnp.full(m_ref.shape, NEG, f32)
    acc_ref[...] = jnp.zeros(acc_ref.shape, f32)


def _flash_result(acc, d):
    return acc[:d] / jnp.maximum(acc[d:d + 1], 1e-30)


def _dsa_attn_kernel(jm_ref, cm_ref, lm_ref, q_ref, k_ref, vt_ref, mask_ref, o_ref, m_ref, acc_ref):
    s = pl.program_id(1)

    @pl.when(cm_ref[s] == 0)
    def _():
        _flash_init(m_ref, acc_ref)

    cap = mask_ref[0]

    def scores(h):
        sl = slice(h * HEAD_DIM, (h + 1) * HEAD_DIM)
        return lambda: _dot_nt(k_ref[0, :, sl], q_ref[0, :, sl])

    v_ts = [vt_ref[0, h, 0] for h in range(A_HEADS)]
    _flash_heads([scores(h) for h in range(A_HEADS)], cap, v_ts, m_ref, acc_ref, _HEAD_GROUP)

    @pl.when(lm_ref[s] == 1)
    def _():
        for h in range(A_HEADS):
            o = _flash_result(acc_ref[h], HEAD_DIM)
            o_ref[0, :, h * HEAD_DIM:(h + 1) * HEAD_DIM] = o.T.astype(o_ref.dtype)


def _dsa_attn(q, k, v_t, mask_t):
    B, L, W = q.shape
    tq, tk = 512, 512
    pairs = [(j, c) for j in range(L // tq) for c in range(((j + 1) * tq - 1) // tk + 1)]
    jm = jnp.asarray([p[0] for p in pairs], i32)
    cm = jnp.asarray([p[1] for p in pairs], i32)
    lm = jnp.asarray([int(i + 1 == len(pairs) or pairs[i + 1][0] != p[0]) for i, p in enumerate(pairs)], i32)
    return pl.pallas_call(
        _dsa_attn_kernel,
        grid_spec=pltpu.PrefetchScalarGridSpec(
            num_scalar_prefetch=3,
            grid=(B, len(pairs)),
            in_specs=[pl.BlockSpec((1, tq, W), lambda b, s, jm, cm, lm: (b, jm[s], 0)),
                      pl.BlockSpec((1, tk, W), lambda b, s, jm, cm, lm: (b, cm[s], 0)),
                      pl.BlockSpec((1, A_HEADS, 1, HEAD_DIM + _ONES, tk),
                                   lambda b, s, jm, cm, lm: (b, 0, cm[s], 0, 0)),
                      pl.BlockSpec((1, tk, tq), lambda b, s, jm, cm, lm: (b, cm[s], jm[s]))],
            out_specs=pl.BlockSpec((1, tq, W), lambda b, s, jm, cm, lm: (b, jm[s], 0)),
            scratch_shapes=[pltpu.VMEM((A_HEADS, 1, tq), f32),
                            pltpu.VMEM((A_HEADS, HEAD_DIM + _ONES, tq), f32)]),
        out_shape=jax.ShapeDtypeStruct((B, L, W), bf16),
        compiler_params=_cparams(("parallel", "arbitrary")),
    )(jm, cm, lm, q, k, v_t, mask_t)


def _nsa_cmp_kernel(x_ref, w1_ref, w2_ref, pe_ref, o_ref):
    x = x_ref[0, 0, 0]
    w1 = w1_ref[0]
    half = CMP_STRIDE * HEAD_DIM
    a = jnp.dot(x, w1[:half], preferred_element_type=f32)
    b = jnp.dot(x, w1[half:], preferred_element_type=f32)
    pe = jnp.dot(pe_ref[0], w1, preferred_element_type=f32)[0:1]
    n = a.shape[0]
    pre = a + pltpu.roll(b, n - 1, 0) + pe
    act = pre * _sigmoid(pre)
    o_ref[0, 0, 0] = jnp.dot(act.astype(bf16), w2_ref[0], preferred_element_type=f32).astype(o_ref.dtype)


def _nsa_compress(x, w1, w2, pe, l):
    B, _, G, n, wd = x.shape
    return pl.pallas_call(
        _nsa_cmp_kernel,
        grid=(B, 2, G),
        in_specs=[pl.BlockSpec((1, 1, 1, n, wd), lambda b, t, g: (b, t, g, 0, 0)),
                  pl.BlockSpec((None, 1, CMP_LEN * HEAD_DIM, HEAD_DIM), lambda b, t, g: (l, t, 0, 0)),
                  pl.BlockSpec((None, 1, HEAD_DIM, HEAD_DIM), lambda b, t, g: (l, t, 0, 0)),
                  pl.BlockSpec((None, 1, 8, CMP_LEN * HEAD_DIM), lambda b, t, g: (l, t, 0, 0))],
        out_specs=pl.BlockSpec((1, 1, 1, n, HEAD_DIM), lambda b, t, g: (b, t, g, 0, 0)),
        out_shape=jax.ShapeDtypeStruct((B, 2, G, n, HEAD_DIM), bf16),
        compiler_params=_cparams(("parallel", "parallel", "parallel")),
    )(x, w1, w2, pe)


def _nsa_attn_kernel(q_ref, kc_ref, vct_ref, ks_ref, vst_ref, kw_ref, vwt_ref, g_ref, cov_ref, o_ref,
                     m_ref, acc_ref, out_ref, sel_ref, *, L, tq, tk, n_sel):
    j = pl.program_id(2)
    t_row = j * tq + lax.broadcasted_iota(i32, (1, tq), 1)
    n_cmp = kc_ref.shape[2]
    n_slc = L // SLC_LEN
    qs = [q_ref[0, :, hh * HEAD_DIM:(hh + 1) * HEAD_DIM] for hh in range(B_HPG)]

    def gate(hh, i):
        return g_ref[0, 0, hh * 3 + i:hh * 3 + i + 1, :]

    kc = kc_ref[0, 0]
    vct = vct_ref[0, 0]
    cend = lax.broadcasted_iota(i32, (n_cmp, tq), 0) * CMP_STRIDE + (CMP_LEN - 1)
    mc = cend <= t_row
    ss = [jnp.where(mc, _dot_nt(kc, qs[hh]), NEG) for hh in range(B_HPG)]
    mxs = [_col_reduce(s, jnp.max) for s in ss]
    ps = [jnp.where(mc, jnp.exp2(ss[hh] - mxs[hh]), 0.0) for hh in range(B_HPG)]
    inv = [1.0 / jnp.maximum(_col_reduce(p, jnp.sum), 1e-30) for p in ps]
    pvs = [jnp.dot(vct, p.astype(bf16), preferred_element_type=f32) for p in ps]
    psum = ps[0] * inv[0]
    for hh in range(1, B_HPG):
        psum = psum + ps[hh] * inv[hh]
    for hh in range(B_HPG):
        out_ref[hh] = (gate(hh, 0) * inv[hh]) * pvs[hh]
    imp = jnp.dot(cov_ref[...], psum, preferred_element_type=f32, precision=lax.Precision.HIGHEST)
    blk = lax.broadcasted_iota(i32, (n_slc, tq), 0)
    cur = t_row >> (SLC_LEN.bit_length() - 1)
    forced = (blk == 0) | (blk == cur) | (blk == cur - 1)
    imp = jnp.where(forced, FORCE, jnp.where(blk <= cur, imp, NEG))
    rank = jnp.zeros((n_slc, tq), f32)
    for r in range(n_slc):
        row = imp[r:r + 1, :]
        rank = rank + jnp.where(row > imp, 1.0, jnp.where(row == imp, jnp.where(blk > r, 1.0, 0.0), 0.0))
    sel_ref[...] = jnp.where(rank < float(n_sel), _INF, NEG)

    def finish(i):
        for hh in range(B_HPG):
            out_ref[hh] = out_ref[hh] + gate(hh, i) * _flash_result(acc_ref[hh], HEAD_DIM)

    def run_branch(k_ref, vt_ref, c_lo, c_hi, mask_fn):
        _flash_init(m_ref, acc_ref)

        def body(c, carry):
            off = pl.multiple_of(c * tk, tk)
            kch = k_ref[0, pl.ds(off, tk), :]
            vch = vt_ref[0, 0, c]
            sidx = off + lax.broadcasted_iota(i32, (tk, tq), 0)
            cap = mask_fn(c, sidx)
            fns = [functools.partial(_dot_nt, kch, qs[hh]) for hh in range(B_HPG)]
            _flash_heads(fns, cap, [vch] * B_HPG, m_ref, acc_ref, _HEAD_GROUP)
            return carry

        lax.fori_loop(c_lo, c_hi, body, 0)

    bpc = tk // SLC_LEN

    def slc_mask(c, sidx):
        rows = [jnp.broadcast_to(sel_ref[pl.ds(c * bpc + b, 1), :], (SLC_LEN, tq)) for b in range(bpc)]
        return jnp.where(sidx <= t_row, jnp.concatenate(rows, axis=0), NEG)

    c_hi = ((j + 1) * tq + tk - 1) // tk
    run_branch(ks_ref, vst_ref, 0, c_hi, slc_mask)
    finish(1)

    def win_mask(c, sidx):
        d = t_row - sidx
        return jnp.where(d >= 0, jnp.where(d < WIN_LEN, _INF, NEG), NEG)

    c_lo = jnp.maximum(j * tq - (WIN_LEN - 1), 0) // tk
    run_branch(kw_ref, vwt_ref, c_lo, c_hi, win_mask)
    finish(2)

    for hh in range(B_HPG):
        o_ref[0, :, hh * HEAD_DIM:(hh + 1) * HEAD_DIM] = out_ref[hh].T.astype(o_ref.dtype)


def _nsa_attn(bq, kc, vc_t, bk, bv_tc, gates_t, cov_t):
    B, L, _ = bq.shape
    G = B_KV_GROUPS
    tq, tk = 512, 512
    n_cmp = kc.shape[2]
    n_slc = L // SLC_LEN
    n_sel = min(SLC_TOPN, n_slc)
    gw = B_HPG * HEAD_DIM
    nc = L // tk
    return pl.pallas_call(
        functools.partial(_nsa_attn_kernel, L=L, tq=tq, tk=tk, n_sel=n_sel),
        grid=(B, G, L // tq),
        in_specs=[pl.BlockSpec((1, tq, gw), lambda b, g, j: (b, j, g)),
                  pl.BlockSpec((1, 1, n_cmp, HEAD_DIM), lambda b, g, j: (b, g, 0, 0)),
                  pl.BlockSpec((1, 1, HEAD_DIM, n_cmp), lambda b, g, j: (b, g, 0, 0)),
                  pl.BlockSpec((1, L, HEAD_DIM), lambda b, g, j: (b, 0, 2 + g)),
                  pl.BlockSpec((1, 1, nc, HEAD_DIM + _ONES, tk), lambda b, g, j: (b, g, 0, 0, 0)),
                  pl.BlockSpec((1, L, HEAD_DIM), lambda b, g, j: (b, 0, 4 + g)),
                  pl.BlockSpec((1, 1, nc, HEAD_DIM + _ONES, tk),
                               lambda b, g, j: (b, B_KV_GROUPS + g, 0, 0, 0)),
                  pl.BlockSpec((1, 1, 3 * B_HPG, tq), lambda b, g, j: (b, g, 0, j)),
                  pl.BlockSpec((n_slc, n_cmp), lambda b, g, j: (0, 0))],
        out_specs=pl.BlockSpec((1, tq, gw), lambda b, g, j: (b, j, g)),
        out_shape=jax.ShapeDtypeStruct((B, L, B_HEADS * HEAD_DIM), bf16),
        scratch_shapes=[pltpu.VMEM((B_HPG, 1, tq), f32),
                        pltpu.VMEM((B_HPG, HEAD_DIM + _ONES, tq), f32),
                        pltpu.VMEM((B_HPG, HEAD_DIM, tq), f32), pltpu.VMEM((n_slc, tq), f32)],
        compiler_params=_cparams(("parallel", "parallel", "parallel")),
    )(bq, kc, vc_t, bk, bv_tc, bk, bv_tc, gates_t, cov_t)


def _diff_attn_kernel(q_ref, k_ref, vt_ref, lam_ref, g_ref, o_ref, m_ref, acc_ref,
                      *, tq, tk, lam_init):
    j = pl.program_id(2)
    t_row = j * tq + lax.broadcasted_iota(i32, (1, tq), 1)
    qs = [q_ref[0, :, mi * C_DIM:(mi + 1) * C_DIM] for mi in range(2)]
    _flash_init(m_ref, acc_ref)

    def chunk(c, lo, width, masked):
        off = pl.multiple_of(c * tk + lo, tq)
        vch = vt_ref[0, 0, c, :, lo:lo + width]
        cap = None
        if masked:
            sidx = off + lax.broadcasted_iota(i32, (width, tq), 0)
            cap = jnp.where(sidx <= t_row, _INF, NEG)
        ss = [_dot_nt(k_ref[0, pl.ds(off, width), mi * C_DIM:(mi + 1) * C_DIM], qs[mi]) for mi in range(2)]
        _flash_group(ss, cap, [vch] * 2, [m_ref.at[mi] for mi in range(2)],
                     [acc_ref.at[mi] for mi in range(2)])

    def body(c, carry):
        chunk(c, 0, tk, False)
        return carry

    c_diag = (j * tq) // tk
    lax.fori_loop(0, c_diag, body, 0)

    @pl.when(j % 2 == 0)
    def _():
        chunk(c_diag, 0, tq, True)

    @pl.when(j % 2 == 1)
    def _():
        chunk(c_diag, 0, tq, False)
        chunk(c_diag, tq, tq, True)

    lam = lam_ref[...]
    lam_val = (jnp.exp(jnp.sum(lam[0:1] * lam[1:2], axis=1, keepdims=True))
               - jnp.exp(jnp.sum(lam[2:3] * lam[3:4], axis=1, keepdims=True)) + lam_init)
    o = _flash_result(acc_ref[0], 2 * C_DIM) - lam_val * _flash_result(acc_ref[1], 2 * C_DIM)
    ms = jnp.mean(o * o, axis=0, keepdims=True)
    y = o * lax.rsqrt(ms + 1e-6) * g_ref[...] * (1.0 - lam_init)
    o_ref[0] = y.T.astype(o_ref.dtype)


def _diff_attn(cq, ck, cv_tc, lam, g_col, lam_init):
    B, L, _ = cq.shape
    tq, tk = _DIFF_TQ, cv_tc.shape[-1]
    assert tk == 2 * tq
    hw = 2 * C_DIM
    nc = L // tk
    return pl.pallas_call(
        functools.partial(_diff_attn_kernel, tq=tq, tk=tk, lam_init=lam_init),
        grid=(B, C_HEADS, L // tq),
        in_specs=[pl.BlockSpec((1, tq, hw), lambda b, h, j: (b, j, h)),
                  pl.BlockSpec((1, L, hw), lambda b, h, j: (b, 0, h)),
                  pl.BlockSpec((1, 1, nc, hw + _ONES, tk), lambda b, h, j: (b, h, 0, 0, 0)),
                  pl.BlockSpec((4, C_DIM), lambda b, h, j: (0, 0)),
                  pl.BlockSpec((hw, 1), lambda b, h, j: (0, 0))],
        out_specs=pl.BlockSpec((1, tq, hw), lambda b, h, j: (b, j, h)),
        out_shape=jax.ShapeDtypeStruct((B, L, C_HEADS * hw), bf16),
        scratch_shapes=[pltpu.VMEM((2, 1, tq), f32), pltpu.VMEM((2, hw + _ONES, tq), f32)],
        compiler_params=_cparams(("parallel", "parallel", "parallel")),
    )(cq, ck, cv_tc, lam, g_col)


def _merge_kernel(ya_ref, yb_ref, yc_ref, w_ref, g0_ref, g1_ref, g2_ref, o_ref, wb_ref):
    @pl.when(pl.program_id(1) == 0)
    def _():
        wb_ref[...] = w_ref[...].astype(bf16)

    acc = None
    for r, (y_ref, g_ref) in enumerate(((ya_ref, g0_ref), (yb_ref, g1_ref), (yc_ref, g2_ref))):
        br = jnp.dot(y_ref[...], wb_ref[r], preferred_element_type=f32)
        t = _sigmoid(g_ref[...].astype(f32)) * br
        acc = t if acc is None else acc + t
    o_ref[...] = acc.astype(o_ref.dtype)


def _merge(ya, yb, yc, w_br, l, h):
    m, kw = ya.shape
    tm, tn = 1024, 512
    npb = D_MODEL // tn
    yspec = pl.BlockSpec((tm, kw), lambda j, i: (i, 0))

    def gspec(r):
        return pl.BlockSpec((tm, tn), lambda j, i, _r=r: (i, _r * npb + j))

    return pl.pallas_call(
        _merge_kernel,
        grid=(npb, m // tm),
        in_specs=[yspec, yspec, yspec,
                  pl.BlockSpec((None, N_BRANCH, kw, tn), lambda j, i: (l, 0, 0, j)),
                  gspec(0), gspec(1), gspec(2)],
        out_specs=pl.BlockSpec((tm, tn), lambda j, i: (i, j)),
        out_shape=jax.ShapeDtypeStruct((m, D_MODEL), bf16),
        scratch_shapes=[pltpu.VMEM((N_BRANCH, kw, tn), bf16)],
        compiler_params=_cparams(("parallel", "arbitrary")),
    )(ya, yb, yc, w_br, h, h, h)


def _mm_res_ln_kernel(a_ref, w_ref, x_ref, gate_ref, lg_ref, lb_ref, sc_ref, sh_ref, xo_ref, *u_refs,
                      alpha, sub):
    tm = a_ref.shape[0]
    for r in range(tm // sub):
        rows = slice(r * sub, (r + 1) * sub)
        y = jnp.dot(a_ref[rows, :], w_ref[...], preferred_element_type=f32)
        z = alpha * x_ref[rows, :] + gate_ref[0] * y
        xn = _ln_rows(z, 1e-5) * lg_ref[...] + lb_ref[...]
        xo_ref[rows, :] = xn
        if u_refs:
            u_refs[0][rows, :] = (_ln_rows(xn, 1e-5) * (1.0 + sc_ref[0]) + sh_ref[0]).astype(bf16)


def _mm_res_ln(a, w, l, x2, gate, ln_g, ln_b, sc, sh, L, alpha, emit_u):
    m, kdim = a.shape
    d = w.shape[2]
    tm = _RES_LN_TM if kdim * d * 2 <= 16 * 1024 * 1024 else _RES_LN_SUB
    per_b = L // tm
    bspec = pl.BlockSpec((1, 1, d), lambda i: (i // per_b, 0, 0))
    vspec = pl.BlockSpec((1, d), lambda i: (0, 0))
    rspec = pl.BlockSpec((tm, d), lambda i: (i, 0))
    out_shape = [jax.ShapeDtypeStruct((m, d), f32)]
    out_specs = [rspec]
    if emit_u:
        out_shape.append(jax.ShapeDtypeStruct((m, d), bf16))
        out_specs.append(rspec)
    res = pl.pallas_call(
        functools.partial(_mm_res_ln_kernel, alpha=alpha, sub=_RES_LN_SUB),
        grid=(m // tm,),
        in_specs=[pl.BlockSpec((tm, kdim), lambda i: (i, 0)),
                  pl.BlockSpec((None, kdim, d), lambda i: (l, 0, 0), pipeline_mode=pl.Buffered(1)),
                  rspec, bspec, vspec, vspec, bspec, bspec],
        out_specs=out_specs,
        out_shape=out_shape,
        compiler_params=_cparams(("parallel",), 60 * 1024 * 1024),
    )(a, w, x2, gate, ln_g.reshape(1, d), ln_b.reshape(1, d), sc, sh)
    return res if emit_u else (res[0], None)


def _ffn_in_kernel(a_ref, wg_ref, wu_ref, o_ref, wgb_ref, wub_ref):
    @pl.when(pl.program_id(1) == 0)
    def _():
        wgb_ref[...] = wg_ref[...].astype(bf16)
        wub_ref[...] = wu_ref[...].astype(bf16)

    a = a_ref[...]
    g = jnp.dot(a, wgb_ref[...], preferred_element_type=f32)
    u = jnp.dot(a, wub_ref[...], preferred_element_type=f32)
    o_ref[...] = (g * _sigmoid(g) * u).astype(o_ref.dtype)


def _ffn_in(a, w, l):
    m, k = a.shape
    tm, tn = 1024, 512
    nb = D_FF // tn
    return pl.pallas_call(
        _ffn_in_kernel,
        grid=(nb, m // tm),
        in_specs=[pl.BlockSpec((tm, k), lambda j, i: (i, 0)),
                  pl.BlockSpec((None, k, tn), lambda j, i: (l, 0, j)),
                  pl.BlockSpec((None, k, tn), lambda j, i: (l, 0, nb + j))],
        out_specs=pl.BlockSpec((tm, tn), lambda j, i: (i, j)),
        out_shape=jax.ShapeDtypeStruct((m, D_FF), bf16),
        scratch_shapes=[pltpu.VMEM((k, tn), bf16), pltpu.VMEM((k, tn), bf16)],
        compiler_params=_cparams(("parallel", "arbitrary")),
    )(a, w, w)


def _rope_tables(L, d, mult=1.0):
    r = d // 4
    half = r // 2
    inv = ROPE_THETA ** (-(jnp.arange(half, dtype=f32) * 2.0) / r)
    ang = jnp.arange(L).astype(f32)[:, None] * inv[None, :]
    cos, sin = jnp.cos(ang), jnp.sin(ang)
    c = jnp.concatenate([cos, cos, jnp.ones((L, d - r), f32)], axis=1)
    s = jnp.concatenate([-sin, sin, jnp.zeros((L, d - r), f32)], axis=1)
    rep = 128 // d
    return tuple(jnp.tile(t * mult, (1, rep)) for t in (c, s))


def _rope_perm(d):
    half = d // 8
    p = np.zeros((256, 256), np.float32)
    for i in range(256):
        li = i % d
        if li < half:
            p[i + half, i] = 1.0
        elif li < 2 * half:
            p[i - half, i] = 1.0
    return jnp.asarray(p, bf16)


def _pack_w_in_kernel(w_ref, o_ref, ov_ref):
    def cp(src, width, dst):
        for o in range(0, width, 1024):
            wd = min(1024, width - o)
            o_ref[:, dst + o:dst + o + wd] = w_ref[src + o:src + o + wd, :].T.astype(bf16)

    def bkv(i, kv, g):
        return _O_BKV + ((i * 2 + kv) * B_KV_GROUPS + g) * HEAD_DIM

    cp(_O_GL, N_BRANCH * D_MODEL, _P_GL)
    for src, dst in ((_O_AQ, _P_AQ), (_O_IQ, _P_IQ), (_O_BQ, _P_BQ), (_O_CQ, _P_CQ), (_O_CK, _P_CK)):
        cp(src, 1024, dst)
    for i in range(3):
        for g in range(B_KV_GROUPS):
            cp(bkv(i, 0, g), HEAD_DIM, _P_BK + (i * B_KV_GROUPS + g) * HEAD_DIM)
    for g in range(B_KV_GROUPS):
        cp(bkv(0, 1, g), HEAD_DIM, _P_BVC + g * HEAD_DIM)
    cp(_O_ALAT, A_LATENT, _P_ALAT)
    ov_ref[_PV_CV:_PV_CV + 1024, :] = w_ref[_O_CV:_O_CV + 1024, :].astype(bf16)
    for i in (1, 2):
        for g in range(B_KV_GROUPS):
            dst = _PV_BV + ((i - 1) * B_KV_GROUPS + g) * HEAD_DIM
            ov_ref[dst:dst + HEAD_DIM, :] = w_ref[bkv(i, 1, g):bkv(i, 1, g) + HEAD_DIM, :].astype(bf16)
    lane = lax.broadcasted_iota(i32, (w_ref.shape[1], 128), 1)
    assert _O_IW == _O_IK + IDX_DIM and _O_IK % 8 == 0 and _O_BG % 8 == 0
    blk = w_ref[_O_IK:_O_IK + 128, :].T
    o_ref[:, _P_IKW:_P_IKW + 128] = jnp.where(lane < IDX_DIM + IDX_HEADS, blk, 0.0).astype(bf16)
    blk = w_ref[_O_BG:_O_BG + 128, :].T
    o_ref[:, _P_BG:_P_BG + 128] = jnp.where(lane < 3 * B_HEADS, blk, 0.0).astype(bf16)


def _pack_w_in(w_in):
    depth, k, n = w_in.shape
    w_t = jnp.swapaxes(w_in, 1, 2)
    tc = 128
    return pl.pallas_call(
        _pack_w_in_kernel,
        grid=(depth, k // tc),
        in_specs=[pl.BlockSpec((None, n, tc), lambda l, i: (l, 0, i))],
        out_specs=[pl.BlockSpec((None, tc, _P_TOT), lambda l, i: (l, i, 0)),
                   pl.BlockSpec((None, _PV_TOT, tc), lambda l, i: (l, 0, i))],
        out_shape=[jax.ShapeDtypeStruct((depth, k, _P_TOT), bf16),
                   jax.ShapeDtypeStruct((depth, _PV_TOT, k), bf16)],
        compiler_params=_cparams(("parallel", "parallel")),
    )(w_t)


def _cover_t(L):
    n_cmp_pad = L // CMP_STRIDE
    starts = np.arange(n_cmp_pad) * CMP_STRIDE
    slc_start = np.arange(L // SLC_LEN) * SLC_LEN
    cover = ((starts[:, None] < slc_start[None, :] + SLC_LEN)
             & (starts[:, None] + CMP_LEN > slc_start[None, :])).astype(np.float32)
    n_cmp = (L - CMP_LEN) // CMP_STRIDE + 1
    cover[n_cmp:] = 0.0
    return jnp.asarray(cover.T)


def _token_mixing(u, h, B, L, l, lw, tabs, cov_t, lam_init):
    M = B * L
    G = B_KV_GROUPS
    aq, iq, bq, cq, ck, bk, alat_n, ikw, gates = _prep(h, lw['a_lat_g'], tabs, L)

    aw = A_HEADS * HEAD_DIM
    ak = _mm_rope(alat_n, lw['a_up'], l, aw, tabs['plain'], L, 512, aw)
    av_tc, = _proj_t(alat_n, lw['a_up_vt'], l, [(0, A_HEADS, HEAD_DIM, 512)], 512, B, L)
    ik = ikw[:, :IDX_DIM].astype(bf16).reshape(B, L, IDX_DIM)
    iw_t = ikw[:, IDX_DIM:IDX_DIM + IDX_HEADS].reshape(B, L, IDX_HEADS).transpose(0, 2, 1)
    mask_t = _dsa_mask(ik, iq.reshape(B, L, -1), iw_t)
    ya = _dsa_attn(aq.reshape(B, L, -1), ak.reshape(B, L, -1), av_tc, mask_t)

    bvc = h[:, _P_BVC:_P_BVC + G * HEAD_DIM].reshape(B, L, G, HEAD_DIM)
    bkc = bk[:, :G * HEAD_DIM].reshape(B, L, G, HEAD_DIM)
    n_row = L // CMP_STRIDE
    xcmp = jnp.stack([bkc, bvc], axis=1)
    xcmp = xcmp.transpose(0, 1, 3, 2, 4).reshape(B, 2, G, n_row, CMP_STRIDE * HEAD_DIM)
    cmp_out = _nsa_compress(xcmp, lw['cmp_w1'], lw['cmp_w2'], lw['cmp_pe'], l)
    kc = cmp_out[:, 0]
    vc_t = cmp_out[:, 1].transpose(0, 1, 3, 2)
    cv_tc, bv_tc = _proj_t(u, lw['w_vt'], l, [(_PV_CV, C_HEADS, 2 * C_DIM, _DIFF_TK),
                                             (_PV_BV, 2 * G, HEAD_DIM, 512)], _DIFF_TK, B, L)
    gates_t = gates[:, :3 * B_HEADS].reshape(B, L, G, 3 * B_HPG).transpose(0, 2, 3, 1)
    yb = _nsa_attn(bq.reshape(B, L, -1), kc, vc_t, bk.reshape(B, L, -1), bv_tc, gates_t, cov_t)

    yc = _diff_attn(cq.reshape(B, L, -1), ck.reshape(B, L, -1), cv_tc, lw['lam'],
                    lw['c_subln_g'].reshape(2 * C_DIM, 1), lam_init)

    return _merge(ya.reshape(M, -1), yb.reshape(M, -1), yc.reshape(M, -1), lw['w_br'], l, h)


def kernel(x, c, w_ada, b_ada, w_in, a_lat_g, a_up, cmp_w1, cmp_w2, cmp_pe, lam, c_subln_g, w_br, w_o,
           w_ffn_in, w_ffn_out, ln_g, ln_b):
    B, L, D = x.shape
    depth = w_ada.shape[0]
    M = B * L
    alpha = (2 * depth) ** 0.25

    c_pad = jnp.zeros((8, D), f32).at[:B].set(c)
    mod = _ada(c_pad, w_ada, b_ada)[:, :B]
    mods = [[mod[l, :, i * D:(i + 1) * D].reshape(B, 1, D) for i in range(6)] for l in range(depth)]

    assert HEAD_DIM == C_DIM
    tabs = dict(plain=_rope_tables(L, HEAD_DIM), query=_rope_tables(L, HEAD_DIM, HEAD_DIM ** -0.5 * _LOG2E),
                idx=_rope_tables(L, IDX_DIM))
    cov_t = _cover_t(L)

    w_in_p, w_vt = _pack_w_in(w_in)
    pe_flat = jnp.zeros((depth, 2, 8, CMP_LEN * HEAD_DIM), f32).at[:, :, 0].set(
        cmp_pe.reshape(depth, 2, CMP_LEN * HEAD_DIM)).astype(bf16)
    a_up_vt = jnp.swapaxes(a_up[:, :, A_HEADS * HEAD_DIM:], 1, 2).astype(bf16)
    wb = dict(a_up=a_up.astype(bf16), a_up_vt=a_up_vt, w_vt=w_vt, cmp_w1=cmp_w1.astype(bf16),
              cmp_w2=cmp_w2.astype(bf16), cmp_pe=pe_flat, w_br=w_br)
    w_o_b = w_o.astype(bf16)
    w_fo_b = w_ffn_out.astype(bf16)

    x2 = x.reshape(M, D)
    u = _lnmod(x2, mods[0][1], mods[0][0], L)
    for l in range(depth):
        lam_init = 0.8 - 0.6 * math.exp(-0.3 * l)
        sh_a, sc_a, g_a, sh_f, sc_f, g_f = mods[l]
        lw = dict(wb, a_lat_g=a_lat_g[l], lam=lam[l], c_subln_g=c_subln_g[l])
        h = _mm(u, w_in_p, l, 1024, 768, bf16)
        merged = _token_mixing(u, h, B, L, l, lw, tabs, cov_t, lam_init)
        x2, u = _mm_res_ln(merged, w_o_b, l, x2, g_a, ln_g[l, 0], ln_b[l, 0], sc_f, sh_f, L, alpha, True)
        f = _ffn_in(u, w_ffn_in, l)
        last = l == depth - 1
        nsc, nsh = (sc_f, sh_f) if last else (mods[l + 1][1], mods[l + 1][0])
        x2, u = _mm_res_ln(f, w_fo_b, l, x2, g_f, ln_g[l, 1], ln_b[l, 1], nsc, nsh, L, alpha, not last)
    return x2.reshape(B, L, D)
```

```python
import functools
import math

import numpy as np
import jax
import jax.numpy as jnp
from jax import lax
from jax.experimental import pallas as pl
from jax.experimental.pallas import tpu as pltpu

f32 = jnp.float32
bf16 = jnp.bfloat16
i32 = jnp.int32

D_MODEL = 2048
HEAD_DIM = 128
ROPE_THETA = 500000.0
NEG = -1e30
FORCE = 1e6
A_HEADS = 8
A_LATENT = 512
IDX_HEADS = 16
IDX_DIM = 64
DSA_TOPK = 256
B_HEADS = 8
B_KV_GROUPS = 2
B_HPG = B_HEADS // B_KV_GROUPS
CMP_LEN = 32
CMP_STRIDE = 16
SLC_LEN = 64
SLC_TOPN = 16
WIN_LEN = 512
C_HEADS = 4
C_DIM = 128
BRANCH_W = A_HEADS * HEAD_DIM
N_BRANCH = 3
D_FF = int(math.ceil(8 * D_MODEL / 3 / 256)) * 256

_O_AQ = 0
_O_ALAT = _O_AQ + A_HEADS * HEAD_DIM
_O_IQ = _O_ALAT + A_LATENT
_O_IK = _O_IQ + IDX_HEADS * IDX_DIM
_O_IW = _O_IK + IDX_DIM
_O_BQ = _O_IW + IDX_HEADS
_O_BKV = _O_BQ + B_HEADS * HEAD_DIM
_O_BG = _O_BKV + 3 * 2 * B_KV_GROUPS * HEAD_DIM
_O_CQ = _O_BG + 3 * B_HEADS
_O_CK = _O_CQ + C_HEADS * 2 * C_DIM
_O_CV = _O_CK + C_HEADS * 2 * C_DIM
_O_GL = _O_CV + C_HEADS * 2 * C_DIM
_N_IN = _O_GL + N_BRANCH * D_MODEL

_P_GL = 0
_P_BK = 6144
_P_BVC = 6912
_P_AQ = 7168
_P_IQ = 8192
_P_BQ = 9216
_P_CQ = 10240
_P_CK = 11264
_P_ALAT = 12288
_P_IKW = 12800
_P_BG = 12928
_P_TOT = 13056
_PV_CV = 0
_PV_BV = 1024
_PV_TOT = 1536

_VMEM_LIMIT = 48 * 1024 * 1024
_LOG2E = 1.4426950408889634
_INF = float("inf")
_ONES = 16
_RES_LN_TM = 512
_RES_LN_SUB = 256
_RADIX_FIRST = 26
_DIFF_TK = 1024
_DIFF_TQ = 512
_HEAD_GROUP = 2


def _cparams(sem, vmem=_VMEM_LIMIT):
    return pltpu.CompilerParams(dimension_semantics=sem, vmem_limit_bytes=vmem)


def _sigmoid(x):
    return 1.0 / (1.0 + jnp.exp(-x))


def _dot_nt(a, b):
    return lax.dot_general(a, b, (((1,), (1,)), ((), ())), preferred_element_type=f32)


def _ada_kernel(c_ref, w_ref, b_ref, o_ref):
    c = c_ref[...]
    cs = c * _sigmoid(c)
    o_ref[0] = jnp.dot(cs, w_ref[0], preferred_element_type=f32,
                       precision=lax.Precision.HIGHEST) + b_ref[0]


def _ada(c_pad, w_ada, b_ada):
    depth, d, n = w_ada.shape
    tn = 512
    return pl.pallas_call(
        _ada_kernel,
        grid=(depth, n // tn),
        in_specs=[pl.BlockSpec((8, d), lambda l, j: (0, 0)),
                  pl.BlockSpec((1, d, tn), lambda l, j: (l, 0, j)),
                  pl.BlockSpec((1, 1, tn), lambda l, j: (l, 0, j))],
        out_specs=pl.BlockSpec((1, 8, tn), lambda l, j: (l, 0, j)),
        out_shape=jax.ShapeDtypeStruct((depth, 8, n), f32),
        compiler_params=_cparams(("parallel", "parallel")),
    )(c_pad, w_ada, b_ada.reshape(depth, 1, n))


def _ln_rows(x, eps):
    mu = jnp.mean(x, axis=-1, keepdims=True)
    d = x - mu
    var = jnp.mean(d * d, axis=-1, keepdims=True)
    return d * lax.rsqrt(var + eps)


def _lnmod_kernel(x_ref, sc_ref, sh_ref, o_ref):
    y = _ln_rows(x_ref[...], 1e-5)
    o_ref[...] = (y * (1.0 + sc_ref[0]) + sh_ref[0]).astype(o_ref.dtype)


def _lnmod(x2, sc, sh, L):
    m, d = x2.shape
    tm = 512
    per_b = L // tm
    return pl.pallas_call(
        _lnmod_kernel,
        grid=(m // tm,),
        in_specs=[pl.BlockSpec((tm, d), lambda i: (i, 0)),
                  pl.BlockSpec((1, 1, d), lambda i: (i // per_b, 0, 0)),
                  pl.BlockSpec((1, 1, d), lambda i: (i // per_b, 0, 0))],
        out_specs=pl.BlockSpec((tm, d), lambda i: (i, 0)),
        out_shape=jax.ShapeDtypeStruct((m, d), bf16),
        compiler_params=_cparams(("parallel",)),
    )(x2, sc, sh)


def _mm_kernel(a_ref, w_ref, o_ref):
    o_ref[...] = jnp.dot(a_ref[...], w_ref[...], preferred_element_type=f32).astype(o_ref.dtype)


def _mm(a, w, l, tm, tn, out_dtype):
    m, k = a.shape
    n = w.shape[2]
    return pl.pallas_call(
        _mm_kernel,
        grid=(n // tn, m // tm),
        in_specs=[pl.BlockSpec((tm, k), lambda j, i: (i, 0)),
                  pl.BlockSpec((None, k, tn), lambda j, i: (l, 0, j))],
        out_specs=pl.BlockSpec((tm, tn), lambda j, i: (i, j)),
        out_shape=jax.ShapeDtypeStruct((m, n), out_dtype),
        compiler_params=_cparams(("parallel", "parallel")),
    )(a, w)


def _proj_t_kernel(w_ref, a_ref, o_ref):
    n, d = o_ref.shape[1], o_ref.shape[3] - _ONES
    tk = o_ref.shape[4]
    res = _dot_nt(w_ref[...], a_ref[...])
    for i in range(n):
        o_ref[0, i, 0, :d, :] = res[i * d:(i + 1) * d, :].astype(o_ref.dtype)
        o_ref[0, i, 0, d:, :] = jnp.ones((_ONES, tk), o_ref.dtype)


def _proj_t(a, w_t, l, row0, n, d, tk, B, L):
    k = a.shape[1]
    rows = n * d
    nc = L // tk
    return pl.pallas_call(
        _proj_t_kernel,
        grid=(B, nc),
        in_specs=[pl.BlockSpec((None, rows, k), lambda b, c: (l, row0 // rows, 0)),
                  pl.BlockSpec((tk, k), lambda b, c: (b * nc + c, 0))],
        out_specs=pl.BlockSpec((1, n, 1, d + _ONES, tk), lambda b, c: (b, 0, c, 0, 0)),
        out_shape=jax.ShapeDtypeStruct((B, n, nc, d + _ONES, tk), bf16),
        compiler_params=_cparams(("parallel", "parallel")),
    )(w_t, a)


def _rope_low(shape, d, half):
    lane = lax.broadcasted_iota(i32, shape, 1)
    return (lane & (d - 1)) < half


def _rope_heads(x, c, s, low, half):
    partner = jnp.where(low, pltpu.roll(x, 128 - half, 1), pltpu.roll(x, half, 1))
    return x * c + partner * s


def _mm_rope_kernel(a_ref, w_ref, c_ref, s_ref, o_ref):
    acc = jnp.dot(a_ref[...], w_ref[...], preferred_element_type=f32)
    c, s = c_ref[...], s_ref[...]
    low = _rope_low(c.shape, HEAD_DIM, HEAD_DIM // 8)
    for h in range(acc.shape[1] // 128):
        sl = slice(h * 128, (h + 1) * 128)
        o_ref[:, sl] = _rope_heads(acc[:, sl], c, s, low, HEAD_DIM // 8).astype(o_ref.dtype)


def _mm_rope(a, w, l, n, tabs, L, tm, tn):
    m, k = a.shape
    per_b = L // tm
    tspec = pl.BlockSpec((tm, 128), lambda j, i: (i % per_b, 0))
    return pl.pallas_call(
        _mm_rope_kernel,
        grid=(n // tn, m // tm),
        in_specs=[pl.BlockSpec((tm, k), lambda j, i: (i, 0)),
                  pl.BlockSpec((None, k, tn), lambda j, i: (l, 0, j)),
                  tspec, tspec],
        out_specs=pl.BlockSpec((tm, tn), lambda j, i: (i, j)),
        out_shape=jax.ShapeDtypeStruct((m, n), bf16),
        compiler_params=_cparams(("parallel", "parallel")),
    )(a, w, *tabs)


def _prep_kernel(aq_ref, iq_ref, bq_ref, cq_ref, ck_ref, bk_ref, alat_ref, ikw_ref, bg_ref, alg_ref,
                 c_ref, s_ref, cq_ref_t, sq_ref_t, c6_ref, s6_ref, p128_ref, p64_ref,
                 aq_o, iq_o, bq_o, cq_o, ck_o, bk_o, alat_o, ikw_o, g_o):
    rows = 64

    def rope_all(src, dst, c_t, s_t, p_ref):
        for blk in range(src.shape[1] // 256):
            cols = slice(blk * 256, (blk + 1) * 256)
            partner = jnp.dot(src[:, cols], p_ref[...], preferred_element_type=f32)
            for r in range(src.shape[0] // rows):
                rs = slice(r * rows, (r + 1) * rows)
                c, s = c_t[rs, :], s_t[rs, :]
                for half in range(2):
                    sl = slice(blk * 256 + half * 128, blk * 256 + (half + 1) * 128)
                    y = src[rs, sl].astype(f32) * c + partner[rs, half * 128:(half + 1) * 128] * s
                    dst[rs, sl] = y.astype(dst.dtype)

    rope_all(aq_ref, aq_o, cq_ref_t, sq_ref_t, p128_ref)
    rope_all(bq_ref, bq_o, cq_ref_t, sq_ref_t, p128_ref)
    rope_all(cq_ref, cq_o, cq_ref_t, sq_ref_t, p128_ref)
    rope_all(ck_ref, ck_o, c_ref, s_ref, p128_ref)
    rope_all(bk_ref, bk_o, c_ref, s_ref, p128_ref)
    rope_all(iq_ref, iq_o, c6_ref, s6_ref, p64_ref)

    idx = (c6_ref, s6_ref, _rope_low((rows, 128), IDX_DIM, IDX_DIM // 8), IDX_DIM // 8)

    def rope(x, tabs, rs):
        return _rope_heads(x, tabs[0][rs, :], tabs[1][rs, :], tabs[2], tabs[3])

    a = alat_ref[...].astype(f32)
    ms = jnp.mean(a * a, axis=-1, keepdims=True)
    alat_o[...] = (a * lax.rsqrt(ms + 1e-6) * alg_ref[...]).astype(alat_o.dtype)

    isk = lax.broadcasted_iota(i32, (rows, 128), 1) < IDX_DIM
    for r in range(ikw_ref.shape[0] // rows):
        rs = slice(r * rows, (r + 1) * rows)
        x = ikw_ref[rs, :].astype(f32)
        mu = jnp.sum(jnp.where(isk, x, 0.0), axis=-1, keepdims=True) * (1.0 / IDX_DIM)
        d = jnp.where(isk, x - mu, 0.0)
        var = jnp.sum(d * d, axis=-1, keepdims=True) * (1.0 / IDX_DIM)
        y = d * lax.rsqrt(var + 1e-5)
        ikw_o[rs, :] = jnp.where(isk, rope(y, idx, rs), x * (IDX_DIM ** -0.5 * IDX_HEADS ** -0.5))

    g_o[...] = _sigmoid(bg_ref[...].astype(f32))


def _prep(h, a_lat_g, tabs, L):
    m = h.shape[0]
    tm = 512
    per_b = L // tm

    def hs(width, off):
        return pl.BlockSpec((tm, width), lambda i, _o=off // width: (i, _o))

    tspec = pl.BlockSpec((tm, 128), lambda i: (i % per_b, 0))
    pspec = pl.BlockSpec((256, 256), lambda i: (0, 0))

    def os(width):
        return pl.BlockSpec((tm, width), lambda i: (i, 0))

    outs = [(1024, bf16)] * 5 + [(768, bf16), (512, bf16), (128, f32), (128, f32)]
    return pl.pallas_call(
        _prep_kernel,
        grid=(m // tm,),
        in_specs=[hs(1024, _P_AQ), hs(1024, _P_IQ), hs(1024, _P_BQ), hs(1024, _P_CQ), hs(1024, _P_CK),
                  hs(768, _P_BK), hs(512, _P_ALAT), hs(128, _P_IKW), hs(128, _P_BG),
                  pl.BlockSpec((1, A_LATENT), lambda i: (0, 0))] + [tspec] * 6 + [pspec, pspec],
        out_specs=[os(w) for w, _ in outs],
        out_shape=[jax.ShapeDtypeStruct((m, w), dt) for w, dt in outs],
        compiler_params=_cparams(("parallel",)),
    )(h, h, h, h, h, h, h, h, h, a_lat_g.reshape(1, A_LATENT), *tabs['plain'], *tabs['query'], *tabs['idx'],
      _rope_perm(HEAD_DIM), _rope_perm(IDX_DIM))


def _f32_order_key(x):
    b = int(np.float32(x).view(np.int32))
    return b ^ ((b >> 31) & 0x7FFFFFFF)


_KEY_NEG = _f32_order_key(NEG)


def _dsa_mask_kernel(ik_ref, iq_ref, iw_ref, o_ref, key_ref, qp_ref, j_ref, thr_ref, cnt_ref,
                     *, L, tq, ksel):
    j = pl.program_id(1)
    ck = 512
    nck = L // ck
    nc = ((j + 1) * tq + ck - 1) // ck
    n_out_i = L - nc * ck
    n_out = n_out_i.astype(f32)
    t_row = j * tq + lax.broadcasted_iota(i32, (1, tq), 1)

    for hp in range(IDX_HEADS // 2):
        for e in range(2):
            h = 2 * hp + e
            qp_ref[hp, e * tq:(e + 1) * tq, :] = iq_ref[0, :, h * IDX_DIM:(h + 1) * IDX_DIM]

    def score_chunk(c, carry):
        off = pl.multiple_of(c * ck, ck)
        ikc = ik_ref[0, pl.ds(off, ck), :]
        acc = jnp.zeros((ck, tq), f32)
        for hp in range(IDX_HEADS // 2):
            s2 = _dot_nt(ikc, qp_ref[hp])
            acc = acc + jnp.maximum(s2[:, :tq], 0.0) * iw_ref[0, 2 * hp:2 * hp + 1, :]
            acc = acc + jnp.maximum(s2[:, tq:], 0.0) * iw_ref[0, 2 * hp + 1:2 * hp + 2, :]
        acc = jnp.where(acc == 0.0, 0.0, acc)
        sidx = off + lax.broadcasted_iota(i32, (ck, tq), 0)
        key_ref[pl.ds(off, ck), :] = jnp.where(sidx <= t_row, acc, NEG)
        return carry

    def score_pair(p, carry):
        score_chunk(2 * p, carry)
        return score_chunk(2 * p + 1, carry)

    lax.fori_loop(0, nc // 2, score_pair, 0)

    @pl.when(nc % 2 == 1)
    def _():
        score_chunk(nc - 1, 0)

    def as_f32(key):
        return lax.bitcast_convert_type(key ^ ((key >> 31) & 0x7FFFFFFF), f32)

    def count(pred_fn):
        def body(c, cnt):
            off = pl.multiple_of(c * ck, ck)
            k = key_ref[pl.ds(off, ck), :]
            sidx = off + lax.broadcasted_iota(i32, (ck, tq), 0)
            return cnt + jnp.sum(pred_fn(k, sidx).reshape(ck // 64, 64, tq), axis=0)
        part = lax.fori_loop(0, nc, body, jnp.zeros((64, tq), f32))
        return jnp.sum(part, axis=0, keepdims=True)

    kf = float(ksel)

    def bit_body(i, carry):
        thr, cnt_thr = carry
        cand = thr + lax.shift_left(jnp.int32(1), 31 - i)
        cand_f = as_f32(cand)
        cnt = count(lambda k, s: jnp.where(k >= cand_f, 1.0, 0.0)) + jnp.where(_KEY_NEG >= cand, n_out, 0.0)
        ok = cnt >= kf
        return jnp.where(ok, cand, thr), jnp.where(ok, cnt, cnt_thr)

    thr_key, cnt_ge = lax.fori_loop(0, _RADIX_FIRST, bit_body, (jnp.full((1, tq), -2 ** 31, i32),
                                                                 jnp.full((1, tq), float(L), f32)))
    thr_ref[...] = thr_key
    cnt_ref[...] = cnt_ge

    @pl.when(jnp.max(jnp.abs(cnt_ge - kf)) > 0.0)
    def _():
        t, c = lax.fori_loop(_RADIX_FIRST, 32, bit_body, (thr_key, cnt_ge))
        thr_ref[...] = t
        cnt_ref[...] = c

    thr_key, cnt_ge = thr_ref[...], cnt_ref[...]
    thr = as_f32(thr_key)

    j_ref[...] = jnp.full((1, tq), L, i32)

    @pl.when(jnp.max(cnt_ge) > kf)
    def _():
        nbits = L.bit_length() - 1
        cnt_gt = count(lambda k, s: jnp.where(k > thr, 1.0, 0.0)) + jnp.where(_KEY_NEG > thr_key, n_out, 0.0)
        need = kf - cnt_gt

        def jbit(i, cur):
            cand = cur | lax.shift_left(jnp.int32(1), nbits - 1 - i)
            f = count(lambda k, s: jnp.where(k == thr, jnp.where(s < cand, 1.0, 0.0), 0.0))
            f = f + jnp.where(thr_key == _KEY_NEG, jnp.clip(cand - nc * ck, 0, n_out_i).astype(f32), 0.0)
            return jnp.where(f < need, cand, cur)

        j_ref[...] = lax.fori_loop(0, nbits, jbit, jnp.zeros((1, tq), i32))

    jlast = j_ref[...]

    def write(c, carry):
        off = pl.multiple_of(c * ck, ck)
        k = key_ref[pl.ds(off, ck), :]
        sidx = off + lax.broadcasted_iota(i32, (ck, tq), 0)
        sel = jnp.where(k > thr, _INF, jnp.where(k == thr, jnp.where(sidx <= jlast, _INF, NEG), NEG))
        o_ref[0, pl.ds(off, ck), :] = jnp.where(sidx <= t_row, sel, NEG)
        return carry

    lax.fori_loop(0, nc, write, 0)

    def write_rest(c, carry):
        off = pl.multiple_of(c * ck, ck)
        o_ref[0, pl.ds(off, ck), :] = jnp.full((ck, tq), NEG, f32)
        return carry

    lax.fori_loop(nc, nck, write_rest, 0)


def _dsa_mask(ik, iq, iw_t):
    B, L, _ = iq.shape
    tq = 128
    ksel = min(DSA_TOPK, L // 4)
    return pl.pallas_call(
        functools.partial(_dsa_mask_kernel, L=L, tq=tq, ksel=ksel),
        grid=(B, L // tq),
        in_specs=[pl.BlockSpec((1, L, IDX_DIM), lambda b, j: (b, 0, 0)),
                  pl.BlockSpec((1, tq, IDX_HEADS * IDX_DIM), lambda b, j: (b, j, 0)),
                  pl.BlockSpec((1, IDX_HEADS, tq), lambda b, j: (b, 0, j))],
        out_specs=pl.BlockSpec((1, L, tq), lambda b, j: (b, 0, j)),
        out_shape=jax.ShapeDtypeStruct((B, L, L), f32),
        scratch_shapes=[pltpu.VMEM((L, tq), f32), pltpu.VMEM((IDX_HEADS // 2, 2 * tq, IDX_DIM), bf16),
                        pltpu.VMEM((1, tq), i32), pltpu.VMEM((1, tq), i32), pltpu.VMEM((1, tq), f32)],
        compiler_params=_cparams(("parallel", "parallel")),
    )(ik, iq, iw_t)


def _col_reduce(x, op):
    r, c = x.shape
    if r > 64:
        x = op(x.reshape(r // 64, 64, c), axis=0)
    return op(x, axis=0, keepdims=True)


def _flash_group(ss, cap, v_ts, m_refs, acc_refs):
    ss, mloc = _flash_mask_max(ss, cap)
    _flash_update(ss, mloc, cap is not None, v_ts, m_refs, acc_refs)


def _flash_mask_max(ss, cap):
    if cap is not None:
        ss = [jnp.minimum(s, cap) for s in ss]
    return ss, [_col_reduce(s, jnp.max) for s in ss]


def _flash_update(ss, mloc, masked, v_ts, m_refs, acc_refs):
    n = len(ss)
    m_prev = [r[...] for r in m_refs]
    m_new = [jnp.maximum(m_prev[i], mloc[i]) for i in range(n)]
    alpha = [jnp.exp2(m_prev[i] - m_new[i]) for i in range(n)]
    ps = [jnp.exp2((ss[i] - m_new[i]).astype(bf16)) for i in range(n)]
    pv = [jnp.dot(v_ts[i], ps[i], preferred_element_type=f32) for i in range(n)]
    for i in range(n):
        acc_new = acc_refs[i][...] * alpha[i] + pv[i]
        if masked:
            acc_new = jnp.where(m_new[i] <= NEG, 0.0, acc_new)
        acc_refs[i][...] = acc_new
        m_refs[i][...] = m_new[i]


def _flash_heads(score_fns, cap, v_ts, m_ref, acc_ref, group):
    n = len(score_fns)
    groups = [list(range(g, min(g + group, n))) for g in range(0, n, group)]
    nxt = _flash_mask_max([score_fns[h]() for h in groups[0]], cap)
    for gi, hs in enumerate(groups):
        ss, mloc = nxt
        if gi + 1 < len(groups):
            nxt = _flash_mask_max([score_fns[h]() for h in groups[gi + 1]], cap)
        _flash_update(ss, mloc, cap is not None, [v_ts[h] for h in hs], [m_ref.at[h] for h in hs],
                      [acc_ref.at[h] for h in hs])


def _flash_init(m_ref, acc_ref):
    m_ref[...] = jnp.full(m_ref.shape, NEG, f32)
    acc_ref[...] = jnp.zeros(acc_ref.shape, f32)


def _flash_result(acc, d):
    return acc[:d] / jnp.maximum(acc[d:d + 1], 1e-30)


def _dsa_attn_kernel(jm_ref, cm_ref, lm_ref, q_ref, k_ref, vt_ref, mask_ref, o_ref, m_ref, acc_ref):
    s = pl.program_id(1)

    @pl.when(cm_ref[s] == 0)
    def _():
        _flash_init(m_ref, acc_ref)

    cap = mask_ref[0]

    def scores(h):
        sl = slice(h * HEAD_DIM, (h + 1) * HEAD_DIM)
        return lambda: _dot_nt(k_ref[0, :, sl], q_ref[0, :, sl])

    v_ts = [vt_ref[0, h, 0] for h in range(A_HEADS)]
    _flash_heads([scores(h) for h in range(A_HEADS)], cap, v_ts, m_ref, acc_ref, _HEAD_GROUP)

    @pl.when(lm_ref[s] == 1)
    def _():
        for h in range(A_HEADS):
            o = _flash_result(acc_ref[h], HEAD_DIM)
            o_ref[0, :, h * HEAD_DIM:(h + 1) * HEAD_DIM] = o.T.astype(o_ref.dtype)


def _dsa_attn(q, k, v_t, mask_t):
    B, L, W = q.shape
    tq, tk = 512, 512
    pairs = [(j, c) for j in range(L // tq) for c in range(((j + 1) * tq - 1) // tk + 1)]
    jm = jnp.asarray([p[0] for p in pairs], i32)
    cm = jnp.asarray([p[1] for p in pairs], i32)
    lm = jnp.asarray([int(i + 1 == len(pairs) or pairs[i + 1][0] != p[0]) for i, p in enumerate(pairs)], i32)
    return pl.pallas_call(
        _dsa_attn_kernel,
        grid_spec=pltpu.PrefetchScalarGridSpec(
            num_scalar_prefetch=3,
            grid=(B, len(pairs)),
            in_specs=[pl.BlockSpec((1, tq, W), lambda b, s, jm, cm, lm: (b, jm[s], 0)),
                      pl.BlockSpec((1, tk, W), lambda b, s, jm, cm, lm: (b, cm[s], 0)),
                      pl.BlockSpec((1, A_HEADS, 1, HEAD_DIM + _ONES, tk),
                                   lambda b, s, jm, cm, lm: (b, 0, cm[s], 0, 0)),
                      pl.BlockSpec((1, tk, tq), lambda b, s, jm, cm, lm: (b, cm[s], jm[s]))],
            out_specs=pl.BlockSpec((1, tq, W), lambda b, s, jm, cm, lm: (b, jm[s], 0)),
            scratch_shapes=[pltpu.VMEM((A_HEADS, 1, tq), f32),
                            pltpu.VMEM((A_HEADS, HEAD_DIM + _ONES, tq), f32)]),
        out_shape=jax.ShapeDtypeStruct((B, L, W), bf16),
        compiler_params=_cparams(("parallel", "arbitrary")),
    )(jm, cm, lm, q, k, v_t, mask_t)


def _nsa_cmp_kernel(x_ref, w1_ref, w2_ref, pe_ref, o_ref):
    x = x_ref[0, 0, 0]
    w1 = w1_ref[0]
    half = CMP_STRIDE * HEAD_DIM
    a = jnp.dot(x, w1[:half], preferred_element_type=f32)
    b = jnp.dot(x, w1[half:], preferred_element_type=f32)
    pe = jnp.dot(pe_ref[0], w1, preferred_element_type=f32)[0:1]
    n = a.shape[0]
    pre = a + pltpu.roll(b, n - 1, 0) + pe
    act = pre * _sigmoid(pre)
    o_ref[0, 0, 0] = jnp.dot(act.astype(bf16), w2_ref[0], preferred_element_type=f32).astype(o_ref.dtype)


def _nsa_compress(x, w1, w2, pe, l):
    B, _, G, n, wd = x.shape
    return pl.pallas_call(
        _nsa_cmp_kernel,
        grid=(B, 2, G),
        in_specs=[pl.BlockSpec((1, 1, 1, n, wd), lambda b, t, g: (b, t, g, 0, 0)),
                  pl.BlockSpec((None, 1, CMP_LEN * HEAD_DIM, HEAD_DIM), lambda b, t, g: (l, t, 0, 0)),
                  pl.BlockSpec((None, 1, HEAD_DIM, HEAD_DIM), lambda b, t, g: (l, t, 0, 0)),
                  pl.BlockSpec((None, 1, 8, CMP_LEN * HEAD_DIM), lambda b, t, g: (l, t, 0, 0))],
        out_specs=pl.BlockSpec((1, 1, 1, n, HEAD_DIM), lambda b, t, g: (b, t, g, 0, 0)),
        out_shape=jax.ShapeDtypeStruct((B, 2, G, n, HEAD_DIM), bf16),
        compiler_params=_cparams(("parallel", "parallel", "parallel")),
    )(x, w1, w2, pe)


def _nsa_attn_kernel(q_ref, kc_ref, vct_ref, ks_ref, vst_ref, kw_ref, vwt_ref, g_ref, cov_ref, o_ref,
                     m_ref, acc_ref, out_ref, sel_ref, *, L, tq, tk, n_sel):
    j = pl.program_id(2)
    t_row = j * tq + lax.broadcasted_iota(i32, (1, tq), 1)
    n_cmp = kc_ref.shape[2]
    n_slc = L // SLC_LEN
    qs = [q_ref[0, :, hh * HEAD_DIM:(hh + 1) * HEAD_DIM] for hh in range(B_HPG)]

    def gate(hh, i):
        return g_ref[0, 0, hh * 3 + i:hh * 3 + i + 1, :]

    kc = kc_ref[0, 0]
    vct = vct_ref[0, 0]
    cend = lax.broadcasted_iota(i32, (n_cmp, tq), 0) * CMP_STRIDE + (CMP_LEN - 1)
    mc = cend <= t_row
    ss = [jnp.where(mc, _dot_nt(kc, qs[hh]), NEG) for hh in range(B_HPG)]
    mxs = [_col_reduce(s, jnp.max) for s in ss]
    ps = [jnp.where(mc, jnp.exp2(ss[hh] - mxs[hh]), 0.0) for hh in range(B_HPG)]
    inv = [1.0 / jnp.maximum(_col_reduce(p, jnp.sum), 1e-30) for p in ps]
    pvs = [jnp.dot(vct, p.astype(bf16), preferred_element_type=f32) for p in ps]
    psum = ps[0] * inv[0]
    for hh in range(1, B_HPG):
        psum = psum + ps[hh] * inv[hh]
    for hh in range(B_HPG):
        out_ref[hh] = (gate(hh, 0) * inv[hh]) * pvs[hh]
    imp = jnp.dot(cov_ref[...], psum, preferred_element_type=f32, precision=lax.Precision.HIGHEST)
    blk = lax.broadcasted_iota(i32, (n_slc, tq), 0)
    cur = t_row >> (SLC_LEN.bit_length() - 1)
    forced = (blk == 0) | (blk == cur) | (blk == cur - 1)
    imp = jnp.where(forced, FORCE, jnp.where(blk <= cur, imp, NEG))
    rank = jnp.zeros((n_slc, tq), f32)
    for r in range(n_slc):
        row = imp[r:r + 1, :]
        rank = rank + jnp.where(row > imp, 1.0, jnp.where(row == imp, jnp.where(blk > r, 1.0, 0.0), 0.0))
    sel_ref[...] = jnp.where(rank < float(n_sel), _INF, NEG)

    def finish(i):
        for hh in range(B_HPG):
            out_ref[hh] = out_ref[hh] + gate(hh, i) * _flash_result(acc_ref[hh], HEAD_DIM)

    def run_branch(k_ref, vt_ref, c_lo, c_hi, mask_fn):
        _flash_init(m_ref, acc_ref)

        def body(c, carry):
            off = pl.multiple_of(c * tk, tk)
            kch = k_ref[0, pl.ds(off, tk), :]
            vch = vt_ref[0, 0, c]
            sidx = off + lax.broadcasted_iota(i32, (tk, tq), 0)
            cap = mask_fn(c, sidx)
            fns = [functools.partial(_dot_nt, kch, qs[hh]) for hh in range(B_HPG)]
            _flash_heads(fns, cap, [vch] * B_HPG, m_ref, acc_ref, _HEAD_GROUP)
            return carry

        lax.fori_loop(c_lo, c_hi, body, 0)

    bpc = tk // SLC_LEN

    def slc_mask(c, sidx):
        rows = [jnp.broadcast_to(sel_ref[pl.ds(c * bpc + b, 1), :], (SLC_LEN, tq)) for b in range(bpc)]
        return jnp.where(sidx <= t_row, jnp.concatenate(rows, axis=0), NEG)

    c_hi = ((j + 1) * tq + tk - 1) // tk
    run_branch(ks_ref, vst_ref, 0, c_hi, slc_mask)
    finish(1)

    def win_mask(c, sidx):
        d = t_row - sidx
        return jnp.where(d >= 0, jnp.where(d < WIN_LEN, _INF, NEG), NEG)

    c_lo = jnp.maximum(j * tq - (WIN_LEN - 1), 0) // tk
    run_branch(kw_ref, vwt_ref, c_lo, c_hi, win_mask)
    finish(2)

    for hh in range(B_HPG):
        o_ref[0, :, hh * HEAD_DIM:(hh + 1) * HEAD_DIM] = out_ref[hh].T.astype(o_ref.dtype)


def _nsa_attn(bq, kc, vc_t, bk, bv_tc, gates_t, cov_t):
    B, L, _ = bq.shape
    G = B_KV_GROUPS
    tq, tk = 512, 512
    n_cmp = kc.shape[2]
    n_slc = L // SLC_LEN
    n_sel = min(SLC_TOPN, n_slc)
    gw = B_HPG * HEAD_DIM
    nc = L // tk
    return pl.pallas_call(
        functools.partial(_nsa_attn_kernel, L=L, tq=tq, tk=tk, n_sel=n_sel),
        grid=(B, G, L // tq),
        in_specs=[pl.BlockSpec((1, tq, gw), lambda b, g, j: (b, j, g)),
                  pl.BlockSpec((1, 1, n_cmp, HEAD_DIM), lambda b, g, j: (b, g, 0, 0)),
                  pl.BlockSpec((1, 1, HEAD_DIM, n_cmp), lambda b, g, j: (b, g, 0, 0)),
                  pl.BlockSpec((1, L, HEAD_DIM), lambda b, g, j: (b, 0, 2 + g)),
                  pl.BlockSpec((1, 1, nc, HEAD_DIM + _ONES, tk), lambda b, g, j: (b, g, 0, 0, 0)),
                  pl.BlockSpec((1, L, HEAD_DIM), lambda b, g, j: (b, 0, 4 + g)),
                  pl.BlockSpec((1, 1, nc, HEAD_DIM + _ONES, tk),
                               lambda b, g, j: (b, B_KV_GROUPS + g, 0, 0, 0)),
                  pl.BlockSpec((1, 1, 3 * B_HPG, tq), lambda b, g, j: (b, g, 0, j)),
                  pl.BlockSpec((n_slc, n_cmp), lambda b, g, j: (0, 0))],
        out_specs=pl.BlockSpec((1, tq, gw), lambda b, g, j: (b, j, g)),
        out_shape=jax.ShapeDtypeStruct((B, L, B_HEADS * HEAD_DIM), bf16),
        scratch_shapes=[pltpu.VMEM((B_HPG, 1, tq), f32),
                        pltpu.VMEM((B_HPG, HEAD_DIM + _ONES, tq), f32),
                        pltpu.VMEM((B_HPG, HEAD_DIM, tq), f32), pltpu.VMEM((n_slc, tq), f32)],
        compiler_params=_cparams(("parallel", "parallel", "parallel")),
    )(bq, kc, vc_t, bk, bv_tc, bk, bv_tc, gates_t, cov_t)


def _diff_attn_kernel(q_ref, k_ref, vt_ref, lam_ref, g_ref, o_ref, m_ref, acc_ref,
                      *, tq, tk, lam_init):
    j = pl.program_id(2)
    t_row = j * tq + lax.broadcasted_iota(i32, (1, tq), 1)
    qs = [q_ref[0, :, mi * C_DIM:(mi + 1) * C_DIM] for mi in range(2)]
    _flash_init(m_ref, acc_ref)

    def chunk(c, lo, width, masked):
        off = pl.multiple_of(c * tk + lo, tq)
        vch = vt_ref[0, 0, c, :, lo:lo + width]
        cap = None
        if masked:
            sidx = off + lax.broadcasted_iota(i32, (width, tq), 0)
            cap = jnp.where(sidx <= t_row, _INF, NEG)
        ss = [_dot_nt(k_ref[0, pl.ds(off, width), mi * C_DIM:(mi + 1) * C_DIM], qs[mi]) for mi in range(2)]
        _flash_group(ss, cap, [vch] * 2, [m_ref.at[mi] for mi in range(2)],
                     [acc_ref.at[mi] for mi in range(2)])

    def body(c, carry):
        chunk(c, 0, tk, False)
        return carry

    c_diag = (j * tq) // tk
    lax.fori_loop(0, c_diag, body, 0)

    @pl.when(j % 2 == 0)
    def _():
        chunk(c_diag, 0, tq, True)

    @pl.when(j % 2 == 1)
    def _():
        chunk(c_diag, 0, tq, False)
        chunk(c_diag, tq, tq, True)

    lam = lam_ref[...]
    lam_val = (jnp.exp(jnp.sum(lam[0:1] * lam[1:2], axis=1, keepdims=True))
               - jnp.exp(jnp.sum(lam[2:3] * lam[3:4], axis=1, keepdims=True)) + lam_init)
    o = _flash_result(acc_ref[0], 2 * C_DIM) - lam_val * _flash_result(acc_ref[1], 2 * C_DIM)
    ms = jnp.mean(o * o, axis=0, keepdims=True)
    y = o * lax.rsqrt(ms + 1e-6) * g_ref[...] * (1.0 - lam_init)
    o_ref[0] = y.T.astype(o_ref.dtype)


def _diff_attn(cq, ck, cv_tc, lam, g_col, lam_init):
    B, L, _ = cq.shape
    tq, tk = _DIFF_TQ, cv_tc.shape[-1]
    assert tk == 2 * tq
    hw = 2 * C_DIM
    nc = L // tk
    return pl.pallas_call(
        functools.partial(_diff_attn_kernel, tq=tq, tk=tk, lam_init=lam_init),
        grid=(B, C_HEADS, L // tq),
        in_specs=[pl.BlockSpec((1, tq, hw), lambda b, h, j: (b, j, h)),
                  pl.BlockSpec((1, L, hw), lambda b, h, j: (b, 0, h)),
                  pl.BlockSpec((1, 1, nc, hw + _ONES, tk), lambda b, h, j: (b, h, 0, 0, 0)),
                  pl.BlockSpec((4, C_DIM), lambda b, h, j: (0, 0)),
                  pl.BlockSpec((hw, 1), lambda b, h, j: (0, 0))],
        out_specs=pl.BlockSpec((1, tq, hw), lambda b, h, j: (b, j, h)),
        out_shape=jax.ShapeDtypeStruct((B, L, C_HEADS * hw), bf16),
        scratch_shapes=[pltpu.VMEM((2, 1, tq), f32), pltpu.VMEM((2, hw + _ONES, tq), f32)],
        compiler_params=_cparams(("parallel", "parallel", "parallel")),
    )(cq, ck, cv_tc, lam, g_col)


def _merge_kernel(ya_ref, yb_ref, yc_ref, w_ref, g0_ref, g1_ref, g2_ref, o_ref, wb_ref):
    @pl.when(pl.program_id(1) == 0)
    def _():
        wb_ref[...] = w_ref[...].astype(bf16)

    acc = None
    for r, (y_ref, g_ref) in enumerate(((ya_ref, g0_ref), (yb_ref, g1_ref), (yc_ref, g2_ref))):
        br = jnp.dot(y_ref[...], wb_ref[r], preferred_element_type=f32)
        t = _sigmoid(g_ref[...].astype(f32)) * br
        acc = t if acc is None else acc + t
    o_ref[...] = acc.astype(o_ref.dtype)


def _merge(ya, yb, yc, w_br, l, h):
    m, kw = ya.shape
    tm, tn = 1024, 512
    npb = D_MODEL // tn
    yspec = pl.BlockSpec((tm, kw), lambda j, i: (i, 0))

    def gspec(r):
        return pl.BlockSpec((tm, tn), lambda j, i, _r=r: (i, _r * npb + j))

    return pl.pallas_call(
        _merge_kernel,
        grid=(npb, m // tm),
        in_specs=[yspec, yspec, yspec,
                  pl.BlockSpec((None, N_BRANCH, kw, tn), lambda j, i: (l, 0, 0, j)),
                  gspec(0), gspec(1), gspec(2)],
        out_specs=pl.BlockSpec((tm, tn), lambda j, i: (i, j)),
        out_shape=jax.ShapeDtypeStruct((m, D_MODEL), bf16),
        scratch_shapes=[pltpu.VMEM((N_BRANCH, kw, tn), bf16)],
        compiler_params=_cparams(("parallel", "arbitrary")),
    )(ya, yb, yc, w_br, h, h, h)


def _mm_res_ln_kernel(a_ref, w_ref, x_ref, gate_ref, lg_ref, lb_ref, sc_ref, sh_ref, xo_ref, *u_refs,
                      alpha, sub):
    tm = a_ref.shape[0]
    for r in range(tm // sub):
        rows = slice(r * sub, (r + 1) * sub)
        y = jnp.dot(a_ref[rows, :], w_ref[...], preferred_element_type=f32)
        z = alpha * x_ref[rows, :] + gate_ref[0] * y
        xn = _ln_rows(z, 1e-5) * lg_ref[...] + lb_ref[...]
        xo_ref[rows, :] = xn
        if u_refs:
            u_refs[0][rows, :] = (_ln_rows(xn, 1e-5) * (1.0 + sc_ref[0]) + sh_ref[0]).astype(bf16)


def _mm_res_ln(a, w, l, x2, gate, ln_g, ln_b, sc, sh, L, alpha, emit_u):
    m, kdim = a.shape
    d = w.shape[2]
    tm = _RES_LN_TM if kdim * d * 2 <= 16 * 1024 * 1024 else _RES_LN_SUB
    per_b = L // tm
    bspec = pl.BlockSpec((1, 1, d), lambda i: (i // per_b, 0, 0))
    vspec = pl.BlockSpec((1, d), lambda i: (0, 0))
    rspec = pl.BlockSpec((tm, d), lambda i: (i, 0))
    out_shape = [jax.ShapeDtypeStruct((m, d), f32)]
    out_specs = [rspec]
    if emit_u:
        out_shape.append(jax.ShapeDtypeStruct((m, d), bf16))
        out_specs.append(rspec)
    res = pl.pallas_call(
        functools.partial(_mm_res_ln_kernel, alpha=alpha, sub=_RES_LN_SUB),
        grid=(m // tm,),
        in_specs=[pl.BlockSpec((tm, kdim), lambda i: (i, 0)),
                  pl.BlockSpec((None, kdim, d), lambda i: (l, 0, 0), pipeline_mode=pl.Buffered(1)),
                  rspec, bspec, vspec, vspec, bspec, bspec],
        out_specs=out_specs,
        out_shape=out_shape,
        compiler_params=_cparams(("parallel",), 60 * 1024 * 1024),
    )(a, w, x2, gate, ln_g.reshape(1, d), ln_b.reshape(1, d), sc, sh)
    return res if emit_u else (res[0], None)


def _ffn_in_kernel(a_ref, wg_ref, wu_ref, o_ref, wgb_ref, wub_ref):
    @pl.when(pl.program_id(1) == 0)
    def _():
        wgb_ref[...] = wg_ref[...].astype(bf16)
        wub_ref[...] = wu_ref[...].astype(bf16)

    a = a_ref[...]
    g = jnp.dot(a, wgb_ref[...], preferred_element_type=f32)
    u = jnp.dot(a, wub_ref[...], preferred_element_type=f32)
    o_ref[...] = (g * _sigmoid(g) * u).astype(o_ref.dtype)


def _ffn_in(a, w, l):
    m, k = a.shape
    tm, tn = 1024, 512
    nb = D_FF // tn
    return pl.pallas_call(
        _ffn_in_kernel,
        grid=(nb, m // tm),
        in_specs=[pl.BlockSpec((tm, k), lambda j, i: (i, 0)),
                  pl.BlockSpec((None, k, tn), lambda j, i: (l, 0, j)),
                  pl.BlockSpec((None, k, tn), lambda j, i: (l, 0, nb + j))],
        out_specs=pl.BlockSpec((tm, tn), lambda j, i: (i, j)),
        out_shape=jax.ShapeDtypeStruct((m, D_FF), bf16),
        scratch_shapes=[pltpu.VMEM((k, tn), bf16), pltpu.VMEM((k, tn), bf16)],
        compiler_params=_cparams(("parallel", "arbitrary")),
    )(a, w, w)


def _rope_tables(L, d, mult=1.0):
    r = d // 4
    half = r // 2
    inv = ROPE_THETA ** (-(jnp.arange(half, dtype=f32) * 2.0) / r)
    ang = jnp.arange(L).astype(f32)[:, None] * inv[None, :]
    cos, sin = jnp.cos(ang), jnp.sin(ang)
    c = jnp.concatenate([cos, cos, jnp.ones((L, d - r), f32)], axis=1)
    s = jnp.concatenate([-sin, sin, jnp.zeros((L, d - r), f32)], axis=1)
    rep = 128 // d
    return tuple(jnp.tile(t * mult, (1, rep)) for t in (c, s))


def _rope_perm(d):
    half = d // 8
    p = np.zeros((256, 256), np.float32)
    for i in range(256):
        li = i % d
        if li < half:
            p[i + half, i] = 1.0
        elif li < 2 * half:
            p[i - half, i] = 1.0
    return jnp.asarray(p, bf16)


def _pack_w_in_kernel(w_ref, o_ref, ov_ref):
    def cp(src, width, dst):
        for o in range(0, width, 1024):
            wd = min(1024, width - o)
            o_ref[:, dst + o:dst + o + wd] = w_ref[src + o:src + o + wd, :].T.astype(bf16)

    def bkv(i, kv, g):
        return _O_BKV + ((i * 2 + kv) * B_KV_GROUPS + g) * HEAD_DIM

    cp(_O_GL, N_BRANCH * D_MODEL, _P_GL)
    for src, dst in ((_O_AQ, _P_AQ), (_O_IQ, _P_IQ), (_O_BQ, _P_BQ), (_O_CQ, _P_CQ), (_O_CK, _P_CK)):
        cp(src, 1024, dst)
    for i in range(3):
        for g in range(B_KV_GROUPS):
            cp(bkv(i, 0, g), HEAD_DIM, _P_BK + (i * B_KV_GROUPS + g) * HEAD_DIM)
    for g in range(B_KV_GROUPS):
        cp(bkv(0, 1, g), HEAD_DIM, _P_BVC + g * HEAD_DIM)
    cp(_O_ALAT, A_LATENT, _P_ALAT)
    ov_ref[_PV_CV:_PV_CV + 1024, :] = w_ref[_O_CV:_O_CV + 1024, :].astype(bf16)
    for i in (1, 2):
        for g in range(B_KV_GROUPS):
            dst = _PV_BV + ((i - 1) * B_KV_GROUPS + g) * HEAD_DIM
            ov_ref[dst:dst + HEAD_DIM, :] = w_ref[bkv(i, 1, g):bkv(i, 1, g) + HEAD_DIM, :].astype(bf16)
    lane = lax.broadcasted_iota(i32, (w_ref.shape[1], 128), 1)
    assert _O_IW == _O_IK + IDX_DIM and _O_IK % 8 == 0 and _O_BG % 8 == 0
    blk = w_ref[_O_IK:_O_IK + 128, :].T
    o_ref[:, _P_IKW:_P_IKW + 128] = jnp.where(lane < IDX_DIM + IDX_HEADS, blk, 0.0).astype(bf16)
    blk = w_ref[_O_BG:_O_BG + 128, :].T
    o_ref[:, _P_BG:_P_BG + 128] = jnp.where(lane < 3 * B_HEADS, blk, 0.0).astype(bf16)


def _pack_w_in(w_in):
    depth, k, n = w_in.shape
    w_t = jnp.swapaxes(w_in, 1, 2)
    tc = 128
    def pipelined(w_hbm, o_hbm, ov_hbm):
        pltpu.emit_pipeline(
            _pack_w_in_kernel,
            grid=(depth, k // tc),
            in_specs=[pl.BlockSpec((None, n, tc), lambda l, i: (l, 0, i), pipeline_mode=pl.Buffered(3))],
            out_specs=[pl.BlockSpec((None, tc, _P_TOT), lambda l, i: (l, i, 0)),
                       pl.BlockSpec((None, _PV_TOT, tc), lambda l, i: (l, 0, i))],
        )(w_hbm, o_hbm, ov_hbm)

    return pl.pallas_call(
        pipelined,
        in_specs=[pl.BlockSpec(memory_space=pl.ANY)],
        out_specs=[pl.BlockSpec(memory_space=pl.ANY), pl.BlockSpec(memory_space=pl.ANY)],
        out_shape=[jax.ShapeDtypeStruct((depth, k, _P_TOT), bf16),
                   jax.ShapeDtypeStruct((depth, _PV_TOT, k), bf16)],
        compiler_params=pltpu.CompilerParams(vmem_limit_bytes=_VMEM_LIMIT),
    )(w_t)


def _cover_t(L):
    n_cmp_pad = L // CMP_STRIDE
    starts = np.arange(n_cmp_pad) * CMP_STRIDE
    slc_start = np.arange(L // SLC_LEN) * SLC_LEN
    cover = ((starts[:, None] < slc_start[None, :] + SLC_LEN)
             & (starts[:, None] + CMP_LEN > slc_start[None, :])).astype(np.float32)
    n_cmp = (L - CMP_LEN) // CMP_STRIDE + 1
    cover[n_cmp:] = 0.0
    return jnp.asarray(cover.T)


def _token_mixing(u, h, B, L, l, lw, tabs, cov_t, lam_init):
    M = B * L
    G = B_KV_GROUPS
    aq, iq, bq, cq, ck, bk, alat_n, ikw, gates = _prep(h, lw['a_lat_g'], tabs, L)

    aw = A_HEADS * HEAD_DIM
    ak = _mm_rope(alat_n, lw['a_up'], l, aw, tabs['plain'], L, 512, aw)
    av_tc = _proj_t(alat_n, lw['a_up_vt'], l, 0, A_HEADS, HEAD_DIM, 512, B, L)
    ik = ikw[:, :IDX_DIM].astype(bf16).reshape(B, L, IDX_DIM)
    iw_t = ikw[:, IDX_DIM:IDX_DIM + IDX_HEADS].reshape(B, L, IDX_HEADS).transpose(0, 2, 1)
    mask_t = _dsa_mask(ik, iq.reshape(B, L, -1), iw_t)
    ya = _dsa_attn(aq.reshape(B, L, -1), ak.reshape(B, L, -1), av_tc, mask_t)

    bvc = h[:, _P_BVC:_P_BVC + G * HEAD_DIM].reshape(B, L, G, HEAD_DIM)
    bkc = bk[:, :G * HEAD_DIM].reshape(B, L, G, HEAD_DIM)
    n_row = L // CMP_STRIDE
    xcmp = jnp.stack([bkc, bvc], axis=1)
    xcmp = xcmp.transpose(0, 1, 3, 2, 4).reshape(B, 2, G, n_row, CMP_STRIDE * HEAD_DIM)
    cmp_out = _nsa_compress(xcmp, lw['cmp_w1'], lw['cmp_w2'], lw['cmp_pe'], l)
    kc = cmp_out[:, 0]
    vc_t = cmp_out[:, 1].transpose(0, 1, 3, 2)
    bv_tc = _proj_t(u, lw['w_vt'], l, _PV_BV, 2 * G, HEAD_DIM, 512, B, L)
    gates_t = gates[:, :3 * B_HEADS].reshape(B, L, G, 3 * B_HPG).transpose(0, 2, 3, 1)
    yb = _nsa_attn(bq.reshape(B, L, -1), kc, vc_t, bk.reshape(B, L, -1), bv_tc, gates_t, cov_t)

    cv_tc = _proj_t(u, lw['w_vt'], l, _PV_CV, C_HEADS, 2 * C_DIM, _DIFF_TK, B, L)
    yc = _diff_attn(cq.reshape(B, L, -1), ck.reshape(B, L, -1), cv_tc, lw['lam'],
                    lw['c_subln_g'].reshape(2 * C_DIM, 1), lam_init)

    return _merge(ya.reshape(M, -1), yb.reshape(M, -1), yc.reshape(M, -1), lw['w_br'], l, h)


def kernel(x, c, w_ada, b_ada, w_in, a_lat_g, a_up, cmp_w1, cmp_w2, cmp_pe, lam, c_subln_g, w_br, w_o,
           w_ffn_in, w_ffn_out, ln_g, ln_b):
    B, L, D = x.shape
    depth = w_ada.shape[0]
    M = B * L
    alpha = (2 * depth) ** 0.25

    c_pad = jnp.zeros((8, D), f32).at[:B].set(c)
    mod = _ada(c_pad, w_ada, b_ada)[:, :B]
    mods = [[mod[l, :, i * D:(i + 1) * D].reshape(B, 1, D) for i in range(6)] for l in range(depth)]

    assert HEAD_DIM == C_DIM
    tabs = dict(plain=_rope_tables(L, HEAD_DIM), query=_rope_tables(L, HEAD_DIM, HEAD_DIM ** -0.5 * _LOG2E),
                idx=_rope_tables(L, IDX_DIM))
    cov_t = _cover_t(L)

    w_in_p, w_vt = _pack_w_in(w_in)
    pe_flat = jnp.zeros((depth, 2, 8, CMP_LEN * HEAD_DIM), f32).at[:, :, 0].set(
        cmp_pe.reshape(depth, 2, CMP_LEN * HEAD_DIM)).astype(bf16)
    a_up_vt = jnp.swapaxes(a_up[:, :, A_HEADS * HEAD_DIM:], 1, 2).astype(bf16)
    wb = dict(a_up=a_up.astype(bf16), a_up_vt=a_up_vt, w_vt=w_vt, cmp_w1=cmp_w1.astype(bf16),
              cmp_w2=cmp_w2.astype(bf16), cmp_pe=pe_flat, w_br=w_br)
    w_o_b = w_o.astype(bf16)
    w_fo_b = w_ffn_out.astype(bf16)

    x2 = x.reshape(M, D)
    u = _lnmod(x2, mods[0][1], mods[0][0], L)
    for l in range(depth):
        lam_init = 0.8 - 0.6 * math.exp(-0.3 * l)
        sh_a, sc_a, g_a, sh_f, sc_f, g_f = mods[l]
        lw = dict(wb, a_lat_g=a_lat_g[l], lam=lam[l], c_subln_g=c_subln_g[l])
        h = _mm(u, w_in_p, l, 1024, 768, bf16)
        merged = _token_mixing(u, h, B, L, l, lw, tabs, cov_t, lam_init)
        x2, u = _mm_res_ln(merged, w_o_b, l, x2, g_a, ln_g[l, 0], ln_b[l, 0], sc_f, sh_f, L, alpha, True)
        f = _ffn_in(u, w_ffn_in, l)
        last = l == depth - 1
        nsc, nsh = (sc_f, sh_f) if last else (mods[l + 1][1], mods[l + 1][0])
        x2, u = _mm_res_ln(f, w_fo_b, l, x2, g_f, ln_g[l, 1], ln_b[l, 1], nsc, nsh, L, alpha, not last)
    return x2.reshape(B, L, D)
```
